```python
import math
import jax, jax.numpy as jnp
from jax import lax
import numpy as np

D_MODEL = 1024
BATCH = 8
SEQ = 8192
DEPTH = 1

D_RNN = D_MODEL // 2
N_RNN_BLOCKS = 8
RNN_BLOCK = D_RNN // N_RNN_BLOCKS
CONV_WIDTH = 4
CONV_LEFT = 2
LRU_C = 8.0
N_ATTN_HEADS = 8
HEAD_DIM = 64
D_ATTN = N_ATTN_HEADS * HEAD_DIM
DILATED_PATTERNS = ((128, 1), (512, 4), (2048, 16))
Q_BLOCK = 128
N_BUCKETS = 32
MAX_DISTANCE = 1024
D_MIX = D_RNN + D_ATTN
D_IN = 2 * D_RNN + 3 * D_ATTN
D_FF = 4 * D_MODEL
EPS = 1e-6
NEG_INF = -1e30

kernel_name = "hybrid_rglru_dilated_attn_block"


def rms_norm(x, g):
    xf = x.astype(jnp.float32)
    y = xf * lax.rsqrt(jnp.mean(xf * xf, axis=-1, keepdims=True) + EPS)
    return (y * g.astype(jnp.float32)).astype(x.dtype)


def t5_bucket(rel):
    nb = N_BUCKETS // 2
    max_exact = nb // 2
    ret = jnp.where(rel > 0, nb, 0)
    n = jnp.abs(rel)
    nf = jnp.maximum(n, 1).astype(jnp.float32)
    large = max_exact + (jnp.log(nf / max_exact) / math.log(MAX_DISTANCE / max_exact)
                         * (nb - max_exact)).astype(jnp.int32)
    large = jnp.minimum(large, nb - 1)
    return ret + jnp.where(n < max_exact, n, large)


def centred_depthwise_conv(x, w, b):
    S = x.shape[1]
    xp = jnp.pad(x, ((0, 0), (CONV_LEFT, CONV_WIDTH - 1 - CONV_LEFT), (0, 0)))
    y = b
    for k in range(CONV_WIDTH):
        y = y + xp[:, k:k + S] * w[k]
    return y


def rg_lru(x, wa, ba, wx, bx, lam, reverse):
    B, S, _ = x.shape
    xb = x.reshape(B, S, N_RNN_BLOCKS, RNN_BLOCK)
    r = jax.nn.sigmoid((jnp.einsum('bsnc,ncd->bsnd', xb, wa).reshape(B, S, D_RNN) + ba).astype(jnp.float32))
    i = jax.nn.sigmoid((jnp.einsum('bsnc,ncd->bsnd', xb, wx).reshape(B, S, D_RNN) + bx).astype(jnp.float32))
    log_a = -LRU_C * jax.nn.softplus(-lam.astype(jnp.float32)) * r
    a = jnp.exp(log_a)
    b_in = jnp.sqrt(-jnp.expm1(2.0 * log_a)) * (i * x.astype(jnp.float32))

    def combine(c1, c2):
        a1, b1 = c1
        a2, b2 = c2
        return a1 * a2, a2 * b1 + b2

    _, h = lax.associative_scan(combine, (a, b_in), reverse=reverse, axis=1)
    return h


def dilated_attention(q, k, v, rel_bias):
    B, S, H, Dh = q.shape
    nb = S // Q_BLOCK
    scale = Dh ** -0.5
    pats = []
    for window, dil in DILATED_PATTERNS:
        half = window // (2 * dil)
        offs = jnp.arange(-half, half + 1, dtype=jnp.int32) * dil
        bias = rel_bias[t5_bucket(offs)].astype(jnp.float32).T
        pats.append((offs, bias))
    q_blocks = q.reshape(B, nb, Q_BLOCK, H, Dh).transpose(1, 0, 2, 3, 4)

    def one_block(args):
        qb, n = args
        pos = n * Q_BLOCK + jnp.arange(Q_BLOCK, dtype=jnp.int32)
        outs, lses = [], []
        for offs, bias in pats:
            kpos = pos[:, None] + offs[None, :]
            valid = (kpos >= 0) & (kpos < S)
            kidx = jnp.clip(kpos, 0, S - 1)
            kg = k[:, kidx]
            vg = v[:, kidx]
            logits = jnp.einsum('bqhd,bqjhd->bhqj', qb, kg).astype(jnp.float32) * scale
            logits = logits + bias[None, :, None, :]
            logits = jnp.where(valid[None, None], logits, NEG_INF)
            lse = jax.nn.logsumexp(logits, axis=-1)
            p = jnp.exp(logits - lse[..., None])
            outs.append(jnp.einsum('bhqj,bqjhd->bqhd', p.astype(v.dtype), vg).astype(jnp.float32))
            lses.append(lse)
        w = jax.nn.softmax(jnp.stack(lses, axis=0), axis=0)
        w = jnp.transpose(w, (0, 1, 3, 2))[..., None]
        o = jnp.sum(w * jnp.stack(outs, axis=0), axis=0)
        return o.astype(q.dtype)

    out = lax.map(one_block, (q_blocks, jnp.arange(nb, dtype=jnp.int32)))
    return out.transpose(1, 0, 2, 3, 4).reshape(B, S, H * Dh)


def _fwd_setup_inputs(seed: int = 0) -> dict:
    key = jax.random.key(seed)
    ks = jax.random.split(key, 24)
    f32 = jnp.float32

    def nrm(k, shape, s):
        return jax.random.normal(k, shape, f32) * s

    def gain(k, shape):
        return 1.0 + 0.02 * jax.random.normal(k, shape, f32)

    def lam(k):
        u = jax.random.uniform(k, (DEPTH, D_RNN), f32, minval=0.9, maxval=0.999)
        s = u ** (1.0 / LRU_C)
        return jnp.log(s) - jnp.log1p(-s)

    return {
        "x": jax.random.normal(ks[0], (BATCH, SEQ, D_MODEL), f32),
        "attn_norm_g": gain(ks[1], (DEPTH, D_MODEL)),
        "w_in": nrm(ks[2], (DEPTH, D_MODEL, D_IN), D_MODEL ** -0.5),
        "conv_w": nrm(ks[3], (DEPTH, CONV_WIDTH, D_RNN), CONV_WIDTH ** -0.5),
        "conv_b": nrm(ks[4], (DEPTH, D_RNN), 0.01),
        "lru_wa_fwd": nrm(ks[5], (DEPTH, N_RNN_BLOCKS, RNN_BLOCK, RNN_BLOCK), RNN_BLOCK ** -0.5),
        "lru_ba_fwd": nrm(ks[6], (DEPTH, D_RNN), 0.01),
        "lru_wx_fwd": nrm(ks[7], (DEPTH, N_RNN_BLOCKS, RNN_BLOCK, RNN_BLOCK), RNN_BLOCK ** -0.5),
        "lru_bx_fwd": nrm(ks[8], (DEPTH, D_RNN), 0.01),
        "lru_lam_fwd": lam(ks[9]),
        "lru_wa_bwd": nrm(ks[10], (DEPTH, N_RNN_BLOCKS, RNN_BLOCK, RNN_BLOCK), RNN_BLOCK ** -0.5),
        "lru_ba_bwd": nrm(ks[11], (DEPTH, D_RNN), 0.01),
        "lru_wx_bwd": nrm(ks[12], (DEPTH, N_RNN_BLOCKS, RNN_BLOCK, RNN_BLOCK), RNN_BLOCK ** -0.5),
        "lru_bx_bwd": nrm(ks[13], (DEPTH, D_RNN), 0.01),
        "lru_lam_bwd": lam(ks[14]),
        "rel_bias": nrm(ks[15], (N_BUCKETS, N_ATTN_HEADS), 0.5),
        "norm_rnn_g": gain(ks[16], (DEPTH, D_RNN)),
        "norm_attn_g": gain(ks[17], (DEPTH, D_ATTN)),
        "w_out": nrm(ks[18], (DEPTH, D_MIX, D_MODEL), D_MIX ** -0.5),
        "mlp_norm_g": gain(ks[19], (DEPTH, D_MODEL)),
        "w_up": nrm(ks[20], (DEPTH, D_MODEL, D_FF), D_MODEL ** -0.5),
        "w_down": nrm(ks[21], (DEPTH, D_FF, D_MODEL), D_FF ** -0.5),
        "final_norm_g": gain(ks[22], (D_MODEL,)),
    }


def _fwd_reference(x, attn_norm_g, w_in, conv_w, conv_b,
              lru_wa_fwd, lru_ba_fwd, lru_wx_fwd, lru_bx_fwd, lru_lam_fwd,
              lru_wa_bwd, lru_ba_bwd, lru_wx_bwd, lru_bx_bwd, lru_lam_bwd,
              rel_bias, norm_rnn_g, norm_attn_g, w_out,
              mlp_norm_g, w_up, w_down, final_norm_g):
    B, S, _ = x.shape
    for l in range(DEPTH):
        h = rms_norm(x, attn_norm_g[l])
        proj = h @ w_in[l]
        xr, gate, q, k, v = jnp.split(
            proj, [D_RNN, 2 * D_RNN, 2 * D_RNN + D_ATTN, 2 * D_RNN + 2 * D_ATTN], axis=-1)
        xr = centred_depthwise_conv(xr, conv_w[l], conv_b[l])
        h_f = rg_lru(xr, lru_wa_fwd[l], lru_ba_fwd[l], lru_wx_fwd[l], lru_bx_fwd[l], lru_lam_fwd[l], False)
        h_b = rg_lru(xr, lru_wa_bwd[l], lru_ba_bwd[l], lru_wx_bwd[l], lru_bx_bwd[l], lru_lam_bwd[l], True)
        y_rnn = (h_f + h_b).astype(x.dtype) * jax.nn.gelu(gate)
        qh = q.reshape(B, S, N_ATTN_HEADS, HEAD_DIM)
        kh = k.reshape(B, S, N_ATTN_HEADS, HEAD_DIM)
        vh = v.reshape(B, S, N_ATTN_HEADS, HEAD_DIM)
        y_attn = dilated_attention(qh, kh, vh, rel_bias)
        mix = jnp.concatenate([rms_norm(y_rnn, norm_rnn_g[l]), rms_norm(y_attn, norm_attn_g[l])], axis=-1)
        x = x + mix @ w_out[l]
        h = rms_norm(x, mlp_norm_g[l])
        x = x + jnp.square(jax.nn.relu(h @ w_up[l])) @ w_down[l]
    return rms_norm(x, final_norm_g)


import jax as _jax
import jax.numpy as _jnp

TWIN_FORMAT = 'train_step'
FWD_PARAMS = ['x', 'attn_norm_g', 'w_in', 'conv_w', 'conv_b', 'lru_wa_fwd', 'lru_ba_fwd', 'lru_wx_fwd', 'lru_bx_fwd', 'lru_lam_fwd', 'lru_wa_bwd', 'lru_ba_bwd', 'lru_wx_bwd', 'lru_bx_bwd', 'lru_lam_bwd', 'rel_bias', 'norm_rnn_g', 'norm_attn_g', 'w_out', 'mlp_norm_g', 'w_up', 'w_down', 'final_norm_g']
TWIN_WEIGHTS = ['attn_norm_g', 'w_in', 'conv_w', 'conv_b', 'lru_wa_fwd', 'lru_ba_fwd', 'lru_wx_fwd', 'lru_bx_fwd', 'lru_lam_fwd', 'lru_wa_bwd', 'lru_ba_bwd', 'lru_wx_bwd', 'lru_bx_bwd', 'lru_lam_bwd', 'rel_bias', 'norm_rnn_g', 'norm_attn_g', 'w_out', 'mlp_norm_g', 'w_up', 'w_down', 'final_norm_g']
TWIN_DIFF_INPUT = 'x'
TWIN_INPUTS = ['x', 'attn_norm_g', 'w_in', 'conv_w', 'conv_b', 'lru_wa_fwd', 'lru_ba_fwd', 'lru_wx_fwd', 'lru_bx_fwd', 'lru_lam_fwd', 'lru_wa_bwd', 'lru_ba_bwd', 'lru_wx_bwd', 'lru_bx_bwd', 'lru_lam_bwd', 'rel_bias', 'norm_rnn_g', 'norm_attn_g', 'w_out', 'mlp_norm_g', 'w_up', 'w_down', 'final_norm_g', 'loss_target', 'm_attn_norm_g', 'm_w_in', 'm_conv_w', 'm_conv_b', 'm_lru_wa_fwd', 'm_lru_ba_fwd', 'm_lru_wx_fwd', 'm_lru_bx_fwd', 'm_lru_lam_fwd', 'm_lru_wa_bwd', 'm_lru_ba_bwd', 'm_lru_wx_bwd', 'm_lru_bx_bwd', 'm_lru_lam_bwd', 'm_rel_bias', 'm_norm_rnn_g', 'm_norm_attn_g', 'm_w_out', 'm_mlp_norm_g', 'm_w_up', 'm_w_down', 'm_final_norm_g', 'v_attn_norm_g', 'v_w_in', 'v_conv_w', 'v_conv_b', 'v_lru_wa_fwd', 'v_lru_ba_fwd', 'v_lru_wx_fwd', 'v_lru_bx_fwd', 'v_lru_lam_fwd', 'v_lru_wa_bwd', 'v_lru_ba_bwd', 'v_lru_wx_bwd', 'v_lru_bx_bwd', 'v_lru_lam_bwd', 'v_rel_bias', 'v_norm_rnn_g', 'v_norm_attn_g', 'v_w_out', 'v_mlp_norm_g', 'v_w_up', 'v_w_down', 'v_final_norm_g']
TWIN_OUTPUTS = ['loss', 'grad_x', 'grad_attn_norm_g', 'grad_w_in', 'grad_conv_w', 'grad_conv_b', 'grad_lru_wa_fwd', 'grad_lru_ba_fwd', 'grad_lru_wx_fwd', 'grad_lru_bx_fwd', 'grad_lru_lam_fwd', 'grad_lru_wa_bwd', 'grad_lru_ba_bwd', 'grad_lru_wx_bwd', 'grad_lru_bx_bwd', 'grad_lru_lam_bwd', 'grad_rel_bias', 'grad_norm_rnn_g', 'grad_norm_attn_g', 'grad_w_out', 'grad_mlp_norm_g', 'grad_w_up', 'grad_w_down', 'grad_final_norm_g', 'delta_attn_norm_g', 'delta_w_in', 'delta_conv_w', 'delta_conv_b', 'delta_lru_wa_fwd', 'delta_lru_ba_fwd', 'delta_lru_wx_fwd', 'delta_lru_bx_fwd', 'delta_lru_lam_fwd', 'delta_lru_wa_bwd', 'delta_lru_ba_bwd', 'delta_lru_wx_bwd', 'delta_lru_bx_bwd', 'delta_lru_lam_bwd', 'delta_rel_bias', 'delta_norm_rnn_g', 'delta_norm_attn_g', 'delta_w_out', 'delta_mlp_norm_g', 'delta_w_up', 'delta_w_down', 'delta_final_norm_g', 'new_m_attn_norm_g', 'new_m_w_in', 'new_m_conv_w', 'new_m_conv_b', 'new_m_lru_wa_fwd', 'new_m_lru_ba_fwd', 'new_m_lru_wx_fwd', 'new_m_lru_bx_fwd', 'new_m_lru_lam_fwd', 'new_m_lru_wa_bwd', 'new_m_lru_ba_bwd', 'new_m_lru_wx_bwd', 'new_m_lru_bx_bwd', 'new_m_lru_lam_bwd', 'new_m_rel_bias', 'new_m_norm_rnn_g', 'new_m_norm_attn_g', 'new_m_w_out', 'new_m_mlp_norm_g', 'new_m_w_up', 'new_m_w_down', 'new_m_final_norm_g', 'new_v_attn_norm_g', 'new_v_w_in', 'new_v_conv_w', 'new_v_conv_b', 'new_v_lru_wa_fwd', 'new_v_lru_ba_fwd', 'new_v_lru_wx_fwd', 'new_v_lru_bx_fwd', 'new_v_lru_lam_fwd', 'new_v_lru_wa_bwd', 'new_v_lru_ba_bwd', 'new_v_lru_wx_bwd', 'new_v_lru_bx_bwd', 'new_v_lru_lam_bwd', 'new_v_rel_bias', 'new_v_norm_rnn_g', 'new_v_norm_attn_g', 'new_v_w_out', 'new_v_mlp_norm_g', 'new_v_w_up', 'new_v_w_down', 'new_v_final_norm_g']
TWIN_LEAF_KINDS = {'loss': 'loss', 'grad_x': 'grad_x', 'grad_attn_norm_g': 'grad_w', 'grad_w_in': 'grad_w', 'grad_conv_w': 'grad_w', 'grad_conv_b': 'grad_w', 'grad_lru_wa_fwd': 'grad_w', 'grad_lru_ba_fwd': 'grad_w', 'grad_lru_wx_fwd': 'grad_w', 'grad_lru_bx_fwd': 'grad_w', 'grad_lru_lam_fwd': 'grad_w', 'grad_lru_wa_bwd': 'grad_w', 'grad_lru_ba_bwd': 'grad_w', 'grad_lru_wx_bwd': 'grad_w', 'grad_lru_bx_bwd': 'grad_w', 'grad_lru_lam_bwd': 'grad_w', 'grad_rel_bias': 'grad_w', 'grad_norm_rnn_g': 'grad_w', 'grad_norm_attn_g': 'grad_w', 'grad_w_out': 'grad_w', 'grad_mlp_norm_g': 'grad_w', 'grad_w_up': 'grad_w', 'grad_w_down': 'grad_w', 'grad_final_norm_g': 'grad_w', 'delta_attn_norm_g': 'delta_w', 'delta_w_in': 'delta_w', 'delta_conv_w': 'delta_w', 'delta_conv_b': 'delta_w', 'delta_lru_wa_fwd': 'delta_w', 'delta_lru_ba_fwd': 'delta_w', 'delta_lru_wx_fwd': 'delta_w', 'delta_lru_bx_fwd': 'delta_w', 'delta_lru_lam_fwd': 'delta_w', 'delta_lru_wa_bwd': 'delta_w', 'delta_lru_ba_bwd': 'delta_w', 'delta_lru_wx_bwd': 'delta_w', 'delta_lru_bx_bwd': 'delta_w', 'delta_lru_lam_bwd': 'delta_w', 'delta_rel_bias': 'delta_w', 'delta_norm_rnn_g': 'delta_w', 'delta_norm_attn_g': 'delta_w', 'delta_w_out': 'delta_w', 'delta_mlp_norm_g': 'delta_w', 'delta_w_up': 'delta_w', 'delta_w_down': 'delta_w', 'delta_final_norm_g': 'delta_w', 'new_m_attn_norm_g': 'new_m', 'new_m_w_in': 'new_m', 'new_m_conv_w': 'new_m', 'new_m_conv_b': 'new_m', 'new_m_lru_wa_fwd': 'new_m', 'new_m_lru_ba_fwd': 'new_m', 'new_m_lru_wx_fwd': 'new_m', 'new_m_lru_bx_fwd': 'new_m', 'new_m_lru_lam_fwd': 'new_m', 'new_m_lru_wa_bwd': 'new_m', 'new_m_lru_ba_bwd': 'new_m', 'new_m_lru_wx_bwd': 'new_m', 'new_m_lru_bx_bwd': 'new_m', 'new_m_lru_lam_bwd': 'new_m', 'new_m_rel_bias': 'new_m', 'new_m_norm_rnn_g': 'new_m', 'new_m_norm_attn_g': 'new_m', 'new_m_w_out': 'new_m', 'new_m_mlp_norm_g': 'new_m', 'new_m_w_up': 'new_m', 'new_m_w_down': 'new_m', 'new_m_final_norm_g': 'new_m', 'new_v_attn_norm_g': 'new_v', 'new_v_w_in': 'new_v', 'new_v_conv_w': 'new_v', 'new_v_conv_b': 'new_v', 'new_v_lru_wa_fwd': 'new_v', 'new_v_lru_ba_fwd': 'new_v', 'new_v_lru_wx_fwd': 'new_v', 'new_v_lru_bx_fwd': 'new_v', 'new_v_lru_lam_fwd': 'new_v', 'new_v_lru_wa_bwd': 'new_v', 'new_v_lru_ba_bwd': 'new_v', 'new_v_lru_wx_bwd': 'new_v', 'new_v_lru_bx_bwd': 'new_v', 'new_v_lru_lam_bwd': 'new_v', 'new_v_rel_bias': 'new_v', 'new_v_norm_rnn_g': 'new_v', 'new_v_norm_attn_g': 'new_v', 'new_v_w_out': 'new_v', 'new_v_mlp_norm_g': 'new_v', 'new_v_w_up': 'new_v', 'new_v_w_down': 'new_v', 'new_v_final_norm_g': 'new_v'}


def _forward(args):
    return _fwd_reference(*[args[k] for k in FWD_PARAMS])


def _output_shape():
    def fwd():
        inp = _fwd_setup_inputs(0)
        return _fwd_reference(*[inp[k] for k in FWD_PARAMS])
    out = _jax.eval_shape(fwd)
    return out.shape, out.dtype

N_MICROBATCH = 1
ADAM_LR = 0.001
ADAM_B1 = 0.9
ADAM_B2 = 0.999
ADAM_EPS = 1e-08
ADAM_WD = 0.01
ADAM_STEP = 10
PER_EXAMPLE_BATCH_AXIS = {'x': 0, 'loss_target': 0}
SHARED_INPUTS = []
_WEIGHT_DTYPES = {'attn_norm_g': _jnp.float32, 'w_in': _jnp.float32, 'conv_w': _jnp.float32, 'conv_b': _jnp.float32, 'lru_wa_fwd': _jnp.float32, 'lru_ba_fwd': _jnp.float32, 'lru_wx_fwd': _jnp.float32, 'lru_bx_fwd': _jnp.float32, 'lru_lam_fwd': _jnp.float32, 'lru_wa_bwd': _jnp.float32, 'lru_ba_bwd': _jnp.float32, 'lru_wx_bwd': _jnp.float32, 'lru_bx_bwd': _jnp.float32, 'lru_lam_bwd': _jnp.float32, 'rel_bias': _jnp.float32, 'norm_rnn_g': _jnp.float32, 'norm_attn_g': _jnp.float32, 'w_out': _jnp.float32, 'mlp_norm_g': _jnp.float32, 'w_up': _jnp.float32, 'w_down': _jnp.float32, 'final_norm_g': _jnp.float32}
MOMENT_SCALE = {'attn_norm_g': 3.205333e-01, 'w_in': 1.865428e-01, 'conv_w': 2.138675e-01, 'conv_b': 3.387896e+00, 'lru_wa_fwd': 5.446693e-02, 'lru_ba_fwd': 4.605751e-02, 'lru_wx_fwd': 9.748209e-02, 'lru_bx_fwd': 4.740902e-02, 'lru_lam_fwd': 7.651410e-02, 'lru_wa_bwd': 5.991718e-02, 'lru_ba_bwd': 3.703054e-02, 'lru_wx_bwd': 1.112269e-01, 'lru_bx_bwd': 4.725174e-02, 'lru_lam_bwd': 6.610549e-02, 'rel_bias': 2.481529e-01, 'norm_rnn_g': 2.145531e-01, 'norm_attn_g': 1.954758e-01, 'w_out': 1.955677e-01, 'mlp_norm_g': 2.246763e-01, 'w_up': 9.606827e-02, 'w_down': 2.035476e-01, 'final_norm_g': 6.439735e+01}


def _to_microbatches(a, axis):
    t = _jnp.moveaxis(a, axis, 0)
    t = t.reshape((N_MICROBATCH, t.shape[0] // N_MICROBATCH) + t.shape[1:])
    return _jnp.moveaxis(t, 1, axis + 1)


def setup_inputs(seed: int = 0) -> dict:
    inp = _fwd_setup_inputs(seed)
    key = _jax.random.fold_in(_jax.random.key(seed), 7919)
    shape, _ = _output_shape()
    out = dict(inp)
    out["loss_target"] = _jax.random.normal(_jax.random.fold_in(key, 0), shape, _jnp.float32)
    for i, name in enumerate(TWIN_WEIGHTS):
        w = inp[name].astype(_jnp.float32)
        if MOMENT_SCALE is None:
            s = _jnp.sqrt(_jnp.mean(_jnp.square(w)) + 1e-30)
        else:
            s = MOMENT_SCALE[name]
        km, kv = _jax.random.split(_jax.random.fold_in(key, i + 1))
        out[name] = w
        out["m_" + name] = s * _jax.random.normal(km, w.shape, _jnp.float32)
        out["v_" + name] = (s * s) * _jax.random.uniform(kv, w.shape, _jnp.float32, 0.5, 1.5)
    if N_MICROBATCH > 1:
        for name, axis in PER_EXAMPLE_BATCH_AXIS.items():
            out[name] = _to_microbatches(out[name], axis)
    return {'x': out['x'], 'attn_norm_g': out['attn_norm_g'], 'w_in': out['w_in'], 'conv_w': out['conv_w'], 'conv_b': out['conv_b'], 'lru_wa_fwd': out['lru_wa_fwd'], 'lru_ba_fwd': out['lru_ba_fwd'], 'lru_wx_fwd': out['lru_wx_fwd'], 'lru_bx_fwd': out['lru_bx_fwd'], 'lru_lam_fwd': out['lru_lam_fwd'], 'lru_wa_bwd': out['lru_wa_bwd'], 'lru_ba_bwd': out['lru_ba_bwd'], 'lru_wx_bwd': out['lru_wx_bwd'], 'lru_bx_bwd': out['lru_bx_bwd'], 'lru_lam_bwd': out['lru_lam_bwd'], 'rel_bias': out['rel_bias'], 'norm_rnn_g': out['norm_rnn_g'], 'norm_attn_g': out['norm_attn_g'], 'w_out': out['w_out'], 'mlp_norm_g': out['mlp_norm_g'], 'w_up': out['w_up'], 'w_down': out['w_down'], 'final_norm_g': out['final_norm_g'], 'loss_target': out['loss_target'], 'm_attn_norm_g': out['m_attn_norm_g'], 'm_w_in': out['m_w_in'], 'm_conv_w': out['m_conv_w'], 'm_conv_b': out['m_conv_b'], 'm_lru_wa_fwd': out['m_lru_wa_fwd'], 'm_lru_ba_fwd': out['m_lru_ba_fwd'], 'm_lru_wx_fwd': out['m_lru_wx_fwd'], 'm_lru_bx_fwd': out['m_lru_bx_fwd'], 'm_lru_lam_fwd': out['m_lru_lam_fwd'], 'm_lru_wa_bwd': out['m_lru_wa_bwd'], 'm_lru_ba_bwd': out['m_lru_ba_bwd'], 'm_lru_wx_bwd': out['m_lru_wx_bwd'], 'm_lru_bx_bwd': out['m_lru_bx_bwd'], 'm_lru_lam_bwd': out['m_lru_lam_bwd'], 'm_rel_bias': out['m_rel_bias'], 'm_norm_rnn_g': out['m_norm_rnn_g'], 'm_norm_attn_g': out['m_norm_attn_g'], 'm_w_out': out['m_w_out'], 'm_mlp_norm_g': out['m_mlp_norm_g'], 'm_w_up': out['m_w_up'], 'm_w_down': out['m_w_down'], 'm_final_norm_g': out['m_final_norm_g'], 'v_attn_norm_g': out['v_attn_norm_g'], 'v_w_in': out['v_w_in'], 'v_conv_w': out['v_conv_w'], 'v_conv_b': out['v_conv_b'], 'v_lru_wa_fwd': out['v_lru_wa_fwd'], 'v_lru_ba_fwd': out['v_lru_ba_fwd'], 'v_lru_wx_fwd': out['v_lru_wx_fwd'], 'v_lru_bx_fwd': out['v_lru_bx_fwd'], 'v_lru_lam_fwd': out['v_lru_lam_fwd'], 'v_lru_wa_bwd': out['v_lru_wa_bwd'], 'v_lru_ba_bwd': out['v_lru_ba_bwd'], 'v_lru_wx_bwd': out['v_lru_wx_bwd'], 'v_lru_bx_bwd': out['v_lru_bx_bwd'], 'v_lru_lam_bwd': out['v_lru_lam_bwd'], 'v_rel_bias': out['v_rel_bias'], 'v_norm_rnn_g': out['v_norm_rnn_g'], 'v_norm_attn_g': out['v_norm_attn_g'], 'v_w_out': out['v_w_out'], 'v_mlp_norm_g': out['v_mlp_norm_g'], 'v_w_up': out['v_w_up'], 'v_w_down': out['v_w_down'], 'v_final_norm_g': out['v_final_norm_g']}


def _loss(weights, diff, rest, loss_target):
    with _jax.named_scope("forward"):
        args = {**rest, TWIN_DIFF_INPUT: diff, **{k: w.astype(_WEIGHT_DTYPES[k]) for k, w in weights.items()}}
        y = _forward(args)
    with _jax.named_scope("loss_head"):
        err = _jnp.square(y.astype(_jnp.float32) - loss_target)
        return 0.5 * _jnp.sum(_jnp.mean(err, axis=-1)) if err.ndim else 0.5 * err


def _adamw(w, g, m, v):
    m = ADAM_B1 * m + (1.0 - ADAM_B1) * g
    v = ADAM_B2 * v + (1.0 - ADAM_B2) * _jnp.square(g)
    m_hat = m / (1.0 - ADAM_B1 ** ADAM_STEP)
    v_hat = v / (1.0 - ADAM_B2 ** ADAM_STEP)
    delta = -ADAM_LR * (m_hat / (_jnp.sqrt(v_hat) + ADAM_EPS) + ADAM_WD * w)
    return delta, m, v


def reference(x, attn_norm_g, w_in, conv_w, conv_b, lru_wa_fwd, lru_ba_fwd, lru_wx_fwd, lru_bx_fwd, lru_lam_fwd, lru_wa_bwd, lru_ba_bwd, lru_wx_bwd, lru_bx_bwd, lru_lam_bwd, rel_bias, norm_rnn_g, norm_attn_g, w_out, mlp_norm_g, w_up, w_down, final_norm_g, loss_target, m_attn_norm_g, m_w_in, m_conv_w, m_conv_b, m_lru_wa_fwd, m_lru_ba_fwd, m_lru_wx_fwd, m_lru_bx_fwd, m_lru_lam_fwd, m_lru_wa_bwd, m_lru_ba_bwd, m_lru_wx_bwd, m_lru_bx_bwd, m_lru_lam_bwd, m_rel_bias, m_norm_rnn_g, m_norm_attn_g, m_w_out, m_mlp_norm_g, m_w_up, m_w_down, m_final_norm_g, v_attn_norm_g, v_w_in, v_conv_w, v_conv_b, v_lru_wa_fwd, v_lru_ba_fwd, v_lru_wx_fwd, v_lru_bx_fwd, v_lru_lam_fwd, v_lru_wa_bwd, v_lru_ba_bwd, v_lru_wx_bwd, v_lru_bx_bwd, v_lru_lam_bwd, v_rel_bias, v_norm_rnn_g, v_norm_attn_g, v_w_out, v_mlp_norm_g, v_w_up, v_w_down, v_final_norm_g):
    given = dict(x=x, attn_norm_g=attn_norm_g, w_in=w_in, conv_w=conv_w, conv_b=conv_b, lru_wa_fwd=lru_wa_fwd, lru_ba_fwd=lru_ba_fwd, lru_wx_fwd=lru_wx_fwd, lru_bx_fwd=lru_bx_fwd, lru_lam_fwd=lru_lam_fwd, lru_wa_bwd=lru_wa_bwd, lru_ba_bwd=lru_ba_bwd, lru_wx_bwd=lru_wx_bwd, lru_bx_bwd=lru_bx_bwd, lru_lam_bwd=lru_lam_bwd, rel_bias=rel_bias, norm_rnn_g=norm_rnn_g, norm_attn_g=norm_attn_g, w_out=w_out, mlp_norm_g=mlp_norm_g, w_up=w_up, w_down=w_down, final_norm_g=final_norm_g, loss_target=loss_target, m_attn_norm_g=m_attn_norm_g, m_w_in=m_w_in, m_conv_w=m_conv_w, m_conv_b=m_conv_b, m_lru_wa_fwd=m_lru_wa_fwd, m_lru_ba_fwd=m_lru_ba_fwd, m_lru_wx_fwd=m_lru_wx_fwd, m_lru_bx_fwd=m_lru_bx_fwd, m_lru_lam_fwd=m_lru_lam_fwd, m_lru_wa_bwd=m_lru_wa_bwd, m_lru_ba_bwd=m_lru_ba_bwd, m_lru_wx_bwd=m_lru_wx_bwd, m_lru_bx_bwd=m_lru_bx_bwd, m_lru_lam_bwd=m_lru_lam_bwd, m_rel_bias=m_rel_bias, m_norm_rnn_g=m_norm_rnn_g, m_norm_attn_g=m_norm_attn_g, m_w_out=m_w_out, m_mlp_norm_g=m_mlp_norm_g, m_w_up=m_w_up, m_w_down=m_w_down, m_final_norm_g=m_final_norm_g, v_attn_norm_g=v_attn_norm_g, v_w_in=v_w_in, v_conv_w=v_conv_w, v_conv_b=v_conv_b, v_lru_wa_fwd=v_lru_wa_fwd, v_lru_ba_fwd=v_lru_ba_fwd, v_lru_wx_fwd=v_lru_wx_fwd, v_lru_bx_fwd=v_lru_bx_fwd, v_lru_lam_fwd=v_lru_lam_fwd, v_lru_wa_bwd=v_lru_wa_bwd, v_lru_ba_bwd=v_lru_ba_bwd, v_lru_wx_bwd=v_lru_wx_bwd, v_lru_bx_bwd=v_lru_bx_bwd, v_lru_lam_bwd=v_lru_lam_bwd, v_rel_bias=v_rel_bias, v_norm_rnn_g=v_norm_rnn_g, v_norm_attn_g=v_norm_attn_g, v_w_out=v_w_out, v_mlp_norm_g=v_mlp_norm_g, v_w_up=v_w_up, v_w_down=v_w_down, v_final_norm_g=v_final_norm_g)
    weights = {n: given[n] for n in TWIN_WEIGHTS}
    shared = {n: given[n] for n in SHARED_INPUTS}
    per_example = {n: given[n] for n in ['x']}
    grad_fn = _jax.value_and_grad(_loss, argnums=(0, 1))

    def one_microbatch(ex, loss_target):
        ex = dict(ex)
        diff = ex.pop(TWIN_DIFF_INPUT)
        return grad_fn(weights, diff, {**shared, **ex}, loss_target)

    if N_MICROBATCH == 1:
        loss, (grad_w, grad_x) = one_microbatch(per_example, given["loss_target"])
    else:
        def body(carry, xs):
            loss_sum, grad_sum = carry
            l_k, (gw_k, gx_k) = one_microbatch(xs[0], xs[1])
            with _jax.named_scope("update"):
                return (loss_sum + l_k, _jax.tree.map(_jnp.add, grad_sum, gw_k)), gx_k

        init = (_jnp.zeros((), _jnp.float32), _jax.tree.map(_jnp.zeros_like, weights))
        (loss, grad_w), grad_x = _jax.lax.scan(body, init, (per_example, given["loss_target"]))
    with _jax.named_scope("update"):
        delta_w, new_m, new_v = {}, {}, {}
        for n in TWIN_WEIGHTS:
            delta_w[n], new_m[n], new_v[n] = _adamw(weights[n], grad_w[n], given["m_" + n], given["v_" + n])
    return (loss, grad_x, *[grad_w[n] for n in TWIN_WEIGHTS], *[delta_w[n] for n in TWIN_WEIGHTS],
            *[new_m[n] for n in TWIN_WEIGHTS], *[new_v[n] for n in TWIN_WEIGHTS])
```

```python
import functools
import math

import numpy as np
import jax
import jax.numpy as jnp
from jax import lax
from jax.experimental import pallas as pl
from jax.experimental.pallas import tpu as pltpu

F32 = jnp.float32
BF16 = jnp.bfloat16

D_MODEL = 1024
D_RNN = 512
D_ATTN = 512
N_HEADS = 8
HEAD_DIM = 64
N_RNN_BLOCKS = 8
RNN_BLOCK = 64
D_IN = 2 * D_RNN + 3 * D_ATTN
D_FF = 4 * D_MODEL
N_SHARD = 4
IN_BLK = D_IN // N_SHARD
OUT_BLK = D_MODEL // N_SHARD
FF_BLK = D_FF // N_SHARD
EPS = 1e-6
NEG_INF = -1e30
LRU_C = 8.0
DILATIONS = (1, 4, 16)
HALF_WIN = 64
Q_BLK = 128
K_WIN = 256
N_BUCKETS = 32
MAX_DISTANCE = 1024
ATTN_SCALE = HEAD_DIM ** -0.5

ADAM_LR = 0.001
ADAM_B1 = 0.9
ADAM_B2 = 0.999
ADAM_EPS = 1e-08
ADAM_WD = 0.01
ADAM_STEP = 10

TS = 512
TS_MLP = 256
TS_INPROJ_BWD = 256
SUB = 8
VMEM_LIMIT = 56 * 1024 * 1024
GELU_C0 = math.sqrt(2.0 / math.pi)
GELU_C1 = 0.044715

MESH = pl.DeviceIdType.MESH


def _params(n_grid=1):
    return pltpu.CompilerParams(vmem_limit_bytes=VMEM_LIMIT, dimension_semantics=("arbitrary",) * n_grid)


def _whole_vmem():
    return pl.BlockSpec(memory_space=pltpu.VMEM)


def _rows(width, tile=TS):
    return pl.BlockSpec((tile, width), lambda i: (i, 0))


def _sigmoid(z):
    return 1.0 / (1.0 + jnp.exp(-z))


def _log1p(u):
    w = 1.0 + u
    return jnp.where(w == 1.0, u, jnp.log(w) * (u / (w - 1.0)))


def _softplus(z):
    return jnp.maximum(z, 0.0) + _log1p(jnp.exp(-jnp.abs(z)))


def _expm1(x):
    small = jnp.abs(x) < 0.3
    xs = jnp.where(small, x, 0.0)
    poly = xs * (1.0 + xs * (0.5 + xs * (1.0 / 6 + xs * (1.0 / 24 + xs * (1.0 / 120 + xs * (1.0 / 720 + xs * (1.0 / 5040)))))))
    return jnp.where(small, poly, jnp.exp(x) - 1.0)


def _gelu_parts(g):
    inner = GELU_C0 * (g + GELU_C1 * g * g * g)
    t = jnp.tanh(inner)
    val = 0.5 * g * (1.0 + t)
    dinner = GELU_C0 * (1.0 + 3.0 * GELU_C1 * g * g)
    grad = 0.5 * (1.0 + t) + 0.5 * g * (1.0 - t * t) * dinner
    return val, grad


def _rms(x):
    rstd = lax.rsqrt(jnp.mean(x * x, axis=-1, keepdims=True) + EPS)
    return rstd, x * rstd


def _rms_bwd(dy, g, xhat, rstd):
    dxh = dy * g
    dx = rstd * (dxh - xhat * jnp.mean(dxh * xhat, axis=-1, keepdims=True))
    dg = jnp.sum(dy * xhat, axis=0, keepdims=True)
    return dx, dg


def _dot(a, b):
    return jnp.dot(a, b, preferred_element_type=F32)


def _dot_nt(a, b):
    return lax.dot_general(a, b, (((1,), (1,)), ((), ())), preferred_element_type=F32)


def _dot_tn(a, b):
    return lax.dot_general(a, b, (((0,), (0,)), ((), ())), preferred_element_type=F32)


def _shifted(tile, prev8, next8, k):
    n = tile.shape[0]
    row = lax.broadcasted_iota(jnp.int32, tile.shape, 0)
    if k == 0:
        return tile
    if k < 0:
        r = pltpu.roll(tile, -k, 0)
        for j in range(-k):
            r = jnp.where(row == j, prev8[SUB + j + k:SUB + j + k + 1, :], r)
        return r
    r = pltpu.roll(tile, n - k, 0)
    for j in range(k):
        r = jnp.where(row == n - k + j, next8[j:j + 1, :], r)
    return r


def _scan_tile(a_ref, b_ref, h_ref, carry_ref, reverse):
    n = a_ref.shape[0]
    width = a_ref.shape[1]
    groups = n // SUB
    row = lax.broadcasted_iota(jnp.int32, (SUB, width), 0)

    def step(i, carry):
        g = (groups - 1 - i) if reverse else i
        r0 = pl.multiple_of(g * SUB, SUB)
        a = a_ref[pl.ds(r0, SUB), :]
        b = b_ref[pl.ds(r0, SUB), :]
        for s in (1, 2, 4):
            if reverse:
                a_sh = pltpu.roll(a, SUB - s, 0)
                b_sh = pltpu.roll(b, SUB - s, 0)
                m = row < SUB - s
            else:
                a_sh = pltpu.roll(a, s, 0)
                b_sh = pltpu.roll(b, s, 0)
                m = row >= s
            b = jnp.where(m, a * b_sh + b, b)
            a = jnp.where(m, a * a_sh, a)
        h = b + a * carry
        h_ref[pl.ds(r0, SUB), :] = h
        edge = h[0:1, :] if reverse else h[SUB - 1:SUB, :]
        return jnp.broadcast_to(edge, (SUB, width))

    carry_ref[...] = lax.fori_loop(0, groups, step, carry_ref[...])


def _conv_fwd(xr, prev8, next8, cw, cb):
    y = cb + _shifted(xr, prev8, next8, -2) * cw[0:1, :]
    y = y + _shifted(xr, prev8, next8, -1) * cw[1:2, :]
    y = y + xr * cw[2:3, :]
    y = y + _shifted(xr, prev8, next8, 1) * cw[3:4, :]
    return y


def _lru_gates(xc, wa_ref, ba, wx_ref, bx, lam):
    xcb = xc.astype(BF16)
    r = _sigmoid(_dot(xcb, wa_ref[...]) + ba)
    i = _sigmoid(_dot(xcb, wx_ref[...]) + bx)
    cl = -LRU_C * _softplus(-lam)
    la = cl * r
    a = jnp.exp(la)
    mult = jnp.sqrt(-_expm1(2.0 * la))
    return xcb, r, i, cl, a, mult


def _inproj_fwd(x, g1, w_in):
    S = x.shape[0]

    def body(x_ref, g_ref, w_ref, xr_ref, gate_ref, q_ref, k_ref, v_ref):
        _, xh = _rms(x_ref[...])
        h = (xh * g_ref[...]).astype(BF16)
        proj = jnp.concatenate([_dot(h, w_ref[j]) for j in range(N_SHARD)], axis=-1)
        xr_ref[...] = proj[:, 0:512]
        gate_ref[...] = proj[:, 512:1024]
        q_ref[...] = proj[:, 1024:1536].astype(BF16)
        k_ref[...] = proj[:, 1536:2048].astype(BF16)
        v_ref[...] = proj[:, 2048:2560].astype(BF16)

    f = jax.ShapeDtypeStruct((S, 512), F32)
    b = jax.ShapeDtypeStruct((S, 512), BF16)
    return pl.pallas_call(
        body, grid=(S // TS,), name="inproj_fwd",
        in_specs=[_rows(D_MODEL), _whole_vmem(), _whole_vmem()],
        out_specs=[_rows(512)] * 5, out_shape=[f, f, b, b, b],
        compiler_params=_params(),
    )(x, g1, w_in)


def _halo_specs(S, order, tile=TS):
    per = tile // SUB
    last = S // SUB - 1
    return [
        pl.BlockSpec((tile, 512), lambda i: (order(i), 0)),
        pl.BlockSpec((SUB, 512), lambda i: (jnp.maximum(order(i) * per - 1, 0), 0)),
        pl.BlockSpec((SUB, 512), lambda i: (jnp.minimum((order(i) + 1) * per, last), 0)),
    ]


def _rnn_fwd(xr, conv_w, conv_b, wa, ba, wx, bx, lam, reverse):
    S = xr.shape[0]
    nt = S // TS
    order = (lambda i: nt - 1 - i) if reverse else (lambda i: i)

    def body(x_ref, xp_ref, xn_ref, cw_ref, cb_ref, wa_ref, ba_ref, wx_ref, bx_ref, lam_ref, h_ref, a_s, b_s, carry):
        i = pl.program_id(0)
        t = order(i)

        @pl.when(i == 0)
        def _():
            carry[...] = jnp.zeros_like(carry)

        prev8 = jnp.where(t > 0, xp_ref[...], 0.0)
        next8 = jnp.where(t < nt - 1, xn_ref[...], 0.0)
        xc = _conv_fwd(x_ref[...], prev8, next8, cw_ref[...], cb_ref[...])
        _, _, gi, _, a, mult = _lru_gates(xc, wa_ref, ba_ref[...], wx_ref, bx_ref[...], lam_ref[...])
        a_s[...] = a
        b_s[...] = mult * (gi * xc)
        _scan_tile(a_s, b_s, h_ref, carry, reverse)

    return pl.pallas_call(
        body, grid=(nt,), name="rnn_fwd_rev" if reverse else "rnn_fwd_fwd",
        in_specs=_halo_specs(S, order) + [_whole_vmem()] * 7,
        out_specs=pl.BlockSpec((TS, 512), lambda i: (order(i), 0)),
        out_shape=jax.ShapeDtypeStruct((S, 512), F32),
        scratch_shapes=[pltpu.VMEM((TS, 512), F32), pltpu.VMEM((TS, 512), F32), pltpu.VMEM((SUB, 512), F32)],
        compiler_params=_params(),
    )(xr, xr, xr, conv_w, conv_b, wa, ba, wx, bx, lam)


def _mix_fwd(o3, l3, hf, hb, gate, x, g_rnn, g_attn, w_out):
    S = x.shape[0]

    def body(o1, o2, o3_, l1, l2, l3_, hf_ref, hb_ref, gate_ref, x_ref, gr_ref, ga_ref, w_ref,
             x1_ref, mix_ref, ya_ref, lse_ref):
        la, lb, lc = l1[...], l2[...], l3_[...]
        m = jnp.maximum(jnp.maximum(la, lb), lc)
        ea, eb, ec = jnp.exp(la - m), jnp.exp(lb - m), jnp.exp(lc - m)
        den = ea + eb + ec
        lse_ref[...] = m + jnp.log(den)
        ya = (ea * o1[...] + eb * o2[...] + ec * o3_[...]) / den
        ya_ref[...] = ya
        gg, _ = _gelu_parts(gate_ref[...])
        yr = (hf_ref[...] + hb_ref[...]) * gg
        _, xh_r = _rms(yr)
        _, xh_a = _rms(ya)
        mix = jnp.concatenate([xh_r * gr_ref[...], xh_a * ga_ref[...]], axis=-1).astype(BF16)
        mix_ref[...] = mix
        acc = x_ref[...]
        for j in range(N_SHARD):
            acc = acc + _dot(mix[:, j * OUT_BLK:(j + 1) * OUT_BLK], w_ref[j])
        x1_ref[...] = acc

    f512 = jax.ShapeDtypeStruct((S, 512), F32)
    return pl.pallas_call(
        body, grid=(S // TS,), name="mix_fwd",
        in_specs=[_rows(512)] * 9 + [_rows(D_MODEL)] + [_whole_vmem()] * 3,
        out_specs=[_rows(D_MODEL), _rows(D_MODEL), _rows(512), _rows(512)],
        out_shape=[jax.ShapeDtypeStruct((S, D_MODEL), F32), jax.ShapeDtypeStruct((S, D_MODEL), BF16), f512, f512],
        compiler_params=_params(),
    )(*o3, *l3, hf, hb, gate, x, g_rnn, g_attn, w_out)


def _mlp_fwd_bwd(x1, target, g_mlp, g_fin, w_up, w_down):
    S = x1.shape[0]
    tm = TS_MLP

    def body(x1_ref, t_ref, gm_ref, gf_ref, wu_ref, wd_ref,
             dx1_ref, h2_ref, a2_ref, du_ref, dx2_ref, loss_ref, dgf_ref, dgm_ref, relu_s):
        @pl.when(pl.program_id(0) == 0)
        def _():
            loss_ref[...] = jnp.zeros_like(loss_ref)
            dgf_ref[...] = jnp.zeros_like(dgf_ref)
            dgm_ref[...] = jnp.zeros_like(dgm_ref)

        x1v = x1_ref[...]
        rstd1, xh1 = _rms(x1v)
        h2 = (xh1 * gm_ref[...]).astype(BF16)
        h2_ref[...] = h2
        x2 = x1v
        for j in range(N_SHARD):
            r = jnp.maximum(_dot(h2, wu_ref[j]), 0.0)
            relu_s[j] = r
            a2 = (r * r).astype(BF16)
            a2_ref[:, j * FF_BLK:(j + 1) * FF_BLK] = a2
            x2 = x2 + _dot(a2, wd_ref[j])
        rstd2, xh2 = _rms(x2)
        err = xh2 * gf_ref[...] - t_ref[...]
        loss_ref[...] += jnp.sum(err * err, axis=0, keepdims=True)
        dy = err * (1.0 / D_MODEL)
        dx2, dgf = _rms_bwd(dy, gf_ref[...], xh2, rstd2)
        dgf_ref[...] += dgf
        dx2b = dx2.astype(BF16)
        dx2_ref[...] = dx2b
        dh2 = jnp.zeros((tm, D_MODEL), F32)
        for j in range(N_SHARD):
            du = (_dot_nt(dx2b, wd_ref[j]) * (2.0 * relu_s[j])).astype(BF16)
            du_ref[:, j * FF_BLK:(j + 1) * FF_BLK] = du
            dh2 = dh2 + _dot_nt(du, wu_ref[j])
        dx1n, dgm = _rms_bwd(dh2, gm_ref[...], xh1, rstd1)
        dgm_ref[...] += dgm
        dx1_ref[...] = dx2 + dx1n

    vec = jax.ShapeDtypeStruct((1, D_MODEL), F32)
    return pl.pallas_call(
        body, grid=(S // tm,), name="mlp_fwd_bwd",
        in_specs=[_rows(D_MODEL, tm), _rows(D_MODEL, tm)] + [_whole_vmem()] * 4,
        out_specs=[_rows(D_MODEL, tm), _rows(D_MODEL, tm), _rows(D_FF, tm), _rows(D_FF, tm), _rows(D_MODEL, tm)]
        + [_whole_vmem()] * 3,
        out_shape=[jax.ShapeDtypeStruct((S, D_MODEL), F32), jax.ShapeDtypeStruct((S, D_MODEL), BF16),
                   jax.ShapeDtypeStruct((S, D_FF), BF16), jax.ShapeDtypeStruct((S, D_FF), BF16),
                   jax.ShapeDtypeStruct((S, D_MODEL), BF16), vec, vec, vec],
        scratch_shapes=[pltpu.VMEM((N_SHARD, tm, FF_BLK), F32)],
        compiler_params=_params(),
    )(x1, target, g_mlp, g_fin, w_up, w_down)


def _mix_bwd(dx1, w_out, ya, hf, hb, gate, g_rnn, g_attn):
    S = dx1.shape[0]

    def body(dx1_ref, w_ref, ya_ref, hf_ref, hb_ref, gate_ref, gr_ref, ga_ref,
             dhs_ref, dgate_ref, dya_ref, dx1b_ref, dgr_ref, dga_ref):
        @pl.when(pl.program_id(0) == 0)
        def _():
            dgr_ref[...] = jnp.zeros_like(dgr_ref)
            dga_ref[...] = jnp.zeros_like(dga_ref)

        dx1b = dx1_ref[...].astype(BF16)
        dx1b_ref[...] = dx1b
        dmix = jnp.concatenate([_dot_nt(dx1b, w_ref[j]) for j in range(N_SHARD)], axis=-1)
        gg, dgg = _gelu_parts(gate_ref[...])
        hs = hf_ref[...] + hb_ref[...]
        rstd_r, xh_r = _rms(hs * gg)
        dyr, dgr = _rms_bwd(dmix[:, 0:D_RNN], gr_ref[...], xh_r, rstd_r)
        dgr_ref[...] += dgr
        rstd_a, xh_a = _rms(ya_ref[...])
        dya, dga = _rms_bwd(dmix[:, D_RNN:], ga_ref[...], xh_a, rstd_a)
        dga_ref[...] += dga
        dya_ref[...] = dya
        dhs_ref[...] = dyr * gg
        dgate_ref[...] = dyr * hs * dgg

    f512 = jax.ShapeDtypeStruct((S, 512), F32)
    vec = jax.ShapeDtypeStruct((1, 512), F32)
    return pl.pallas_call(
        body, grid=(S // TS,), name="mix_bwd",
        in_specs=[_rows(D_MODEL), _whole_vmem()] + [_rows(512)] * 4 + [_whole_vmem()] * 2,
        out_specs=[_rows(512)] * 3 + [_rows(D_MODEL)] + [_whole_vmem()] * 2,
        out_shape=[f512, f512, f512, jax.ShapeDtypeStruct((S, D_MODEL), BF16), vec, vec],
        compiler_params=_params(),
    )(dx1, w_out, ya, hf, hb, gate, g_rnn, g_attn)


def _rnn_bwd(xr, h, dhs, conv_w, conv_b, wa, ba, wx, bx, lam, reverse):
    S = xr.shape[0]
    nt = S // TS
    order = (lambda i: i) if reverse else (lambda i: nt - 1 - i)
    per = TS // SUB
    last = S // SUB - 1
    if reverse:
        h_halo = pl.BlockSpec((SUB, 512), lambda i: (jnp.minimum((order(i) + 1) * per, last), 0))
    else:
        h_halo = pl.BlockSpec((SUB, 512), lambda i: (jnp.maximum(order(i) * per - 1, 0), 0))
    tile = pl.BlockSpec((TS, 512), lambda i: (order(i), 0))

    def body(x_ref, xp_ref, xn_ref, h_ref, hh_ref, dh_ref, cw_ref, cb_ref, wa_ref, ba_ref, wx_ref, bx_ref, lam_ref,
             dxc_ref, dwa_ref, dwx_ref, dvec_ref, a_s, g_s, carry, edge):
        i = pl.program_id(0)
        t = order(i)

        @pl.when(i == 0)
        def _():
            carry[...] = jnp.zeros_like(carry)
            edge[...] = jnp.zeros_like(edge)
            dwa_ref[...] = jnp.zeros_like(dwa_ref)
            dwx_ref[...] = jnp.zeros_like(dwx_ref)
            dvec_ref[...] = jnp.zeros_like(dvec_ref)

        prev8 = jnp.where(t > 0, xp_ref[...], 0.0)
        next8 = jnp.where(t < nt - 1, xn_ref[...], 0.0)
        xc = _conv_fwd(x_ref[...], prev8, next8, cw_ref[...], cb_ref[...])
        xcb, r, gi, cl, a, mult = _lru_gates(xc, wa_ref, ba_ref[...], wx_ref, bx_ref[...], lam_ref[...])
        hv = h_ref[...]
        if reverse:
            a_s[...] = _shifted(a, edge[...], None, -1)
            edge[...] = a[TS - SUB:TS, :]
            hh = jnp.where(t < nt - 1, hh_ref[...], 0.0)
            h_prev = _shifted(hv, None, hh, 1)
        else:
            a_s[...] = _shifted(a, None, edge[...], 1)
            edge[...] = a[0:SUB, :]
            hh = jnp.where(t > 0, hh_ref[...], 0.0)
            h_prev = _shifted(hv, hh, None, -1)
        _scan_tile(a_s, dh_ref, g_s, carry, not reverse)
        g = g_s[...]
        da = g * h_prev
        gm = g * mult
        d_i = gm * xc
        dmult = g * gi * xc
        dla = da * a - dmult * (a * a) / mult
        d_r = dla * cl
        dpre_r = d_r * r * (1.0 - r)
        dpre_i = d_i * gi * (1.0 - gi)
        dprb = dpre_r.astype(BF16)
        dpib = dpre_i.astype(BF16)
        dwa_ref[...] += _dot_tn(xcb, dprb)
        dwx_ref[...] += _dot_tn(xcb, dpib)
        dvec_ref[0:1, :] += jnp.sum(dpre_r, axis=0, keepdims=True)
        dvec_ref[1:2, :] += jnp.sum(dpre_i, axis=0, keepdims=True)
        dvec_ref[2:3, :] += jnp.sum(dla * r, axis=0, keepdims=True)
        dvec_ref[3:4, :] = dvec_ref[2:3, :] * (LRU_C * _sigmoid(-lam_ref[...]))
        dxc_ref[...] = gm * gi + _dot_nt(dprb, wa_ref[...]) + _dot_nt(dpib, wx_ref[...])

    sq = jax.ShapeDtypeStruct((D_RNN, D_RNN), F32)
    return pl.pallas_call(
        body, grid=(nt,), name="rnn_bwd_rev" if reverse else "rnn_bwd_fwd",
        in_specs=_halo_specs(S, order) + [tile, h_halo, tile] + [_whole_vmem()] * 7,
        out_specs=[tile, _whole_vmem(), _whole_vmem(), _whole_vmem()],
        out_shape=[jax.ShapeDtypeStruct((S, 512), F32), sq, sq, jax.ShapeDtypeStruct((SUB, 512), F32)],
        scratch_shapes=[pltpu.VMEM((TS, 512), F32), pltpu.VMEM((TS, 512), F32), pltpu.VMEM((SUB, 512), F32),
                        pltpu.VMEM((SUB, 512), F32)],
        compiler_params=_params(),
    )(xr, xr, xr, h, h, dhs, conv_w, conv_b, wa, ba, wx, bx, lam)


def _inproj_bwd(x, dx1, xr, dxc_f, dxc_b, dgate, dq3, dk3, dv3, g1, conv_w, w_in):
    S = x.shape[0]
    tb = TS_INPROJ_BWD
    nt = S // tb
    ident = lambda i: i

    def body(x_ref, dx1_ref, xr_ref, xrp_ref, xrn_ref, cf_ref, cfp_ref, cfn_ref, cb_ref, cbp_ref, cbn_ref, dgate_ref,
             dq1, dq2, dq3_, dk1, dk2, dk3_, dv1, dv2, dv3_, g_ref, cw_ref, w_ref,
             dx_ref, dw_ref, dg_ref, dcw_ref):
        i = pl.program_id(0)

        @pl.when(i == 0)
        def _():
            dw_ref[...] = jnp.zeros_like(dw_ref)
            dg_ref[...] = jnp.zeros_like(dg_ref)
            dcw_ref[...] = jnp.zeros_like(dcw_ref)

        first, last = i > 0, i < nt - 1
        dxc = cf_ref[...] + cb_ref[...]
        dxc_p = jnp.where(first, cfp_ref[...] + cbp_ref[...], 0.0)
        dxc_n = jnp.where(last, cfn_ref[...] + cbn_ref[...], 0.0)
        cw = cw_ref[...]
        dxr = (_shifted(dxc, dxc_p, dxc_n, 2) * cw[0:1, :] + _shifted(dxc, dxc_p, dxc_n, 1) * cw[1:2, :]
               + dxc * cw[2:3, :] + _shifted(dxc, dxc_p, dxc_n, -1) * cw[3:4, :])
        xrv = xr_ref[...]
        xr_p = jnp.where(first, xrp_ref[...], 0.0)
        xr_n = jnp.where(last, xrn_ref[...], 0.0)
        for k, off in enumerate((-2, -1, 0, 1)):
            dcw_ref[k:k + 1, :] += jnp.sum(dxc * _shifted(xrv, xr_p, xr_n, off), axis=0, keepdims=True)
        dcw_ref[4:5, :] += jnp.sum(dxc, axis=0, keepdims=True)
        dproj = jnp.concatenate(
            [dxr, dgate_ref[...], dq1[...] + dq2[...] + dq3_[...], dk1[...] + dk2[...] + dk3_[...],
             dv1[...] + dv2[...] + dv3_[...]], axis=-1).astype(BF16)
        xv = x_ref[...]
        rstd, xh = _rms(xv)
        hb = (xh * g_ref[...]).astype(BF16)
        dh = jnp.zeros((tb, D_MODEL), F32)
        for j in range(N_SHARD):
            dpj = dproj[:, j * IN_BLK:(j + 1) * IN_BLK]
            dh = dh + _dot_nt(dpj, w_ref[j])
            dw_ref[j] += _dot_tn(hb, dpj)
        dxn, dg = _rms_bwd(dh, g_ref[...], xh, rstd)
        dg_ref[...] += dg
        dx_ref[...] = dx1_ref[...] + dxn

    halo = _halo_specs(S, ident, tb)
    return pl.pallas_call(
        body, grid=(nt,), name="inproj_bwd",
        in_specs=[_rows(D_MODEL, tb), _rows(D_MODEL, tb)] + halo * 3 + [_rows(512, tb)] * 10 + [_whole_vmem()] * 3,
        out_specs=[_rows(D_MODEL, tb), _whole_vmem(), _whole_vmem(), _whole_vmem()],
        out_shape=[jax.ShapeDtypeStruct((S, D_MODEL), F32), jax.ShapeDtypeStruct((N_SHARD, D_MODEL, IN_BLK), F32),
                   jax.ShapeDtypeStruct((1, D_MODEL), F32), jax.ShapeDtypeStruct((SUB, 512), F32)],
        compiler_params=_params(),
    )(x, dx1, xr, xr, xr, dxc_f, dxc_f, dxc_f, dxc_b, dxc_b, dxc_b, dgate, *dq3, *dk3, *dv3, g1, conv_w, w_in)


def _dw_matmul(a, b, a_cols, b_cols, name):
    S = a.shape[0]
    tk = 1024
    a_shared = a.shape[1] == a_cols
    b_shared = b.shape[1] == b_cols

    def body(a_ref, b_ref, o_ref):
        @pl.when(pl.program_id(1) == 0)
        def _():
            o_ref[...] = jnp.zeros_like(o_ref)
        o_ref[0] += _dot_tn(a_ref[...], b_ref[...])

    return pl.pallas_call(
        body, grid=(N_SHARD, S // tk), name=name,
        in_specs=[pl.BlockSpec((tk, a_cols), (lambda j, k: (k, 0)) if a_shared else (lambda j, k: (k, j))),
                  pl.BlockSpec((tk, b_cols), (lambda j, k: (k, 0)) if b_shared else (lambda j, k: (k, j)))],
        out_specs=pl.BlockSpec((1, a_cols, b_cols), lambda j, k: (j, 0, 0)),
        out_shape=jax.ShapeDtypeStruct((N_SHARD, a_cols, b_cols), F32),
        compiler_params=_params(2),
    )(a, b)


def _t5_bucket_np(rel):
    nb = N_BUCKETS // 2
    max_exact = nb // 2
    ret = np.where(rel > 0, nb, 0)
    n = np.abs(rel)
    nf = np.maximum(n, 1).astype(np.float32)
    large = max_exact + (np.log(nf / np.float32(max_exact)) / np.float32(math.log(MAX_DISTANCE / max_exact))
                         * np.float32(nb - max_exact)).astype(np.int32)
    large = np.minimum(large, nb - 1)
    return ret + np.where(n < max_exact, n, large)


_VARIANT_OFFSETS = (0, -HALF_WIN, Q_BLK - K_WIN)


def _band_index():
    kk = np.arange(K_WIN)[None, :]
    ql = np.arange(Q_BLK)[:, None]
    rel = np.stack([kk - ql + off for off in _VARIANT_OFFSETS])
    return rel, np.abs(rel) <= HALF_WIN


def _bias_mats(rel_bias, dil):
    rel, valid = _band_index()
    bucket = _t5_bucket_np(np.clip(rel, -HALF_WIN, HALF_WIN) * dil)
    b = jnp.take(rel_bias, jnp.asarray(bucket.reshape(-1)), axis=0).reshape(3, Q_BLK, K_WIN, N_HEADS)
    b = jnp.where(jnp.asarray(valid)[..., None], b, NEG_INF)
    return jnp.transpose(b, (0, 3, 1, 2))


def _variant(qb, nq):
    return jnp.where(qb == 0, 0, jnp.where(qb == nq - 1, 2, 1))


def _win_start(qb, L):
    return pl.multiple_of(jnp.clip(qb * Q_BLK - HALF_WIN, 0, L - K_WIN), HALF_WIN)


def _attn_specs(L, dil):
    nq = L // Q_BLK
    qspec = pl.BlockSpec((Q_BLK, 128), lambda c, qb: (qb, c))
    kspec = pl.BlockSpec((L, 128), lambda c, qb: (0, c))
    bspec = pl.BlockSpec((1, 2, Q_BLK, K_WIN), lambda c, qb: (_variant(qb, nq), c % 4, 0, 0))
    return nq, qspec, kspec, bspec


def _head_masks():
    lane = lax.broadcasted_iota(jnp.int32, (Q_BLK, 128), 1)
    return lane < HEAD_DIM


def _attn_fwd(q, k, v, bias, dil):
    S = q.shape[0]
    L = S // dil
    qv, kv, vv = (t.reshape(L, dil * D_ATTN) for t in (q, k, v))
    nq, qspec, kspec, bspec = _attn_specs(L, dil)

    def body(q_ref, k_ref, v_ref, b_ref, o_ref, l_ref):
        qb = pl.program_id(1)
        st = _win_start(qb, L)
        kw = k_ref[pl.ds(st, K_WIN), :]
        vw = v_ref[pl.ds(st, K_WIN), :]
        qs = q_ref[...] * ATTN_SCALE
        h0 = _head_masks()
        outs, lses = [], []
        for h in range(2):
            qh = jnp.where(h0 if h == 0 else ~h0, qs, jnp.zeros_like(qs))
            s = _dot_nt(qh, kw) + b_ref[0, h]
            m = jnp.max(s, axis=-1, keepdims=True)
            p = jnp.exp(s - m)
            l = jnp.sum(p, axis=-1, keepdims=True)
            outs.append(_dot(p.astype(BF16), vw) / l)
            lses.append(m + jnp.log(l))
        o_ref[...] = jnp.where(h0, outs[0], outs[1])
        l_ref[...] = jnp.where(h0, lses[0], lses[1])

    shp = jax.ShapeDtypeStruct((L, dil * D_ATTN), F32)
    o, l = pl.pallas_call(
        body, grid=(dil * 4, nq), name=f"attn_fwd_d{dil}",
        in_specs=[qspec, kspec, kspec, bspec], out_specs=[qspec, qspec], out_shape=[shp, shp],
        compiler_params=_params(2),
    )(qv, kv, vv, bias)
    return o.reshape(S, D_ATTN), l.reshape(S, D_ATTN)


def _attn_bwd(q, k, v, bias, do, o, lse, dil):
    S = q.shape[0]
    L = S // dil
    qv, kv, vv = (t.reshape(L, dil * D_ATTN) for t in (q, k, v))
    dov, ov, lv = (t.reshape(L, dil * D_ATTN) for t in (do, o, lse))
    nq, qspec, kspec, bspec = _attn_specs(L, dil)
    ncol = dil * 4

    def body(q_ref, k_ref, v_ref, b_ref, do_ref, o_ref, l_ref, dq_ref, dk_ref, dv_ref, db_ref, db_s):
        c, qb = pl.program_id(0), pl.program_id(1)

        @pl.when((c == 0) & (qb == 0))
        def _():
            db_s[...] = jnp.zeros_like(db_s)

        @pl.when(qb == 0)
        def _():
            dk_ref[...] = jnp.zeros_like(dk_ref)
            dv_ref[...] = jnp.zeros_like(dv_ref)

        st = _win_start(qb, L)
        var = _variant(qb, nq)
        kw = k_ref[pl.ds(st, K_WIN), :]
        vw = v_ref[pl.ds(st, K_WIN), :]
        qs = q_ref[...] * ATTN_SCALE
        dof = do_ref[...]
        dob = dof.astype(BF16)
        prod = dof * o_ref[...]
        lsev = l_ref[...]
        h0 = _head_masks()
        dk_acc = jnp.zeros((K_WIN, 128), F32)
        dv_acc = jnp.zeros((K_WIN, 128), F32)
        dqs = []
        for h in range(2):
            hm = h0 if h == 0 else ~h0
            qh = jnp.where(hm, qs, jnp.zeros_like(qs))
            doh = jnp.where(hm, dob, jnp.zeros_like(dob))
            s = _dot_nt(qh, kw) + b_ref[0, h]
            p = jnp.exp(s - lsev[:, h * HEAD_DIM:h * HEAD_DIM + 1])
            dp = _dot_nt(doh, vw)
            dd = jnp.sum(jnp.where(hm, prod, 0.0), axis=-1, keepdims=True)
            ds = p * (dp - dd)
            db_s[var, (c % 4) * 2 + h] += ds
            dsb = ds.astype(BF16)
            dv_acc = dv_acc + _dot_tn(p.astype(BF16), doh)
            dk_acc = dk_acc + _dot_tn(dsb, qh)
            dqs.append(_dot(dsb, kw) * ATTN_SCALE)
        dq_ref[...] = jnp.where(h0, dqs[0], dqs[1])
        dk_ref[pl.ds(st, K_WIN), :] += dk_acc
        dv_ref[pl.ds(st, K_WIN), :] += dv_acc

        @pl.when((c == ncol - 1) & (qb == nq - 1))
        def _():
            db_ref[...] = db_s[...]

    shp = jax.ShapeDtypeStruct((L, dil * D_ATTN), F32)
    dbshape = (3, N_HEADS, Q_BLK, K_WIN)
    dq, dk, dv, db = pl.pallas_call(
        body, grid=(ncol, nq), name=f"attn_bwd_d{dil}",
        in_specs=[qspec, kspec, kspec, bspec, qspec, qspec, qspec],
        out_specs=[qspec, kspec, kspec, _whole_vmem()],
        out_shape=[shp, shp, shp, jax.ShapeDtypeStruct(dbshape, F32)],
        scratch_shapes=[pltpu.VMEM(dbshape, F32)],
        compiler_params=_params(2),
    )(qv, kv, vv, bias, dov, ov, lv)
    return dq.reshape(S, D_ATTN), dk.reshape(S, D_ATTN), dv.reshape(S, D_ATTN), db


def _bucket_onehots(dil):
    m = np.zeros((3, K_WIN, N_BUCKETS), np.float32)
    for var, off in enumerate(_VARIANT_OFFSETS):
        for rel in range(-HALF_WIN, HALF_WIN + 1):
            col = (rel - off + Q_BLK - 1) % K_WIN
            m[var, col, int(_t5_bucket_np(np.asarray(rel * dil)))] = 1.0
    return jnp.asarray(m)


def _bias_grad(dbs):
    onehots = [_bucket_onehots(d) for d in DILATIONS]
    flip = jnp.asarray(np.eye(Q_BLK, dtype=np.float32)[::-1].copy())

    def body(d1, d2, d3, m1, m2, m3, flip_ref, out_ref):
        hp = lax.Precision.HIGHEST
        acc = jnp.zeros((N_HEADS, N_BUCKETS), F32)
        for d_ref, m_ref in ((d1, m1), (d2, m2), (d3, m3)):
            for var in range(3):
                rows = []
                for h in range(N_HEADS):
                    xrev = jnp.dot(flip_ref[...], d_ref[var, h], precision=hp, preferred_element_type=F32)
                    y = pltpu.roll(xrev, 0, 1, stride=1, stride_axis=0)
                    rows.append(jnp.sum(y, axis=0, keepdims=True))
                vec = jnp.concatenate(rows, axis=0)
                acc = acc + jnp.dot(vec, m_ref[var], precision=hp, preferred_element_type=F32)
        out_ref[...] = acc

    return pl.pallas_call(
        body, name="bias_grad", out_shape=jax.ShapeDtypeStruct((N_HEADS, N_BUCKETS), F32),
        compiler_params=_params(0),
    )(*dbs, *onehots, flip)


def _block_diag(w):
    eye = jnp.eye(N_RNN_BLOCKS, dtype=w.dtype)
    return jnp.einsum("ncd,nm->ncmd", w, eye).reshape(D_RNN, D_RNN).astype(BF16)


def _diag_blocks(dense):
    d = dense.reshape(N_RNN_BLOCKS, RNN_BLOCK, N_RNN_BLOCKS, RNN_BLOCK)
    return jnp.stack([d[n, :, n, :] for n in range(N_RNN_BLOCKS)])


def _local_step(x, target, p):
    biases = [_bias_mats(p["rel_bias"], d) for d in DILATIONS]
    lru = {}
    for dname in ("fwd", "bwd"):
        lru[dname] = (_block_diag(p["lru_wa_" + dname]), p["lru_ba_" + dname], _block_diag(p["lru_wx_" + dname]),
                      p["lru_bx_" + dname], p["lru_lam_" + dname])

    xr, gate, q, k, v = _inproj_fwd(x, p["attn_norm_g"], p["w_in"])
    hf = _rnn_fwd(xr, p["conv_w"], p["conv_b"], *lru["fwd"], reverse=False)
    hb = _rnn_fwd(xr, p["conv_w"], p["conv_b"], *lru["bwd"], reverse=True)
    outs, lses = [], []
    for d, bias in zip(DILATIONS, biases):
        o, l = _attn_fwd(q, k, v, bias, d)
        outs.append(o)
        lses.append(l)
    x1, mixb, ya, lse = _mix_fwd(outs, lses, hf, hb, gate, x, p["norm_rnn_g"], p["norm_attn_g"], p["w_out"])
    dx1, h2b, a2b, dub, dx2b, loss_vec, dg_fin, dg_mlp = _mlp_fwd_bwd(
        x1, target, p["mlp_norm_g"], p["final_norm_g"], p["w_up"], p["w_down"])
    dhs, dgate, dya, dx1b, dg_rnn, dg_attn = _mix_bwd(dx1, p["w_out"], ya, hf, hb, gate, p["norm_rnn_g"], p["norm_attn_g"])
    dw_up = _dw_matmul(h2b, dub, D_MODEL, FF_BLK, "dw_up")
    dw_down = _dw_matmul(a2b, dx2b, FF_BLK, D_MODEL, "dw_down")
    dw_out = _dw_matmul(mixb, dx1b, OUT_BLK, D_MODEL, "dw_out")
    dqs, dks, dvs, dbs = [], [], [], []
    for d, bias in zip(DILATIONS, biases):
        dq, dk, dv, db = _attn_bwd(q, k, v, bias, dya, ya, lse, d)
        dqs.append(dq)
        dks.append(dk)
        dvs.append(dv)
        dbs.append(db)
    d_rel_bias = _bias_grad(dbs).T
    dxc_f, dwa_f, dwx_f, dvec_f = _rnn_bwd(xr, hf, dhs, p["conv_w"], p["conv_b"], *lru["fwd"], reverse=False)
    dxc_b, dwa_b, dwx_b, dvec_b = _rnn_bwd(xr, hb, dhs, p["conv_w"], p["conv_b"], *lru["bwd"], reverse=True)
    grad_x, dw_in, dg1, dconv = _inproj_bwd(x, dx1, xr, dxc_f, dxc_b, dgate, dqs, dks, dvs,
                                            p["attn_norm_g"], p["conv_w"], p["w_in"])
    small = {
        "attn_norm_g": dg1, "conv_w": dconv[0:4], "conv_b": dconv[4:5],
        "lru_wa_fwd": _diag_blocks(dwa_f), "lru_ba_fwd": dvec_f[0:1], "lru_wx_fwd": _diag_blocks(dwx_f),
        "lru_bx_fwd": dvec_f[1:2], "lru_lam_fwd": dvec_f[3:4],
        "lru_wa_bwd": _diag_blocks(dwa_b), "lru_ba_bwd": dvec_b[0:1], "lru_wx_bwd": _diag_blocks(dwx_b),
        "lru_bx_bwd": dvec_b[1:2], "lru_lam_bwd": dvec_b[3:4],
        "rel_bias": d_rel_bias, "norm_rnn_g": dg_rnn, "norm_attn_g": dg_attn,
        "mlp_norm_g": dg_mlp, "final_norm_g": dg_fin,
    }
    big = {"w_in": dw_in, "w_out": dw_out, "w_up": dw_up, "w_down": dw_down}
    return loss_vec, grad_x, small, big


BIG = ("w_in", "w_out", "w_up", "w_down")
BIG_SHARD = {"w_in": (D_MODEL, IN_BLK), "w_out": (OUT_BLK, D_MODEL), "w_up": (D_MODEL, FF_BLK), "w_down": (FF_BLK, D_MODEL)}
N_BIG = len(BIG)
N_CHIP_PEERS = 3
ANY = pl.BlockSpec(memory_space=pl.ANY)


def _place():
    x, y, c = lax.axis_index("x"), lax.axis_index("y"), lax.axis_index("c")
    chips = [(1 - x, y), (x, 1 - y), (1 - x, 1 - y)]
    return x, y, c, chips


def _remote(src, dst, send_sem, recv_sem, dev):
    return pltpu.make_async_remote_copy(src_ref=src, dst_ref=dst, send_sem=send_sem, recv_sem=recv_sem,
                                        device_id=dev, device_id_type=MESH)


def _allgather_weights(shards, conv_w_shard):
    def body(*refs):
        ins, outs = refs[:N_BIG + 1], refs[N_BIG + 1:2 * N_BIG + 2]
        ici_send, ici_recv, d2d_send, d2d_recv, cw_send, cw_recv, loc_sem = refs[2 * N_BIG + 2:]
        x, y, c, chips = _place()
        me = 2 * x + y
        sibling = (x, y, 1 - c)
        local = [pltpu.make_async_copy(ins[w], outs[w].at[me], loc_sem.at[w]) for w in range(N_BIG + 1)]
        for cp in local:
            cp.start()
        pending = list(local)
        halves = [BIG_SHARD[n][0] // 2 for n in BIG]

        def piece(w, chip, core_half):
            return outs[w].at[chip, pl.ds(core_half * halves[w], halves[w])]

        for w in range(N_BIG):
            for k, (px, py) in enumerate(chips):
                cp = _remote(ins[w].at[pl.ds(c * halves[w], halves[w])], piece(w, me, c),
                             ici_send.at[w, k], ici_recv.at[w, k], (px, py, c))
                cp.start()
                pending.append(cp)
        for k, (px, py) in enumerate(chips):
            cp = _remote(ins[N_BIG], outs[N_BIG].at[me], cw_send.at[k], cw_recv.at[k], (px, py, c))
            cp.start()
            pending.append(cp)
        for w in range(N_BIG):
            for k, (px, py) in enumerate(chips):
                landed = piece(w, 2 * px + py, c)
                _remote(landed, landed, ici_send.at[w, k], ici_recv.at[w, k], (px, py, c)).wait_recv()
                cp = _remote(landed, landed, d2d_send.at[w, k], d2d_recv.at[w, k], sibling)
                cp.start()
                pending.append(cp)
        for w in range(N_BIG):
            for k, (px, py) in enumerate(chips):
                other = piece(w, 2 * px + py, 1 - c)
                _remote(other, other, d2d_send.at[w, k], d2d_recv.at[w, k], sibling).wait_recv()
        for k, (px, py) in enumerate(chips):
            got = outs[N_BIG].at[2 * px + py]
            _remote(got, got, cw_send.at[k], cw_recv.at[k], (px, py, c)).wait_recv()
        for cp in pending[:N_BIG + 1]:
            cp.wait()
        for cp in pending[N_BIG + 1:]:
            cp.wait_send()

    out_shape = [jax.ShapeDtypeStruct((N_SHARD,) + BIG_SHARD[n], BF16) for n in BIG]
    out_shape.append(jax.ShapeDtypeStruct((N_SHARD,) + conv_w_shard.shape, F32))
    pair = [pltpu.SemaphoreType.DMA((N_BIG, N_CHIP_PEERS))] * 2
    return pl.pallas_call(
        body, name="allgather_weights", in_specs=[ANY] * (N_BIG + 1), out_specs=[ANY] * (N_BIG + 1), out_shape=out_shape,
        scratch_shapes=pair + pair + [pltpu.SemaphoreType.DMA((N_CHIP_PEERS,))] * 2 + [pltpu.SemaphoreType.DMA((N_BIG + 1,))],
        compiler_params=pltpu.CompilerParams(has_side_effects=True),
    )(*shards, conv_w_shard)


def _pair_exchange(grads):
    def body(*refs):
        ins, outs = refs[:N_BIG], refs[N_BIG:2 * N_BIG]
        send, recv = refs[2 * N_BIG:]
        x, y, c, _ = _place()
        cps = []
        for w, n in enumerate(BIG):
            h = BIG_SHARD[n][0] // 2
            cp = _remote(ins[w].at[:, pl.ds((1 - c) * h, h), :], outs[w], send.at[w], recv.at[w], (x, y, 1 - c))
            cp.start()
            cps.append(cp)
        for cp in cps:
            cp.wait()

    out_shape = [jax.ShapeDtypeStruct((N_SHARD, BIG_SHARD[n][0] // 2, BIG_SHARD[n][1]), F32) for n in BIG]
    return pl.pallas_call(
        body, name="grad_pair_exchange", in_specs=[ANY] * N_BIG, out_specs=[ANY] * N_BIG, out_shape=out_shape,
        scratch_shapes=[pltpu.SemaphoreType.DMA((N_BIG,))] * 2,
        compiler_params=pltpu.CompilerParams(has_side_effects=True),
    )(*grads)


def _pair_add(core, grad, other, name):
    _, r, cols = grad.shape
    h = r // 2
    th = min(h, 256)
    per = h // th

    def body(c_ref, g_ref, o_ref, out_ref):
        out_ref[...] = (g_ref[...] + o_ref[...]).astype(BF16)

    return pl.pallas_call(
        body, name=name,
        grid_spec=pltpu.PrefetchScalarGridSpec(
            num_scalar_prefetch=1, grid=(N_SHARD, per),
            in_specs=[pl.BlockSpec((1, th, cols), lambda j, i, c_ref: (j, c_ref[0] * per + i, 0)),
                      pl.BlockSpec((1, th, cols), lambda j, i, c_ref: (j, i, 0))],
            out_specs=pl.BlockSpec((1, th, cols), lambda j, i, c_ref: (j, i, 0))),
        out_shape=jax.ShapeDtypeStruct((N_SHARD, h, cols), BF16),
        compiler_params=_params(2),
    )(core, grad, other)


def _chip_exchange(parts):
    def body(*refs):
        ins, outs = refs[:N_BIG], refs[N_BIG:2 * N_BIG]
        send, recv, loc_sem = refs[2 * N_BIG:]
        x, y, c, chips = _place()
        me = 2 * x + y
        local = [pltpu.make_async_copy(ins[w].at[me], outs[w].at[me], loc_sem.at[w]) for w in range(N_BIG)]
        for cp in local:
            cp.start()
        sends = []
        for w in range(N_BIG):
            for k, (px, py) in enumerate(chips):
                cp = _remote(ins[w].at[2 * px + py], outs[w].at[me], send.at[w, k], recv.at[w, k], (px, py, c))
                cp.start()
                sends.append(cp)
        for w in range(N_BIG):
            for k, (px, py) in enumerate(chips):
                got = outs[w].at[2 * px + py]
                _remote(got, got, send.at[w, k], recv.at[w, k], (px, py, c)).wait_recv()
        for cp in sends:
            cp.wait_send()
        for cp in local:
            cp.wait()

    out_shape = [jax.ShapeDtypeStruct(p.shape, BF16) for p in parts]
    return pl.pallas_call(
        body, name="grad_chip_exchange", in_specs=[ANY] * N_BIG, out_specs=[ANY] * N_BIG, out_shape=out_shape,
        scratch_shapes=[pltpu.SemaphoreType.DMA((N_BIG, N_CHIP_PEERS))] * 2 + [pltpu.SemaphoreType.DMA((N_BIG,))],
        compiler_params=pltpu.CompilerParams(has_side_effects=True),
    )(*parts)


def _chip_sum(parts, name):
    _, h, cols = parts.shape
    th = min(h, 256)

    def body(p_ref, out_ref):
        acc = p_ref[0].astype(F32)
        for j in range(1, N_SHARD):
            acc = acc + p_ref[j].astype(F32)
        out_ref[...] = acc

    return pl.pallas_call(
        body, name=name, grid=(h // th,),
        in_specs=[pl.BlockSpec((N_SHARD, th, cols), lambda i: (0, i, 0))],
        out_specs=pl.BlockSpec((th, cols), lambda i: (i, 0)),
        out_shape=jax.ShapeDtypeStruct((h, cols), F32),
        compiler_params=_params(),
    )(parts)


def _pair_share(halves):
    def body(*refs):
        ins, outs = refs[:N_BIG], refs[N_BIG:2 * N_BIG]
        send, recv, loc_sem = refs[2 * N_BIG:]
        x, y, c, _ = _place()
        local, sends = [], []
        for w, n in enumerate(BIG):
            h = BIG_SHARD[n][0] // 2
            mine = outs[w].at[pl.ds(c * h, h)]
            lc = pltpu.make_async_copy(ins[w], mine, loc_sem.at[w])
            lc.start()
            local.append(lc)
            cp = _remote(ins[w], mine, send.at[w], recv.at[w], (x, y, 1 - c))
            cp.start()
            sends.append(cp)
        for w, n in enumerate(BIG):
            h = BIG_SHARD[n][0] // 2
            theirs = outs[w].at[pl.ds((1 - c) * h, h)]
            _remote(theirs, theirs, send.at[w], recv.at[w], (x, y, 1 - c)).wait_recv()
        for cp in sends:
            cp.wait_send()
        for cp in local:
            cp.wait()

    out_shape = [jax.ShapeDtypeStruct(BIG_SHARD[n], F32) for n in BIG]
    return pl.pallas_call(
        body, name="grad_pair_share", in_specs=[ANY] * N_BIG, out_specs=[ANY] * N_BIG, out_shape=out_shape,
        scratch_shapes=[pltpu.SemaphoreType.DMA((N_BIG,))] * 3,
        compiler_params=pltpu.CompilerParams(has_side_effects=True),
    )(*halves)


N_DEV = 8


def _allreduce_small(vec):
    rows = vec.shape[0]

    def body(v_ref, sum_ref, gat_ref, send, recv, loc_sem):
        x, y, c, _ = _place()
        me = 4 * x + 2 * y + c
        lc = pltpu.make_async_copy(v_ref, gat_ref.at[me], loc_sem)
        lc.start()
        peers = []
        for fx in (0, 1):
            for fy in (0, 1):
                for fc in (0, 1):
                    if fx or fy or fc:
                        peers.append(((1 - x) if fx else x, (1 - y) if fy else y, (1 - c) if fc else c))
        sends = []
        for k, dev in enumerate(peers):
            cp = _remote(v_ref, gat_ref.at[me], send.at[k], recv.at[k], dev)
            cp.start()
            sends.append(cp)
        for k, (px, py, pc) in enumerate(peers):
            got = gat_ref.at[4 * px + 2 * py + pc]
            _remote(got, got, send.at[k], recv.at[k], (px, py, pc)).wait_recv()
        for cp in sends:
            cp.wait_send()
        lc.wait()
        acc = gat_ref[0]
        for j in range(1, N_DEV):
            acc = acc + gat_ref[j]
        sum_ref[...] = acc

    total, _ = pl.pallas_call(
        body, name="allreduce_small",
        in_specs=[_whole_vmem()], out_specs=[_whole_vmem(), _whole_vmem()],
        out_shape=[jax.ShapeDtypeStruct((rows, 128), F32), jax.ShapeDtypeStruct((N_DEV, rows, 128), F32)],
        scratch_shapes=[pltpu.SemaphoreType.DMA((N_DEV - 1,))] * 2 + [pltpu.SemaphoreType.DMA(())],
        compiler_params=pltpu.CompilerParams(has_side_effects=True, vmem_limit_bytes=VMEM_LIMIT),
    )(vec)
    return total


def _adamw(w, g, m, v, name):
    rows, cols = w.shape
    tr = 256 if rows % 256 == 0 else rows
    c1 = 1.0 - ADAM_B1 ** ADAM_STEP
    c2 = 1.0 - ADAM_B2 ** ADAM_STEP

    def body(w_ref, g_ref, m_ref, v_ref, d_ref, m2_ref, v2_ref):
        gv = g_ref[...]
        m2 = ADAM_B1 * m_ref[...] + (1.0 - ADAM_B1) * gv
        v2 = ADAM_B2 * v_ref[...] + (1.0 - ADAM_B2) * (gv * gv)
        m2_ref[...] = m2
        v2_ref[...] = v2
        d_ref[...] = -ADAM_LR * ((m2 / c1) / (jnp.sqrt(v2 / c2) + ADAM_EPS) + ADAM_WD * w_ref[...])

    spec = pl.BlockSpec((tr, cols), lambda i: (i, 0))
    shp = jax.ShapeDtypeStruct((rows, cols), F32)
    return pl.pallas_call(
        body, name=name, grid=(rows // tr,), in_specs=[spec] * 4, out_specs=[spec] * 3, out_shape=[shp] * 3,
        compiler_params=_params(),
    )(w, g, m, v)


SMALL = (
    ("attn_norm_g", (1, 1024)), ("conv_w", (1, 4, 512)), ("conv_b", (1, 512)),
    ("lru_wa_fwd", (1, 8, 64, 64)), ("lru_ba_fwd", (1, 512)), ("lru_wx_fwd", (1, 8, 64, 64)), ("lru_bx_fwd", (1, 512)),
    ("lru_lam_fwd", (1, 512)),
    ("lru_wa_bwd", (1, 8, 64, 64)), ("lru_ba_bwd", (1, 512)), ("lru_wx_bwd", (1, 8, 64, 64)), ("lru_bx_bwd", (1, 512)),
    ("lru_lam_bwd", (1, 512)),
    ("rel_bias", (32, 8)), ("norm_rnn_g", (1, 512)), ("norm_attn_g", (1, 512)), ("mlp_norm_g", (1, 1024)),
    ("final_norm_g", (1024,)),
)
PACK_ROW = 8 * 128


def _pack(parts):
    flat = jnp.concatenate([p.reshape(-1) for p in parts])
    pad = (-flat.shape[0]) % PACK_ROW
    return jnp.pad(flat, (0, pad)).reshape(-1, 128)


def _unpack(packed, shapes):
    flat = packed.reshape(-1)
    out, off = [], 0
    for shp in shapes:
        n = int(np.prod(shp))
        out.append(flat[off:off + n].reshape(shp))
        off += n
    return out


WEIGHT_ORDER = ("attn_norm_g", "w_in", "conv_w", "conv_b", "lru_wa_fwd", "lru_ba_fwd", "lru_wx_fwd", "lru_bx_fwd",
                "lru_lam_fwd", "lru_wa_bwd", "lru_ba_bwd", "lru_wx_bwd", "lru_bx_bwd", "lru_lam_bwd", "rel_bias",
                "norm_rnn_g", "norm_attn_g", "w_out", "mlp_norm_g", "w_up", "w_down", "final_norm_g")


def kernel(x, attn_norm_g, w_in, conv_w, conv_b, lru_wa_fwd, lru_ba_fwd, lru_wx_fwd, lru_bx_fwd, lru_lam_fwd, lru_wa_bwd, lru_ba_bwd, lru_wx_bwd, lru_bx_bwd, lru_lam_bwd, rel_bias, norm_rnn_g, norm_attn_g, w_out, mlp_norm_g, w_up, w_down, final_norm_g, loss_target, m_attn_norm_g, m_w_in, m_conv_w, m_conv_b, m_lru_wa_fwd, m_lru_ba_fwd, m_lru_wx_fwd, m_lru_bx_fwd, m_lru_lam_fwd, m_lru_wa_bwd, m_lru_ba_bwd, m_lru_wx_bwd, m_lru_bx_bwd, m_lru_lam_bwd, m_rel_bias, m_norm_rnn_g, m_norm_attn_g, m_w_out, m_mlp_norm_g, m_w_up, m_w_down, m_final_norm_g, v_attn_norm_g, v_w_in, v_conv_w, v_conv_b, v_lru_wa_fwd, v_lru_ba_fwd, v_lru_wx_fwd, v_lru_bx_fwd, v_lru_lam_fwd, v_lru_wa_bwd, v_lru_ba_bwd, v_lru_wx_bwd, v_lru_bx_bwd, v_lru_lam_bwd, v_rel_bias, v_norm_rnn_g, v_norm_attn_g, v_w_out, v_mlp_norm_g, v_w_up, v_w_down, v_final_norm_g):
    w = dict(attn_norm_g=attn_norm_g, w_in=w_in, conv_w=conv_w, conv_b=conv_b, lru_wa_fwd=lru_wa_fwd, lru_ba_fwd=lru_ba_fwd,
             lru_wx_fwd=lru_wx_fwd, lru_bx_fwd=lru_bx_fwd, lru_lam_fwd=lru_lam_fwd, lru_wa_bwd=lru_wa_bwd,
             lru_ba_bwd=lru_ba_bwd, lru_wx_bwd=lru_wx_bwd, lru_bx_bwd=lru_bx_bwd, lru_lam_bwd=lru_lam_bwd,
             rel_bias=rel_bias, norm_rnn_g=norm_rnn_g, norm_attn_g=norm_attn_g, w_out=w_out, mlp_norm_g=mlp_norm_g,
             w_up=w_up, w_down=w_down, final_norm_g=final_norm_g)
    m = dict(attn_norm_g=m_attn_norm_g, w_in=m_w_in, conv_w=m_conv_w, conv_b=m_conv_b, lru_wa_fwd=m_lru_wa_fwd,
             lru_ba_fwd=m_lru_ba_fwd, lru_wx_fwd=m_lru_wx_fwd, lru_bx_fwd=m_lru_bx_fwd, lru_lam_fwd=m_lru_lam_fwd,
             lru_wa_bwd=m_lru_wa_bwd, lru_ba_bwd=m_lru_ba_bwd, lru_wx_bwd=m_lru_wx_bwd, lru_bx_bwd=m_lru_bx_bwd,
             lru_lam_bwd=m_lru_lam_bwd, rel_bias=m_rel_bias, norm_rnn_g=m_norm_rnn_g, norm_attn_g=m_norm_attn_g,
             w_out=m_w_out, mlp_norm_g=m_mlp_norm_g, w_up=m_w_up, w_down=m_w_down, final_norm_g=m_final_norm_g)
    v = dict(attn_norm_g=v_attn_norm_g, w_in=v_w_in, conv_w=v_conv_w, conv_b=v_conv_b, lru_wa_fwd=v_lru_wa_fwd,
             lru_ba_fwd=v_lru_ba_fwd, lru_wx_fwd=v_lru_wx_fwd, lru_bx_fwd=v_lru_bx_fwd, lru_lam_fwd=v_lru_lam_fwd,
             lru_wa_bwd=v_lru_wa_bwd, lru_ba_bwd=v_lru_ba_bwd, lru_wx_bwd=v_lru_wx_bwd, lru_bx_bwd=v_lru_bx_bwd,
             lru_lam_bwd=v_lru_lam_bwd, rel_bias=v_rel_bias, norm_rnn_g=v_norm_rnn_g, norm_attn_g=v_norm_attn_g,
             w_out=v_w_out, mlp_norm_g=v_mlp_norm_g, w_up=v_w_up, w_down=v_w_down, final_norm_g=v_final_norm_g)

    chip = lax.axis_index("x") * 2 + lax.axis_index("y")
    core = lax.axis_index("c")

    shards = [w[n][0].astype(BF16) for n in BIG]
    *gathered, conv_all = _allgather_weights(shards, w["conv_w"][0])
    p = {n: (t[0] if t.ndim >= 3 else t) for n, t in w.items() if n not in BIG}
    p["final_norm_g"] = w["final_norm_g"].reshape(1, D_MODEL)
    p["conv_w"] = jnp.transpose(conv_all, (1, 0, 2)).reshape(4, D_RNN)
    p.update(dict(zip(BIG, gathered)))

    loss_vec, grad_x, small, big = _local_step(x[0], loss_target[0], p)

    grads = [big[n] for n in BIG]
    others = _pair_exchange(grads)
    core_arr = core.reshape(1).astype(jnp.int32)
    parts = [_pair_add(core_arr, g, o, "grad_pair_add_" + n) for n, g, o in zip(BIG, grads, others)]
    landed = _chip_exchange(parts)
    halves = [_chip_sum(t, "grad_chip_sum_" + n) for n, t in zip(BIG, landed)]
    reduced = dict(zip(BIG, _pair_share(halves)))

    loss_local = (0.5 / D_MODEL) * jnp.sum(loss_vec)
    packed = _pack([small[n].reshape(shp) for n, shp in SMALL] + [loss_local.reshape(1)])
    total = _allreduce_small(packed)
    *small_g, loss = _unpack(total, [shp for _, shp in SMALL] + [(1,)])
    g = dict(zip([n for n, _ in SMALL], small_g))
    g["conv_w"] = lax.dynamic_slice_in_dim(g["conv_w"], chip * (D_RNN // N_SHARD), D_RNN // N_SHARD, axis=2)
    for n in BIG:
        g[n] = reduced[n][None]

    delta, new_m, new_v = {}, {}, {}
    for n in BIG:
        d2, m2, v2 = _adamw(w[n][0], reduced[n], m[n][0], v[n][0], "adamw_" + n)
        delta[n], new_m[n], new_v[n] = d2[None], m2[None], v2[None]
    names = [n for n, _ in SMALL]
    shapes = [w[n].shape for n in names]
    d2, m2, v2 = _adamw(_pack([w[n] for n in names]), _pack([g[n] for n in names]), _pack([m[n] for n in names]),
                        _pack([v[n] for n in names]), "adamw_small")
    for dst, src in ((delta, d2), (new_m, m2), (new_v, v2)):
        dst.update(dict(zip(names, _unpack(src, shapes))))

    return (loss.reshape(()), grad_x[None], *[g[n] for n in WEIGHT_ORDER], *[delta[n] for n in WEIGHT_ORDER],
            *[new_m[n] for n in WEIGHT_ORDER], *[new_v[n] for n in WEIGHT_ORDER])
```

```python
import functools
import math

import numpy as np
import jax
import jax.numpy as jnp
from jax import lax
from jax.experimental import pallas as pl
from jax.experimental.pallas import tpu as pltpu

F32 = jnp.float32
BF16 = jnp.bfloat16

D_MODEL = 1024
D_RNN = 512
D_ATTN = 512
N_HEADS = 8
HEAD_DIM = 64
N_RNN_BLOCKS = 8
RNN_BLOCK = 64
D_IN = 2 * D_RNN + 3 * D_ATTN
D_FF = 4 * D_MODEL
N_SHARD = 4
IN_BLK = D_IN // N_SHARD
OUT_BLK = D_MODEL // N_SHARD
FF_BLK = D_FF // N_SHARD
EPS = 1e-6
NEG_INF = -1e30
LRU_C = 8.0
DILATIONS = (1, 4, 16)
HALF_WIN = 64
Q_BLK = 128
K_WIN = 256
N_BUCKETS = 32
MAX_DISTANCE = 1024
ATTN_SCALE = HEAD_DIM ** -0.5

ADAM_LR = 0.001
ADAM_B1 = 0.9
ADAM_B2 = 0.999
ADAM_EPS = 1e-08
ADAM_WD = 0.01
ADAM_STEP = 10

TS = 512
TS_MLP = 256
TS_INPROJ_BWD = 256
ATTN_SUB = 4
SUB = 8
VMEM_LIMIT = 56 * 1024 * 1024
GELU_C0 = math.sqrt(2.0 / math.pi)
GELU_C1 = 0.044715

MESH = pl.DeviceIdType.MESH


def _params(n_grid=1):
    return pltpu.CompilerParams(vmem_limit_bytes=VMEM_LIMIT, dimension_semantics=("arbitrary",) * n_grid)


def _whole_vmem():
    return pl.BlockSpec(memory_space=pltpu.VMEM)


def _rows(width, tile=TS):
    return pl.BlockSpec((tile, width), lambda i: (i, 0))


def _sigmoid(z):
    return 1.0 / (1.0 + jnp.exp(-z))


def _log1p(u):
    w = 1.0 + u
    return jnp.where(w == 1.0, u, jnp.log(w) * (u / (w - 1.0)))


def _softplus(z):
    return jnp.maximum(z, 0.0) + _log1p(jnp.exp(-jnp.abs(z)))


def _expm1(x):
    small = jnp.abs(x) < 0.3
    xs = jnp.where(small, x, 0.0)
    poly = xs * (1.0 + xs * (0.5 + xs * (1.0 / 6 + xs * (1.0 / 24 + xs * (1.0 / 120 + xs * (1.0 / 720 + xs * (1.0 / 5040)))))))
    return jnp.where(small, poly, jnp.exp(x) - 1.0)


def _gelu_parts(g):
    inner = GELU_C0 * (g + GELU_C1 * g * g * g)
    t = jnp.tanh(inner)
    val = 0.5 * g * (1.0 + t)
    dinner = GELU_C0 * (1.0 + 3.0 * GELU_C1 * g * g)
    grad = 0.5 * (1.0 + t) + 0.5 * g * (1.0 - t * t) * dinner
    return val, grad


def _rms(x):
    rstd = lax.rsqrt(jnp.mean(x * x, axis=-1, keepdims=True) + EPS)
    return rstd, x * rstd


def _rms_bwd(dy, g, xhat, rstd):
    dxh = dy * g
    dx = rstd * (dxh - xhat * jnp.mean(dxh * xhat, axis=-1, keepdims=True))
    dg = jnp.sum(dy * xhat, axis=0, keepdims=True)
    return dx, dg


def _dot(a, b):
    return jnp.dot(a, b, preferred_element_type=F32)


def _dot_nt(a, b):
    return lax.dot_general(a, b, (((1,), (1,)), ((), ())), preferred_element_type=F32)


def _dot_tn(a, b):
    return lax.dot_general(a, b, (((0,), (0,)), ((), ())), preferred_element_type=F32)


def _shifted(tile, prev8, next8, k):
    n = tile.shape[0]
    row = lax.broadcasted_iota(jnp.int32, tile.shape, 0)
    if k == 0:
        return tile
    if k < 0:
        r = pltpu.roll(tile, -k, 0)
        for j in range(-k):
            r = jnp.where(row == j, prev8[SUB + j + k:SUB + j + k + 1, :], r)
        return r
    r = pltpu.roll(tile, n - k, 0)
    for j in range(k):
        r = jnp.where(row == n - k + j, next8[j:j + 1, :], r)
    return r


def _to_lane_blocks(val, s_ref):
    for j in range(val.shape[1] // 128):
        s_ref[j] = val[:, j * 128:(j + 1) * 128]


def _from_lane_blocks(s_ref):
    return jnp.concatenate([s_ref[j] for j in range(s_ref.shape[0])], axis=-1)


def _class_rows(s_ref, r, dil):
    n = s_ref.shape[1] // dil
    return jnp.concatenate([s_ref[j, pl.ds(r, n, stride=dil), :] for j in range(s_ref.shape[0])], axis=-1)


def _split_classes(val, s_ref, out_ref, dil):
    _to_lane_blocks(val, s_ref)
    for r in range(dil):
        out_ref[r] = _class_rows(s_ref, r, dil).astype(out_ref.dtype)


def _merge_classes(in_ref, s_ref, dil):
    n = s_ref.shape[1] // dil
    for r in range(dil):
        v = in_ref[r]
        for j in range(s_ref.shape[0]):
            s_ref[j, pl.ds(r, n, stride=dil), :] = v[:, j * 128:(j + 1) * 128]
    return _from_lane_blocks(s_ref)


def _class_spec(dil, tile=TS):
    return pl.BlockSpec((dil, tile // dil, 512), lambda i: (0, i, 0))


def _class_shape(S, dil, dtype):
    return jax.ShapeDtypeStruct((dil, S // dil, 512), dtype)


def _scan_tile(a_ref, b_ref, h_ref, carry_ref, reverse):
    n = a_ref.shape[0]
    width = a_ref.shape[1]
    groups = n // SUB
    row = lax.broadcasted_iota(jnp.int32, (SUB, width), 0)

    def step(i, carry):
        g = (groups - 1 - i) if reverse else i
        r0 = pl.multiple_of(g * SUB, SUB)
        a = a_ref[pl.ds(r0, SUB), :]
        b = b_ref[pl.ds(r0, SUB), :]
        for s in (1, 2, 4):
            if reverse:
                a_sh = pltpu.roll(a, SUB - s, 0)
                b_sh = pltpu.roll(b, SUB - s, 0)
                m = row < SUB - s
            else:
                a_sh = pltpu.roll(a, s, 0)
                b_sh = pltpu.roll(b, s, 0)
                m = row >= s
            b = jnp.where(m, a * b_sh + b, b)
            a = jnp.where(m, a * a_sh, a)
        h = b + a * carry
        h_ref[pl.ds(r0, SUB), :] = h
        edge = h[0:1, :] if reverse else h[SUB - 1:SUB, :]
        return jnp.broadcast_to(edge, (SUB, width))

    carry_ref[...] = lax.fori_loop(0, groups, step, carry_ref[...])


def _conv_fwd(xr, prev8, next8, cw, cb):
    y = cb + _shifted(xr, prev8, next8, -2) * cw[0:1, :]
    y = y + _shifted(xr, prev8, next8, -1) * cw[1:2, :]
    y = y + xr * cw[2:3, :]
    y = y + _shifted(xr, prev8, next8, 1) * cw[3:4, :]
    return y


def _lru_gates(xc, wa_ref, ba, wx_ref, bx, lam):
    xcb = xc.astype(BF16)
    r = _sigmoid(_dot(xcb, wa_ref[...]) + ba)
    i = _sigmoid(_dot(xcb, wx_ref[...]) + bx)
    cl = -LRU_C * _softplus(-lam)
    la = cl * r
    a = jnp.exp(la)
    mult = jnp.sqrt(-_expm1(2.0 * la))
    return xcb, r, i, cl, a, mult


def _inproj_fwd(x, g1, w_in):
    S = x.shape[0]

    def body(x_ref, g_ref, w_ref, xr_ref, gate_ref, *rest):
        qkv_refs, s_ref = rest[:9], rest[9]
        _, xh = _rms(x_ref[...])
        h = (xh * g_ref[...]).astype(BF16)
        proj = jnp.concatenate([_dot(h, w_ref[j]) for j in range(N_SHARD)], axis=-1)
        xr_ref[...] = proj[:, 0:512]
        gate_ref[...] = proj[:, 512:1024]
        for t in range(3):
            val = proj[:, 1024 + 512 * t:1536 + 512 * t]
            qkv_refs[3 * t][0] = val.astype(BF16)
            _to_lane_blocks(val, s_ref)
            for p, dil in enumerate(DILATIONS[1:]):
                for r in range(dil):
                    qkv_refs[3 * t + 1 + p][r] = _class_rows(s_ref, r, dil).astype(BF16)

    f = jax.ShapeDtypeStruct((S, 512), F32)
    return pl.pallas_call(
        body, grid=(S // TS,), name="inproj_fwd",
        in_specs=[_rows(D_MODEL), _whole_vmem(), _whole_vmem()],
        out_specs=[_rows(512)] * 2 + [_class_spec(d) for d in DILATIONS] * 3,
        out_shape=[f, f] + [_class_shape(S, d, BF16) for d in DILATIONS] * 3,
        scratch_shapes=[pltpu.VMEM((4, TS, 128), F32)],
        compiler_params=_params(),
    )(x, g1, w_in)


def _halo_specs(S, order, tile=TS):
    per = tile // SUB
    last = S // SUB - 1
    return [
        pl.BlockSpec((tile, 512), lambda i: (order(i), 0)),
        pl.BlockSpec((SUB, 512), lambda i: (jnp.maximum(order(i) * per - 1, 0), 0)),
        pl.BlockSpec((SUB, 512), lambda i: (jnp.minimum((order(i) + 1) * per, last), 0)),
    ]


def _rnn_fwd(xr, conv_w, conv_b, wa, ba, wx, bx, lam, reverse):
    S = xr.shape[0]
    nt = S // TS
    order = (lambda i: nt - 1 - i) if reverse else (lambda i: i)

    def body(x_ref, xp_ref, xn_ref, cw_ref, cb_ref, wa_ref, ba_ref, wx_ref, bx_ref, lam_ref, h_ref, a_s, b_s, carry):
        i = pl.program_id(0)
        t = order(i)

        @pl.when(i == 0)
        def _():
            carry[...] = jnp.zeros_like(carry)

        prev8 = jnp.where(t > 0, xp_ref[...], 0.0)
        next8 = jnp.where(t < nt - 1, xn_ref[...], 0.0)
        xc = _conv_fwd(x_ref[...], prev8, next8, cw_ref[...], cb_ref[...])
        _, _, gi, _, a, mult = _lru_gates(xc, wa_ref, ba_ref[...], wx_ref, bx_ref[...], lam_ref[...])
        a_s[...] = a
        b_s[...] = mult * (gi * xc)
        _scan_tile(a_s, b_s, h_ref, carry, reverse)

    return pl.pallas_call(
        body, grid=(nt,), name="rnn_fwd_rev" if reverse else "rnn_fwd_fwd",
        in_specs=_halo_specs(S, order) + [_whole_vmem()] * 7,
        out_specs=pl.BlockSpec((TS, 512), lambda i: (order(i), 0)),
        out_shape=jax.ShapeDtypeStruct((S, 512), F32),
        scratch_shapes=[pltpu.VMEM((TS, 512), F32), pltpu.VMEM((TS, 512), F32), pltpu.VMEM((SUB, 512), F32)],
        compiler_params=_params(),
    )(xr, xr, xr, conv_w, conv_b, wa, ba, wx, bx, lam)


def _mix_fwd(o3, l3, hf, hb, gate, x, g_rnn, g_attn, w_out):
    S = x.shape[0]

    def body(o1, o2, o3_, l1, l2, l3_, hf_ref, hb_ref, gate_ref, x_ref, gr_ref, ga_ref, w_ref,
             x1_ref, mix_ref, ya1, ya2, ya3, ls1, ls2, ls3, s_ref):
        la, lb, lc = l1[0], _merge_classes(l2, s_ref, DILATIONS[1]), _merge_classes(l3_, s_ref, DILATIONS[2])
        m = jnp.maximum(jnp.maximum(la, lb), lc)
        ea, eb, ec = jnp.exp(la - m), jnp.exp(lb - m), jnp.exp(lc - m)
        den = ea + eb + ec
        lse = m + jnp.log(den)
        ya = (ea * o1[0] + eb * _merge_classes(o2, s_ref, DILATIONS[1]) + ec * _merge_classes(o3_, s_ref, DILATIONS[2])) / den
        ya1[0] = ya
        ls1[0] = lse
        _split_classes(ya, s_ref, ya2, DILATIONS[1])
        _split_classes(ya, s_ref, ya3, DILATIONS[2])
        _split_classes(lse, s_ref, ls2, DILATIONS[1])
        _split_classes(lse, s_ref, ls3, DILATIONS[2])
        gg, _ = _gelu_parts(gate_ref[...])
        yr = (hf_ref[...] + hb_ref[...]) * gg
        _, xh_r = _rms(yr)
        _, xh_a = _rms(ya)
        mix = jnp.concatenate([xh_r * gr_ref[...], xh_a * ga_ref[...]], axis=-1).astype(BF16)
        mix_ref[...] = mix
        acc = x_ref[...]
        for j in range(N_SHARD):
            acc = acc + _dot(mix[:, j * OUT_BLK:(j + 1) * OUT_BLK], w_ref[j])
        x1_ref[...] = acc

    cls = [_class_spec(d) for d in DILATIONS]
    return pl.pallas_call(
        body, grid=(S // TS,), name="mix_fwd",
        in_specs=cls * 2 + [_rows(512)] * 3 + [_rows(D_MODEL)] + [_whole_vmem()] * 3,
        out_specs=[_rows(D_MODEL), _rows(D_MODEL)] + cls * 2,
        out_shape=[jax.ShapeDtypeStruct((S, D_MODEL), F32), jax.ShapeDtypeStruct((S, D_MODEL), BF16)]
        + [_class_shape(S, d, F32) for d in DILATIONS] * 2,
        scratch_shapes=[pltpu.VMEM((4, TS, 128), F32)],
        compiler_params=_params(),
    )(*o3, *l3, hf, hb, gate, x, g_rnn, g_attn, w_out)


def _mlp_fwd_bwd(x1, target, g_mlp, g_fin, w_up, w_down):
    S = x1.shape[0]
    tm = TS_MLP

    def body(x1_ref, t_ref, gm_ref, gf_ref, wu_ref, wd_ref,
             dx1_ref, h2_ref, a2_ref, du_ref, dx2_ref, loss_ref, dgf_ref, dgm_ref, relu_s):
        @pl.when(pl.program_id(0) == 0)
        def _():
            loss_ref[...] = jnp.zeros_like(loss_ref)
            dgf_ref[...] = jnp.zeros_like(dgf_ref)
            dgm_ref[...] = jnp.zeros_like(dgm_ref)

        x1v = x1_ref[...]
        rstd1, xh1 = _rms(x1v)
        h2 = (xh1 * gm_ref[...]).astype(BF16)
        h2_ref[...] = h2
        x2 = x1v
        for j in range(N_SHARD):
            r = jnp.maximum(_dot(h2, wu_ref[j]), 0.0)
            relu_s[j] = r
            a2 = (r * r).astype(BF16)
            a2_ref[:, j * FF_BLK:(j + 1) * FF_BLK] = a2
            x2 = x2 + _dot(a2, wd_ref[j])
        rstd2, xh2 = _rms(x2)
        err = xh2 * gf_ref[...] - t_ref[...]
        loss_ref[...] += jnp.sum(err * err, axis=0, keepdims=True)
        dy = err * (1.0 / D_MODEL)
        dx2, dgf = _rms_bwd(dy, gf_ref[...], xh2, rstd2)
        dgf_ref[...] += dgf
        dx2b = dx2.astype(BF16)
        dx2_ref[...] = dx2b
        dh2 = jnp.zeros((tm, D_MODEL), F32)
        for j in range(N_SHARD):
            du = (_dot_nt(dx2b, wd_ref[j]) * (2.0 * relu_s[j])).astype(BF16)
            du_ref[:, j * FF_BLK:(j + 1) * FF_BLK] = du
            dh2 = dh2 + _dot_nt(du, wu_ref[j])
        dx1n, dgm = _rms_bwd(dh2, gm_ref[...], xh1, rstd1)
        dgm_ref[...] += dgm
        dx1_ref[...] = dx2 + dx1n

    vec = jax.ShapeDtypeStruct((1, D_MODEL), F32)
    return pl.pallas_call(
        body, grid=(S // tm,), name="mlp_fwd_bwd",
        in_specs=[_rows(D_MODEL, tm), _rows(D_MODEL, tm)] + [_whole_vmem()] * 4,
        out_specs=[_rows(D_MODEL, tm), _rows(D_MODEL, tm), _rows(D_FF, tm), _rows(D_FF, tm), _rows(D_MODEL, tm)]
        + [_whole_vmem()] * 3,
        out_shape=[jax.ShapeDtypeStruct((S, D_MODEL), F32), jax.ShapeDtypeStruct((S, D_MODEL), BF16),
                   jax.ShapeDtypeStruct((S, D_FF), BF16), jax.ShapeDtypeStruct((S, D_FF), BF16),
                   jax.ShapeDtypeStruct((S, D_MODEL), BF16), vec, vec, vec],
        scratch_shapes=[pltpu.VMEM((N_SHARD, tm, FF_BLK), F32)],
        compiler_params=_params(),
    )(x1, target, g_mlp, g_fin, w_up, w_down)


def _mix_bwd(dx1, w_out, ya, hf, hb, gate, g_rnn, g_attn):
    S = dx1.shape[0]

    def body(dx1_ref, w_ref, ya_ref, hf_ref, hb_ref, gate_ref, gr_ref, ga_ref,
             dhs_ref, dgate_ref, dya1, dya2, dya3, dx1b_ref, dgr_ref, dga_ref, s_ref):
        @pl.when(pl.program_id(0) == 0)
        def _():
            dgr_ref[...] = jnp.zeros_like(dgr_ref)
            dga_ref[...] = jnp.zeros_like(dga_ref)

        dx1b = dx1_ref[...].astype(BF16)
        dx1b_ref[...] = dx1b
        dmix = jnp.concatenate([_dot_nt(dx1b, w_ref[j]) for j in range(N_SHARD)], axis=-1)
        gg, dgg = _gelu_parts(gate_ref[...])
        hs = hf_ref[...] + hb_ref[...]
        rstd_r, xh_r = _rms(hs * gg)
        dyr, dgr = _rms_bwd(dmix[:, 0:D_RNN], gr_ref[...], xh_r, rstd_r)
        dgr_ref[...] += dgr
        rstd_a, xh_a = _rms(ya_ref[0])
        dya, dga = _rms_bwd(dmix[:, D_RNN:], ga_ref[...], xh_a, rstd_a)
        dga_ref[...] += dga
        dya1[0] = dya
        _split_classes(dya, s_ref, dya2, DILATIONS[1])
        _split_classes(dya, s_ref, dya3, DILATIONS[2])
        dhs_ref[...] = dyr * gg
        dgate_ref[...] = dyr * hs * dgg

    f512 = jax.ShapeDtypeStruct((S, 512), F32)
    vec = jax.ShapeDtypeStruct((1, 512), F32)
    return pl.pallas_call(
        body, grid=(S // TS,), name="mix_bwd",
        in_specs=[_rows(D_MODEL), _whole_vmem(), _class_spec(1)] + [_rows(512)] * 3 + [_whole_vmem()] * 2,
        out_specs=[_rows(512)] * 2 + [_class_spec(d) for d in DILATIONS] + [_rows(D_MODEL)] + [_whole_vmem()] * 2,
        out_shape=[f512, f512] + [_class_shape(S, d, F32) for d in DILATIONS]
        + [jax.ShapeDtypeStruct((S, D_MODEL), BF16), vec, vec],
        scratch_shapes=[pltpu.VMEM((4, TS, 128), F32)],
        compiler_params=_params(),
    )(dx1, w_out, ya, hf, hb, gate, g_rnn, g_attn)


def _rnn_bwd(xr, h, dhs, conv_w, conv_b, wa, ba, wx, bx, lam, reverse):
    S = xr.shape[0]
    nt = S // TS
    order = (lambda i: i) if reverse else (lambda i: nt - 1 - i)
    per = TS // SUB
    last = S // SUB - 1
    if reverse:
        h_halo = pl.BlockSpec((SUB, 512), lambda i: (jnp.minimum((order(i) + 1) * per, last), 0))
    else:
        h_halo = pl.BlockSpec((SUB, 512), lambda i: (jnp.maximum(order(i) * per - 1, 0), 0))
    tile = pl.BlockSpec((TS, 512), lambda i: (order(i), 0))

    def body(x_ref, xp_ref, xn_ref, h_ref, hh_ref, dh_ref, cw_ref, cb_ref, wa_ref, ba_ref, wx_ref, bx_ref, lam_ref,
             dxc_ref, dwa_ref, dwx_ref, dvec_ref, a_s, g_s, carry, edge):
        i = pl.program_id(0)
        t = order(i)

        @pl.when(i == 0)
        def _():
            carry[...] = jnp.zeros_like(carry)
            edge[...] = jnp.zeros_like(edge)
            dwa_ref[...] = jnp.zeros_like(dwa_ref)
            dwx_ref[...] = jnp.zeros_like(dwx_ref)
            dvec_ref[...] = jnp.zeros_like(dvec_ref)

        prev8 = jnp.where(t > 0, xp_ref[...], 0.0)
        next8 = jnp.where(t < nt - 1, xn_ref[...], 0.0)
        xc = _conv_fwd(x_ref[...], prev8, next8, cw_ref[...], cb_ref[...])
        xcb, r, gi, cl, a, mult = _lru_gates(xc, wa_ref, ba_ref[...], wx_ref, bx_ref[...], lam_ref[...])
        hv = h_ref[...]
        if reverse:
            a_s[...] = _shifted(a, edge[...], None, -1)
            edge[...] = a[TS - SUB:TS, :]
            hh = jnp.where(t < nt - 1, hh_ref[...], 0.0)
            h_prev = _shifted(hv, None, hh, 1)
        else:
            a_s[...] = _shifted(a, None, edge[...], 1)
            edge[...] = a[0:SUB, :]
            hh = jnp.where(t > 0, hh_ref[...], 0.0)
            h_prev = _shifted(hv, hh, None, -1)
        _scan_tile(a_s, dh_ref, g_s, carry, not reverse)
        g = g_s[...]
        da = g * h_prev
        gm = g * mult
        d_i = gm * xc
        dmult = g * gi * xc
        dla = da * a - dmult * (a * a) / mult
        d_r = dla * cl
        dpre_r = d_r * r * (1.0 - r)
        dpre_i = d_i * gi * (1.0 - gi)
        dprb = dpre_r.astype(BF16)
        dpib = dpre_i.astype(BF16)
        dwa_ref[...] += _dot_tn(xcb, dprb)
        dwx_ref[...] += _dot_tn(xcb, dpib)
        dvec_ref[0:1, :] += jnp.sum(dpre_r, axis=0, keepdims=True)
        dvec_ref[1:2, :] += jnp.sum(dpre_i, axis=0, keepdims=True)
        dvec_ref[2:3, :] += jnp.sum(dla * r, axis=0, keepdims=True)
        dvec_ref[3:4, :] = dvec_ref[2:3, :] * (LRU_C * _sigmoid(-lam_ref[...]))
        dxc_ref[...] = gm * gi + _dot_nt(dprb, wa_ref[...]) + _dot_nt(dpib, wx_ref[...])

    sq = jax.ShapeDtypeStruct((D_RNN, D_RNN), F32)
    return pl.pallas_call(
        body, grid=(nt,), name="rnn_bwd_rev" if reverse else "rnn_bwd_fwd",
        in_specs=_halo_specs(S, order) + [tile, h_halo, tile] + [_whole_vmem()] * 7,
        out_specs=[tile, _whole_vmem(), _whole_vmem(), _whole_vmem()],
        out_shape=[jax.ShapeDtypeStruct((S, 512), F32), sq, sq, jax.ShapeDtypeStruct((SUB, 512), F32)],
        scratch_shapes=[pltpu.VMEM((TS, 512), F32), pltpu.VMEM((TS, 512), F32), pltpu.VMEM((SUB, 512), F32),
                        pltpu.VMEM((SUB, 512), F32)],
        compiler_params=_params(),
    )(xr, xr, xr, h, h, dhs, conv_w, conv_b, wa, ba, wx, bx, lam)


def _inproj_bwd(x, dx1, xr, dxc_f, dxc_b, dgate, dq3, dk3, dv3, g1, conv_w, w_in):
    S = x.shape[0]
    tb = TS_INPROJ_BWD
    nt = S // tb
    ident = lambda i: i

    def body(x_ref, dx1_ref, xr_ref, xrp_ref, xrn_ref, cf_ref, cfp_ref, cfn_ref, cb_ref, cbp_ref, cbn_ref, dgate_ref,
             dq1, dq2, dq3_, dk1, dk2, dk3_, dv1, dv2, dv3_, g_ref, cw_ref, w_ref,
             dx_ref, dw_ref, dg_ref, dcw_ref, s_ref):
        i = pl.program_id(0)

        @pl.when(i == 0)
        def _():
            dw_ref[...] = jnp.zeros_like(dw_ref)
            dg_ref[...] = jnp.zeros_like(dg_ref)
            dcw_ref[...] = jnp.zeros_like(dcw_ref)

        first, last = i > 0, i < nt - 1
        dxc = cf_ref[...] + cb_ref[...]
        dxc_p = jnp.where(first, cfp_ref[...] + cbp_ref[...], 0.0)
        dxc_n = jnp.where(last, cfn_ref[...] + cbn_ref[...], 0.0)
        cw = cw_ref[...]
        dxr = (_shifted(dxc, dxc_p, dxc_n, 2) * cw[0:1, :] + _shifted(dxc, dxc_p, dxc_n, 1) * cw[1:2, :]
               + dxc * cw[2:3, :] + _shifted(dxc, dxc_p, dxc_n, -1) * cw[3:4, :])
        xrv = xr_ref[...]
        xr_p = jnp.where(first, xrp_ref[...], 0.0)
        xr_n = jnp.where(last, xrn_ref[...], 0.0)
        for k, off in enumerate((-2, -1, 0, 1)):
            dcw_ref[k:k + 1, :] += jnp.sum(dxc * _shifted(xrv, xr_p, xr_n, off), axis=0, keepdims=True)
        dcw_ref[4:5, :] += jnp.sum(dxc, axis=0, keepdims=True)

        def total(a, b, c_):
            return a[0] + _merge_classes(b, s_ref, DILATIONS[1]) + _merge_classes(c_, s_ref, DILATIONS[2])

        dproj = jnp.concatenate(
            [dxr, dgate_ref[...], total(dq1, dq2, dq3_), total(dk1, dk2, dk3_), total(dv1, dv2, dv3_)],
            axis=-1).astype(BF16)
        xv = x_ref[...]
        rstd, xh = _rms(xv)
        hb = (xh * g_ref[...]).astype(BF16)
        dh = jnp.zeros((tb, D_MODEL), F32)
        for j in range(N_SHARD):
            dpj = dproj[:, j * IN_BLK:(j + 1) * IN_BLK]
            dh = dh + _dot_nt(dpj, w_ref[j])
            dw_ref[j] += _dot_tn(hb, dpj)
        dxn, dg = _rms_bwd(dh, g_ref[...], xh, rstd)
        dg_ref[...] += dg
        dx_ref[...] = dx1_ref[...] + dxn

    halo = _halo_specs(S, ident, tb)
    return pl.pallas_call(
        body, grid=(nt,), name="inproj_bwd",
        in_specs=[_rows(D_MODEL, tb), _rows(D_MODEL, tb)] + halo * 3 + [_rows(512, tb)]
        + [_class_spec(d, tb) for d in DILATIONS] * 3 + [_whole_vmem()] * 3,
        out_specs=[_rows(D_MODEL, tb), _whole_vmem(), _whole_vmem(), _whole_vmem()],
        out_shape=[jax.ShapeDtypeStruct((S, D_MODEL), F32), jax.ShapeDtypeStruct((N_SHARD, D_MODEL, IN_BLK), F32),
                   jax.ShapeDtypeStruct((1, D_MODEL), F32), jax.ShapeDtypeStruct((SUB, 512), F32)],
        scratch_shapes=[pltpu.VMEM((4, tb, 128), F32)],
        compiler_params=_params(),
    )(x, dx1, xr, xr, xr, dxc_f, dxc_f, dxc_f, dxc_b, dxc_b, dxc_b, dgate, *dq3, *dk3, *dv3, g1, conv_w, w_in)


def _dw_matmul(a, b, a_cols, b_cols, name):
    S = a.shape[0]
    tk = 1024
    a_shared = a.shape[1] == a_cols
    b_shared = b.shape[1] == b_cols

    def body(a_ref, b_ref, o_ref):
        @pl.when(pl.program_id(1) == 0)
        def _():
            o_ref[...] = jnp.zeros_like(o_ref)
        o_ref[0] += _dot_tn(a_ref[...], b_ref[...])

    return pl.pallas_call(
        body, grid=(N_SHARD, S // tk), name=name,
        in_specs=[pl.BlockSpec((tk, a_cols), (lambda j, k: (k, 0)) if a_shared else (lambda j, k: (k, j))),
                  pl.BlockSpec((tk, b_cols), (lambda j, k: (k, 0)) if b_shared else (lambda j, k: (k, j)))],
        out_specs=pl.BlockSpec((1, a_cols, b_cols), lambda j, k: (j, 0, 0)),
        out_shape=jax.ShapeDtypeStruct((N_SHARD, a_cols, b_cols), F32),
        compiler_params=_params(2),
    )(a, b)


def _t5_bucket_np(rel):
    nb = N_BUCKETS // 2
    max_exact = nb // 2
    ret = np.where(rel > 0, nb, 0)
    n = np.abs(rel)
    nf = np.maximum(n, 1).astype(np.float32)
    large = max_exact + (np.log(nf / np.float32(max_exact)) / np.float32(math.log(MAX_DISTANCE / max_exact))
                         * np.float32(nb - max_exact)).astype(np.int32)
    large = np.minimum(large, nb - 1)
    return ret + np.where(n < max_exact, n, large)


_VARIANT_OFFSETS = (0, -HALF_WIN, Q_BLK - K_WIN)


def _band_index():
    kk = np.arange(K_WIN)[None, :]
    ql = np.arange(Q_BLK)[:, None]
    rel = np.stack([kk - ql + off for off in _VARIANT_OFFSETS])
    return rel, np.abs(rel) <= HALF_WIN


def _bucket_tables(dil):
    rel, valid = _band_index()
    bucket = _t5_bucket_np(np.clip(rel, -HALF_WIN, HALF_WIN) * dil)
    return np.where(valid, bucket, -1).astype(np.int32)


def _bias_mats(rel_bias):
    tables = [_bucket_tables(d) for d in DILATIONS]
    used = [sorted(set(t[t >= 0].tolist())) for t in tables]

    def body(rb_ref, t1, t2, t3, o1, o2, o3):
        for t_ref, o_ref, buckets in ((t1, o1, used[0]), (t2, o2, used[1]), (t3, o3, used[2])):
            for var in range(3):
                bk = t_ref[var]
                for h in range(N_HEADS):
                    acc = jnp.full((Q_BLK, K_WIN), NEG_INF, F32)
                    for b in buckets:
                        acc = jnp.where(bk == b, rb_ref[b, h], acc)
                    o_ref[var, h] = acc

    shp = jax.ShapeDtypeStruct((3, N_HEADS, Q_BLK, K_WIN), F32)
    return pl.pallas_call(
        body, name="bias_tables", in_specs=[pl.BlockSpec(memory_space=pltpu.SMEM)] + [_whole_vmem()] * 3,
        out_shape=[shp] * 3, compiler_params=_params(0),
    )(rel_bias, *[jnp.asarray(t) for t in tables])


def _variant(qb, nq):
    return jnp.where(qb == 0, 0, jnp.where(qb == nq - 1, 2, 1))


def _win_start(qb, L):
    return pl.multiple_of(jnp.clip(qb * Q_BLK - HALF_WIN, 0, L - K_WIN), HALF_WIN)


def _attn_specs(L):
    nsub = min(ATTN_SUB, L // Q_BLK)
    qt = nsub * Q_BLK
    qspec = pl.BlockSpec((None, qt, 128), lambda c, s: (c // 4, s, c % 4))
    kspec = pl.BlockSpec((None, L, 128), lambda c, s: (c // 4, 0, c % 4))
    bspec = pl.BlockSpec((3, 2, Q_BLK, K_WIN), lambda c, s: (0, c % 4, 0, 0))
    return nsub, qspec, kspec, bspec


def _head_masks():
    lane = lax.broadcasted_iota(jnp.int32, (Q_BLK, 128), 1)
    return lane < HEAD_DIM


def _attn_fwd(q, k, v, bias):
    dil, L, _ = q.shape
    nq = L // Q_BLK
    nsub, qspec, kspec, bspec = _attn_specs(L)

    def body(q_ref, k_ref, v_ref, b_ref, o_ref, l_ref):
        h0 = _head_masks()
        for sub in range(nsub):
            qb = pl.program_id(1) * nsub + sub
            rows = slice(sub * Q_BLK, (sub + 1) * Q_BLK)
            st = _win_start(qb, L)
            var = _variant(qb, nq)
            kw = k_ref[pl.ds(st, K_WIN), :]
            vw = v_ref[pl.ds(st, K_WIN), :]
            qs = q_ref[rows, :] * ATTN_SCALE
            outs, lses = [], []
            for h in range(2):
                qh = jnp.where(h0 if h == 0 else ~h0, qs, jnp.zeros_like(qs))
                s = _dot_nt(qh, kw) + b_ref[var, h]
                m = jnp.max(s, axis=-1, keepdims=True)
                p = jnp.exp(s - m)
                l = jnp.sum(p, axis=-1, keepdims=True)
                outs.append(_dot(p.astype(BF16), vw) / l)
                lses.append(m + jnp.log(l))
            o_ref[rows, :] = jnp.where(h0, outs[0], outs[1])
            l_ref[rows, :] = jnp.where(h0, lses[0], lses[1])

    shp = jax.ShapeDtypeStruct((dil, L, D_ATTN), F32)
    return pl.pallas_call(
        body, grid=(dil * 4, nq // nsub), name=f"attn_fwd_d{dil}",
        in_specs=[qspec, kspec, kspec, bspec], out_specs=[qspec, qspec], out_shape=[shp, shp],
        compiler_params=_params(2),
    )(q, k, v, bias)


def _attn_bwd(q, k, v, bias, do, o, lse):
    dil, L, _ = q.shape
    nq = L // Q_BLK
    nsub, qspec, kspec, bspec = _attn_specs(L)
    ncol = dil * 4
    nstep = nq // nsub

    def body(q_ref, k_ref, v_ref, b_ref, do_ref, o_ref, l_ref, dq_ref, dk_ref, dv_ref, db_ref, db_s):
        c, step = pl.program_id(0), pl.program_id(1)

        @pl.when((c == 0) & (step == 0))
        def _():
            db_s[...] = jnp.zeros_like(db_s)

        @pl.when(step == 0)
        def _():
            dk_ref[...] = jnp.zeros_like(dk_ref)
            dv_ref[...] = jnp.zeros_like(dv_ref)

        h0 = _head_masks()
        for sub in range(nsub):
            qb = step * nsub + sub
            rows = slice(sub * Q_BLK, (sub + 1) * Q_BLK)
            st = _win_start(qb, L)
            var = _variant(qb, nq)
            kw = k_ref[pl.ds(st, K_WIN), :]
            vw = v_ref[pl.ds(st, K_WIN), :]
            qs = q_ref[rows, :] * ATTN_SCALE
            dof = do_ref[rows, :]
            dob = dof.astype(BF16)
            prod = dof * o_ref[rows, :]
            lsev = l_ref[rows, :]
            dk_acc = jnp.zeros((K_WIN, 128), F32)
            dv_acc = jnp.zeros((K_WIN, 128), F32)
            dqs = []
            for h in range(2):
                hm = h0 if h == 0 else ~h0
                qh = jnp.where(hm, qs, jnp.zeros_like(qs))
                doh = jnp.where(hm, dob, jnp.zeros_like(dob))
                s = _dot_nt(qh, kw) + b_ref[var, h]
                p = jnp.exp(s - lsev[:, h * HEAD_DIM:h * HEAD_DIM + 1])
                dp = _dot_nt(doh, vw)
                dd = jnp.sum(jnp.where(hm, prod, 0.0), axis=-1, keepdims=True)
                ds = p * (dp - dd)
                db_s[var, (c % 4) * 2 + h] += ds
                dsb = ds.astype(BF16)
                dv_acc = dv_acc + _dot_tn(p.astype(BF16), doh)
                dk_acc = dk_acc + _dot_tn(dsb, qh)
                dqs.append(_dot(dsb, kw) * ATTN_SCALE)
            dq_ref[rows, :] = jnp.where(h0, dqs[0], dqs[1])
            dk_ref[pl.ds(st, K_WIN), :] += dk_acc
            dv_ref[pl.ds(st, K_WIN), :] += dv_acc

        @pl.when((c == ncol - 1) & (step == nstep - 1))
        def _():
            db_ref[...] = db_s[...]

    shp = jax.ShapeDtypeStruct((dil, L, D_ATTN), F32)
    dbshape = (3, N_HEADS, Q_BLK, K_WIN)
    return pl.pallas_call(
        body, grid=(ncol, nstep), name=f"attn_bwd_d{dil}",
        in_specs=[qspec, kspec, kspec, bspec, qspec, qspec, qspec],
        out_specs=[qspec, kspec, kspec, _whole_vmem()],
        out_shape=[shp, shp, shp, jax.ShapeDtypeStruct(dbshape, F32)],
        scratch_shapes=[pltpu.VMEM(dbshape, F32)],
        compiler_params=_params(2),
    )(q, k, v, bias, do, o, lse)


def _bucket_onehots(dil):
    m = np.zeros((3, K_WIN, N_BUCKETS), np.float32)
    for var, off in enumerate(_VARIANT_OFFSETS):
        for rel in range(-HALF_WIN, HALF_WIN + 1):
            col = (rel - off + Q_BLK - 1) % K_WIN
            m[var, col, int(_t5_bucket_np(np.asarray(rel * dil)))] = 1.0
    return jnp.asarray(m)


def _bias_grad(dbs):
    onehots = [_bucket_onehots(d) for d in DILATIONS]
    flip = jnp.asarray(np.eye(Q_BLK, dtype=np.float32)[::-1].copy())

    def body(d1, d2, d3, m1, m2, m3, flip_ref, out_ref):
        hp = lax.Precision.HIGHEST
        acc = jnp.zeros((N_HEADS, N_BUCKETS), F32)
        for d_ref, m_ref in ((d1, m1), (d2, m2), (d3, m3)):
            for var in range(3):
                rows = []
                for h in range(N_HEADS):
                    xrev = jnp.dot(flip_ref[...], d_ref[var, h], precision=hp, preferred_element_type=F32)
                    y = pltpu.roll(xrev, 0, 1, stride=1, stride_axis=0)
                    rows.append(jnp.sum(y, axis=0, keepdims=True))
                vec = jnp.concatenate(rows, axis=0)
                acc = acc + jnp.dot(vec, m_ref[var], precision=hp, preferred_element_type=F32)
        out_ref[...] = acc

    return pl.pallas_call(
        body, name="bias_grad", out_shape=jax.ShapeDtypeStruct((N_HEADS, N_BUCKETS), F32),
        compiler_params=_params(0),
    )(*dbs, *onehots, flip)


def _block_diag(w):
    eye = jnp.eye(N_RNN_BLOCKS, dtype=w.dtype)
    return jnp.einsum("ncd,nm->ncmd", w, eye).reshape(D_RNN, D_RNN).astype(BF16)


def _diag_blocks(dense):
    d = dense.reshape(N_RNN_BLOCKS, RNN_BLOCK, N_RNN_BLOCKS, RNN_BLOCK)
    return jnp.stack([d[n, :, n, :] for n in range(N_RNN_BLOCKS)])


def _local_step(x, target, p):
    biases = _bias_mats(p["rel_bias"])
    lru = {}
    for dname in ("fwd", "bwd"):
        lru[dname] = (_block_diag(p["lru_wa_" + dname]), p["lru_ba_" + dname], _block_diag(p["lru_wx_" + dname]),
                      p["lru_bx_" + dname], p["lru_lam_" + dname])

    xr, gate, *qkv = _inproj_fwd(x, p["attn_norm_g"], p["w_in"])
    qs, ks, vs = qkv[0:3], qkv[3:6], qkv[6:9]
    hf = _rnn_fwd(xr, p["conv_w"], p["conv_b"], *lru["fwd"], reverse=False)
    hb = _rnn_fwd(xr, p["conv_w"], p["conv_b"], *lru["bwd"], reverse=True)
    outs, lses = [], []
    for q, k, v, bias in zip(qs, ks, vs, biases):
        o, l = _attn_fwd(q, k, v, bias)
        outs.append(o)
        lses.append(l)
    x1, mixb, *yl = _mix_fwd(outs, lses, hf, hb, gate, x, p["norm_rnn_g"], p["norm_attn_g"], p["w_out"])
    yas, lsts = yl[0:3], yl[3:6]
    dx1, h2b, a2b, dub, dx2b, loss_vec, dg_fin, dg_mlp = _mlp_fwd_bwd(
        x1, target, p["mlp_norm_g"], p["final_norm_g"], p["w_up"], p["w_down"])
    dhs, dgate, *dyas, dx1b, dg_rnn, dg_attn = _mix_bwd(dx1, p["w_out"], yas[0], hf, hb, gate,
                                                        p["norm_rnn_g"], p["norm_attn_g"])
    dw_up = _dw_matmul(h2b, dub, D_MODEL, FF_BLK, "dw_up")
    dw_down = _dw_matmul(a2b, dx2b, FF_BLK, D_MODEL, "dw_down")
    dw_out = _dw_matmul(mixb, dx1b, OUT_BLK, D_MODEL, "dw_out")
    dqs, dks, dvs, dbs = [], [], [], []
    for q, k, v, bias, dya, ya, lse in zip(qs, ks, vs, biases, dyas, yas, lsts):
        dq, dk, dv, db = _attn_bwd(q, k, v, bias, dya, ya, lse)
        dqs.append(dq)
        dks.append(dk)
        dvs.append(dv)
        dbs.append(db)
    d_rel_bias = _bias_grad(dbs).T
    dxc_f, dwa_f, dwx_f, dvec_f = _rnn_bwd(xr, hf, dhs, p["conv_w"], p["conv_b"], *lru["fwd"], reverse=False)
    dxc_b, dwa_b, dwx_b, dvec_b = _rnn_bwd(xr, hb, dhs, p["conv_w"], p["conv_b"], *lru["bwd"], reverse=True)
    grad_x, dw_in, dg1, dconv = _inproj_bwd(x, dx1, xr, dxc_f, dxc_b, dgate, dqs, dks, dvs,
                                            p["attn_norm_g"], p["conv_w"], p["w_in"])
    small = {
        "attn_norm_g": dg1, "conv_w": dconv[0:4], "conv_b": dconv[4:5],
        "lru_wa_fwd": _diag_blocks(dwa_f), "lru_ba_fwd": dvec_f[0:1], "lru_wx_fwd": _diag_blocks(dwx_f),
        "lru_bx_fwd": dvec_f[1:2], "lru_lam_fwd": dvec_f[3:4],
        "lru_wa_bwd": _diag_blocks(dwa_b), "lru_ba_bwd": dvec_b[0:1], "lru_wx_bwd": _diag_blocks(dwx_b),
        "lru_bx_bwd": dvec_b[1:2], "lru_lam_bwd": dvec_b[3:4],
        "rel_bias": d_rel_bias, "norm_rnn_g": dg_rnn, "norm_attn_g": dg_attn,
        "mlp_norm_g": dg_mlp, "final_norm_g": dg_fin,
    }
    big = {"w_in": dw_in, "w_out": dw_out, "w_up": dw_up, "w_down": dw_down}
    return loss_vec, grad_x, small, big


BIG = ("w_in", "w_out", "w_up", "w_down")
BIG_SHARD = {"w_in": (D_MODEL, IN_BLK), "w_out": (OUT_BLK, D_MODEL), "w_up": (D_MODEL, FF_BLK), "w_down": (FF_BLK, D_MODEL)}
N_BIG = len(BIG)
N_CHIP_PEERS = 3
ANY = pl.BlockSpec(memory_space=pl.ANY)


def _place():
    x, y, c = lax.axis_index("x"), lax.axis_index("y"), lax.axis_index("c")
    chips = [(1 - x, y), (x, 1 - y), (1 - x, 1 - y)]
    return x, y, c, chips


def _remote(src, dst, send_sem, recv_sem, dev):
    return pltpu.make_async_remote_copy(src_ref=src, dst_ref=dst, send_sem=send_sem, recv_sem=recv_sem,
                                        device_id=dev, device_id_type=MESH)


def _staged_start(srcs, bufs, sems):
    legs = [pltpu.make_async_copy(s, b, sems.at[i]) for i, (s, b) in enumerate(zip(srcs, bufs))]
    for cp in legs:
        cp.start()
    return legs


def _staged_finish(legs, bufs, dsts, sems):
    out = []
    for i, (leg, b, d) in enumerate(zip(legs, bufs, dsts)):
        leg.wait()
        cp = pltpu.make_async_copy(b, d, sems.at[i])
        cp.start()
        out.append(cp)
    return out


def _allgather_weights(shards, conv_w_shard):
    def body(*refs):
        ins, outs = refs[:N_BIG + 1], refs[N_BIG + 1:2 * N_BIG + 2]
        ici_send, ici_recv, d2d_send, d2d_recv, cw_send, cw_recv, loc_in, loc_out = refs[2 * N_BIG + 2:2 * N_BIG + 10]
        bufs = refs[2 * N_BIG + 10:]
        x, y, c, chips = _place()
        me = 2 * x + y
        sibling = (x, y, 1 - c)
        legs = _staged_start(ins, bufs, loc_in)
        pending = []
        halves = [BIG_SHARD[n][0] // 2 for n in BIG]

        def piece(w, chip, core_half):
            return outs[w].at[chip, pl.ds(core_half * halves[w], halves[w])]

        for w in range(N_BIG):
            for k, (px, py) in enumerate(chips):
                cp = _remote(ins[w].at[pl.ds(c * halves[w], halves[w])], piece(w, me, c),
                             ici_send.at[w, k], ici_recv.at[w, k], (px, py, c))
                cp.start()
                pending.append(cp)
        for k, (px, py) in enumerate(chips):
            cp = _remote(ins[N_BIG], outs[N_BIG].at[me], cw_send.at[k], cw_recv.at[k], (px, py, c))
            cp.start()
            pending.append(cp)
        local = _staged_finish(legs, bufs, [o.at[me] for o in outs], loc_out)
        for w in range(N_BIG):
            for k, (px, py) in enumerate(chips):
                landed = piece(w, 2 * px + py, c)
                _remote(landed, landed, ici_send.at[w, k], ici_recv.at[w, k], (px, py, c)).wait_recv()
                cp = _remote(landed, landed, d2d_send.at[w, k], d2d_recv.at[w, k], sibling)
                cp.start()
                pending.append(cp)
        for w in range(N_BIG):
            for k, (px, py) in enumerate(chips):
                other = piece(w, 2 * px + py, 1 - c)
                _remote(other, other, d2d_send.at[w, k], d2d_recv.at[w, k], sibling).wait_recv()
        for k, (px, py) in enumerate(chips):
            got = outs[N_BIG].at[2 * px + py]
            _remote(got, got, cw_send.at[k], cw_recv.at[k], (px, py, c)).wait_recv()
        for cp in local:
            cp.wait()
        for cp in pending:
            cp.wait_send()

    out_shape = [jax.ShapeDtypeStruct((N_SHARD,) + BIG_SHARD[n], BF16) for n in BIG]
    out_shape.append(jax.ShapeDtypeStruct((N_SHARD,) + conv_w_shard.shape, F32))
    pair = [pltpu.SemaphoreType.DMA((N_BIG, N_CHIP_PEERS))] * 2
    stage = [pltpu.VMEM(BIG_SHARD[n], BF16) for n in BIG] + [pltpu.VMEM(conv_w_shard.shape, F32)]
    return pl.pallas_call(
        body, name="allgather_weights", in_specs=[ANY] * (N_BIG + 1), out_specs=[ANY] * (N_BIG + 1), out_shape=out_shape,
        scratch_shapes=pair + pair + [pltpu.SemaphoreType.DMA((N_CHIP_PEERS,))] * 2
        + [pltpu.SemaphoreType.DMA((N_BIG + 1,))] * 2 + stage,
        compiler_params=pltpu.CompilerParams(has_side_effects=True, vmem_limit_bytes=VMEM_LIMIT),
    )(*shards, conv_w_shard)


def _pair_exchange(grads):
    def body(*refs):
        ins, outs = refs[:N_BIG], refs[N_BIG:2 * N_BIG]
        send, recv = refs[2 * N_BIG:]
        x, y, c, _ = _place()
        cps = []
        for w, n in enumerate(BIG):
            h = BIG_SHARD[n][0] // 2
            cp = _remote(ins[w].at[:, pl.ds((1 - c) * h, h), :], outs[w], send.at[w], recv.at[w], (x, y, 1 - c))
            cp.start()
            cps.append(cp)
        for cp in cps:
            cp.wait()

    out_shape = [jax.ShapeDtypeStruct((N_SHARD, BIG_SHARD[n][0] // 2, BIG_SHARD[n][1]), F32) for n in BIG]
    return pl.pallas_call(
        body, name="grad_pair_exchange", in_specs=[ANY] * N_BIG, out_specs=[ANY] * N_BIG, out_shape=out_shape,
        scratch_shapes=[pltpu.SemaphoreType.DMA((N_BIG,))] * 2,
        compiler_params=pltpu.CompilerParams(has_side_effects=True),
    )(*grads)


def _pair_add(core, grad, other, name):
    _, r, cols = grad.shape
    h = r // 2
    th = min(h, 256)
    per = h // th

    def body(c_ref, g_ref, o_ref, out_ref):
        out_ref[...] = (g_ref[...] + o_ref[...]).astype(BF16)

    return pl.pallas_call(
        body, name=name,
        grid_spec=pltpu.PrefetchScalarGridSpec(
            num_scalar_prefetch=1, grid=(N_SHARD, per),
            in_specs=[pl.BlockSpec((1, th, cols), lambda j, i, c_ref: (j, c_ref[0] * per + i, 0)),
                      pl.BlockSpec((1, th, cols), lambda j, i, c_ref: (j, i, 0))],
            out_specs=pl.BlockSpec((1, th, cols), lambda j, i, c_ref: (j, i, 0))),
        out_shape=jax.ShapeDtypeStruct((N_SHARD, h, cols), BF16),
        compiler_params=_params(2),
    )(core, grad, other)


def _chip_exchange(parts):
    def body(*refs):
        ins, outs = refs[:N_BIG], refs[N_BIG:2 * N_BIG]
        send, recv, loc_in, loc_out = refs[2 * N_BIG:2 * N_BIG + 4]
        bufs = refs[2 * N_BIG + 4:]
        x, y, c, chips = _place()
        me = 2 * x + y
        legs = _staged_start([r.at[me] for r in ins], bufs, loc_in)
        sends = []
        for w in range(N_BIG):
            for k, (px, py) in enumerate(chips):
                cp = _remote(ins[w].at[2 * px + py], outs[w].at[me], send.at[w, k], recv.at[w, k], (px, py, c))
                cp.start()
                sends.append(cp)
        local = _staged_finish(legs, bufs, [o.at[me] for o in outs], loc_out)
        for w in range(N_BIG):
            for k, (px, py) in enumerate(chips):
                got = outs[w].at[2 * px + py]
                _remote(got, got, send.at[w, k], recv.at[w, k], (px, py, c)).wait_recv()
        for cp in sends:
            cp.wait_send()
        for cp in local:
            cp.wait()

    out_shape = [jax.ShapeDtypeStruct(p.shape, BF16) for p in parts]
    return pl.pallas_call(
        body, name="grad_chip_exchange", in_specs=[ANY] * N_BIG, out_specs=[ANY] * N_BIG, out_shape=out_shape,
        scratch_shapes=[pltpu.SemaphoreType.DMA((N_BIG, N_CHIP_PEERS))] * 2 + [pltpu.SemaphoreType.DMA((N_BIG,))] * 2
        + [pltpu.VMEM(p.shape[1:], BF16) for p in parts],
        compiler_params=pltpu.CompilerParams(has_side_effects=True, vmem_limit_bytes=VMEM_LIMIT),
    )(*parts)


def _chip_sum(parts, name):
    _, h, cols = parts.shape
    th = min(h, 256)

    def body(p_ref, out_ref):
        acc = p_ref[0].astype(F32)
        for j in range(1, N_SHARD):
            acc = acc + p_ref[j].astype(F32)
        out_ref[...] = acc

    return pl.pallas_call(
        body, name=name, grid=(h // th,),
        in_specs=[pl.BlockSpec((N_SHARD, th, cols), lambda i: (0, i, 0))],
        out_specs=pl.BlockSpec((th, cols), lambda i: (i, 0)),
        out_shape=jax.ShapeDtypeStruct((h, cols), F32),
        compiler_params=_params(),
    )(parts)


def _pair_share(halves):
    def body(*refs):
        ins, outs = refs[:N_BIG], refs[N_BIG:2 * N_BIG]
        send, recv, loc_in, loc_out = refs[2 * N_BIG:2 * N_BIG + 4]
        bufs = refs[2 * N_BIG + 4:]
        x, y, c, _ = _place()
        legs = _staged_start(ins, bufs, loc_in)
        sends, mine = [], []
        for w, n in enumerate(BIG):
            h = BIG_SHARD[n][0] // 2
            mine.append(outs[w].at[pl.ds(c * h, h)])
            cp = _remote(ins[w], mine[w], send.at[w], recv.at[w], (x, y, 1 - c))
            cp.start()
            sends.append(cp)
        local = _staged_finish(legs, bufs, mine, loc_out)
        for w, n in enumerate(BIG):
            h = BIG_SHARD[n][0] // 2
            theirs = outs[w].at[pl.ds((1 - c) * h, h)]
            _remote(theirs, theirs, send.at[w], recv.at[w], (x, y, 1 - c)).wait_recv()
        for cp in sends:
            cp.wait_send()
        for cp in local:
            cp.wait()

    out_shape = [jax.ShapeDtypeStruct(BIG_SHARD[n], F32) for n in BIG]
    return pl.pallas_call(
        body, name="grad_pair_share", in_specs=[ANY] * N_BIG, out_specs=[ANY] * N_BIG, out_shape=out_shape,
        scratch_shapes=[pltpu.SemaphoreType.DMA((N_BIG,))] * 4
        + [pltpu.VMEM((BIG_SHARD[n][0] // 2, BIG_SHARD[n][1]), F32) for n in BIG],
        compiler_params=pltpu.CompilerParams(has_side_effects=True, vmem_limit_bytes=VMEM_LIMIT),
    )(*halves)


N_DEV = 8


def _allreduce_small(vec):
    rows = vec.shape[0]

    def body(v_ref, sum_ref, gat_ref, send, recv, loc_sem):
        x, y, c, _ = _place()
        me = 4 * x + 2 * y + c
        lc = pltpu.make_async_copy(v_ref, gat_ref.at[me], loc_sem)
        lc.start()
        peers = []
        for fx in (0, 1):
            for fy in (0, 1):
                for fc in (0, 1):
                    if fx or fy or fc:
                        peers.append(((1 - x) if fx else x, (1 - y) if fy else y, (1 - c) if fc else c))
        sends = []
        for k, dev in enumerate(peers):
            cp = _remote(v_ref, gat_ref.at[me], send.at[k], recv.at[k], dev)
            cp.start()
            sends.append(cp)
        for k, (px, py, pc) in enumerate(peers):
            got = gat_ref.at[4 * px + 2 * py + pc]
            _remote(got, got, send.at[k], recv.at[k], (px, py, pc)).wait_recv()
        for cp in sends:
            cp.wait_send()
        lc.wait()
        acc = gat_ref[0]
        for j in range(1, N_DEV):
            acc = acc + gat_ref[j]
        sum_ref[...] = acc

    total, _ = pl.pallas_call(
        body, name="allreduce_small",
        in_specs=[_whole_vmem()], out_specs=[_whole_vmem(), _whole_vmem()],
        out_shape=[jax.ShapeDtypeStruct((rows, 128), F32), jax.ShapeDtypeStruct((N_DEV, rows, 128), F32)],
        scratch_shapes=[pltpu.SemaphoreType.DMA((N_DEV - 1,))] * 2 + [pltpu.SemaphoreType.DMA(())],
        compiler_params=pltpu.CompilerParams(has_side_effects=True, vmem_limit_bytes=VMEM_LIMIT),
    )(vec)
    return total


def _adamw(w, g, m, v, name):
    rows, cols = w.shape
    tr = 256 if rows % 256 == 0 else rows
    c1 = 1.0 - ADAM_B1 ** ADAM_STEP
    c2 = 1.0 - ADAM_B2 ** ADAM_STEP

    def body(w_ref, g_ref, m_ref, v_ref, d_ref, m2_ref, v2_ref):
        gv = g_ref[...]
        m2 = ADAM_B1 * m_ref[...] + (1.0 - ADAM_B1) * gv
        v2 = ADAM_B2 * v_ref[...] + (1.0 - ADAM_B2) * (gv * gv)
        m2_ref[...] = m2
        v2_ref[...] = v2
        d_ref[...] = -ADAM_LR * ((m2 / c1) / (jnp.sqrt(v2 / c2) + ADAM_EPS) + ADAM_WD * w_ref[...])

    spec = pl.BlockSpec((tr, cols), lambda i: (i, 0))
    shp = jax.ShapeDtypeStruct((rows, cols), F32)
    return pl.pallas_call(
        body, name=name, grid=(rows // tr,), in_specs=[spec] * 4, out_specs=[spec] * 3, out_shape=[shp] * 3,
        compiler_params=_params(),
    )(w, g, m, v)


SMALL = (
    ("attn_norm_g", (1, 1024)), ("conv_w", (1, 4, 512)), ("conv_b", (1, 512)),
    ("lru_wa_fwd", (1, 8, 64, 64)), ("lru_ba_fwd", (1, 512)), ("lru_wx_fwd", (1, 8, 64, 64)), ("lru_bx_fwd", (1, 512)),
    ("lru_lam_fwd", (1, 512)),
    ("lru_wa_bwd", (1, 8, 64, 64)), ("lru_ba_bwd", (1, 512)), ("lru_wx_bwd", (1, 8, 64, 64)), ("lru_bx_bwd", (1, 512)),
    ("lru_lam_bwd", (1, 512)),
    ("rel_bias", (32, 8)), ("norm_rnn_g", (1, 512)), ("norm_attn_g", (1, 512)), ("mlp_norm_g", (1, 1024)),
    ("final_norm_g", (1024,)),
)
PACK_ROW = 8 * 128


def _pack(parts):
    flat = jnp.concatenate([p.reshape(-1) for p in parts])
    pad = (-flat.shape[0]) % PACK_ROW
    return jnp.pad(flat, (0, pad)).reshape(-1, 128)


def _unpack(packed, shapes):
    flat = packed.reshape(-1)
    out, off = [], 0
    for shp in shapes:
        n = int(np.prod(shp))
        out.append(flat[off:off + n].reshape(shp))
        off += n
    return out


WEIGHT_ORDER = ("attn_norm_g", "w_in", "conv_w", "conv_b", "lru_wa_fwd", "lru_ba_fwd", "lru_wx_fwd", "lru_bx_fwd",
                "lru_lam_fwd", "lru_wa_bwd", "lru_ba_bwd", "lru_wx_bwd", "lru_bx_bwd", "lru_lam_bwd", "rel_bias",
                "norm_rnn_g", "norm_attn_g", "w_out", "mlp_norm_g", "w_up", "w_down", "final_norm_g")


def kernel(x, attn_norm_g, w_in, conv_w, conv_b, lru_wa_fwd, lru_ba_fwd, lru_wx_fwd, lru_bx_fwd, lru_lam_fwd, lru_wa_bwd, lru_ba_bwd, lru_wx_bwd, lru_bx_bwd, lru_lam_bwd, rel_bias, norm_rnn_g, norm_attn_g, w_out, mlp_norm_g, w_up, w_down, final_norm_g, loss_target, m_attn_norm_g, m_w_in, m_conv_w, m_conv_b, m_lru_wa_fwd, m_lru_ba_fwd, m_lru_wx_fwd, m_lru_bx_fwd, m_lru_lam_fwd, m_lru_wa_bwd, m_lru_ba_bwd, m_lru_wx_bwd, m_lru_bx_bwd, m_lru_lam_bwd, m_rel_bias, m_norm_rnn_g, m_norm_attn_g, m_w_out, m_mlp_norm_g, m_w_up, m_w_down, m_final_norm_g, v_attn_norm_g, v_w_in, v_conv_w, v_conv_b, v_lru_wa_fwd, v_lru_ba_fwd, v_lru_wx_fwd, v_lru_bx_fwd, v_lru_lam_fwd, v_lru_wa_bwd, v_lru_ba_bwd, v_lru_wx_bwd, v_lru_bx_bwd, v_lru_lam_bwd, v_rel_bias, v_norm_rnn_g, v_norm_attn_g, v_w_out, v_mlp_norm_g, v_w_up, v_w_down, v_final_norm_g):
    given = dict(locals())
    w = {n: given[n] for n in WEIGHT_ORDER}
    m = {n: given["m_" + n] for n in WEIGHT_ORDER}
    v = {n: given["v_" + n] for n in WEIGHT_ORDER}

    chip = lax.axis_index("x") * 2 + lax.axis_index("y")
    core = lax.axis_index("c")

    shards = [w[n][0].astype(BF16) for n in BIG]
    *gathered, conv_all = _allgather_weights(shards, w["conv_w"][0])
    p = {n: (t[0] if t.ndim >= 3 else t) for n, t in w.items() if n not in BIG}
    p["final_norm_g"] = w["final_norm_g"].reshape(1, D_MODEL)
    p["conv_w"] = jnp.transpose(conv_all, (1, 0, 2)).reshape(4, D_RNN)
    p.update(dict(zip(BIG, gathered)))

    loss_vec, grad_x, small, big = _local_step(x[0], loss_target[0], p)

    grads = [big[n] for n in BIG]
    others = _pair_exchange(grads)
    core_arr = core.reshape(1).astype(jnp.int32)
    parts = [_pair_add(core_arr, g, o, "grad_pair_add_" + n) for n, g, o in zip(BIG, grads, others)]
    landed = _chip_exchange(parts)
    halves = [_chip_sum(t, "grad_chip_sum_" + n) for n, t in zip(BIG, landed)]
    reduced = dict(zip(BIG, _pair_share(halves)))

    loss_local = (0.5 / D_MODEL) * jnp.sum(loss_vec)
    packed = _pack([small[n].reshape(shp) for n, shp in SMALL] + [loss_local.reshape(1)])
    total = _allreduce_small(packed)
    *small_g, loss = _unpack(total, [shp for _, shp in SMALL] + [(1,)])
    g = dict(zip([n for n, _ in SMALL], small_g))
    g["conv_w"] = lax.dynamic_slice_in_dim(g["conv_w"], chip * (D_RNN // N_SHARD), D_RNN // N_SHARD, axis=2)
    for n in BIG:
        g[n] = reduced[n][None]

    delta, new_m, new_v = {}, {}, {}
    for n in BIG:
        d2, m2, v2 = _adamw(w[n][0], reduced[n], m[n][0], v[n][0], "adamw_" + n)
        delta[n], new_m[n], new_v[n] = d2[None], m2[None], v2[None]
    names = [n for n, _ in SMALL]
    shapes = [w[n].shape for n in names]
    d2, m2, v2 = _adamw(_pack([w[n] for n in names]), _pack([g[n] for n in names]), _pack([m[n] for n in names]),
                        _pack([v[n] for n in names]), "adamw_small")
    for dst, src in ((delta, d2), (new_m, m2), (new_v, v2)):
        dst.update(dict(zip(names, _unpack(src, shapes))))

    return (loss.reshape(()), grad_x[None], *[g[n] for n in WEIGHT_ORDER], *[delta[n] for n in WEIGHT_ORDER],
            *[new_m[n] for n in WEIGHT_ORDER], *[new_v[n] for n in WEIGHT_ORDER])
```

```python
import functools
import math

import numpy as np
import jax
import jax.numpy as jnp
from jax import lax
from jax.experimental import pallas as pl
from jax.experimental.pallas import tpu as pltpu

F32 = jnp.float32
BF16 = jnp.bfloat16

D_MODEL = 1024
D_RNN = 512
D_ATTN = 512
N_HEADS = 8
HEAD_DIM = 64
N_RNN_BLOCKS = 8
RNN_BLOCK = 64
D_IN = 2 * D_RNN + 3 * D_ATTN
D_FF = 4 * D_MODEL
N_SHARD = 4
IN_BLK = D_IN // N_SHARD
OUT_BLK = D_MODEL // N_SHARD
FF_BLK = D_FF // N_SHARD
EPS = 1e-6
NEG_INF = -1e30
LRU_C = 8.0
DILATIONS = (1, 4, 16)
HALF_WIN = 64
Q_BLK = 128
K_WIN = 256
N_BUCKETS = 32
MAX_DISTANCE = 1024
ATTN_SCALE = HEAD_DIM ** -0.5

ADAM_LR = 0.001
ADAM_B1 = 0.9
ADAM_B2 = 0.999
ADAM_EPS = 1e-08
ADAM_WD = 0.01
ADAM_STEP = 10

TS = 512
TS_MLP = 256
TS_INPROJ_BWD = 256
ATTN_SUB = 4
SUB = 8
VMEM_LIMIT = 56 * 1024 * 1024
GELU_C0 = math.sqrt(2.0 / math.pi)
GELU_C1 = 0.044715

MESH = pl.DeviceIdType.MESH


def _params(n_grid=1):
    return pltpu.CompilerParams(vmem_limit_bytes=VMEM_LIMIT, dimension_semantics=("arbitrary",) * n_grid)


def _whole_vmem():
    return pl.BlockSpec(memory_space=pltpu.VMEM)


def _rows(width, tile=TS):
    return pl.BlockSpec((tile, width), lambda i: (i, 0))


def _sigmoid(z):
    return 1.0 / (1.0 + jnp.exp(-z))


def _log1p(u):
    w = 1.0 + u
    return jnp.where(w == 1.0, u, jnp.log(w) * (u / (w - 1.0)))


def _softplus(z):
    return jnp.maximum(z, 0.0) + _log1p(jnp.exp(-jnp.abs(z)))


def _expm1(x):
    small = jnp.abs(x) < 0.3
    xs = jnp.where(small, x, 0.0)
    poly = xs * (1.0 + xs * (0.5 + xs * (1.0 / 6 + xs * (1.0 / 24 + xs * (1.0 / 120 + xs * (1.0 / 720 + xs * (1.0 / 5040)))))))
    return jnp.where(small, poly, jnp.exp(x) - 1.0)


def _gelu_parts(g):
    inner = GELU_C0 * (g + GELU_C1 * g * g * g)
    t = jnp.tanh(inner)
    val = 0.5 * g * (1.0 + t)
    dinner = GELU_C0 * (1.0 + 3.0 * GELU_C1 * g * g)
    grad = 0.5 * (1.0 + t) + 0.5 * g * (1.0 - t * t) * dinner
    return val, grad


def _rms(x):
    rstd = lax.rsqrt(jnp.mean(x * x, axis=-1, keepdims=True) + EPS)
    return rstd, x * rstd


def _rms_bwd(dy, g, xhat, rstd):
    dxh = dy * g
    dx = rstd * (dxh - xhat * jnp.mean(dxh * xhat, axis=-1, keepdims=True))
    dg = jnp.sum(dy * xhat, axis=0, keepdims=True)
    return dx, dg


def _dot(a, b):
    return jnp.dot(a, b, preferred_element_type=F32)


def _dot_nt(a, b):
    return lax.dot_general(a, b, (((1,), (1,)), ((), ())), preferred_element_type=F32)


def _dot_tn(a, b):
    return lax.dot_general(a, b, (((0,), (0,)), ((), ())), preferred_element_type=F32)


def _shifted(tile, prev8, next8, k):
    n = tile.shape[0]
    row = lax.broadcasted_iota(jnp.int32, tile.shape, 0)
    if k == 0:
        return tile
    if k < 0:
        r = pltpu.roll(tile, -k, 0)
        for j in range(-k):
            r = jnp.where(row == j, prev8[SUB + j + k:SUB + j + k + 1, :], r)
        return r
    r = pltpu.roll(tile, n - k, 0)
    for j in range(k):
        r = jnp.where(row == n - k + j, next8[j:j + 1, :], r)
    return r


def _to_lane_blocks(val, s_ref):
    for j in range(val.shape[1] // 128):
        s_ref[j] = val[:, j * 128:(j + 1) * 128]


def _from_lane_blocks(s_ref):
    return jnp.concatenate([s_ref[j] for j in range(s_ref.shape[0])], axis=-1)


def _class_rows(s_ref, r, dil):
    n = s_ref.shape[1] // dil
    return jnp.concatenate([s_ref[j, pl.ds(r, n, stride=dil), :] for j in range(s_ref.shape[0])], axis=-1)


def _split_classes(val, s_ref, out_ref, dil):
    _to_lane_blocks(val, s_ref)
    for r in range(dil):
        out_ref[r] = _class_rows(s_ref, r, dil).astype(out_ref.dtype)


def _merge_classes(in_ref, s_ref, dil):
    n = s_ref.shape[1] // dil
    for r in range(dil):
        v = in_ref[r]
        for j in range(s_ref.shape[0]):
            s_ref[j, pl.ds(r, n, stride=dil), :] = v[:, j * 128:(j + 1) * 128]
    return _from_lane_blocks(s_ref)


def _class_spec(dil, tile=TS):
    return pl.BlockSpec((dil, tile // dil, 512), lambda i: (0, i, 0))


def _class_shape(S, dil, dtype):
    return jax.ShapeDtypeStruct((dil, S // dil, 512), dtype)


def _scan_tile(a_ref, b_ref, h_ref, carry_ref, reverse):
    n = a_ref.shape[0]
    width = a_ref.shape[1]
    groups = n // SUB
    row = lax.broadcasted_iota(jnp.int32, (SUB, width), 0)

    def step(i, carry):
        g = (groups - 1 - i) if reverse else i
        r0 = pl.multiple_of(g * SUB, SUB)
        a = a_ref[pl.ds(r0, SUB), :]
        b = b_ref[pl.ds(r0, SUB), :]
        for s in (1, 2, 4):
            if reverse:
                a_sh = pltpu.roll(a, SUB - s, 0)
                b_sh = pltpu.roll(b, SUB - s, 0)
                m = row < SUB - s
            else:
                a_sh = pltpu.roll(a, s, 0)
                b_sh = pltpu.roll(b, s, 0)
                m = row >= s
            b = jnp.where(m, a * b_sh + b, b)
            a = jnp.where(m, a * a_sh, a)
        h = b + a * carry
        h_ref[pl.ds(r0, SUB), :] = h
        edge = h[0:1, :] if reverse else h[SUB - 1:SUB, :]
        return jnp.broadcast_to(edge, (SUB, width))

    carry_ref[...] = lax.fori_loop(0, groups, step, carry_ref[...])


def _conv_fwd(xr, prev8, next8, cw, cb):
    y = cb + _shifted(xr, prev8, next8, -2) * cw[0:1, :]
    y = y + _shifted(xr, prev8, next8, -1) * cw[1:2, :]
    y = y + xr * cw[2:3, :]
    y = y + _shifted(xr, prev8, next8, 1) * cw[3:4, :]
    return y


def _lru_gates(xc, wa_ref, ba, wx_ref, bx, lam):
    xcb = xc.astype(BF16)
    r = _sigmoid(_dot(xcb, wa_ref[...]) + ba)
    i = _sigmoid(_dot(xcb, wx_ref[...]) + bx)
    cl = -LRU_C * _softplus(-lam)
    la = cl * r
    a = jnp.exp(la)
    mult = jnp.sqrt(-_expm1(2.0 * la))
    return xcb, r, i, cl, a, mult


def _inproj_fwd(x, g1, w_in, rider=None):
    S = x.shape[0]

    def body(x_ref, g_ref, w_ref, xr_ref, gate_ref, *rest):
        qkv_refs, s_ref = rest[:9], rest[9]
        _, xh = _rms(x_ref[...])
        h = (xh * g_ref[...]).astype(BF16)
        proj = jnp.concatenate([_dot(h, w_ref[j]) for j in range(N_SHARD)], axis=-1)
        xr_ref[...] = proj[:, 0:512]
        gate_ref[...] = proj[:, 512:1024]
        for t in range(3):
            val = proj[:, 1024 + 512 * t:1536 + 512 * t]
            qkv_refs[3 * t][0] = val.astype(BF16)
            _to_lane_blocks(val, s_ref)
            for p, dil in enumerate(DILATIONS[1:]):
                for r in range(dil):
                    qkv_refs[3 * t + 1 + p][r] = _class_rows(s_ref, r, dil).astype(BF16)

    f = jax.ShapeDtypeStruct((S, 512), F32)
    return _call(
        body, "inproj_fwd", (S // TS,),
        [_rows(D_MODEL), _whole_vmem(), _whole_vmem()],
        [_rows(512)] * 2 + [_class_spec(d) for d in DILATIONS] * 3,
        [f, f] + [_class_shape(S, d, BF16) for d in DILATIONS] * 3,
        [pltpu.VMEM((4, TS, 128), F32)], (x, g1, w_in), rider)


def _halo_specs(S, order, tile=TS):
    per = tile // SUB
    last = S // SUB - 1
    return [
        pl.BlockSpec((tile, 512), lambda i: (order(i), 0)),
        pl.BlockSpec((SUB, 512), lambda i: (jnp.maximum(order(i) * per - 1, 0), 0)),
        pl.BlockSpec((SUB, 512), lambda i: (jnp.minimum((order(i) + 1) * per, last), 0)),
    ]


def _rnn_fwd(xr, conv_w, conv_b, wa, ba, wx, bx, lam, reverse, rider=None):
    S = xr.shape[0]
    nt = S // TS
    order = (lambda i: nt - 1 - i) if reverse else (lambda i: i)

    def body(x_ref, xp_ref, xn_ref, cw_ref, cb_ref, wa_ref, ba_ref, wx_ref, bx_ref, lam_ref, h_ref, a_s, b_s, carry):
        i = pl.program_id(0)
        t = order(i)

        @pl.when(i == 0)
        def _():
            carry[...] = jnp.zeros_like(carry)

        prev8 = jnp.where(t > 0, xp_ref[...], 0.0)
        next8 = jnp.where(t < nt - 1, xn_ref[...], 0.0)
        xc = _conv_fwd(x_ref[...], prev8, next8, cw_ref[...], cb_ref[...])
        _, _, gi, _, a, mult = _lru_gates(xc, wa_ref, ba_ref[...], wx_ref, bx_ref[...], lam_ref[...])
        a_s[...] = a
        b_s[...] = mult * (gi * xc)
        _scan_tile(a_s, b_s, h_ref, carry, reverse)

    (h,), carried = _call(
        body, "rnn_fwd_rev" if reverse else "rnn_fwd_fwd", (nt,),
        _halo_specs(S, order) + [_whole_vmem()] * 7,
        [pl.BlockSpec((TS, 512), lambda i: (order(i), 0))],
        [jax.ShapeDtypeStruct((S, 512), F32)],
        [pltpu.VMEM((TS, 512), F32), pltpu.VMEM((TS, 512), F32), pltpu.VMEM((SUB, 512), F32)],
        (xr, xr, xr, conv_w, conv_b, wa, ba, wx, bx, lam), rider)
    return h, carried


def _mix_fwd(o3, l3, hf, hb, gate, x, g_rnn, g_attn, w_out):
    S = x.shape[0]

    def body(o1, o2, o3_, l1, l2, l3_, hf_ref, hb_ref, gate_ref, x_ref, gr_ref, ga_ref, w_ref,
             x1_ref, mix_ref, ya1, ya2, ya3, ls1, ls2, ls3, s_ref):
        la, lb, lc = l1[0], _merge_classes(l2, s_ref, DILATIONS[1]), _merge_classes(l3_, s_ref, DILATIONS[2])
        m = jnp.maximum(jnp.maximum(la, lb), lc)
        ea, eb, ec = jnp.exp(la - m), jnp.exp(lb - m), jnp.exp(lc - m)
        den = ea + eb + ec
        lse = m + jnp.log(den)
        ya = (ea * o1[0] + eb * _merge_classes(o2, s_ref, DILATIONS[1]) + ec * _merge_classes(o3_, s_ref, DILATIONS[2])) / den
        ya1[0] = ya
        ls1[0] = lse
        _split_classes(ya, s_ref, ya2, DILATIONS[1])
        _split_classes(ya, s_ref, ya3, DILATIONS[2])
        _split_classes(lse, s_ref, ls2, DILATIONS[1])
        _split_classes(lse, s_ref, ls3, DILATIONS[2])
        gg, _ = _gelu_parts(gate_ref[...])
        yr = (hf_ref[...] + hb_ref[...]) * gg
        _, xh_r = _rms(yr)
        _, xh_a = _rms(ya)
        mix = jnp.concatenate([xh_r * gr_ref[...], xh_a * ga_ref[...]], axis=-1).astype(BF16)
        mix_ref[...] = mix
        acc = x_ref[...]
        for j in range(N_SHARD):
            acc = acc + _dot(mix[:, j * OUT_BLK:(j + 1) * OUT_BLK], w_ref[j])
        x1_ref[...] = acc

    cls = [_class_spec(d) for d in DILATIONS]
    return pl.pallas_call(
        body, grid=(S // TS,), name="mix_fwd",
        in_specs=cls * 2 + [_rows(512)] * 3 + [_rows(D_MODEL)] + [_whole_vmem()] * 3,
        out_specs=[_rows(D_MODEL), _rows(D_MODEL)] + cls * 2,
        out_shape=[jax.ShapeDtypeStruct((S, D_MODEL), F32), jax.ShapeDtypeStruct((S, D_MODEL), BF16)]
        + [_class_shape(S, d, F32) for d in DILATIONS] * 2,
        scratch_shapes=[pltpu.VMEM((4, TS, 128), F32)],
        compiler_params=_params(),
    )(*o3, *l3, hf, hb, gate, x, g_rnn, g_attn, w_out)


def _mlp_fwd_bwd(x1, target, g_mlp, g_fin, w_up, w_down):
    S = x1.shape[0]
    tm = TS_MLP

    def body(x1_ref, t_ref, gm_ref, gf_ref, wu_ref, wd_ref,
             dx1_ref, h2_ref, a2_ref, du_ref, dx2_ref, loss_ref, dgf_ref, dgm_ref, relu_s):
        @pl.when(pl.program_id(0) == 0)
        def _():
            loss_ref[...] = jnp.zeros_like(loss_ref)
            dgf_ref[...] = jnp.zeros_like(dgf_ref)
            dgm_ref[...] = jnp.zeros_like(dgm_ref)

        x1v = x1_ref[...]
        rstd1, xh1 = _rms(x1v)
        h2 = (xh1 * gm_ref[...]).astype(BF16)
        h2_ref[...] = h2
        x2 = x1v
        for j in range(N_SHARD):
            r = jnp.maximum(_dot(h2, wu_ref[j]), 0.0)
            relu_s[j] = r
            a2 = (r * r).astype(BF16)
            a2_ref[:, j * FF_BLK:(j + 1) * FF_BLK] = a2
            x2 = x2 + _dot(a2, wd_ref[j])
        rstd2, xh2 = _rms(x2)
        err = xh2 * gf_ref[...] - t_ref[...]
        loss_ref[...] += jnp.sum(err * err, axis=0, keepdims=True)
        dy = err * (1.0 / D_MODEL)
        dx2, dgf = _rms_bwd(dy, gf_ref[...], xh2, rstd2)
        dgf_ref[...] += dgf
        dx2b = dx2.astype(BF16)
        dx2_ref[...] = dx2b
        dh2 = jnp.zeros((tm, D_MODEL), F32)
        for j in range(N_SHARD):
            du = (_dot_nt(dx2b, wd_ref[j]) * (2.0 * relu_s[j])).astype(BF16)
            du_ref[:, j * FF_BLK:(j + 1) * FF_BLK] = du
            dh2 = dh2 + _dot_nt(du, wu_ref[j])
        dx1n, dgm = _rms_bwd(dh2, gm_ref[...], xh1, rstd1)
        dgm_ref[...] += dgm
        dx1_ref[...] = dx2 + dx1n

    vec = jax.ShapeDtypeStruct((1, D_MODEL), F32)
    return pl.pallas_call(
        body, grid=(S // tm,), name="mlp_fwd_bwd",
        in_specs=[_rows(D_MODEL, tm), _rows(D_MODEL, tm)] + [_whole_vmem()] * 4,
        out_specs=[_rows(D_MODEL, tm), _rows(D_MODEL, tm), _rows(D_FF, tm), _rows(D_FF, tm), _rows(D_MODEL, tm)]
        + [_whole_vmem()] * 3,
        out_shape=[jax.ShapeDtypeStruct((S, D_MODEL), F32), jax.ShapeDtypeStruct((S, D_MODEL), BF16),
                   jax.ShapeDtypeStruct((S, D_FF), BF16), jax.ShapeDtypeStruct((S, D_FF), BF16),
                   jax.ShapeDtypeStruct((S, D_MODEL), BF16), vec, vec, vec],
        scratch_shapes=[pltpu.VMEM((N_SHARD, tm, FF_BLK), F32)],
        compiler_params=_params(),
    )(x1, target, g_mlp, g_fin, w_up, w_down)


def _mix_bwd(dx1, w_out, ya, hf, hb, gate, g_rnn, g_attn):
    S = dx1.shape[0]

    def body(dx1_ref, w_ref, ya_ref, hf_ref, hb_ref, gate_ref, gr_ref, ga_ref,
             dhs_ref, dgate_ref, dya1, dya2, dya3, dx1b_ref, dgr_ref, dga_ref, s_ref):
        @pl.when(pl.program_id(0) == 0)
        def _():
            dgr_ref[...] = jnp.zeros_like(dgr_ref)
            dga_ref[...] = jnp.zeros_like(dga_ref)

        dx1b = dx1_ref[...].astype(BF16)
        dx1b_ref[...] = dx1b
        dmix = jnp.concatenate([_dot_nt(dx1b, w_ref[j]) for j in range(N_SHARD)], axis=-1)
        gg, dgg = _gelu_parts(gate_ref[...])
        hs = hf_ref[...] + hb_ref[...]
        rstd_r, xh_r = _rms(hs * gg)
        dyr, dgr = _rms_bwd(dmix[:, 0:D_RNN], gr_ref[...], xh_r, rstd_r)
        dgr_ref[...] += dgr
        rstd_a, xh_a = _rms(ya_ref[0])
        dya, dga = _rms_bwd(dmix[:, D_RNN:], ga_ref[...], xh_a, rstd_a)
        dga_ref[...] += dga
        dya1[0] = dya
        _split_classes(dya, s_ref, dya2, DILATIONS[1])
        _split_classes(dya, s_ref, dya3, DILATIONS[2])
        dhs_ref[...] = dyr * gg
        dgate_ref[...] = dyr * hs * dgg

    f512 = jax.ShapeDtypeStruct((S, 512), F32)
    vec = jax.ShapeDtypeStruct((1, 512), F32)
    return pl.pallas_call(
        body, grid=(S // TS,), name="mix_bwd",
        in_specs=[_rows(D_MODEL), _whole_vmem(), _class_spec(1)] + [_rows(512)] * 3 + [_whole_vmem()] * 2,
        out_specs=[_rows(512)] * 2 + [_class_spec(d) for d in DILATIONS] + [_rows(D_MODEL)] + [_whole_vmem()] * 2,
        out_shape=[f512, f512] + [_class_shape(S, d, F32) for d in DILATIONS]
        + [jax.ShapeDtypeStruct((S, D_MODEL), BF16), vec, vec],
        scratch_shapes=[pltpu.VMEM((4, TS, 128), F32)],
        compiler_params=_params(),
    )(dx1, w_out, ya, hf, hb, gate, g_rnn, g_attn)


def _rnn_bwd(xr, h, dhs, conv_w, conv_b, wa, ba, wx, bx, lam, reverse):
    S = xr.shape[0]
    nt = S // TS
    order = (lambda i: i) if reverse else (lambda i: nt - 1 - i)
    per = TS // SUB
    last = S // SUB - 1
    if reverse:
        h_halo = pl.BlockSpec((SUB, 512), lambda i: (jnp.minimum((order(i) + 1) * per, last), 0))
    else:
        h_halo = pl.BlockSpec((SUB, 512), lambda i: (jnp.maximum(order(i) * per - 1, 0), 0))
    tile = pl.BlockSpec((TS, 512), lambda i: (order(i), 0))

    def body(x_ref, xp_ref, xn_ref, h_ref, hh_ref, dh_ref, cw_ref, cb_ref, wa_ref, ba_ref, wx_ref, bx_ref, lam_ref,
             dxc_ref, dwa_ref, dwx_ref, dvec_ref, a_s, g_s, carry, edge):
        i = pl.program_id(0)
        t = order(i)

        @pl.when(i == 0)
        def _():
            carry[...] = jnp.zeros_like(carry)
            edge[...] = jnp.zeros_like(edge)
            dwa_ref[...] = jnp.zeros_like(dwa_ref)
            dwx_ref[...] = jnp.zeros_like(dwx_ref)
            dvec_ref[...] = jnp.zeros_like(dvec_ref)

        prev8 = jnp.where(t > 0, xp_ref[...], 0.0)
        next8 = jnp.where(t < nt - 1, xn_ref[...], 0.0)
        xc = _conv_fwd(x_ref[...], prev8, next8, cw_ref[...], cb_ref[...])
        xcb, r, gi, cl, a, mult = _lru_gates(xc, wa_ref, ba_ref[...], wx_ref, bx_ref[...], lam_ref[...])
        hv = h_ref[...]
        if reverse:
            a_s[...] = _shifted(a, edge[...], None, -1)
            edge[...] = a[TS - SUB:TS, :]
            hh = jnp.where(t < nt - 1, hh_ref[...], 0.0)
            h_prev = _shifted(hv, None, hh, 1)
        else:
            a_s[...] = _shifted(a, None, edge[...], 1)
            edge[...] = a[0:SUB, :]
            hh = jnp.where(t > 0, hh_ref[...], 0.0)
            h_prev = _shifted(hv, hh, None, -1)
        _scan_tile(a_s, dh_ref, g_s, carry, not reverse)
        g = g_s[...]
        da = g * h_prev
        gm = g * mult
        d_i = gm * xc
        dmult = g * gi * xc
        dla = da * a - dmult * (a * a) / mult
        d_r = dla * cl
        dpre_r = d_r * r * (1.0 - r)
        dpre_i = d_i * gi * (1.0 - gi)
        dprb = dpre_r.astype(BF16)
        dpib = dpre_i.astype(BF16)
        dwa_ref[...] += _dot_tn(xcb, dprb)
        dwx_ref[...] += _dot_tn(xcb, dpib)
        dvec_ref[0:1, :] += jnp.sum(dpre_r, axis=0, keepdims=True)
        dvec_ref[1:2, :] += jnp.sum(dpre_i, axis=0, keepdims=True)
        dvec_ref[2:3, :] += jnp.sum(dla * r, axis=0, keepdims=True)
        dvec_ref[3:4, :] = dvec_ref[2:3, :] * (LRU_C * _sigmoid(-lam_ref[...]))
        dxc_ref[...] = gm * gi + _dot_nt(dprb, wa_ref[...]) + _dot_nt(dpib, wx_ref[...])

    sq = jax.ShapeDtypeStruct((D_RNN, D_RNN), F32)
    return pl.pallas_call(
        body, grid=(nt,), name="rnn_bwd_rev" if reverse else "rnn_bwd_fwd",
        in_specs=_halo_specs(S, order) + [tile, h_halo, tile] + [_whole_vmem()] * 7,
        out_specs=[tile, _whole_vmem(), _whole_vmem(), _whole_vmem()],
        out_shape=[jax.ShapeDtypeStruct((S, 512), F32), sq, sq, jax.ShapeDtypeStruct((SUB, 512), F32)],
        scratch_shapes=[pltpu.VMEM((TS, 512), F32), pltpu.VMEM((TS, 512), F32), pltpu.VMEM((SUB, 512), F32),
                        pltpu.VMEM((SUB, 512), F32)],
        compiler_params=_params(),
    )(xr, xr, xr, h, h, dhs, conv_w, conv_b, wa, ba, wx, bx, lam)


def _inproj_bwd(x, dx1, xr, dxc_f, dxc_b, dgate, dq3, dk3, dv3, g1, conv_w, w_in):
    S = x.shape[0]
    tb = TS_INPROJ_BWD
    nt = S // tb
    ident = lambda i: i

    def body(x_ref, dx1_ref, xr_ref, xrp_ref, xrn_ref, cf_ref, cfp_ref, cfn_ref, cb_ref, cbp_ref, cbn_ref, dgate_ref,
             dq1, dq2, dq3_, dk1, dk2, dk3_, dv1, dv2, dv3_, g_ref, cw_ref, w_ref,
             dx_ref, dw_ref, dg_ref, dcw_ref, s_ref):
        i = pl.program_id(0)

        @pl.when(i == 0)
        def _():
            dw_ref[...] = jnp.zeros_like(dw_ref)
            dg_ref[...] = jnp.zeros_like(dg_ref)
            dcw_ref[...] = jnp.zeros_like(dcw_ref)

        first, last = i > 0, i < nt - 1
        dxc = cf_ref[...] + cb_ref[...]
        dxc_p = jnp.where(first, cfp_ref[...] + cbp_ref[...], 0.0)
        dxc_n = jnp.where(last, cfn_ref[...] + cbn_ref[...], 0.0)
        cw = cw_ref[...]
        dxr = (_shifted(dxc, dxc_p, dxc_n, 2) * cw[0:1, :] + _shifted(dxc, dxc_p, dxc_n, 1) * cw[1:2, :]
               + dxc * cw[2:3, :] + _shifted(dxc, dxc_p, dxc_n, -1) * cw[3:4, :])
        xrv = xr_ref[...]
        xr_p = jnp.where(first, xrp_ref[...], 0.0)
        xr_n = jnp.where(last, xrn_ref[...], 0.0)
        for k, off in enumerate((-2, -1, 0, 1)):
            dcw_ref[k:k + 1, :] += jnp.sum(dxc * _shifted(xrv, xr_p, xr_n, off), axis=0, keepdims=True)
        dcw_ref[4:5, :] += jnp.sum(dxc, axis=0, keepdims=True)

        def total(a, b, c_):
            return a[0] + _merge_classes(b, s_ref, DILATIONS[1]) + _merge_classes(c_, s_ref, DILATIONS[2])

        dproj = jnp.concatenate(
            [dxr, dgate_ref[...], total(dq1, dq2, dq3_), total(dk1, dk2, dk3_), total(dv1, dv2, dv3_)],
            axis=-1).astype(BF16)
        xv = x_ref[...]
        rstd, xh = _rms(xv)
        hb = (xh * g_ref[...]).astype(BF16)
        dh = jnp.zeros((tb, D_MODEL), F32)
        for j in range(N_SHARD):
            dpj = dproj[:, j * IN_BLK:(j + 1) * IN_BLK]
            dh = dh + _dot_nt(dpj, w_ref[j])
            dw_ref[j] += _dot_tn(hb, dpj)
        dxn, dg = _rms_bwd(dh, g_ref[...], xh, rstd)
        dg_ref[...] += dg
        dx_ref[...] = dx1_ref[...] + dxn

    halo = _halo_specs(S, ident, tb)
    return pl.pallas_call(
        body, grid=(nt,), name="inproj_bwd",
        in_specs=[_rows(D_MODEL, tb), _rows(D_MODEL, tb)] + halo * 3 + [_rows(512, tb)]
        + [_class_spec(d, tb) for d in DILATIONS] * 3 + [_whole_vmem()] * 3,
        out_specs=[_rows(D_MODEL, tb), _whole_vmem(), _whole_vmem(), _whole_vmem()],
        out_shape=[jax.ShapeDtypeStruct((S, D_MODEL), F32), jax.ShapeDtypeStruct((N_SHARD, D_MODEL, IN_BLK), F32),
                   jax.ShapeDtypeStruct((1, D_MODEL), F32), jax.ShapeDtypeStruct((SUB, 512), F32)],
        scratch_shapes=[pltpu.VMEM((4, tb, 128), F32)],
        compiler_params=_params(),
    )(x, dx1, xr, xr, xr, dxc_f, dxc_f, dxc_f, dxc_b, dxc_b, dxc_b, dgate, *dq3, *dk3, *dv3, g1, conv_w, w_in)


def _dw_matmul(a, b, a_cols, b_cols, name):
    S = a.shape[0]
    tk = 1024
    a_shared = a.shape[1] == a_cols
    b_shared = b.shape[1] == b_cols

    def body(a_ref, b_ref, o_ref):
        @pl.when(pl.program_id(1) == 0)
        def _():
            o_ref[...] = jnp.zeros_like(o_ref)
        o_ref[0] += _dot_tn(a_ref[...], b_ref[...])

    return pl.pallas_call(
        body, grid=(N_SHARD, S // tk), name=name,
        in_specs=[pl.BlockSpec((tk, a_cols), (lambda j, k: (k, 0)) if a_shared else (lambda j, k: (k, j))),
                  pl.BlockSpec((tk, b_cols), (lambda j, k: (k, 0)) if b_shared else (lambda j, k: (k, j)))],
        out_specs=pl.BlockSpec((1, a_cols, b_cols), lambda j, k: (j, 0, 0)),
        out_shape=jax.ShapeDtypeStruct((N_SHARD, a_cols, b_cols), F32),
        compiler_params=_params(2),
    )(a, b)


def _t5_bucket_np(rel):
    nb = N_BUCKETS // 2
    max_exact = nb // 2
    ret = np.where(rel > 0, nb, 0)
    n = np.abs(rel)
    nf = np.maximum(n, 1).astype(np.float32)
    large = max_exact + (np.log(nf / np.float32(max_exact)) / np.float32(math.log(MAX_DISTANCE / max_exact))
                         * np.float32(nb - max_exact)).astype(np.int32)
    large = np.minimum(large, nb - 1)
    return ret + np.where(n < max_exact, n, large)


_VARIANT_OFFSETS = (0, -HALF_WIN, Q_BLK - K_WIN)


def _band_index():
    kk = np.arange(K_WIN)[None, :]
    ql = np.arange(Q_BLK)[:, None]
    rel = np.stack([kk - ql + off for off in _VARIANT_OFFSETS])
    return rel, np.abs(rel) <= HALF_WIN


def _bucket_tables(dil):
    rel, valid = _band_index()
    bucket = _t5_bucket_np(np.clip(rel, -HALF_WIN, HALF_WIN) * dil)
    return np.where(valid, bucket, -1).astype(np.int32)


def _bias_mats(rel_bias):
    tables = [_bucket_tables(d) for d in DILATIONS]
    used = [sorted(set(t[t >= 0].tolist())) for t in tables]

    def body(rb_ref, t1, t2, t3, o1, o2, o3):
        for t_ref, o_ref, buckets in ((t1, o1, used[0]), (t2, o2, used[1]), (t3, o3, used[2])):
            for var in range(3):
                bk = t_ref[var]
                for h in range(N_HEADS):
                    acc = jnp.full((Q_BLK, K_WIN), NEG_INF, F32)
                    for b in buckets:
                        acc = jnp.where(bk == b, rb_ref[b, h], acc)
                    o_ref[var, h] = acc

    shp = jax.ShapeDtypeStruct((3, N_HEADS, Q_BLK, K_WIN), F32)
    return pl.pallas_call(
        body, name="bias_tables", in_specs=[pl.BlockSpec(memory_space=pltpu.SMEM)] + [_whole_vmem()] * 3,
        out_shape=[shp] * 3, compiler_params=_params(0),
    )(rel_bias, *[jnp.asarray(t) for t in tables])


def _variant(qb, nq):
    return jnp.where(qb == 0, 0, jnp.where(qb == nq - 1, 2, 1))


def _win_start(qb, L):
    return pl.multiple_of(jnp.clip(qb * Q_BLK - HALF_WIN, 0, L - K_WIN), HALF_WIN)


def _attn_specs(L):
    nsub = min(ATTN_SUB, L // Q_BLK)
    qt = nsub * Q_BLK
    qspec = pl.BlockSpec((None, qt, 128), lambda c, s: (c // 4, s, c % 4))
    kspec = pl.BlockSpec((None, L, 128), lambda c, s: (c // 4, 0, c % 4))
    bspec = pl.BlockSpec((3, 2, Q_BLK, K_WIN), lambda c, s: (0, c % 4, 0, 0))
    return nsub, qspec, kspec, bspec


def _head_masks():
    lane = lax.broadcasted_iota(jnp.int32, (Q_BLK, 128), 1)
    return lane < HEAD_DIM


def _attn_fwd(q, k, v, bias):
    dil, L, _ = q.shape
    nq = L // Q_BLK
    nsub, qspec, kspec, bspec = _attn_specs(L)

    def body(q_ref, k_ref, v_ref, b_ref, o_ref, l_ref):
        h0 = _head_masks()
        for sub in range(nsub):
            qb = pl.program_id(1) * nsub + sub
            rows = slice(sub * Q_BLK, (sub + 1) * Q_BLK)
            st = _win_start(qb, L)
            var = _variant(qb, nq)
            kw = k_ref[pl.ds(st, K_WIN), :]
            vw = v_ref[pl.ds(st, K_WIN), :]
            qs = q_ref[rows, :] * ATTN_SCALE
            outs, lses = [], []
            for h in range(2):
                qh = jnp.where(h0 if h == 0 else ~h0, qs, jnp.zeros_like(qs))
                s = _dot_nt(qh, kw) + b_ref[var, h]
                m = jnp.max(s, axis=-1, keepdims=True)
                p = jnp.exp(s - m)
                l = jnp.sum(p, axis=-1, keepdims=True)
                outs.append(_dot(p.astype(BF16), vw) / l)
                lses.append(m + jnp.log(l))
            o_ref[rows, :] = jnp.where(h0, outs[0], outs[1])
            l_ref[rows, :] = jnp.where(h0, lses[0], lses[1])

    shp = jax.ShapeDtypeStruct((dil, L, D_ATTN), F32)
    return pl.pallas_call(
        body, grid=(dil * 4, nq // nsub), name=f"attn_fwd_d{dil}",
        in_specs=[qspec, kspec, kspec, bspec], out_specs=[qspec, qspec], out_shape=[shp, shp],
        compiler_params=_params(2),
    )(q, k, v, bias)


def _attn_bwd(q, k, v, bias, do, o, lse, rider=None):
    dil, L, _ = q.shape
    nq = L // Q_BLK
    nsub, qspec, kspec, bspec = _attn_specs(L)
    ncol = dil * 4
    nstep = nq // nsub

    def body(q_ref, k_ref, v_ref, b_ref, do_ref, o_ref, l_ref, dq_ref, dk_ref, dv_ref, db_ref, db_s):
        c, step = pl.program_id(0), pl.program_id(1)

        @pl.when((c == 0) & (step == 0))
        def _():
            db_s[...] = jnp.zeros_like(db_s)

        @pl.when(step == 0)
        def _():
            dk_ref[...] = jnp.zeros_like(dk_ref)
            dv_ref[...] = jnp.zeros_like(dv_ref)

        h0 = _head_masks()
        for sub in range(nsub):
            qb = step * nsub + sub
            rows = slice(sub * Q_BLK, (sub + 1) * Q_BLK)
            st = _win_start(qb, L)
            var = _variant(qb, nq)
            kw = k_ref[pl.ds(st, K_WIN), :]
            vw = v_ref[pl.ds(st, K_WIN), :]
            qs = q_ref[rows, :] * ATTN_SCALE
            dof = do_ref[rows, :]
            dob = dof.astype(BF16)
            prod = dof * o_ref[rows, :]
            lsev = l_ref[rows, :]
            dk_acc = jnp.zeros((K_WIN, 128), F32)
            dv_acc = jnp.zeros((K_WIN, 128), F32)
            dqs = []
            for h in range(2):
                hm = h0 if h == 0 else ~h0
                qh = jnp.where(hm, qs, jnp.zeros_like(qs))
                doh = jnp.where(hm, dob, jnp.zeros_like(dob))
                s = _dot_nt(qh, kw) + b_ref[var, h]
                p = jnp.exp(s - lsev[:, h * HEAD_DIM:h * HEAD_DIM + 1])
                dp = _dot_nt(doh, vw)
                dd = jnp.sum(jnp.where(hm, prod, 0.0), axis=-1, keepdims=True)
                ds = p * (dp - dd)
                db_s[var, (c % 4) * 2 + h] += ds
                dsb = ds.astype(BF16)
                dv_acc = dv_acc + _dot_tn(p.astype(BF16), doh)
                dk_acc = dk_acc + _dot_tn(dsb, qh)
                dqs.append(_dot(dsb, kw) * ATTN_SCALE)
            dq_ref[rows, :] = jnp.where(h0, dqs[0], dqs[1])
            dk_ref[pl.ds(st, K_WIN), :] += dk_acc
            dv_ref[pl.ds(st, K_WIN), :] += dv_acc

        @pl.when((c == ncol - 1) & (step == nstep - 1))
        def _():
            db_ref[...] = db_s[...]

    shp = jax.ShapeDtypeStruct((dil, L, D_ATTN), F32)
    dbshape = (3, N_HEADS, Q_BLK, K_WIN)
    return _call(
        body, f"attn_bwd_d{dil}", (ncol, nstep),
        [qspec, kspec, kspec, bspec, qspec, qspec, qspec],
        [qspec, kspec, kspec, _whole_vmem()],
        [shp, shp, shp, jax.ShapeDtypeStruct(dbshape, F32)],
        [pltpu.VMEM(dbshape, F32)], (q, k, v, bias, do, o, lse), rider)


def _bucket_onehots(dil):
    m = np.zeros((3, K_WIN, N_BUCKETS), np.float32)
    for var, off in enumerate(_VARIANT_OFFSETS):
        for rel in range(-HALF_WIN, HALF_WIN + 1):
            col = (rel - off + Q_BLK - 1) % K_WIN
            m[var, col, int(_t5_bucket_np(np.asarray(rel * dil)))] = 1.0
    return jnp.asarray(m)


def _bias_grad(dbs):
    onehots = [_bucket_onehots(d) for d in DILATIONS]
    flip = jnp.asarray(np.eye(Q_BLK, dtype=np.float32)[::-1].copy())

    def body(d1, d2, d3, m1, m2, m3, flip_ref, out_ref):
        hp = lax.Precision.HIGHEST
        acc = jnp.zeros((N_HEADS, N_BUCKETS), F32)
        for d_ref, m_ref in ((d1, m1), (d2, m2), (d3, m3)):
            for var in range(3):
                rows = []
                for h in range(N_HEADS):
                    xrev = jnp.dot(flip_ref[...], d_ref[var, h], precision=hp, preferred_element_type=F32)
                    y = pltpu.roll(xrev, 0, 1, stride=1, stride_axis=0)
                    rows.append(jnp.sum(y, axis=0, keepdims=True))
                vec = jnp.concatenate(rows, axis=0)
                acc = acc + jnp.dot(vec, m_ref[var], precision=hp, preferred_element_type=F32)
        out_ref[...] = acc

    return pl.pallas_call(
        body, name="bias_grad", out_shape=jax.ShapeDtypeStruct((N_HEADS, N_BUCKETS), F32),
        compiler_params=_params(0),
    )(*dbs, *onehots, flip)


def _block_diag(w):
    eye = jnp.eye(N_RNN_BLOCKS, dtype=w.dtype)
    return jnp.einsum("ncd,nm->ncmd", w, eye).reshape(D_RNN, D_RNN).astype(BF16)


def _diag_blocks(dense):
    d = dense.reshape(N_RNN_BLOCKS, RNN_BLOCK, N_RNN_BLOCKS, RNN_BLOCK)
    return jnp.stack([d[n, :, n, :] for n in range(N_RNN_BLOCKS)])


EARLY = ("w_out", "w_up", "w_down")


def _local_step(x, target, p, shards=None):
    p = dict(p)
    biases = _bias_mats(p["rel_bias"])
    lru = {}
    for dname in ("fwd", "bwd"):
        lru[dname] = (_block_diag(p["lru_wa_" + dname]), p["lru_ba_" + dname], _block_diag(p["lru_wx_" + dname]),
                      p["lru_bx_" + dname], p["lru_lam_" + dname])

    def gather(name):
        return None if shards is None else _gather_rider([name], [shards[name]])

    (xr, gate, *qkv), got = _inproj_fwd(x, p["attn_norm_g"], p["w_in"], gather("w_out"))
    p.update(zip(["w_out"], got))
    qs, ks, vs = qkv[0:3], qkv[3:6], qkv[6:9]
    hf, got = _rnn_fwd(xr, p["conv_w"], p["conv_b"], *lru["fwd"], reverse=False, rider=gather("w_up"))
    p.update(zip(["w_up"], got))
    hb, got = _rnn_fwd(xr, p["conv_w"], p["conv_b"], *lru["bwd"], reverse=True, rider=gather("w_down"))
    p.update(zip(["w_down"], got))
    outs, lses = [], []
    for q, k, v, bias in zip(qs, ks, vs, biases):
        o, l = _attn_fwd(q, k, v, bias)
        outs.append(o)
        lses.append(l)
    x1, mixb, *yl = _mix_fwd(outs, lses, hf, hb, gate, x, p["norm_rnn_g"], p["norm_attn_g"], p["w_out"])
    yas, lsts = yl[0:3], yl[3:6]
    dx1, h2b, a2b, dub, dx2b, loss_vec, dg_fin, dg_mlp = _mlp_fwd_bwd(
        x1, target, p["mlp_norm_g"], p["final_norm_g"], p["w_up"], p["w_down"])
    dhs, dgate, *dyas, dx1b, dg_rnn, dg_attn = _mix_bwd(dx1, p["w_out"], yas[0], hf, hb, gate,
                                                        p["norm_rnn_g"], p["norm_attn_g"])
    dw_up = _dw_matmul(h2b, dub, D_MODEL, FF_BLK, "dw_up")
    dw_down = _dw_matmul(a2b, dx2b, FF_BLK, D_MODEL, "dw_down")
    dw_out = _dw_matmul(mixb, dx1b, OUT_BLK, D_MODEL, "dw_out")
    early = [dw_out, dw_up, dw_down]
    dqs, dks, dvs, dbs = [], [], [], []
    for i, (q, k, v, bias, dya, ya, lse) in enumerate(zip(qs, ks, vs, biases, dyas, yas, lsts)):
        rider = None
        if shards is not None:
            make = (lambda: _pair_exchange_rider(EARLY, early), lambda: _chip_exchange_rider(early),
                    lambda: _pair_share_rider(EARLY, early))[i]
            rider = make()
        (dq, dk, dv, db), got = _attn_bwd(q, k, v, bias, dya, ya, lse, rider)
        if shards is not None and i == 0:
            core = lax.axis_index("c").reshape(1).astype(jnp.int32)
            early = [_pair_add(core, g, o, "grad_pair_add_" + n) for n, g, o in zip(EARLY, early, got)]
        elif shards is not None and i == 1:
            early = [_chip_sum(t, "grad_chip_sum_" + n) for n, t in zip(EARLY, got)]
        elif shards is not None:
            early = got
        dqs.append(dq)
        dks.append(dk)
        dvs.append(dv)
        dbs.append(db)
    d_rel_bias = _bias_grad(dbs).T
    dxc_f, dwa_f, dwx_f, dvec_f = _rnn_bwd(xr, hf, dhs, p["conv_w"], p["conv_b"], *lru["fwd"], reverse=False)
    dxc_b, dwa_b, dwx_b, dvec_b = _rnn_bwd(xr, hb, dhs, p["conv_w"], p["conv_b"], *lru["bwd"], reverse=True)
    grad_x, dw_in, dg1, dconv = _inproj_bwd(x, dx1, xr, dxc_f, dxc_b, dgate, dqs, dks, dvs,
                                            p["attn_norm_g"], p["conv_w"], p["w_in"])
    small = {
        "attn_norm_g": dg1, "conv_w": dconv[0:4], "conv_b": dconv[4:5],
        "lru_wa_fwd": _diag_blocks(dwa_f), "lru_ba_fwd": dvec_f[0:1], "lru_wx_fwd": _diag_blocks(dwx_f),
        "lru_bx_fwd": dvec_f[1:2], "lru_lam_fwd": dvec_f[3:4],
        "lru_wa_bwd": _diag_blocks(dwa_b), "lru_ba_bwd": dvec_b[0:1], "lru_wx_bwd": _diag_blocks(dwx_b),
        "lru_bx_bwd": dvec_b[1:2], "lru_lam_bwd": dvec_b[3:4],
        "rel_bias": d_rel_bias, "norm_rnn_g": dg_rnn, "norm_attn_g": dg_attn,
        "mlp_norm_g": dg_mlp, "final_norm_g": dg_fin,
    }
    if shards is None:
        return loss_vec, grad_x, small, {"w_in": dw_in, "w_out": dw_out, "w_up": dw_up, "w_down": dw_down}, {}
    return loss_vec, grad_x, small, {"w_in": dw_in}, dict(zip(EARLY, early))


BIG = ("w_in", "w_out", "w_up", "w_down")
BIG_SHARD = {"w_in": (D_MODEL, IN_BLK), "w_out": (OUT_BLK, D_MODEL), "w_up": (D_MODEL, FF_BLK), "w_down": (FF_BLK, D_MODEL)}
N_BIG = len(BIG)
N_CHIP_PEERS = 3
ANY = pl.BlockSpec(memory_space=pl.ANY)


def _place():
    x, y, c = lax.axis_index("x"), lax.axis_index("y"), lax.axis_index("c")
    chips = [(1 - x, y), (x, 1 - y), (1 - x, 1 - y)]
    return x, y, c, chips


def _remote(src, dst, send_sem, recv_sem, dev):
    return pltpu.make_async_remote_copy(src_ref=src, dst_ref=dst, send_sem=send_sem, recv_sem=recv_sem,
                                        device_id=dev, device_id_type=MESH)


def _staged_start(srcs, bufs, sems):
    legs = [pltpu.make_async_copy(s, b, sems.at[i]) for i, (s, b) in enumerate(zip(srcs, bufs))]
    for cp in legs:
        cp.start()
    return legs


def _staged_finish(legs, bufs, dsts, sems):
    out = []
    for i, (leg, b, d) in enumerate(zip(legs, bufs, dsts)):
        leg.wait()
        cp = pltpu.make_async_copy(b, d, sems.at[i])
        cp.start()
        out.append(cp)
    return out


class _Rider:
    def __init__(self, inputs, out_shape, scratch, first, late, last):
        self.inputs, self.out_shape, self.scratch = list(inputs), list(out_shape), list(scratch)
        self.first, self.late, self.last = first, late, last


def _call(body, name, grid, in_specs, out_specs, out_shape, scratch, operands, rider=None):
    n_grid = len(grid)
    if rider is None:
        res = pl.pallas_call(body, grid=grid, name=name, in_specs=in_specs, out_specs=out_specs, out_shape=out_shape,
                             scratch_shapes=scratch, compiler_params=_params(n_grid))(*operands)
        return list(res), []
    n_in, n_out, n_scr = len(in_specs), len(out_specs), len(scratch)
    ri, ro = len(rider.inputs), len(rider.out_shape)
    nsteps = int(np.prod(grid))
    late_step = max(nsteps - 3, 1)

    def wrapped(*refs):
        a, b = n_in, n_in + ri
        c, d = b + n_out, b + n_out + ro
        e = d + n_scr
        mine = refs[:a] + refs[b:c] + refs[d:e]
        theirs = (refs[a:b], refs[c:d], refs[e:])
        step = pl.program_id(0)
        for ax in range(1, n_grid):
            step = step * grid[ax] + pl.program_id(ax)
        pl.when(step == 0)(lambda: rider.first(*theirs))
        pl.when(step == late_step)(lambda: rider.late(*theirs))
        body(*mine)
        pl.when(step == nsteps - 1)(lambda: rider.last(*theirs))

    res = pl.pallas_call(
        wrapped, grid=grid, name=name, in_specs=list(in_specs) + [ANY] * ri, out_specs=list(out_specs) + [ANY] * ro,
        out_shape=list(out_shape) + rider.out_shape, scratch_shapes=list(scratch) + rider.scratch,
        compiler_params=_params(n_grid),
    )(*operands, *rider.inputs)
    return list(res[:n_out]), list(res[n_out:])


def _run_rider(rider, name):
    ri, ro = len(rider.inputs), len(rider.out_shape)

    def body(*refs):
        parts = (refs[:ri], refs[ri:ri + ro], refs[ri + ro:])
        rider.first(*parts)
        rider.late(*parts)
        rider.last(*parts)

    return list(pl.pallas_call(
        body, name=name, in_specs=[ANY] * ri, out_specs=[ANY] * ro, out_shape=rider.out_shape, scratch_shapes=rider.scratch,
        compiler_params=pltpu.CompilerParams(has_side_effects=True, vmem_limit_bytes=VMEM_LIMIT),
    )(*rider.inputs))


def _nothing(ins, outs, scr):
    return None


def _gather_rider(names, shards, conv_w=None):
    n = len(names)
    items = n + (conv_w is not None)
    halves = [BIG_SHARD[nm][0] // 2 for nm in names]

    def parts(ins, outs, scr):
        x, y, c, chips = _place()
        return x, y, c, chips, 2 * x + y, (x, y, 1 - c), scr[:8], scr[8:]

    def piece(outs, w, chip, core_half):
        return outs[w].at[chip, pl.ds(core_half * halves[w], halves[w])]

    def ici(ins, outs, sems, w, k, chip_xy, c, me):
        return _remote(ins[w].at[pl.ds(c * halves[w], halves[w])], piece(outs, w, me, c),
                       sems[0].at[w, k], sems[1].at[w, k], (*chip_xy, c))

    def first(ins, outs, scr):
        x, y, c, chips, me, sibling, sems, bufs = parts(ins, outs, scr)
        legs = _staged_start(ins, bufs, sems[6])
        for w in range(n):
            for k, chip_xy in enumerate(chips):
                ici(ins, outs, sems, w, k, chip_xy, c, me).start()
        if conv_w is not None:
            for k, (px, py) in enumerate(chips):
                _remote(ins[n], outs[n].at[me], sems[4].at[k], sems[5].at[k], (px, py, c)).start()
        _staged_finish(legs, bufs, [o.at[me] for o in outs], sems[7])

    def late(ins, outs, scr):
        x, y, c, chips, me, sibling, sems, bufs = parts(ins, outs, scr)
        for w in range(n):
            for k, (px, py) in enumerate(chips):
                landed = piece(outs, w, 2 * px + py, c)
                _remote(landed, landed, sems[0].at[w, k], sems[1].at[w, k], (px, py, c)).wait_recv()
                _remote(landed, landed, sems[2].at[w, k], sems[3].at[w, k], sibling).start()

    def last(ins, outs, scr):
        x, y, c, chips, me, sibling, sems, bufs = parts(ins, outs, scr)
        for w in range(n):
            for k, (px, py) in enumerate(chips):
                other = piece(outs, w, 2 * px + py, 1 - c)
                _remote(other, other, sems[2].at[w, k], sems[3].at[w, k], sibling).wait_recv()
        if conv_w is not None:
            for k, (px, py) in enumerate(chips):
                got = outs[n].at[2 * px + py]
                _remote(got, got, sems[4].at[k], sems[5].at[k], (px, py, c)).wait_recv()
                _remote(ins[n], outs[n].at[me], sems[4].at[k], sems[5].at[k], (px, py, c)).wait_send()
        for i in range(items):
            pltpu.make_async_copy(bufs[i], outs[i].at[me], sems[7].at[i]).wait()
        for w in range(n):
            for k, (px, py) in enumerate(chips):
                ici(ins, outs, sems, w, k, (px, py), c, me).wait_send()
                landed = piece(outs, w, 2 * px + py, c)
                _remote(landed, landed, sems[2].at[w, k], sems[3].at[w, k], sibling).wait_send()

    out_shape = [jax.ShapeDtypeStruct((N_SHARD,) + BIG_SHARD[nm], BF16) for nm in names]
    stage = [pltpu.VMEM(BIG_SHARD[nm], BF16) for nm in names]
    inputs = list(shards)
    if conv_w is not None:
        out_shape.append(jax.ShapeDtypeStruct((N_SHARD,) + conv_w.shape, F32))
        stage.append(pltpu.VMEM(conv_w.shape, F32))
        inputs.append(conv_w)
    scratch = ([pltpu.SemaphoreType.DMA((n, N_CHIP_PEERS))] * 4 + [pltpu.SemaphoreType.DMA((N_CHIP_PEERS,))] * 2
               + [pltpu.SemaphoreType.DMA((items,))] * 2 + stage)
    return _Rider(inputs, out_shape, scratch, first, late, last)


def _pair_exchange_rider(names, grads):
    def copies(ins, outs, scr):
        x, y, c, _ = _place()
        out = []
        for w, nm in enumerate(names):
            h = BIG_SHARD[nm][0] // 2
            out.append(_remote(ins[w].at[:, pl.ds((1 - c) * h, h), :], outs[w], scr[0].at[w], scr[1].at[w], (x, y, 1 - c)))
        return out

    def first(ins, outs, scr):
        for cp in copies(ins, outs, scr):
            cp.start()

    def last(ins, outs, scr):
        for cp in copies(ins, outs, scr):
            cp.wait()

    out_shape = [jax.ShapeDtypeStruct((N_SHARD, BIG_SHARD[nm][0] // 2, BIG_SHARD[nm][1]), F32) for nm in names]
    return _Rider(grads, out_shape, [pltpu.SemaphoreType.DMA((len(names),))] * 2, first, _nothing, last)


def _pair_add(core, grad, other, name):
    _, r, cols = grad.shape
    h = r // 2
    th = min(h, 256)
    per = h // th

    def body(c_ref, g_ref, o_ref, out_ref):
        out_ref[...] = (g_ref[...] + o_ref[...]).astype(BF16)

    return pl.pallas_call(
        body, name=name,
        grid_spec=pltpu.PrefetchScalarGridSpec(
            num_scalar_prefetch=1, grid=(N_SHARD, per),
            in_specs=[pl.BlockSpec((1, th, cols), lambda j, i, c_ref: (j, c_ref[0] * per + i, 0)),
                      pl.BlockSpec((1, th, cols), lambda j, i, c_ref: (j, i, 0))],
            out_specs=pl.BlockSpec((1, th, cols), lambda j, i, c_ref: (j, i, 0))),
        out_shape=jax.ShapeDtypeStruct((N_SHARD, h, cols), BF16),
        compiler_params=_params(2),
    )(core, grad, other)


def _chip_exchange_rider(parts):
    n = len(parts)

    def sends(ins, outs, scr):
        x, y, c, chips = _place()
        me = 2 * x + y
        return [_remote(ins[w].at[2 * px + py], outs[w].at[me], scr[0].at[w, k], scr[1].at[w, k], (px, py, c))
                for w in range(n) for k, (px, py) in enumerate(chips)]

    def first(ins, outs, scr):
        x, y, c, chips = _place()
        me = 2 * x + y
        legs = _staged_start([r.at[me] for r in ins], scr[4:], scr[2])
        for cp in sends(ins, outs, scr):
            cp.start()
        _staged_finish(legs, scr[4:], [o.at[me] for o in outs], scr[3])

    def last(ins, outs, scr):
        x, y, c, chips = _place()
        me = 2 * x + y
        for w in range(n):
            for k, (px, py) in enumerate(chips):
                got = outs[w].at[2 * px + py]
                _remote(got, got, scr[0].at[w, k], scr[1].at[w, k], (px, py, c)).wait_recv()
        for cp in sends(ins, outs, scr):
            cp.wait_send()
        for w in range(n):
            pltpu.make_async_copy(scr[4 + w], outs[w].at[me], scr[3].at[w]).wait()

    out_shape = [jax.ShapeDtypeStruct(p.shape, BF16) for p in parts]
    scratch = ([pltpu.SemaphoreType.DMA((n, N_CHIP_PEERS))] * 2 + [pltpu.SemaphoreType.DMA((n,))] * 2
               + [pltpu.VMEM(p.shape[1:], BF16) for p in parts])
    return _Rider(parts, out_shape, scratch, first, _nothing, last)


def _chip_sum(parts, name):
    _, h, cols = parts.shape
    th = min(h, 256)

    def body(p_ref, out_ref):
        acc = p_ref[0].astype(F32)
        for j in range(1, N_SHARD):
            acc = acc + p_ref[j].astype(F32)
        out_ref[...] = acc

    return pl.pallas_call(
        body, name=name, grid=(h // th,),
        in_specs=[pl.BlockSpec((N_SHARD, th, cols), lambda i: (0, i, 0))],
        out_specs=pl.BlockSpec((th, cols), lambda i: (i, 0)),
        out_shape=jax.ShapeDtypeStruct((h, cols), F32),
        compiler_params=_params(),
    )(parts)


def _pair_share_rider(names, halves):
    n = len(names)
    hs = [BIG_SHARD[nm][0] // 2 for nm in names]

    def mine(outs, c):
        return [outs[w].at[pl.ds(c * hs[w], hs[w])] for w in range(n)]

    def first(ins, outs, scr):
        x, y, c, _ = _place()
        legs = _staged_start(ins, scr[4:], scr[2])
        for w, dst in enumerate(mine(outs, c)):
            _remote(ins[w], dst, scr[0].at[w], scr[1].at[w], (x, y, 1 - c)).start()
        _staged_finish(legs, scr[4:], mine(outs, c), scr[3])

    def last(ins, outs, scr):
        x, y, c, _ = _place()
        for w, (theirs, dst) in enumerate(zip(mine(outs, 1 - c), mine(outs, c))):
            _remote(theirs, theirs, scr[0].at[w], scr[1].at[w], (x, y, 1 - c)).wait_recv()
            _remote(ins[w], dst, scr[0].at[w], scr[1].at[w], (x, y, 1 - c)).wait_send()
            pltpu.make_async_copy(scr[4 + w], dst, scr[3].at[w]).wait()

    out_shape = [jax.ShapeDtypeStruct(BIG_SHARD[nm], F32) for nm in names]
    scratch = [pltpu.SemaphoreType.DMA((n,))] * 4 + [pltpu.VMEM((h, BIG_SHARD[nm][1]), F32) for nm, h in zip(names, hs)]
    return _Rider(halves, out_shape, scratch, first, _nothing, last)


N_DEV = 8


def _allreduce_small(vec):
    rows = vec.shape[0]

    def body(v_ref, sum_ref, gat_ref, send, recv, loc_sem):
        x, y, c, _ = _place()
        me = 4 * x + 2 * y + c
        lc = pltpu.make_async_copy(v_ref, gat_ref.at[me], loc_sem)
        lc.start()
        peers = []
        for fx in (0, 1):
            for fy in (0, 1):
                for fc in (0, 1):
                    if fx or fy or fc:
                        peers.append(((1 - x) if fx else x, (1 - y) if fy else y, (1 - c) if fc else c))
        sends = []
        for k, dev in enumerate(peers):
            cp = _remote(v_ref, gat_ref.at[me], send.at[k], recv.at[k], dev)
            cp.start()
            sends.append(cp)
        for k, (px, py, pc) in enumerate(peers):
            got = gat_ref.at[4 * px + 2 * py + pc]
            _remote(got, got, send.at[k], recv.at[k], (px, py, pc)).wait_recv()
        for cp in sends:
            cp.wait_send()
        lc.wait()
        acc = gat_ref[0]
        for j in range(1, N_DEV):
            acc = acc + gat_ref[j]
        sum_ref[...] = acc

    total, _ = pl.pallas_call(
        body, name="allreduce_small",
        in_specs=[_whole_vmem()], out_specs=[_whole_vmem(), _whole_vmem()],
        out_shape=[jax.ShapeDtypeStruct((rows, 128), F32), jax.ShapeDtypeStruct((N_DEV, rows, 128), F32)],
        scratch_shapes=[pltpu.SemaphoreType.DMA((N_DEV - 1,))] * 2 + [pltpu.SemaphoreType.DMA(())],
        compiler_params=pltpu.CompilerParams(has_side_effects=True, vmem_limit_bytes=VMEM_LIMIT),
    )(vec)
    return total


def _adamw(w, g, m, v, name):
    rows, cols = w.shape
    tr = 256 if rows % 256 == 0 else rows
    c1 = 1.0 - ADAM_B1 ** ADAM_STEP
    c2 = 1.0 - ADAM_B2 ** ADAM_STEP

    def body(w_ref, g_ref, m_ref, v_ref, d_ref, m2_ref, v2_ref):
        gv = g_ref[...]
        m2 = ADAM_B1 * m_ref[...] + (1.0 - ADAM_B1) * gv
        v2 = ADAM_B2 * v_ref[...] + (1.0 - ADAM_B2) * (gv * gv)
        m2_ref[...] = m2
        v2_ref[...] = v2
        d_ref[...] = -ADAM_LR * ((m2 / c1) / (jnp.sqrt(v2 / c2) + ADAM_EPS) + ADAM_WD * w_ref[...])

    spec = pl.BlockSpec((tr, cols), lambda i: (i, 0))
    shp = jax.ShapeDtypeStruct((rows, cols), F32)
    return pl.pallas_call(
        body, name=name, grid=(rows // tr,), in_specs=[spec] * 4, out_specs=[spec] * 3, out_shape=[shp] * 3,
        compiler_params=_params(),
    )(w, g, m, v)


SMALL = (
    ("attn_norm_g", (1, 1024)), ("conv_w", (1, 4, 512)), ("conv_b", (1, 512)),
    ("lru_wa_fwd", (1, 8, 64, 64)), ("lru_ba_fwd", (1, 512)), ("lru_wx_fwd", (1, 8, 64, 64)), ("lru_bx_fwd", (1, 512)),
    ("lru_lam_fwd", (1, 512)),
    ("lru_wa_bwd", (1, 8, 64, 64)), ("lru_ba_bwd", (1, 512)), ("lru_wx_bwd", (1, 8, 64, 64)), ("lru_bx_bwd", (1, 512)),
    ("lru_lam_bwd", (1, 512)),
    ("rel_bias", (32, 8)), ("norm_rnn_g", (1, 512)), ("norm_attn_g", (1, 512)), ("mlp_norm_g", (1, 1024)),
    ("final_norm_g", (1024,)),
)
PACK_ROW = 8 * 128


def _pack(parts):
    flat = jnp.concatenate([p.reshape(-1) for p in parts])
    pad = (-flat.shape[0]) % PACK_ROW
    return jnp.pad(flat, (0, pad)).reshape(-1, 128)


def _unpack(packed, shapes):
    flat = packed.reshape(-1)
    out, off = [], 0
    for shp in shapes:
        n = int(np.prod(shp))
        out.append(flat[off:off + n].reshape(shp))
        off += n
    return out


WEIGHT_ORDER = ("attn_norm_g", "w_in", "conv_w", "conv_b", "lru_wa_fwd", "lru_ba_fwd", "lru_wx_fwd", "lru_bx_fwd",
                "lru_lam_fwd", "lru_wa_bwd", "lru_ba_bwd", "lru_wx_bwd", "lru_bx_bwd", "lru_lam_bwd", "rel_bias",
                "norm_rnn_g", "norm_attn_g", "w_out", "mlp_norm_g", "w_up", "w_down", "final_norm_g")


def kernel(x, attn_norm_g, w_in, conv_w, conv_b, lru_wa_fwd, lru_ba_fwd, lru_wx_fwd, lru_bx_fwd, lru_lam_fwd, lru_wa_bwd, lru_ba_bwd, lru_wx_bwd, lru_bx_bwd, lru_lam_bwd, rel_bias, norm_rnn_g, norm_attn_g, w_out, mlp_norm_g, w_up, w_down, final_norm_g, loss_target, m_attn_norm_g, m_w_in, m_conv_w, m_conv_b, m_lru_wa_fwd, m_lru_ba_fwd, m_lru_wx_fwd, m_lru_bx_fwd, m_lru_lam_fwd, m_lru_wa_bwd, m_lru_ba_bwd, m_lru_wx_bwd, m_lru_bx_bwd, m_lru_lam_bwd, m_rel_bias, m_norm_rnn_g, m_norm_attn_g, m_w_out, m_mlp_norm_g, m_w_up, m_w_down, m_final_norm_g, v_attn_norm_g, v_w_in, v_conv_w, v_conv_b, v_lru_wa_fwd, v_lru_ba_fwd, v_lru_wx_fwd, v_lru_bx_fwd, v_lru_lam_fwd, v_lru_wa_bwd, v_lru_ba_bwd, v_lru_wx_bwd, v_lru_bx_bwd, v_lru_lam_bwd, v_rel_bias, v_norm_rnn_g, v_norm_attn_g, v_w_out, v_mlp_norm_g, v_w_up, v_w_down, v_final_norm_g):
    given = dict(locals())
    w = {n: given[n] for n in WEIGHT_ORDER}
    m = {n: given["m_" + n] for n in WEIGHT_ORDER}
    v = {n: given["v_" + n] for n in WEIGHT_ORDER}

    chip = lax.axis_index("x") * 2 + lax.axis_index("y")
    core = lax.axis_index("c")

    shards = {n: w[n][0].astype(BF16) for n in BIG}
    w_in_all, conv_all = _run_rider(_gather_rider(["w_in"], [shards["w_in"]], w["conv_w"][0]), "allgather_w_in")
    p = {n: (t[0] if t.ndim >= 3 else t) for n, t in w.items() if n not in BIG}
    p["final_norm_g"] = w["final_norm_g"].reshape(1, D_MODEL)
    p["conv_w"] = jnp.transpose(conv_all, (1, 0, 2)).reshape(4, D_RNN)
    p["w_in"] = w_in_all

    loss_vec, grad_x, small, big, reduced = _local_step(x[0], loss_target[0], p, {n: shards[n] for n in EARLY})

    late = tuple(big)
    grads = [big[n] for n in late]
    others = _run_rider(_pair_exchange_rider(late, grads), "grad_pair_exchange")
    core_arr = core.reshape(1).astype(jnp.int32)
    parts = [_pair_add(core_arr, g, o, "grad_pair_add_" + n) for n, g, o in zip(late, grads, others)]
    landed = _run_rider(_chip_exchange_rider(parts), "grad_chip_exchange")
    halves = [_chip_sum(t, "grad_chip_sum_" + n) for n, t in zip(late, landed)]
    reduced.update(zip(late, _run_rider(_pair_share_rider(late, halves), "grad_pair_share")))

    loss_local = (0.5 / D_MODEL) * jnp.sum(loss_vec)
    packed = _pack([small[n].reshape(shp) for n, shp in SMALL] + [loss_local.reshape(1)])
    total = _allreduce_small(packed)
    *small_g, loss = _unpack(total, [shp for _, shp in SMALL] + [(1,)])
    g = dict(zip([n for n, _ in SMALL], small_g))
    g["conv_w"] = lax.dynamic_slice_in_dim(g["conv_w"], chip * (D_RNN // N_SHARD), D_RNN // N_SHARD, axis=2)
    for n in BIG:
        g[n] = reduced[n][None]

    delta, new_m, new_v = {}, {}, {}
    for n in BIG:
        d2, m2, v2 = _adamw(w[n][0], reduced[n], m[n][0], v[n][0], "adamw_" + n)
        delta[n], new_m[n], new_v[n] = d2[None], m2[None], v2[None]
    names = [n for n, _ in SMALL]
    shapes = [w[n].shape for n in names]
    d2, m2, v2 = _adamw(_pack([w[n] for n in names]), _pack([g[n] for n in names]), _pack([m[n] for n in names]),
                        _pack([v[n] for n in names]), "adamw_small")
    for dst, src in ((delta, d2), (new_m, m2), (new_v, v2)):
        dst.update(dict(zip(names, _unpack(src, shapes))))

    return (loss.reshape(()), grad_x[None], *[g[n] for n in WEIGHT_ORDER], *[delta[n] for n in WEIGHT_ORDER],
            *[new_m[n] for n in WEIGHT_ORDER], *[new_v[n] for n in WEIGHT_ORDER])
```

```python
import functools
import math

import numpy as np
import jax
import jax.numpy as jnp
from jax import lax
from jax.experimental import pallas as pl
from jax.experimental.pallas import tpu as pltpu

F32 = jnp.float32
BF16 = jnp.bfloat16

D_MODEL = 1024
D_RNN = 512
D_ATTN = 512
N_HEADS = 8
HEAD_DIM = 64
N_RNN_BLOCKS = 8
RNN_BLOCK = 64
D_IN = 2 * D_RNN + 3 * D_ATTN
D_FF = 4 * D_MODEL
N_SHARD = 4
IN_BLK = D_IN // N_SHARD
OUT_BLK = D_MODEL // N_SHARD
FF_BLK = D_FF // N_SHARD
EPS = 1e-6
NEG_INF = -1e30
LRU_C = 8.0
DILATIONS = (1, 4, 16)
HALF_WIN = 64
Q_BLK = 128
K_WIN = 256
N_BUCKETS = 32
MAX_DISTANCE = 1024
ATTN_SCALE = HEAD_DIM ** -0.5

ADAM_LR = 0.001
ADAM_B1 = 0.9
ADAM_B2 = 0.999
ADAM_EPS = 1e-08
ADAM_WD = 0.01
ADAM_STEP = 10

TS = 512
TS_MLP = 256
TS_INPROJ_BWD = 256
ATTN_SUB = 4
SCAN_UNROLL = 4
SUB = 8
VMEM_LIMIT = 56 * 1024 * 1024
GELU_C0 = math.sqrt(2.0 / math.pi)
GELU_C1 = 0.044715

MESH = pl.DeviceIdType.MESH


def _params(n_grid=1):
    return pltpu.CompilerParams(vmem_limit_bytes=VMEM_LIMIT, dimension_semantics=("arbitrary",) * n_grid)


def _whole_vmem():
    return pl.BlockSpec(memory_space=pltpu.VMEM)


def _rows(width, tile=TS):
    return pl.BlockSpec((tile, width), lambda i: (i, 0))


def _sigmoid(z):
    return 0.5 * jnp.tanh(0.5 * z) + 0.5


def _log1p(u):
    w = 1.0 + u
    return jnp.where(w == 1.0, u, jnp.log(w) * (u / (w - 1.0)))


def _softplus(z):
    return jnp.maximum(z, 0.0) + _log1p(jnp.exp(-jnp.abs(z)))


def _gelu_parts(g):
    inner = GELU_C0 * (g + GELU_C1 * g * g * g)
    t = jnp.tanh(inner)
    val = 0.5 * g * (1.0 + t)
    dinner = GELU_C0 * (1.0 + 3.0 * GELU_C1 * g * g)
    grad = 0.5 * (1.0 + t) + 0.5 * g * (1.0 - t * t) * dinner
    return val, grad


def _rms(x):
    rstd = lax.rsqrt(jnp.mean(x * x, axis=-1, keepdims=True) + EPS)
    return rstd, x * rstd


def _rms_bwd(dy, g, xhat, rstd):
    dxh = dy * g
    dx = rstd * (dxh - xhat * jnp.mean(dxh * xhat, axis=-1, keepdims=True))
    dg = jnp.sum(dy * xhat, axis=0, keepdims=True)
    return dx, dg


def _dot(a, b):
    return jnp.dot(a, b, preferred_element_type=F32)


def _dot_nt(a, b):
    return lax.dot_general(a, b, (((1,), (1,)), ((), ())), preferred_element_type=F32)


def _dot_tn(a, b):
    return lax.dot_general(a, b, (((0,), (0,)), ((), ())), preferred_element_type=F32)


def _shifted(tile, prev8, next8, k):
    n = tile.shape[0]
    row = lax.broadcasted_iota(jnp.int32, tile.shape, 0)
    if k == 0:
        return tile
    if k < 0:
        r = pltpu.roll(tile, -k, 0)
        for j in range(-k):
            r = jnp.where(row == j, prev8[SUB + j + k:SUB + j + k + 1, :], r)
        return r
    r = pltpu.roll(tile, n - k, 0)
    for j in range(k):
        r = jnp.where(row == n - k + j, next8[j:j + 1, :], r)
    return r


def _to_lane_blocks(val, s_ref):
    for j in range(val.shape[1] // 128):
        s_ref[j] = val[:, j * 128:(j + 1) * 128]


def _from_lane_blocks(s_ref):
    return jnp.concatenate([s_ref[j] for j in range(s_ref.shape[0])], axis=-1)


def _class_rows(s_ref, r, dil):
    n = s_ref.shape[1] // dil
    return jnp.concatenate([s_ref[j, pl.ds(r, n, stride=dil), :] for j in range(s_ref.shape[0])], axis=-1)


def _split_classes(val, s_ref, out_ref, dil):
    _to_lane_blocks(val, s_ref)
    for r in range(dil):
        out_ref[r] = _class_rows(s_ref, r, dil).astype(out_ref.dtype)


def _merge_classes(in_ref, s_ref, dil):
    n = s_ref.shape[1] // dil
    for r in range(dil):
        v = in_ref[r]
        for j in range(s_ref.shape[0]):
            s_ref[j, pl.ds(r, n, stride=dil), :] = v[:, j * 128:(j + 1) * 128]
    return _from_lane_blocks(s_ref)


def _class_spec(dil, tile=TS):
    return pl.BlockSpec((dil, tile // dil, 512), lambda i: (0, i, 0))


def _class_shape(S, dil, dtype):
    return jax.ShapeDtypeStruct((dil, S // dil, 512), dtype)


def _scan_tile(a_ref, b_ref, h_ref, carry_ref, reverse):
    n = a_ref.shape[0]
    width = a_ref.shape[1]
    groups = n // SUB
    row = lax.broadcasted_iota(jnp.int32, (SUB, width), 0)

    def group_scan(g):
        r0 = pl.multiple_of(g * SUB, SUB)
        a = a_ref[pl.ds(r0, SUB), :]
        b = b_ref[pl.ds(r0, SUB), :]
        for s in (1, 2, 4):
            if reverse:
                a_sh = pltpu.roll(a, SUB - s, 0)
                b_sh = pltpu.roll(b, SUB - s, 0)
                m = row < SUB - s
            else:
                a_sh = pltpu.roll(a, s, 0)
                b_sh = pltpu.roll(b, s, 0)
                m = row >= s
            b = jnp.where(m, a * b_sh + b, b)
            a = jnp.where(m, a * a_sh, a)
        return r0, a, b

    def step(i, carry):
        first = i * SCAN_UNROLL
        order = [(groups - 1 - (first + u)) if reverse else (first + u) for u in range(SCAN_UNROLL)]
        scans = [group_scan(g) for g in order]
        for r0, a, b in scans:
            h = b + a * carry
            h_ref[pl.ds(r0, SUB), :] = h
            edge = h[0:1, :] if reverse else h[SUB - 1:SUB, :]
            carry = jnp.broadcast_to(edge, (SUB, width))
        return carry

    carry_ref[...] = lax.fori_loop(0, groups // SCAN_UNROLL, step, carry_ref[...])


def _conv_fwd(xr, prev8, next8, cw, cb):
    y = cb + _shifted(xr, prev8, next8, -2) * cw[0:1, :]
    y = y + _shifted(xr, prev8, next8, -1) * cw[1:2, :]
    y = y + xr * cw[2:3, :]
    y = y + _shifted(xr, prev8, next8, 1) * cw[3:4, :]
    return y


def _lru_gates(xc, wa_ref, ba, wx_ref, bx, lam):
    xcb = xc.astype(BF16)
    r = _sigmoid(_dot(xcb, wa_ref[...]) + ba)
    i = _sigmoid(_dot(xcb, wx_ref[...]) + bx)
    cl = -LRU_C * _softplus(-lam)
    la = cl * r
    a = jnp.exp(la)
    m2 = -jnp.tanh(la) * (a * a + 1.0)
    inv = lax.rsqrt(m2)
    mult = jnp.where(m2 > 0.0, m2 * inv, 0.0)
    return xcb, r, i, cl, a, mult, inv


def _inproj_fwd(x, g1, w_in, rider=None):
    S = x.shape[0]

    def body(x_ref, g_ref, w_ref, xr_ref, gate_ref, *rest):
        qkv_refs, s_ref = rest[:9], rest[9]
        _, xh = _rms(x_ref[...])
        h = (xh * g_ref[...]).astype(BF16)
        proj = jnp.concatenate([_dot(h, w_ref[j]) for j in range(N_SHARD)], axis=-1)
        xr_ref[...] = proj[:, 0:512]
        gate_ref[...] = proj[:, 512:1024]
        for t in range(3):
            val = proj[:, 1024 + 512 * t:1536 + 512 * t]
            qkv_refs[3 * t][0] = val.astype(BF16)
            _to_lane_blocks(val, s_ref)
            for p, dil in enumerate(DILATIONS[1:]):
                for r in range(dil):
                    qkv_refs[3 * t + 1 + p][r] = _class_rows(s_ref, r, dil).astype(BF16)

    f = jax.ShapeDtypeStruct((S, 512), F32)
    return _call(
        body, "inproj_fwd", (S // TS,),
        [_rows(D_MODEL), _whole_vmem(), _whole_vmem()],
        [_rows(512)] * 2 + [_class_spec(d) for d in DILATIONS] * 3,
        [f, f] + [_class_shape(S, d, BF16) for d in DILATIONS] * 3,
        [pltpu.VMEM((4, TS, 128), F32)], (x, g1, w_in), rider)


def _halo_specs(S, order, tile=TS):
    per = tile // SUB
    last = S // SUB - 1
    return [
        pl.BlockSpec((tile, 512), lambda i: (order(i), 0)),
        pl.BlockSpec((SUB, 512), lambda i: (jnp.maximum(order(i) * per - 1, 0), 0)),
        pl.BlockSpec((SUB, 512), lambda i: (jnp.minimum((order(i) + 1) * per, last), 0)),
    ]


def _rnn_fwd(xr, conv_w, conv_b, wa, ba, wx, bx, lam, reverse, rider=None):
    S = xr.shape[0]
    nt = S // TS
    order = (lambda i: nt - 1 - i) if reverse else (lambda i: i)

    def body(x_ref, xp_ref, xn_ref, cw_ref, cb_ref, wa_ref, ba_ref, wx_ref, bx_ref, lam_ref, h_ref, a_s, b_s, carry):
        i = pl.program_id(0)
        t = order(i)

        @pl.when(i == 0)
        def _():
            carry[...] = jnp.zeros_like(carry)

        prev8 = jnp.where(t > 0, xp_ref[...], 0.0)
        next8 = jnp.where(t < nt - 1, xn_ref[...], 0.0)
        xc = _conv_fwd(x_ref[...], prev8, next8, cw_ref[...], cb_ref[...])
        _, _, gi, _, a, mult, _ = _lru_gates(xc, wa_ref, ba_ref[...], wx_ref, bx_ref[...], lam_ref[...])
        a_s[...] = a
        b_s[...] = mult * (gi * xc)
        _scan_tile(a_s, b_s, h_ref, carry, reverse)

    (h,), carried = _call(
        body, "rnn_fwd_rev" if reverse else "rnn_fwd_fwd", (nt,),
        _halo_specs(S, order) + [_whole_vmem()] * 7,
        [pl.BlockSpec((TS, 512), lambda i: (order(i), 0))],
        [jax.ShapeDtypeStruct((S, 512), F32)],
        [pltpu.VMEM((TS, 512), F32), pltpu.VMEM((TS, 512), F32), pltpu.VMEM((SUB, 512), F32)],
        (xr, xr, xr, conv_w, conv_b, wa, ba, wx, bx, lam), rider)
    return h, carried


def _mix_fwd(o3, l3, hf, hb, gate, x, g_rnn, g_attn, w_out):
    S = x.shape[0]

    def body(o1, o2, o3_, l1, l2, l3_, hf_ref, hb_ref, gate_ref, x_ref, gr_ref, ga_ref, w_ref,
             x1_ref, mix_ref, ya1, ya2, ya3, ls1, ls2, ls3, s_ref):
        la, lb, lc = l1[0], _merge_classes(l2, s_ref, DILATIONS[1]), _merge_classes(l3_, s_ref, DILATIONS[2])
        m = jnp.maximum(jnp.maximum(la, lb), lc)
        ea, eb, ec = jnp.exp(la - m), jnp.exp(lb - m), jnp.exp(lc - m)
        den = ea + eb + ec
        lse = m + jnp.log(den)
        ya = (ea * o1[0] + eb * _merge_classes(o2, s_ref, DILATIONS[1]) + ec * _merge_classes(o3_, s_ref, DILATIONS[2])) / den
        ya1[0] = ya
        ls1[0] = lse
        _split_classes(ya, s_ref, ya2, DILATIONS[1])
        _split_classes(ya, s_ref, ya3, DILATIONS[2])
        _split_classes(lse, s_ref, ls2, DILATIONS[1])
        _split_classes(lse, s_ref, ls3, DILATIONS[2])
        gg, _ = _gelu_parts(gate_ref[...])
        yr = (hf_ref[...] + hb_ref[...]) * gg
        _, xh_r = _rms(yr)
        _, xh_a = _rms(ya)
        mix = jnp.concatenate([xh_r * gr_ref[...], xh_a * ga_ref[...]], axis=-1).astype(BF16)
        mix_ref[...] = mix
        acc = x_ref[...]
        for j in range(N_SHARD):
            acc = acc + _dot(mix[:, j * OUT_BLK:(j + 1) * OUT_BLK], w_ref[j])
        x1_ref[...] = acc

    cls = [_class_spec(d) for d in DILATIONS]
    return pl.pallas_call(
        body, grid=(S // TS,), name="mix_fwd",
        in_specs=cls * 2 + [_rows(512)] * 3 + [_rows(D_MODEL)] + [_whole_vmem()] * 3,
        out_specs=[_rows(D_MODEL), _rows(D_MODEL)] + cls * 2,
        out_shape=[jax.ShapeDtypeStruct((S, D_MODEL), F32), jax.ShapeDtypeStruct((S, D_MODEL), BF16)]
        + [_class_shape(S, d, F32) for d in DILATIONS] * 2,
        scratch_shapes=[pltpu.VMEM((4, TS, 128), F32)],
        compiler_params=_params(),
    )(*o3, *l3, hf, hb, gate, x, g_rnn, g_attn, w_out)


def _mlp_fwd_bwd(x1, target, g_mlp, g_fin, w_up, w_down):
    S = x1.shape[0]
    tm = TS_MLP

    def body(x1_ref, t_ref, gm_ref, gf_ref, wu_ref, wd_ref,
             dx1_ref, h2_ref, a2_ref, du_ref, dx2_ref, loss_ref, dgf_ref, dgm_ref, relu_s):
        @pl.when(pl.program_id(0) == 0)
        def _():
            loss_ref[...] = jnp.zeros_like(loss_ref)
            dgf_ref[...] = jnp.zeros_like(dgf_ref)
            dgm_ref[...] = jnp.zeros_like(dgm_ref)

        x1v = x1_ref[...]
        rstd1, xh1 = _rms(x1v)
        h2 = (xh1 * gm_ref[...]).astype(BF16)
        h2_ref[...] = h2
        x2 = x1v
        for j in range(N_SHARD):
            r = jnp.maximum(_dot(h2, wu_ref[j]), 0.0)
            relu_s[j] = r
            a2 = (r * r).astype(BF16)
            a2_ref[:, j * FF_BLK:(j + 1) * FF_BLK] = a2
            x2 = x2 + _dot(a2, wd_ref[j])
        rstd2, xh2 = _rms(x2)
        err = xh2 * gf_ref[...] - t_ref[...]
        loss_ref[...] += jnp.sum(err * err, axis=0, keepdims=True)
        dy = err * (1.0 / D_MODEL)
        dx2, dgf = _rms_bwd(dy, gf_ref[...], xh2, rstd2)
        dgf_ref[...] += dgf
        dx2b = dx2.astype(BF16)
        dx2_ref[...] = dx2b
        dh2 = jnp.zeros((tm, D_MODEL), F32)
        for j in range(N_SHARD):
            du = (_dot_nt(dx2b, wd_ref[j]) * (2.0 * relu_s[j])).astype(BF16)
            du_ref[:, j * FF_BLK:(j + 1) * FF_BLK] = du
            dh2 = dh2 + _dot_nt(du, wu_ref[j])
        dx1n, dgm = _rms_bwd(dh2, gm_ref[...], xh1, rstd1)
        dgm_ref[...] += dgm
        dx1_ref[...] = dx2 + dx1n

    vec = jax.ShapeDtypeStruct((1, D_MODEL), F32)
    return pl.pallas_call(
        body, grid=(S // tm,), name="mlp_fwd_bwd",
        in_specs=[_rows(D_MODEL, tm), _rows(D_MODEL, tm)] + [_whole_vmem()] * 4,
        out_specs=[_rows(D_MODEL, tm), _rows(D_MODEL, tm), _rows(D_FF, tm), _rows(D_FF, tm), _rows(D_MODEL, tm)]
        + [_whole_vmem()] * 3,
        out_shape=[jax.ShapeDtypeStruct((S, D_MODEL), F32), jax.ShapeDtypeStruct((S, D_MODEL), BF16),
                   jax.ShapeDtypeStruct((S, D_FF), BF16), jax.ShapeDtypeStruct((S, D_FF), BF16),
                   jax.ShapeDtypeStruct((S, D_MODEL), BF16), vec, vec, vec],
        scratch_shapes=[pltpu.VMEM((N_SHARD, tm, FF_BLK), F32)],
        compiler_params=_params(),
    )(x1, target, g_mlp, g_fin, w_up, w_down)


def _mix_bwd(dx1, w_out, ya, hf, hb, gate, g_rnn, g_attn):
    S = dx1.shape[0]

    def body(dx1_ref, w_ref, ya_ref, hf_ref, hb_ref, gate_ref, gr_ref, ga_ref,
             dhs_ref, dgate_ref, dya1, dya2, dya3, dx1b_ref, dgr_ref, dga_ref, s_ref):
        @pl.when(pl.program_id(0) == 0)
        def _():
            dgr_ref[...] = jnp.zeros_like(dgr_ref)
            dga_ref[...] = jnp.zeros_like(dga_ref)

        dx1b = dx1_ref[...].astype(BF16)
        dx1b_ref[...] = dx1b
        dmix = jnp.concatenate([_dot_nt(dx1b, w_ref[j]) for j in range(N_SHARD)], axis=-1)
        gg, dgg = _gelu_parts(gate_ref[...])
        hs = hf_ref[...] + hb_ref[...]
        rstd_r, xh_r = _rms(hs * gg)
        dyr, dgr = _rms_bwd(dmix[:, 0:D_RNN], gr_ref[...], xh_r, rstd_r)
        dgr_ref[...] += dgr
        rstd_a, xh_a = _rms(ya_ref[0])
        dya, dga = _rms_bwd(dmix[:, D_RNN:], ga_ref[...], xh_a, rstd_a)
        dga_ref[...] += dga
        dya1[0] = dya
        _split_classes(dya, s_ref, dya2, DILATIONS[1])
        _split_classes(dya, s_ref, dya3, DILATIONS[2])
        dhs_ref[...] = dyr * gg
        dgate_ref[...] = dyr * hs * dgg

    f512 = jax.ShapeDtypeStruct((S, 512), F32)
    vec = jax.ShapeDtypeStruct((1, 512), F32)
    return pl.pallas_call(
        body, grid=(S // TS,), name="mix_bwd",
        in_specs=[_rows(D_MODEL), _whole_vmem(), _class_spec(1)] + [_rows(512)] * 3 + [_whole_vmem()] * 2,
        out_specs=[_rows(512)] * 2 + [_class_spec(d) for d in DILATIONS] + [_rows(D_MODEL)] + [_whole_vmem()] * 2,
        out_shape=[f512, f512] + [_class_shape(S, d, F32) for d in DILATIONS]
        + [jax.ShapeDtypeStruct((S, D_MODEL), BF16), vec, vec],
        scratch_shapes=[pltpu.VMEM((4, TS, 128), F32)],
        compiler_params=_params(),
    )(dx1, w_out, ya, hf, hb, gate, g_rnn, g_attn)


def _rnn_bwd(xr, h, dhs, conv_w, conv_b, wa, ba, wx, bx, lam, reverse):
    S = xr.shape[0]
    nt = S // TS
    order = (lambda i: i) if reverse else (lambda i: nt - 1 - i)
    per = TS // SUB
    last = S // SUB - 1
    if reverse:
        h_halo = pl.BlockSpec((SUB, 512), lambda i: (jnp.minimum((order(i) + 1) * per, last), 0))
    else:
        h_halo = pl.BlockSpec((SUB, 512), lambda i: (jnp.maximum(order(i) * per - 1, 0), 0))
    tile = pl.BlockSpec((TS, 512), lambda i: (order(i), 0))

    def body(x_ref, xp_ref, xn_ref, h_ref, hh_ref, dh_ref, cw_ref, cb_ref, wa_ref, ba_ref, wx_ref, bx_ref, lam_ref,
             dxc_ref, dwa_ref, dwx_ref, dvec_ref, a_s, g_s, carry, edge):
        i = pl.program_id(0)
        t = order(i)

        @pl.when(i == 0)
        def _():
            carry[...] = jnp.zeros_like(carry)
            edge[...] = jnp.zeros_like(edge)
            dwa_ref[...] = jnp.zeros_like(dwa_ref)
            dwx_ref[...] = jnp.zeros_like(dwx_ref)
            dvec_ref[...] = jnp.zeros_like(dvec_ref)

        prev8 = jnp.where(t > 0, xp_ref[...], 0.0)
        next8 = jnp.where(t < nt - 1, xn_ref[...], 0.0)
        xc = _conv_fwd(x_ref[...], prev8, next8, cw_ref[...], cb_ref[...])
        xcb, r, gi, cl, a, mult, inv_mult = _lru_gates(xc, wa_ref, ba_ref[...], wx_ref, bx_ref[...], lam_ref[...])
        hv = h_ref[...]
        if reverse:
            a_s[...] = _shifted(a, edge[...], None, -1)
            edge[...] = a[TS - SUB:TS, :]
            hh = jnp.where(t < nt - 1, hh_ref[...], 0.0)
            h_prev = _shifted(hv, None, hh, 1)
        else:
            a_s[...] = _shifted(a, None, edge[...], 1)
            edge[...] = a[0:SUB, :]
            hh = jnp.where(t > 0, hh_ref[...], 0.0)
            h_prev = _shifted(hv, hh, None, -1)
        _scan_tile(a_s, dh_ref, g_s, carry, not reverse)
        g = g_s[...]
        da = g * h_prev
        gm = g * mult
        d_i = gm * xc
        dmult = g * gi * xc
        dla = da * a - dmult * (a * a) * inv_mult
        d_r = dla * cl
        dpre_r = d_r * r * (1.0 - r)
        dpre_i = d_i * gi * (1.0 - gi)
        dprb = dpre_r.astype(BF16)
        dpib = dpre_i.astype(BF16)
        dwa_ref[...] += _dot_tn(xcb, dprb)
        dwx_ref[...] += _dot_tn(xcb, dpib)
        dvec_ref[0:1, :] += jnp.sum(dpre_r, axis=0, keepdims=True)
        dvec_ref[1:2, :] += jnp.sum(dpre_i, axis=0, keepdims=True)
        dvec_ref[2:3, :] += jnp.sum(dla * r, axis=0, keepdims=True)
        dvec_ref[3:4, :] = dvec_ref[2:3, :] * (LRU_C * _sigmoid(-lam_ref[...]))
        dxc_ref[...] = gm * gi + _dot_nt(dprb, wa_ref[...]) + _dot_nt(dpib, wx_ref[...])

    sq = jax.ShapeDtypeStruct((D_RNN, D_RNN), F32)
    return pl.pallas_call(
        body, grid=(nt,), name="rnn_bwd_rev" if reverse else "rnn_bwd_fwd",
        in_specs=_halo_specs(S, order) + [tile, h_halo, tile] + [_whole_vmem()] * 7,
        out_specs=[tile, _whole_vmem(), _whole_vmem(), _whole_vmem()],
        out_shape=[jax.ShapeDtypeStruct((S, 512), F32), sq, sq, jax.ShapeDtypeStruct((SUB, 512), F32)],
        scratch_shapes=[pltpu.VMEM((TS, 512), F32), pltpu.VMEM((TS, 512), F32), pltpu.VMEM((SUB, 512), F32),
                        pltpu.VMEM((SUB, 512), F32)],
        compiler_params=_params(),
    )(xr, xr, xr, h, h, dhs, conv_w, conv_b, wa, ba, wx, bx, lam)


def _inproj_bwd(x, dx1, xr, dxc_f, dxc_b, dgate, dq3, dk3, dv3, g1, conv_w, w_in, rider=None):
    S = x.shape[0]
    tb = TS_INPROJ_BWD
    nt = S // tb
    ident = lambda i: i

    def body(x_ref, dx1_ref, xr_ref, xrp_ref, xrn_ref, cf_ref, cfp_ref, cfn_ref, cb_ref, cbp_ref, cbn_ref, dgate_ref,
             dq1, dq2, dq3_, dk1, dk2, dk3_, dv1, dv2, dv3_, g_ref, cw_ref, w_ref,
             dx_ref, dw_ref, dg_ref, dcw_ref, s_ref):
        i = pl.program_id(0)

        @pl.when(i == 0)
        def _():
            dw_ref[...] = jnp.zeros_like(dw_ref)
            dg_ref[...] = jnp.zeros_like(dg_ref)
            dcw_ref[...] = jnp.zeros_like(dcw_ref)

        first, last = i > 0, i < nt - 1
        dxc = cf_ref[...] + cb_ref[...]
        dxc_p = jnp.where(first, cfp_ref[...] + cbp_ref[...], 0.0)
        dxc_n = jnp.where(last, cfn_ref[...] + cbn_ref[...], 0.0)
        cw = cw_ref[...]
        dxr = (_shifted(dxc, dxc_p, dxc_n, 2) * cw[0:1, :] + _shifted(dxc, dxc_p, dxc_n, 1) * cw[1:2, :]
               + dxc * cw[2:3, :] + _shifted(dxc, dxc_p, dxc_n, -1) * cw[3:4, :])
        xrv = xr_ref[...]
        xr_p = jnp.where(first, xrp_ref[...], 0.0)
        xr_n = jnp.where(last, xrn_ref[...], 0.0)
        for k, off in enumerate((-2, -1, 0, 1)):
            dcw_ref[k:k + 1, :] += jnp.sum(dxc * _shifted(xrv, xr_p, xr_n, off), axis=0, keepdims=True)
        dcw_ref[4:5, :] += jnp.sum(dxc, axis=0, keepdims=True)

        def total(a, b, c_):
            return a[0] + _merge_classes(b, s_ref, DILATIONS[1]) + _merge_classes(c_, s_ref, DILATIONS[2])

        dproj = jnp.concatenate(
            [dxr, dgate_ref[...], total(dq1, dq2, dq3_), total(dk1, dk2, dk3_), total(dv1, dv2, dv3_)],
            axis=-1).astype(BF16)
        xv = x_ref[...]
        rstd, xh = _rms(xv)
        hb = (xh * g_ref[...]).astype(BF16)
        dh = jnp.zeros((tb, D_MODEL), F32)
        for j in range(N_SHARD):
            dpj = dproj[:, j * IN_BLK:(j + 1) * IN_BLK]
            dh = dh + _dot_nt(dpj, w_ref[j])
            dw_ref[j] += _dot_tn(hb, dpj)
        dxn, dg = _rms_bwd(dh, g_ref[...], xh, rstd)
        dg_ref[...] += dg
        dx_ref[...] = dx1_ref[...] + dxn

    halo = _halo_specs(S, ident, tb)
    return _call(
        body, "inproj_bwd", (nt,),
        [_rows(D_MODEL, tb), _rows(D_MODEL, tb)] + halo * 3 + [_rows(512, tb)]
        + [_class_spec(d, tb) for d in DILATIONS] * 3 + [_whole_vmem()] * 3,
        [_rows(D_MODEL, tb), _whole_vmem(), _whole_vmem(), _whole_vmem()],
        [jax.ShapeDtypeStruct((S, D_MODEL), F32), jax.ShapeDtypeStruct((N_SHARD, D_MODEL, IN_BLK), F32),
         jax.ShapeDtypeStruct((1, D_MODEL), F32), jax.ShapeDtypeStruct((SUB, 512), F32)],
        [pltpu.VMEM((4, tb, 128), F32)],
        (x, dx1, xr, xr, xr, dxc_f, dxc_f, dxc_f, dxc_b, dxc_b, dxc_b, dgate, *dq3, *dk3, *dv3, g1, conv_w, w_in), rider)


def _dw_matmul(a, b, a_cols, b_cols, name):
    S = a.shape[0]
    tk = 1024
    a_shared = a.shape[1] == a_cols
    b_shared = b.shape[1] == b_cols

    def body(a_ref, b_ref, o_ref):
        @pl.when(pl.program_id(1) == 0)
        def _():
            o_ref[...] = jnp.zeros_like(o_ref)
        o_ref[0] += _dot_tn(a_ref[...], b_ref[...])

    return pl.pallas_call(
        body, grid=(N_SHARD, S // tk), name=name,
        in_specs=[pl.BlockSpec((tk, a_cols), (lambda j, k: (k, 0)) if a_shared else (lambda j, k: (k, j))),
                  pl.BlockSpec((tk, b_cols), (lambda j, k: (k, 0)) if b_shared else (lambda j, k: (k, j)))],
        out_specs=pl.BlockSpec((1, a_cols, b_cols), lambda j, k: (j, 0, 0)),
        out_shape=jax.ShapeDtypeStruct((N_SHARD, a_cols, b_cols), F32),
        compiler_params=_params(2),
    )(a, b)


def _t5_bucket_np(rel):
    nb = N_BUCKETS // 2
    max_exact = nb // 2
    ret = np.where(rel > 0, nb, 0)
    n = np.abs(rel)
    nf = np.maximum(n, 1).astype(np.float32)
    large = max_exact + (np.log(nf / np.float32(max_exact)) / np.float32(math.log(MAX_DISTANCE / max_exact))
                         * np.float32(nb - max_exact)).astype(np.int32)
    large = np.minimum(large, nb - 1)
    return ret + np.where(n < max_exact, n, large)


_VARIANT_OFFSETS = (0, -HALF_WIN, Q_BLK - K_WIN)


def _band_index():
    kk = np.arange(K_WIN)[None, :]
    ql = np.arange(Q_BLK)[:, None]
    rel = np.stack([kk - ql + off for off in _VARIANT_OFFSETS])
    return rel, np.abs(rel) <= HALF_WIN


def _bucket_tables(dil):
    rel, valid = _band_index()
    bucket = _t5_bucket_np(np.clip(rel, -HALF_WIN, HALF_WIN) * dil)
    return np.where(valid, bucket, -1).astype(np.int32)


def _bias_mats(rel_bias):
    tables = [_bucket_tables(d) for d in DILATIONS]
    used = [sorted(set(t[t >= 0].tolist())) for t in tables]

    def body(rb_ref, t1, t2, t3, o1, o2, o3):
        for t_ref, o_ref, buckets in ((t1, o1, used[0]), (t2, o2, used[1]), (t3, o3, used[2])):
            for var in range(3):
                bk = t_ref[var]
                for h in range(N_HEADS):
                    acc = jnp.full((Q_BLK, K_WIN), NEG_INF, F32)
                    for b in buckets:
                        acc = jnp.where(bk == b, rb_ref[b, h], acc)
                    o_ref[var, h] = acc

    shp = jax.ShapeDtypeStruct((3, N_HEADS, Q_BLK, K_WIN), F32)
    return pl.pallas_call(
        body, name="bias_tables", in_specs=[pl.BlockSpec(memory_space=pltpu.SMEM)] + [_whole_vmem()] * 3,
        out_shape=[shp] * 3, compiler_params=_params(0),
    )(rel_bias, *[jnp.asarray(t) for t in tables])


def _variant(qb, nq):
    return jnp.where(qb == 0, 0, jnp.where(qb == nq - 1, 2, 1))


def _win_start(qb, L):
    return pl.multiple_of(jnp.clip(qb * Q_BLK - HALF_WIN, 0, L - K_WIN), HALF_WIN)


def _attn_specs(L):
    nsub = min(ATTN_SUB, L // Q_BLK)
    qt = nsub * Q_BLK
    qspec = pl.BlockSpec((None, qt, 128), lambda c, s: (c // 4, s, c % 4))
    kspec = pl.BlockSpec((None, L, 128), lambda c, s: (c // 4, 0, c % 4))
    bspec = pl.BlockSpec((3, 2, Q_BLK, K_WIN), lambda c, s: (0, c % 4, 0, 0))
    return nsub, qspec, kspec, bspec


def _head_masks():
    lane = lax.broadcasted_iota(jnp.int32, (Q_BLK, 128), 1)
    return lane < HEAD_DIM


def _attn_fwd(q, k, v, bias):
    dil, L, _ = q.shape
    nq = L // Q_BLK
    nsub, qspec, kspec, bspec = _attn_specs(L)

    def body(q_ref, k_ref, v_ref, b_ref, o_ref, l_ref):
        h0 = _head_masks()
        for sub in range(nsub):
            qb = pl.program_id(1) * nsub + sub
            rows = slice(sub * Q_BLK, (sub + 1) * Q_BLK)
            st = _win_start(qb, L)
            var = _variant(qb, nq)
            kw = k_ref[pl.ds(st, K_WIN), :]
            vw = v_ref[pl.ds(st, K_WIN), :]
            qs = q_ref[rows, :] * ATTN_SCALE
            outs, lses = [], []
            for h in range(2):
                qh = jnp.where(h0 if h == 0 else ~h0, qs, jnp.zeros_like(qs))
                s = _dot_nt(qh, kw) + b_ref[var, h]
                m = jnp.max(s, axis=-1, keepdims=True)
                p = jnp.exp(s - m)
                l = jnp.sum(p, axis=-1, keepdims=True)
                outs.append(_dot(p.astype(BF16), vw) / l)
                lses.append(m + jnp.log(l))
            o_ref[rows, :] = jnp.where(h0, outs[0], outs[1])
            l_ref[rows, :] = jnp.where(h0, lses[0], lses[1])

    shp = jax.ShapeDtypeStruct((dil, L, D_ATTN), F32)
    return pl.pallas_call(
        body, grid=(dil * 4, nq // nsub), name=f"attn_fwd_d{dil}",
        in_specs=[qspec, kspec, kspec, bspec], out_specs=[qspec, qspec], out_shape=[shp, shp],
        compiler_params=_params(2),
    )(q, k, v, bias)


def _attn_bwd(q, k, v, bias, do, o, lse, rider=None):
    dil, L, _ = q.shape
    nq = L // Q_BLK
    nsub, qspec, kspec, bspec = _attn_specs(L)
    ncol = dil * 4
    nstep = nq // nsub

    def body(q_ref, k_ref, v_ref, b_ref, do_ref, o_ref, l_ref, dq_ref, dk_ref, dv_ref, db_ref, db_s):
        c, step = pl.program_id(0), pl.program_id(1)

        @pl.when((c == 0) & (step == 0))
        def _():
            db_s[...] = jnp.zeros_like(db_s)

        @pl.when(step == 0)
        def _():
            dk_ref[...] = jnp.zeros_like(dk_ref)
            dv_ref[...] = jnp.zeros_like(dv_ref)

        h0 = _head_masks()
        for sub in range(nsub):
            qb = step * nsub + sub
            rows = slice(sub * Q_BLK, (sub + 1) * Q_BLK)
            st = _win_start(qb, L)
            var = _variant(qb, nq)
            kw = k_ref[pl.ds(st, K_WIN), :]
            vw = v_ref[pl.ds(st, K_WIN), :]
            qs = q_ref[rows, :] * ATTN_SCALE
            dof = do_ref[rows, :]
            dob = dof.astype(BF16)
            prod = dof * o_ref[rows, :]
            lsev = l_ref[rows, :]
            dk_acc = jnp.zeros((K_WIN, 128), F32)
            dv_acc = jnp.zeros((K_WIN, 128), F32)
            dqs = []
            for h in range(2):
                hm = h0 if h == 0 else ~h0
                qh = jnp.where(hm, qs, jnp.zeros_like(qs))
                doh = jnp.where(hm, dob, jnp.zeros_like(dob))
                s = _dot_nt(qh, kw) + b_ref[var, h]
                p = jnp.exp(s - lsev[:, h * HEAD_DIM:h * HEAD_DIM + 1])
                dp = _dot_nt(doh, vw)
                dd = jnp.sum(jnp.where(hm, prod, 0.0), axis=-1, keepdims=True)
                ds = p * (dp - dd)
                db_s[var, (c % 4) * 2 + h] += ds
                dsb = ds.astype(BF16)
                dv_acc = dv_acc + _dot_tn(p.astype(BF16), doh)
                dk_acc = dk_acc + _dot_tn(dsb, qh)
                dqs.append(_dot(dsb, kw) * ATTN_SCALE)
            dq_ref[rows, :] = jnp.where(h0, dqs[0], dqs[1])
            dk_ref[pl.ds(st, K_WIN), :] += dk_acc
            dv_ref[pl.ds(st, K_WIN), :] += dv_acc

        @pl.when((c == ncol - 1) & (step == nstep - 1))
        def _():
            db_ref[...] = db_s[...]

    shp = jax.ShapeDtypeStruct((dil, L, D_ATTN), F32)
    dbshape = (3, N_HEADS, Q_BLK, K_WIN)
    return _call(
        body, f"attn_bwd_d{dil}", (ncol, nstep),
        [qspec, kspec, kspec, bspec, qspec, qspec, qspec],
        [qspec, kspec, kspec, _whole_vmem()],
        [shp, shp, shp, jax.ShapeDtypeStruct(dbshape, F32)],
        [pltpu.VMEM(dbshape, F32)], (q, k, v, bias, do, o, lse), rider)


def _bucket_onehots(dil):
    m = np.zeros((3, K_WIN, N_BUCKETS), np.float32)
    for var, off in enumerate(_VARIANT_OFFSETS):
        for rel in range(-HALF_WIN, HALF_WIN + 1):
            col = (rel - off + Q_BLK - 1) % K_WIN
            m[var, col, int(_t5_bucket_np(np.asarray(rel * dil)))] = 1.0
    return jnp.asarray(m)


def _bias_grad(dbs):
    onehots = [_bucket_onehots(d) for d in DILATIONS]
    flip = jnp.asarray(np.eye(Q_BLK, dtype=np.float32)[::-1].copy())

    def body(d1, d2, d3, m1, m2, m3, flip_ref, out_ref):
        hp = lax.Precision.HIGHEST
        acc = jnp.zeros((N_HEADS, N_BUCKETS), F32)
        for d_ref, m_ref in ((d1, m1), (d2, m2), (d3, m3)):
            for var in range(3):
                rows = []
                for h in range(N_HEADS):
                    xrev = jnp.dot(flip_ref[...], d_ref[var, h], precision=hp, preferred_element_type=F32)
                    y = pltpu.roll(xrev, 0, 1, stride=1, stride_axis=0)
                    rows.append(jnp.sum(y, axis=0, keepdims=True))
                vec = jnp.concatenate(rows, axis=0)
                acc = acc + jnp.dot(vec, m_ref[var], precision=hp, preferred_element_type=F32)
        out_ref[...] = acc

    return pl.pallas_call(
        body, name="bias_grad", out_shape=jax.ShapeDtypeStruct((N_HEADS, N_BUCKETS), F32),
        compiler_params=_params(0),
    )(*dbs, *onehots, flip)


def _block_diag(w):
    eye = jnp.eye(N_RNN_BLOCKS, dtype=w.dtype)
    return jnp.einsum("ncd,nm->ncmd", w, eye).reshape(D_RNN, D_RNN).astype(BF16)


def _diag_blocks(dense):
    d = dense.reshape(N_RNN_BLOCKS, RNN_BLOCK, N_RNN_BLOCKS, RNN_BLOCK)
    return jnp.stack([d[n, :, n, :] for n in range(N_RNN_BLOCKS)])


EARLY = ("w_out", "w_up", "w_down")


def _local_step(x, target, p, shards=None):
    p = dict(p)
    biases = _bias_mats(p["rel_bias"])
    lru = {}
    for dname in ("fwd", "bwd"):
        lru[dname] = (_block_diag(p["lru_wa_" + dname]), p["lru_ba_" + dname], _block_diag(p["lru_wx_" + dname]),
                      p["lru_bx_" + dname], p["lru_lam_" + dname])

    def gather(name):
        return None if shards is None else _gather_rider([name], [shards[name]])

    (xr, gate, *qkv), got = _inproj_fwd(x, p["attn_norm_g"], p["w_in"], gather("w_out"))
    p.update(zip(["w_out"], got))
    qs, ks, vs = qkv[0:3], qkv[3:6], qkv[6:9]
    hf, got = _rnn_fwd(xr, p["conv_w"], p["conv_b"], *lru["fwd"], reverse=False, rider=gather("w_up"))
    p.update(zip(["w_up"], got))
    hb, got = _rnn_fwd(xr, p["conv_w"], p["conv_b"], *lru["bwd"], reverse=True, rider=gather("w_down"))
    p.update(zip(["w_down"], got))
    outs, lses = [], []
    for q, k, v, bias in zip(qs, ks, vs, biases):
        o, l = _attn_fwd(q, k, v, bias)
        outs.append(o)
        lses.append(l)
    x1, mixb, *yl = _mix_fwd(outs, lses, hf, hb, gate, x, p["norm_rnn_g"], p["norm_attn_g"], p["w_out"])
    yas, lsts = yl[0:3], yl[3:6]
    dx1, h2b, a2b, dub, dx2b, loss_vec, dg_fin, dg_mlp = _mlp_fwd_bwd(
        x1, target, p["mlp_norm_g"], p["final_norm_g"], p["w_up"], p["w_down"])
    dhs, dgate, *dyas, dx1b, dg_rnn, dg_attn = _mix_bwd(dx1, p["w_out"], yas[0], hf, hb, gate,
                                                        p["norm_rnn_g"], p["norm_attn_g"])
    dw_up = _dw_matmul(h2b, dub, D_MODEL, FF_BLK, "dw_up")
    dw_down = _dw_matmul(a2b, dx2b, FF_BLK, D_MODEL, "dw_down")
    dw_out = _dw_matmul(mixb, dx1b, OUT_BLK, D_MODEL, "dw_out")
    early = [dw_out, dw_up, dw_down]
    dqs, dks, dvs, dbs = [], [], [], []
    for i, (q, k, v, bias, dya, ya, lse) in enumerate(zip(qs, ks, vs, biases, dyas, yas, lsts)):
        rider = None
        if shards is not None:
            make = (lambda: _pair_exchange_rider(EARLY, early), lambda: _chip_exchange_rider(early),
                    lambda: _pair_share_rider(EARLY, early))[i]
            rider = make()
        (dq, dk, dv, db), got = _attn_bwd(q, k, v, bias, dya, ya, lse, rider)
        if shards is not None and i == 0:
            core = lax.axis_index("c").reshape(1).astype(jnp.int32)
            early = [_pair_add(core, g, o, "grad_pair_add_" + n) for n, g, o in zip(EARLY, early, got)]
        elif shards is not None and i == 1:
            early = [_chip_sum(t, "grad_chip_sum_" + n) for n, t in zip(EARLY, got)]
        elif shards is not None:
            early = got
        dqs.append(dq)
        dks.append(dk)
        dvs.append(dv)
        dbs.append(db)
    d_rel_bias = _bias_grad(dbs).T
    dxc_f, dwa_f, dwx_f, dvec_f = _rnn_bwd(xr, hf, dhs, p["conv_w"], p["conv_b"], *lru["fwd"], reverse=False)
    dxc_b, dwa_b, dwx_b, dvec_b = _rnn_bwd(xr, hb, dhs, p["conv_w"], p["conv_b"], *lru["bwd"], reverse=True)
    small = {
        "lru_wa_fwd": _diag_blocks(dwa_f), "lru_ba_fwd": dvec_f[0:1], "lru_wx_fwd": _diag_blocks(dwx_f),
        "lru_bx_fwd": dvec_f[1:2], "lru_lam_fwd": dvec_f[3:4],
        "lru_wa_bwd": _diag_blocks(dwa_b), "lru_ba_bwd": dvec_b[0:1], "lru_wx_bwd": _diag_blocks(dwx_b),
        "lru_bx_bwd": dvec_b[1:2], "lru_lam_bwd": dvec_b[3:4],
        "rel_bias": d_rel_bias, "norm_rnn_g": dg_rnn, "norm_attn_g": dg_attn,
        "mlp_norm_g": dg_mlp, "final_norm_g": dg_fin,
    }
    loss_local = (0.5 / D_MODEL) * jnp.sum(loss_vec)
    rider = None
    if shards is not None:
        rider = _small_gather_rider(_pack([small[n].reshape(shp) for n, shp in SMALL if n in small]
                                          + [loss_local.reshape(1)]))
    (grad_x, dw_in, dg1, dconv), gathered = _inproj_bwd(x, dx1, xr, dxc_f, dxc_b, dgate, dqs, dks, dvs,
                                                        p["attn_norm_g"], p["conv_w"], p["w_in"], rider)
    last = {"attn_norm_g": dg1, "conv_w": dconv[0:4], "conv_b": dconv[4:5]}
    if shards is None:
        big = {"w_in": dw_in, "w_out": dw_out, "w_up": dw_up, "w_down": dw_down}
        return loss_local, grad_x, {**small, **last}, None, big, {}
    return loss_local, grad_x, last, gathered[0], {"w_in": dw_in}, dict(zip(EARLY, early))


BIG = ("w_in", "w_out", "w_up", "w_down")
BIG_SHARD = {"w_in": (D_MODEL, IN_BLK), "w_out": (OUT_BLK, D_MODEL), "w_up": (D_MODEL, FF_BLK), "w_down": (FF_BLK, D_MODEL)}
N_BIG = len(BIG)
N_CHIP_PEERS = 3
ANY = pl.BlockSpec(memory_space=pl.ANY)


def _place():
    x, y, c = lax.axis_index("x"), lax.axis_index("y"), lax.axis_index("c")
    chips = [(1 - x, y), (x, 1 - y), (1 - x, 1 - y)]
    return x, y, c, chips


def _remote(src, dst, send_sem, recv_sem, dev):
    return pltpu.make_async_remote_copy(src_ref=src, dst_ref=dst, send_sem=send_sem, recv_sem=recv_sem,
                                        device_id=dev, device_id_type=MESH)


def _staged_start(srcs, bufs, sems):
    legs = [pltpu.make_async_copy(s, b, sems.at[i]) for i, (s, b) in enumerate(zip(srcs, bufs))]
    for cp in legs:
        cp.start()
    return legs


def _staged_finish(legs, bufs, dsts, sems):
    out = []
    for i, (leg, b, d) in enumerate(zip(legs, bufs, dsts)):
        leg.wait()
        cp = pltpu.make_async_copy(b, d, sems.at[i])
        cp.start()
        out.append(cp)
    return out


class _Rider:
    def __init__(self, inputs, out_shape, scratch, first, late, last):
        self.inputs, self.out_shape, self.scratch = list(inputs), list(out_shape), list(scratch)
        self.first, self.late, self.last = first, late, last


def _call(body, name, grid, in_specs, out_specs, out_shape, scratch, operands, rider=None):
    n_grid = len(grid)
    if rider is None:
        res = pl.pallas_call(body, grid=grid, name=name, in_specs=in_specs, out_specs=out_specs, out_shape=out_shape,
                             scratch_shapes=scratch, compiler_params=_params(n_grid))(*operands)
        return list(res), []
    n_in, n_out, n_scr = len(in_specs), len(out_specs), len(scratch)
    ri, ro = len(rider.inputs), len(rider.out_shape)
    nsteps = int(np.prod(grid))
    late_step = max(nsteps - 3, 1)

    def wrapped(*refs):
        a, b = n_in, n_in + ri
        c, d = b + n_out, b + n_out + ro
        e = d + n_scr
        mine = refs[:a] + refs[b:c] + refs[d:e]
        theirs = (refs[a:b], refs[c:d], refs[e:])
        step = pl.program_id(0)
        for ax in range(1, n_grid):
            step = step * grid[ax] + pl.program_id(ax)
        pl.when(step == 0)(lambda: rider.first(*theirs))
        pl.when(step == late_step)(lambda: rider.late(*theirs))
        body(*mine)
        pl.when(step == nsteps - 1)(lambda: rider.last(*theirs))

    res = pl.pallas_call(
        wrapped, grid=grid, name=name, in_specs=list(in_specs) + [ANY] * ri, out_specs=list(out_specs) + [ANY] * ro,
        out_shape=list(out_shape) + rider.out_shape, scratch_shapes=list(scratch) + rider.scratch,
        compiler_params=_params(n_grid),
    )(*operands, *rider.inputs)
    return list(res[:n_out]), list(res[n_out:])


def _run_rider(rider, name):
    ri, ro = len(rider.inputs), len(rider.out_shape)

    def body(*refs):
        parts = (refs[:ri], refs[ri:ri + ro], refs[ri + ro:])
        rider.first(*parts)
        rider.late(*parts)
        rider.last(*parts)

    return list(pl.pallas_call(
        body, name=name, in_specs=[ANY] * ri, out_specs=[ANY] * ro, out_shape=rider.out_shape, scratch_shapes=rider.scratch,
        compiler_params=pltpu.CompilerParams(has_side_effects=True, vmem_limit_bytes=VMEM_LIMIT),
    )(*rider.inputs))


def _nothing(ins, outs, scr):
    return None


def _gather_rider(names, shards, conv_w=None):
    n = len(names)
    items = n + (conv_w is not None)
    halves = [BIG_SHARD[nm][0] // 2 for nm in names]

    def parts(ins, outs, scr):
        x, y, c, chips = _place()
        return x, y, c, chips, 2 * x + y, (x, y, 1 - c), scr[:8], scr[8:]

    def piece(outs, w, chip, core_half):
        return outs[w].at[chip, pl.ds(core_half * halves[w], halves[w])]

    def ici(ins, outs, sems, w, k, chip_xy, c, me):
        return _remote(ins[w].at[pl.ds(c * halves[w], halves[w])], piece(outs, w, me, c),
                       sems[0].at[w, k], sems[1].at[w, k], (*chip_xy, c))

    def first(ins, outs, scr):
        x, y, c, chips, me, sibling, sems, bufs = parts(ins, outs, scr)
        legs = _staged_start(ins, bufs, sems[6])
        for w in range(n):
            for k, chip_xy in enumerate(chips):
                ici(ins, outs, sems, w, k, chip_xy, c, me).start()
        if conv_w is not None:
            for k, (px, py) in enumerate(chips):
                _remote(ins[n], outs[n].at[me], sems[4].at[k], sems[5].at[k], (px, py, c)).start()
        _staged_finish(legs, bufs, [o.at[me] for o in outs], sems[7])

    def late(ins, outs, scr):
        x, y, c, chips, me, sibling, sems, bufs = parts(ins, outs, scr)
        for w in range(n):
            for k, (px, py) in enumerate(chips):
                landed = piece(outs, w, 2 * px + py, c)
                _remote(landed, landed, sems[0].at[w, k], sems[1].at[w, k], (px, py, c)).wait_recv()
                _remote(landed, landed, sems[2].at[w, k], sems[3].at[w, k], sibling).start()

    def last(ins, outs, scr):
        x, y, c, chips, me, sibling, sems, bufs = parts(ins, outs, scr)
        for w in range(n):
            for k, (px, py) in enumerate(chips):
                other = piece(outs, w, 2 * px + py, 1 - c)
                _remote(other, other, sems[2].at[w, k], sems[3].at[w, k], sibling).wait_recv()
        if conv_w is not None:
            for k, (px, py) in enumerate(chips):
                got = outs[n].at[2 * px + py]
                _remote(got, got, sems[4].at[k], sems[5].at[k], (px, py, c)).wait_recv()
                _remote(ins[n], outs[n].at[me], sems[4].at[k], sems[5].at[k], (px, py, c)).wait_send()
        for i in range(items):
            pltpu.make_async_copy(bufs[i], outs[i].at[me], sems[7].at[i]).wait()
        for w in range(n):
            for k, (px, py) in enumerate(chips):
                ici(ins, outs, sems, w, k, (px, py), c, me).wait_send()
                landed = piece(outs, w, 2 * px + py, c)
                _remote(landed, landed, sems[2].at[w, k], sems[3].at[w, k], sibling).wait_send()

    out_shape = [jax.ShapeDtypeStruct((N_SHARD,) + BIG_SHARD[nm], BF16) for nm in names]
    stage = [pltpu.VMEM(BIG_SHARD[nm], BF16) for nm in names]
    inputs = list(shards)
    if conv_w is not None:
        out_shape.append(jax.ShapeDtypeStruct((N_SHARD,) + conv_w.shape, F32))
        stage.append(pltpu.VMEM(conv_w.shape, F32))
        inputs.append(conv_w)
    scratch = ([pltpu.SemaphoreType.DMA((n, N_CHIP_PEERS))] * 4 + [pltpu.SemaphoreType.DMA((N_CHIP_PEERS,))] * 2
               + [pltpu.SemaphoreType.DMA((items,))] * 2 + stage)
    return _Rider(inputs, out_shape, scratch, first, late, last)


def _pair_exchange_rider(names, grads):
    def copies(ins, outs, scr):
        x, y, c, _ = _place()
        out = []
        for w, nm in enumerate(names):
            h = BIG_SHARD[nm][0] // 2
            out.append(_remote(ins[w].at[:, pl.ds((1 - c) * h, h), :], outs[w], scr[0].at[w], scr[1].at[w], (x, y, 1 - c)))
        return out

    def first(ins, outs, scr):
        for cp in copies(ins, outs, scr):
            cp.start()

    def last(ins, outs, scr):
        for cp in copies(ins, outs, scr):
            cp.wait()

    out_shape = [jax.ShapeDtypeStruct((N_SHARD, BIG_SHARD[nm][0] // 2, BIG_SHARD[nm][1]), F32) for nm in names]
    return _Rider(grads, out_shape, [pltpu.SemaphoreType.DMA((len(names),))] * 2, first, _nothing, last)


def _pair_add(core, grad, other, name):
    _, r, cols = grad.shape
    h = r // 2
    th = min(h, 256)
    per = h // th

    def body(c_ref, g_ref, o_ref, out_ref):
        out_ref[...] = (g_ref[...] + o_ref[...]).astype(BF16)

    return pl.pallas_call(
        body, name=name,
        grid_spec=pltpu.PrefetchScalarGridSpec(
            num_scalar_prefetch=1, grid=(N_SHARD, per),
            in_specs=[pl.BlockSpec((1, th, cols), lambda j, i, c_ref: (j, c_ref[0] * per + i, 0)),
                      pl.BlockSpec((1, th, cols), lambda j, i, c_ref: (j, i, 0))],
            out_specs=pl.BlockSpec((1, th, cols), lambda j, i, c_ref: (j, i, 0))),
        out_shape=jax.ShapeDtypeStruct((N_SHARD, h, cols), BF16),
        compiler_params=_params(2),
    )(core, grad, other)


def _chip_exchange_rider(parts):
    n = len(parts)

    def sends(ins, outs, scr):
        x, y, c, chips = _place()
        me = 2 * x + y
        return [_remote(ins[w].at[2 * px + py], outs[w].at[me], scr[0].at[w, k], scr[1].at[w, k], (px, py, c))
                for w in range(n) for k, (px, py) in enumerate(chips)]

    def first(ins, outs, scr):
        x, y, c, chips = _place()
        me = 2 * x + y
        legs = _staged_start([r.at[me] for r in ins], scr[4:], scr[2])
        for cp in sends(ins, outs, scr):
            cp.start()
        _staged_finish(legs, scr[4:], [o.at[me] for o in outs], scr[3])

    def last(ins, outs, scr):
        x, y, c, chips = _place()
        me = 2 * x + y
        for w in range(n):
            for k, (px, py) in enumerate(chips):
                got = outs[w].at[2 * px + py]
                _remote(got, got, scr[0].at[w, k], scr[1].at[w, k], (px, py, c)).wait_recv()
        for cp in sends(ins, outs, scr):
            cp.wait_send()
        for w in range(n):
            pltpu.make_async_copy(scr[4 + w], outs[w].at[me], scr[3].at[w]).wait()

    out_shape = [jax.ShapeDtypeStruct(p.shape, BF16) for p in parts]
    scratch = ([pltpu.SemaphoreType.DMA((n, N_CHIP_PEERS))] * 2 + [pltpu.SemaphoreType.DMA((n,))] * 2
               + [pltpu.VMEM(p.shape[1:], BF16) for p in parts])
    return _Rider(parts, out_shape, scratch, first, _nothing, last)


def _chip_sum(parts, name):
    _, h, cols = parts.shape
    th = min(h, 256)

    def body(p_ref, out_ref):
        acc = p_ref[0].astype(F32)
        for j in range(1, N_SHARD):
            acc = acc + p_ref[j].astype(F32)
        out_ref[...] = acc

    return pl.pallas_call(
        body, name=name, grid=(h // th,),
        in_specs=[pl.BlockSpec((N_SHARD, th, cols), lambda i: (0, i, 0))],
        out_specs=pl.BlockSpec((th, cols), lambda i: (i, 0)),
        out_shape=jax.ShapeDtypeStruct((h, cols), F32),
        compiler_params=_params(),
    )(parts)


def _pair_share_rider(names, halves):
    n = len(names)
    hs = [BIG_SHARD[nm][0] // 2 for nm in names]

    def mine(outs, c):
        return [outs[w].at[pl.ds(c * hs[w], hs[w])] for w in range(n)]

    def first(ins, outs, scr):
        x, y, c, _ = _place()
        legs = _staged_start(ins, scr[4:], scr[2])
        for w, dst in enumerate(mine(outs, c)):
            _remote(ins[w], dst, scr[0].at[w], scr[1].at[w], (x, y, 1 - c)).start()
        _staged_finish(legs, scr[4:], mine(outs, c), scr[3])

    def last(ins, outs, scr):
        x, y, c, _ = _place()
        for w, (theirs, dst) in enumerate(zip(mine(outs, 1 - c), mine(outs, c))):
            _remote(theirs, theirs, scr[0].at[w], scr[1].at[w], (x, y, 1 - c)).wait_recv()
            _remote(ins[w], dst, scr[0].at[w], scr[1].at[w], (x, y, 1 - c)).wait_send()
            pltpu.make_async_copy(scr[4 + w], dst, scr[3].at[w]).wait()

    out_shape = [jax.ShapeDtypeStruct(BIG_SHARD[nm], F32) for nm in names]
    scratch = [pltpu.SemaphoreType.DMA((n,))] * 4 + [pltpu.VMEM((h, BIG_SHARD[nm][1]), F32) for nm, h in zip(names, hs)]
    return _Rider(halves, out_shape, scratch, first, _nothing, last)


N_DEV = 8


def _all_peers(x, y, c):
    return [((1 - x) if fx else x, (1 - y) if fy else y, (1 - c) if fc else c)
            for fx in (0, 1) for fy in (0, 1) for fc in (0, 1) if fx or fy or fc]


def _small_gather_rider(vec):
    def sends(ins, outs, scr):
        x, y, c, _ = _place()
        me = 4 * x + 2 * y + c
        return [_remote(ins[0], outs[0].at[me], scr[0].at[k], scr[1].at[k], dev) for k, dev in enumerate(_all_peers(x, y, c))]

    def first(ins, outs, scr):
        x, y, c, _ = _place()
        legs = _staged_start(ins, scr[4:], scr[2])
        for cp in sends(ins, outs, scr):
            cp.start()
        _staged_finish(legs, scr[4:], [outs[0].at[4 * x + 2 * y + c]], scr[3])

    def last(ins, outs, scr):
        x, y, c, _ = _place()
        for k, (px, py, pc) in enumerate(_all_peers(x, y, c)):
            got = outs[0].at[4 * px + 2 * py + pc]
            _remote(got, got, scr[0].at[k], scr[1].at[k], (px, py, pc)).wait_recv()
        for cp in sends(ins, outs, scr):
            cp.wait_send()
        pltpu.make_async_copy(scr[4], outs[0].at[4 * x + 2 * y + c], scr[3].at[0]).wait()

    scratch = ([pltpu.SemaphoreType.DMA((N_DEV - 1,))] * 2 + [pltpu.SemaphoreType.DMA((1,))] * 2
               + [pltpu.VMEM(vec.shape, F32)])
    return _Rider([vec], [jax.ShapeDtypeStruct((N_DEV,) + vec.shape, F32)], scratch, first, _nothing, last)


def _sum_devices(gathered):
    def body(g_ref, out_ref):
        acc = g_ref[0]
        for j in range(1, N_DEV):
            acc = acc + g_ref[j]
        out_ref[...] = acc

    return pl.pallas_call(body, name="sum_devices", out_shape=jax.ShapeDtypeStruct(gathered.shape[1:], F32),
                          compiler_params=_params(0))(gathered)


def _allreduce_small(vec):
    rows = vec.shape[0]

    def body(v_ref, sum_ref, gat_ref, send, recv, loc_sem):
        x, y, c, _ = _place()
        me = 4 * x + 2 * y + c
        lc = pltpu.make_async_copy(v_ref, gat_ref.at[me], loc_sem)
        lc.start()
        peers = []
        for fx in (0, 1):
            for fy in (0, 1):
                for fc in (0, 1):
                    if fx or fy or fc:
                        peers.append(((1 - x) if fx else x, (1 - y) if fy else y, (1 - c) if fc else c))
        sends = []
        for k, dev in enumerate(peers):
            cp = _remote(v_ref, gat_ref.at[me], send.at[k], recv.at[k], dev)
            cp.start()
            sends.append(cp)
        for k, (px, py, pc) in enumerate(peers):
            got = gat_ref.at[4 * px + 2 * py + pc]
            _remote(got, got, send.at[k], recv.at[k], (px, py, pc)).wait_recv()
        for cp in sends:
            cp.wait_send()
        lc.wait()
        acc = gat_ref[0]
        for j in range(1, N_DEV):
            acc = acc + gat_ref[j]
        sum_ref[...] = acc

    total, _ = pl.pallas_call(
        body, name="allreduce_small",
        in_specs=[_whole_vmem()], out_specs=[_whole_vmem(), _whole_vmem()],
        out_shape=[jax.ShapeDtypeStruct((rows, 128), F32), jax.ShapeDtypeStruct((N_DEV, rows, 128), F32)],
        scratch_shapes=[pltpu.SemaphoreType.DMA((N_DEV - 1,))] * 2 + [pltpu.SemaphoreType.DMA(())],
        compiler_params=pltpu.CompilerParams(has_side_effects=True, vmem_limit_bytes=VMEM_LIMIT),
    )(vec)
    return total


def _adamw(w, g, m, v, name):
    rows, cols = w.shape
    tr = 256 if rows % 256 == 0 else rows
    c1 = 1.0 - ADAM_B1 ** ADAM_STEP
    c2 = 1.0 - ADAM_B2 ** ADAM_STEP

    def body(w_ref, g_ref, m_ref, v_ref, d_ref, m2_ref, v2_ref):
        gv = g_ref[...]
        m2 = ADAM_B1 * m_ref[...] + (1.0 - ADAM_B1) * gv
        v2 = ADAM_B2 * v_ref[...] + (1.0 - ADAM_B2) * (gv * gv)
        m2_ref[...] = m2
        v2_ref[...] = v2
        d_ref[...] = -ADAM_LR * ((m2 / c1) / (jnp.sqrt(v2 / c2) + ADAM_EPS) + ADAM_WD * w_ref[...])

    spec = pl.BlockSpec((tr, cols), lambda i: (i, 0))
    shp = jax.ShapeDtypeStruct((rows, cols), F32)
    return pl.pallas_call(
        body, name=name, grid=(rows // tr,), in_specs=[spec] * 4, out_specs=[spec] * 3, out_shape=[shp] * 3,
        compiler_params=_params(),
    )(w, g, m, v)


SMALL = (
    ("attn_norm_g", (1, 1024)), ("conv_w", (1, 4, 512)), ("conv_b", (1, 512)),
    ("lru_wa_fwd", (1, 8, 64, 64)), ("lru_ba_fwd", (1, 512)), ("lru_wx_fwd", (1, 8, 64, 64)), ("lru_bx_fwd", (1, 512)),
    ("lru_lam_fwd", (1, 512)),
    ("lru_wa_bwd", (1, 8, 64, 64)), ("lru_ba_bwd", (1, 512)), ("lru_wx_bwd", (1, 8, 64, 64)), ("lru_bx_bwd", (1, 512)),
    ("lru_lam_bwd", (1, 512)),
    ("rel_bias", (32, 8)), ("norm_rnn_g", (1, 512)), ("norm_attn_g", (1, 512)), ("mlp_norm_g", (1, 1024)),
    ("final_norm_g", (1024,)),
)
PACK_ROW = 8 * 128


def _pack(parts):
    flat = jnp.concatenate([p.reshape(-1) for p in parts])
    pad = (-flat.shape[0]) % PACK_ROW
    return jnp.pad(flat, (0, pad)).reshape(-1, 128)


def _unpack(packed, shapes):
    flat = packed.reshape(-1)
    out, off = [], 0
    for shp in shapes:
        n = int(np.prod(shp))
        out.append(flat[off:off + n].reshape(shp))
        off += n
    return out


WEIGHT_ORDER = ("attn_norm_g", "w_in", "conv_w", "conv_b", "lru_wa_fwd", "lru_ba_fwd", "lru_wx_fwd", "lru_bx_fwd",
                "lru_lam_fwd", "lru_wa_bwd", "lru_ba_bwd", "lru_wx_bwd", "lru_bx_bwd", "lru_lam_bwd", "rel_bias",
                "norm_rnn_g", "norm_attn_g", "w_out", "mlp_norm_g", "w_up", "w_down", "final_norm_g")


def kernel(x, attn_norm_g, w_in, conv_w, conv_b, lru_wa_fwd, lru_ba_fwd, lru_wx_fwd, lru_bx_fwd, lru_lam_fwd, lru_wa_bwd, lru_ba_bwd, lru_wx_bwd, lru_bx_bwd, lru_lam_bwd, rel_bias, norm_rnn_g, norm_attn_g, w_out, mlp_norm_g, w_up, w_down, final_norm_g, loss_target, m_attn_norm_g, m_w_in, m_conv_w, m_conv_b, m_lru_wa_fwd, m_lru_ba_fwd, m_lru_wx_fwd, m_lru_bx_fwd, m_lru_lam_fwd, m_lru_wa_bwd, m_lru_ba_bwd, m_lru_wx_bwd, m_lru_bx_bwd, m_lru_lam_bwd, m_rel_bias, m_norm_rnn_g, m_norm_attn_g, m_w_out, m_mlp_norm_g, m_w_up, m_w_down, m_final_norm_g, v_attn_norm_g, v_w_in, v_conv_w, v_conv_b, v_lru_wa_fwd, v_lru_ba_fwd, v_lru_wx_fwd, v_lru_bx_fwd, v_lru_lam_fwd, v_lru_wa_bwd, v_lru_ba_bwd, v_lru_wx_bwd, v_lru_bx_bwd, v_lru_lam_bwd, v_rel_bias, v_norm_rnn_g, v_norm_attn_g, v_w_out, v_mlp_norm_g, v_w_up, v_w_down, v_final_norm_g):
    given = dict(locals())
    w = {n: given[n] for n in WEIGHT_ORDER}
    m = {n: given["m_" + n] for n in WEIGHT_ORDER}
    v = {n: given["v_" + n] for n in WEIGHT_ORDER}

    chip = lax.axis_index("x") * 2 + lax.axis_index("y")
    core = lax.axis_index("c")

    shards = {n: w[n][0].astype(BF16) for n in BIG}
    w_in_all, conv_all = _run_rider(_gather_rider(["w_in"], [shards["w_in"]], w["conv_w"][0]), "allgather_w_in")
    p = {n: (t[0] if t.ndim >= 3 else t) for n, t in w.items() if n not in BIG}
    p["final_norm_g"] = w["final_norm_g"].reshape(1, D_MODEL)
    p["conv_w"] = jnp.transpose(conv_all, (1, 0, 2)).reshape(4, D_RNN)
    p["w_in"] = w_in_all

    _, grad_x, small, gathered, big, reduced = _local_step(x[0], loss_target[0], p, {n: shards[n] for n in EARLY})

    late = tuple(big)
    grads = [big[n] for n in late]
    others = _run_rider(_pair_exchange_rider(late, grads), "grad_pair_exchange")
    core_arr = core.reshape(1).astype(jnp.int32)
    parts = [_pair_add(core_arr, g, o, "grad_pair_add_" + n) for n, g, o in zip(late, grads, others)]
    landed = _run_rider(_chip_exchange_rider(parts), "grad_chip_exchange")
    halves = [_chip_sum(t, "grad_chip_sum_" + n) for n, t in zip(late, landed)]
    reduced.update(zip(late, _run_rider(_pair_share_rider(late, halves), "grad_pair_share")))

    early_small = [(n, shp) for n, shp in SMALL if n not in small]
    late_small = [(n, shp) for n, shp in SMALL if n in small]
    *early_g, loss = _unpack(_sum_devices(gathered), [shp for _, shp in early_small] + [(1,)])
    late_g = _unpack(_allreduce_small(_pack([small[n].reshape(shp) for n, shp in late_small])),
                     [shp for _, shp in late_small])
    g = dict(zip([n for n, _ in early_small + late_small], early_g + late_g))
    g["conv_w"] = lax.dynamic_slice_in_dim(g["conv_w"], chip * (D_RNN // N_SHARD), D_RNN // N_SHARD, axis=2)
    for n in BIG:
        g[n] = reduced[n][None]

    delta, new_m, new_v = {}, {}, {}
    for n in BIG:
        d2, m2, v2 = _adamw(w[n][0], reduced[n], m[n][0], v[n][0], "adamw_" + n)
        delta[n], new_m[n], new_v[n] = d2[None], m2[None], v2[None]
    names = [n for n, _ in SMALL]
    shapes = [w[n].shape for n in names]
    d2, m2, v2 = _adamw(_pack([w[n] for n in names]), _pack([g[n] for n in names]), _pack([m[n] for n in names]),
                        _pack([v[n] for n in names]), "adamw_small")
    for dst, src in ((delta, d2), (new_m, m2), (new_v, v2)):
        dst.update(dict(zip(names, _unpack(src, shapes))))

    return (loss.reshape(()), grad_x[None], *[g[n] for n in WEIGHT_ORDER], *[delta[n] for n in WEIGHT_ORDER],
            *[new_m[n] for n in WEIGHT_ORDER], *[new_v[n] for n in WEIGHT_ORDER])
```

```python
import functools
import math

import numpy as np
import jax
import jax.numpy as jnp
from jax import lax
from jax.experimental import pallas as pl
from jax.experimental.pallas import tpu as pltpu

F32 = jnp.float32
BF16 = jnp.bfloat16

D_MODEL = 1024
D_RNN = 512
D_ATTN = 512
N_HEADS = 8
HEAD_DIM = 64
N_RNN_BLOCKS = 8
RNN_BLOCK = 64
D_IN = 2 * D_RNN + 3 * D_ATTN
D_FF = 4 * D_MODEL
N_SHARD = 4
IN_BLK = D_IN // N_SHARD
OUT_BLK = D_MODEL // N_SHARD
FF_BLK = D_FF // N_SHARD
EPS = 1e-6
NEG_INF = -1e30
LRU_C = 8.0
DILATIONS = (1, 4, 16)
HALF_WIN = 64
Q_BLK = 128
K_WIN = 256
N_BUCKETS = 32
MAX_DISTANCE = 1024
ATTN_SCALE = HEAD_DIM ** -0.5

ADAM_LR = 0.001
ADAM_B1 = 0.9
ADAM_B2 = 0.999
ADAM_EPS = 1e-08
ADAM_WD = 0.01
ADAM_STEP = 10

TS = 512
TS_MLP = 256
TS_INPROJ_BWD = 256
ATTN_SUB = 4
SCAN_UNROLL = 4
SUB = 8
VMEM_LIMIT = 56 * 1024 * 1024
GELU_C0 = math.sqrt(2.0 / math.pi)
GELU_C1 = 0.044715

MESH = pl.DeviceIdType.MESH


def _params(n_grid=1):
    return pltpu.CompilerParams(vmem_limit_bytes=VMEM_LIMIT, dimension_semantics=("arbitrary",) * n_grid)


def _whole_vmem():
    return pl.BlockSpec(memory_space=pltpu.VMEM)


def _rows(width, tile=TS):
    return pl.BlockSpec((tile, width), lambda i: (i, 0))


def _sigmoid(z):
    return 0.5 * jnp.tanh(0.5 * z) + 0.5


def _log1p(u):
    w = 1.0 + u
    return jnp.where(w == 1.0, u, jnp.log(w) * (u / (w - 1.0)))


def _softplus(z):
    return jnp.maximum(z, 0.0) + _log1p(jnp.exp(-jnp.abs(z)))


def _gelu_parts(g):
    inner = GELU_C0 * (g + GELU_C1 * g * g * g)
    t = jnp.tanh(inner)
    val = 0.5 * g * (1.0 + t)
    dinner = GELU_C0 * (1.0 + 3.0 * GELU_C1 * g * g)
    grad = 0.5 * (1.0 + t) + 0.5 * g * (1.0 - t * t) * dinner
    return val, grad


def _rms(x):
    rstd = lax.rsqrt(jnp.mean(x * x, axis=-1, keepdims=True) + EPS)
    return rstd, x * rstd


def _rms_bwd(dy, g, xhat, rstd):
    dxh = dy * g
    dx = rstd * (dxh - xhat * jnp.mean(dxh * xhat, axis=-1, keepdims=True))
    dg = jnp.sum(dy * xhat, axis=0, keepdims=True)
    return dx, dg


def _dot(a, b):
    return jnp.dot(a, b, preferred_element_type=F32)


def _dot_nt(a, b):
    return lax.dot_general(a, b, (((1,), (1,)), ((), ())), preferred_element_type=F32)


def _dot_tn(a, b):
    return lax.dot_general(a, b, (((0,), (0,)), ((), ())), preferred_element_type=F32)


def _shifted(tile, prev8, next8, k):
    n = tile.shape[0]
    row = lax.broadcasted_iota(jnp.int32, tile.shape, 0)
    if k == 0:
        return tile
    if k < 0:
        r = pltpu.roll(tile, -k, 0)
        for j in range(-k):
            r = jnp.where(row == j, prev8[SUB + j + k:SUB + j + k + 1, :], r)
        return r
    r = pltpu.roll(tile, n - k, 0)
    for j in range(k):
        r = jnp.where(row == n - k + j, next8[j:j + 1, :], r)
    return r


def _to_lane_blocks(val, s_ref):
    for j in range(val.shape[1] // 128):
        s_ref[j] = val[:, j * 128:(j + 1) * 128]


def _from_lane_blocks(s_ref):
    return jnp.concatenate([s_ref[j] for j in range(s_ref.shape[0])], axis=-1)


def _class_rows(s_ref, r, dil):
    n = s_ref.shape[1] // dil
    return jnp.concatenate([s_ref[j, pl.ds(r, n, stride=dil), :] for j in range(s_ref.shape[0])], axis=-1)


def _split_classes(val, s_ref, out_ref, dil):
    _to_lane_blocks(val, s_ref)
    for r in range(dil):
        out_ref[r] = _class_rows(s_ref, r, dil).astype(out_ref.dtype)


def _merge_classes(in_ref, s_ref, dil):
    n = s_ref.shape[1] // dil
    for r in range(dil):
        v = in_ref[r]
        for j in range(s_ref.shape[0]):
            s_ref[j, pl.ds(r, n, stride=dil), :] = v[:, j * 128:(j + 1) * 128]
    return _from_lane_blocks(s_ref)


def _class_spec(dil, tile=TS):
    return pl.BlockSpec((dil, tile // dil, 512), lambda i: (0, i, 0))


def _class_shape(S, dil, dtype):
    return jax.ShapeDtypeStruct((dil, S // dil, 512), dtype)


def _scan_tile(a_ref, b_ref, h_ref, carry_ref, reverse):
    n = a_ref.shape[0]
    width = a_ref.shape[1]
    groups = n // SUB
    row = lax.broadcasted_iota(jnp.int32, (SUB, width), 0)

    def group_scan(g):
        r0 = pl.multiple_of(g * SUB, SUB)
        a = a_ref[pl.ds(r0, SUB), :]
        b = b_ref[pl.ds(r0, SUB), :]
        for s in (1, 2, 4):
            if reverse:
                a_sh = pltpu.roll(a, SUB - s, 0)
                b_sh = pltpu.roll(b, SUB - s, 0)
                m = row < SUB - s
            else:
                a_sh = pltpu.roll(a, s, 0)
                b_sh = pltpu.roll(b, s, 0)
                m = row >= s
            b = jnp.where(m, a * b_sh + b, b)
            a = jnp.where(m, a * a_sh, a)
        return r0, a, b

    def step(i, carry):
        first = i * SCAN_UNROLL
        order = [(groups - 1 - (first + u)) if reverse else (first + u) for u in range(SCAN_UNROLL)]
        scans = [group_scan(g) for g in order]
        for r0, a, b in scans:
            h = b + a * carry
            h_ref[pl.ds(r0, SUB), :] = h
            edge = h[0:1, :] if reverse else h[SUB - 1:SUB, :]
            carry = jnp.broadcast_to(edge, (SUB, width))
        return carry

    carry_ref[...] = lax.fori_loop(0, groups // SCAN_UNROLL, step, carry_ref[...])


def _conv_fwd(xr, prev8, next8, cw, cb):
    y = cb + _shifted(xr, prev8, next8, -2) * cw[0:1, :]
    y = y + _shifted(xr, prev8, next8, -1) * cw[1:2, :]
    y = y + xr * cw[2:3, :]
    y = y + _shifted(xr, prev8, next8, 1) * cw[3:4, :]
    return y


def _lru_gates(xc, wa_ref, ba, wx_ref, bx, lam):
    xcb = xc.astype(BF16)
    r = _sigmoid(_dot(xcb, wa_ref[...]) + ba)
    i = _sigmoid(_dot(xcb, wx_ref[...]) + bx)
    cl = -LRU_C * _softplus(-lam)
    la = cl * r
    a = jnp.exp(la)
    m2 = -jnp.tanh(la) * (a * a + 1.0)
    inv = lax.rsqrt(m2)
    mult = jnp.where(m2 > 0.0, m2 * inv, 0.0)
    return xcb, r, i, cl, a, mult, inv


def _inproj_fwd(x, g1, w_in, rider=None):
    S = x.shape[0]

    def body(x_ref, g_ref, w_ref, xr_ref, gate_ref, *rest):
        qkv_refs, s_ref = rest[:9], rest[9]
        _, xh = _rms(x_ref[...])
        h = (xh * g_ref[...]).astype(BF16)
        proj = jnp.concatenate([_dot(h, w_ref[j]) for j in range(N_SHARD)], axis=-1)
        xr_ref[...] = proj[:, 0:512]
        gate_ref[...] = proj[:, 512:1024]
        for t in range(3):
            val = proj[:, 1024 + 512 * t:1536 + 512 * t]
            qkv_refs[3 * t][0] = val.astype(BF16)
            _to_lane_blocks(val, s_ref)
            for p, dil in enumerate(DILATIONS[1:]):
                for r in range(dil):
                    qkv_refs[3 * t + 1 + p][r] = _class_rows(s_ref, r, dil).astype(BF16)

    f = jax.ShapeDtypeStruct((S, 512), F32)
    return _call(
        body, "inproj_fwd", (S // TS,),
        [_rows(D_MODEL), _whole_vmem(), _whole_vmem()],
        [_rows(512)] * 2 + [_class_spec(d) for d in DILATIONS] * 3,
        [f, f] + [_class_shape(S, d, BF16) for d in DILATIONS] * 3,
        [pltpu.VMEM((4, TS, 128), F32)], (x, g1, w_in), rider)


def _halo_specs(S, order, tile=TS):
    per = tile // SUB
    last = S // SUB - 1
    return [
        pl.BlockSpec((tile, 512), lambda i: (order(i), 0)),
        pl.BlockSpec((SUB, 512), lambda i: (jnp.maximum(order(i) * per - 1, 0), 0)),
        pl.BlockSpec((SUB, 512), lambda i: (jnp.minimum((order(i) + 1) * per, last), 0)),
    ]


def _rnn_fwd(xr, conv_w, conv_b, wa, ba, wx, bx, lam, reverse, rider=None):
    S = xr.shape[0]
    nt = S // TS
    order = (lambda i: nt - 1 - i) if reverse else (lambda i: i)

    def body(x_ref, xp_ref, xn_ref, cw_ref, cb_ref, wa_ref, ba_ref, wx_ref, bx_ref, lam_ref, h_ref, a_s, b_s, carry):
        i = pl.program_id(0)
        t = order(i)

        @pl.when(i == 0)
        def _():
            carry[...] = jnp.zeros_like(carry)

        prev8 = jnp.where(t > 0, xp_ref[...], 0.0)
        next8 = jnp.where(t < nt - 1, xn_ref[...], 0.0)
        xc = _conv_fwd(x_ref[...], prev8, next8, cw_ref[...], cb_ref[...])
        _, _, gi, _, a, mult, _ = _lru_gates(xc, wa_ref, ba_ref[...], wx_ref, bx_ref[...], lam_ref[...])
        a_s[...] = a
        b_s[...] = mult * (gi * xc)
        _scan_tile(a_s, b_s, h_ref, carry, reverse)

    (h,), carried = _call(
        body, "rnn_fwd_rev" if reverse else "rnn_fwd_fwd", (nt,),
        _halo_specs(S, order) + [_whole_vmem()] * 7,
        [pl.BlockSpec((TS, 512), lambda i: (order(i), 0))],
        [jax.ShapeDtypeStruct((S, 512), F32)],
        [pltpu.VMEM((TS, 512), F32), pltpu.VMEM((TS, 512), F32), pltpu.VMEM((SUB, 512), F32)],
        (xr, xr, xr, conv_w, conv_b, wa, ba, wx, bx, lam), rider)
    return h, carried


def _mix_fwd(o3, l3, hf, hb, gate, x, g_rnn, g_attn, w_out):
    S = x.shape[0]

    def body(o1, o2, o3_, l1, l2, l3_, hf_ref, hb_ref, gate_ref, x_ref, gr_ref, ga_ref, w_ref,
             x1_ref, mix_ref, ya1, ya2, ya3, ls1, ls2, ls3, s_ref):
        la, lb, lc = l1[0], _merge_classes(l2, s_ref, DILATIONS[1]), _merge_classes(l3_, s_ref, DILATIONS[2])
        m = jnp.maximum(jnp.maximum(la, lb), lc)
        ea, eb, ec = jnp.exp(la - m), jnp.exp(lb - m), jnp.exp(lc - m)
        den = ea + eb + ec
        lse = m + jnp.log(den)
        ya = (ea * o1[0] + eb * _merge_classes(o2, s_ref, DILATIONS[1]) + ec * _merge_classes(o3_, s_ref, DILATIONS[2])) / den
        ya1[0] = ya
        ls1[0] = lse
        _split_classes(ya, s_ref, ya2, DILATIONS[1])
        _split_classes(ya, s_ref, ya3, DILATIONS[2])
        _split_classes(lse, s_ref, ls2, DILATIONS[1])
        _split_classes(lse, s_ref, ls3, DILATIONS[2])
        gg, _ = _gelu_parts(gate_ref[...])
        yr = (hf_ref[...] + hb_ref[...]) * gg
        _, xh_r = _rms(yr)
        _, xh_a = _rms(ya)
        mix = jnp.concatenate([xh_r * gr_ref[...], xh_a * ga_ref[...]], axis=-1).astype(BF16)
        mix_ref[...] = mix
        acc = x_ref[...]
        for j in range(N_SHARD):
            acc = acc + _dot(mix[:, j * OUT_BLK:(j + 1) * OUT_BLK], w_ref[j])
        x1_ref[...] = acc

    cls = [_class_spec(d) for d in DILATIONS]
    return pl.pallas_call(
        body, grid=(S // TS,), name="mix_fwd",
        in_specs=cls * 2 + [_rows(512)] * 3 + [_rows(D_MODEL)] + [_whole_vmem()] * 3,
        out_specs=[_rows(D_MODEL), _rows(D_MODEL)] + cls * 2,
        out_shape=[jax.ShapeDtypeStruct((S, D_MODEL), F32), jax.ShapeDtypeStruct((S, D_MODEL), BF16)]
        + [_class_shape(S, d, F32) for d in DILATIONS] * 2,
        scratch_shapes=[pltpu.VMEM((4, TS, 128), F32)],
        compiler_params=_params(),
    )(*o3, *l3, hf, hb, gate, x, g_rnn, g_attn, w_out)


def _mlp_fwd_bwd(x1, target, g_mlp, g_fin, w_up, w_down):
    S = x1.shape[0]
    tm = TS_MLP

    def body(x1_ref, t_ref, gm_ref, gf_ref, wu_ref, wd_ref,
             dx1_ref, h2_ref, a2_ref, du_ref, dx2_ref, loss_ref, dgf_ref, dgm_ref, relu_s):
        @pl.when(pl.program_id(0) == 0)
        def _():
            loss_ref[...] = jnp.zeros_like(loss_ref)
            dgf_ref[...] = jnp.zeros_like(dgf_ref)
            dgm_ref[...] = jnp.zeros_like(dgm_ref)

        x1v = x1_ref[...]
        rstd1, xh1 = _rms(x1v)
        h2 = (xh1 * gm_ref[...]).astype(BF16)
        h2_ref[...] = h2
        x2 = x1v
        for j in range(N_SHARD):
            r = jnp.maximum(_dot(h2, wu_ref[j]), 0.0)
            relu_s[j] = r
            a2 = (r * r).astype(BF16)
            a2_ref[:, j * FF_BLK:(j + 1) * FF_BLK] = a2
            x2 = x2 + _dot(a2, wd_ref[j])
        rstd2, xh2 = _rms(x2)
        err = xh2 * gf_ref[...] - t_ref[...]
        loss_ref[...] += jnp.sum(err * err, axis=0, keepdims=True)
        dy = err * (1.0 / D_MODEL)
        dx2, dgf = _rms_bwd(dy, gf_ref[...], xh2, rstd2)
        dgf_ref[...] += dgf
        dx2b = dx2.astype(BF16)
        dx2_ref[...] = dx2b
        dh2 = jnp.zeros((tm, D_MODEL), F32)
        for j in range(N_SHARD):
            du = (_dot_nt(dx2b, wd_ref[j]) * (2.0 * relu_s[j])).astype(BF16)
            du_ref[:, j * FF_BLK:(j + 1) * FF_BLK] = du
            dh2 = dh2 + _dot_nt(du, wu_ref[j])
        dx1n, dgm = _rms_bwd(dh2, gm_ref[...], xh1, rstd1)
        dgm_ref[...] += dgm
        dx1_ref[...] = dx2 + dx1n

    vec = jax.ShapeDtypeStruct((1, D_MODEL), F32)
    return pl.pallas_call(
        body, grid=(S // tm,), name="mlp_fwd_bwd",
        in_specs=[_rows(D_MODEL, tm), _rows(D_MODEL, tm)] + [_whole_vmem()] * 4,
        out_specs=[_rows(D_MODEL, tm), _rows(D_MODEL, tm), _rows(D_FF, tm), _rows(D_FF, tm), _rows(D_MODEL, tm)]
        + [_whole_vmem()] * 3,
        out_shape=[jax.ShapeDtypeStruct((S, D_MODEL), F32), jax.ShapeDtypeStruct((S, D_MODEL), BF16),
                   jax.ShapeDtypeStruct((S, D_FF), BF16), jax.ShapeDtypeStruct((S, D_FF), BF16),
                   jax.ShapeDtypeStruct((S, D_MODEL), BF16), vec, vec, vec],
        scratch_shapes=[pltpu.VMEM((N_SHARD, tm, FF_BLK), F32)],
        compiler_params=_params(),
    )(x1, target, g_mlp, g_fin, w_up, w_down)


def _mix_bwd(dx1, w_out, mixb, ya, hf, hb, gate, g_rnn, g_attn):
    S = dx1.shape[0]

    def body(dx1_ref, w_ref, mix_ref, ya_ref, hf_ref, hb_ref, gate_ref, gr_ref, ga_ref,
             dhs_ref, dgate_ref, dya1, dya2, dya3, dw_ref, dgr_ref, dga_ref, s_ref):
        @pl.when(pl.program_id(0) == 0)
        def _():
            dw_ref[...] = jnp.zeros_like(dw_ref)
            dgr_ref[...] = jnp.zeros_like(dgr_ref)
            dga_ref[...] = jnp.zeros_like(dga_ref)

        dx1b = dx1_ref[...].astype(BF16)
        mix = mix_ref[...]
        for j in range(N_SHARD):
            dw_ref[j] += _dot_tn(mix[:, j * OUT_BLK:(j + 1) * OUT_BLK], dx1b)
        dmix = jnp.concatenate([_dot_nt(dx1b, w_ref[j]) for j in range(N_SHARD)], axis=-1)
        gg, dgg = _gelu_parts(gate_ref[...])
        hs = hf_ref[...] + hb_ref[...]
        rstd_r, xh_r = _rms(hs * gg)
        dyr, dgr = _rms_bwd(dmix[:, 0:D_RNN], gr_ref[...], xh_r, rstd_r)
        dgr_ref[...] += dgr
        rstd_a, xh_a = _rms(ya_ref[0])
        dya, dga = _rms_bwd(dmix[:, D_RNN:], ga_ref[...], xh_a, rstd_a)
        dga_ref[...] += dga
        dya1[0] = dya
        _split_classes(dya, s_ref, dya2, DILATIONS[1])
        _split_classes(dya, s_ref, dya3, DILATIONS[2])
        dhs_ref[...] = dyr * gg
        dgate_ref[...] = dyr * hs * dgg

    f512 = jax.ShapeDtypeStruct((S, 512), F32)
    vec = jax.ShapeDtypeStruct((1, 512), F32)
    return pl.pallas_call(
        body, grid=(S // TS,), name="mix_bwd",
        in_specs=[_rows(D_MODEL), _whole_vmem(), _rows(D_MODEL), _class_spec(1)] + [_rows(512)] * 3 + [_whole_vmem()] * 2,
        out_specs=[_rows(512)] * 2 + [_class_spec(d) for d in DILATIONS] + [_whole_vmem()] * 3,
        out_shape=[f512, f512] + [_class_shape(S, d, F32) for d in DILATIONS]
        + [jax.ShapeDtypeStruct((N_SHARD, OUT_BLK, D_MODEL), F32), vec, vec],
        scratch_shapes=[pltpu.VMEM((4, TS, 128), F32)],
        compiler_params=_params(),
    )(dx1, w_out, mixb, ya, hf, hb, gate, g_rnn, g_attn)


def _rnn_bwd(xr, h, dhs, conv_w, conv_b, wa, ba, wx, bx, lam, reverse):
    S = xr.shape[0]
    nt = S // TS
    order = (lambda i: i) if reverse else (lambda i: nt - 1 - i)
    per = TS // SUB
    last = S // SUB - 1
    if reverse:
        h_halo = pl.BlockSpec((SUB, 512), lambda i: (jnp.minimum((order(i) + 1) * per, last), 0))
    else:
        h_halo = pl.BlockSpec((SUB, 512), lambda i: (jnp.maximum(order(i) * per - 1, 0), 0))
    tile = pl.BlockSpec((TS, 512), lambda i: (order(i), 0))

    def body(x_ref, xp_ref, xn_ref, h_ref, hh_ref, dh_ref, cw_ref, cb_ref, wa_ref, ba_ref, wx_ref, bx_ref, lam_ref,
             dxc_ref, dwa_ref, dwx_ref, dvec_ref, a_s, g_s, carry, edge):
        i = pl.program_id(0)
        t = order(i)

        @pl.when(i == 0)
        def _():
            carry[...] = jnp.zeros_like(carry)
            edge[...] = jnp.zeros_like(edge)
            dwa_ref[...] = jnp.zeros_like(dwa_ref)
            dwx_ref[...] = jnp.zeros_like(dwx_ref)
            dvec_ref[...] = jnp.zeros_like(dvec_ref)

        prev8 = jnp.where(t > 0, xp_ref[...], 0.0)
        next8 = jnp.where(t < nt - 1, xn_ref[...], 0.0)
        xc = _conv_fwd(x_ref[...], prev8, next8, cw_ref[...], cb_ref[...])
        xcb, r, gi, cl, a, mult, inv_mult = _lru_gates(xc, wa_ref, ba_ref[...], wx_ref, bx_ref[...], lam_ref[...])
        hv = h_ref[...]
        if reverse:
            a_s[...] = _shifted(a, edge[...], None, -1)
            edge[...] = a[TS - SUB:TS, :]
            hh = jnp.where(t < nt - 1, hh_ref[...], 0.0)
            h_prev = _shifted(hv, None, hh, 1)
        else:
            a_s[...] = _shifted(a, None, edge[...], 1)
            edge[...] = a[0:SUB, :]
            hh = jnp.where(t > 0, hh_ref[...], 0.0)
            h_prev = _shifted(hv, hh, None, -1)
        _scan_tile(a_s, dh_ref, g_s, carry, not reverse)
        g = g_s[...]
        da = g * h_prev
        gm = g * mult
        d_i = gm * xc
        dmult = g * gi * xc
        dla = da * a - dmult * (a * a) * inv_mult
        d_r = dla * cl
        dpre_r = d_r * r * (1.0 - r)
        dpre_i = d_i * gi * (1.0 - gi)
        dprb = dpre_r.astype(BF16)
        dpib = dpre_i.astype(BF16)
        dwa_ref[...] += _dot_tn(xcb, dprb)
        dwx_ref[...] += _dot_tn(xcb, dpib)
        dvec_ref[0:1, :] += jnp.sum(dpre_r, axis=0, keepdims=True)
        dvec_ref[1:2, :] += jnp.sum(dpre_i, axis=0, keepdims=True)
        dvec_ref[2:3, :] += jnp.sum(dla * r, axis=0, keepdims=True)
        dvec_ref[3:4, :] = dvec_ref[2:3, :] * (LRU_C * _sigmoid(-lam_ref[...]))
        dxc_ref[...] = gm * gi + _dot_nt(dprb, wa_ref[...]) + _dot_nt(dpib, wx_ref[...])

    sq = jax.ShapeDtypeStruct((D_RNN, D_RNN), F32)
    return pl.pallas_call(
        body, grid=(nt,), name="rnn_bwd_rev" if reverse else "rnn_bwd_fwd",
        in_specs=_halo_specs(S, order) + [tile, h_halo, tile] + [_whole_vmem()] * 7,
        out_specs=[tile, _whole_vmem(), _whole_vmem(), _whole_vmem()],
        out_shape=[jax.ShapeDtypeStruct((S, 512), F32), sq, sq, jax.ShapeDtypeStruct((SUB, 512), F32)],
        scratch_shapes=[pltpu.VMEM((TS, 512), F32), pltpu.VMEM((TS, 512), F32), pltpu.VMEM((SUB, 512), F32),
                        pltpu.VMEM((SUB, 512), F32)],
        compiler_params=_params(),
    )(xr, xr, xr, h, h, dhs, conv_w, conv_b, wa, ba, wx, bx, lam)


def _inproj_bwd(x, dx1, xr, dxc_f, dxc_b, dgate, dq3, dk3, dv3, g1, conv_w, w_in, rider=None):
    S = x.shape[0]
    tb = TS_INPROJ_BWD
    nt = S // tb
    ident = lambda i: i

    def body(x_ref, dx1_ref, xr_ref, xrp_ref, xrn_ref, cf_ref, cfp_ref, cfn_ref, cb_ref, cbp_ref, cbn_ref, dgate_ref,
             dq1, dq2, dq3_, dk1, dk2, dk3_, dv1, dv2, dv3_, g_ref, cw_ref, w_ref,
             dx_ref, dw_ref, dg_ref, dcw_ref, s_ref):
        i = pl.program_id(0)

        @pl.when(i == 0)
        def _():
            dw_ref[...] = jnp.zeros_like(dw_ref)
            dg_ref[...] = jnp.zeros_like(dg_ref)
            dcw_ref[...] = jnp.zeros_like(dcw_ref)

        first, last = i > 0, i < nt - 1
        dxc = cf_ref[...] + cb_ref[...]
        dxc_p = jnp.where(first, cfp_ref[...] + cbp_ref[...], 0.0)
        dxc_n = jnp.where(last, cfn_ref[...] + cbn_ref[...], 0.0)
        cw = cw_ref[...]
        dxr = (_shifted(dxc, dxc_p, dxc_n, 2) * cw[0:1, :] + _shifted(dxc, dxc_p, dxc_n, 1) * cw[1:2, :]
               + dxc * cw[2:3, :] + _shifted(dxc, dxc_p, dxc_n, -1) * cw[3:4, :])
        xrv = xr_ref[...]
        xr_p = jnp.where(first, xrp_ref[...], 0.0)
        xr_n = jnp.where(last, xrn_ref[...], 0.0)
        for k, off in enumerate((-2, -1, 0, 1)):
            dcw_ref[k:k + 1, :] += jnp.sum(dxc * _shifted(xrv, xr_p, xr_n, off), axis=0, keepdims=True)
        dcw_ref[4:5, :] += jnp.sum(dxc, axis=0, keepdims=True)

        def total(a, b, c_):
            return a[0] + _merge_classes(b, s_ref, DILATIONS[1]) + _merge_classes(c_, s_ref, DILATIONS[2])

        dproj = jnp.concatenate(
            [dxr, dgate_ref[...], total(dq1, dq2, dq3_), total(dk1, dk2, dk3_), total(dv1, dv2, dv3_)],
            axis=-1).astype(BF16)
        xv = x_ref[...]
        rstd, xh = _rms(xv)
        hb = (xh * g_ref[...]).astype(BF16)
        dh = jnp.zeros((tb, D_MODEL), F32)
        for j in range(N_SHARD):
            dpj = dproj[:, j * IN_BLK:(j + 1) * IN_BLK]
            dh = dh + _dot_nt(dpj, w_ref[j])
            dw_ref[j] += _dot_tn(hb, dpj)
        dxn, dg = _rms_bwd(dh, g_ref[...], xh, rstd)
        dg_ref[...] += dg
        dx_ref[...] = dx1_ref[...] + dxn

    halo = _halo_specs(S, ident, tb)
    return _call(
        body, "inproj_bwd", (nt,),
        [_rows(D_MODEL, tb), _rows(D_MODEL, tb)] + halo * 3 + [_rows(512, tb)]
        + [_class_spec(d, tb) for d in DILATIONS] * 3 + [_whole_vmem()] * 3,
        [_rows(D_MODEL, tb), _whole_vmem(), _whole_vmem(), _whole_vmem()],
        [jax.ShapeDtypeStruct((S, D_MODEL), F32), jax.ShapeDtypeStruct((N_SHARD, D_MODEL, IN_BLK), F32),
         jax.ShapeDtypeStruct((1, D_MODEL), F32), jax.ShapeDtypeStruct((SUB, 512), F32)],
        [pltpu.VMEM((4, tb, 128), F32)],
        (x, dx1, xr, xr, xr, dxc_f, dxc_f, dxc_f, dxc_b, dxc_b, dxc_b, dgate, *dq3, *dk3, *dv3, g1, conv_w, w_in), rider)


def _dw_matmul(a, b, a_cols, b_cols, name):
    S = a.shape[0]
    tk = 1024
    a_shared = a.shape[1] == a_cols
    b_shared = b.shape[1] == b_cols

    def body(a_ref, b_ref, o_ref):
        @pl.when(pl.program_id(1) == 0)
        def _():
            o_ref[...] = jnp.zeros_like(o_ref)
        o_ref[0] += _dot_tn(a_ref[...], b_ref[...])

    return pl.pallas_call(
        body, grid=(N_SHARD, S // tk), name=name,
        in_specs=[pl.BlockSpec((tk, a_cols), (lambda j, k: (k, 0)) if a_shared else (lambda j, k: (k, j))),
                  pl.BlockSpec((tk, b_cols), (lambda j, k: (k, 0)) if b_shared else (lambda j, k: (k, j)))],
        out_specs=pl.BlockSpec((1, a_cols, b_cols), lambda j, k: (j, 0, 0)),
        out_shape=jax.ShapeDtypeStruct((N_SHARD, a_cols, b_cols), F32),
        compiler_params=_params(2),
    )(a, b)


def _t5_bucket_np(rel):
    nb = N_BUCKETS // 2
    max_exact = nb // 2
    ret = np.where(rel > 0, nb, 0)
    n = np.abs(rel)
    nf = np.maximum(n, 1).astype(np.float32)
    large = max_exact + (np.log(nf / np.float32(max_exact)) / np.float32(math.log(MAX_DISTANCE / max_exact))
                         * np.float32(nb - max_exact)).astype(np.int32)
    large = np.minimum(large, nb - 1)
    return ret + np.where(n < max_exact, n, large)


_VARIANT_OFFSETS = (-HALF_WIN,) * 3


def _band_index():
    kk = np.arange(K_WIN)[None, :]
    ql = np.arange(Q_BLK)[:, None]
    rel = np.stack([kk - ql + off for off in _VARIANT_OFFSETS])
    band = np.abs(rel) <= HALF_WIN
    inside = np.stack([np.broadcast_to(kk >= HALF_WIN, band[0].shape), np.ones_like(band[0]),
                       np.broadcast_to(kk < K_WIN - HALF_WIN, band[0].shape)])
    return rel, band & inside


def _bucket_tables(dil):
    rel, valid = _band_index()
    bucket = _t5_bucket_np(np.clip(rel, -HALF_WIN, HALF_WIN) * dil)
    return np.where(valid, bucket, -1).astype(np.int32)


def _bias_mats(rel_bias):
    tables = [_bucket_tables(d) for d in DILATIONS]
    used = [sorted(set(t[t >= 0].tolist())) for t in tables]

    def body(rb_ref, t1, t2, t3, o1, o2, o3):
        for t_ref, o_ref, buckets in ((t1, o1, used[0]), (t2, o2, used[1]), (t3, o3, used[2])):
            for var in range(3):
                bk = t_ref[var]
                for h in range(N_HEADS):
                    acc = jnp.full((Q_BLK, K_WIN), NEG_INF, F32)
                    for b in buckets:
                        acc = jnp.where(bk == b, rb_ref[b, h], acc)
                    o_ref[var, h] = acc

    shp = jax.ShapeDtypeStruct((3, N_HEADS, Q_BLK, K_WIN), F32)
    return pl.pallas_call(
        body, name="bias_tables", in_specs=[pl.BlockSpec(memory_space=pltpu.SMEM)] + [_whole_vmem()] * 3,
        out_shape=[shp] * 3, compiler_params=_params(0),
    )(rel_bias, *[jnp.asarray(t) for t in tables])


def _variant(qb, nq):
    return jnp.where(qb == 0, 0, jnp.where(qb == nq - 1, 2, 1))


def _win_start(qb):
    return pl.multiple_of(qb * Q_BLK, Q_BLK)


def _fill_padded(src_ref, pad_ref):
    L = src_ref.shape[0]
    edge = jnp.zeros((HALF_WIN, 128), pad_ref.dtype)
    pad_ref[0:HALF_WIN, :] = edge
    pad_ref[HALF_WIN:HALF_WIN + L, :] = src_ref[...]
    pad_ref[HALF_WIN + L:2 * HALF_WIN + L, :] = edge


def _attn_specs(L):
    nsub = min(ATTN_SUB, L // Q_BLK)
    qt = nsub * Q_BLK
    qspec = pl.BlockSpec((None, qt, 128), lambda c, s: (c // 4, s, c % 4))
    kspec = pl.BlockSpec((None, L, 128), lambda c, s: (c // 4, 0, c % 4))
    bspec = pl.BlockSpec((3, 2, Q_BLK, K_WIN), lambda c, s: (0, c % 4, 0, 0))
    return nsub, qspec, kspec, bspec


def _head_masks():
    lane = lax.broadcasted_iota(jnp.int32, (Q_BLK, 128), 1)
    return lane < HEAD_DIM


def _attn_fwd(q, k, v, bias):
    dil, L, _ = q.shape
    nq = L // Q_BLK
    nsub, qspec, kspec, bspec = _attn_specs(L)

    def body(q_ref, k_ref, v_ref, b_ref, o_ref, l_ref, kp, vp):
        @pl.when(pl.program_id(1) == 0)
        def _():
            _fill_padded(k_ref, kp)
            _fill_padded(v_ref, vp)

        h0 = _head_masks()
        for sub in range(nsub):
            qb = pl.program_id(1) * nsub + sub
            rows = slice(sub * Q_BLK, (sub + 1) * Q_BLK)
            st = _win_start(qb)
            var = _variant(qb, nq)
            kw = kp[pl.ds(st, K_WIN), :]
            vw = vp[pl.ds(st, K_WIN), :]
            qs = q_ref[rows, :] * ATTN_SCALE
            outs, lses = [], []
            for h in range(2):
                qh = jnp.where(h0 if h == 0 else ~h0, qs, jnp.zeros_like(qs))
                s = _dot_nt(qh, kw) + b_ref[var, h]
                m = jnp.max(s, axis=-1, keepdims=True)
                p = jnp.exp(s - m)
                l = jnp.sum(p, axis=-1, keepdims=True)
                outs.append(_dot(p.astype(BF16), vw) / l)
                lses.append(m + jnp.log(l))
            o_ref[rows, :] = jnp.where(h0, outs[0], outs[1])
            l_ref[rows, :] = jnp.where(h0, lses[0], lses[1])

    shp = jax.ShapeDtypeStruct((dil, L, D_ATTN), F32)
    return pl.pallas_call(
        body, grid=(dil * 4, nq // nsub), name=f"attn_fwd_d{dil}",
        in_specs=[qspec, kspec, kspec, bspec], out_specs=[qspec, qspec], out_shape=[shp, shp],
        scratch_shapes=[pltpu.VMEM((L + 2 * HALF_WIN, 128), BF16)] * 2,
        compiler_params=_params(2),
    )(q, k, v, bias)


def _attn_bwd(q, k, v, bias, do, o, lse, rider=None):
    dil, L, _ = q.shape
    nq = L // Q_BLK
    nsub, qspec, kspec, bspec = _attn_specs(L)
    ncol = dil * 4
    nstep = nq // nsub

    def body(q_ref, k_ref, v_ref, b_ref, do_ref, o_ref, l_ref, dq_ref, dk_ref, dv_ref, db_ref, db_s,
             kp, vp, dkp, dvp, carry):
        c, step = pl.program_id(0), pl.program_id(1)

        @pl.when((c == 0) & (step == 0))
        def _():
            db_s[...] = jnp.zeros_like(db_s)

        @pl.when(step == 0)
        def _():
            _fill_padded(k_ref, kp)
            _fill_padded(v_ref, vp)
            carry[...] = jnp.zeros_like(carry)

        carry_k, carry_v = carry[0], carry[1]
        h0 = _head_masks()
        for sub in range(nsub):
            qb = step * nsub + sub
            rows = slice(sub * Q_BLK, (sub + 1) * Q_BLK)
            st = _win_start(qb)
            var = _variant(qb, nq)
            kw = kp[pl.ds(st, K_WIN), :]
            vw = vp[pl.ds(st, K_WIN), :]
            qs = q_ref[rows, :] * ATTN_SCALE
            dof = do_ref[rows, :]
            dob = dof.astype(BF16)
            prod = dof * o_ref[rows, :]
            lsev = l_ref[rows, :]
            dk_acc = jnp.zeros((K_WIN, 128), F32)
            dv_acc = jnp.zeros((K_WIN, 128), F32)
            dqs = []
            for h in range(2):
                hm = h0 if h == 0 else ~h0
                qh = jnp.where(hm, qs, jnp.zeros_like(qs))
                doh = jnp.where(hm, dob, jnp.zeros_like(dob))
                s = _dot_nt(qh, kw) + b_ref[var, h]
                p = jnp.exp(s - lsev[:, h * HEAD_DIM:h * HEAD_DIM + 1])
                dp = _dot_nt(doh, vw)
                dd = jnp.sum(jnp.where(hm, prod, 0.0), axis=-1, keepdims=True)
                ds = p * (dp - dd)
                db_s[var, (c % 4) * 2 + h] += ds
                dsb = ds.astype(BF16)
                dv_acc = dv_acc + _dot_tn(p.astype(BF16), doh)
                dk_acc = dk_acc + _dot_tn(dsb, qh)
                dqs.append(_dot(dsb, kw) * ATTN_SCALE)
            dq_ref[rows, :] = jnp.where(h0, dqs[0], dqs[1])
            dkp[pl.ds(st, Q_BLK), :] = carry_k + dk_acc[0:Q_BLK]
            dvp[pl.ds(st, Q_BLK), :] = carry_v + dv_acc[0:Q_BLK]
            carry_k, carry_v = dk_acc[Q_BLK:K_WIN], dv_acc[Q_BLK:K_WIN]
        carry[0] = carry_k
        carry[1] = carry_v

        @pl.when(step == nstep - 1)
        def _():
            dkp[L:L + Q_BLK, :] = carry_k
            dvp[L:L + Q_BLK, :] = carry_v
            dk_ref[...] = dkp[HALF_WIN:HALF_WIN + L, :]
            dv_ref[...] = dvp[HALF_WIN:HALF_WIN + L, :]

        @pl.when((c == ncol - 1) & (step == nstep - 1))
        def _():
            db_ref[...] = db_s[...]

    shp = jax.ShapeDtypeStruct((dil, L, D_ATTN), F32)
    dbshape = (3, N_HEADS, Q_BLK, K_WIN)
    return _call(
        body, f"attn_bwd_d{dil}", (ncol, nstep),
        [qspec, kspec, kspec, bspec, qspec, qspec, qspec],
        [qspec, kspec, kspec, _whole_vmem()],
        [shp, shp, shp, jax.ShapeDtypeStruct(dbshape, F32)],
        [pltpu.VMEM(dbshape, F32)] + [pltpu.VMEM((L + 2 * HALF_WIN, 128), BF16)] * 2
        + [pltpu.VMEM((L + 2 * HALF_WIN, 128), F32)] * 2 + [pltpu.VMEM((2, Q_BLK, 128), F32)],
        (q, k, v, bias, do, o, lse), rider)


def _bucket_onehots(dil):
    m = np.zeros((3, K_WIN, N_BUCKETS), np.float32)
    for var, off in enumerate(_VARIANT_OFFSETS):
        for rel in range(-HALF_WIN, HALF_WIN + 1):
            col = (rel - off + Q_BLK - 1) % K_WIN
            m[var, col, int(_t5_bucket_np(np.asarray(rel * dil)))] = 1.0
    return jnp.asarray(m)


def _bias_grad(dbs):
    onehots = [_bucket_onehots(d) for d in DILATIONS]
    flip = jnp.asarray(np.eye(Q_BLK, dtype=np.float32)[::-1].copy())

    def body(d1, d2, d3, m1, m2, m3, flip_ref, out_ref):
        hp = lax.Precision.HIGHEST
        acc = jnp.zeros((N_HEADS, N_BUCKETS), F32)
        for d_ref, m_ref in ((d1, m1), (d2, m2), (d3, m3)):
            for var in range(3):
                rows = []
                for h in range(N_HEADS):
                    xrev = jnp.dot(flip_ref[...], d_ref[var, h], precision=hp, preferred_element_type=F32)
                    y = pltpu.roll(xrev, 0, 1, stride=1, stride_axis=0)
                    rows.append(jnp.sum(y, axis=0, keepdims=True))
                vec = jnp.concatenate(rows, axis=0)
                acc = acc + jnp.dot(vec, m_ref[var], precision=hp, preferred_element_type=F32)
        out_ref[...] = acc

    return pl.pallas_call(
        body, name="bias_grad", out_shape=jax.ShapeDtypeStruct((N_HEADS, N_BUCKETS), F32),
        compiler_params=_params(0),
    )(*dbs, *onehots, flip)


def _block_diag(w):
    eye = jnp.eye(N_RNN_BLOCKS, dtype=w.dtype)
    return jnp.einsum("ncd,nm->ncmd", w, eye).reshape(D_RNN, D_RNN).astype(BF16)


def _diag_blocks(dense):
    d = dense.reshape(N_RNN_BLOCKS, RNN_BLOCK, N_RNN_BLOCKS, RNN_BLOCK)
    return jnp.stack([d[n, :, n, :] for n in range(N_RNN_BLOCKS)])


EARLY = ("w_out", "w_up", "w_down")


def _local_step(x, target, p, shards=None):
    p = dict(p)
    biases = _bias_mats(p["rel_bias"])
    lru = {}
    for dname in ("fwd", "bwd"):
        lru[dname] = (_block_diag(p["lru_wa_" + dname]), p["lru_ba_" + dname], _block_diag(p["lru_wx_" + dname]),
                      p["lru_bx_" + dname], p["lru_lam_" + dname])

    def gather(name):
        return None if shards is None else _gather_rider([name], [shards[name]])

    (xr, gate, *qkv), got = _inproj_fwd(x, p["attn_norm_g"], p["w_in"], gather("w_out"))
    p.update(zip(["w_out"], got))
    qs, ks, vs = qkv[0:3], qkv[3:6], qkv[6:9]
    hf, got = _rnn_fwd(xr, p["conv_w"], p["conv_b"], *lru["fwd"], reverse=False, rider=gather("w_up"))
    p.update(zip(["w_up"], got))
    hb, got = _rnn_fwd(xr, p["conv_w"], p["conv_b"], *lru["bwd"], reverse=True, rider=gather("w_down"))
    p.update(zip(["w_down"], got))
    outs, lses = [], []
    for q, k, v, bias in zip(qs, ks, vs, biases):
        o, l = _attn_fwd(q, k, v, bias)
        outs.append(o)
        lses.append(l)
    x1, mixb, *yl = _mix_fwd(outs, lses, hf, hb, gate, x, p["norm_rnn_g"], p["norm_attn_g"], p["w_out"])
    yas, lsts = yl[0:3], yl[3:6]
    dx1, h2b, a2b, dub, dx2b, loss_vec, dg_fin, dg_mlp = _mlp_fwd_bwd(
        x1, target, p["mlp_norm_g"], p["final_norm_g"], p["w_up"], p["w_down"])
    dhs, dgate, *dyas, dw_out, dg_rnn, dg_attn = _mix_bwd(dx1, p["w_out"], mixb, yas[0], hf, hb, gate,
                                                          p["norm_rnn_g"], p["norm_attn_g"])
    dw_up = _dw_matmul(h2b, dub, D_MODEL, FF_BLK, "dw_up")
    dw_down = _dw_matmul(a2b, dx2b, FF_BLK, D_MODEL, "dw_down")
    early = [dw_out, dw_up, dw_down]
    dqs, dks, dvs, dbs = [], [], [], []
    for i, (q, k, v, bias, dya, ya, lse) in enumerate(zip(qs, ks, vs, biases, dyas, yas, lsts)):
        rider = None
        if shards is not None:
            make = (lambda: _pair_exchange_rider(EARLY, early), lambda: _chip_exchange_rider(early),
                    lambda: _pair_share_rider(EARLY, early))[i]
            rider = make()
        (dq, dk, dv, db), got = _attn_bwd(q, k, v, bias, dya, ya, lse, rider)
        if shards is not None and i == 0:
            core = lax.axis_index("c").reshape(1).astype(jnp.int32)
            early = [_pair_add(core, g, o, "grad_pair_add_" + n) for n, g, o in zip(EARLY, early, got)]
        elif shards is not None and i == 1:
            early = [_chip_sum(t, "grad_chip_sum_" + n) for n, t in zip(EARLY, got)]
        elif shards is not None:
            early = got
        dqs.append(dq)
        dks.append(dk)
        dvs.append(dv)
        dbs.append(db)
    d_rel_bias = _bias_grad(dbs).T
    dxc_f, dwa_f, dwx_f, dvec_f = _rnn_bwd(xr, hf, dhs, p["conv_w"], p["conv_b"], *lru["fwd"], reverse=False)
    dxc_b, dwa_b, dwx_b, dvec_b = _rnn_bwd(xr, hb, dhs, p["conv_w"], p["conv_b"], *lru["bwd"], reverse=True)
    small = {
        "lru_wa_fwd": _diag_blocks(dwa_f), "lru_ba_fwd": dvec_f[0:1], "lru_wx_fwd": _diag_blocks(dwx_f),
        "lru_bx_fwd": dvec_f[1:2], "lru_lam_fwd": dvec_f[3:4],
        "lru_wa_bwd": _diag_blocks(dwa_b), "lru_ba_bwd": dvec_b[0:1], "lru_wx_bwd": _diag_blocks(dwx_b),
        "lru_bx_bwd": dvec_b[1:2], "lru_lam_bwd": dvec_b[3:4],
        "rel_bias": d_rel_bias, "norm_rnn_g": dg_rnn, "norm_attn_g": dg_attn,
        "mlp_norm_g": dg_mlp, "final_norm_g": dg_fin,
    }
    loss_local = (0.5 / D_MODEL) * jnp.sum(loss_vec)
    rider = None
    if shards is not None:
        rider = _small_gather_rider(_pack([small[n].reshape(shp) for n, shp in SMALL if n in small]
                                          + [loss_local.reshape(1)]))
    (grad_x, dw_in, dg1, dconv), gathered = _inproj_bwd(x, dx1, xr, dxc_f, dxc_b, dgate, dqs, dks, dvs,
                                                        p["attn_norm_g"], p["conv_w"], p["w_in"], rider)
    last = {"attn_norm_g": dg1, "conv_w": dconv[0:4], "conv_b": dconv[4:5]}
    if shards is None:
        big = {"w_in": dw_in, "w_out": dw_out, "w_up": dw_up, "w_down": dw_down}
        return loss_local, grad_x, {**small, **last}, None, big, {}
    return loss_local, grad_x, last, gathered[0], {"w_in": dw_in}, dict(zip(EARLY, early))


BIG = ("w_in", "w_out", "w_up", "w_down")
BIG_SHARD = {"w_in": (D_MODEL, IN_BLK), "w_out": (OUT_BLK, D_MODEL), "w_up": (D_MODEL, FF_BLK), "w_down": (FF_BLK, D_MODEL)}
N_BIG = len(BIG)
N_CHIP_PEERS = 3
ANY = pl.BlockSpec(memory_space=pl.ANY)


def _place():
    x, y, c = lax.axis_index("x"), lax.axis_index("y"), lax.axis_index("c")
    chips = [(1 - x, y), (x, 1 - y), (1 - x, 1 - y)]
    return x, y, c, chips


def _remote(src, dst, send_sem, recv_sem, dev):
    return pltpu.make_async_remote_copy(src_ref=src, dst_ref=dst, send_sem=send_sem, recv_sem=recv_sem,
                                        device_id=dev, device_id_type=MESH)


def _staged_start(srcs, bufs, sems):
    legs = [pltpu.make_async_copy(s, b, sems.at[i]) for i, (s, b) in enumerate(zip(srcs, bufs))]
    for cp in legs:
        cp.start()
    return legs


def _staged_finish(legs, bufs, dsts, sems):
    out = []
    for i, (leg, b, d) in enumerate(zip(legs, bufs, dsts)):
        leg.wait()
        cp = pltpu.make_async_copy(b, d, sems.at[i])
        cp.start()
        out.append(cp)
    return out


class _Rider:
    def __init__(self, inputs, out_shape, scratch, first, late, last):
        self.inputs, self.out_shape, self.scratch = list(inputs), list(out_shape), list(scratch)
        self.first, self.late, self.last = first, late, last


def _call(body, name, grid, in_specs, out_specs, out_shape, scratch, operands, rider=None):
    n_grid = len(grid)
    if rider is None:
        res = pl.pallas_call(body, grid=grid, name=name, in_specs=in_specs, out_specs=out_specs, out_shape=out_shape,
                             scratch_shapes=scratch, compiler_params=_params(n_grid))(*operands)
        return list(res), []
    n_in, n_out, n_scr = len(in_specs), len(out_specs), len(scratch)
    ri, ro = len(rider.inputs), len(rider.out_shape)
    nsteps = int(np.prod(grid))
    late_step = max(nsteps - 3, 1)

    def wrapped(*refs):
        a, b = n_in, n_in + ri
        c, d = b + n_out, b + n_out + ro
        e = d + n_scr
        mine = refs[:a] + refs[b:c] + refs[d:e]
        theirs = (refs[a:b], refs[c:d], refs[e:])
        step = pl.program_id(0)
        for ax in range(1, n_grid):
            step = step * grid[ax] + pl.program_id(ax)
        pl.when(step == 0)(lambda: rider.first(*theirs))
        pl.when(step == late_step)(lambda: rider.late(*theirs))
        body(*mine)
        pl.when(step == nsteps - 1)(lambda: rider.last(*theirs))

    res = pl.pallas_call(
        wrapped, grid=grid, name=name, in_specs=list(in_specs) + [ANY] * ri, out_specs=list(out_specs) + [ANY] * ro,
        out_shape=list(out_shape) + rider.out_shape, scratch_shapes=list(scratch) + rider.scratch,
        compiler_params=_params(n_grid),
    )(*operands, *rider.inputs)
    return list(res[:n_out]), list(res[n_out:])


def _run_rider(rider, name):
    ri, ro = len(rider.inputs), len(rider.out_shape)

    def body(*refs):
        parts = (refs[:ri], refs[ri:ri + ro], refs[ri + ro:])
        rider.first(*parts)
        rider.late(*parts)
        rider.last(*parts)

    return list(pl.pallas_call(
        body, name=name, in_specs=[ANY] * ri, out_specs=[ANY] * ro, out_shape=rider.out_shape, scratch_shapes=rider.scratch,
        compiler_params=pltpu.CompilerParams(has_side_effects=True, vmem_limit_bytes=VMEM_LIMIT),
    )(*rider.inputs))


def _nothing(ins, outs, scr):
    return None


def _gather_rider(names, shards, conv_w=None):
    n = len(names)
    items = n + (conv_w is not None)
    halves = [BIG_SHARD[nm][0] // 2 for nm in names]

    def parts(ins, outs, scr):
        x, y, c, chips = _place()
        return x, y, c, chips, 2 * x + y, (x, y, 1 - c), scr[:8], scr[8:]

    def piece(outs, w, chip, core_half):
        return outs[w].at[chip, pl.ds(core_half * halves[w], halves[w])]

    def ici(ins, outs, sems, w, k, chip_xy, c, me):
        return _remote(ins[w].at[pl.ds(c * halves[w], halves[w])], piece(outs, w, me, c),
                       sems[0].at[w, k], sems[1].at[w, k], (*chip_xy, c))

    def first(ins, outs, scr):
        x, y, c, chips, me, sibling, sems, bufs = parts(ins, outs, scr)
        legs = _staged_start(ins, bufs, sems[6])
        for w in range(n):
            for k, chip_xy in enumerate(chips):
                ici(ins, outs, sems, w, k, chip_xy, c, me).start()
        if conv_w is not None:
            for k, (px, py) in enumerate(chips):
                _remote(ins[n], outs[n].at[me], sems[4].at[k], sems[5].at[k], (px, py, c)).start()
        _staged_finish(legs, bufs, [o.at[me] for o in outs], sems[7])

    def late(ins, outs, scr):
        x, y, c, chips, me, sibling, sems, bufs = parts(ins, outs, scr)
        for w in range(n):
            for k, (px, py) in enumerate(chips):
                landed = piece(outs, w, 2 * px + py, c)
                _remote(landed, landed, sems[0].at[w, k], sems[1].at[w, k], (px, py, c)).wait_recv()
                _remote(landed, landed, sems[2].at[w, k], sems[3].at[w, k], sibling).start()

    def last(ins, outs, scr):
        x, y, c, chips, me, sibling, sems, bufs = parts(ins, outs, scr)
        for w in range(n):
            for k, (px, py) in enumerate(chips):
                other = piece(outs, w, 2 * px + py, 1 - c)
                _remote(other, other, sems[2].at[w, k], sems[3].at[w, k], sibling).wait_recv()
        if conv_w is not None:
            for k, (px, py) in enumerate(chips):
                got = outs[n].at[2 * px + py]
                _remote(got, got, sems[4].at[k], sems[5].at[k], (px, py, c)).wait_recv()
                _remote(ins[n], outs[n].at[me], sems[4].at[k], sems[5].at[k], (px, py, c)).wait_send()
        for i in range(items):
            pltpu.make_async_copy(bufs[i], outs[i].at[me], sems[7].at[i]).wait()
        for w in range(n):
            for k, (px, py) in enumerate(chips):
                ici(ins, outs, sems, w, k, (px, py), c, me).wait_send()
                landed = piece(outs, w, 2 * px + py, c)
                _remote(landed, landed, sems[2].at[w, k], sems[3].at[w, k], sibling).wait_send()

    out_shape = [jax.ShapeDtypeStruct((N_SHARD,) + BIG_SHARD[nm], BF16) for nm in names]
    stage = [pltpu.VMEM(BIG_SHARD[nm], BF16) for nm in names]
    inputs = list(shards)
    if conv_w is not None:
        out_shape.append(jax.ShapeDtypeStruct((N_SHARD,) + conv_w.shape, F32))
        stage.append(pltpu.VMEM(conv_w.shape, F32))
        inputs.append(conv_w)
    scratch = ([pltpu.SemaphoreType.DMA((n, N_CHIP_PEERS))] * 4 + [pltpu.SemaphoreType.DMA((N_CHIP_PEERS,))] * 2
               + [pltpu.SemaphoreType.DMA((items,))] * 2 + stage)
    return _Rider(inputs, out_shape, scratch, first, late, last)


def _pair_exchange_rider(names, grads):
    def copies(ins, outs, scr):
        x, y, c, _ = _place()
        out = []
        for w, nm in enumerate(names):
            h = BIG_SHARD[nm][0] // 2
            out.append(_remote(ins[w].at[:, pl.ds((1 - c) * h, h), :], outs[w], scr[0].at[w], scr[1].at[w], (x, y, 1 - c)))
        return out

    def first(ins, outs, scr):
        for cp in copies(ins, outs, scr):
            cp.start()

    def last(ins, outs, scr):
        for cp in copies(ins, outs, scr):
            cp.wait()

    out_shape = [jax.ShapeDtypeStruct((N_SHARD, BIG_SHARD[nm][0] // 2, BIG_SHARD[nm][1]), F32) for nm in names]
    return _Rider(grads, out_shape, [pltpu.SemaphoreType.DMA((len(names),))] * 2, first, _nothing, last)


def _pair_add(core, grad, other, name):
    _, r, cols = grad.shape
    h = r // 2
    th = min(h, 256)
    per = h // th

    def body(c_ref, g_ref, o_ref, out_ref):
        out_ref[...] = (g_ref[...] + o_ref[...]).astype(BF16)

    return pl.pallas_call(
        body, name=name,
        grid_spec=pltpu.PrefetchScalarGridSpec(
            num_scalar_prefetch=1, grid=(N_SHARD, per),
            in_specs=[pl.BlockSpec((1, th, cols), lambda j, i, c_ref: (j, c_ref[0] * per + i, 0)),
                      pl.BlockSpec((1, th, cols), lambda j, i, c_ref: (j, i, 0))],
            out_specs=pl.BlockSpec((1, th, cols), lambda j, i, c_ref: (j, i, 0))),
        out_shape=jax.ShapeDtypeStruct((N_SHARD, h, cols), BF16),
        compiler_params=_params(2),
    )(core, grad, other)


def _chip_exchange_rider(parts):
    n = len(parts)

    def sends(ins, outs, scr):
        x, y, c, chips = _place()
        me = 2 * x + y
        return [_remote(ins[w].at[2 * px + py], outs[w].at[me], scr[0].at[w, k], scr[1].at[w, k], (px, py, c))
                for w in range(n) for k, (px, py) in enumerate(chips)]

    def first(ins, outs, scr):
        x, y, c, chips = _place()
        me = 2 * x + y
        legs = _staged_start([r.at[me] for r in ins], scr[4:], scr[2])
        for cp in sends(ins, outs, scr):
            cp.start()
        _staged_finish(legs, scr[4:], [o.at[me] for o in outs], scr[3])

    def last(ins, outs, scr):
        x, y, c, chips = _place()
        me = 2 * x + y
        for w in range(n):
            for k, (px, py) in enumerate(chips):
                got = outs[w].at[2 * px + py]
                _remote(got, got, scr[0].at[w, k], scr[1].at[w, k], (px, py, c)).wait_recv()
        for cp in sends(ins, outs, scr):
            cp.wait_send()
        for w in range(n):
            pltpu.make_async_copy(scr[4 + w], outs[w].at[me], scr[3].at[w]).wait()

    out_shape = [jax.ShapeDtypeStruct(p.shape, BF16) for p in parts]
    scratch = ([pltpu.SemaphoreType.DMA((n, N_CHIP_PEERS))] * 2 + [pltpu.SemaphoreType.DMA((n,))] * 2
               + [pltpu.VMEM(p.shape[1:], BF16) for p in parts])
    return _Rider(parts, out_shape, scratch, first, _nothing, last)


def _chip_sum(parts, name):
    _, h, cols = parts.shape
    th = min(h, 256)

    def body(p_ref, out_ref):
        acc = p_ref[0].astype(F32)
        for j in range(1, N_SHARD):
            acc = acc + p_ref[j].astype(F32)
        out_ref[...] = acc

    return pl.pallas_call(
        body, name=name, grid=(h // th,),
        in_specs=[pl.BlockSpec((N_SHARD, th, cols), lambda i: (0, i, 0))],
        out_specs=pl.BlockSpec((th, cols), lambda i: (i, 0)),
        out_shape=jax.ShapeDtypeStruct((h, cols), F32),
        compiler_params=_params(),
    )(parts)


def _pair_share_rider(names, halves):
    n = len(names)
    hs = [BIG_SHARD[nm][0] // 2 for nm in names]

    def mine(outs, c):
        return [outs[w].at[pl.ds(c * hs[w], hs[w])] for w in range(n)]

    def first(ins, outs, scr):
        x, y, c, _ = _place()
        legs = _staged_start(ins, scr[4:], scr[2])
        for w, dst in enumerate(mine(outs, c)):
            _remote(ins[w], dst, scr[0].at[w], scr[1].at[w], (x, y, 1 - c)).start()
        _staged_finish(legs, scr[4:], mine(outs, c), scr[3])

    def last(ins, outs, scr):
        x, y, c, _ = _place()
        for w, (theirs, dst) in enumerate(zip(mine(outs, 1 - c), mine(outs, c))):
            _remote(theirs, theirs, scr[0].at[w], scr[1].at[w], (x, y, 1 - c)).wait_recv()
            _remote(ins[w], dst, scr[0].at[w], scr[1].at[w], (x, y, 1 - c)).wait_send()
            pltpu.make_async_copy(scr[4 + w], dst, scr[3].at[w]).wait()

    out_shape = [jax.ShapeDtypeStruct(BIG_SHARD[nm], F32) for nm in names]
    scratch = [pltpu.SemaphoreType.DMA((n,))] * 4 + [pltpu.VMEM((h, BIG_SHARD[nm][1]), F32) for nm, h in zip(names, hs)]
    return _Rider(halves, out_shape, scratch, first, _nothing, last)


N_DEV = 8


def _all_peers(x, y, c):
    return [((1 - x) if fx else x, (1 - y) if fy else y, (1 - c) if fc else c)
            for fx in (0, 1) for fy in (0, 1) for fc in (0, 1) if fx or fy or fc]


def _small_gather_rider(vec):
    def sends(ins, outs, scr):
        x, y, c, _ = _place()
        me = 4 * x + 2 * y + c
        return [_remote(ins[0], outs[0].at[me], scr[0].at[k], scr[1].at[k], dev) for k, dev in enumerate(_all_peers(x, y, c))]

    def first(ins, outs, scr):
        x, y, c, _ = _place()
        legs = _staged_start(ins, scr[4:], scr[2])
        for cp in sends(ins, outs, scr):
            cp.start()
        _staged_finish(legs, scr[4:], [outs[0].at[4 * x + 2 * y + c]], scr[3])

    def last(ins, outs, scr):
        x, y, c, _ = _place()
        for k, (px, py, pc) in enumerate(_all_peers(x, y, c)):
            got = outs[0].at[4 * px + 2 * py + pc]
            _remote(got, got, scr[0].at[k], scr[1].at[k], (px, py, pc)).wait_recv()
        for cp in sends(ins, outs, scr):
            cp.wait_send()
        pltpu.make_async_copy(scr[4], outs[0].at[4 * x + 2 * y + c], scr[3].at[0]).wait()

    scratch = ([pltpu.SemaphoreType.DMA((N_DEV - 1,))] * 2 + [pltpu.SemaphoreType.DMA((1,))] * 2
               + [pltpu.VMEM(vec.shape, F32)])
    return _Rider([vec], [jax.ShapeDtypeStruct((N_DEV,) + vec.shape, F32)], scratch, first, _nothing, last)


def _sum_devices(gathered):
    def body(g_ref, out_ref):
        acc = g_ref[0]
        for j in range(1, N_DEV):
            acc = acc + g_ref[j]
        out_ref[...] = acc

    return pl.pallas_call(body, name="sum_devices", out_shape=jax.ShapeDtypeStruct(gathered.shape[1:], F32),
                          compiler_params=_params(0))(gathered)


def _allreduce_small(vec):
    rows = vec.shape[0]

    def body(v_ref, sum_ref, gat_ref, send, recv, loc_sem):
        x, y, c, _ = _place()
        me = 4 * x + 2 * y + c
        lc = pltpu.make_async_copy(v_ref, gat_ref.at[me], loc_sem)
        lc.start()
        peers = []
        for fx in (0, 1):
            for fy in (0, 1):
                for fc in (0, 1):
                    if fx or fy or fc:
                        peers.append(((1 - x) if fx else x, (1 - y) if fy else y, (1 - c) if fc else c))
        sends = []
        for k, dev in enumerate(peers):
            cp = _remote(v_ref, gat_ref.at[me], send.at[k], recv.at[k], dev)
            cp.start()
            sends.append(cp)
        for k, (px, py, pc) in enumerate(peers):
            got = gat_ref.at[4 * px + 2 * py + pc]
            _remote(got, got, send.at[k], recv.at[k], (px, py, pc)).wait_recv()
        for cp in sends:
            cp.wait_send()
        lc.wait()
        acc = gat_ref[0]
        for j in range(1, N_DEV):
            acc = acc + gat_ref[j]
        sum_ref[...] = acc

    total, _ = pl.pallas_call(
        body, name="allreduce_small",
        in_specs=[_whole_vmem()], out_specs=[_whole_vmem(), _whole_vmem()],
        out_shape=[jax.ShapeDtypeStruct((rows, 128), F32), jax.ShapeDtypeStruct((N_DEV, rows, 128), F32)],
        scratch_shapes=[pltpu.SemaphoreType.DMA((N_DEV - 1,))] * 2 + [pltpu.SemaphoreType.DMA(())],
        compiler_params=pltpu.CompilerParams(has_side_effects=True, vmem_limit_bytes=VMEM_LIMIT),
    )(vec)
    return total


def _adam_math(w_ref, g_ref, m_ref, v_ref, d_ref, m2_ref, v2_ref):
    c1 = 1.0 - ADAM_B1 ** ADAM_STEP
    c2 = 1.0 - ADAM_B2 ** ADAM_STEP
    gv = g_ref[...]
    m2 = ADAM_B1 * m_ref[...] + (1.0 - ADAM_B1) * gv
    v2 = ADAM_B2 * v_ref[...] + (1.0 - ADAM_B2) * (gv * gv)
    m2_ref[...] = m2
    v2_ref[...] = v2
    d_ref[...] = -ADAM_LR * ((m2 / c1) / (jnp.sqrt(v2 / c2) + ADAM_EPS) + ADAM_WD * w_ref[...])


def _adamw_many(ws, gs, ms, vs):
    n = len(ws)

    def body(*refs):
        for i in range(n):
            _adam_math(*[refs[k * n + i] for k in range(7)])

    shapes = [jax.ShapeDtypeStruct(w.shape, F32) for w in ws]
    res = pl.pallas_call(body, name="adamw_small", out_shape=shapes * 3, compiler_params=_params(0))(*ws, *gs, *ms, *vs)
    return res[:n], res[n:2 * n], res[2 * n:]


def _adamw(w, g, m, v, name):
    rows, cols = w.shape
    tr = 256 if rows % 256 == 0 else rows

    def body(w_ref, g_ref, m_ref, v_ref, d_ref, m2_ref, v2_ref):
        _adam_math(w_ref, g_ref, m_ref, v_ref, d_ref, m2_ref, v2_ref)

    spec = pl.BlockSpec((tr, cols), lambda i: (i, 0))
    shp = jax.ShapeDtypeStruct((rows, cols), F32)
    return pl.pallas_call(
        body, name=name, grid=(rows // tr,), in_specs=[spec] * 4, out_specs=[spec] * 3, out_shape=[shp] * 3,
        compiler_params=_params(),
    )(w, g, m, v)


SMALL = (
    ("attn_norm_g", (1, 1024)), ("conv_w", (1, 4, 512)), ("conv_b", (1, 512)),
    ("lru_wa_fwd", (1, 8, 64, 64)), ("lru_ba_fwd", (1, 512)), ("lru_wx_fwd", (1, 8, 64, 64)), ("lru_bx_fwd", (1, 512)),
    ("lru_lam_fwd", (1, 512)),
    ("lru_wa_bwd", (1, 8, 64, 64)), ("lru_ba_bwd", (1, 512)), ("lru_wx_bwd", (1, 8, 64, 64)), ("lru_bx_bwd", (1, 512)),
    ("lru_lam_bwd", (1, 512)),
    ("rel_bias", (32, 8)), ("norm_rnn_g", (1, 512)), ("norm_attn_g", (1, 512)), ("mlp_norm_g", (1, 1024)),
    ("final_norm_g", (1024,)),
)
PACK_ROW = 8 * 128


def _pack(parts):
    flat = jnp.concatenate([p.reshape(-1) for p in parts])
    pad = (-flat.shape[0]) % PACK_ROW
    return jnp.pad(flat, (0, pad)).reshape(-1, 128)


def _unpack(packed, shapes):
    flat = packed.reshape(-1)
    out, off = [], 0
    for shp in shapes:
        n = int(np.prod(shp))
        out.append(flat[off:off + n].reshape(shp))
        off += n
    return out


WEIGHT_ORDER = ("attn_norm_g", "w_in", "conv_w", "conv_b", "lru_wa_fwd", "lru_ba_fwd", "lru_wx_fwd", "lru_bx_fwd",
                "lru_lam_fwd", "lru_wa_bwd", "lru_ba_bwd", "lru_wx_bwd", "lru_bx_bwd", "lru_lam_bwd", "rel_bias",
                "norm_rnn_g", "norm_attn_g", "w_out", "mlp_norm_g", "w_up", "w_down", "final_norm_g")


def kernel(x, attn_norm_g, w_in, conv_w, conv_b, lru_wa_fwd, lru_ba_fwd, lru_wx_fwd, lru_bx_fwd, lru_lam_fwd, lru_wa_bwd, lru_ba_bwd, lru_wx_bwd, lru_bx_bwd, lru_lam_bwd, rel_bias, norm_rnn_g, norm_attn_g, w_out, mlp_norm_g, w_up, w_down, final_norm_g, loss_target, m_attn_norm_g, m_w_in, m_conv_w, m_conv_b, m_lru_wa_fwd, m_lru_ba_fwd, m_lru_wx_fwd, m_lru_bx_fwd, m_lru_lam_fwd, m_lru_wa_bwd, m_lru_ba_bwd, m_lru_wx_bwd, m_lru_bx_bwd, m_lru_lam_bwd, m_rel_bias, m_norm_rnn_g, m_norm_attn_g, m_w_out, m_mlp_norm_g, m_w_up, m_w_down, m_final_norm_g, v_attn_norm_g, v_w_in, v_conv_w, v_conv_b, v_lru_wa_fwd, v_lru_ba_fwd, v_lru_wx_fwd, v_lru_bx_fwd, v_lru_lam_fwd, v_lru_wa_bwd, v_lru_ba_bwd, v_lru_wx_bwd, v_lru_bx_bwd, v_lru_lam_bwd, v_rel_bias, v_norm_rnn_g, v_norm_attn_g, v_w_out, v_mlp_norm_g, v_w_up, v_w_down, v_final_norm_g):
    given = dict(locals())
    w = {n: given[n] for n in WEIGHT_ORDER}
    m = {n: given["m_" + n] for n in WEIGHT_ORDER}
    v = {n: given["v_" + n] for n in WEIGHT_ORDER}

    chip = lax.axis_index("x") * 2 + lax.axis_index("y")
    core = lax.axis_index("c")

    shards = {n: w[n][0].astype(BF16) for n in BIG}
    w_in_all, conv_all = _run_rider(_gather_rider(["w_in"], [shards["w_in"]], w["conv_w"][0]), "allgather_w_in")
    p = {n: (t[0] if t.ndim >= 3 else t) for n, t in w.items() if n not in BIG}
    p["final_norm_g"] = w["final_norm_g"].reshape(1, D_MODEL)
    p["conv_w"] = jnp.transpose(conv_all, (1, 0, 2)).reshape(4, D_RNN)
    p["w_in"] = w_in_all

    _, grad_x, small, gathered, big, reduced = _local_step(x[0], loss_target[0], p, {n: shards[n] for n in EARLY})

    late = tuple(big)
    grads = [big[n] for n in late]
    others = _run_rider(_pair_exchange_rider(late, grads), "grad_pair_exchange")
    core_arr = core.reshape(1).astype(jnp.int32)
    parts = [_pair_add(core_arr, g, o, "grad_pair_add_" + n) for n, g, o in zip(late, grads, others)]
    landed = _run_rider(_chip_exchange_rider(parts), "grad_chip_exchange")
    halves = [_chip_sum(t, "grad_chip_sum_" + n) for n, t in zip(late, landed)]
    reduced.update(zip(late, _run_rider(_pair_share_rider(late, halves), "grad_pair_share")))

    early_small = [(n, shp) for n, shp in SMALL if n not in small]
    late_small = [(n, shp) for n, shp in SMALL if n in small]
    *early_g, loss = _unpack(_sum_devices(gathered), [shp for _, shp in early_small] + [(1,)])
    late_g = _unpack(_allreduce_small(_pack([small[n].reshape(shp) for n, shp in late_small])),
                     [shp for _, shp in late_small])
    g = dict(zip([n for n, _ in early_small + late_small], early_g + late_g))
    g["conv_w"] = lax.dynamic_slice_in_dim(g["conv_w"], chip * (D_RNN // N_SHARD), D_RNN // N_SHARD, axis=2)
    for n in BIG:
        g[n] = reduced[n][None]

    delta, new_m, new_v = {}, {}, {}
    for n in BIG:
        d2, m2, v2 = _adamw(w[n][0], reduced[n], m[n][0], v[n][0], "adamw_" + n)
        delta[n], new_m[n], new_v[n] = d2[None], m2[None], v2[None]
    names = [n for n, _ in SMALL]
    for dst, src in zip((delta, new_m, new_v), _adamw_many(*[[t[n] for n in names] for t in (w, g, m, v)])):
        dst.update(dict(zip(names, src)))

    return (loss.reshape(()), grad_x[None], *[g[n] for n in WEIGHT_ORDER], *[delta[n] for n in WEIGHT_ORDER],
            *[new_m[n] for n in WEIGHT_ORDER], *[new_v[n] for n in WEIGHT_ORDER])
```

```python
import functools
import math

import numpy as np
import jax
import jax.numpy as jnp
from jax import lax
from jax.experimental import pallas as pl
from jax.experimental.pallas import tpu as pltpu

F32 = jnp.float32
BF16 = jnp.bfloat16

D_MODEL = 1024
D_RNN = 512
D_ATTN = 512
N_HEADS = 8
HEAD_DIM = 64
N_RNN_BLOCKS = 8
RNN_BLOCK = 64
D_IN = 2 * D_RNN + 3 * D_ATTN
D_FF = 4 * D_MODEL
N_SHARD = 4
IN_BLK = D_IN // N_SHARD
OUT_BLK = D_MODEL // N_SHARD
FF_BLK = D_FF // N_SHARD
EPS = 1e-6
NEG_INF = -1e30
LRU_C = 8.0
DILATIONS = (1, 4, 16)
F32_LAYOUT = 4
HALF_WIN = 64
Q_BLK = 128
K_WIN = 256
N_BUCKETS = 32
MAX_DISTANCE = 1024
ATTN_SCALE = HEAD_DIM ** -0.5

ADAM_LR = 0.001
ADAM_B1 = 0.9
ADAM_B2 = 0.999
ADAM_EPS = 1e-08
ADAM_WD = 0.01
ADAM_STEP = 10

TS = 512
TS_MLP = 256
TS_INPROJ_BWD = 256
ATTN_SUB = 4
SCAN_UNROLL = 4
SUB = 8
VMEM_LIMIT = 56 * 1024 * 1024
GELU_C0 = math.sqrt(2.0 / math.pi)
GELU_C1 = 0.044715

MESH = pl.DeviceIdType.MESH


def _params(n_grid=1):
    return pltpu.CompilerParams(vmem_limit_bytes=VMEM_LIMIT, dimension_semantics=("arbitrary",) * n_grid)


def _whole_vmem():
    return pl.BlockSpec(memory_space=pltpu.VMEM)


def _rows(width, tile=TS):
    return pl.BlockSpec((tile, width), lambda i: (i, 0))


def _sigmoid(z):
    return 0.5 * jnp.tanh(0.5 * z) + 0.5


def _log1p(u):
    w = 1.0 + u
    return jnp.where(w == 1.0, u, jnp.log(w) * (u / (w - 1.0)))


def _softplus(z):
    return jnp.maximum(z, 0.0) + _log1p(jnp.exp(-jnp.abs(z)))


def _gelu_parts(g):
    inner = GELU_C0 * (g + GELU_C1 * g * g * g)
    t = jnp.tanh(inner)
    val = 0.5 * g * (1.0 + t)
    dinner = GELU_C0 * (1.0 + 3.0 * GELU_C1 * g * g)
    grad = 0.5 * (1.0 + t) + 0.5 * g * (1.0 - t * t) * dinner
    return val, grad


def _rms(x):
    rstd = lax.rsqrt(jnp.mean(x * x, axis=-1, keepdims=True) + EPS)
    return rstd, x * rstd


def _rms_bwd(dy, g, xhat, rstd):
    dxh = dy * g
    dx = rstd * (dxh - xhat * jnp.mean(dxh * xhat, axis=-1, keepdims=True))
    dg = jnp.sum(dy * xhat, axis=0, keepdims=True)
    return dx, dg


def _dot(a, b):
    return jnp.dot(a, b, preferred_element_type=F32)


def _dot_nt(a, b):
    return lax.dot_general(a, b, (((1,), (1,)), ((), ())), preferred_element_type=F32)


def _dot_tn(a, b):
    return lax.dot_general(a, b, (((0,), (0,)), ((), ())), preferred_element_type=F32)


def _shifted(tile, prev8, next8, k):
    n = tile.shape[0]
    row = lax.broadcasted_iota(jnp.int32, tile.shape, 0)
    if k == 0:
        return tile
    if k < 0:
        r = pltpu.roll(tile, -k, 0)
        for j in range(-k):
            r = jnp.where(row == j, prev8[SUB + j + k:SUB + j + k + 1, :], r)
        return r
    r = pltpu.roll(tile, n - k, 0)
    for j in range(k):
        r = jnp.where(row == n - k + j, next8[j:j + 1, :], r)
    return r


def _to_lane_blocks(val, s_ref):
    for j in range(val.shape[1] // 128):
        s_ref[j] = val[:, j * 128:(j + 1) * 128]


def _from_lane_blocks(s_ref):
    return jnp.concatenate([s_ref[j] for j in range(s_ref.shape[0])], axis=-1)


def _class_rows(s_ref, r, dil):
    n = s_ref.shape[1] // dil
    return jnp.concatenate([s_ref[j, pl.ds(r, n, stride=dil), :] for j in range(s_ref.shape[0])], axis=-1)


def _split_classes(val, s_ref, out_ref, dil):
    _to_lane_blocks(val, s_ref)
    for r in range(dil):
        out_ref[r] = _class_rows(s_ref, r, dil).astype(out_ref.dtype)


def _merge_classes(in_ref, s_ref, dil, also_ref=None):
    n = s_ref.shape[1] // dil
    for r in range(dil):
        v = in_ref[r] if also_ref is None else in_ref[r] + also_ref[r]
        for j in range(s_ref.shape[0]):
            s_ref[j, pl.ds(r, n, stride=dil), :] = v[:, j * 128:(j + 1) * 128]
    return _from_lane_blocks(s_ref)


def _class_spec(dil, tile=TS):
    return pl.BlockSpec((dil, tile // dil, 512), lambda i: (0, i, 0))


def _class_shape(S, dil, dtype):
    return jax.ShapeDtypeStruct((dil, S // dil, 512), dtype)


def _scan_tile(a_ref, b_ref, h_ref, carry_ref, reverse):
    n = a_ref.shape[0]
    width = a_ref.shape[1]
    groups = n // SUB
    row = lax.broadcasted_iota(jnp.int32, (SUB, width), 0)

    def group_scan(g):
        r0 = pl.multiple_of(g * SUB, SUB)
        a = a_ref[pl.ds(r0, SUB), :]
        b = b_ref[pl.ds(r0, SUB), :]
        for s in (1, 2, 4):
            if reverse:
                a_sh = pltpu.roll(a, SUB - s, 0)
                b_sh = pltpu.roll(b, SUB - s, 0)
                m = row < SUB - s
            else:
                a_sh = pltpu.roll(a, s, 0)
                b_sh = pltpu.roll(b, s, 0)
                m = row >= s
            b = jnp.where(m, a * b_sh + b, b)
            a = jnp.where(m, a * a_sh, a)
        return r0, a, b

    def step(i, carry):
        first = i * SCAN_UNROLL
        order = [(groups - 1 - (first + u)) if reverse else (first + u) for u in range(SCAN_UNROLL)]
        scans = [group_scan(g) for g in order]
        for r0, a, b in scans:
            h = b + a * carry
            h_ref[pl.ds(r0, SUB), :] = h
            edge = h[0:1, :] if reverse else h[SUB - 1:SUB, :]
            carry = jnp.broadcast_to(edge, (SUB, width))
        return carry

    carry_ref[...] = lax.fori_loop(0, groups // SCAN_UNROLL, step, carry_ref[...])


def _conv_fwd(xr, prev8, next8, cw, cb):
    y = cb + _shifted(xr, prev8, next8, -2) * cw[0:1, :]
    y = y + _shifted(xr, prev8, next8, -1) * cw[1:2, :]
    y = y + xr * cw[2:3, :]
    y = y + _shifted(xr, prev8, next8, 1) * cw[3:4, :]
    return y


def _lru_gates(xc, wa_ref, ba, wx_ref, bx, lam):
    xcb = xc.astype(BF16)
    r = _sigmoid(_dot(xcb, wa_ref[...]) + ba)
    i = _sigmoid(_dot(xcb, wx_ref[...]) + bx)
    cl = -LRU_C * _softplus(-lam)
    la = cl * r
    a = jnp.exp(la)
    m2 = -jnp.tanh(la) * (a * a + 1.0)
    inv = lax.rsqrt(m2)
    mult = jnp.where(m2 > 0.0, m2 * inv, 0.0)
    return xcb, r, i, cl, a, mult, inv


def _inproj_fwd(x, g1, w_in, rider=None):
    S = x.shape[0]

    def body(x_ref, g_ref, w_ref, xr_ref, gate_ref, *rest):
        qkv_refs, s_ref = rest[:9], rest[9]
        _, xh = _rms(x_ref[...])
        h = (xh * g_ref[...]).astype(BF16)
        proj = jnp.concatenate([_dot(h, w_ref[j]) for j in range(N_SHARD)], axis=-1)
        xr_ref[...] = proj[:, 0:512]
        gate_ref[...] = proj[:, 512:1024]
        for t in range(3):
            val = proj[:, 1024 + 512 * t:1536 + 512 * t]
            qkv_refs[3 * t][0] = val.astype(BF16)
            _to_lane_blocks(val, s_ref)
            for p, dil in enumerate(DILATIONS[1:]):
                for r in range(dil):
                    qkv_refs[3 * t + 1 + p][r] = _class_rows(s_ref, r, dil).astype(BF16)

    f = jax.ShapeDtypeStruct((S, 512), F32)
    return _call(
        body, "inproj_fwd", (S // TS,),
        [_rows(D_MODEL), _whole_vmem(), _whole_vmem()],
        [_rows(512)] * 2 + [_class_spec(d) for d in DILATIONS] * 3,
        [f, f] + [_class_shape(S, d, BF16) for d in DILATIONS] * 3,
        [pltpu.VMEM((4, TS, 128), F32)], (x, g1, w_in), rider)


def _halo_specs(S, order, tile=TS):
    per = tile // SUB
    last = S // SUB - 1
    return [
        pl.BlockSpec((tile, 512), lambda i: (order(i), 0)),
        pl.BlockSpec((SUB, 512), lambda i: (jnp.maximum(order(i) * per - 1, 0), 0)),
        pl.BlockSpec((SUB, 512), lambda i: (jnp.minimum((order(i) + 1) * per, last), 0)),
    ]


def _rnn_fwd(xr, conv_w, conv_b, wa, ba, wx, bx, lam, reverse, rider=None):
    S = xr.shape[0]
    nt = S // TS
    order = (lambda i: nt - 1 - i) if reverse else (lambda i: i)

    def body(x_ref, xp_ref, xn_ref, cw_ref, cb_ref, wa_ref, ba_ref, wx_ref, bx_ref, lam_ref, h_ref, a_s, b_s, carry):
        i = pl.program_id(0)
        t = order(i)

        @pl.when(i == 0)
        def _():
            carry[...] = jnp.zeros_like(carry)

        prev8 = jnp.where(t > 0, xp_ref[...], 0.0)
        next8 = jnp.where(t < nt - 1, xn_ref[...], 0.0)
        xc = _conv_fwd(x_ref[...], prev8, next8, cw_ref[...], cb_ref[...])
        _, _, gi, _, a, mult, _ = _lru_gates(xc, wa_ref, ba_ref[...], wx_ref, bx_ref[...], lam_ref[...])
        a_s[...] = a
        b_s[...] = mult * (gi * xc)
        _scan_tile(a_s, b_s, h_ref, carry, reverse)

    (h,), carried = _call(
        body, "rnn_fwd_rev" if reverse else "rnn_fwd_fwd", (nt,),
        _halo_specs(S, order) + [_whole_vmem()] * 7,
        [pl.BlockSpec((TS, 512), lambda i: (order(i), 0))],
        [jax.ShapeDtypeStruct((S, 512), F32)],
        [pltpu.VMEM((TS, 512), F32), pltpu.VMEM((TS, 512), F32), pltpu.VMEM((SUB, 512), F32)],
        (xr, xr, xr, conv_w, conv_b, wa, ba, wx, bx, lam), rider)
    return h, carried


def _mix_fwd(o3, l3, hf, hb, gate, x, g_rnn, g_attn, w_out):
    S = x.shape[0]

    def body(o1, o2, o3_, l1, l2, l3_, hf_ref, hb_ref, gate_ref, x_ref, gr_ref, ga_ref, w_ref,
             x1_ref, mix_ref, ya1, ya2, ls1, ls2, s_ref):
        la, lb, lc = l1[0], _merge_classes(l2, s_ref, F32_LAYOUT), _merge_classes(l3_, s_ref, F32_LAYOUT)
        m = jnp.maximum(jnp.maximum(la, lb), lc)
        ea, eb, ec = jnp.exp(la - m), jnp.exp(lb - m), jnp.exp(lc - m)
        den = ea + eb + ec
        lse = m + jnp.log(den)
        ya = (ea * o1[0] + eb * _merge_classes(o2, s_ref, F32_LAYOUT) + ec * _merge_classes(o3_, s_ref, F32_LAYOUT)) / den
        ya1[0] = ya
        ls1[0] = lse
        _split_classes(ya, s_ref, ya2, F32_LAYOUT)
        _split_classes(lse, s_ref, ls2, F32_LAYOUT)
        gg, _ = _gelu_parts(gate_ref[...])
        yr = (hf_ref[...] + hb_ref[...]) * gg
        _, xh_r = _rms(yr)
        _, xh_a = _rms(ya)
        mix = jnp.concatenate([xh_r * gr_ref[...], xh_a * ga_ref[...]], axis=-1).astype(BF16)
        mix_ref[...] = mix
        acc = x_ref[...]
        for j in range(N_SHARD):
            acc = acc + _dot(mix[:, j * OUT_BLK:(j + 1) * OUT_BLK], w_ref[j])
        x1_ref[...] = acc

    one, four = _class_spec(1), _class_spec(F32_LAYOUT)
    return pl.pallas_call(
        body, grid=(S // TS,), name="mix_fwd",
        in_specs=[one, four, four] * 2 + [_rows(512)] * 3 + [_rows(D_MODEL)] + [_whole_vmem()] * 3,
        out_specs=[_rows(D_MODEL), _rows(D_MODEL)] + [one, four] * 2,
        out_shape=[jax.ShapeDtypeStruct((S, D_MODEL), F32), jax.ShapeDtypeStruct((S, D_MODEL), BF16)]
        + [_class_shape(S, 1, F32), _class_shape(S, F32_LAYOUT, F32)] * 2,
        scratch_shapes=[pltpu.VMEM((4, TS, 128), F32)],
        compiler_params=_params(),
    )(*o3, *l3, hf, hb, gate, x, g_rnn, g_attn, w_out)


def _mlp_fwd_bwd(x1, target, g_mlp, g_fin, w_up, w_down):
    S = x1.shape[0]
    tm = TS_MLP

    def body(x1_ref, t_ref, gm_ref, gf_ref, wu_ref, wd_ref,
             dx1_ref, h2_ref, a2_ref, du_ref, dx2_ref, loss_ref, dgf_ref, dgm_ref, relu_s):
        @pl.when(pl.program_id(0) == 0)
        def _():
            loss_ref[...] = jnp.zeros_like(loss_ref)
            dgf_ref[...] = jnp.zeros_like(dgf_ref)
            dgm_ref[...] = jnp.zeros_like(dgm_ref)

        x1v = x1_ref[...]
        rstd1, xh1 = _rms(x1v)
        h2 = (xh1 * gm_ref[...]).astype(BF16)
        h2_ref[...] = h2
        x2 = x1v
        for j in range(N_SHARD):
            r = jnp.maximum(_dot(h2, wu_ref[j]), 0.0)
            relu_s[j] = r
            a2 = (r * r).astype(BF16)
            a2_ref[:, j * FF_BLK:(j + 1) * FF_BLK] = a2
            x2 = x2 + _dot(a2, wd_ref[j])
        rstd2, xh2 = _rms(x2)
        err = xh2 * gf_ref[...] - t_ref[...]
        loss_ref[...] += jnp.sum(err * err, axis=0, keepdims=True)
        dy = err * (1.0 / D_MODEL)
        dx2, dgf = _rms_bwd(dy, gf_ref[...], xh2, rstd2)
        dgf_ref[...] += dgf
        dx2b = dx2.astype(BF16)
        dx2_ref[...] = dx2b
        dh2 = jnp.zeros((tm, D_MODEL), F32)
        for j in range(N_SHARD):
            du = (_dot_nt(dx2b, wd_ref[j]) * (2.0 * relu_s[j])).astype(BF16)
            du_ref[:, j * FF_BLK:(j + 1) * FF_BLK] = du
            dh2 = dh2 + _dot_nt(du, wu_ref[j])
        dx1n, dgm = _rms_bwd(dh2, gm_ref[...], xh1, rstd1)
        dgm_ref[...] += dgm
        dx1_ref[...] = dx2 + dx1n

    vec = jax.ShapeDtypeStruct((1, D_MODEL), F32)
    return pl.pallas_call(
        body, grid=(S // tm,), name="mlp_fwd_bwd",
        in_specs=[_rows(D_MODEL, tm), _rows(D_MODEL, tm)] + [_whole_vmem()] * 4,
        out_specs=[_rows(D_MODEL, tm), _rows(D_MODEL, tm), _rows(D_FF, tm), _rows(D_FF, tm), _rows(D_MODEL, tm)]
        + [_whole_vmem()] * 3,
        out_shape=[jax.ShapeDtypeStruct((S, D_MODEL), F32), jax.ShapeDtypeStruct((S, D_MODEL), BF16),
                   jax.ShapeDtypeStruct((S, D_FF), BF16), jax.ShapeDtypeStruct((S, D_FF), BF16),
                   jax.ShapeDtypeStruct((S, D_MODEL), BF16), vec, vec, vec],
        scratch_shapes=[pltpu.VMEM((N_SHARD, tm, FF_BLK), F32)],
        compiler_params=_params(),
    )(x1, target, g_mlp, g_fin, w_up, w_down)


def _mix_bwd(dx1, w_out, mixb, ya, hf, hb, gate, g_rnn, g_attn):
    S = dx1.shape[0]

    def body(dx1_ref, w_ref, mix_ref, ya_ref, hf_ref, hb_ref, gate_ref, gr_ref, ga_ref,
             dhs_ref, dgate_ref, dya1, dya2, dw_ref, dgr_ref, dga_ref, s_ref):
        @pl.when(pl.program_id(0) == 0)
        def _():
            dw_ref[...] = jnp.zeros_like(dw_ref)
            dgr_ref[...] = jnp.zeros_like(dgr_ref)
            dga_ref[...] = jnp.zeros_like(dga_ref)

        dx1b = dx1_ref[...].astype(BF16)
        mix = mix_ref[...]
        for j in range(N_SHARD):
            dw_ref[j] += _dot_tn(mix[:, j * OUT_BLK:(j + 1) * OUT_BLK], dx1b)
        dmix = jnp.concatenate([_dot_nt(dx1b, w_ref[j]) for j in range(N_SHARD)], axis=-1)
        gg, dgg = _gelu_parts(gate_ref[...])
        hs = hf_ref[...] + hb_ref[...]
        rstd_r, xh_r = _rms(hs * gg)
        dyr, dgr = _rms_bwd(dmix[:, 0:D_RNN], gr_ref[...], xh_r, rstd_r)
        dgr_ref[...] += dgr
        rstd_a, xh_a = _rms(ya_ref[0])
        dya, dga = _rms_bwd(dmix[:, D_RNN:], ga_ref[...], xh_a, rstd_a)
        dga_ref[...] += dga
        dya1[0] = dya
        _split_classes(dya, s_ref, dya2, F32_LAYOUT)
        dhs_ref[...] = dyr * gg
        dgate_ref[...] = dyr * hs * dgg

    f512 = jax.ShapeDtypeStruct((S, 512), F32)
    vec = jax.ShapeDtypeStruct((1, 512), F32)
    return pl.pallas_call(
        body, grid=(S // TS,), name="mix_bwd",
        in_specs=[_rows(D_MODEL), _whole_vmem(), _rows(D_MODEL), _class_spec(1)] + [_rows(512)] * 3 + [_whole_vmem()] * 2,
        out_specs=[_rows(512)] * 2 + [_class_spec(1), _class_spec(F32_LAYOUT)] + [_whole_vmem()] * 3,
        out_shape=[f512, f512, _class_shape(S, 1, F32), _class_shape(S, F32_LAYOUT, F32),
                   jax.ShapeDtypeStruct((N_SHARD, OUT_BLK, D_MODEL), F32), vec, vec],
        scratch_shapes=[pltpu.VMEM((4, TS, 128), F32)],
        compiler_params=_params(),
    )(dx1, w_out, mixb, ya, hf, hb, gate, g_rnn, g_attn)


def _rnn_bwd(xr, h, dhs, conv_w, conv_b, wa, ba, wx, bx, lam, reverse):
    S = xr.shape[0]
    nt = S // TS
    order = (lambda i: i) if reverse else (lambda i: nt - 1 - i)
    per = TS // SUB
    last = S // SUB - 1
    if reverse:
        h_halo = pl.BlockSpec((SUB, 512), lambda i: (jnp.minimum((order(i) + 1) * per, last), 0))
    else:
        h_halo = pl.BlockSpec((SUB, 512), lambda i: (jnp.maximum(order(i) * per - 1, 0), 0))
    tile = pl.BlockSpec((TS, 512), lambda i: (order(i), 0))

    def body(x_ref, xp_ref, xn_ref, h_ref, hh_ref, dh_ref, cw_ref, cb_ref, wa_ref, ba_ref, wx_ref, bx_ref, lam_ref,
             dxc_ref, dwa_ref, dwx_ref, dvec_ref, a_s, g_s, carry, edge):
        i = pl.program_id(0)
        t = order(i)

        @pl.when(i == 0)
        def _():
            carry[...] = jnp.zeros_like(carry)
            edge[...] = jnp.zeros_like(edge)
            dwa_ref[...] = jnp.zeros_like(dwa_ref)
            dwx_ref[...] = jnp.zeros_like(dwx_ref)
            dvec_ref[...] = jnp.zeros_like(dvec_ref)

        prev8 = jnp.where(t > 0, xp_ref[...], 0.0)
        next8 = jnp.where(t < nt - 1, xn_ref[...], 0.0)
        xc = _conv_fwd(x_ref[...], prev8, next8, cw_ref[...], cb_ref[...])
        xcb, r, gi, cl, a, mult, inv_mult = _lru_gates(xc, wa_ref, ba_ref[...], wx_ref, bx_ref[...], lam_ref[...])
        hv = h_ref[...]
        if reverse:
            a_s[...] = _shifted(a, edge[...], None, -1)
            edge[...] = a[TS - SUB:TS, :]
            hh = jnp.where(t < nt - 1, hh_ref[...], 0.0)
            h_prev = _shifted(hv, None, hh, 1)
        else:
            a_s[...] = _shifted(a, None, edge[...], 1)
            edge[...] = a[0:SUB, :]
            hh = jnp.where(t > 0, hh_ref[...], 0.0)
            h_prev = _shifted(hv, hh, None, -1)
        _scan_tile(a_s, dh_ref, g_s, carry, not reverse)
        g = g_s[...]
        da = g * h_prev
        gm = g * mult
        d_i = gm * xc
        dmult = g * gi * xc
        dla = da * a - dmult * (a * a) * inv_mult
        d_r = dla * cl
        dpre_r = d_r * r * (1.0 - r)
        dpre_i = d_i * gi * (1.0 - gi)
        dprb = dpre_r.astype(BF16)
        dpib = dpre_i.astype(BF16)
        dwa_ref[...] += _dot_tn(xcb, dprb)
        dwx_ref[...] += _dot_tn(xcb, dpib)
        dvec_ref[0:1, :] += jnp.sum(dpre_r, axis=0, keepdims=True)
        dvec_ref[1:2, :] += jnp.sum(dpre_i, axis=0, keepdims=True)
        dvec_ref[2:3, :] += jnp.sum(dla * r, axis=0, keepdims=True)
        dvec_ref[3:4, :] = dvec_ref[2:3, :] * (LRU_C * _sigmoid(-lam_ref[...]))
        dxc_ref[...] = gm * gi + _dot_nt(dprb, wa_ref[...]) + _dot_nt(dpib, wx_ref[...])

    sq = jax.ShapeDtypeStruct((D_RNN, D_RNN), F32)
    return pl.pallas_call(
        body, grid=(nt,), name="rnn_bwd_rev" if reverse else "rnn_bwd_fwd",
        in_specs=_halo_specs(S, order) + [tile, h_halo, tile] + [_whole_vmem()] * 7,
        out_specs=[tile, _whole_vmem(), _whole_vmem(), _whole_vmem()],
        out_shape=[jax.ShapeDtypeStruct((S, 512), F32), sq, sq, jax.ShapeDtypeStruct((SUB, 512), F32)],
        scratch_shapes=[pltpu.VMEM((TS, 512), F32), pltpu.VMEM((TS, 512), F32), pltpu.VMEM((SUB, 512), F32),
                        pltpu.VMEM((SUB, 512), F32)],
        compiler_params=_params(),
    )(xr, xr, xr, h, h, dhs, conv_w, conv_b, wa, ba, wx, bx, lam)


def _inproj_bwd(x, dx1, xr, dxc_f, dxc_b, dgate, dq3, dk3, dv3, g1, conv_w, w_in, rider=None):
    S = x.shape[0]
    tb = TS_INPROJ_BWD
    nt = S // tb
    ident = lambda i: i

    def body(x_ref, dx1_ref, xr_ref, xrp_ref, xrn_ref, cf_ref, cfp_ref, cfn_ref, cb_ref, cbp_ref, cbn_ref, dgate_ref,
             dq1, dq2, dq3_, dk1, dk2, dk3_, dv1, dv2, dv3_, g_ref, cw_ref, w_ref,
             dx_ref, dw_ref, dg_ref, dcw_ref, s_ref):
        i = pl.program_id(0)

        @pl.when(i == 0)
        def _():
            dw_ref[...] = jnp.zeros_like(dw_ref)
            dg_ref[...] = jnp.zeros_like(dg_ref)
            dcw_ref[...] = jnp.zeros_like(dcw_ref)

        first, last = i > 0, i < nt - 1
        dxc = cf_ref[...] + cb_ref[...]
        dxc_p = jnp.where(first, cfp_ref[...] + cbp_ref[...], 0.0)
        dxc_n = jnp.where(last, cfn_ref[...] + cbn_ref[...], 0.0)
        cw = cw_ref[...]
        dxr = (_shifted(dxc, dxc_p, dxc_n, 2) * cw[0:1, :] + _shifted(dxc, dxc_p, dxc_n, 1) * cw[1:2, :]
               + dxc * cw[2:3, :] + _shifted(dxc, dxc_p, dxc_n, -1) * cw[3:4, :])
        xrv = xr_ref[...]
        xr_p = jnp.where(first, xrp_ref[...], 0.0)
        xr_n = jnp.where(last, xrn_ref[...], 0.0)
        for k, off in enumerate((-2, -1, 0, 1)):
            dcw_ref[k:k + 1, :] += jnp.sum(dxc * _shifted(xrv, xr_p, xr_n, off), axis=0, keepdims=True)
        dcw_ref[4:5, :] += jnp.sum(dxc, axis=0, keepdims=True)

        def total(a, b, c_):
            return a[0] + _merge_classes(b, s_ref, F32_LAYOUT, c_)

        dproj = jnp.concatenate(
            [dxr, dgate_ref[...], total(dq1, dq2, dq3_), total(dk1, dk2, dk3_), total(dv1, dv2, dv3_)],
            axis=-1).astype(BF16)
        xv = x_ref[...]
        rstd, xh = _rms(xv)
        hb = (xh * g_ref[...]).astype(BF16)
        dh = jnp.zeros((tb, D_MODEL), F32)
        for j in range(N_SHARD):
            dpj = dproj[:, j * IN_BLK:(j + 1) * IN_BLK]
            dh = dh + _dot_nt(dpj, w_ref[j])
            dw_ref[j] += _dot_tn(hb, dpj)
        dxn, dg = _rms_bwd(dh, g_ref[...], xh, rstd)
        dg_ref[...] += dg
        dx_ref[...] = dx1_ref[...] + dxn

    halo = _halo_specs(S, ident, tb)
    return _call(
        body, "inproj_bwd", (nt,),
        [_rows(D_MODEL, tb), _rows(D_MODEL, tb)] + halo * 3 + [_rows(512, tb)]
        + [_class_spec(1, tb), _class_spec(F32_LAYOUT, tb), _class_spec(F32_LAYOUT, tb)] * 3 + [_whole_vmem()] * 3,
        [_rows(D_MODEL, tb), _whole_vmem(), _whole_vmem(), _whole_vmem()],
        [jax.ShapeDtypeStruct((S, D_MODEL), F32), jax.ShapeDtypeStruct((N_SHARD, D_MODEL, IN_BLK), F32),
         jax.ShapeDtypeStruct((1, D_MODEL), F32), jax.ShapeDtypeStruct((SUB, 512), F32)],
        [pltpu.VMEM((4, tb, 128), F32)],
        (x, dx1, xr, xr, xr, dxc_f, dxc_f, dxc_f, dxc_b, dxc_b, dxc_b, dgate, *dq3, *dk3, *dv3, g1, conv_w, w_in), rider)


def _dw_matmul(a, b, a_cols, b_cols, name):
    S = a.shape[0]
    tk = 1024
    a_shared = a.shape[1] == a_cols
    b_shared = b.shape[1] == b_cols

    def body(a_ref, b_ref, o_ref):
        @pl.when(pl.program_id(1) == 0)
        def _():
            o_ref[...] = jnp.zeros_like(o_ref)
        o_ref[0] += _dot_tn(a_ref[...], b_ref[...])

    return pl.pallas_call(
        body, grid=(N_SHARD, S // tk), name=name,
        in_specs=[pl.BlockSpec((tk, a_cols), (lambda j, k: (k, 0)) if a_shared else (lambda j, k: (k, j))),
                  pl.BlockSpec((tk, b_cols), (lambda j, k: (k, 0)) if b_shared else (lambda j, k: (k, j)))],
        out_specs=pl.BlockSpec((1, a_cols, b_cols), lambda j, k: (j, 0, 0)),
        out_shape=jax.ShapeDtypeStruct((N_SHARD, a_cols, b_cols), F32),
        compiler_params=_params(2),
    )(a, b)


def _t5_bucket_np(rel):
    nb = N_BUCKETS // 2
    max_exact = nb // 2
    ret = np.where(rel > 0, nb, 0)
    n = np.abs(rel)
    nf = np.maximum(n, 1).astype(np.float32)
    large = max_exact + (np.log(nf / np.float32(max_exact)) / np.float32(math.log(MAX_DISTANCE / max_exact))
                         * np.float32(nb - max_exact)).astype(np.int32)
    large = np.minimum(large, nb - 1)
    return ret + np.where(n < max_exact, n, large)


_VARIANT_OFFSETS = (-HALF_WIN,) * 3


def _band_index():
    kk = np.arange(K_WIN)[None, :]
    ql = np.arange(Q_BLK)[:, None]
    rel = np.stack([kk - ql + off for off in _VARIANT_OFFSETS])
    band = np.abs(rel) <= HALF_WIN
    inside = np.stack([np.broadcast_to(kk >= HALF_WIN, band[0].shape), np.ones_like(band[0]),
                       np.broadcast_to(kk < K_WIN - HALF_WIN, band[0].shape)])
    return rel, band & inside


def _bucket_tables(dil):
    rel, valid = _band_index()
    bucket = _t5_bucket_np(np.clip(rel, -HALF_WIN, HALF_WIN) * dil)
    return np.where(valid, bucket, -1).astype(np.int32)


def _bias_mats(rel_bias):
    tables = [_bucket_tables(d) for d in DILATIONS]
    used = [sorted(set(t[t >= 0].tolist())) for t in tables]

    def body(rb_ref, t1, t2, t3, o1, o2, o3):
        for t_ref, o_ref, buckets in ((t1, o1, used[0]), (t2, o2, used[1]), (t3, o3, used[2])):
            for var in range(3):
                bk = t_ref[var]
                for h in range(N_HEADS):
                    acc = jnp.full((Q_BLK, K_WIN), NEG_INF, F32)
                    for b in buckets:
                        acc = jnp.where(bk == b, rb_ref[b, h], acc)
                    o_ref[var, h] = acc

    shp = jax.ShapeDtypeStruct((3, N_HEADS, Q_BLK, K_WIN), F32)
    return pl.pallas_call(
        body, name="bias_tables", in_specs=[pl.BlockSpec(memory_space=pltpu.SMEM)] + [_whole_vmem()] * 3,
        out_shape=[shp] * 3, compiler_params=_params(0),
    )(rel_bias, *[jnp.asarray(t) for t in tables])


def _variant(qb, nq):
    return jnp.where(qb == 0, 0, jnp.where(qb == nq - 1, 2, 1))


def _win_start(qb):
    return pl.multiple_of(qb * Q_BLK, Q_BLK)


def _fill_padded(src_ref, pad_ref):
    L = src_ref.shape[0]
    edge = jnp.zeros((HALF_WIN, 128), pad_ref.dtype)
    pad_ref[0:HALF_WIN, :] = edge
    pad_ref[HALF_WIN:HALF_WIN + L, :] = src_ref[...]
    pad_ref[HALF_WIN + L:2 * HALF_WIN + L, :] = edge


INNER = {1: 1, 4: 1, 16: 4}


def _attn_layout(dil, L):
    inner = INNER[dil]
    n_outer = dil // inner
    nsub = min(ATTN_SUB, L // Q_BLK)
    qt = nsub * Q_BLK
    grid = (4, n_outer, inner, L // qt)
    cls = lambda r, m: r + n_outer * m
    qspec = pl.BlockSpec((None, qt, 128), lambda hp, r, m, s: (cls(r, m), s, hp))
    kspec = pl.BlockSpec((None, L, 128), lambda hp, r, m, s: (cls(r, m), 0, hp))
    bspec = pl.BlockSpec((3, 2, Q_BLK, K_WIN), lambda hp, r, m, s: (0, hp, 0, 0))
    kfspec = pl.BlockSpec((None, inner * L, 128), lambda hp, r, m, s: (r, 0, hp))
    qfspec = kfspec if inner > 1 else pl.BlockSpec((None, qt, 128), lambda hp, r, m, s: (r, s, hp))
    fshape = jax.ShapeDtypeStruct((n_outer, inner * L, D_ATTN), F32)

    def qrows(sub):
        if inner == 1:
            return (slice(sub * Q_BLK, (sub + 1) * Q_BLK), slice(None))
        first = (pl.program_id(3) * nsub + sub) * Q_BLK
        return (pl.ds(pl.program_id(2) + inner * first, Q_BLK, stride=inner), slice(None))

    def krows():
        if inner == 1:
            return (slice(None), slice(None))
        return (pl.ds(pl.program_id(2), L, stride=inner), slice(None))

    return inner, nsub, grid, qspec, kspec, bspec, qfspec, kfspec, fshape, qrows, krows


def _head_masks():
    lane = lax.broadcasted_iota(jnp.int32, (Q_BLK, 128), 1)
    return lane < HEAD_DIM


def _attn_fwd(q, k, v, bias):
    dil, L, _ = q.shape
    nq = L // Q_BLK
    inner, nsub, grid, qspec, kspec, bspec, qfspec, kfspec, fshape, qrows, krows = _attn_layout(dil, L)

    def body(q_ref, k_ref, v_ref, b_ref, o_ref, l_ref, kp, vp):
        step = pl.program_id(3)

        @pl.when(step == 0)
        def _():
            _fill_padded(k_ref, kp)
            _fill_padded(v_ref, vp)

        h0 = _head_masks()
        for sub in range(nsub):
            qb = step * nsub + sub
            st = _win_start(qb)
            var = _variant(qb, nq)
            kw = kp[pl.ds(st, K_WIN), :]
            vw = vp[pl.ds(st, K_WIN), :]
            qs = q_ref[sub * Q_BLK:(sub + 1) * Q_BLK, :] * ATTN_SCALE
            outs, lses = [], []
            for h in range(2):
                qh = jnp.where(h0 if h == 0 else ~h0, qs, jnp.zeros_like(qs))
                s = _dot_nt(qh, kw) + b_ref[var, h]
                m = jnp.max(s, axis=-1, keepdims=True)
                p = jnp.exp(s - m)
                l = jnp.sum(p, axis=-1, keepdims=True)
                outs.append(_dot(p.astype(BF16), vw) / l)
                lses.append(m + jnp.log(l))
            o_ref[qrows(sub)] = jnp.where(h0, outs[0], outs[1])
            l_ref[qrows(sub)] = jnp.where(h0, lses[0], lses[1])

    return pl.pallas_call(
        body, grid=grid, name=f"attn_fwd_d{dil}",
        in_specs=[qspec, kspec, kspec, bspec], out_specs=[qfspec, qfspec], out_shape=[fshape, fshape],
        scratch_shapes=[pltpu.VMEM((L + 2 * HALF_WIN, 128), BF16)] * 2,
        compiler_params=_params(4),
    )(q, k, v, bias)


def _attn_bwd(q, k, v, bias, do, o, lse, rider=None):
    dil, L, _ = q.shape
    nq = L // Q_BLK
    inner, nsub, grid, qspec, kspec, bspec, qfspec, kfspec, fshape, qrows, krows = _attn_layout(dil, L)
    nstep = grid[3]

    def body(q_ref, k_ref, v_ref, b_ref, do_ref, o_ref, l_ref, dq_ref, dk_ref, dv_ref, db_ref, db_s,
             kp, vp, dkp, dvp, carry):
        hp, step = pl.program_id(0), pl.program_id(3)
        first = (hp == 0) & (pl.program_id(1) == 0) & (pl.program_id(2) == 0) & (step == 0)
        last = ((hp == grid[0] - 1) & (pl.program_id(1) == grid[1] - 1) & (pl.program_id(2) == grid[2] - 1)
                & (step == nstep - 1))

        @pl.when(first)
        def _():
            db_s[...] = jnp.zeros_like(db_s)

        @pl.when(step == 0)
        def _():
            _fill_padded(k_ref, kp)
            _fill_padded(v_ref, vp)
            carry[...] = jnp.zeros_like(carry)

        carry_k, carry_v = carry[0], carry[1]
        h0 = _head_masks()
        for sub in range(nsub):
            qb = step * nsub + sub
            st = _win_start(qb)
            var = _variant(qb, nq)
            kw = kp[pl.ds(st, K_WIN), :]
            vw = vp[pl.ds(st, K_WIN), :]
            qs = q_ref[sub * Q_BLK:(sub + 1) * Q_BLK, :] * ATTN_SCALE
            dof = do_ref[qrows(sub)]
            dob = dof.astype(BF16)
            prod = dof * o_ref[qrows(sub)]
            lsev = l_ref[qrows(sub)]
            dk_acc = jnp.zeros((K_WIN, 128), F32)
            dv_acc = jnp.zeros((K_WIN, 128), F32)
            dqs = []
            for h in range(2):
                hm = h0 if h == 0 else ~h0
                qh = jnp.where(hm, qs, jnp.zeros_like(qs))
                doh = jnp.where(hm, dob, jnp.zeros_like(dob))
                s = _dot_nt(qh, kw) + b_ref[var, h]
                p = jnp.exp(s - lsev[:, h * HEAD_DIM:h * HEAD_DIM + 1])
                dp = _dot_nt(doh, vw)
                dd = jnp.sum(jnp.where(hm, prod, 0.0), axis=-1, keepdims=True)
                ds = p * (dp - dd)
                db_s[var, hp * 2 + h] += ds
                dsb = ds.astype(BF16)
                dv_acc = dv_acc + _dot_tn(p.astype(BF16), doh)
                dk_acc = dk_acc + _dot_tn(dsb, qh)
                dqs.append(_dot(dsb, kw) * ATTN_SCALE)
            dq_ref[qrows(sub)] = jnp.where(h0, dqs[0], dqs[1])
            dkp[pl.ds(st, Q_BLK), :] = carry_k + dk_acc[0:Q_BLK]
            dvp[pl.ds(st, Q_BLK), :] = carry_v + dv_acc[0:Q_BLK]
            carry_k, carry_v = dk_acc[Q_BLK:K_WIN], dv_acc[Q_BLK:K_WIN]
        carry[0] = carry_k
        carry[1] = carry_v

        @pl.when(step == nstep - 1)
        def _():
            dkp[L:L + Q_BLK, :] = carry_k
            dvp[L:L + Q_BLK, :] = carry_v
            dk_ref[krows()] = dkp[HALF_WIN:HALF_WIN + L, :]
            dv_ref[krows()] = dvp[HALF_WIN:HALF_WIN + L, :]

        @pl.when(last)
        def _():
            db_ref[...] = db_s[...]

    dbshape = (3, N_HEADS, Q_BLK, K_WIN)
    return _call(
        body, f"attn_bwd_d{dil}", grid,
        [qspec, kspec, kspec, bspec, qfspec, qfspec, qfspec],
        [qfspec, kfspec, kfspec, _whole_vmem()],
        [fshape, fshape, fshape, jax.ShapeDtypeStruct(dbshape, F32)],
        [pltpu.VMEM(dbshape, F32)] + [pltpu.VMEM((L + 2 * HALF_WIN, 128), BF16)] * 2
        + [pltpu.VMEM((L + 2 * HALF_WIN, 128), F32)] * 2 + [pltpu.VMEM((2, Q_BLK, 128), F32)],
        (q, k, v, bias, do, o, lse), rider)


def _bucket_onehots(dil):
    m = np.zeros((3, K_WIN, N_BUCKETS), np.float32)
    for var, off in enumerate(_VARIANT_OFFSETS):
        for rel in range(-HALF_WIN, HALF_WIN + 1):
            col = (rel - off + Q_BLK - 1) % K_WIN
            m[var, col, int(_t5_bucket_np(np.asarray(rel * dil)))] = 1.0
    return jnp.asarray(m)


def _bias_grad(dbs):
    onehots = [_bucket_onehots(d) for d in DILATIONS]
    flip = jnp.asarray(np.eye(Q_BLK, dtype=np.float32)[::-1].copy())

    def body(d1, d2, d3, m1, m2, m3, flip_ref, out_ref):
        hp = lax.Precision.HIGHEST
        acc = jnp.zeros((N_HEADS, N_BUCKETS), F32)
        for d_ref, m_ref in ((d1, m1), (d2, m2), (d3, m3)):
            for var in range(3):
                rows = []
                for h in range(N_HEADS):
                    xrev = jnp.dot(flip_ref[...], d_ref[var, h], precision=hp, preferred_element_type=F32)
                    y = pltpu.roll(xrev, 0, 1, stride=1, stride_axis=0)
                    rows.append(jnp.sum(y, axis=0, keepdims=True))
                vec = jnp.concatenate(rows, axis=0)
                acc = acc + jnp.dot(vec, m_ref[var], precision=hp, preferred_element_type=F32)
        out_ref[...] = acc

    return pl.pallas_call(
        body, name="bias_grad", out_shape=jax.ShapeDtypeStruct((N_HEADS, N_BUCKETS), F32),
        compiler_params=_params(0),
    )(*dbs, *onehots, flip)


def _block_diag(w):
    eye = jnp.eye(N_RNN_BLOCKS, dtype=w.dtype)
    return jnp.einsum("ncd,nm->ncmd", w, eye).reshape(D_RNN, D_RNN).astype(BF16)


def _diag_blocks(dense):
    d = dense.reshape(N_RNN_BLOCKS, RNN_BLOCK, N_RNN_BLOCKS, RNN_BLOCK)
    return jnp.stack([d[n, :, n, :] for n in range(N_RNN_BLOCKS)])


EARLY = ("w_out", "w_up", "w_down")


def _local_step(x, target, p, shards=None):
    p = dict(p)
    biases = _bias_mats(p["rel_bias"])
    lru = {}
    for dname in ("fwd", "bwd"):
        lru[dname] = (_block_diag(p["lru_wa_" + dname]), p["lru_ba_" + dname], _block_diag(p["lru_wx_" + dname]),
                      p["lru_bx_" + dname], p["lru_lam_" + dname])

    def gather(name):
        return None if shards is None else _gather_rider([name], [shards[name]])

    (xr, gate, *qkv), got = _inproj_fwd(x, p["attn_norm_g"], p["w_in"], gather("w_out"))
    p.update(zip(["w_out"], got))
    qs, ks, vs = qkv[0:3], qkv[3:6], qkv[6:9]
    hf, got = _rnn_fwd(xr, p["conv_w"], p["conv_b"], *lru["fwd"], reverse=False, rider=gather("w_up"))
    p.update(zip(["w_up"], got))
    hb, got = _rnn_fwd(xr, p["conv_w"], p["conv_b"], *lru["bwd"], reverse=True, rider=gather("w_down"))
    p.update(zip(["w_down"], got))
    outs, lses = [], []
    for q, k, v, bias in zip(qs, ks, vs, biases):
        o, l = _attn_fwd(q, k, v, bias)
        outs.append(o)
        lses.append(l)
    x1, mixb, *yl = _mix_fwd(outs, lses, hf, hb, gate, x, p["norm_rnn_g"], p["norm_attn_g"], p["w_out"])
    yas, lsts = [yl[0], yl[1], yl[1]], [yl[2], yl[3], yl[3]]
    dx1, h2b, a2b, dub, dx2b, loss_vec, dg_fin, dg_mlp = _mlp_fwd_bwd(
        x1, target, p["mlp_norm_g"], p["final_norm_g"], p["w_up"], p["w_down"])
    dhs, dgate, dya1, dya4, dw_out, dg_rnn, dg_attn = _mix_bwd(dx1, p["w_out"], mixb, yas[0], hf, hb, gate,
                                                               p["norm_rnn_g"], p["norm_attn_g"])
    dyas = [dya1, dya4, dya4]
    dw_up = _dw_matmul(h2b, dub, D_MODEL, FF_BLK, "dw_up")
    dw_down = _dw_matmul(a2b, dx2b, FF_BLK, D_MODEL, "dw_down")
    early = [dw_out, dw_up, dw_down]
    dqs, dks, dvs, dbs = [], [], [], []
    for i, (q, k, v, bias, dya, ya, lse) in enumerate(zip(qs, ks, vs, biases, dyas, yas, lsts)):
        rider = None
        if shards is not None:
            make = (lambda: _pair_exchange_rider(EARLY, early), lambda: _chip_exchange_rider(early),
                    lambda: _pair_share_rider(EARLY, early))[i]
            rider = make()
        (dq, dk, dv, db), got = _attn_bwd(q, k, v, bias, dya, ya, lse, rider)
        if shards is not None and i == 0:
            core = lax.axis_index("c").reshape(1).astype(jnp.int32)
            early = [_pair_add(core, g, o, "grad_pair_add_" + n) for n, g, o in zip(EARLY, early, got)]
        elif shards is not None and i == 1:
            early = [_chip_sum(t, "grad_chip_sum_" + n) for n, t in zip(EARLY, got)]
        elif shards is not None:
            early = got
        dqs.append(dq)
        dks.append(dk)
        dvs.append(dv)
        dbs.append(db)
    d_rel_bias = _bias_grad(dbs).T
    dxc_f, dwa_f, dwx_f, dvec_f = _rnn_bwd(xr, hf, dhs, p["conv_w"], p["conv_b"], *lru["fwd"], reverse=False)
    dxc_b, dwa_b, dwx_b, dvec_b = _rnn_bwd(xr, hb, dhs, p["conv_w"], p["conv_b"], *lru["bwd"], reverse=True)
    small = {
        "lru_wa_fwd": _diag_blocks(dwa_f), "lru_ba_fwd": dvec_f[0:1], "lru_wx_fwd": _diag_blocks(dwx_f),
        "lru_bx_fwd": dvec_f[1:2], "lru_lam_fwd": dvec_f[3:4],
        "lru_wa_bwd": _diag_blocks(dwa_b), "lru_ba_bwd": dvec_b[0:1], "lru_wx_bwd": _diag_blocks(dwx_b),
        "lru_bx_bwd": dvec_b[1:2], "lru_lam_bwd": dvec_b[3:4],
        "rel_bias": d_rel_bias, "norm_rnn_g": dg_rnn, "norm_attn_g": dg_attn,
        "mlp_norm_g": dg_mlp, "final_norm_g": dg_fin,
    }
    loss_local = (0.5 / D_MODEL) * jnp.sum(loss_vec)
    rider = None
    if shards is not None:
        rider = _small_gather_rider(_pack([small[n].reshape(shp) for n, shp in SMALL if n in small]
                                          + [loss_local.reshape(1)]))
    (grad_x, dw_in, dg1, dconv), gathered = _inproj_bwd(x, dx1, xr, dxc_f, dxc_b, dgate, dqs, dks, dvs,
                                                        p["attn_norm_g"], p["conv_w"], p["w_in"], rider)
    last = {"attn_norm_g": dg1, "conv_w": dconv[0:4], "conv_b": dconv[4:5]}
    if shards is None:
        big = {"w_in": dw_in, "w_out": dw_out, "w_up": dw_up, "w_down": dw_down}
        return loss_local, grad_x, {**small, **last}, None, big, {}
    return loss_local, grad_x, last, gathered[0], {"w_in": dw_in}, dict(zip(EARLY, early))


BIG = ("w_in", "w_out", "w_up", "w_down")
BIG_SHARD = {"w_in": (D_MODEL, IN_BLK), "w_out": (OUT_BLK, D_MODEL), "w_up": (D_MODEL, FF_BLK), "w_down": (FF_BLK, D_MODEL)}
N_BIG = len(BIG)
N_CHIP_PEERS = 3
ANY = pl.BlockSpec(memory_space=pl.ANY)


def _place():
    x, y, c = lax.axis_index("x"), lax.axis_index("y"), lax.axis_index("c")
    chips = [(1 - x, y), (x, 1 - y), (1 - x, 1 - y)]
    return x, y, c, chips


def _remote(src, dst, send_sem, recv_sem, dev):
    return pltpu.make_async_remote_copy(src_ref=src, dst_ref=dst, send_sem=send_sem, recv_sem=recv_sem,
                                        device_id=dev, device_id_type=MESH)


def _staged_start(srcs, bufs, sems):
    legs = [pltpu.make_async_copy(s, b, sems.at[i]) for i, (s, b) in enumerate(zip(srcs, bufs))]
    for cp in legs:
        cp.start()
    return legs


def _staged_finish(legs, bufs, dsts, sems):
    out = []
    for i, (leg, b, d) in enumerate(zip(legs, bufs, dsts)):
        leg.wait()
        cp = pltpu.make_async_copy(b, d, sems.at[i])
        cp.start()
        out.append(cp)
    return out


class _Rider:
    def __init__(self, inputs, out_shape, scratch, first, late, last):
        self.inputs, self.out_shape, self.scratch = list(inputs), list(out_shape), list(scratch)
        self.first, self.late, self.last = first, late, last


def _call(body, name, grid, in_specs, out_specs, out_shape, scratch, operands, rider=None):
    n_grid = len(grid)
    if rider is None:
        res = pl.pallas_call(body, grid=grid, name=name, in_specs=in_specs, out_specs=out_specs, out_shape=out_shape,
                             scratch_shapes=scratch, compiler_params=_params(n_grid))(*operands)
        return list(res), []
    n_in, n_out, n_scr = len(in_specs), len(out_specs), len(scratch)
    ri, ro = len(rider.inputs), len(rider.out_shape)
    nsteps = int(np.prod(grid))
    late_step = max(nsteps - 3, 1)

    def wrapped(*refs):
        a, b = n_in, n_in + ri
        c, d = b + n_out, b + n_out + ro
        e = d + n_scr
        mine = refs[:a] + refs[b:c] + refs[d:e]
        theirs = (refs[a:b], refs[c:d], refs[e:])
        step = pl.program_id(0)
        for ax in range(1, n_grid):
            step = step * grid[ax] + pl.program_id(ax)
        pl.when(step == 0)(lambda: rider.first(*theirs))
        pl.when(step == late_step)(lambda: rider.late(*theirs))
        body(*mine)
        pl.when(step == nsteps - 1)(lambda: rider.last(*theirs))

    res = pl.pallas_call(
        wrapped, grid=grid, name=name, in_specs=list(in_specs) + [ANY] * ri, out_specs=list(out_specs) + [ANY] * ro,
        out_shape=list(out_shape) + rider.out_shape, scratch_shapes=list(scratch) + rider.scratch,
        compiler_params=_params(n_grid),
    )(*operands, *rider.inputs)
    return list(res[:n_out]), list(res[n_out:])


def _run_rider(rider, name):
    ri, ro = len(rider.inputs), len(rider.out_shape)

    def body(*refs):
        parts = (refs[:ri], refs[ri:ri + ro], refs[ri + ro:])
        rider.first(*parts)
        rider.late(*parts)
        rider.last(*parts)

    return list(pl.pallas_call(
        body, name=name, in_specs=[ANY] * ri, out_specs=[ANY] * ro, out_shape=rider.out_shape, scratch_shapes=rider.scratch,
        compiler_params=pltpu.CompilerParams(has_side_effects=True, vmem_limit_bytes=VMEM_LIMIT),
    )(*rider.inputs))


def _nothing(ins, outs, scr):
    return None


def _gather_rider(names, shards, conv_w=None):
    n = len(names)
    items = n + (conv_w is not None)
    halves = [BIG_SHARD[nm][0] // 2 for nm in names]

    def parts(ins, outs, scr):
        x, y, c, chips = _place()
        return x, y, c, chips, 2 * x + y, (x, y, 1 - c), scr[:8], scr[8:]

    def piece(outs, w, chip, core_half):
        return outs[w].at[chip, pl.ds(core_half * halves[w], halves[w])]

    def ici(ins, outs, sems, w, k, chip_xy, c, me):
        return _remote(ins[w].at[pl.ds(c * halves[w], halves[w])], piece(outs, w, me, c),
                       sems[0].at[w, k], sems[1].at[w, k], (*chip_xy, c))

    def first(ins, outs, scr):
        x, y, c, chips, me, sibling, sems, bufs = parts(ins, outs, scr)
        legs = _staged_start(ins, bufs, sems[6])
        for w in range(n):
            for k, chip_xy in enumerate(chips):
                ici(ins, outs, sems, w, k, chip_xy, c, me).start()
        if conv_w is not None:
            for k, (px, py) in enumerate(chips):
                _remote(ins[n], outs[n].at[me], sems[4].at[k], sems[5].at[k], (px, py, c)).start()
        _staged_finish(legs, bufs, [o.at[me] for o in outs], sems[7])

    def late(ins, outs, scr):
        x, y, c, chips, me, sibling, sems, bufs = parts(ins, outs, scr)
        for w in range(n):
            for k, (px, py) in enumerate(chips):
                landed = piece(outs, w, 2 * px + py, c)
                _remote(landed, landed, sems[0].at[w, k], sems[1].at[w, k], (px, py, c)).wait_recv()
                _remote(landed, landed, sems[2].at[w, k], sems[3].at[w, k], sibling).start()

    def last(ins, outs, scr):
        x, y, c, chips, me, sibling, sems, bufs = parts(ins, outs, scr)
        for w in range(n):
            for k, (px, py) in enumerate(chips):
                other = piece(outs, w, 2 * px + py, 1 - c)
                _remote(other, other, sems[2].at[w, k], sems[3].at[w, k], sibling).wait_recv()
        if conv_w is not None:
            for k, (px, py) in enumerate(chips):
                got = outs[n].at[2 * px + py]
                _remote(got, got, sems[4].at[k], sems[5].at[k], (px, py, c)).wait_recv()
                _remote(ins[n], outs[n].at[me], sems[4].at[k], sems[5].at[k], (px, py, c)).wait_send()
        for i in range(items):
            pltpu.make_async_copy(bufs[i], outs[i].at[me], sems[7].at[i]).wait()
        for w in range(n):
            for k, (px, py) in enumerate(chips):
                ici(ins, outs, sems, w, k, (px, py), c, me).wait_send()
                landed = piece(outs, w, 2 * px + py, c)
                _remote(landed, landed, sems[2].at[w, k], sems[3].at[w, k], sibling).wait_send()

    out_shape = [jax.ShapeDtypeStruct((N_SHARD,) + BIG_SHARD[nm], BF16) for nm in names]
    stage = [pltpu.VMEM(BIG_SHARD[nm], BF16) for nm in names]
    inputs = list(shards)
    if conv_w is not None:
        out_shape.append(jax.ShapeDtypeStruct((N_SHARD,) + conv_w.shape, F32))
        stage.append(pltpu.VMEM(conv_w.shape, F32))
        inputs.append(conv_w)
    scratch = ([pltpu.SemaphoreType.DMA((n, N_CHIP_PEERS))] * 4 + [pltpu.SemaphoreType.DMA((N_CHIP_PEERS,))] * 2
               + [pltpu.SemaphoreType.DMA((items,))] * 2 + stage)
    return _Rider(inputs, out_shape, scratch, first, late, last)


def _pair_exchange_rider(names, grads):
    def copies(ins, outs, scr):
        x, y, c, _ = _place()
        out = []
        for w, nm in enumerate(names):
            h = BIG_SHARD[nm][0] // 2
            out.append(_remote(ins[w].at[:, pl.ds((1 - c) * h, h), :], outs[w], scr[0].at[w], scr[1].at[w], (x, y, 1 - c)))
        return out

    def first(ins, outs, scr):
        for cp in copies(ins, outs, scr):
            cp.start()

    def last(ins, outs, scr):
        for cp in copies(ins, outs, scr):
            cp.wait()

    out_shape = [jax.ShapeDtypeStruct((N_SHARD, BIG_SHARD[nm][0] // 2, BIG_SHARD[nm][1]), F32) for nm in names]
    return _Rider(grads, out_shape, [pltpu.SemaphoreType.DMA((len(names),))] * 2, first, _nothing, last)


def _pair_add(core, grad, other, name):
    _, r, cols = grad.shape
    h = r // 2
    th = min(h, 256)
    per = h // th

    def body(c_ref, g_ref, o_ref, out_ref):
        out_ref[...] = (g_ref[...] + o_ref[...]).astype(BF16)

    return pl.pallas_call(
        body, name=name,
        grid_spec=pltpu.PrefetchScalarGridSpec(
            num_scalar_prefetch=1, grid=(N_SHARD, per),
            in_specs=[pl.BlockSpec((1, th, cols), lambda j, i, c_ref: (j, c_ref[0] * per + i, 0)),
                      pl.BlockSpec((1, th, cols), lambda j, i, c_ref: (j, i, 0))],
            out_specs=pl.BlockSpec((1, th, cols), lambda j, i, c_ref: (j, i, 0))),
        out_shape=jax.ShapeDtypeStruct((N_SHARD, h, cols), BF16),
        compiler_params=_params(2),
    )(core, grad, other)


def _chip_exchange_rider(parts):
    n = len(parts)

    def sends(ins, outs, scr):
        x, y, c, chips = _place()
        me = 2 * x + y
        return [_remote(ins[w].at[2 * px + py], outs[w].at[me], scr[0].at[w, k], scr[1].at[w, k], (px, py, c))
                for w in range(n) for k, (px, py) in enumerate(chips)]

    def first(ins, outs, scr):
        x, y, c, chips = _place()
        me = 2 * x + y
        legs = _staged_start([r.at[me] for r in ins], scr[4:], scr[2])
        for cp in sends(ins, outs, scr):
            cp.start()
        _staged_finish(legs, scr[4:], [o.at[me] for o in outs], scr[3])

    def last(ins, outs, scr):
        x, y, c, chips = _place()
        me = 2 * x + y
        for w in range(n):
            for k, (px, py) in enumerate(chips):
                got = outs[w].at[2 * px + py]
                _remote(got, got, scr[0].at[w, k], scr[1].at[w, k], (px, py, c)).wait_recv()
        for cp in sends(ins, outs, scr):
            cp.wait_send()
        for w in range(n):
            pltpu.make_async_copy(scr[4 + w], outs[w].at[me], scr[3].at[w]).wait()

    out_shape = [jax.ShapeDtypeStruct(p.shape, BF16) for p in parts]
    scratch = ([pltpu.SemaphoreType.DMA((n, N_CHIP_PEERS))] * 2 + [pltpu.SemaphoreType.DMA((n,))] * 2
               + [pltpu.VMEM(p.shape[1:], BF16) for p in parts])
    return _Rider(parts, out_shape, scratch, first, _nothing, last)


def _chip_sum(parts, name):
    _, h, cols = parts.shape
    th = min(h, 256)

    def body(p_ref, out_ref):
        acc = p_ref[0].astype(F32)
        for j in range(1, N_SHARD):
            acc = acc + p_ref[j].astype(F32)
        out_ref[...] = acc

    return pl.pallas_call(
        body, name=name, grid=(h // th,),
        in_specs=[pl.BlockSpec((N_SHARD, th, cols), lambda i: (0, i, 0))],
        out_specs=pl.BlockSpec((th, cols), lambda i: (i, 0)),
        out_shape=jax.ShapeDtypeStruct((h, cols), F32),
        compiler_params=_params(),
    )(parts)


def _pair_share_rider(names, halves):
    n = len(names)
    hs = [BIG_SHARD[nm][0] // 2 for nm in names]

    def mine(outs, c):
        return [outs[w].at[pl.ds(c * hs[w], hs[w])] for w in range(n)]

    def first(ins, outs, scr):
        x, y, c, _ = _place()
        legs = _staged_start(ins, scr[4:], scr[2])
        for w, dst in enumerate(mine(outs, c)):
            _remote(ins[w], dst, scr[0].at[w], scr[1].at[w], (x, y, 1 - c)).start()
        _staged_finish(legs, scr[4:], mine(outs, c), scr[3])

    def last(ins, outs, scr):
        x, y, c, _ = _place()
        for w, (theirs, dst) in enumerate(zip(mine(outs, 1 - c), mine(outs, c))):
            _remote(theirs, theirs, scr[0].at[w], scr[1].at[w], (x, y, 1 - c)).wait_recv()
            _remote(ins[w], dst, scr[0].at[w], scr[1].at[w], (x, y, 1 - c)).wait_send()
            pltpu.make_async_copy(scr[4 + w], dst, scr[3].at[w]).wait()

    out_shape = [jax.ShapeDtypeStruct(BIG_SHARD[nm], F32) for nm in names]
    scratch = [pltpu.SemaphoreType.DMA((n,))] * 4 + [pltpu.VMEM((h, BIG_SHARD[nm][1]), F32) for nm, h in zip(names, hs)]
    return _Rider(halves, out_shape, scratch, first, _nothing, last)


N_DEV = 8


def _all_peers(x, y, c):
    return [((1 - x) if fx else x, (1 - y) if fy else y, (1 - c) if fc else c)
            for fx in (0, 1) for fy in (0, 1) for fc in (0, 1) if fx or fy or fc]


def _small_gather_rider(vec):
    def sends(ins, outs, scr):
        x, y, c, _ = _place()
        me = 4 * x + 2 * y + c
        return [_remote(ins[0], outs[0].at[me], scr[0].at[k], scr[1].at[k], dev) for k, dev in enumerate(_all_peers(x, y, c))]

    def first(ins, outs, scr):
        x, y, c, _ = _place()
        legs = _staged_start(ins, scr[4:], scr[2])
        for cp in sends(ins, outs, scr):
            cp.start()
        _staged_finish(legs, scr[4:], [outs[0].at[4 * x + 2 * y + c]], scr[3])

    def last(ins, outs, scr):
        x, y, c, _ = _place()
        for k, (px, py, pc) in enumerate(_all_peers(x, y, c)):
            got = outs[0].at[4 * px + 2 * py + pc]
            _remote(got, got, scr[0].at[k], scr[1].at[k], (px, py, pc)).wait_recv()
        for cp in sends(ins, outs, scr):
            cp.wait_send()
        pltpu.make_async_copy(scr[4], outs[0].at[4 * x + 2 * y + c], scr[3].at[0]).wait()

    scratch = ([pltpu.SemaphoreType.DMA((N_DEV - 1,))] * 2 + [pltpu.SemaphoreType.DMA((1,))] * 2
               + [pltpu.VMEM(vec.shape, F32)])
    return _Rider([vec], [jax.ShapeDtypeStruct((N_DEV,) + vec.shape, F32)], scratch, first, _nothing, last)


def _sum_devices(gathered):
    def body(g_ref, out_ref):
        acc = g_ref[0]
        for j in range(1, N_DEV):
            acc = acc + g_ref[j]
        out_ref[...] = acc

    return pl.pallas_call(body, name="sum_devices", out_shape=jax.ShapeDtypeStruct(gathered.shape[1:], F32),
                          compiler_params=_params(0))(gathered)


def _allreduce_small(vec):
    rows = vec.shape[0]

    def body(v_ref, sum_ref, gat_ref, send, recv, loc_sem):
        x, y, c, _ = _place()
        me = 4 * x + 2 * y + c
        lc = pltpu.make_async_copy(v_ref, gat_ref.at[me], loc_sem)
        lc.start()
        peers = []
        for fx in (0, 1):
            for fy in (0, 1):
                for fc in (0, 1):
                    if fx or fy or fc:
                        peers.append(((1 - x) if fx else x, (1 - y) if fy else y, (1 - c) if fc else c))
        sends = []
        for k, dev in enumerate(peers):
            cp = _remote(v_ref, gat_ref.at[me], send.at[k], recv.at[k], dev)
            cp.start()
            sends.append(cp)
        for k, (px, py, pc) in enumerate(peers):
            got = gat_ref.at[4 * px + 2 * py + pc]
            _remote(got, got, send.at[k], recv.at[k], (px, py, pc)).wait_recv()
        for cp in sends:
            cp.wait_send()
        lc.wait()
        acc = gat_ref[0]
        for j in range(1, N_DEV):
            acc = acc + gat_ref[j]
        sum_ref[...] = acc

    total, _ = pl.pallas_call(
        body, name="allreduce_small",
        in_specs=[_whole_vmem()], out_specs=[_whole_vmem(), _whole_vmem()],
        out_shape=[jax.ShapeDtypeStruct((rows, 128), F32), jax.ShapeDtypeStruct((N_DEV, rows, 128), F32)],
        scratch_shapes=[pltpu.SemaphoreType.DMA((N_DEV - 1,))] * 2 + [pltpu.SemaphoreType.DMA(())],
        compiler_params=pltpu.CompilerParams(has_side_effects=True, vmem_limit_bytes=VMEM_LIMIT),
    )(vec)
    return total


def _adam_math(w_ref, g_ref, m_ref, v_ref, d_ref, m2_ref, v2_ref):
    c1 = 1.0 - ADAM_B1 ** ADAM_STEP
    c2 = 1.0 - ADAM_B2 ** ADAM_STEP
    gv = g_ref[...]
    m2 = ADAM_B1 * m_ref[...] + (1.0 - ADAM_B1) * gv
    v2 = ADAM_B2 * v_ref[...] + (1.0 - ADAM_B2) * (gv * gv)
    m2_ref[...] = m2
    v2_ref[...] = v2
    d_ref[...] = -ADAM_LR * ((m2 / c1) / (jnp.sqrt(v2 / c2) + ADAM_EPS) + ADAM_WD * w_ref[...])


def _adamw_many(ws, gs, ms, vs):
    n = len(ws)

    def body(*refs):
        for i in range(n):
            _adam_math(*[refs[k * n + i] for k in range(7)])

    shapes = [jax.ShapeDtypeStruct(w.shape, F32) for w in ws]
    res = pl.pallas_call(body, name="adamw_small", out_shape=shapes * 3, compiler_params=_params(0))(*ws, *gs, *ms, *vs)
    return res[:n], res[n:2 * n], res[2 * n:]


def _adamw(w, g, m, v, name):
    rows, cols = w.shape
    tr = 256 if rows % 256 == 0 else rows

    def body(w_ref, g_ref, m_ref, v_ref, d_ref, m2_ref, v2_ref):
        _adam_math(w_ref, g_ref, m_ref, v_ref, d_ref, m2_ref, v2_ref)

    spec = pl.BlockSpec((tr, cols), lambda i: (i, 0))
    shp = jax.ShapeDtypeStruct((rows, cols), F32)
    return pl.pallas_call(
        body, name=name, grid=(rows // tr,), in_specs=[spec] * 4, out_specs=[spec] * 3, out_shape=[shp] * 3,
        compiler_params=_params(),
    )(w, g, m, v)


SMALL = (
    ("attn_norm_g", (1, 1024)), ("conv_w", (1, 4, 512)), ("conv_b", (1, 512)),
    ("lru_wa_fwd", (1, 8, 64, 64)), ("lru_ba_fwd", (1, 512)), ("lru_wx_fwd", (1, 8, 64, 64)), ("lru_bx_fwd", (1, 512)),
    ("lru_lam_fwd", (1, 512)),
    ("lru_wa_bwd", (1, 8, 64, 64)), ("lru_ba_bwd", (1, 512)), ("lru_wx_bwd", (1, 8, 64, 64)), ("lru_bx_bwd", (1, 512)),
    ("lru_lam_bwd", (1, 512)),
    ("rel_bias", (32, 8)), ("norm_rnn_g", (1, 512)), ("norm_attn_g", (1, 512)), ("mlp_norm_g", (1, 1024)),
    ("final_norm_g", (1024,)),
)
PACK_ROW = 8 * 128


def _pack(parts):
    flat = jnp.concatenate([p.reshape(-1) for p in parts])
    pad = (-flat.shape[0]) % PACK_ROW
    return jnp.pad(flat, (0, pad)).reshape(-1, 128)


def _unpack(packed, shapes):
    flat = packed.reshape(-1)
    out, off = [], 0
    for shp in shapes:
        n = int(np.prod(shp))
        out.append(flat[off:off + n].reshape(shp))
        off += n
    return out


WEIGHT_ORDER = ("attn_norm_g", "w_in", "conv_w", "conv_b", "lru_wa_fwd", "lru_ba_fwd", "lru_wx_fwd", "lru_bx_fwd",
                "lru_lam_fwd", "lru_wa_bwd", "lru_ba_bwd", "lru_wx_bwd", "lru_bx_bwd", "lru_lam_bwd", "rel_bias",
                "norm_rnn_g", "norm_attn_g", "w_out", "mlp_norm_g", "w_up", "w_down", "final_norm_g")


def kernel(x, attn_norm_g, w_in, conv_w, conv_b, lru_wa_fwd, lru_ba_fwd, lru_wx_fwd, lru_bx_fwd, lru_lam_fwd, lru_wa_bwd, lru_ba_bwd, lru_wx_bwd, lru_bx_bwd, lru_lam_bwd, rel_bias, norm_rnn_g, norm_attn_g, w_out, mlp_norm_g, w_up, w_down, final_norm_g, loss_target, m_attn_norm_g, m_w_in, m_conv_w, m_conv_b, m_lru_wa_fwd, m_lru_ba_fwd, m_lru_wx_fwd, m_lru_bx_fwd, m_lru_lam_fwd, m_lru_wa_bwd, m_lru_ba_bwd, m_lru_wx_bwd, m_lru_bx_bwd, m_lru_lam_bwd, m_rel_bias, m_norm_rnn_g, m_norm_attn_g, m_w_out, m_mlp_norm_g, m_w_up, m_w_down, m_final_norm_g, v_attn_norm_g, v_w_in, v_conv_w, v_conv_b, v_lru_wa_fwd, v_lru_ba_fwd, v_lru_wx_fwd, v_lru_bx_fwd, v_lru_lam_fwd, v_lru_wa_bwd, v_lru_ba_bwd, v_lru_wx_bwd, v_lru_bx_bwd, v_lru_lam_bwd, v_rel_bias, v_norm_rnn_g, v_norm_attn_g, v_w_out, v_mlp_norm_g, v_w_up, v_w_down, v_final_norm_g):
    given = dict(locals())
    w = {n: given[n] for n in WEIGHT_ORDER}
    m = {n: given["m_" + n] for n in WEIGHT_ORDER}
    v = {n: given["v_" + n] for n in WEIGHT_ORDER}

    chip = lax.axis_index("x") * 2 + lax.axis_index("y")
    core = lax.axis_index("c")

    shards = {n: w[n][0].astype(BF16) for n in BIG}
    w_in_all, conv_all = _run_rider(_gather_rider(["w_in"], [shards["w_in"]], w["conv_w"][0]), "allgather_w_in")
    p = {n: (t[0] if t.ndim >= 3 else t) for n, t in w.items() if n not in BIG}
    p["final_norm_g"] = w["final_norm_g"].reshape(1, D_MODEL)
    p["conv_w"] = jnp.transpose(conv_all, (1, 0, 2)).reshape(4, D_RNN)
    p["w_in"] = w_in_all

    _, grad_x, small, gathered, big, reduced = _local_step(x[0], loss_target[0], p, {n: shards[n] for n in EARLY})

    late = tuple(big)
    grads = [big[n] for n in late]
    others = _run_rider(_pair_exchange_rider(late, grads), "grad_pair_exchange")
    core_arr = core.reshape(1).astype(jnp.int32)
    parts = [_pair_add(core_arr, g, o, "grad_pair_add_" + n) for n, g, o in zip(late, grads, others)]
    landed = _run_rider(_chip_exchange_rider(parts), "grad_chip_exchange")
    halves = [_chip_sum(t, "grad_chip_sum_" + n) for n, t in zip(late, landed)]
    reduced.update(zip(late, _run_rider(_pair_share_rider(late, halves), "grad_pair_share")))

    early_small = [(n, shp) for n, shp in SMALL if n not in small]
    late_small = [(n, shp) for n, shp in SMALL if n in small]
    *early_g, loss = _unpack(_sum_devices(gathered), [shp for _, shp in early_small] + [(1,)])
    late_g = _unpack(_allreduce_small(_pack([small[n].reshape(shp) for n, shp in late_small])),
                     [shp for _, shp in late_small])
    g = dict(zip([n for n, _ in early_small + late_small], early_g + late_g))
    g["conv_w"] = lax.dynamic_slice_in_dim(g["conv_w"], chip * (D_RNN // N_SHARD), D_RNN // N_SHARD, axis=2)
    for n in BIG:
        g[n] = reduced[n][None]

    delta, new_m, new_v = {}, {}, {}
    for n in BIG:
        d2, m2, v2 = _adamw(w[n][0], reduced[n], m[n][0], v[n][0], "adamw_" + n)
        delta[n], new_m[n], new_v[n] = d2[None], m2[None], v2[None]
    names = [n for n, _ in SMALL]
    for dst, src in zip((delta, new_m, new_v), _adamw_many(*[[t[n] for n in names] for t in (w, g, m, v)])):
        dst.update(dict(zip(names, src)))

    return (loss.reshape(()), grad_x[None], *[g[n] for n in WEIGHT_ORDER], *[delta[n] for n in WEIGHT_ORDER],
            *[new_m[n] for n in WEIGHT_ORDER], *[new_v[n] for n in WEIGHT_ORDER])
```

```python
import functools
import math

import numpy as np
import jax
import jax.numpy as jnp
from jax import lax
from jax.experimental import pallas as pl
from jax.experimental.pallas import tpu as pltpu

F32 = jnp.float32
BF16 = jnp.bfloat16

D_MODEL = 1024
D_RNN = 512
D_ATTN = 512
N_HEADS = 8
HEAD_DIM = 64
N_RNN_BLOCKS = 8
RNN_BLOCK = 64
D_IN = 2 * D_RNN + 3 * D_ATTN
D_FF = 4 * D_MODEL
N_SHARD = 4
IN_BLK = D_IN // N_SHARD
OUT_BLK = D_MODEL // N_SHARD
FF_BLK = D_FF // N_SHARD
EPS = 1e-6
NEG_INF = -1e30
LRU_C = 8.0
DILATIONS = (1, 4, 16)
F32_LAYOUT = 4
HALF_WIN = 64
Q_BLK = 128
K_WIN = 256
N_BUCKETS = 32
MAX_DISTANCE = 1024
ATTN_SCALE = HEAD_DIM ** -0.5

ADAM_LR = 0.001
ADAM_B1 = 0.9
ADAM_B2 = 0.999
ADAM_EPS = 1e-08
ADAM_WD = 0.01
ADAM_STEP = 10

TS = 512
TS_MLP = 256
TS_INPROJ_BWD = 256
ATTN_SUB = 16
SCAN_UNROLL = 4
SUB = 8
VMEM_LIMIT = 56 * 1024 * 1024
GELU_C0 = math.sqrt(2.0 / math.pi)
GELU_C1 = 0.044715

MESH = pl.DeviceIdType.MESH


def _params(n_grid=1):
    return pltpu.CompilerParams(vmem_limit_bytes=VMEM_LIMIT, dimension_semantics=("arbitrary",) * n_grid)


def _whole_vmem():
    return pl.BlockSpec(memory_space=pltpu.VMEM)


def _rows(width, tile=TS):
    return pl.BlockSpec((tile, width), lambda i: (i, 0))


def _sigmoid(z):
    return 0.5 * jnp.tanh(0.5 * z) + 0.5


def _log1p(u):
    w = 1.0 + u
    return jnp.where(w == 1.0, u, jnp.log(w) * (u / (w - 1.0)))


def _softplus(z):
    return jnp.maximum(z, 0.0) + _log1p(jnp.exp(-jnp.abs(z)))


def _gelu_parts(g):
    inner = GELU_C0 * (g + GELU_C1 * g * g * g)
    t = jnp.tanh(inner)
    val = 0.5 * g * (1.0 + t)
    dinner = GELU_C0 * (1.0 + 3.0 * GELU_C1 * g * g)
    grad = 0.5 * (1.0 + t) + 0.5 * g * (1.0 - t * t) * dinner
    return val, grad


def _rms(x):
    rstd = lax.rsqrt(jnp.mean(x * x, axis=-1, keepdims=True) + EPS)
    return rstd, x * rstd


def _rms_bwd(dy, g, xhat, rstd):
    dxh = dy * g
    dx = rstd * (dxh - xhat * jnp.mean(dxh * xhat, axis=-1, keepdims=True))
    dg = jnp.sum(dy * xhat, axis=0, keepdims=True)
    return dx, dg


def _dot(a, b):
    return jnp.dot(a, b, preferred_element_type=F32)


def _dot_nt(a, b):
    return lax.dot_general(a, b, (((1,), (1,)), ((), ())), preferred_element_type=F32)


def _dot_tn(a, b):
    return lax.dot_general(a, b, (((0,), (0,)), ((), ())), preferred_element_type=F32)


def _shifted(tile, prev8, next8, k):
    n = tile.shape[0]
    row = lax.broadcasted_iota(jnp.int32, tile.shape, 0)
    if k == 0:
        return tile
    if k < 0:
        r = pltpu.roll(tile, -k, 0)
        for j in range(-k):
            r = jnp.where(row == j, prev8[SUB + j + k:SUB + j + k + 1, :], r)
        return r
    r = pltpu.roll(tile, n - k, 0)
    for j in range(k):
        r = jnp.where(row == n - k + j, next8[j:j + 1, :], r)
    return r


def _to_lane_blocks(val, s_ref):
    for j in range(val.shape[1] // 128):
        s_ref[j] = val[:, j * 128:(j + 1) * 128]


def _from_lane_blocks(s_ref):
    return jnp.concatenate([s_ref[j] for j in range(s_ref.shape[0])], axis=-1)


def _class_rows(s_ref, r, dil):
    n = s_ref.shape[1] // dil
    return jnp.concatenate([s_ref[j, pl.ds(r, n, stride=dil), :] for j in range(s_ref.shape[0])], axis=-1)


def _split_classes(val, s_ref, out_ref, dil):
    _to_lane_blocks(val, s_ref)
    for r in range(dil):
        out_ref[r] = _class_rows(s_ref, r, dil).astype(out_ref.dtype)


def _merge_classes(in_ref, s_ref, dil, also_ref=None):
    n = s_ref.shape[1] // dil
    for r in range(dil):
        v = in_ref[r] if also_ref is None else in_ref[r] + also_ref[r]
        for j in range(s_ref.shape[0]):
            s_ref[j, pl.ds(r, n, stride=dil), :] = v[:, j * 128:(j + 1) * 128]
    return _from_lane_blocks(s_ref)


def _class_spec(dil, tile=TS):
    return pl.BlockSpec((dil, tile // dil, 512), lambda i: (0, i, 0))


def _class_shape(S, dil, dtype):
    return jax.ShapeDtypeStruct((dil, S // dil, 512), dtype)


def _scan_tile(a_ref, b_ref, h_ref, carry_ref, reverse):
    n = a_ref.shape[0]
    width = a_ref.shape[1]
    groups = n // SUB
    row = lax.broadcasted_iota(jnp.int32, (SUB, width), 0)

    def group_scan(g):
        r0 = pl.multiple_of(g * SUB, SUB)
        a = a_ref[pl.ds(r0, SUB), :]
        b = b_ref[pl.ds(r0, SUB), :]
        for s in (1, 2, 4):
            if reverse:
                a_sh = pltpu.roll(a, SUB - s, 0)
                b_sh = pltpu.roll(b, SUB - s, 0)
                m = row < SUB - s
            else:
                a_sh = pltpu.roll(a, s, 0)
                b_sh = pltpu.roll(b, s, 0)
                m = row >= s
            b = jnp.where(m, a * b_sh + b, b)
            a = jnp.where(m, a * a_sh, a)
        return r0, a, b

    def step(i, carry):
        first = i * SCAN_UNROLL
        order = [(groups - 1 - (first + u)) if reverse else (first + u) for u in range(SCAN_UNROLL)]
        scans = [group_scan(g) for g in order]
        for r0, a, b in scans:
            h = b + a * carry
            h_ref[pl.ds(r0, SUB), :] = h
            edge = h[0:1, :] if reverse else h[SUB - 1:SUB, :]
            carry = jnp.broadcast_to(edge, (SUB, width))
        return carry

    carry_ref[...] = lax.fori_loop(0, groups // SCAN_UNROLL, step, carry_ref[...])


def _conv_fwd(xr, prev8, next8, cw, cb):
    y = cb + _shifted(xr, prev8, next8, -2) * cw[0:1, :]
    y = y + _shifted(xr, prev8, next8, -1) * cw[1:2, :]
    y = y + xr * cw[2:3, :]
    y = y + _shifted(xr, prev8, next8, 1) * cw[3:4, :]
    return y


def _lru_gates(xc, wa_ref, ba, wx_ref, bx, lam):
    xcb = xc.astype(BF16)
    r = _sigmoid(_dot(xcb, wa_ref[...]) + ba)
    i = _sigmoid(_dot(xcb, wx_ref[...]) + bx)
    cl = -LRU_C * _softplus(-lam)
    la = cl * r
    a = jnp.exp(la)
    m2 = -jnp.tanh(la) * (a * a + 1.0)
    inv = lax.rsqrt(m2)
    mult = jnp.where(m2 > 0.0, m2 * inv, 0.0)
    return xcb, r, i, cl, a, mult, inv


def _inproj_fwd(x, g1, w_in, rider=None):
    S = x.shape[0]

    def body(x_ref, g_ref, w_ref, xr_ref, gate_ref, *rest):
        qkv_refs, s_ref = rest[:9], rest[9]
        _, xh = _rms(x_ref[...])
        h = (xh * g_ref[...]).astype(BF16)
        proj = jnp.concatenate([_dot(h, w_ref[j]) for j in range(N_SHARD)], axis=-1)
        xr_ref[...] = proj[:, 0:512]
        gate_ref[...] = proj[:, 512:1024]
        for t in range(3):
            val = proj[:, 1024 + 512 * t:1536 + 512 * t]
            qkv_refs[3 * t][0] = val.astype(BF16)
            _to_lane_blocks(val, s_ref)
            for p, dil in enumerate(DILATIONS[1:]):
                for r in range(dil):
                    qkv_refs[3 * t + 1 + p][r] = _class_rows(s_ref, r, dil).astype(BF16)

    f = jax.ShapeDtypeStruct((S, 512), F32)
    return _call(
        body, "inproj_fwd", (S // TS,),
        [_rows(D_MODEL), _whole_vmem(), _whole_vmem()],
        [_rows(512)] * 2 + [_class_spec(d) for d in DILATIONS] * 3,
        [f, f] + [_class_shape(S, d, BF16) for d in DILATIONS] * 3,
        [pltpu.VMEM((4, TS, 128), F32)], (x, g1, w_in), rider)


def _halo_specs(S, order, tile=TS):
    per = tile // SUB
    last = S // SUB - 1
    return [
        pl.BlockSpec((tile, 512), lambda i: (order(i), 0)),
        pl.BlockSpec((SUB, 512), lambda i: (jnp.maximum(order(i) * per - 1, 0), 0)),
        pl.BlockSpec((SUB, 512), lambda i: (jnp.minimum((order(i) + 1) * per, last), 0)),
    ]


def _rnn_fwd(xr, conv_w, conv_b, wa, ba, wx, bx, lam, reverse, rider=None):
    S = xr.shape[0]
    nt = S // TS
    order = (lambda i: nt - 1 - i) if reverse else (lambda i: i)

    def body(x_ref, xp_ref, xn_ref, cw_ref, cb_ref, wa_ref, ba_ref, wx_ref, bx_ref, lam_ref, h_ref, a_s, b_s, carry):
        i = pl.program_id(0)
        t = order(i)

        @pl.when(i == 0)
        def _():
            carry[...] = jnp.zeros_like(carry)

        prev8 = jnp.where(t > 0, xp_ref[...], 0.0)
        next8 = jnp.where(t < nt - 1, xn_ref[...], 0.0)
        xc = _conv_fwd(x_ref[...], prev8, next8, cw_ref[...], cb_ref[...])
        _, _, gi, _, a, mult, _ = _lru_gates(xc, wa_ref, ba_ref[...], wx_ref, bx_ref[...], lam_ref[...])
        a_s[...] = a
        b_s[...] = mult * (gi * xc)
        _scan_tile(a_s, b_s, h_ref, carry, reverse)

    (h,), carried = _call(
        body, "rnn_fwd_rev" if reverse else "rnn_fwd_fwd", (nt,),
        _halo_specs(S, order) + [_whole_vmem()] * 7,
        [pl.BlockSpec((TS, 512), lambda i: (order(i), 0))],
        [jax.ShapeDtypeStruct((S, 512), F32)],
        [pltpu.VMEM((TS, 512), F32), pltpu.VMEM((TS, 512), F32), pltpu.VMEM((SUB, 512), F32)],
        (xr, xr, xr, conv_w, conv_b, wa, ba, wx, bx, lam), rider)
    return h, carried


def _mix_fwd(o3, l3, hf, hb, gate, x, g_rnn, g_attn, w_out):
    S = x.shape[0]

    def body(o1, o2, o3_, l1, l2, l3_, hf_ref, hb_ref, gate_ref, x_ref, gr_ref, ga_ref, w_ref,
             x1_ref, mix_ref, ya1, ya2, ls1, ls2, s_ref):
        la, lb, lc = l1[0], _merge_classes(l2, s_ref, F32_LAYOUT), _merge_classes(l3_, s_ref, F32_LAYOUT)
        m = jnp.maximum(jnp.maximum(la, lb), lc)
        ea, eb, ec = jnp.exp(la - m), jnp.exp(lb - m), jnp.exp(lc - m)
        den = ea + eb + ec
        lse = m + jnp.log(den)
        ya = (ea * o1[0] + eb * _merge_classes(o2, s_ref, F32_LAYOUT) + ec * _merge_classes(o3_, s_ref, F32_LAYOUT)) / den
        ya1[0] = ya
        ls1[0] = lse
        _split_classes(ya, s_ref, ya2, F32_LAYOUT)
        _split_classes(lse, s_ref, ls2, F32_LAYOUT)
        gg, _ = _gelu_parts(gate_ref[...])
        yr = (hf_ref[...] + hb_ref[...]) * gg
        _, xh_r = _rms(yr)
        _, xh_a = _rms(ya)
        mix = jnp.concatenate([xh_r * gr_ref[...], xh_a * ga_ref[...]], axis=-1).astype(BF16)
        mix_ref[...] = mix
        acc = x_ref[...]
        for j in range(N_SHARD):
            acc = acc + _dot(mix[:, j * OUT_BLK:(j + 1) * OUT_BLK], w_ref[j])
        x1_ref[...] = acc

    one, four = _class_spec(1), _class_spec(F32_LAYOUT)
    return pl.pallas_call(
        body, grid=(S // TS,), name="mix_fwd",
        in_specs=[one, four, four] * 2 + [_rows(512)] * 3 + [_rows(D_MODEL)] + [_whole_vmem()] * 3,
        out_specs=[_rows(D_MODEL), _rows(D_MODEL)] + [one, four] * 2,
        out_shape=[jax.ShapeDtypeStruct((S, D_MODEL), F32), jax.ShapeDtypeStruct((S, D_MODEL), BF16)]
        + [_class_shape(S, 1, F32), _class_shape(S, F32_LAYOUT, F32)] * 2,
        scratch_shapes=[pltpu.VMEM((4, TS, 128), F32)],
        compiler_params=_params(),
    )(*o3, *l3, hf, hb, gate, x, g_rnn, g_attn, w_out)


def _mlp_fwd_bwd(x1, target, g_mlp, g_fin, w_up, w_down):
    S = x1.shape[0]
    tm = TS_MLP

    def body(x1_ref, t_ref, gm_ref, gf_ref, wu_ref, wd_ref,
             dx1_ref, h2_ref, a2_ref, du_ref, dx2_ref, loss_ref, dgf_ref, dgm_ref, relu_s):
        @pl.when(pl.program_id(0) == 0)
        def _():
            loss_ref[...] = jnp.zeros_like(loss_ref)
            dgf_ref[...] = jnp.zeros_like(dgf_ref)
            dgm_ref[...] = jnp.zeros_like(dgm_ref)

        x1v = x1_ref[...]
        rstd1, xh1 = _rms(x1v)
        h2 = (xh1 * gm_ref[...]).astype(BF16)
        h2_ref[...] = h2
        x2 = x1v
        for j in range(N_SHARD):
            r = jnp.maximum(_dot(h2, wu_ref[j]), 0.0)
            relu_s[j] = r
            a2 = (r * r).astype(BF16)
            a2_ref[:, j * FF_BLK:(j + 1) * FF_BLK] = a2
            x2 = x2 + _dot(a2, wd_ref[j])
        rstd2, xh2 = _rms(x2)
        err = xh2 * gf_ref[...] - t_ref[...]
        loss_ref[...] += jnp.sum(err * err, axis=0, keepdims=True)
        dy = err * (1.0 / D_MODEL)
        dx2, dgf = _rms_bwd(dy, gf_ref[...], xh2, rstd2)
        dgf_ref[...] += dgf
        dx2b = dx2.astype(BF16)
        dx2_ref[...] = dx2b
        dh2 = jnp.zeros((tm, D_MODEL), F32)
        for j in range(N_SHARD):
            du = (_dot_nt(dx2b, wd_ref[j]) * (2.0 * relu_s[j])).astype(BF16)
            du_ref[:, j * FF_BLK:(j + 1) * FF_BLK] = du
            dh2 = dh2 + _dot_nt(du, wu_ref[j])
        dx1n, dgm = _rms_bwd(dh2, gm_ref[...], xh1, rstd1)
        dgm_ref[...] += dgm
        dx1_ref[...] = dx2 + dx1n

    vec = jax.ShapeDtypeStruct((1, D_MODEL), F32)
    return pl.pallas_call(
        body, grid=(S // tm,), name="mlp_fwd_bwd",
        in_specs=[_rows(D_MODEL, tm), _rows(D_MODEL, tm)] + [_whole_vmem()] * 4,
        out_specs=[_rows(D_MODEL, tm), _rows(D_MODEL, tm), _rows(D_FF, tm), _rows(D_FF, tm), _rows(D_MODEL, tm)]
        + [_whole_vmem()] * 3,
        out_shape=[jax.ShapeDtypeStruct((S, D_MODEL), F32), jax.ShapeDtypeStruct((S, D_MODEL), BF16),
                   jax.ShapeDtypeStruct((S, D_FF), BF16), jax.ShapeDtypeStruct((S, D_FF), BF16),
                   jax.ShapeDtypeStruct((S, D_MODEL), BF16), vec, vec, vec],
        scratch_shapes=[pltpu.VMEM((N_SHARD, tm, FF_BLK), F32)],
        compiler_params=_params(),
    )(x1, target, g_mlp, g_fin, w_up, w_down)


def _mix_bwd(dx1, w_out, mixb, ya, hf, hb, gate, g_rnn, g_attn):
    S = dx1.shape[0]

    def body(dx1_ref, w_ref, mix_ref, ya_ref, hf_ref, hb_ref, gate_ref, gr_ref, ga_ref,
             dhs_ref, dgate_ref, dya1, dya2, dw_ref, dgr_ref, dga_ref, s_ref):
        @pl.when(pl.program_id(0) == 0)
        def _():
            dw_ref[...] = jnp.zeros_like(dw_ref)
            dgr_ref[...] = jnp.zeros_like(dgr_ref)
            dga_ref[...] = jnp.zeros_like(dga_ref)

        dx1b = dx1_ref[...].astype(BF16)
        mix = mix_ref[...]
        for j in range(N_SHARD):
            dw_ref[j] += _dot_tn(mix[:, j * OUT_BLK:(j + 1) * OUT_BLK], dx1b)
        dmix = jnp.concatenate([_dot_nt(dx1b, w_ref[j]) for j in range(N_SHARD)], axis=-1)
        gg, dgg = _gelu_parts(gate_ref[...])
        hs = hf_ref[...] + hb_ref[...]
        rstd_r, xh_r = _rms(hs * gg)
        dyr, dgr = _rms_bwd(dmix[:, 0:D_RNN], gr_ref[...], xh_r, rstd_r)
        dgr_ref[...] += dgr
        rstd_a, xh_a = _rms(ya_ref[0])
        dya, dga = _rms_bwd(dmix[:, D_RNN:], ga_ref[...], xh_a, rstd_a)
        dga_ref[...] += dga
        dya1[0] = dya
        _split_classes(dya, s_ref, dya2, F32_LAYOUT)
        dhs_ref[...] = dyr * gg
        dgate_ref[...] = dyr * hs * dgg

    f512 = jax.ShapeDtypeStruct((S, 512), F32)
    vec = jax.ShapeDtypeStruct((1, 512), F32)
    return pl.pallas_call(
        body, grid=(S // TS,), name="mix_bwd",
        in_specs=[_rows(D_MODEL), _whole_vmem(), _rows(D_MODEL), _class_spec(1)] + [_rows(512)] * 3 + [_whole_vmem()] * 2,
        out_specs=[_rows(512)] * 2 + [_class_spec(1), _class_spec(F32_LAYOUT)] + [_whole_vmem()] * 3,
        out_shape=[f512, f512, _class_shape(S, 1, F32), _class_shape(S, F32_LAYOUT, F32),
                   jax.ShapeDtypeStruct((N_SHARD, OUT_BLK, D_MODEL), F32), vec, vec],
        scratch_shapes=[pltpu.VMEM((4, TS, 128), F32)],
        compiler_params=_params(),
    )(dx1, w_out, mixb, ya, hf, hb, gate, g_rnn, g_attn)


def _rnn_bwd(xr, h, dhs, conv_w, conv_b, wa, ba, wx, bx, lam, reverse):
    S = xr.shape[0]
    nt = S // TS
    order = (lambda i: i) if reverse else (lambda i: nt - 1 - i)
    per = TS // SUB
    last = S // SUB - 1
    if reverse:
        h_halo = pl.BlockSpec((SUB, 512), lambda i: (jnp.minimum((order(i) + 1) * per, last), 0))
    else:
        h_halo = pl.BlockSpec((SUB, 512), lambda i: (jnp.maximum(order(i) * per - 1, 0), 0))
    tile = pl.BlockSpec((TS, 512), lambda i: (order(i), 0))

    def body(x_ref, xp_ref, xn_ref, h_ref, hh_ref, dh_ref, cw_ref, cb_ref, wa_ref, ba_ref, wx_ref, bx_ref, lam_ref,
             dxc_ref, dwa_ref, dwx_ref, dvec_ref, a_s, g_s, carry, edge):
        i = pl.program_id(0)
        t = order(i)

        @pl.when(i == 0)
        def _():
            carry[...] = jnp.zeros_like(carry)
            edge[...] = jnp.zeros_like(edge)
            dwa_ref[...] = jnp.zeros_like(dwa_ref)
            dwx_ref[...] = jnp.zeros_like(dwx_ref)
            dvec_ref[...] = jnp.zeros_like(dvec_ref)

        prev8 = jnp.where(t > 0, xp_ref[...], 0.0)
        next8 = jnp.where(t < nt - 1, xn_ref[...], 0.0)
        xc = _conv_fwd(x_ref[...], prev8, next8, cw_ref[...], cb_ref[...])
        xcb, r, gi, cl, a, mult, inv_mult = _lru_gates(xc, wa_ref, ba_ref[...], wx_ref, bx_ref[...], lam_ref[...])
        hv = h_ref[...]
        if reverse:
            a_s[...] = _shifted(a, edge[...], None, -1)
            edge[...] = a[TS - SUB:TS, :]
            hh = jnp.where(t < nt - 1, hh_ref[...], 0.0)
            h_prev = _shifted(hv, None, hh, 1)
        else:
            a_s[...] = _shifted(a, None, edge[...], 1)
            edge[...] = a[0:SUB, :]
            hh = jnp.where(t > 0, hh_ref[...], 0.0)
            h_prev = _shifted(hv, hh, None, -1)
        _scan_tile(a_s, dh_ref, g_s, carry, not reverse)
        g = g_s[...]
        da = g * h_prev
        gm = g * mult
        d_i = gm * xc
        dmult = g * gi * xc
        dla = da * a - dmult * (a * a) * inv_mult
        d_r = dla * cl
        dpre_r = d_r * r * (1.0 - r)
        dpre_i = d_i * gi * (1.0 - gi)
        dprb = dpre_r.astype(BF16)
        dpib = dpre_i.astype(BF16)
        dwa_ref[...] += _dot_tn(xcb, dprb)
        dwx_ref[...] += _dot_tn(xcb, dpib)
        dvec_ref[0:1, :] += jnp.sum(dpre_r, axis=0, keepdims=True)
        dvec_ref[1:2, :] += jnp.sum(dpre_i, axis=0, keepdims=True)
        dvec_ref[2:3, :] += jnp.sum(dla * r, axis=0, keepdims=True)
        dvec_ref[3:4, :] = dvec_ref[2:3, :] * (LRU_C * _sigmoid(-lam_ref[...]))
        dxc_ref[...] = gm * gi + _dot_nt(dprb, wa_ref[...]) + _dot_nt(dpib, wx_ref[...])

    sq = jax.ShapeDtypeStruct((D_RNN, D_RNN), F32)
    return pl.pallas_call(
        body, grid=(nt,), name="rnn_bwd_rev" if reverse else "rnn_bwd_fwd",
        in_specs=_halo_specs(S, order) + [tile, h_halo, tile] + [_whole_vmem()] * 7,
        out_specs=[tile, _whole_vmem(), _whole_vmem(), _whole_vmem()],
        out_shape=[jax.ShapeDtypeStruct((S, 512), F32), sq, sq, jax.ShapeDtypeStruct((SUB, 512), F32)],
        scratch_shapes=[pltpu.VMEM((TS, 512), F32), pltpu.VMEM((TS, 512), F32), pltpu.VMEM((SUB, 512), F32),
                        pltpu.VMEM((SUB, 512), F32)],
        compiler_params=_params(),
    )(xr, xr, xr, h, h, dhs, conv_w, conv_b, wa, ba, wx, bx, lam)


def _inproj_bwd(x, dx1, xr, dxc_f, dxc_b, dgate, dq3, dk3, dv3, g1, conv_w, w_in, rider=None):
    S = x.shape[0]
    tb = TS_INPROJ_BWD
    nt = S // tb
    ident = lambda i: i

    def body(x_ref, dx1_ref, xr_ref, xrp_ref, xrn_ref, cf_ref, cfp_ref, cfn_ref, cb_ref, cbp_ref, cbn_ref, dgate_ref,
             dq1, dq2, dq3_, dk1, dk2, dk3_, dv1, dv2, dv3_, g_ref, cw_ref, w_ref,
             dx_ref, dw_ref, dg_ref, dcw_ref, s_ref):
        i = pl.program_id(0)

        @pl.when(i == 0)
        def _():
            dw_ref[...] = jnp.zeros_like(dw_ref)
            dg_ref[...] = jnp.zeros_like(dg_ref)
            dcw_ref[...] = jnp.zeros_like(dcw_ref)

        first, last = i > 0, i < nt - 1
        dxc = cf_ref[...] + cb_ref[...]
        dxc_p = jnp.where(first, cfp_ref[...] + cbp_ref[...], 0.0)
        dxc_n = jnp.where(last, cfn_ref[...] + cbn_ref[...], 0.0)
        cw = cw_ref[...]
        dxr = (_shifted(dxc, dxc_p, dxc_n, 2) * cw[0:1, :] + _shifted(dxc, dxc_p, dxc_n, 1) * cw[1:2, :]
               + dxc * cw[2:3, :] + _shifted(dxc, dxc_p, dxc_n, -1) * cw[3:4, :])
        xrv = xr_ref[...]
        xr_p = jnp.where(first, xrp_ref[...], 0.0)
        xr_n = jnp.where(last, xrn_ref[...], 0.0)
        for k, off in enumerate((-2, -1, 0, 1)):
            dcw_ref[k:k + 1, :] += jnp.sum(dxc * _shifted(xrv, xr_p, xr_n, off), axis=0, keepdims=True)
        dcw_ref[4:5, :] += jnp.sum(dxc, axis=0, keepdims=True)

        def total(a, b, c_):
            return a[0] + _merge_classes(b, s_ref, F32_LAYOUT, c_)

        dproj = jnp.concatenate(
            [dxr, dgate_ref[...], total(dq1, dq2, dq3_), total(dk1, dk2, dk3_), total(dv1, dv2, dv3_)],
            axis=-1).astype(BF16)
        xv = x_ref[...]
        rstd, xh = _rms(xv)
        hb = (xh * g_ref[...]).astype(BF16)
        dh = jnp.zeros((tb, D_MODEL), F32)
        for j in range(N_SHARD):
            dpj = dproj[:, j * IN_BLK:(j + 1) * IN_BLK]
            dh = dh + _dot_nt(dpj, w_ref[j])
            dw_ref[j] += _dot_tn(hb, dpj)
        dxn, dg = _rms_bwd(dh, g_ref[...], xh, rstd)
        dg_ref[...] += dg
        dx_ref[...] = dx1_ref[...] + dxn

    halo = _halo_specs(S, ident, tb)
    return _call(
        body, "inproj_bwd", (nt,),
        [_rows(D_MODEL, tb), _rows(D_MODEL, tb)] + halo * 3 + [_rows(512, tb)]
        + [_class_spec(1, tb), _class_spec(F32_LAYOUT, tb), _class_spec(F32_LAYOUT, tb)] * 3 + [_whole_vmem()] * 3,
        [_rows(D_MODEL, tb), _whole_vmem(), _whole_vmem(), _whole_vmem()],
        [jax.ShapeDtypeStruct((S, D_MODEL), F32), jax.ShapeDtypeStruct((N_SHARD, D_MODEL, IN_BLK), F32),
         jax.ShapeDtypeStruct((1, D_MODEL), F32), jax.ShapeDtypeStruct((SUB, 512), F32)],
        [pltpu.VMEM((4, tb, 128), F32)],
        (x, dx1, xr, xr, xr, dxc_f, dxc_f, dxc_f, dxc_b, dxc_b, dxc_b, dgate, *dq3, *dk3, *dv3, g1, conv_w, w_in), rider)


def _dw_matmul(a, b, a_cols, b_cols, name):
    S = a.shape[0]
    tk = 1024
    a_shared = a.shape[1] == a_cols
    b_shared = b.shape[1] == b_cols

    def body(a_ref, b_ref, o_ref):
        @pl.when(pl.program_id(1) == 0)
        def _():
            o_ref[...] = jnp.zeros_like(o_ref)
        o_ref[0] += _dot_tn(a_ref[...], b_ref[...])

    return pl.pallas_call(
        body, grid=(N_SHARD, S // tk), name=name,
        in_specs=[pl.BlockSpec((tk, a_cols), (lambda j, k: (k, 0)) if a_shared else (lambda j, k: (k, j))),
                  pl.BlockSpec((tk, b_cols), (lambda j, k: (k, 0)) if b_shared else (lambda j, k: (k, j)))],
        out_specs=pl.BlockSpec((1, a_cols, b_cols), lambda j, k: (j, 0, 0)),
        out_shape=jax.ShapeDtypeStruct((N_SHARD, a_cols, b_cols), F32),
        compiler_params=_params(2),
    )(a, b)


def _t5_bucket_np(rel):
    nb = N_BUCKETS // 2
    max_exact = nb // 2
    ret = np.where(rel > 0, nb, 0)
    n = np.abs(rel)
    nf = np.maximum(n, 1).astype(np.float32)
    large = max_exact + (np.log(nf / np.float32(max_exact)) / np.float32(math.log(MAX_DISTANCE / max_exact))
                         * np.float32(nb - max_exact)).astype(np.int32)
    large = np.minimum(large, nb - 1)
    return ret + np.where(n < max_exact, n, large)


_VARIANT_OFFSETS = (-HALF_WIN,) * 3


def _band_index():
    kk = np.arange(K_WIN)[None, :]
    ql = np.arange(Q_BLK)[:, None]
    rel = np.stack([kk - ql + off for off in _VARIANT_OFFSETS])
    band = np.abs(rel) <= HALF_WIN
    inside = np.stack([np.broadcast_to(kk >= HALF_WIN, band[0].shape), np.ones_like(band[0]),
                       np.broadcast_to(kk < K_WIN - HALF_WIN, band[0].shape)])
    return rel, band & inside


def _bucket_tables(dil):
    rel, valid = _band_index()
    bucket = _t5_bucket_np(np.clip(rel, -HALF_WIN, HALF_WIN) * dil)
    return np.where(valid, bucket, -1).astype(np.int32)


def _bias_mats(rel_bias):
    tables = [_bucket_tables(d) for d in DILATIONS]
    used = [sorted(set(t[t >= 0].tolist())) for t in tables]

    def body(rb_ref, t1, t2, t3, o1, o2, o3):
        for t_ref, o_ref, buckets in ((t1, o1, used[0]), (t2, o2, used[1]), (t3, o3, used[2])):
            for var in range(3):
                bk = t_ref[var]
                for h in range(N_HEADS):
                    acc = jnp.full((Q_BLK, K_WIN), NEG_INF, F32)
                    for b in buckets:
                        acc = jnp.where(bk == b, rb_ref[b, h], acc)
                    o_ref[var, h] = acc

    shp = jax.ShapeDtypeStruct((3, N_HEADS, Q_BLK, K_WIN), F32)
    return pl.pallas_call(
        body, name="bias_tables", in_specs=[pl.BlockSpec(memory_space=pltpu.SMEM)] + [_whole_vmem()] * 3,
        out_shape=[shp] * 3, compiler_params=_params(0),
    )(rel_bias, *[jnp.asarray(t) for t in tables])


def _variant(qb, nq):
    return jnp.where(qb == 0, 0, jnp.where(qb == nq - 1, 2, 1))


def _win_start(qb):
    return pl.multiple_of(qb * Q_BLK, Q_BLK)


def _fill_padded(src_ref, pad_ref):
    L = src_ref.shape[0]
    edge = jnp.zeros((HALF_WIN, 128), pad_ref.dtype)
    pad_ref[0:HALF_WIN, :] = edge
    pad_ref[HALF_WIN:HALF_WIN + L, :] = src_ref[...]
    pad_ref[HALF_WIN + L:2 * HALF_WIN + L, :] = edge


INNER = {1: 1, 4: 1, 16: 4}


def _attn_layout(dil, L):
    inner = INNER[dil]
    n_outer = dil // inner
    nsub = min(ATTN_SUB // inner, L // Q_BLK)
    qt = nsub * Q_BLK
    grid = (4, n_outer, L // qt)
    qspec = pl.BlockSpec((inner, None, qt, 128), lambda hp, r, s: (0, r, s, hp))
    kspec = pl.BlockSpec((inner, None, L, 128), lambda hp, r, s: (0, r, 0, hp))
    bspec = pl.BlockSpec((3, 2, Q_BLK, K_WIN), lambda hp, r, s: (0, hp, 0, 0))
    kfspec = pl.BlockSpec((None, inner * L, 128), lambda hp, r, s: (r, 0, hp))
    qfspec = kfspec if inner > 1 else pl.BlockSpec((None, qt, 128), lambda hp, r, s: (r, s, hp))
    fshape = jax.ShapeDtypeStruct((n_outer, inner * L, D_ATTN), F32)
    view = lambda t: t.reshape(inner, n_outer, L, D_ATTN)

    def qrows(m, sub):
        if inner == 1:
            return (slice(sub * Q_BLK, (sub + 1) * Q_BLK), slice(None))
        first = (pl.program_id(2) * nsub + sub) * Q_BLK
        return (pl.ds(m + inner * first, Q_BLK, stride=inner), slice(None))

    def krows(m):
        if inner == 1:
            return (slice(None), slice(None))
        return (pl.ds(m, L, stride=inner), slice(None))

    return inner, nsub, grid, qspec, kspec, bspec, qfspec, kfspec, fshape, view, qrows, krows


def _head_masks():
    lane = lax.broadcasted_iota(jnp.int32, (Q_BLK, 128), 1)
    return lane < HEAD_DIM


def _attn_fwd(q, k, v, bias):
    dil, L, _ = q.shape
    nq = L // Q_BLK
    inner, nsub, grid, qspec, kspec, bspec, qfspec, kfspec, fshape, view, qrows, krows = _attn_layout(dil, L)

    def body(q_ref, k_ref, v_ref, b_ref, o_ref, l_ref, kp, vp):
        step = pl.program_id(2)

        @pl.when(step == 0)
        def _():
            for m in range(inner):
                _fill_padded(k_ref.at[m], kp.at[m])
                _fill_padded(v_ref.at[m], vp.at[m])

        h0 = _head_masks()
        for m, sub in [(m, sub) for m in range(inner) for sub in range(nsub)]:
            qb = step * nsub + sub
            st = _win_start(qb)
            var = _variant(qb, nq)
            kw = kp[m, pl.ds(st, K_WIN), :]
            vw = vp[m, pl.ds(st, K_WIN), :]
            qs = q_ref[m, sub * Q_BLK:(sub + 1) * Q_BLK, :] * ATTN_SCALE
            zq = jnp.zeros_like(qs)
            q2 = jnp.concatenate([jnp.where(h0, qs, zq), jnp.where(h0, zq, qs)], axis=0)
            s = _dot_nt(q2, kw) + b_ref[var].reshape(2 * Q_BLK, K_WIN)
            top = jnp.max(s, axis=-1, keepdims=True)
            p = jnp.exp(s - top)
            l = jnp.sum(p, axis=-1, keepdims=True)
            out = _dot(p.astype(BF16), vw) / l
            lse = top + jnp.log(l)
            o_ref[qrows(m, sub)] = jnp.where(h0, out[0:Q_BLK], out[Q_BLK:2 * Q_BLK])
            l_ref[qrows(m, sub)] = jnp.where(h0, lse[0:Q_BLK], lse[Q_BLK:2 * Q_BLK])

    return pl.pallas_call(
        body, grid=grid, name=f"attn_fwd_d{dil}",
        in_specs=[qspec, kspec, kspec, bspec], out_specs=[qfspec, qfspec], out_shape=[fshape, fshape],
        scratch_shapes=[pltpu.VMEM((inner, L + 2 * HALF_WIN, 128), BF16)] * 2,
        compiler_params=_params(3),
    )(view(q), view(k), view(v), bias)


def _attn_bwd(q, k, v, bias, do, o, lse, rider=None):
    dil, L, _ = q.shape
    nq = L // Q_BLK
    inner, nsub, grid, qspec, kspec, bspec, qfspec, kfspec, fshape, view, qrows, krows = _attn_layout(dil, L)
    nstep = grid[2]

    def body(q_ref, k_ref, v_ref, b_ref, do_ref, o_ref, l_ref, dq_ref, dk_ref, dv_ref, db_ref, db_s,
             kp, vp, dkp, dvp, carry):
        hp, step = pl.program_id(0), pl.program_id(2)
        first = (hp == 0) & (pl.program_id(1) == 0) & (step == 0)
        last = (hp == grid[0] - 1) & (pl.program_id(1) == grid[1] - 1) & (step == nstep - 1)

        @pl.when(first)
        def _():
            db_s[...] = jnp.zeros_like(db_s)

        @pl.when(step == 0)
        def _():
            for m in range(inner):
                _fill_padded(k_ref.at[m], kp.at[m])
                _fill_padded(v_ref.at[m], vp.at[m])
            carry[...] = jnp.zeros_like(carry)

        h0 = _head_masks()
        for m, sub in [(m, sub) for m in range(inner) for sub in range(nsub)]:
            if sub == 0:
                carry_k, carry_v = carry[m, 0], carry[m, 1]
            qb = step * nsub + sub
            st = _win_start(qb)
            var = _variant(qb, nq)
            kw = kp[m, pl.ds(st, K_WIN), :]
            vw = vp[m, pl.ds(st, K_WIN), :]
            qs = q_ref[m, sub * Q_BLK:(sub + 1) * Q_BLK, :] * ATTN_SCALE
            dof = do_ref[qrows(m, sub)]
            dob = dof.astype(BF16)
            prod = dof * o_ref[qrows(m, sub)]
            lsev = l_ref[qrows(m, sub)]
            zq, zd = jnp.zeros_like(qs), jnp.zeros_like(dob)
            q2 = jnp.concatenate([jnp.where(h0, qs, zq), jnp.where(h0, zq, qs)], axis=0)
            do2 = jnp.concatenate([jnp.where(h0, dob, zd), jnp.where(h0, zd, dob)], axis=0)
            lse2 = jnp.concatenate([lsev[:, 0:1], lsev[:, HEAD_DIM:HEAD_DIM + 1]], axis=0)
            dd2 = jnp.concatenate([jnp.sum(jnp.where(h0, prod, 0.0), axis=-1, keepdims=True),
                                   jnp.sum(jnp.where(h0, 0.0, prod), axis=-1, keepdims=True)], axis=0)
            s = _dot_nt(q2, kw) + b_ref[var].reshape(2 * Q_BLK, K_WIN)
            p = jnp.exp(s - lse2)
            ds = p * (_dot_nt(do2, vw) - dd2)
            db_s[var, pl.ds(hp * 2, 2)] += ds.reshape(2, Q_BLK, K_WIN)
            dsb = ds.astype(BF16)
            dv_acc = _dot_tn(p.astype(BF16), do2)
            dk_acc = _dot_tn(dsb, q2)
            dq2 = _dot(dsb, kw) * ATTN_SCALE
            dq_ref[qrows(m, sub)] = jnp.where(h0, dq2[0:Q_BLK], dq2[Q_BLK:2 * Q_BLK])
            dkp[m, pl.ds(st, Q_BLK), :] = carry_k + dk_acc[0:Q_BLK]
            dvp[m, pl.ds(st, Q_BLK), :] = carry_v + dv_acc[0:Q_BLK]
            carry_k, carry_v = dk_acc[Q_BLK:K_WIN], dv_acc[Q_BLK:K_WIN]
            if sub == nsub - 1:
                carry[m, 0] = carry_k
                carry[m, 1] = carry_v

        @pl.when(step == nstep - 1)
        def _():
            for m in range(inner):
                dkp[m, L:L + Q_BLK, :] = carry[m, 0]
                dvp[m, L:L + Q_BLK, :] = carry[m, 1]
                dk_ref[krows(m)] = dkp[m, HALF_WIN:HALF_WIN + L, :]
                dv_ref[krows(m)] = dvp[m, HALF_WIN:HALF_WIN + L, :]

        @pl.when(last)
        def _():
            db_ref[...] = db_s[...]

    dbshape = (3, N_HEADS, Q_BLK, K_WIN)
    return _call(
        body, f"attn_bwd_d{dil}", grid,
        [qspec, kspec, kspec, bspec, qfspec, qfspec, qfspec],
        [qfspec, kfspec, kfspec, _whole_vmem()],
        [fshape, fshape, fshape, jax.ShapeDtypeStruct(dbshape, F32)],
        [pltpu.VMEM(dbshape, F32)] + [pltpu.VMEM((inner, L + 2 * HALF_WIN, 128), BF16)] * 2
        + [pltpu.VMEM((inner, L + 2 * HALF_WIN, 128), F32)] * 2 + [pltpu.VMEM((inner, 2, Q_BLK, 128), F32)],
        (view(q), view(k), view(v), bias, do, o, lse), rider)


def _bucket_onehots(dil):
    m = np.zeros((3, K_WIN, N_BUCKETS), np.float32)
    for var, off in enumerate(_VARIANT_OFFSETS):
        for rel in range(-HALF_WIN, HALF_WIN + 1):
            col = (rel - off + Q_BLK - 1) % K_WIN
            m[var, col, int(_t5_bucket_np(np.asarray(rel * dil)))] = 1.0
    return jnp.asarray(m)


def _bias_grad(dbs):
    onehots = [_bucket_onehots(d) for d in DILATIONS]
    flip = jnp.asarray(np.eye(Q_BLK, dtype=np.float32)[::-1].copy())

    def body(d1, d2, d3, m1, m2, m3, flip_ref, out_ref):
        hp = lax.Precision.HIGHEST
        acc = jnp.zeros((N_HEADS, N_BUCKETS), F32)
        for d_ref, m_ref in ((d1, m1), (d2, m2), (d3, m3)):
            for var in range(3):
                rows = []
                for h in range(N_HEADS):
                    xrev = jnp.dot(flip_ref[...], d_ref[var, h], precision=hp, preferred_element_type=F32)
                    y = pltpu.roll(xrev, 0, 1, stride=1, stride_axis=0)
                    rows.append(jnp.sum(y, axis=0, keepdims=True))
                vec = jnp.concatenate(rows, axis=0)
                acc = acc + jnp.dot(vec, m_ref[var], precision=hp, preferred_element_type=F32)
        out_ref[...] = acc

    return pl.pallas_call(
        body, name="bias_grad", out_shape=jax.ShapeDtypeStruct((N_HEADS, N_BUCKETS), F32),
        compiler_params=_params(0),
    )(*dbs, *onehots, flip)


def _block_diag(w):
    eye = jnp.eye(N_RNN_BLOCKS, dtype=w.dtype)
    return jnp.einsum("ncd,nm->ncmd", w, eye).reshape(D_RNN, D_RNN).astype(BF16)


def _diag_blocks(dense):
    d = dense.reshape(N_RNN_BLOCKS, RNN_BLOCK, N_RNN_BLOCKS, RNN_BLOCK)
    return jnp.stack([d[n, :, n, :] for n in range(N_RNN_BLOCKS)])


EARLY = ("w_out", "w_up", "w_down")


def _local_step(x, target, p, shards=None):
    p = dict(p)
    biases = _bias_mats(p["rel_bias"])
    lru = {}
    for dname in ("fwd", "bwd"):
        lru[dname] = (_block_diag(p["lru_wa_" + dname]), p["lru_ba_" + dname], _block_diag(p["lru_wx_" + dname]),
                      p["lru_bx_" + dname], p["lru_lam_" + dname])

    def gather(name):
        return None if shards is None else _gather_rider([name], [shards[name]])

    (xr, gate, *qkv), got = _inproj_fwd(x, p["attn_norm_g"], p["w_in"], gather("w_out"))
    p.update(zip(["w_out"], got))
    qs, ks, vs = qkv[0:3], qkv[3:6], qkv[6:9]
    hf, got = _rnn_fwd(xr, p["conv_w"], p["conv_b"], *lru["fwd"], reverse=False, rider=gather("w_up"))
    p.update(zip(["w_up"], got))
    hb, got = _rnn_fwd(xr, p["conv_w"], p["conv_b"], *lru["bwd"], reverse=True, rider=gather("w_down"))
    p.update(zip(["w_down"], got))
    outs, lses = [], []
    for q, k, v, bias in zip(qs, ks, vs, biases):
        o, l = _attn_fwd(q, k, v, bias)
        outs.append(o)
        lses.append(l)
    x1, mixb, *yl = _mix_fwd(outs, lses, hf, hb, gate, x, p["norm_rnn_g"], p["norm_attn_g"], p["w_out"])
    yas, lsts = [yl[0], yl[1], yl[1]], [yl[2], yl[3], yl[3]]
    dx1, h2b, a2b, dub, dx2b, loss_vec, dg_fin, dg_mlp = _mlp_fwd_bwd(
        x1, target, p["mlp_norm_g"], p["final_norm_g"], p["w_up"], p["w_down"])
    dhs, dgate, dya1, dya4, dw_out, dg_rnn, dg_attn = _mix_bwd(dx1, p["w_out"], mixb, yas[0], hf, hb, gate,
                                                               p["norm_rnn_g"], p["norm_attn_g"])
    dyas = [dya1, dya4, dya4]
    dw_up = _dw_matmul(h2b, dub, D_MODEL, FF_BLK, "dw_up")
    dw_down = _dw_matmul(a2b, dx2b, FF_BLK, D_MODEL, "dw_down")
    early = [dw_out, dw_up, dw_down]
    dqs, dks, dvs, dbs = [], [], [], []
    for i, (q, k, v, bias, dya, ya, lse) in enumerate(zip(qs, ks, vs, biases, dyas, yas, lsts)):
        rider = None
        if shards is not None:
            make = (lambda: _pair_exchange_rider(EARLY, early), lambda: _chip_exchange_rider(early),
                    lambda: _pair_share_rider(EARLY, early))[i]
            rider = make()
        (dq, dk, dv, db), got = _attn_bwd(q, k, v, bias, dya, ya, lse, rider)
        if shards is not None and i == 0:
            core = lax.axis_index("c").reshape(1).astype(jnp.int32)
            early = [_pair_add(core, g, o, "grad_pair_add_" + n) for n, g, o in zip(EARLY, early, got)]
        elif shards is not None and i == 1:
            early = [_chip_sum(t, "grad_chip_sum_" + n) for n, t in zip(EARLY, got)]
        elif shards is not None:
            early = got
        dqs.append(dq)
        dks.append(dk)
        dvs.append(dv)
        dbs.append(db)
    d_rel_bias = _bias_grad(dbs).T
    dxc_f, dwa_f, dwx_f, dvec_f = _rnn_bwd(xr, hf, dhs, p["conv_w"], p["conv_b"], *lru["fwd"], reverse=False)
    dxc_b, dwa_b, dwx_b, dvec_b = _rnn_bwd(xr, hb, dhs, p["conv_w"], p["conv_b"], *lru["bwd"], reverse=True)
    small = {
        "lru_wa_fwd": _diag_blocks(dwa_f), "lru_ba_fwd": dvec_f[0:1], "lru_wx_fwd": _diag_blocks(dwx_f),
        "lru_bx_fwd": dvec_f[1:2], "lru_lam_fwd": dvec_f[3:4],
        "lru_wa_bwd": _diag_blocks(dwa_b), "lru_ba_bwd": dvec_b[0:1], "lru_wx_bwd": _diag_blocks(dwx_b),
        "lru_bx_bwd": dvec_b[1:2], "lru_lam_bwd": dvec_b[3:4],
        "rel_bias": d_rel_bias, "norm_rnn_g": dg_rnn, "norm_attn_g": dg_attn,
        "mlp_norm_g": dg_mlp, "final_norm_g": dg_fin,
    }
    loss_local = (0.5 / D_MODEL) * jnp.sum(loss_vec)
    rider = None
    if shards is not None:
        rider = _small_gather_rider(_pack([small[n].reshape(shp) for n, shp in SMALL if n in small]
                                          + [loss_local.reshape(1)]))
    (grad_x, dw_in, dg1, dconv), gathered = _inproj_bwd(x, dx1, xr, dxc_f, dxc_b, dgate, dqs, dks, dvs,
                                                        p["attn_norm_g"], p["conv_w"], p["w_in"], rider)
    last = {"attn_norm_g": dg1, "conv_w": dconv[0:4], "conv_b": dconv[4:5]}
    if shards is None:
        big = {"w_in": dw_in, "w_out": dw_out, "w_up": dw_up, "w_down": dw_down}
        return loss_local, grad_x, {**small, **last}, None, big, {}
    return loss_local, grad_x, last, gathered[0], {"w_in": dw_in}, dict(zip(EARLY, early))


BIG = ("w_in", "w_out", "w_up", "w_down")
BIG_SHARD = {"w_in": (D_MODEL, IN_BLK), "w_out": (OUT_BLK, D_MODEL), "w_up": (D_MODEL, FF_BLK), "w_down": (FF_BLK, D_MODEL)}
N_BIG = len(BIG)
N_CHIP_PEERS = 3
ANY = pl.BlockSpec(memory_space=pl.ANY)


def _place():
    x, y, c = lax.axis_index("x"), lax.axis_index("y"), lax.axis_index("c")
    chips = [(1 - x, y), (x, 1 - y), (1 - x, 1 - y)]
    return x, y, c, chips


def _remote(src, dst, send_sem, recv_sem, dev):
    return pltpu.make_async_remote_copy(src_ref=src, dst_ref=dst, send_sem=send_sem, recv_sem=recv_sem,
                                        device_id=dev, device_id_type=MESH)


def _staged_start(srcs, bufs, sems):
    legs = [pltpu.make_async_copy(s, b, sems.at[i]) for i, (s, b) in enumerate(zip(srcs, bufs))]
    for cp in legs:
        cp.start()
    return legs


def _staged_finish(legs, bufs, dsts, sems):
    out = []
    for i, (leg, b, d) in enumerate(zip(legs, bufs, dsts)):
        leg.wait()
        cp = pltpu.make_async_copy(b, d, sems.at[i])
        cp.start()
        out.append(cp)
    return out


class _Rider:
    def __init__(self, inputs, out_shape, scratch, first, late, last):
        self.inputs, self.out_shape, self.scratch = list(inputs), list(out_shape), list(scratch)
        self.first, self.late, self.last = first, late, last


def _call(body, name, grid, in_specs, out_specs, out_shape, scratch, operands, rider=None):
    n_grid = len(grid)
    if rider is None:
        res = pl.pallas_call(body, grid=grid, name=name, in_specs=in_specs, out_specs=out_specs, out_shape=out_shape,
                             scratch_shapes=scratch, compiler_params=_params(n_grid))(*operands)
        return list(res), []
    n_in, n_out, n_scr = len(in_specs), len(out_specs), len(scratch)
    ri, ro = len(rider.inputs), len(rider.out_shape)
    nsteps = int(np.prod(grid))
    late_step = max(nsteps - 3, 1)

    def wrapped(*refs):
        a, b = n_in, n_in + ri
        c, d = b + n_out, b + n_out + ro
        e = d + n_scr
        mine = refs[:a] + refs[b:c] + refs[d:e]
        theirs = (refs[a:b], refs[c:d], refs[e:])
        step = pl.program_id(0)
        for ax in range(1, n_grid):
            step = step * grid[ax] + pl.program_id(ax)
        pl.when(step == 0)(lambda: rider.first(*theirs))
        pl.when(step == late_step)(lambda: rider.late(*theirs))
        body(*mine)
        pl.when(step == nsteps - 1)(lambda: rider.last(*theirs))

    res = pl.pallas_call(
        wrapped, grid=grid, name=name, in_specs=list(in_specs) + [ANY] * ri, out_specs=list(out_specs) + [ANY] * ro,
        out_shape=list(out_shape) + rider.out_shape, scratch_shapes=list(scratch) + rider.scratch,
        compiler_params=_params(n_grid),
    )(*operands, *rider.inputs)
    return list(res[:n_out]), list(res[n_out:])


def _run_rider(rider, name):
    ri, ro = len(rider.inputs), len(rider.out_shape)

    def body(*refs):
        parts = (refs[:ri], refs[ri:ri + ro], refs[ri + ro:])
        rider.first(*parts)
        rider.late(*parts)
        rider.last(*parts)

    return list(pl.pallas_call(
        body, name=name, in_specs=[ANY] * ri, out_specs=[ANY] * ro, out_shape=rider.out_shape, scratch_shapes=rider.scratch,
        compiler_params=pltpu.CompilerParams(has_side_effects=True, vmem_limit_bytes=VMEM_LIMIT),
    )(*rider.inputs))


def _nothing(ins, outs, scr):
    return None


def _gather_rider(names, shards, conv_w=None):
    n = len(names)
    items = n + (conv_w is not None)
    halves = [BIG_SHARD[nm][0] // 2 for nm in names]

    def parts(ins, outs, scr):
        x, y, c, chips = _place()
        return x, y, c, chips, 2 * x + y, (x, y, 1 - c), scr[:8], scr[8:]

    def piece(outs, w, chip, core_half):
        return outs[w].at[chip, pl.ds(core_half * halves[w], halves[w])]

    def ici(ins, outs, sems, w, k, chip_xy, c, me):
        return _remote(ins[w].at[pl.ds(c * halves[w], halves[w])], piece(outs, w, me, c),
                       sems[0].at[w, k], sems[1].at[w, k], (*chip_xy, c))

    def first(ins, outs, scr):
        x, y, c, chips, me, sibling, sems, bufs = parts(ins, outs, scr)
        legs = _staged_start(ins, bufs, sems[6])
        for w in range(n):
            for k, chip_xy in enumerate(chips):
                ici(ins, outs, sems, w, k, chip_xy, c, me).start()
        if conv_w is not None:
            for k, (px, py) in enumerate(chips):
                _remote(ins[n], outs[n].at[me], sems[4].at[k], sems[5].at[k], (px, py, c)).start()
        _staged_finish(legs, bufs, [o.at[me] for o in outs], sems[7])

    def late(ins, outs, scr):
        x, y, c, chips, me, sibling, sems, bufs = parts(ins, outs, scr)
        for w in range(n):
            for k, (px, py) in enumerate(chips):
                landed = piece(outs, w, 2 * px + py, c)
                _remote(landed, landed, sems[0].at[w, k], sems[1].at[w, k], (px, py, c)).wait_recv()
                _remote(landed, landed, sems[2].at[w, k], sems[3].at[w, k], sibling).start()

    def last(ins, outs, scr):
        x, y, c, chips, me, sibling, sems, bufs = parts(ins, outs, scr)
        for w in range(n):
            for k, (px, py) in enumerate(chips):
                other = piece(outs, w, 2 * px + py, 1 - c)
                _remote(other, other, sems[2].at[w, k], sems[3].at[w, k], sibling).wait_recv()
        if conv_w is not None:
            for k, (px, py) in enumerate(chips):
                got = outs[n].at[2 * px + py]
                _remote(got, got, sems[4].at[k], sems[5].at[k], (px, py, c)).wait_recv()
                _remote(ins[n], outs[n].at[me], sems[4].at[k], sems[5].at[k], (px, py, c)).wait_send()
        for i in range(items):
            pltpu.make_async_copy(bufs[i], outs[i].at[me], sems[7].at[i]).wait()
        for w in range(n):
            for k, (px, py) in enumerate(chips):
                ici(ins, outs, sems, w, k, (px, py), c, me).wait_send()
                landed = piece(outs, w, 2 * px + py, c)
                _remote(landed, landed, sems[2].at[w, k], sems[3].at[w, k], sibling).wait_send()

    out_shape = [jax.ShapeDtypeStruct((N_SHARD,) + BIG_SHARD[nm], BF16) for nm in names]
    stage = [pltpu.VMEM(BIG_SHARD[nm], BF16) for nm in names]
    inputs = list(shards)
    if conv_w is not None:
        out_shape.append(jax.ShapeDtypeStruct((N_SHARD,) + conv_w.shape, F32))
        stage.append(pltpu.VMEM(conv_w.shape, F32))
        inputs.append(conv_w)
    scratch = ([pltpu.SemaphoreType.DMA((n, N_CHIP_PEERS))] * 4 + [pltpu.SemaphoreType.DMA((N_CHIP_PEERS,))] * 2
               + [pltpu.SemaphoreType.DMA((items,))] * 2 + stage)
    return _Rider(inputs, out_shape, scratch, first, late, last)


def _pair_exchange_rider(names, grads):
    def copies(ins, outs, scr):
        x, y, c, _ = _place()
        out = []
        for w, nm in enumerate(names):
            h = BIG_SHARD[nm][0] // 2
            out.append(_remote(ins[w].at[:, pl.ds((1 - c) * h, h), :], outs[w], scr[0].at[w], scr[1].at[w], (x, y, 1 - c)))
        return out

    def first(ins, outs, scr):
        for cp in copies(ins, outs, scr):
            cp.start()

    def last(ins, outs, scr):
        for cp in copies(ins, outs, scr):
            cp.wait()

    out_shape = [jax.ShapeDtypeStruct((N_SHARD, BIG_SHARD[nm][0] // 2, BIG_SHARD[nm][1]), F32) for nm in names]
    return _Rider(grads, out_shape, [pltpu.SemaphoreType.DMA((len(names),))] * 2, first, _nothing, last)


def _pair_add(core, grad, other, name):
    _, r, cols = grad.shape
    h = r // 2
    th = min(h, 256)
    per = h // th

    def body(c_ref, g_ref, o_ref, out_ref):
        out_ref[...] = (g_ref[...] + o_ref[...]).astype(BF16)

    return pl.pallas_call(
        body, name=name,
        grid_spec=pltpu.PrefetchScalarGridSpec(
            num_scalar_prefetch=1, grid=(N_SHARD, per),
            in_specs=[pl.BlockSpec((1, th, cols), lambda j, i, c_ref: (j, c_ref[0] * per + i, 0)),
                      pl.BlockSpec((1, th, cols), lambda j, i, c_ref: (j, i, 0))],
            out_specs=pl.BlockSpec((1, th, cols), lambda j, i, c_ref: (j, i, 0))),
        out_shape=jax.ShapeDtypeStruct((N_SHARD, h, cols), BF16),
        compiler_params=_params(2),
    )(core, grad, other)


def _chip_exchange_rider(parts):
    n = len(parts)

    def sends(ins, outs, scr):
        x, y, c, chips = _place()
        me = 2 * x + y
        return [_remote(ins[w].at[2 * px + py], outs[w].at[me], scr[0].at[w, k], scr[1].at[w, k], (px, py, c))
                for w in range(n) for k, (px, py) in enumerate(chips)]

    def first(ins, outs, scr):
        x, y, c, chips = _place()
        me = 2 * x + y
        legs = _staged_start([r.at[me] for r in ins], scr[4:], scr[2])
        for cp in sends(ins, outs, scr):
            cp.start()
        _staged_finish(legs, scr[4:], [o.at[me] for o in outs], scr[3])

    def last(ins, outs, scr):
        x, y, c, chips = _place()
        me = 2 * x + y
        for w in range(n):
            for k, (px, py) in enumerate(chips):
                got = outs[w].at[2 * px + py]
                _remote(got, got, scr[0].at[w, k], scr[1].at[w, k], (px, py, c)).wait_recv()
        for cp in sends(ins, outs, scr):
            cp.wait_send()
        for w in range(n):
            pltpu.make_async_copy(scr[4 + w], outs[w].at[me], scr[3].at[w]).wait()

    out_shape = [jax.ShapeDtypeStruct(p.shape, BF16) for p in parts]
    scratch = ([pltpu.SemaphoreType.DMA((n, N_CHIP_PEERS))] * 2 + [pltpu.SemaphoreType.DMA((n,))] * 2
               + [pltpu.VMEM(p.shape[1:], BF16) for p in parts])
    return _Rider(parts, out_shape, scratch, first, _nothing, last)


def _chip_sum(parts, name):
    _, h, cols = parts.shape
    th = min(h, 256)

    def body(p_ref, out_ref):
        acc = p_ref[0].astype(F32)
        for j in range(1, N_SHARD):
            acc = acc + p_ref[j].astype(F32)
        out_ref[...] = acc

    return pl.pallas_call(
        body, name=name, grid=(h // th,),
        in_specs=[pl.BlockSpec((N_SHARD, th, cols), lambda i: (0, i, 0))],
        out_specs=pl.BlockSpec((th, cols), lambda i: (i, 0)),
        out_shape=jax.ShapeDtypeStruct((h, cols), F32),
        compiler_params=_params(),
    )(parts)


def _pair_share_rider(names, halves):
    n = len(names)
    hs = [BIG_SHARD[nm][0] // 2 for nm in names]

    def mine(outs, c):
        return [outs[w].at[pl.ds(c * hs[w], hs[w])] for w in range(n)]

    def first(ins, outs, scr):
        x, y, c, _ = _place()
        legs = _staged_start(ins, scr[4:], scr[2])
        for w, dst in enumerate(mine(outs, c)):
            _remote(ins[w], dst, scr[0].at[w], scr[1].at[w], (x, y, 1 - c)).start()
        _staged_finish(legs, scr[4:], mine(outs, c), scr[3])

    def last(ins, outs, scr):
        x, y, c, _ = _place()
        for w, (theirs, dst) in enumerate(zip(mine(outs, 1 - c), mine(outs, c))):
            _remote(theirs, theirs, scr[0].at[w], scr[1].at[w], (x, y, 1 - c)).wait_recv()
            _remote(ins[w], dst, scr[0].at[w], scr[1].at[w], (x, y, 1 - c)).wait_send()
            pltpu.make_async_copy(scr[4 + w], dst, scr[3].at[w]).wait()

    out_shape = [jax.ShapeDtypeStruct(BIG_SHARD[nm], F32) for nm in names]
    scratch = [pltpu.SemaphoreType.DMA((n,))] * 4 + [pltpu.VMEM((h, BIG_SHARD[nm][1]), F32) for nm, h in zip(names, hs)]
    return _Rider(halves, out_shape, scratch, first, _nothing, last)


N_DEV = 8


def _all_peers(x, y, c):
    return [((1 - x) if fx else x, (1 - y) if fy else y, (1 - c) if fc else c)
            for fx in (0, 1) for fy in (0, 1) for fc in (0, 1) if fx or fy or fc]


def _small_gather_rider(vec):
    def sends(ins, outs, scr):
        x, y, c, _ = _place()
        me = 4 * x + 2 * y + c
        return [_remote(ins[0], outs[0].at[me], scr[0].at[k], scr[1].at[k], dev) for k, dev in enumerate(_all_peers(x, y, c))]

    def first(ins, outs, scr):
        x, y, c, _ = _place()
        legs = _staged_start(ins, scr[4:], scr[2])
        for cp in sends(ins, outs, scr):
            cp.start()
        _staged_finish(legs, scr[4:], [outs[0].at[4 * x + 2 * y + c]], scr[3])

    def last(ins, outs, scr):
        x, y, c, _ = _place()
        for k, (px, py, pc) in enumerate(_all_peers(x, y, c)):
            got = outs[0].at[4 * px + 2 * py + pc]
            _remote(got, got, scr[0].at[k], scr[1].at[k], (px, py, pc)).wait_recv()
        for cp in sends(ins, outs, scr):
            cp.wait_send()
        pltpu.make_async_copy(scr[4], outs[0].at[4 * x + 2 * y + c], scr[3].at[0]).wait()

    scratch = ([pltpu.SemaphoreType.DMA((N_DEV - 1,))] * 2 + [pltpu.SemaphoreType.DMA((1,))] * 2
               + [pltpu.VMEM(vec.shape, F32)])
    return _Rider([vec], [jax.ShapeDtypeStruct((N_DEV,) + vec.shape, F32)], scratch, first, _nothing, last)


def _sum_devices(gathered):
    def body(g_ref, out_ref):
        acc = g_ref[0]
        for j in range(1, N_DEV):
            acc = acc + g_ref[j]
        out_ref[...] = acc

    return pl.pallas_call(body, name="sum_devices", out_shape=jax.ShapeDtypeStruct(gathered.shape[1:], F32),
                          compiler_params=_params(0))(gathered)


def _allreduce_small(vec):
    rows = vec.shape[0]

    def body(v_ref, sum_ref, gat_ref, send, recv, loc_sem):
        x, y, c, _ = _place()
        me = 4 * x + 2 * y + c
        lc = pltpu.make_async_copy(v_ref, gat_ref.at[me], loc_sem)
        lc.start()
        peers = []
        for fx in (0, 1):
            for fy in (0, 1):
                for fc in (0, 1):
                    if fx or fy or fc:
                        peers.append(((1 - x) if fx else x, (1 - y) if fy else y, (1 - c) if fc else c))
        sends = []
        for k, dev in enumerate(peers):
            cp = _remote(v_ref, gat_ref.at[me], send.at[k], recv.at[k], dev)
            cp.start()
            sends.append(cp)
        for k, (px, py, pc) in enumerate(peers):
            got = gat_ref.at[4 * px + 2 * py + pc]
            _remote(got, got, send.at[k], recv.at[k], (px, py, pc)).wait_recv()
        for cp in sends:
            cp.wait_send()
        lc.wait()
        acc = gat_ref[0]
        for j in range(1, N_DEV):
            acc = acc + gat_ref[j]
        sum_ref[...] = acc

    total, _ = pl.pallas_call(
        body, name="allreduce_small",
        in_specs=[_whole_vmem()], out_specs=[_whole_vmem(), _whole_vmem()],
        out_shape=[jax.ShapeDtypeStruct((rows, 128), F32), jax.ShapeDtypeStruct((N_DEV, rows, 128), F32)],
        scratch_shapes=[pltpu.SemaphoreType.DMA((N_DEV - 1,))] * 2 + [pltpu.SemaphoreType.DMA(())],
        compiler_params=pltpu.CompilerParams(has_side_effects=True, vmem_limit_bytes=VMEM_LIMIT),
    )(vec)
    return total


def _adam_math(w_ref, g_ref, m_ref, v_ref, d_ref, m2_ref, v2_ref):
    c1 = 1.0 - ADAM_B1 ** ADAM_STEP
    c2 = 1.0 - ADAM_B2 ** ADAM_STEP
    gv = g_ref[...]
    m2 = ADAM_B1 * m_ref[...] + (1.0 - ADAM_B1) * gv
    v2 = ADAM_B2 * v_ref[...] + (1.0 - ADAM_B2) * (gv * gv)
    m2_ref[...] = m2
    v2_ref[...] = v2
    d_ref[...] = -ADAM_LR * ((m2 / c1) / (jnp.sqrt(v2 / c2) + ADAM_EPS) + ADAM_WD * w_ref[...])


def _adamw_many(ws, gs, ms, vs):
    n = len(ws)

    def body(*refs):
        for i in range(n):
            _adam_math(*[refs[k * n + i] for k in range(7)])

    shapes = [jax.ShapeDtypeStruct(w.shape, F32) for w in ws]
    res = pl.pallas_call(body, name="adamw_small", out_shape=shapes * 3, compiler_params=_params(0))(*ws, *gs, *ms, *vs)
    return res[:n], res[n:2 * n], res[2 * n:]


def _adamw(w, g, m, v, name):
    rows, cols = w.shape
    tr = 256 if rows % 256 == 0 else rows

    def body(w_ref, g_ref, m_ref, v_ref, d_ref, m2_ref, v2_ref):
        _adam_math(w_ref, g_ref, m_ref, v_ref, d_ref, m2_ref, v2_ref)

    spec = pl.BlockSpec((tr, cols), lambda i: (i, 0))
    shp = jax.ShapeDtypeStruct((rows, cols), F32)
    return pl.pallas_call(
        body, name=name, grid=(rows // tr,), in_specs=[spec] * 4, out_specs=[spec] * 3, out_shape=[shp] * 3,
        compiler_params=_params(),
    )(w, g, m, v)


SMALL = (
    ("attn_norm_g", (1, 1024)), ("conv_w", (1, 4, 512)), ("conv_b", (1, 512)),
    ("lru_wa_fwd", (1, 8, 64, 64)), ("lru_ba_fwd", (1, 512)), ("lru_wx_fwd", (1, 8, 64, 64)), ("lru_bx_fwd", (1, 512)),
    ("lru_lam_fwd", (1, 512)),
    ("lru_wa_bwd", (1, 8, 64, 64)), ("lru_ba_bwd", (1, 512)), ("lru_wx_bwd", (1, 8, 64, 64)), ("lru_bx_bwd", (1, 512)),
    ("lru_lam_bwd", (1, 512)),
    ("rel_bias", (32, 8)), ("norm_rnn_g", (1, 512)), ("norm_attn_g", (1, 512)), ("mlp_norm_g", (1, 1024)),
    ("final_norm_g", (1024,)),
)
PACK_ROW = 8 * 128


def _pack(parts):
    flat = jnp.concatenate([p.reshape(-1) for p in parts])
    pad = (-flat.shape[0]) % PACK_ROW
    return jnp.pad(flat, (0, pad)).reshape(-1, 128)


def _unpack(packed, shapes):
    flat = packed.reshape(-1)
    out, off = [], 0
    for shp in shapes:
        n = int(np.prod(shp))
        out.append(flat[off:off + n].reshape(shp))
        off += n
    return out


WEIGHT_ORDER = ("attn_norm_g", "w_in", "conv_w", "conv_b", "lru_wa_fwd", "lru_ba_fwd", "lru_wx_fwd", "lru_bx_fwd",
                "lru_lam_fwd", "lru_wa_bwd", "lru_ba_bwd", "lru_wx_bwd", "lru_bx_bwd", "lru_lam_bwd", "rel_bias",
                "norm_rnn_g", "norm_attn_g", "w_out", "mlp_norm_g", "w_up", "w_down", "final_norm_g")


def kernel(x, attn_norm_g, w_in, conv_w, conv_b, lru_wa_fwd, lru_ba_fwd, lru_wx_fwd, lru_bx_fwd, lru_lam_fwd, lru_wa_bwd, lru_ba_bwd, lru_wx_bwd, lru_bx_bwd, lru_lam_bwd, rel_bias, norm_rnn_g, norm_attn_g, w_out, mlp_norm_g, w_up, w_down, final_norm_g, loss_target, m_attn_norm_g, m_w_in, m_conv_w, m_conv_b, m_lru_wa_fwd, m_lru_ba_fwd, m_lru_wx_fwd, m_lru_bx_fwd, m_lru_lam_fwd, m_lru_wa_bwd, m_lru_ba_bwd, m_lru_wx_bwd, m_lru_bx_bwd, m_lru_lam_bwd, m_rel_bias, m_norm_rnn_g, m_norm_attn_g, m_w_out, m_mlp_norm_g, m_w_up, m_w_down, m_final_norm_g, v_attn_norm_g, v_w_in, v_conv_w, v_conv_b, v_lru_wa_fwd, v_lru_ba_fwd, v_lru_wx_fwd, v_lru_bx_fwd, v_lru_lam_fwd, v_lru_wa_bwd, v_lru_ba_bwd, v_lru_wx_bwd, v_lru_bx_bwd, v_lru_lam_bwd, v_rel_bias, v_norm_rnn_g, v_norm_attn_g, v_w_out, v_mlp_norm_g, v_w_up, v_w_down, v_final_norm_g):
    given = dict(locals())
    w = {n: given[n] for n in WEIGHT_ORDER}
    m = {n: given["m_" + n] for n in WEIGHT_ORDER}
    v = {n: given["v_" + n] for n in WEIGHT_ORDER}

    chip = lax.axis_index("x") * 2 + lax.axis_index("y")
    core = lax.axis_index("c")

    shards = {n: w[n][0].astype(BF16) for n in BIG}
    w_in_all, conv_all = _run_rider(_gather_rider(["w_in"], [shards["w_in"]], w["conv_w"][0]), "allgather_w_in")
    p = {n: (t[0] if t.ndim >= 3 else t) for n, t in w.items() if n not in BIG}
    p["final_norm_g"] = w["final_norm_g"].reshape(1, D_MODEL)
    p["conv_w"] = jnp.transpose(conv_all, (1, 0, 2)).reshape(4, D_RNN)
    p["w_in"] = w_in_all

    _, grad_x, small, gathered, big, reduced = _local_step(x[0], loss_target[0], p, {n: shards[n] for n in EARLY})

    late = tuple(big)
    grads = [big[n] for n in late]
    others = _run_rider(_pair_exchange_rider(late, grads), "grad_pair_exchange")
    core_arr = core.reshape(1).astype(jnp.int32)
    parts = [_pair_add(core_arr, g, o, "grad_pair_add_" + n) for n, g, o in zip(late, grads, others)]
    landed = _run_rider(_chip_exchange_rider(parts), "grad_chip_exchange")
    halves = [_chip_sum(t, "grad_chip_sum_" + n) for n, t in zip(late, landed)]
    reduced.update(zip(late, _run_rider(_pair_share_rider(late, halves), "grad_pair_share")))

    early_small = [(n, shp) for n, shp in SMALL if n not in small]
    late_small = [(n, shp) for n, shp in SMALL if n in small]
    *early_g, loss = _unpack(_sum_devices(gathered), [shp for _, shp in early_small] + [(1,)])
    late_g = _unpack(_allreduce_small(_pack([small[n].reshape(shp) for n, shp in late_small])),
                     [shp for _, shp in late_small])
    g = dict(zip([n for n, _ in early_small + late_small], early_g + late_g))
    g["conv_w"] = lax.dynamic_slice_in_dim(g["conv_w"], chip * (D_RNN // N_SHARD), D_RNN // N_SHARD, axis=2)
    for n in BIG:
        g[n] = reduced[n][None]

    delta, new_m, new_v = {}, {}, {}
    for n in BIG:
        d2, m2, v2 = _adamw(w[n][0], reduced[n], m[n][0], v[n][0], "adamw_" + n)
        delta[n], new_m[n], new_v[n] = d2[None], m2[None], v2[None]
    names = [n for n, _ in SMALL]
    for dst, src in zip((delta, new_m, new_v), _adamw_many(*[[t[n] for n in names] for t in (w, g, m, v)])):
        dst.update(dict(zip(names, src)))

    return (loss.reshape(()), grad_x[None], *[g[n] for n in WEIGHT_ORDER], *[delta[n] for n in WEIGHT_ORDER],
            *[new_m[n] for n in WEIGHT_ORDER], *[new_v[n] for n in WEIGHT_ORDER])
```

```python
import functools
import math

import numpy as np
import jax
import jax.numpy as jnp
from jax import lax
from jax.experimental import pallas as pl
from jax.experimental.pallas import tpu as pltpu

F32 = jnp.float32
BF16 = jnp.bfloat16

D_MODEL = 1024
D_RNN = 512
D_ATTN = 512
N_HEADS = 8
HEAD_DIM = 64
N_RNN_BLOCKS = 8
RNN_BLOCK = 64
D_IN = 2 * D_RNN + 3 * D_ATTN
D_FF = 4 * D_MODEL
N_SHARD = 4
IN_BLK = D_IN // N_SHARD
OUT_BLK = D_MODEL // N_SHARD
FF_BLK = D_FF // N_SHARD
EPS = 1e-6
NEG_INF = -1e30
LRU_C = 8.0
DILATIONS = (1, 4, 16)
F32_LAYOUT = 4
HALF_WIN = 64
Q_BLK = 128
K_WIN = 256
N_BUCKETS = 32
MAX_DISTANCE = 1024
ATTN_SCALE = HEAD_DIM ** -0.5

ADAM_LR = 0.001
ADAM_B1 = 0.9
ADAM_B2 = 0.999
ADAM_EPS = 1e-08
ADAM_WD = 0.01
ADAM_STEP = 10

TS = 512
TS_MLP = 256
TS_INPROJ_BWD = 256
ATTN_SUB = 16
SCAN_UNROLL = 4
SUB = 8
VMEM_LIMIT = 56 * 1024 * 1024
GELU_C0 = math.sqrt(2.0 / math.pi)
GELU_C1 = 0.044715

MESH = pl.DeviceIdType.MESH


def _params(n_grid=1):
    return pltpu.CompilerParams(vmem_limit_bytes=VMEM_LIMIT, dimension_semantics=("arbitrary",) * n_grid)


def _whole_vmem():
    return pl.BlockSpec(memory_space=pltpu.VMEM)


def _rows(width, tile=TS):
    return pl.BlockSpec((tile, width), lambda i: (i, 0))


def _sigmoid(z):
    return 0.5 * jnp.tanh(0.5 * z) + 0.5


def _log1p(u):
    w = 1.0 + u
    return jnp.where(w == 1.0, u, jnp.log(w) * (u / (w - 1.0)))


def _softplus(z):
    return jnp.maximum(z, 0.0) + _log1p(jnp.exp(-jnp.abs(z)))


def _gelu_parts(g):
    inner = GELU_C0 * (g + GELU_C1 * g * g * g)
    t = jnp.tanh(inner)
    val = 0.5 * g * (1.0 + t)
    dinner = GELU_C0 * (1.0 + 3.0 * GELU_C1 * g * g)
    grad = 0.5 * (1.0 + t) + 0.5 * g * (1.0 - t * t) * dinner
    return val, grad


def _rms(x):
    rstd = lax.rsqrt(jnp.mean(x * x, axis=-1, keepdims=True) + EPS)
    return rstd, x * rstd


def _rms_bwd(dy, g, xhat, rstd):
    dxh = dy * g
    dx = rstd * (dxh - xhat * jnp.mean(dxh * xhat, axis=-1, keepdims=True))
    dg = jnp.sum(dy * xhat, axis=0, keepdims=True)
    return dx, dg


def _dot(a, b):
    return jnp.dot(a, b, preferred_element_type=F32)


def _dot_nt(a, b):
    return lax.dot_general(a, b, (((1,), (1,)), ((), ())), preferred_element_type=F32)


def _dot_tn(a, b):
    return lax.dot_general(a, b, (((0,), (0,)), ((), ())), preferred_element_type=F32)


def _shifted(tile, prev8, next8, k):
    n = tile.shape[0]
    row = lax.broadcasted_iota(jnp.int32, tile.shape, 0)
    if k == 0:
        return tile
    if k < 0:
        r = pltpu.roll(tile, -k, 0)
        for j in range(-k):
            r = jnp.where(row == j, prev8[SUB + j + k:SUB + j + k + 1, :], r)
        return r
    r = pltpu.roll(tile, n - k, 0)
    for j in range(k):
        r = jnp.where(row == n - k + j, next8[j:j + 1, :], r)
    return r


def _to_lane_blocks(val, s_ref):
    for j in range(val.shape[1] // 128):
        s_ref[j] = val[:, j * 128:(j + 1) * 128]


def _from_lane_blocks(s_ref):
    return jnp.concatenate([s_ref[j] for j in range(s_ref.shape[0])], axis=-1)


def _class_rows(s_ref, r, dil):
    n = s_ref.shape[1] // dil
    return jnp.concatenate([s_ref[j, pl.ds(r, n, stride=dil), :] for j in range(s_ref.shape[0])], axis=-1)


def _split_classes(val, s_ref, out_ref, dil):
    _to_lane_blocks(val, s_ref)
    for r in range(dil):
        out_ref[r] = _class_rows(s_ref, r, dil).astype(out_ref.dtype)


def _merge_classes(in_ref, s_ref, dil, also_ref=None):
    n = s_ref.shape[1] // dil
    for r in range(dil):
        v = in_ref[r] if also_ref is None else in_ref[r] + also_ref[r]
        for j in range(s_ref.shape[0]):
            s_ref[j, pl.ds(r, n, stride=dil), :] = v[:, j * 128:(j + 1) * 128]
    return _from_lane_blocks(s_ref)


def _class_spec(dil, tile=TS):
    return pl.BlockSpec((dil, tile // dil, 512), lambda i: (0, i, 0))


def _class_shape(S, dil, dtype):
    return jax.ShapeDtypeStruct((dil, S // dil, 512), dtype)


def _scan_tile(a_ref, b_ref, h_ref, carry_ref, reverse):
    n = a_ref.shape[0]
    width = a_ref.shape[1]
    groups = n // SUB
    row = lax.broadcasted_iota(jnp.int32, (SUB, width), 0)

    def group_scan(g):
        r0 = pl.multiple_of(g * SUB, SUB)
        a = a_ref[pl.ds(r0, SUB), :]
        b = b_ref[pl.ds(r0, SUB), :]
        for s in (1, 2, 4):
            if reverse:
                a_sh = pltpu.roll(a, SUB - s, 0)
                b_sh = pltpu.roll(b, SUB - s, 0)
                m = row < SUB - s
            else:
                a_sh = pltpu.roll(a, s, 0)
                b_sh = pltpu.roll(b, s, 0)
                m = row >= s
            b = jnp.where(m, a * b_sh + b, b)
            a = jnp.where(m, a * a_sh, a)
        return r0, a, b

    def step(i, carry):
        first = i * SCAN_UNROLL
        order = [(groups - 1 - (first + u)) if reverse else (first + u) for u in range(SCAN_UNROLL)]
        scans = [group_scan(g) for g in order]
        for r0, a, b in scans:
            h = b + a * carry
            h_ref[pl.ds(r0, SUB), :] = h
            edge = h[0:1, :] if reverse else h[SUB - 1:SUB, :]
            carry = jnp.broadcast_to(edge, (SUB, width))
        return carry

    carry_ref[...] = lax.fori_loop(0, groups // SCAN_UNROLL, step, carry_ref[...])


def _conv_fwd(xr, prev8, next8, cw, cb):
    y = cb + _shifted(xr, prev8, next8, -2) * cw[0:1, :]
    y = y + _shifted(xr, prev8, next8, -1) * cw[1:2, :]
    y = y + xr * cw[2:3, :]
    y = y + _shifted(xr, prev8, next8, 1) * cw[3:4, :]
    return y


def _lru_gates(xc, wa_ref, ba, wx_ref, bx, lam):
    xcb = xc.astype(BF16)
    r = _sigmoid(_dot(xcb, wa_ref[...]) + ba)
    i = _sigmoid(_dot(xcb, wx_ref[...]) + bx)
    cl = -LRU_C * _softplus(-lam)
    la = cl * r
    a = jnp.exp(la)
    m2 = -jnp.tanh(la) * (a * a + 1.0)
    inv = lax.rsqrt(m2)
    mult = jnp.where(m2 > 0.0, m2 * inv, 0.0)
    return xcb, r, i, cl, a, mult, inv


def _inproj_fwd(x, g1, w_in, rider=None):
    S = x.shape[0]

    def body(x_ref, g_ref, w_ref, xr_ref, gate_ref, *rest):
        qkv_refs, s_ref, s4_ref = rest[:9], rest[9], rest[10]
        _, xh = _rms(x_ref[...])
        h = (xh * g_ref[...]).astype(BF16)
        proj = jnp.concatenate([_dot(h, w_ref[j]) for j in range(N_SHARD)], axis=-1)
        xr_ref[...] = proj[:, 0:512]
        gate_ref[...] = proj[:, 512:1024]
        for t in range(3):
            val = proj[:, 1024 + 512 * t:1536 + 512 * t]
            d1_ref, d4_ref, d16_ref = qkv_refs[3 * t:3 * t + 3]
            d1_ref[0] = val.astype(BF16)
            _to_lane_blocks(val, s_ref)
            for r4 in range(4):
                c4 = _class_rows(s_ref, r4, 4)
                d4_ref[r4] = c4.astype(BF16)
                _to_lane_blocks(c4, s4_ref.at[r4])
            for r4 in range(4):
                for m in range(4):
                    d16_ref[r4 + 4 * m] = _class_rows(s4_ref.at[r4], m, 4).astype(BF16)

    f = jax.ShapeDtypeStruct((S, 512), F32)
    return _call(
        body, "inproj_fwd", (S // TS,),
        [_rows(D_MODEL), _whole_vmem(), _whole_vmem()],
        [_rows(512)] * 2 + [_class_spec(d) for d in DILATIONS] * 3,
        [f, f] + [_class_shape(S, d, BF16) for d in DILATIONS] * 3,
        [pltpu.VMEM((4, TS, 128), F32), pltpu.VMEM((4, 4, TS // 4, 128), F32)], (x, g1, w_in), rider)


def _halo_specs(S, order, tile=TS):
    per = tile // SUB
    last = S // SUB - 1
    return [
        pl.BlockSpec((tile, 512), lambda i: (order(i), 0)),
        pl.BlockSpec((SUB, 512), lambda i: (jnp.maximum(order(i) * per - 1, 0), 0)),
        pl.BlockSpec((SUB, 512), lambda i: (jnp.minimum((order(i) + 1) * per, last), 0)),
    ]


def _rnn_fwd(xr, conv_w, conv_b, wa, ba, wx, bx, lam, reverse, rider=None):
    S = xr.shape[0]
    nt = S // TS
    order = (lambda i: nt - 1 - i) if reverse else (lambda i: i)

    def body(x_ref, xp_ref, xn_ref, cw_ref, cb_ref, wa_ref, ba_ref, wx_ref, bx_ref, lam_ref, h_ref, a_s, b_s, carry):
        i = pl.program_id(0)
        t = order(i)

        @pl.when(i == 0)
        def _():
            carry[...] = jnp.zeros_like(carry)

        prev8 = jnp.where(t > 0, xp_ref[...], 0.0)
        next8 = jnp.where(t < nt - 1, xn_ref[...], 0.0)
        xc = _conv_fwd(x_ref[...], prev8, next8, cw_ref[...], cb_ref[...])
        _, _, gi, _, a, mult, _ = _lru_gates(xc, wa_ref, ba_ref[...], wx_ref, bx_ref[...], lam_ref[...])
        a_s[...] = a
        b_s[...] = mult * (gi * xc)
        _scan_tile(a_s, b_s, h_ref, carry, reverse)

    (h,), carried = _call(
        body, "rnn_fwd_rev" if reverse else "rnn_fwd_fwd", (nt,),
        _halo_specs(S, order) + [_whole_vmem()] * 7,
        [pl.BlockSpec((TS, 512), lambda i: (order(i), 0))],
        [jax.ShapeDtypeStruct((S, 512), F32)],
        [pltpu.VMEM((TS, 512), F32), pltpu.VMEM((TS, 512), F32), pltpu.VMEM((SUB, 512), F32)],
        (xr, xr, xr, conv_w, conv_b, wa, ba, wx, bx, lam), rider)
    return h, carried


def _mix_fwd(o3, l3, hf, hb, gate, x, g_rnn, g_attn, w_out):
    S = x.shape[0]

    def body(o1, o2, o3_, l1, l2, l3_, hf_ref, hb_ref, gate_ref, x_ref, gr_ref, ga_ref, w_ref,
             x1_ref, mix_ref, ya1, ya2, ls1, ls2, s_ref):
        la, lb, lc = l1[0], _merge_classes(l2, s_ref, F32_LAYOUT), _merge_classes(l3_, s_ref, F32_LAYOUT)
        m = jnp.maximum(jnp.maximum(la, lb), lc)
        ea, eb, ec = jnp.exp(la - m), jnp.exp(lb - m), jnp.exp(lc - m)
        den = ea + eb + ec
        lse = m + jnp.log(den)
        ya = (ea * o1[0] + eb * _merge_classes(o2, s_ref, F32_LAYOUT) + ec * _merge_classes(o3_, s_ref, F32_LAYOUT)) / den
        ya1[0] = ya
        ls1[0] = lse
        _split_classes(ya, s_ref, ya2, F32_LAYOUT)
        _split_classes(lse, s_ref, ls2, F32_LAYOUT)
        gg, _ = _gelu_parts(gate_ref[...])
        yr = (hf_ref[...] + hb_ref[...]) * gg
        _, xh_r = _rms(yr)
        _, xh_a = _rms(ya)
        mix = jnp.concatenate([xh_r * gr_ref[...], xh_a * ga_ref[...]], axis=-1).astype(BF16)
        mix_ref[...] = mix
        acc = x_ref[...]
        for j in range(N_SHARD):
            acc = acc + _dot(mix[:, j * OUT_BLK:(j + 1) * OUT_BLK], w_ref[j])
        x1_ref[...] = acc

    one, four = _class_spec(1), _class_spec(F32_LAYOUT)
    return pl.pallas_call(
        body, grid=(S // TS,), name="mix_fwd",
        in_specs=[one, four, four] * 2 + [_rows(512)] * 3 + [_rows(D_MODEL)] + [_whole_vmem()] * 3,
        out_specs=[_rows(D_MODEL), _rows(D_MODEL)] + [one, four] * 2,
        out_shape=[jax.ShapeDtypeStruct((S, D_MODEL), F32), jax.ShapeDtypeStruct((S, D_MODEL), BF16)]
        + [_class_shape(S, 1, F32), _class_shape(S, F32_LAYOUT, F32)] * 2,
        scratch_shapes=[pltpu.VMEM((4, TS, 128), F32)],
        compiler_params=_params(),
    )(*o3, *l3, hf, hb, gate, x, g_rnn, g_attn, w_out)


def _mlp_fwd_bwd(x1, target, g_mlp, g_fin, w_up, w_down):
    S = x1.shape[0]
    tm = TS_MLP

    def body(x1_ref, t_ref, gm_ref, gf_ref, wu_ref, wd_ref,
             dx1_ref, h2_ref, a2_ref, du_ref, dx2_ref, loss_ref, dgf_ref, dgm_ref, relu_s):
        @pl.when(pl.program_id(0) == 0)
        def _():
            loss_ref[...] = jnp.zeros_like(loss_ref)
            dgf_ref[...] = jnp.zeros_like(dgf_ref)
            dgm_ref[...] = jnp.zeros_like(dgm_ref)

        x1v = x1_ref[...]
        rstd1, xh1 = _rms(x1v)
        h2 = (xh1 * gm_ref[...]).astype(BF16)
        h2_ref[...] = h2
        x2 = x1v
        for j in range(N_SHARD):
            r = jnp.maximum(_dot(h2, wu_ref[j]), 0.0)
            relu_s[j] = r
            a2 = (r * r).astype(BF16)
            a2_ref[:, j * FF_BLK:(j + 1) * FF_BLK] = a2
            x2 = x2 + _dot(a2, wd_ref[j])
        rstd2, xh2 = _rms(x2)
        err = xh2 * gf_ref[...] - t_ref[...]
        loss_ref[...] += jnp.sum(err * err, axis=0, keepdims=True)
        dy = err * (1.0 / D_MODEL)
        dx2, dgf = _rms_bwd(dy, gf_ref[...], xh2, rstd2)
        dgf_ref[...] += dgf
        dx2b = dx2.astype(BF16)
        dx2_ref[...] = dx2b
        dh2 = jnp.zeros((tm, D_MODEL), F32)
        for j in range(N_SHARD):
            du = (_dot_nt(dx2b, wd_ref[j]) * (2.0 * relu_s[j])).astype(BF16)
            du_ref[:, j * FF_BLK:(j + 1) * FF_BLK] = du
            dh2 = dh2 + _dot_nt(du, wu_ref[j])
        dx1n, dgm = _rms_bwd(dh2, gm_ref[...], xh1, rstd1)
        dgm_ref[...] += dgm
        dx1_ref[...] = dx2 + dx1n

    vec = jax.ShapeDtypeStruct((1, D_MODEL), F32)
    return pl.pallas_call(
        body, grid=(S // tm,), name="mlp_fwd_bwd",
        in_specs=[_rows(D_MODEL, tm), _rows(D_MODEL, tm)] + [_whole_vmem()] * 4,
        out_specs=[_rows(D_MODEL, tm), _rows(D_MODEL, tm), _rows(D_FF, tm), _rows(D_FF, tm), _rows(D_MODEL, tm)]
        + [_whole_vmem()] * 3,
        out_shape=[jax.ShapeDtypeStruct((S, D_MODEL), F32), jax.ShapeDtypeStruct((S, D_MODEL), BF16),
                   jax.ShapeDtypeStruct((S, D_FF), BF16), jax.ShapeDtypeStruct((S, D_FF), BF16),
                   jax.ShapeDtypeStruct((S, D_MODEL), BF16), vec, vec, vec],
        scratch_shapes=[pltpu.VMEM((N_SHARD, tm, FF_BLK), F32)],
        compiler_params=_params(),
    )(x1, target, g_mlp, g_fin, w_up, w_down)


def _mix_bwd(dx1, w_out, mixb, ya, hf, hb, gate, g_rnn, g_attn):
    S = dx1.shape[0]

    def body(dx1_ref, w_ref, mix_ref, ya_ref, hf_ref, hb_ref, gate_ref, gr_ref, ga_ref,
             dhs_ref, dgate_ref, dya1, dya2, dw_ref, dgr_ref, dga_ref, s_ref):
        @pl.when(pl.program_id(0) == 0)
        def _():
            dw_ref[...] = jnp.zeros_like(dw_ref)
            dgr_ref[...] = jnp.zeros_like(dgr_ref)
            dga_ref[...] = jnp.zeros_like(dga_ref)

        dx1b = dx1_ref[...].astype(BF16)
        mix = mix_ref[...]
        for j in range(N_SHARD):
            dw_ref[j] += _dot_tn(mix[:, j * OUT_BLK:(j + 1) * OUT_BLK], dx1b)
        dmix = jnp.concatenate([_dot_nt(dx1b, w_ref[j]) for j in range(N_SHARD)], axis=-1)
        gg, dgg = _gelu_parts(gate_ref[...])
        hs = hf_ref[...] + hb_ref[...]
        rstd_r, xh_r = _rms(hs * gg)
        dyr, dgr = _rms_bwd(dmix[:, 0:D_RNN], gr_ref[...], xh_r, rstd_r)
        dgr_ref[...] += dgr
        rstd_a, xh_a = _rms(ya_ref[0])
        dya, dga = _rms_bwd(dmix[:, D_RNN:], ga_ref[...], xh_a, rstd_a)
        dga_ref[...] += dga
        dya1[0] = dya
        _split_classes(dya, s_ref, dya2, F32_LAYOUT)
        dhs_ref[...] = dyr * gg
        dgate_ref[...] = dyr * hs * dgg

    f512 = jax.ShapeDtypeStruct((S, 512), F32)
    vec = jax.ShapeDtypeStruct((1, 512), F32)
    return pl.pallas_call(
        body, grid=(S // TS,), name="mix_bwd",
        in_specs=[_rows(D_MODEL), _whole_vmem(), _rows(D_MODEL), _class_spec(1)] + [_rows(512)] * 3 + [_whole_vmem()] * 2,
        out_specs=[_rows(512)] * 2 + [_class_spec(1), _class_spec(F32_LAYOUT)] + [_whole_vmem()] * 3,
        out_shape=[f512, f512, _class_shape(S, 1, F32), _class_shape(S, F32_LAYOUT, F32),
                   jax.ShapeDtypeStruct((N_SHARD, OUT_BLK, D_MODEL), F32), vec, vec],
        scratch_shapes=[pltpu.VMEM((4, TS, 128), F32)],
        compiler_params=_params(),
    )(dx1, w_out, mixb, ya, hf, hb, gate, g_rnn, g_attn)


def _rnn_bwd(xr, h, dhs, conv_w, conv_b, wa, ba, wx, bx, lam, reverse, rider=None):
    S = xr.shape[0]
    nt = S // TS
    order = (lambda i: i) if reverse else (lambda i: nt - 1 - i)
    per = TS // SUB
    last = S // SUB - 1
    if reverse:
        h_halo = pl.BlockSpec((SUB, 512), lambda i: (jnp.minimum((order(i) + 1) * per, last), 0))
    else:
        h_halo = pl.BlockSpec((SUB, 512), lambda i: (jnp.maximum(order(i) * per - 1, 0), 0))
    tile = pl.BlockSpec((TS, 512), lambda i: (order(i), 0))

    def body(x_ref, xp_ref, xn_ref, h_ref, hh_ref, dh_ref, cw_ref, cb_ref, wa_ref, ba_ref, wx_ref, bx_ref, lam_ref,
             dxc_ref, dwa_ref, dwx_ref, dvec_ref, a_s, g_s, carry, edge):
        i = pl.program_id(0)
        t = order(i)

        @pl.when(i == 0)
        def _():
            carry[...] = jnp.zeros_like(carry)
            edge[...] = jnp.zeros_like(edge)
            dwa_ref[...] = jnp.zeros_like(dwa_ref)
            dwx_ref[...] = jnp.zeros_like(dwx_ref)
            dvec_ref[...] = jnp.zeros_like(dvec_ref)

        prev8 = jnp.where(t > 0, xp_ref[...], 0.0)
        next8 = jnp.where(t < nt - 1, xn_ref[...], 0.0)
        xc = _conv_fwd(x_ref[...], prev8, next8, cw_ref[...], cb_ref[...])
        xcb, r, gi, cl, a, mult, inv_mult = _lru_gates(xc, wa_ref, ba_ref[...], wx_ref, bx_ref[...], lam_ref[...])
        hv = h_ref[...]
        if reverse:
            a_s[...] = _shifted(a, edge[...], None, -1)
            edge[...] = a[TS - SUB:TS, :]
            hh = jnp.where(t < nt - 1, hh_ref[...], 0.0)
            h_prev = _shifted(hv, None, hh, 1)
        else:
            a_s[...] = _shifted(a, None, edge[...], 1)
            edge[...] = a[0:SUB, :]
            hh = jnp.where(t > 0, hh_ref[...], 0.0)
            h_prev = _shifted(hv, hh, None, -1)
        _scan_tile(a_s, dh_ref, g_s, carry, not reverse)
        g = g_s[...]
        da = g * h_prev
        gm = g * mult
        d_i = gm * xc
        dmult = g * gi * xc
        dla = da * a - dmult * (a * a) * inv_mult
        d_r = dla * cl
        dpre_r = d_r * r * (1.0 - r)
        dpre_i = d_i * gi * (1.0 - gi)
        dprb = dpre_r.astype(BF16)
        dpib = dpre_i.astype(BF16)
        dwa_ref[...] += _dot_tn(xcb, dprb)
        dwx_ref[...] += _dot_tn(xcb, dpib)
        dvec_ref[0:1, :] += jnp.sum(dpre_r, axis=0, keepdims=True)
        dvec_ref[1:2, :] += jnp.sum(dpre_i, axis=0, keepdims=True)
        dvec_ref[2:3, :] += jnp.sum(dla * r, axis=0, keepdims=True)
        dvec_ref[3:4, :] = dvec_ref[2:3, :] * (LRU_C * _sigmoid(-lam_ref[...]))
        dxc_ref[...] = gm * gi + _dot_nt(dprb, wa_ref[...]) + _dot_nt(dpib, wx_ref[...])

    sq = jax.ShapeDtypeStruct((D_RNN, D_RNN), F32)
    return _call(
        body, "rnn_bwd_rev" if reverse else "rnn_bwd_fwd", (nt,),
        _halo_specs(S, order) + [tile, h_halo, tile] + [_whole_vmem()] * 7,
        [tile, _whole_vmem(), _whole_vmem(), _whole_vmem()],
        [jax.ShapeDtypeStruct((S, 512), F32), sq, sq, jax.ShapeDtypeStruct((SUB, 512), F32)],
        [pltpu.VMEM((TS, 512), F32), pltpu.VMEM((TS, 512), F32), pltpu.VMEM((SUB, 512), F32),
         pltpu.VMEM((SUB, 512), F32)],
        (xr, xr, xr, h, h, dhs, conv_w, conv_b, wa, ba, wx, bx, lam), rider)


def _inproj_bwd(x, dx1, xr, dxc_f, dxc_b, dgate, dq3, dk3, dv3, g1, conv_w, w_in, rider=None):
    S = x.shape[0]
    tb = TS_INPROJ_BWD
    nt = S // tb
    ident = lambda i: i

    def body(x_ref, dx1_ref, xr_ref, xrp_ref, xrn_ref, cf_ref, cfp_ref, cfn_ref, cb_ref, cbp_ref, cbn_ref, dgate_ref,
             dq1, dq2, dq3_, dk1, dk2, dk3_, dv1, dv2, dv3_, g_ref, cw_ref, w_ref,
             dx_ref, dw_ref, dg_ref, dcw_ref, s_ref):
        i = pl.program_id(0)

        @pl.when(i == 0)
        def _():
            dw_ref[...] = jnp.zeros_like(dw_ref)
            dg_ref[...] = jnp.zeros_like(dg_ref)
            dcw_ref[...] = jnp.zeros_like(dcw_ref)

        first, last = i > 0, i < nt - 1
        dxc = cf_ref[...] + cb_ref[...]
        dxc_p = jnp.where(first, cfp_ref[...] + cbp_ref[...], 0.0)
        dxc_n = jnp.where(last, cfn_ref[...] + cbn_ref[...], 0.0)
        cw = cw_ref[...]
        dxr = (_shifted(dxc, dxc_p, dxc_n, 2) * cw[0:1, :] + _shifted(dxc, dxc_p, dxc_n, 1) * cw[1:2, :]
               + dxc * cw[2:3, :] + _shifted(dxc, dxc_p, dxc_n, -1) * cw[3:4, :])
        xrv = xr_ref[...]
        xr_p = jnp.where(first, xrp_ref[...], 0.0)
        xr_n = jnp.where(last, xrn_ref[...], 0.0)
        for k, off in enumerate((-2, -1, 0, 1)):
            dcw_ref[k:k + 1, :] += jnp.sum(dxc * _shifted(xrv, xr_p, xr_n, off), axis=0, keepdims=True)
        dcw_ref[4:5, :] += jnp.sum(dxc, axis=0, keepdims=True)

        def total(a, b, c_):
            return a[0] + _merge_classes(b, s_ref, F32_LAYOUT, c_)

        dproj = jnp.concatenate(
            [dxr, dgate_ref[...], total(dq1, dq2, dq3_), total(dk1, dk2, dk3_), total(dv1, dv2, dv3_)],
            axis=-1).astype(BF16)
        xv = x_ref[...]
        rstd, xh = _rms(xv)
        hb = (xh * g_ref[...]).astype(BF16)
        dh = jnp.zeros((tb, D_MODEL), F32)
        for j in range(N_SHARD):
            dpj = dproj[:, j * IN_BLK:(j + 1) * IN_BLK]
            dh = dh + _dot_nt(dpj, w_ref[j])
            dw_ref[j] += _dot_tn(hb, dpj)
        dxn, dg = _rms_bwd(dh, g_ref[...], xh, rstd)
        dg_ref[...] += dg
        dx_ref[...] = dx1_ref[...] + dxn

    halo = _halo_specs(S, ident, tb)
    return _call(
        body, "inproj_bwd", (nt,),
        [_rows(D_MODEL, tb), _rows(D_MODEL, tb)] + halo * 3 + [_rows(512, tb)]
        + [_class_spec(1, tb), _class_spec(F32_LAYOUT, tb), _class_spec(F32_LAYOUT, tb)] * 3 + [_whole_vmem()] * 3,
        [_rows(D_MODEL, tb), _whole_vmem(), _whole_vmem(), _whole_vmem()],
        [jax.ShapeDtypeStruct((S, D_MODEL), F32), jax.ShapeDtypeStruct((N_SHARD, D_MODEL, IN_BLK), F32),
         jax.ShapeDtypeStruct((1, D_MODEL), F32), jax.ShapeDtypeStruct((SUB, 512), F32)],
        [pltpu.VMEM((4, tb, 128), F32)],
        (x, dx1, xr, xr, xr, dxc_f, dxc_f, dxc_f, dxc_b, dxc_b, dxc_b, dgate, *dq3, *dk3, *dv3, g1, conv_w, w_in), rider)


def _dw_matmul(a, b, a_cols, b_cols, name):
    S = a.shape[0]
    tk = 1024
    a_shared = a.shape[1] == a_cols
    b_shared = b.shape[1] == b_cols

    def body(a_ref, b_ref, o_ref):
        @pl.when(pl.program_id(1) == 0)
        def _():
            o_ref[...] = jnp.zeros_like(o_ref)
        o_ref[0] += _dot_tn(a_ref[...], b_ref[...])

    return pl.pallas_call(
        body, grid=(N_SHARD, S // tk), name=name,
        in_specs=[pl.BlockSpec((tk, a_cols), (lambda j, k: (k, 0)) if a_shared else (lambda j, k: (k, j))),
                  pl.BlockSpec((tk, b_cols), (lambda j, k: (k, 0)) if b_shared else (lambda j, k: (k, j)))],
        out_specs=pl.BlockSpec((1, a_cols, b_cols), lambda j, k: (j, 0, 0)),
        out_shape=jax.ShapeDtypeStruct((N_SHARD, a_cols, b_cols), F32),
        compiler_params=_params(2),
    )(a, b)


def _t5_bucket_np(rel):
    nb = N_BUCKETS // 2
    max_exact = nb // 2
    ret = np.where(rel > 0, nb, 0)
    n = np.abs(rel)
    nf = np.maximum(n, 1).astype(np.float32)
    large = max_exact + (np.log(nf / np.float32(max_exact)) / np.float32(math.log(MAX_DISTANCE / max_exact))
                         * np.float32(nb - max_exact)).astype(np.int32)
    large = np.minimum(large, nb - 1)
    return ret + np.where(n < max_exact, n, large)


_VARIANT_OFFSETS = (-HALF_WIN,) * 3


def _band_index():
    kk = np.arange(K_WIN)[None, :]
    ql = np.arange(Q_BLK)[:, None]
    rel = np.stack([kk - ql + off for off in _VARIANT_OFFSETS])
    band = np.abs(rel) <= HALF_WIN
    inside = np.stack([np.broadcast_to(kk >= HALF_WIN, band[0].shape), np.ones_like(band[0]),
                       np.broadcast_to(kk < K_WIN - HALF_WIN, band[0].shape)])
    return rel, band & inside


def _bucket_tables(dil):
    rel, valid = _band_index()
    bucket = _t5_bucket_np(np.clip(rel, -HALF_WIN, HALF_WIN) * dil)
    return np.where(valid, bucket, -1).astype(np.int32)


def _bias_mats(rel_bias, rider=None):
    tables = [_bucket_tables(d) for d in DILATIONS]
    used = [sorted(set(t[t >= 0].tolist())) for t in tables]

    def one_pattern(rb_ref, t_ref, o_ref, buckets):
        bk = t_ref[1]
        for h in range(N_HEADS):
            acc = jnp.full((Q_BLK, K_WIN), NEG_INF, F32)
            for b in buckets:
                acc = jnp.where(bk == b, rb_ref[b, h], acc)
            o_ref[1, h] = acc
            for var in (0, 2):
                o_ref[var, h] = jnp.where(t_ref[var] >= 0, acc, NEG_INF)

    def body(rb_ref, t1, t2, t3, o1, o2, o3):
        for i, (t_ref, o_ref) in enumerate(((t1, o1), (t2, o2), (t3, o3))):
            pl.when(pl.program_id(0) == i)(functools.partial(one_pattern, rb_ref, t_ref, o_ref, used[i]))

    shp = jax.ShapeDtypeStruct((3, N_HEADS, Q_BLK, K_WIN), F32)
    return _call(
        body, "bias_tables", (len(DILATIONS),), [pl.BlockSpec(memory_space=pltpu.SMEM)] + [_whole_vmem()] * 3,
        [_whole_vmem()] * 3, [shp] * 3, [], (rel_bias, *[jnp.asarray(t) for t in tables]), rider)


def _variant(qb, nq):
    return jnp.where(qb == 0, 0, jnp.where(qb == nq - 1, 2, 1))


def _win_start(qb):
    return pl.multiple_of(qb * Q_BLK, Q_BLK)


def _fill_padded(src_ref, pad_ref):
    L = src_ref.shape[0]
    edge = jnp.zeros((HALF_WIN, 128), pad_ref.dtype)
    pad_ref[0:HALF_WIN, :] = edge
    pad_ref[HALF_WIN:HALF_WIN + L, :] = src_ref[...]
    pad_ref[HALF_WIN + L:2 * HALF_WIN + L, :] = edge


INNER = {1: 1, 4: 1, 16: 4}


def _attn_layout(dil, L):
    inner = INNER[dil]
    n_outer = dil // inner
    nsub = min(ATTN_SUB // inner, L // Q_BLK)
    qt = nsub * Q_BLK
    grid = (4, n_outer, L // qt)
    qspec = pl.BlockSpec((inner, None, qt, 128), lambda hp, r, s: (0, r, s, hp))
    kspec = pl.BlockSpec((inner, None, L, 128), lambda hp, r, s: (0, r, 0, hp))
    bspec = pl.BlockSpec((3, 2, Q_BLK, K_WIN), lambda hp, r, s: (0, hp, 0, 0))
    kfspec = pl.BlockSpec((None, inner * L, 128), lambda hp, r, s: (r, 0, hp))
    qfspec = kfspec if inner > 1 else pl.BlockSpec((None, qt, 128), lambda hp, r, s: (r, s, hp))
    fshape = jax.ShapeDtypeStruct((n_outer, inner * L, D_ATTN), F32)
    view = lambda t: t.reshape(inner, n_outer, L, D_ATTN)

    def qrows(m, sub):
        if inner == 1:
            return (slice(sub * Q_BLK, (sub + 1) * Q_BLK), slice(None))
        first = (pl.program_id(2) * nsub + sub) * Q_BLK
        return (pl.ds(m + inner * first, Q_BLK, stride=inner), slice(None))

    def krows(m):
        if inner == 1:
            return (slice(None), slice(None))
        return (pl.ds(m, L, stride=inner), slice(None))

    return inner, nsub, grid, qspec, kspec, bspec, qfspec, kfspec, fshape, view, qrows, krows


def _head_masks():
    lane = lax.broadcasted_iota(jnp.int32, (Q_BLK, 128), 1)
    return lane < HEAD_DIM


def _attn_fwd(q, k, v, bias):
    dil, L, _ = q.shape
    nq = L // Q_BLK
    inner, nsub, grid, qspec, kspec, bspec, qfspec, kfspec, fshape, view, qrows, krows = _attn_layout(dil, L)

    def body(q_ref, k_ref, v_ref, b_ref, o_ref, l_ref, kp, vp):
        step = pl.program_id(2)

        @pl.when(step == 0)
        def _():
            for m in range(inner):
                _fill_padded(k_ref.at[m], kp.at[m])
                _fill_padded(v_ref.at[m], vp.at[m])

        h0 = _head_masks()
        for m, sub in [(m, sub) for m in range(inner) for sub in range(nsub)]:
            qb = step * nsub + sub
            st = _win_start(qb)
            var = _variant(qb, nq)
            kw = kp[m, pl.ds(st, K_WIN), :]
            vw = vp[m, pl.ds(st, K_WIN), :]
            qs = q_ref[m, sub * Q_BLK:(sub + 1) * Q_BLK, :] * ATTN_SCALE
            zq = jnp.zeros_like(qs)
            q2 = jnp.concatenate([jnp.where(h0, qs, zq), jnp.where(h0, zq, qs)], axis=0)
            s = _dot_nt(q2, kw) + b_ref[var].reshape(2 * Q_BLK, K_WIN)
            top = jnp.max(s, axis=-1, keepdims=True)
            p = jnp.exp(s - top)
            l = jnp.sum(p, axis=-1, keepdims=True)
            out = _dot(p.astype(BF16), vw) / l
            lse = top + jnp.log(l)
            o_ref[qrows(m, sub)] = jnp.where(h0, out[0:Q_BLK], out[Q_BLK:2 * Q_BLK])
            l_ref[qrows(m, sub)] = jnp.where(h0, lse[0:Q_BLK], lse[Q_BLK:2 * Q_BLK])

    return pl.pallas_call(
        body, grid=grid, name=f"attn_fwd_d{dil}",
        in_specs=[qspec, kspec, kspec, bspec], out_specs=[qfspec, qfspec], out_shape=[fshape, fshape],
        scratch_shapes=[pltpu.VMEM((inner, L + 2 * HALF_WIN, 128), BF16)] * 2,
        compiler_params=_params(3),
    )(view(q), view(k), view(v), bias)


def _attn_bwd(q, k, v, bias, do, o, lse, rider=None):
    dil, L, _ = q.shape
    nq = L // Q_BLK
    inner, nsub, grid, qspec, kspec, bspec, qfspec, kfspec, fshape, view, qrows, krows = _attn_layout(dil, L)
    nstep = grid[2]

    def body(q_ref, k_ref, v_ref, b_ref, do_ref, o_ref, l_ref, dq_ref, dk_ref, dv_ref, db_ref, db_s,
             kp, vp, dkp, dvp, carry):
        hp, step = pl.program_id(0), pl.program_id(2)
        first = (hp == 0) & (pl.program_id(1) == 0) & (step == 0)
        last = (hp == grid[0] - 1) & (pl.program_id(1) == grid[1] - 1) & (step == nstep - 1)

        @pl.when(first)
        def _():
            db_s[...] = jnp.zeros_like(db_s)

        @pl.when(step == 0)
        def _():
            for m in range(inner):
                _fill_padded(k_ref.at[m], kp.at[m])
                _fill_padded(v_ref.at[m], vp.at[m])
            carry[...] = jnp.zeros_like(carry)

        h0 = _head_masks()
        for m, sub in [(m, sub) for m in range(inner) for sub in range(nsub)]:
            if sub == 0:
                carry_k, carry_v = carry[m, 0], carry[m, 1]
            qb = step * nsub + sub
            st = _win_start(qb)
            var = _variant(qb, nq)
            kw = kp[m, pl.ds(st, K_WIN), :]
            vw = vp[m, pl.ds(st, K_WIN), :]
            qs = q_ref[m, sub * Q_BLK:(sub + 1) * Q_BLK, :] * ATTN_SCALE
            dof = do_ref[qrows(m, sub)]
            dob = dof.astype(BF16)
            prod = dof * o_ref[qrows(m, sub)]
            lsev = l_ref[qrows(m, sub)]
            zq, zd = jnp.zeros_like(qs), jnp.zeros_like(dob)
            q2 = jnp.concatenate([jnp.where(h0, qs, zq), jnp.where(h0, zq, qs)], axis=0)
            do2 = jnp.concatenate([jnp.where(h0, dob, zd), jnp.where(h0, zd, dob)], axis=0)
            lse2 = jnp.concatenate([lsev[:, 0:1], lsev[:, HEAD_DIM:HEAD_DIM + 1]], axis=0)
            dd2 = jnp.concatenate([jnp.sum(jnp.where(h0, prod, 0.0), axis=-1, keepdims=True),
                                   jnp.sum(jnp.where(h0, 0.0, prod), axis=-1, keepdims=True)], axis=0)
            s = _dot_nt(q2, kw) + b_ref[var].reshape(2 * Q_BLK, K_WIN)
            p = jnp.exp(s - lse2)
            ds = p * (_dot_nt(do2, vw) - dd2)
            db_s[var, pl.ds(hp * 2, 2)] += ds.reshape(2, Q_BLK, K_WIN)
            dsb = ds.astype(BF16)
            dv_acc = _dot_tn(p.astype(BF16), do2)
            dk_acc = _dot_tn(dsb, q2)
            dq2 = _dot(dsb, kw) * ATTN_SCALE
            dq_ref[qrows(m, sub)] = jnp.where(h0, dq2[0:Q_BLK], dq2[Q_BLK:2 * Q_BLK])
            dkp[m, pl.ds(st, Q_BLK), :] = carry_k + dk_acc[0:Q_BLK]
            dvp[m, pl.ds(st, Q_BLK), :] = carry_v + dv_acc[0:Q_BLK]
            carry_k, carry_v = dk_acc[Q_BLK:K_WIN], dv_acc[Q_BLK:K_WIN]
            if sub == nsub - 1:
                carry[m, 0] = carry_k
                carry[m, 1] = carry_v

        @pl.when(step == nstep - 1)
        def _():
            for m in range(inner):
                dkp[m, L:L + Q_BLK, :] = carry[m, 0]
                dvp[m, L:L + Q_BLK, :] = carry[m, 1]
                dk_ref[krows(m)] = dkp[m, HALF_WIN:HALF_WIN + L, :]
                dv_ref[krows(m)] = dvp[m, HALF_WIN:HALF_WIN + L, :]

        @pl.when(last)
        def _():
            db_ref[...] = db_s[...]

    dbshape = (3, N_HEADS, Q_BLK, K_WIN)
    return _call(
        body, f"attn_bwd_d{dil}", grid,
        [qspec, kspec, kspec, bspec, qfspec, qfspec, qfspec],
        [qfspec, kfspec, kfspec, _whole_vmem()],
        [fshape, fshape, fshape, jax.ShapeDtypeStruct(dbshape, F32)],
        [pltpu.VMEM(dbshape, F32)] + [pltpu.VMEM((inner, L + 2 * HALF_WIN, 128), BF16)] * 2
        + [pltpu.VMEM((inner, L + 2 * HALF_WIN, 128), F32)] * 2 + [pltpu.VMEM((inner, 2, Q_BLK, 128), F32)],
        (view(q), view(k), view(v), bias, do, o, lse), rider)


def _bucket_onehots(dil):
    m = np.zeros((3, K_WIN, N_BUCKETS), np.float32)
    for var, off in enumerate(_VARIANT_OFFSETS):
        for rel in range(-HALF_WIN, HALF_WIN + 1):
            col = (rel - off + Q_BLK - 1) % K_WIN
            m[var, col, int(_t5_bucket_np(np.asarray(rel * dil)))] = 1.0
    return jnp.asarray(m)


def _bias_grad(dbs):
    onehots = [_bucket_onehots(d) for d in DILATIONS]
    flip = jnp.asarray(np.eye(Q_BLK, dtype=np.float32)[::-1].copy())

    def body(d1, d2, d3, m1, m2, m3, flip_ref, out_ref):
        hp = lax.Precision.HIGHEST
        acc = jnp.zeros((N_HEADS, N_BUCKETS), F32)
        for d_ref, m_ref in ((d1, m1), (d2, m2), (d3, m3)):
            for var in range(3):
                rows = []
                for h in range(N_HEADS):
                    xrev = jnp.dot(flip_ref[...], d_ref[var, h], precision=hp, preferred_element_type=F32)
                    y = pltpu.roll(xrev, 0, 1, stride=1, stride_axis=0)
                    rows.append(jnp.sum(y, axis=0, keepdims=True))
                vec = jnp.concatenate(rows, axis=0)
                acc = acc + jnp.dot(vec, m_ref[var], precision=hp, preferred_element_type=F32)
        out_ref[...] = acc

    return pl.pallas_call(
        body, name="bias_grad", out_shape=jax.ShapeDtypeStruct((N_HEADS, N_BUCKETS), F32),
        compiler_params=_params(0),
    )(*dbs, *onehots, flip)


def _block_diag(w):
    eye = jnp.eye(N_RNN_BLOCKS, dtype=w.dtype)
    return jnp.einsum("ncd,nm->ncmd", w, eye).reshape(D_RNN, D_RNN).astype(BF16)


def _diag_blocks(dense):
    d = dense.reshape(N_RNN_BLOCKS, RNN_BLOCK, N_RNN_BLOCKS, RNN_BLOCK)
    return jnp.stack([d[n, :, n, :] for n in range(N_RNN_BLOCKS)])


EARLY = ("w_out", "w_up", "w_down")


def _local_step(x, target, p, shards=None):
    p = dict(p)
    first = None if shards is None else _gather_rider(["w_in"], [shards["w_in"]], shards["conv_w"])
    biases, got = _bias_mats(p["rel_bias"], first)
    if shards is not None:
        p["w_in"] = got[0]
        p["conv_w"] = jnp.transpose(got[1], (1, 0, 2)).reshape(4, D_RNN)
    lru = {}
    for dname in ("fwd", "bwd"):
        lru[dname] = (_block_diag(p["lru_wa_" + dname]), p["lru_ba_" + dname], _block_diag(p["lru_wx_" + dname]),
                      p["lru_bx_" + dname], p["lru_lam_" + dname])

    def gather(name):
        return None if shards is None else _gather_rider([name], [shards[name]])

    (xr, gate, *qkv), got = _inproj_fwd(x, p["attn_norm_g"], p["w_in"], gather("w_out"))
    p.update(zip(["w_out"], got))
    qs, ks, vs = qkv[0:3], qkv[3:6], qkv[6:9]
    hf, got = _rnn_fwd(xr, p["conv_w"], p["conv_b"], *lru["fwd"], reverse=False, rider=gather("w_up"))
    p.update(zip(["w_up"], got))
    hb, got = _rnn_fwd(xr, p["conv_w"], p["conv_b"], *lru["bwd"], reverse=True, rider=gather("w_down"))
    p.update(zip(["w_down"], got))
    outs, lses = [], []
    for q, k, v, bias in zip(qs, ks, vs, biases):
        o, l = _attn_fwd(q, k, v, bias)
        outs.append(o)
        lses.append(l)
    x1, mixb, *yl = _mix_fwd(outs, lses, hf, hb, gate, x, p["norm_rnn_g"], p["norm_attn_g"], p["w_out"])
    yas, lsts = [yl[0], yl[1], yl[1]], [yl[2], yl[3], yl[3]]
    dx1, h2b, a2b, dub, dx2b, loss_vec, dg_fin, dg_mlp = _mlp_fwd_bwd(
        x1, target, p["mlp_norm_g"], p["final_norm_g"], p["w_up"], p["w_down"])
    dhs, dgate, dya1, dya4, dw_out, dg_rnn, dg_attn = _mix_bwd(dx1, p["w_out"], mixb, yas[0], hf, hb, gate,
                                                               p["norm_rnn_g"], p["norm_attn_g"])
    dyas = [dya1, dya4, dya4]
    dw_up = _dw_matmul(h2b, dub, D_MODEL, FF_BLK, "dw_up")
    dw_down = _dw_matmul(a2b, dx2b, FF_BLK, D_MODEL, "dw_down")
    early = [dw_out, dw_up, dw_down]
    dqs, dks, dvs, dbs = [], [], [], []
    for i, (q, k, v, bias, dya, ya, lse) in enumerate(zip(qs, ks, vs, biases, dyas, yas, lsts)):
        rider = None
        if shards is not None:
            make = (lambda: _pair_exchange_rider(EARLY, early), lambda: _chip_exchange_rider(early),
                    lambda: _pair_share_rider(EARLY, early))[i]
            rider = make()
        (dq, dk, dv, db), got = _attn_bwd(q, k, v, bias, dya, ya, lse, rider)
        if shards is not None and i == 0:
            core = lax.axis_index("c").reshape(1).astype(jnp.int32)
            early = [_pair_add(core, g, o, "grad_pair_add_" + n) for n, g, o in zip(EARLY, early, got)]
        elif shards is not None and i == 1:
            early = [_chip_sum(t, "grad_chip_sum_" + n) for n, t in zip(EARLY, got)]
        elif shards is not None:
            early = got
        dqs.append(dq)
        dks.append(dk)
        dvs.append(dv)
        dbs.append(db)
    d_rel_bias = _bias_grad(dbs).T
    (dxc_f, dwa_f, dwx_f, dvec_f), _ = _rnn_bwd(xr, hf, dhs, p["conv_w"], p["conv_b"], *lru["fwd"], reverse=False)
    small = {
        "lru_wa_fwd": _diag_blocks(dwa_f), "lru_ba_fwd": dvec_f[0:1], "lru_wx_fwd": _diag_blocks(dwx_f),
        "lru_bx_fwd": dvec_f[1:2], "lru_lam_fwd": dvec_f[3:4],
        "rel_bias": d_rel_bias, "norm_rnn_g": dg_rnn, "norm_attn_g": dg_attn,
        "mlp_norm_g": dg_mlp, "final_norm_g": dg_fin,
    }
    loss_local = (0.5 / D_MODEL) * jnp.sum(loss_vec)
    rider = None
    if shards is not None:
        rider = _small_gather_rider(_pack([small[n].reshape(shp) for n, shp in SMALL if n in small]
                                          + [loss_local.reshape(1)]))
    (dxc_b, dwa_b, dwx_b, dvec_b), gathered = _rnn_bwd(xr, hb, dhs, p["conv_w"], p["conv_b"], *lru["bwd"], reverse=True,
                                                       rider=rider)
    grad_x, dw_in, dg1, dconv = _inproj_bwd(x, dx1, xr, dxc_f, dxc_b, dgate, dqs, dks, dvs,
                                            p["attn_norm_g"], p["conv_w"], p["w_in"])[0]
    last = {"lru_wa_bwd": _diag_blocks(dwa_b), "lru_ba_bwd": dvec_b[0:1], "lru_wx_bwd": _diag_blocks(dwx_b),
            "lru_bx_bwd": dvec_b[1:2], "lru_lam_bwd": dvec_b[3:4],
            "attn_norm_g": dg1, "conv_w": dconv[0:4], "conv_b": dconv[4:5]}
    if shards is None:
        big = {"w_in": dw_in, "w_out": dw_out, "w_up": dw_up, "w_down": dw_down}
        return loss_local, grad_x, {**small, **last}, None, big, {}
    return loss_local, grad_x, last, gathered[0], {"w_in": dw_in}, dict(zip(EARLY, early))


BIG = ("w_in", "w_out", "w_up", "w_down")
BIG_SHARD = {"w_in": (D_MODEL, IN_BLK), "w_out": (OUT_BLK, D_MODEL), "w_up": (D_MODEL, FF_BLK), "w_down": (FF_BLK, D_MODEL)}
N_BIG = len(BIG)
N_CHIP_PEERS = 3
ANY = pl.BlockSpec(memory_space=pl.ANY)


def _place():
    x, y, c = lax.axis_index("x"), lax.axis_index("y"), lax.axis_index("c")
    chips = [(1 - x, y), (x, 1 - y), (1 - x, 1 - y)]
    return x, y, c, chips


def _remote(src, dst, send_sem, recv_sem, dev):
    return pltpu.make_async_remote_copy(src_ref=src, dst_ref=dst, send_sem=send_sem, recv_sem=recv_sem,
                                        device_id=dev, device_id_type=MESH)


def _staged_start(srcs, bufs, sems):
    legs = [pltpu.make_async_copy(s, b, sems.at[i]) for i, (s, b) in enumerate(zip(srcs, bufs))]
    for cp in legs:
        cp.start()
    return legs


def _staged_finish(legs, bufs, dsts, sems):
    out = []
    for i, (leg, b, d) in enumerate(zip(legs, bufs, dsts)):
        leg.wait()
        cp = pltpu.make_async_copy(b, d, sems.at[i])
        cp.start()
        out.append(cp)
    return out


class _Rider:
    def __init__(self, inputs, out_shape, scratch, first, late, last):
        self.inputs, self.out_shape, self.scratch = list(inputs), list(out_shape), list(scratch)
        self.first, self.late, self.last = first, late, last


def _call(body, name, grid, in_specs, out_specs, out_shape, scratch, operands, rider=None):
    n_grid = len(grid)
    if rider is None:
        res = pl.pallas_call(body, grid=grid, name=name, in_specs=in_specs, out_specs=out_specs, out_shape=out_shape,
                             scratch_shapes=scratch, compiler_params=_params(n_grid))(*operands)
        return list(res), []
    n_in, n_out, n_scr = len(in_specs), len(out_specs), len(scratch)
    ri, ro = len(rider.inputs), len(rider.out_shape)
    nsteps = int(np.prod(grid))
    late_step = max(nsteps - 3, 1)

    def wrapped(*refs):
        a, b = n_in, n_in + ri
        c, d = b + n_out, b + n_out + ro
        e = d + n_scr
        mine = refs[:a] + refs[b:c] + refs[d:e]
        theirs = (refs[a:b], refs[c:d], refs[e:])
        step = pl.program_id(0)
        for ax in range(1, n_grid):
            step = step * grid[ax] + pl.program_id(ax)
        pl.when(step == 0)(lambda: rider.first(*theirs))
        pl.when(step == late_step)(lambda: rider.late(*theirs))
        body(*mine)
        pl.when(step == nsteps - 1)(lambda: rider.last(*theirs))

    res = pl.pallas_call(
        wrapped, grid=grid, name=name, in_specs=list(in_specs) + [ANY] * ri, out_specs=list(out_specs) + [ANY] * ro,
        out_shape=list(out_shape) + rider.out_shape, scratch_shapes=list(scratch) + rider.scratch,
        compiler_params=_params(n_grid),
    )(*operands, *rider.inputs)
    return list(res[:n_out]), list(res[n_out:])


def _run_rider(rider, name):
    ri, ro = len(rider.inputs), len(rider.out_shape)

    def body(*refs):
        parts = (refs[:ri], refs[ri:ri + ro], refs[ri + ro:])
        rider.first(*parts)
        rider.late(*parts)
        rider.last(*parts)

    return list(pl.pallas_call(
        body, name=name, in_specs=[ANY] * ri, out_specs=[ANY] * ro, out_shape=rider.out_shape, scratch_shapes=rider.scratch,
        compiler_params=pltpu.CompilerParams(has_side_effects=True, vmem_limit_bytes=VMEM_LIMIT),
    )(*rider.inputs))


def _nothing(ins, outs, scr):
    return None


def _gather_rider(names, shards, conv_w=None):
    n = len(names)
    items = n + (conv_w is not None)
    halves = [BIG_SHARD[nm][0] // 2 for nm in names]

    def parts(ins, outs, scr):
        x, y, c, chips = _place()
        return x, y, c, chips, 2 * x + y, (x, y, 1 - c), scr[:8], scr[8:]

    def piece(outs, w, chip, core_half):
        return outs[w].at[chip, pl.ds(core_half * halves[w], halves[w])]

    def ici(ins, outs, sems, w, k, chip_xy, c, me):
        return _remote(ins[w].at[pl.ds(c * halves[w], halves[w])], piece(outs, w, me, c),
                       sems[0].at[w, k], sems[1].at[w, k], (*chip_xy, c))

    def first(ins, outs, scr):
        x, y, c, chips, me, sibling, sems, bufs = parts(ins, outs, scr)
        legs = _staged_start(ins, bufs, sems[6])
        for w in range(n):
            for k, chip_xy in enumerate(chips):
                ici(ins, outs, sems, w, k, chip_xy, c, me).start()
        if conv_w is not None:
            for k, (px, py) in enumerate(chips):
                _remote(ins[n], outs[n].at[me], sems[4].at[k], sems[5].at[k], (px, py, c)).start()
        _staged_finish(legs, bufs, [o.at[me] for o in outs], sems[7])

    def late(ins, outs, scr):
        x, y, c, chips, me, sibling, sems, bufs = parts(ins, outs, scr)
        for w in range(n):
            for k, (px, py) in enumerate(chips):
                landed = piece(outs, w, 2 * px + py, c)
                _remote(landed, landed, sems[0].at[w, k], sems[1].at[w, k], (px, py, c)).wait_recv()
                _remote(landed, landed, sems[2].at[w, k], sems[3].at[w, k], sibling).start()

    def last(ins, outs, scr):
        x, y, c, chips, me, sibling, sems, bufs = parts(ins, outs, scr)
        for w in range(n):
            for k, (px, py) in enumerate(chips):
                other = piece(outs, w, 2 * px + py, 1 - c)
                _remote(other, other, sems[2].at[w, k], sems[3].at[w, k], sibling).wait_recv()
        if conv_w is not None:
            for k, (px, py) in enumerate(chips):
                got = outs[n].at[2 * px + py]
                _remote(got, got, sems[4].at[k], sems[5].at[k], (px, py, c)).wait_recv()
                _remote(ins[n], outs[n].at[me], sems[4].at[k], sems[5].at[k], (px, py, c)).wait_send()
        for i in range(items):
            pltpu.make_async_copy(bufs[i], outs[i].at[me], sems[7].at[i]).wait()
        for w in range(n):
            for k, (px, py) in enumerate(chips):
                ici(ins, outs, sems, w, k, (px, py), c, me).wait_send()
                landed = piece(outs, w, 2 * px + py, c)
                _remote(landed, landed, sems[2].at[w, k], sems[3].at[w, k], sibling).wait_send()

    out_shape = [jax.ShapeDtypeStruct((N_SHARD,) + BIG_SHARD[nm], BF16) for nm in names]
    stage = [pltpu.VMEM(BIG_SHARD[nm], BF16) for nm in names]
    inputs = list(shards)
    if conv_w is not None:
        out_shape.append(jax.ShapeDtypeStruct((N_SHARD,) + conv_w.shape, F32))
        stage.append(pltpu.VMEM(conv_w.shape, F32))
        inputs.append(conv_w)
    scratch = ([pltpu.SemaphoreType.DMA((n, N_CHIP_PEERS))] * 4 + [pltpu.SemaphoreType.DMA((N_CHIP_PEERS,))] * 2
               + [pltpu.SemaphoreType.DMA((items,))] * 2 + stage)
    return _Rider(inputs, out_shape, scratch, first, late, last)


def _pair_exchange_rider(names, grads):
    def copies(ins, outs, scr):
        x, y, c, _ = _place()
        out = []
        for w, nm in enumerate(names):
            h = BIG_SHARD[nm][0] // 2
            out.append(_remote(ins[w].at[:, pl.ds((1 - c) * h, h), :], outs[w], scr[0].at[w], scr[1].at[w], (x, y, 1 - c)))
        return out

    def first(ins, outs, scr):
        for cp in copies(ins, outs, scr):
            cp.start()

    def last(ins, outs, scr):
        for cp in copies(ins, outs, scr):
            cp.wait()

    out_shape = [jax.ShapeDtypeStruct((N_SHARD, BIG_SHARD[nm][0] // 2, BIG_SHARD[nm][1]), F32) for nm in names]
    return _Rider(grads, out_shape, [pltpu.SemaphoreType.DMA((len(names),))] * 2, first, _nothing, last)


def _pair_add(core, grad, other, name):
    _, r, cols = grad.shape
    h = r // 2
    th = min(h, 256)
    per = h // th

    def body(c_ref, g_ref, o_ref, out_ref):
        out_ref[...] = (g_ref[...] + o_ref[...]).astype(BF16)

    return pl.pallas_call(
        body, name=name,
        grid_spec=pltpu.PrefetchScalarGridSpec(
            num_scalar_prefetch=1, grid=(N_SHARD, per),
            in_specs=[pl.BlockSpec((1, th, cols), lambda j, i, c_ref: (j, c_ref[0] * per + i, 0)),
                      pl.BlockSpec((1, th, cols), lambda j, i, c_ref: (j, i, 0))],
            out_specs=pl.BlockSpec((1, th, cols), lambda j, i, c_ref: (j, i, 0))),
        out_shape=jax.ShapeDtypeStruct((N_SHARD, h, cols), BF16),
        compiler_params=_params(2),
    )(core, grad, other)


def _chip_exchange_rider(parts):
    n = len(parts)

    def sends(ins, outs, scr):
        x, y, c, chips = _place()
        me = 2 * x + y
        return [_remote(ins[w].at[2 * px + py], outs[w].at[me], scr[0].at[w, k], scr[1].at[w, k], (px, py, c))
                for w in range(n) for k, (px, py) in enumerate(chips)]

    def first(ins, outs, scr):
        x, y, c, chips = _place()
        me = 2 * x + y
        legs = _staged_start([r.at[me] for r in ins], scr[4:], scr[2])
        for cp in sends(ins, outs, scr):
            cp.start()
        _staged_finish(legs, scr[4:], [o.at[me] for o in outs], scr[3])

    def last(ins, outs, scr):
        x, y, c, chips = _place()
        me = 2 * x + y
        for w in range(n):
            for k, (px, py) in enumerate(chips):
                got = outs[w].at[2 * px + py]
                _remote(got, got, scr[0].at[w, k], scr[1].at[w, k], (px, py, c)).wait_recv()
        for cp in sends(ins, outs, scr):
            cp.wait_send()
        for w in range(n):
            pltpu.make_async_copy(scr[4 + w], outs[w].at[me], scr[3].at[w]).wait()

    out_shape = [jax.ShapeDtypeStruct(p.shape, BF16) for p in parts]
    scratch = ([pltpu.SemaphoreType.DMA((n, N_CHIP_PEERS))] * 2 + [pltpu.SemaphoreType.DMA((n,))] * 2
               + [pltpu.VMEM(p.shape[1:], BF16) for p in parts])
    return _Rider(parts, out_shape, scratch, first, _nothing, last)


def _chip_sum(parts, name):
    _, h, cols = parts.shape
    th = min(h, 256)

    def body(p_ref, out_ref):
        acc = p_ref[0].astype(F32)
        for j in range(1, N_SHARD):
            acc = acc + p_ref[j].astype(F32)
        out_ref[...] = acc

    return pl.pallas_call(
        body, name=name, grid=(h // th,),
        in_specs=[pl.BlockSpec((N_SHARD, th, cols), lambda i: (0, i, 0))],
        out_specs=pl.BlockSpec((th, cols), lambda i: (i, 0)),
        out_shape=jax.ShapeDtypeStruct((h, cols), F32),
        compiler_params=_params(),
    )(parts)


def _pair_share_rider(names, halves):
    n = len(names)
    hs = [BIG_SHARD[nm][0] // 2 for nm in names]

    def mine(outs, c):
        return [outs[w].at[pl.ds(c * hs[w], hs[w])] for w in range(n)]

    def first(ins, outs, scr):
        x, y, c, _ = _place()
        legs = _staged_start(ins, scr[4:], scr[2])
        for w, dst in enumerate(mine(outs, c)):
            _remote(ins[w], dst, scr[0].at[w], scr[1].at[w], (x, y, 1 - c)).start()
        _staged_finish(legs, scr[4:], mine(outs, c), scr[3])

    def last(ins, outs, scr):
        x, y, c, _ = _place()
        for w, (theirs, dst) in enumerate(zip(mine(outs, 1 - c), mine(outs, c))):
            _remote(theirs, theirs, scr[0].at[w], scr[1].at[w], (x, y, 1 - c)).wait_recv()
            _remote(ins[w], dst, scr[0].at[w], scr[1].at[w], (x, y, 1 - c)).wait_send()
            pltpu.make_async_copy(scr[4 + w], dst, scr[3].at[w]).wait()

    out_shape = [jax.ShapeDtypeStruct(BIG_SHARD[nm], F32) for nm in names]
    scratch = [pltpu.SemaphoreType.DMA((n,))] * 4 + [pltpu.VMEM((h, BIG_SHARD[nm][1]), F32) for nm, h in zip(names, hs)]
    return _Rider(halves, out_shape, scratch, first, _nothing, last)


N_DEV = 8


def _all_peers(x, y, c):
    return [((1 - x) if fx else x, (1 - y) if fy else y, (1 - c) if fc else c)
            for fx in (0, 1) for fy in (0, 1) for fc in (0, 1) if fx or fy or fc]


def _small_gather_rider(vec):
    def sends(ins, outs, scr):
        x, y, c, _ = _place()
        me = 4 * x + 2 * y + c
        return [_remote(ins[0], outs[0].at[me], scr[0].at[k], scr[1].at[k], dev) for k, dev in enumerate(_all_peers(x, y, c))]

    def first(ins, outs, scr):
        x, y, c, _ = _place()
        legs = _staged_start(ins, scr[4:], scr[2])
        for cp in sends(ins, outs, scr):
            cp.start()
        _staged_finish(legs, scr[4:], [outs[0].at[4 * x + 2 * y + c]], scr[3])

    def last(ins, outs, scr):
        x, y, c, _ = _place()
        for k, (px, py, pc) in enumerate(_all_peers(x, y, c)):
            got = outs[0].at[4 * px + 2 * py + pc]
            _remote(got, got, scr[0].at[k], scr[1].at[k], (px, py, pc)).wait_recv()
        for cp in sends(ins, outs, scr):
            cp.wait_send()
        pltpu.make_async_copy(scr[4], outs[0].at[4 * x + 2 * y + c], scr[3].at[0]).wait()

    scratch = ([pltpu.SemaphoreType.DMA((N_DEV - 1,))] * 2 + [pltpu.SemaphoreType.DMA((1,))] * 2
               + [pltpu.VMEM(vec.shape, F32)])
    return _Rider([vec], [jax.ShapeDtypeStruct((N_DEV,) + vec.shape, F32)], scratch, first, _nothing, last)


def _sum_devices(gathered):
    def body(g_ref, out_ref):
        acc = g_ref[0]
        for j in range(1, N_DEV):
            acc = acc + g_ref[j]
        out_ref[...] = acc

    return pl.pallas_call(body, name="sum_devices", out_shape=jax.ShapeDtypeStruct(gathered.shape[1:], F32),
                          compiler_params=_params(0))(gathered)


def _allreduce_small(vec):
    rows = vec.shape[0]

    def body(v_ref, sum_ref, gat_ref, send, recv, loc_sem):
        x, y, c, _ = _place()
        me = 4 * x + 2 * y + c
        lc = pltpu.make_async_copy(v_ref, gat_ref.at[me], loc_sem)
        lc.start()
        peers = []
        for fx in (0, 1):
            for fy in (0, 1):
                for fc in (0, 1):
                    if fx or fy or fc:
                        peers.append(((1 - x) if fx else x, (1 - y) if fy else y, (1 - c) if fc else c))
        sends = []
        for k, dev in enumerate(peers):
            cp = _remote(v_ref, gat_ref.at[me], send.at[k], recv.at[k], dev)
            cp.start()
            sends.append(cp)
        for k, (px, py, pc) in enumerate(peers):
            got = gat_ref.at[4 * px + 2 * py + pc]
            _remote(got, got, send.at[k], recv.at[k], (px, py, pc)).wait_recv()
        for cp in sends:
            cp.wait_send()
        lc.wait()
        acc = gat_ref[0]
        for j in range(1, N_DEV):
            acc = acc + gat_ref[j]
        sum_ref[...] = acc

    total, _ = pl.pallas_call(
        body, name="allreduce_small",
        in_specs=[_whole_vmem()], out_specs=[_whole_vmem(), _whole_vmem()],
        out_shape=[jax.ShapeDtypeStruct((rows, 128), F32), jax.ShapeDtypeStruct((N_DEV, rows, 128), F32)],
        scratch_shapes=[pltpu.SemaphoreType.DMA((N_DEV - 1,))] * 2 + [pltpu.SemaphoreType.DMA(())],
        compiler_params=pltpu.CompilerParams(has_side_effects=True, vmem_limit_bytes=VMEM_LIMIT),
    )(vec)
    return total


def _adam_math(w_ref, g_ref, m_ref, v_ref, d_ref, m2_ref, v2_ref):
    c1 = 1.0 - ADAM_B1 ** ADAM_STEP
    c2 = 1.0 - ADAM_B2 ** ADAM_STEP
    gv = g_ref[...]
    m2 = ADAM_B1 * m_ref[...] + (1.0 - ADAM_B1) * gv
    v2 = ADAM_B2 * v_ref[...] + (1.0 - ADAM_B2) * (gv * gv)
    m2_ref[...] = m2
    v2_ref[...] = v2
    d_ref[...] = -ADAM_LR * ((m2 / c1) / (jnp.sqrt(v2 / c2) + ADAM_EPS) + ADAM_WD * w_ref[...])


def _adamw_many(ws, gs, ms, vs):
    n = len(ws)

    def body(*refs):
        for i in range(n):
            _adam_math(*[refs[k * n + i] for k in range(7)])

    shapes = [jax.ShapeDtypeStruct(w.shape, F32) for w in ws]
    res = pl.pallas_call(body, name="adamw_small", out_shape=shapes * 3, compiler_params=_params(0))(*ws, *gs, *ms, *vs)
    return res[:n], res[n:2 * n], res[2 * n:]


def _adamw(w, g, m, v, name):
    rows, cols = w.shape
    tr = 256 if rows % 256 == 0 else rows

    def body(w_ref, g_ref, m_ref, v_ref, d_ref, m2_ref, v2_ref):
        _adam_math(w_ref, g_ref, m_ref, v_ref, d_ref, m2_ref, v2_ref)

    spec = pl.BlockSpec((tr, cols), lambda i: (i, 0))
    shp = jax.ShapeDtypeStruct((rows, cols), F32)
    return pl.pallas_call(
        body, name=name, grid=(rows // tr,), in_specs=[spec] * 4, out_specs=[spec] * 3, out_shape=[shp] * 3,
        compiler_params=_params(),
    )(w, g, m, v)


SMALL = (
    ("attn_norm_g", (1, 1024)), ("conv_w", (1, 4, 512)), ("conv_b", (1, 512)),
    ("lru_wa_fwd", (1, 8, 64, 64)), ("lru_ba_fwd", (1, 512)), ("lru_wx_fwd", (1, 8, 64, 64)), ("lru_bx_fwd", (1, 512)),
    ("lru_lam_fwd", (1, 512)),
    ("lru_wa_bwd", (1, 8, 64, 64)), ("lru_ba_bwd", (1, 512)), ("lru_wx_bwd", (1, 8, 64, 64)), ("lru_bx_bwd", (1, 512)),
    ("lru_lam_bwd", (1, 512)),
    ("rel_bias", (32, 8)), ("norm_rnn_g", (1, 512)), ("norm_attn_g", (1, 512)), ("mlp_norm_g", (1, 1024)),
    ("final_norm_g", (1024,)),
)
PACK_ROW = 8 * 128


def _pack(parts):
    flat = jnp.concatenate([p.reshape(-1) for p in parts])
    pad = (-flat.shape[0]) % PACK_ROW
    return jnp.pad(flat, (0, pad)).reshape(-1, 128)


def _unpack(packed, shapes):
    flat = packed.reshape(-1)
    out, off = [], 0
    for shp in shapes:
        n = int(np.prod(shp))
        out.append(flat[off:off + n].reshape(shp))
        off += n
    return out


WEIGHT_ORDER = ("attn_norm_g", "w_in", "conv_w", "conv_b", "lru_wa_fwd", "lru_ba_fwd", "lru_wx_fwd", "lru_bx_fwd",
                "lru_lam_fwd", "lru_wa_bwd", "lru_ba_bwd", "lru_wx_bwd", "lru_bx_bwd", "lru_lam_bwd", "rel_bias",
                "norm_rnn_g", "norm_attn_g", "w_out", "mlp_norm_g", "w_up", "w_down", "final_norm_g")


def kernel(x, attn_norm_g, w_in, conv_w, conv_b, lru_wa_fwd, lru_ba_fwd, lru_wx_fwd, lru_bx_fwd, lru_lam_fwd, lru_wa_bwd, lru_ba_bwd, lru_wx_bwd, lru_bx_bwd, lru_lam_bwd, rel_bias, norm_rnn_g, norm_attn_g, w_out, mlp_norm_g, w_up, w_down, final_norm_g, loss_target, m_attn_norm_g, m_w_in, m_conv_w, m_conv_b, m_lru_wa_fwd, m_lru_ba_fwd, m_lru_wx_fwd, m_lru_bx_fwd, m_lru_lam_fwd, m_lru_wa_bwd, m_lru_ba_bwd, m_lru_wx_bwd, m_lru_bx_bwd, m_lru_lam_bwd, m_rel_bias, m_norm_rnn_g, m_norm_attn_g, m_w_out, m_mlp_norm_g, m_w_up, m_w_down, m_final_norm_g, v_attn_norm_g, v_w_in, v_conv_w, v_conv_b, v_lru_wa_fwd, v_lru_ba_fwd, v_lru_wx_fwd, v_lru_bx_fwd, v_lru_lam_fwd, v_lru_wa_bwd, v_lru_ba_bwd, v_lru_wx_bwd, v_lru_bx_bwd, v_lru_lam_bwd, v_rel_bias, v_norm_rnn_g, v_norm_attn_g, v_w_out, v_mlp_norm_g, v_w_up, v_w_down, v_final_norm_g):
    given = dict(locals())
    w = {n: given[n] for n in WEIGHT_ORDER}
    m = {n: given["m_" + n] for n in WEIGHT_ORDER}
    v = {n: given["v_" + n] for n in WEIGHT_ORDER}

    chip = lax.axis_index("x") * 2 + lax.axis_index("y")
    core = lax.axis_index("c")

    shards = {n: w[n][0].astype(BF16) for n in BIG}
    shards["conv_w"] = w["conv_w"][0]
    p = {n: (t[0] if t.ndim >= 3 else t) for n, t in w.items() if n not in BIG and n != "conv_w"}
    p["final_norm_g"] = w["final_norm_g"].reshape(1, D_MODEL)

    _, grad_x, small, gathered, big, reduced = _local_step(x[0], loss_target[0], p, shards)

    late = tuple(big)
    grads = [big[n] for n in late]
    others = _run_rider(_pair_exchange_rider(late, grads), "grad_pair_exchange")
    core_arr = core.reshape(1).astype(jnp.int32)
    parts = [_pair_add(core_arr, g, o, "grad_pair_add_" + n) for n, g, o in zip(late, grads, others)]
    landed = _run_rider(_chip_exchange_rider(parts), "grad_chip_exchange")
    halves = [_chip_sum(t, "grad_chip_sum_" + n) for n, t in zip(late, landed)]
    reduced.update(zip(late, _run_rider(_pair_share_rider(late, halves), "grad_pair_share")))

    early_small = [(n, shp) for n, shp in SMALL if n not in small]
    late_small = [(n, shp) for n, shp in SMALL if n in small]
    *early_g, loss = _unpack(_sum_devices(gathered), [shp for _, shp in early_small] + [(1,)])
    late_g = _unpack(_allreduce_small(_pack([small[n].reshape(shp) for n, shp in late_small])),
                     [shp for _, shp in late_small])
    g = dict(zip([n for n, _ in early_small + late_small], early_g + late_g))
    g["conv_w"] = lax.dynamic_slice_in_dim(g["conv_w"], chip * (D_RNN // N_SHARD), D_RNN // N_SHARD, axis=2)
    for n in BIG:
        g[n] = reduced[n][None]

    delta, new_m, new_v = {}, {}, {}
    for n in BIG:
        d2, m2, v2 = _adamw(w[n][0], reduced[n], m[n][0], v[n][0], "adamw_" + n)
        delta[n], new_m[n], new_v[n] = d2[None], m2[None], v2[None]
    names = [n for n, _ in SMALL]
    for dst, src in zip((delta, new_m, new_v), _adamw_many(*[[t[n] for n in names] for t in (w, g, m, v)])):
        dst.update(dict(zip(names, src)))

    return (loss.reshape(()), grad_x[None], *[g[n] for n in WEIGHT_ORDER], *[delta[n] for n in WEIGHT_ORDER],
            *[new_m[n] for n in WEIGHT_ORDER], *[new_v[n] for n in WEIGHT_ORDER])
```

```python
import functools
import math

import numpy as np
import jax
import jax.numpy as jnp
from jax import lax
from jax.experimental import pallas as pl
from jax.experimental.pallas import tpu as pltpu

F32 = jnp.float32
BF16 = jnp.bfloat16

D_MODEL = 1024
D_RNN = 512
D_ATTN = 512
N_HEADS = 8
HEAD_DIM = 64
N_RNN_BLOCKS = 8
RNN_BLOCK = 64
D_IN = 2 * D_RNN + 3 * D_ATTN
D_FF = 4 * D_MODEL
N_SHARD = 4
IN_BLK = D_IN // N_SHARD
OUT_BLK = D_MODEL // N_SHARD
FF_BLK = D_FF // N_SHARD
EPS = 1e-6
NEG_INF = -1e30
LRU_C = 8.0
DILATIONS = (1, 4, 16)
F32_LAYOUT = 4
HALF_WIN = 64
Q_BLK = 128
K_WIN = 256
N_BUCKETS = 32
MAX_DISTANCE = 1024
ATTN_SCALE = HEAD_DIM ** -0.5

ADAM_LR = 0.001
ADAM_B1 = 0.9
ADAM_B2 = 0.999
ADAM_EPS = 1e-08
ADAM_WD = 0.01
ADAM_STEP = 10

TS = 512
TS_MLP = 256
TS_INPROJ_BWD = 256
ATTN_SUB = 16
TK_DW = 2048
SCAN_UNROLL = 8
SUB = 8
VMEM_LIMIT = 56 * 1024 * 1024
GELU_C0 = math.sqrt(2.0 / math.pi)
GELU_C1 = 0.044715

MESH = pl.DeviceIdType.MESH


def _params(n_grid=1):
    return pltpu.CompilerParams(vmem_limit_bytes=VMEM_LIMIT, dimension_semantics=("arbitrary",) * n_grid)


def _whole_vmem():
    return pl.BlockSpec(memory_space=pltpu.VMEM)


def _rows(width, tile=TS):
    return pl.BlockSpec((tile, width), lambda i: (i, 0))


def _sigmoid(z):
    return 0.5 * jnp.tanh(0.5 * z) + 0.5


def _log1p(u):
    w = 1.0 + u
    return jnp.where(w == 1.0, u, jnp.log(w) * (u / (w - 1.0)))


def _softplus(z):
    return jnp.maximum(z, 0.0) + _log1p(jnp.exp(-jnp.abs(z)))


def _gelu_parts(g):
    inner = GELU_C0 * (g + GELU_C1 * g * g * g)
    t = jnp.tanh(inner)
    val = 0.5 * g * (1.0 + t)
    dinner = GELU_C0 * (1.0 + 3.0 * GELU_C1 * g * g)
    grad = 0.5 * (1.0 + t) + 0.5 * g * (1.0 - t * t) * dinner
    return val, grad


def _rms(x):
    rstd = lax.rsqrt(jnp.mean(x * x, axis=-1, keepdims=True) + EPS)
    return rstd, x * rstd


def _rms_bwd(dy, g, xhat, rstd):
    dxh = dy * g
    dx = rstd * (dxh - xhat * jnp.mean(dxh * xhat, axis=-1, keepdims=True))
    dg = jnp.sum(dy * xhat, axis=0, keepdims=True)
    return dx, dg


def _dot(a, b):
    return jnp.dot(a, b, preferred_element_type=F32)


def _dot_nt(a, b):
    return lax.dot_general(a, b, (((1,), (1,)), ((), ())), preferred_element_type=F32)


def _dot_tn(a, b):
    return lax.dot_general(a, b, (((0,), (0,)), ((), ())), preferred_element_type=F32)


def _shifted(tile, prev8, next8, k):
    n = tile.shape[0]
    row = lax.broadcasted_iota(jnp.int32, tile.shape, 0)
    if k == 0:
        return tile
    if k < 0:
        r = pltpu.roll(tile, -k, 0)
        for j in range(-k):
            r = jnp.where(row == j, prev8[SUB + j + k:SUB + j + k + 1, :], r)
        return r
    r = pltpu.roll(tile, n - k, 0)
    for j in range(k):
        r = jnp.where(row == n - k + j, next8[j:j + 1, :], r)
    return r


def _to_lane_blocks(val, s_ref):
    for j in range(val.shape[1] // 128):
        s_ref[j] = val[:, j * 128:(j + 1) * 128]


def _from_lane_blocks(s_ref):
    return jnp.concatenate([s_ref[j] for j in range(s_ref.shape[0])], axis=-1)


def _class_rows(s_ref, r, dil):
    n = s_ref.shape[1] // dil
    return jnp.concatenate([s_ref[j, pl.ds(r, n, stride=dil), :] for j in range(s_ref.shape[0])], axis=-1)


def _split_classes(val, s_ref, out_ref, dil):
    _to_lane_blocks(val, s_ref)
    for r in range(dil):
        out_ref[r] = _class_rows(s_ref, r, dil).astype(out_ref.dtype)


def _merge_classes(in_ref, s_ref, dil, also_ref=None):
    n = s_ref.shape[1] // dil
    for r in range(dil):
        v = in_ref[r] if also_ref is None else in_ref[r] + also_ref[r]
        for j in range(s_ref.shape[0]):
            s_ref[j, pl.ds(r, n, stride=dil), :] = v[:, j * 128:(j + 1) * 128]
    return _from_lane_blocks(s_ref)


def _class_spec(dil, tile=TS):
    return pl.BlockSpec((dil, tile // dil, 512), lambda i: (0, i, 0))


def _class_shape(S, dil, dtype):
    return jax.ShapeDtypeStruct((dil, S // dil, 512), dtype)


def _scan_tile(a_ref, b_ref, h_ref, carry_ref, reverse):
    n = a_ref.shape[0]
    width = a_ref.shape[1]
    groups = n // SUB
    row = lax.broadcasted_iota(jnp.int32, (SUB, width), 0)

    def group_scan(g):
        r0 = pl.multiple_of(g * SUB, SUB)
        a = a_ref[pl.ds(r0, SUB), :]
        b = b_ref[pl.ds(r0, SUB), :]
        for s in (1, 2, 4):
            if reverse:
                a_sh = pltpu.roll(a, SUB - s, 0)
                b_sh = pltpu.roll(b, SUB - s, 0)
                m = row < SUB - s
            else:
                a_sh = pltpu.roll(a, s, 0)
                b_sh = pltpu.roll(b, s, 0)
                m = row >= s
            b = jnp.where(m, a * b_sh + b, b)
            a = jnp.where(m, a * a_sh, a)
        return r0, a, b

    def step(i, carry):
        first = i * SCAN_UNROLL
        order = [(groups - 1 - (first + u)) if reverse else (first + u) for u in range(SCAN_UNROLL)]
        scans = [group_scan(g) for g in order]
        for r0, a, b in scans:
            h = b + a * carry
            h_ref[pl.ds(r0, SUB), :] = h
            edge = h[0:1, :] if reverse else h[SUB - 1:SUB, :]
            carry = jnp.broadcast_to(edge, (SUB, width))
        return carry

    carry_ref[...] = lax.fori_loop(0, groups // SCAN_UNROLL, step, carry_ref[...])


def _conv_fwd(xr, prev8, next8, cw, cb):
    y = cb + _shifted(xr, prev8, next8, -2) * cw[0:1, :]
    y = y + _shifted(xr, prev8, next8, -1) * cw[1:2, :]
    y = y + xr * cw[2:3, :]
    y = y + _shifted(xr, prev8, next8, 1) * cw[3:4, :]
    return y


def _lru_gates(xc, wa_ref, ba, wx_ref, bx, lam):
    xcb = xc.astype(BF16)
    r = _sigmoid(_dot(xcb, wa_ref[...]) + ba)
    i = _sigmoid(_dot(xcb, wx_ref[...]) + bx)
    cl = -LRU_C * _softplus(-lam)
    la = cl * r
    a = jnp.exp(la)
    m2 = -jnp.tanh(la) * (a * a + 1.0)
    inv = jnp.where(m2 > 0.0, lax.rsqrt(m2), 0.0)
    mult = m2 * inv
    return xcb, r, i, cl, a, mult, inv


def _inproj_fwd(x, g1, w_in, rider=None):
    S = x.shape[0]

    def body(x_ref, g_ref, w_ref, xr_ref, gate_ref, *rest):
        qkv_refs, s_ref, s4_ref = rest[:9], rest[9], rest[10]
        _, xh = _rms(x_ref[...])
        h = (xh * g_ref[...]).astype(BF16)
        proj = jnp.concatenate([_dot(h, w_ref[j]) for j in range(N_SHARD)], axis=-1)
        xr_ref[...] = proj[:, 0:512]
        gate_ref[...] = proj[:, 512:1024]
        for t in range(3):
            val = proj[:, 1024 + 512 * t:1536 + 512 * t]
            d1_ref, d4_ref, d16_ref = qkv_refs[3 * t:3 * t + 3]
            d1_ref[0] = val.astype(BF16)
            _to_lane_blocks(val, s_ref)
            for r4 in range(4):
                c4 = _class_rows(s_ref, r4, 4)
                d4_ref[r4] = c4.astype(BF16)
                _to_lane_blocks(c4, s4_ref.at[r4])
            for r4 in range(4):
                for m in range(4):
                    d16_ref[r4 + 4 * m] = _class_rows(s4_ref.at[r4], m, 4).astype(BF16)

    f = jax.ShapeDtypeStruct((S, 512), F32)
    return _call(
        body, "inproj_fwd", (S // TS,),
        [_rows(D_MODEL), _whole_vmem(), _whole_vmem()],
        [_rows(512)] * 2 + [_class_spec(d) for d in DILATIONS] * 3,
        [f, f] + [_class_shape(S, d, BF16) for d in DILATIONS] * 3,
        [pltpu.VMEM((4, TS, 128), F32), pltpu.VMEM((4, 4, TS // 4, 128), F32)], (x, g1, w_in), rider)


def _halo_specs(S, order, tile=TS):
    per = tile // SUB
    last = S // SUB - 1
    return [
        pl.BlockSpec((tile, 512), lambda i: (order(i), 0)),
        pl.BlockSpec((SUB, 512), lambda i: (jnp.maximum(order(i) * per - 1, 0), 0)),
        pl.BlockSpec((SUB, 512), lambda i: (jnp.minimum((order(i) + 1) * per, last), 0)),
    ]


def _rnn_fwd(xr, conv_w, conv_b, wa, ba, wx, bx, lam, reverse, rider=None):
    S = xr.shape[0]
    nt = S // TS
    order = (lambda i: nt - 1 - i) if reverse else (lambda i: i)

    def body(x_ref, xp_ref, xn_ref, cw_ref, cb_ref, wa_ref, ba_ref, wx_ref, bx_ref, lam_ref, h_ref, a_s, b_s, carry):
        i = pl.program_id(0)
        t = order(i)

        @pl.when(i == 0)
        def _():
            carry[...] = jnp.zeros_like(carry)

        prev8 = jnp.where(t > 0, xp_ref[...], 0.0)
        next8 = jnp.where(t < nt - 1, xn_ref[...], 0.0)
        xc = _conv_fwd(x_ref[...], prev8, next8, cw_ref[...], cb_ref[...])
        _, _, gi, _, a, mult, _ = _lru_gates(xc, wa_ref, ba_ref[...], wx_ref, bx_ref[...], lam_ref[...])
        a_s[...] = a
        b_s[...] = mult * (gi * xc)
        _scan_tile(a_s, b_s, h_ref, carry, reverse)

    (h,), carried = _call(
        body, "rnn_fwd_rev" if reverse else "rnn_fwd_fwd", (nt,),
        _halo_specs(S, order) + [_whole_vmem()] * 7,
        [pl.BlockSpec((TS, 512), lambda i: (order(i), 0))],
        [jax.ShapeDtypeStruct((S, 512), F32)],
        [pltpu.VMEM((TS, 512), F32), pltpu.VMEM((TS, 512), F32), pltpu.VMEM((SUB, 512), F32)],
        (xr, xr, xr, conv_w, conv_b, wa, ba, wx, bx, lam), rider)
    return h, carried


def _mix_fwd(o3, l3, hf, hb, gate, x, g_rnn, g_attn, w_out):
    S = x.shape[0]

    def body(o1, o2, o3_, l1, l2, l3_, hf_ref, hb_ref, gate_ref, x_ref, gr_ref, ga_ref, w_ref,
             x1_ref, mix_ref, ya1, ya2, ls1, ls2, s_ref):
        la, lb, lc = l1[0], _merge_classes(l2, s_ref, F32_LAYOUT), _merge_classes(l3_, s_ref, F32_LAYOUT)
        m = jnp.maximum(jnp.maximum(la, lb), lc)
        ea, eb, ec = jnp.exp(la - m), jnp.exp(lb - m), jnp.exp(lc - m)
        den = ea + eb + ec
        lse = m + jnp.log(den)
        ya = (ea * o1[0] + eb * _merge_classes(o2, s_ref, F32_LAYOUT) + ec * _merge_classes(o3_, s_ref, F32_LAYOUT)) / den
        ya1[0] = ya
        ls1[0] = lse
        _split_classes(ya, s_ref, ya2, F32_LAYOUT)
        _split_classes(lse, s_ref, ls2, F32_LAYOUT)
        gg, _ = _gelu_parts(gate_ref[...])
        yr = (hf_ref[...] + hb_ref[...]) * gg
        _, xh_r = _rms(yr)
        _, xh_a = _rms(ya)
        mix = jnp.concatenate([xh_r * gr_ref[...], xh_a * ga_ref[...]], axis=-1).astype(BF16)
        mix_ref[...] = mix
        acc = x_ref[...]
        for j in range(N_SHARD):
            acc = acc + _dot(mix[:, j * OUT_BLK:(j + 1) * OUT_BLK], w_ref[j])
        x1_ref[...] = acc

    one, four = _class_spec(1), _class_spec(F32_LAYOUT)
    return pl.pallas_call(
        body, grid=(S // TS,), name="mix_fwd",
        in_specs=[one, four, four] * 2 + [_rows(512)] * 3 + [_rows(D_MODEL)] + [_whole_vmem()] * 3,
        out_specs=[_rows(D_MODEL), _rows(D_MODEL)] + [one, four] * 2,
        out_shape=[jax.ShapeDtypeStruct((S, D_MODEL), F32), jax.ShapeDtypeStruct((S, D_MODEL), BF16)]
        + [_class_shape(S, 1, F32), _class_shape(S, F32_LAYOUT, F32)] * 2,
        scratch_shapes=[pltpu.VMEM((4, TS, 128), F32)],
        compiler_params=_params(),
    )(*o3, *l3, hf, hb, gate, x, g_rnn, g_attn, w_out)


def _mlp_fwd_bwd(x1, target, g_mlp, g_fin, w_up, w_down):
    S = x1.shape[0]
    tm = TS_MLP

    def body(x1_ref, t_ref, gm_ref, gf_ref, wu_ref, wd_ref,
             dx1_ref, h2_ref, a2_ref, du_ref, dx2_ref, loss_ref, dgf_ref, dgm_ref, relu_s):
        @pl.when(pl.program_id(0) == 0)
        def _():
            loss_ref[...] = jnp.zeros_like(loss_ref)
            dgf_ref[...] = jnp.zeros_like(dgf_ref)
            dgm_ref[...] = jnp.zeros_like(dgm_ref)

        x1v = x1_ref[...]
        rstd1, xh1 = _rms(x1v)
        h2 = (xh1 * gm_ref[...]).astype(BF16)
        h2_ref[...] = h2
        x2 = x1v
        for j in range(N_SHARD):
            r = jnp.maximum(_dot(h2, wu_ref[j]), 0.0)
            relu_s[j] = r
            a2 = (r * r).astype(BF16)
            a2_ref[:, j * FF_BLK:(j + 1) * FF_BLK] = a2
            x2 = x2 + _dot(a2, wd_ref[j])
        rstd2, xh2 = _rms(x2)
        err = xh2 * gf_ref[...] - t_ref[...]
        loss_ref[...] += jnp.sum(err * err, axis=0, keepdims=True)
        dy = err * (1.0 / D_MODEL)
        dx2, dgf = _rms_bwd(dy, gf_ref[...], xh2, rstd2)
        dgf_ref[...] += dgf
        dx2b = dx2.astype(BF16)
        dx2_ref[...] = dx2b
        dh2 = jnp.zeros((tm, D_MODEL), F32)
        for j in range(N_SHARD):
            du = (_dot_nt(dx2b, wd_ref[j]) * (2.0 * relu_s[j])).astype(BF16)
            du_ref[:, j * FF_BLK:(j + 1) * FF_BLK] = du
            dh2 = dh2 + _dot_nt(du, wu_ref[j])
        dx1n, dgm = _rms_bwd(dh2, gm_ref[...], xh1, rstd1)
        dgm_ref[...] += dgm
        dx1_ref[...] = dx2 + dx1n

    vec = jax.ShapeDtypeStruct((1, D_MODEL), F32)
    return pl.pallas_call(
        body, grid=(S // tm,), name="mlp_fwd_bwd",
        in_specs=[_rows(D_MODEL, tm), _rows(D_MODEL, tm)] + [_whole_vmem()] * 4,
        out_specs=[_rows(D_MODEL, tm), _rows(D_MODEL, tm), _rows(D_FF, tm), _rows(D_FF, tm), _rows(D_MODEL, tm)]
        + [_whole_vmem()] * 3,
        out_shape=[jax.ShapeDtypeStruct((S, D_MODEL), F32), jax.ShapeDtypeStruct((S, D_MODEL), BF16),
                   jax.ShapeDtypeStruct((S, D_FF), BF16), jax.ShapeDtypeStruct((S, D_FF), BF16),
                   jax.ShapeDtypeStruct((S, D_MODEL), BF16), vec, vec, vec],
        scratch_shapes=[pltpu.VMEM((N_SHARD, tm, FF_BLK), F32)],
        compiler_params=_params(),
    )(x1, target, g_mlp, g_fin, w_up, w_down)


def _mix_bwd(dx1, w_out, mixb, ya, hf, hb, gate, g_rnn, g_attn):
    S = dx1.shape[0]

    def body(dx1_ref, w_ref, mix_ref, ya_ref, hf_ref, hb_ref, gate_ref, gr_ref, ga_ref,
             dhs_ref, dgate_ref, dya1, dya2, dw_ref, dgr_ref, dga_ref, s_ref):
        @pl.when(pl.program_id(0) == 0)
        def _():
            dw_ref[...] = jnp.zeros_like(dw_ref)
            dgr_ref[...] = jnp.zeros_like(dgr_ref)
            dga_ref[...] = jnp.zeros_like(dga_ref)

        dx1b = dx1_ref[...].astype(BF16)
        mix = mix_ref[...]
        for j in range(N_SHARD):
            dw_ref[j] += _dot_tn(mix[:, j * OUT_BLK:(j + 1) * OUT_BLK], dx1b)
        dmix = jnp.concatenate([_dot_nt(dx1b, w_ref[j]) for j in range(N_SHARD)], axis=-1)
        gg, dgg = _gelu_parts(gate_ref[...])
        hs = hf_ref[...] + hb_ref[...]
        rstd_r, xh_r = _rms(hs * gg)
        dyr, dgr = _rms_bwd(dmix[:, 0:D_RNN], gr_ref[...], xh_r, rstd_r)
        dgr_ref[...] += dgr
        rstd_a, xh_a = _rms(ya_ref[0])
        dya, dga = _rms_bwd(dmix[:, D_RNN:], ga_ref[...], xh_a, rstd_a)
        dga_ref[...] += dga
        dya1[0] = dya
        _split_classes(dya, s_ref, dya2, F32_LAYOUT)
        dhs_ref[...] = dyr * gg
        dgate_ref[...] = dyr * hs * dgg

    f512 = jax.ShapeDtypeStruct((S, 512), F32)
    vec = jax.ShapeDtypeStruct((1, 512), F32)
    return pl.pallas_call(
        body, grid=(S // TS,), name="mix_bwd",
        in_specs=[_rows(D_MODEL), _whole_vmem(), _rows(D_MODEL), _class_spec(1)] + [_rows(512)] * 3 + [_whole_vmem()] * 2,
        out_specs=[_rows(512)] * 2 + [_class_spec(1), _class_spec(F32_LAYOUT)] + [_whole_vmem()] * 3,
        out_shape=[f512, f512, _class_shape(S, 1, F32), _class_shape(S, F32_LAYOUT, F32),
                   jax.ShapeDtypeStruct((N_SHARD, OUT_BLK, D_MODEL), F32), vec, vec],
        scratch_shapes=[pltpu.VMEM((4, TS, 128), F32)],
        compiler_params=_params(),
    )(dx1, w_out, mixb, ya, hf, hb, gate, g_rnn, g_attn)


def _rnn_bwd(xr, h, dhs, conv_w, conv_b, wa, ba, wx, bx, lam, reverse, rider=None):
    S = xr.shape[0]
    nt = S // TS
    order = (lambda i: i) if reverse else (lambda i: nt - 1 - i)
    per = TS // SUB
    last = S // SUB - 1
    if reverse:
        h_halo = pl.BlockSpec((SUB, 512), lambda i: (jnp.minimum((order(i) + 1) * per, last), 0))
    else:
        h_halo = pl.BlockSpec((SUB, 512), lambda i: (jnp.maximum(order(i) * per - 1, 0), 0))
    tile = pl.BlockSpec((TS, 512), lambda i: (order(i), 0))

    def body(x_ref, xp_ref, xn_ref, h_ref, hh_ref, dh_ref, cw_ref, cb_ref, wa_ref, ba_ref, wx_ref, bx_ref, lam_ref,
             dxc_ref, dwa_ref, dwx_ref, dvec_ref, a_s, g_s, carry, edge):
        i = pl.program_id(0)
        t = order(i)

        @pl.when(i == 0)
        def _():
            carry[...] = jnp.zeros_like(carry)
            edge[...] = jnp.zeros_like(edge)
            dwa_ref[...] = jnp.zeros_like(dwa_ref)
            dwx_ref[...] = jnp.zeros_like(dwx_ref)
            dvec_ref[...] = jnp.zeros_like(dvec_ref)

        prev8 = jnp.where(t > 0, xp_ref[...], 0.0)
        next8 = jnp.where(t < nt - 1, xn_ref[...], 0.0)
        xc = _conv_fwd(x_ref[...], prev8, next8, cw_ref[...], cb_ref[...])
        xcb, r, gi, cl, a, mult, inv_mult = _lru_gates(xc, wa_ref, ba_ref[...], wx_ref, bx_ref[...], lam_ref[...])
        hv = h_ref[...]
        if reverse:
            a_s[...] = _shifted(a, edge[...], None, -1)
            edge[...] = a[TS - SUB:TS, :]
            hh = jnp.where(t < nt - 1, hh_ref[...], 0.0)
            h_prev = _shifted(hv, None, hh, 1)
        else:
            a_s[...] = _shifted(a, None, edge[...], 1)
            edge[...] = a[0:SUB, :]
            hh = jnp.where(t > 0, hh_ref[...], 0.0)
            h_prev = _shifted(hv, hh, None, -1)
        _scan_tile(a_s, dh_ref, g_s, carry, not reverse)
        g = g_s[...]
        da = g * h_prev
        gm = g * mult
        d_i = gm * xc
        dmult = g * gi * xc
        dla = da * a - dmult * (a * a) * inv_mult
        d_r = dla * cl
        dpre_r = d_r * r * (1.0 - r)
        dpre_i = d_i * gi * (1.0 - gi)
        dprb = dpre_r.astype(BF16)
        dpib = dpre_i.astype(BF16)
        dwa_ref[...] += _dot_tn(xcb, dprb)
        dwx_ref[...] += _dot_tn(xcb, dpib)
        dvec_ref[0:1, :] += jnp.sum(dpre_r, axis=0, keepdims=True)
        dvec_ref[1:2, :] += jnp.sum(dpre_i, axis=0, keepdims=True)
        dvec_ref[2:3, :] += jnp.sum(dla * r, axis=0, keepdims=True)
        dvec_ref[3:4, :] = dvec_ref[2:3, :] * (LRU_C * _sigmoid(-lam_ref[...]))
        dxc_ref[...] = gm * gi + _dot_nt(dprb, wa_ref[...]) + _dot_nt(dpib, wx_ref[...])

    sq = jax.ShapeDtypeStruct((D_RNN, D_RNN), F32)
    return _call(
        body, "rnn_bwd_rev" if reverse else "rnn_bwd_fwd", (nt,),
        _halo_specs(S, order) + [tile, h_halo, tile] + [_whole_vmem()] * 7,
        [tile, _whole_vmem(), _whole_vmem(), _whole_vmem()],
        [jax.ShapeDtypeStruct((S, 512), F32), sq, sq, jax.ShapeDtypeStruct((SUB, 512), F32)],
        [pltpu.VMEM((TS, 512), F32), pltpu.VMEM((TS, 512), F32), pltpu.VMEM((SUB, 512), F32),
         pltpu.VMEM((SUB, 512), F32)],
        (xr, xr, xr, h, h, dhs, conv_w, conv_b, wa, ba, wx, bx, lam), rider)


def _inproj_bwd(x, dx1, xr, dxc_f, dxc_b, dgate, dq3, dk3, dv3, g1, conv_w, w_in, rider=None):
    S = x.shape[0]
    tb = TS_INPROJ_BWD
    nt = S // tb
    ident = lambda i: i

    def body(x_ref, dx1_ref, xr_ref, xrp_ref, xrn_ref, cf_ref, cfp_ref, cfn_ref, cb_ref, cbp_ref, cbn_ref, dgate_ref,
             dq1, dq2, dq3_, dk1, dk2, dk3_, dv1, dv2, dv3_, g_ref, cw_ref, w_ref,
             dx_ref, dw_ref, dg_ref, dcw_ref, s_ref):
        i = pl.program_id(0)

        @pl.when(i == 0)
        def _():
            dw_ref[...] = jnp.zeros_like(dw_ref)
            dg_ref[...] = jnp.zeros_like(dg_ref)
            dcw_ref[...] = jnp.zeros_like(dcw_ref)

        first, last = i > 0, i < nt - 1
        dxc = cf_ref[...] + cb_ref[...]
        dxc_p = jnp.where(first, cfp_ref[...] + cbp_ref[...], 0.0)
        dxc_n = jnp.where(last, cfn_ref[...] + cbn_ref[...], 0.0)
        cw = cw_ref[...]
        dxr = (_shifted(dxc, dxc_p, dxc_n, 2) * cw[0:1, :] + _shifted(dxc, dxc_p, dxc_n, 1) * cw[1:2, :]
               + dxc * cw[2:3, :] + _shifted(dxc, dxc_p, dxc_n, -1) * cw[3:4, :])
        xrv = xr_ref[...]
        xr_p = jnp.where(first, xrp_ref[...], 0.0)
        xr_n = jnp.where(last, xrn_ref[...], 0.0)
        for k, off in enumerate((-2, -1, 0, 1)):
            dcw_ref[k:k + 1, :] += jnp.sum(dxc * _shifted(xrv, xr_p, xr_n, off), axis=0, keepdims=True)
        dcw_ref[4:5, :] += jnp.sum(dxc, axis=0, keepdims=True)

        def total(a, b, c_):
            return a[0] + _merge_classes(b, s_ref, F32_LAYOUT, c_)

        dproj = jnp.concatenate(
            [dxr, dgate_ref[...], total(dq1, dq2, dq3_), total(dk1, dk2, dk3_), total(dv1, dv2, dv3_)],
            axis=-1).astype(BF16)
        xv = x_ref[...]
        rstd, xh = _rms(xv)
        hb = (xh * g_ref[...]).astype(BF16)
        dh = jnp.zeros((tb, D_MODEL), F32)
        for j in range(N_SHARD):
            dpj = dproj[:, j * IN_BLK:(j + 1) * IN_BLK]
            dh = dh + _dot_nt(dpj, w_ref[j])
            dw_ref[j] += _dot_tn(hb, dpj)
        dxn, dg = _rms_bwd(dh, g_ref[...], xh, rstd)
        dg_ref[...] += dg
        dx_ref[...] = dx1_ref[...] + dxn

    halo = _halo_specs(S, ident, tb)
    return _call(
        body, "inproj_bwd", (nt,),
        [_rows(D_MODEL, tb), _rows(D_MODEL, tb)] + halo * 3 + [_rows(512, tb)]
        + [_class_spec(1, tb), _class_spec(F32_LAYOUT, tb), _class_spec(F32_LAYOUT, tb)] * 3 + [_whole_vmem()] * 3,
        [_rows(D_MODEL, tb), _whole_vmem(), _whole_vmem(), _whole_vmem()],
        [jax.ShapeDtypeStruct((S, D_MODEL), F32), jax.ShapeDtypeStruct((N_SHARD, D_MODEL, IN_BLK), F32),
         jax.ShapeDtypeStruct((1, D_MODEL), F32), jax.ShapeDtypeStruct((SUB, 512), F32)],
        [pltpu.VMEM((4, tb, 128), F32)],
        (x, dx1, xr, xr, xr, dxc_f, dxc_f, dxc_f, dxc_b, dxc_b, dxc_b, dgate, *dq3, *dk3, *dv3, g1, conv_w, w_in), rider)


def _dw_matmul(a, b, a_cols, b_cols, name):
    S = a.shape[0]
    tk = min(S, TK_DW)
    a_shared = a.shape[1] == a_cols
    b_shared = b.shape[1] == b_cols

    def body(a_ref, b_ref, o_ref):
        @pl.when(pl.program_id(1) == 0)
        def _():
            o_ref[...] = jnp.zeros_like(o_ref)
        o_ref[0] += _dot_tn(a_ref[...], b_ref[...])

    return pl.pallas_call(
        body, grid=(N_SHARD, S // tk), name=name,
        in_specs=[pl.BlockSpec((tk, a_cols), (lambda j, k: (k, 0)) if a_shared else (lambda j, k: (k, j))),
                  pl.BlockSpec((tk, b_cols), (lambda j, k: (k, 0)) if b_shared else (lambda j, k: (k, j)))],
        out_specs=pl.BlockSpec((1, a_cols, b_cols), lambda j, k: (j, 0, 0)),
        out_shape=jax.ShapeDtypeStruct((N_SHARD, a_cols, b_cols), F32),
        compiler_params=_params(2),
    )(a, b)


def _t5_bucket_np(rel):
    nb = N_BUCKETS // 2
    max_exact = nb // 2
    ret = np.where(rel > 0, nb, 0)
    n = np.abs(rel)
    nf = np.maximum(n, 1).astype(np.float32)
    large = max_exact + (np.log(nf / np.float32(max_exact)) / np.float32(math.log(MAX_DISTANCE / max_exact))
                         * np.float32(nb - max_exact)).astype(np.int32)
    large = np.minimum(large, nb - 1)
    return ret + np.where(n < max_exact, n, large)


_VARIANT_OFFSETS = (-HALF_WIN,) * 3


def _band_index():
    kk = np.arange(K_WIN)[None, :]
    ql = np.arange(Q_BLK)[:, None]
    rel = np.stack([kk - ql + off for off in _VARIANT_OFFSETS])
    band = np.abs(rel) <= HALF_WIN
    inside = np.stack([np.broadcast_to(kk >= HALF_WIN, band[0].shape), np.ones_like(band[0]),
                       np.broadcast_to(kk < K_WIN - HALF_WIN, band[0].shape)])
    return rel, band & inside


def _bucket_tables(dil):
    rel, valid = _band_index()
    bucket = _t5_bucket_np(np.clip(rel, -HALF_WIN, HALF_WIN) * dil)
    return np.where(valid, bucket, -1).astype(np.int32)


def _bias_mats(rel_bias, rider=None):
    tables = [_bucket_tables(d) for d in DILATIONS]
    used = [sorted(set(t[t >= 0].tolist())) for t in tables]

    def one_pattern(rb_ref, t_ref, o_ref, buckets):
        bk = t_ref[1]
        for h in range(N_HEADS):
            acc = jnp.full((Q_BLK, K_WIN), NEG_INF, F32)
            for b in buckets:
                acc = jnp.where(bk == b, rb_ref[b, h], acc)
            o_ref[1, h] = acc
            for var in (0, 2):
                o_ref[var, h] = jnp.where(t_ref[var] >= 0, acc, NEG_INF)

    def body(rb_ref, t1, t2, t3, o1, o2, o3):
        for i, (t_ref, o_ref) in enumerate(((t1, o1), (t2, o2), (t3, o3))):
            pl.when(pl.program_id(0) == i)(functools.partial(one_pattern, rb_ref, t_ref, o_ref, used[i]))

    shp = jax.ShapeDtypeStruct((3, N_HEADS, Q_BLK, K_WIN), F32)
    return _call(
        body, "bias_tables", (len(DILATIONS),), [pl.BlockSpec(memory_space=pltpu.SMEM)] + [_whole_vmem()] * 3,
        [_whole_vmem()] * 3, [shp] * 3, [], (rel_bias, *[jnp.asarray(t) for t in tables]), rider)


def _variant(qb, nq):
    return jnp.where(qb == 0, 0, jnp.where(qb == nq - 1, 2, 1))


def _win_start(qb):
    return pl.multiple_of(qb * Q_BLK, Q_BLK)


def _fill_padded(src_ref, pad_ref):
    L = src_ref.shape[0]
    edge = jnp.zeros((HALF_WIN, 128), pad_ref.dtype)
    pad_ref[0:HALF_WIN, :] = edge
    pad_ref[HALF_WIN:HALF_WIN + L, :] = src_ref[...]
    pad_ref[HALF_WIN + L:2 * HALF_WIN + L, :] = edge


INNER = {1: 1, 4: 1, 16: 4}


def _attn_layout(dil, L):
    inner = INNER[dil]
    n_outer = dil // inner
    nsub = min(ATTN_SUB // inner, L // Q_BLK)
    qt = nsub * Q_BLK
    grid = (4, n_outer, L // qt)
    qspec = pl.BlockSpec((inner, None, qt, 128), lambda hp, r, s: (0, r, s, hp))
    kspec = pl.BlockSpec((inner, None, L, 128), lambda hp, r, s: (0, r, 0, hp))
    bspec = pl.BlockSpec((3, 2, Q_BLK, K_WIN), lambda hp, r, s: (0, hp, 0, 0))
    kfspec = pl.BlockSpec((None, inner * L, 128), lambda hp, r, s: (r, 0, hp))
    qfspec = kfspec if inner > 1 else pl.BlockSpec((None, qt, 128), lambda hp, r, s: (r, s, hp))
    fshape = jax.ShapeDtypeStruct((n_outer, inner * L, D_ATTN), F32)
    view = lambda t: t.reshape(inner, n_outer, L, D_ATTN)

    def qrows(m, sub):
        if inner == 1:
            return (slice(sub * Q_BLK, (sub + 1) * Q_BLK), slice(None))
        first = (pl.program_id(2) * nsub + sub) * Q_BLK
        return (pl.ds(m + inner * first, Q_BLK, stride=inner), slice(None))

    def krows(m):
        if inner == 1:
            return (slice(None), slice(None))
        return (pl.ds(m, L, stride=inner), slice(None))

    return inner, nsub, grid, qspec, kspec, bspec, qfspec, kfspec, fshape, view, qrows, krows


def _head_masks():
    lane = lax.broadcasted_iota(jnp.int32, (Q_BLK, 128), 1)
    return lane < HEAD_DIM


def _attn_fwd(q, k, v, bias):
    dil, L, _ = q.shape
    nq = L // Q_BLK
    inner, nsub, grid, qspec, kspec, bspec, qfspec, kfspec, fshape, view, qrows, krows = _attn_layout(dil, L)

    def body(q_ref, k_ref, v_ref, b_ref, o_ref, l_ref, kp, vp):
        step = pl.program_id(2)

        @pl.when(step == 0)
        def _():
            for m in range(inner):
                _fill_padded(k_ref.at[m], kp.at[m])
                _fill_padded(v_ref.at[m], vp.at[m])

        h0 = _head_masks()
        for m, sub in [(m, sub) for m in range(inner) for sub in range(nsub)]:
            qb = step * nsub + sub
            st = _win_start(qb)
            var = _variant(qb, nq)
            kw = kp[m, pl.ds(st, K_WIN), :]
            vw = vp[m, pl.ds(st, K_WIN), :]
            qs = q_ref[m, sub * Q_BLK:(sub + 1) * Q_BLK, :] * ATTN_SCALE
            zq = jnp.zeros_like(qs)
            q2 = jnp.concatenate([jnp.where(h0, qs, zq), jnp.where(h0, zq, qs)], axis=0)
            s = _dot_nt(q2, kw) + b_ref[var].reshape(2 * Q_BLK, K_WIN)
            top = jnp.max(s, axis=-1, keepdims=True)
            p = jnp.exp(s - top)
            l = jnp.sum(p, axis=-1, keepdims=True)
            out = _dot(p.astype(BF16), vw) / l
            lse = top + jnp.log(l)
            o_ref[qrows(m, sub)] = jnp.where(h0, out[0:Q_BLK], out[Q_BLK:2 * Q_BLK])
            l_ref[qrows(m, sub)] = jnp.where(h0, lse[0:Q_BLK], lse[Q_BLK:2 * Q_BLK])

    return pl.pallas_call(
        body, grid=grid, name=f"attn_fwd_d{dil}",
        in_specs=[qspec, kspec, kspec, bspec], out_specs=[qfspec, qfspec], out_shape=[fshape, fshape],
        scratch_shapes=[pltpu.VMEM((inner, L + 2 * HALF_WIN, 128), BF16)] * 2,
        compiler_params=_params(3),
    )(view(q), view(k), view(v), bias)


def _attn_bwd(q, k, v, bias, do, o, lse, rider=None):
    dil, L, _ = q.shape
    nq = L // Q_BLK
    inner, nsub, grid, qspec, kspec, bspec, qfspec, kfspec, fshape, view, qrows, krows = _attn_layout(dil, L)
    nstep = grid[2]

    def body(q_ref, k_ref, v_ref, b_ref, do_ref, o_ref, l_ref, dq_ref, dk_ref, dv_ref, db_ref, db_s,
             kp, vp, dkp, dvp, carry):
        hp, step = pl.program_id(0), pl.program_id(2)
        first = (hp == 0) & (pl.program_id(1) == 0) & (step == 0)
        last = (hp == grid[0] - 1) & (pl.program_id(1) == grid[1] - 1) & (step == nstep - 1)

        @pl.when(first)
        def _():
            db_s[...] = jnp.zeros_like(db_s)

        @pl.when(step == 0)
        def _():
            for m in range(inner):
                _fill_padded(k_ref.at[m], kp.at[m])
                _fill_padded(v_ref.at[m], vp.at[m])
            carry[...] = jnp.zeros_like(carry)

        h0 = _head_masks()
        for m, sub in [(m, sub) for m in range(inner) for sub in range(nsub)]:
            if sub == 0:
                carry_k, carry_v = carry[m, 0], carry[m, 1]
            qb = step * nsub + sub
            st = _win_start(qb)
            var = _variant(qb, nq)
            kw = kp[m, pl.ds(st, K_WIN), :]
            vw = vp[m, pl.ds(st, K_WIN), :]
            qs = q_ref[m, sub * Q_BLK:(sub + 1) * Q_BLK, :] * ATTN_SCALE
            dof = do_ref[qrows(m, sub)]
            dob = dof.astype(BF16)
            prod = dof * o_ref[qrows(m, sub)]
            lsev = l_ref[qrows(m, sub)]
            zq, zd = jnp.zeros_like(qs), jnp.zeros_like(dob)
            q2 = jnp.concatenate([jnp.where(h0, qs, zq), jnp.where(h0, zq, qs)], axis=0)
            do2 = jnp.concatenate([jnp.where(h0, dob, zd), jnp.where(h0, zd, dob)], axis=0)
            lse2 = jnp.concatenate([lsev[:, 0:1], lsev[:, HEAD_DIM:HEAD_DIM + 1]], axis=0)
            dd2 = jnp.concatenate([jnp.sum(jnp.where(h0, prod, 0.0), axis=-1, keepdims=True),
                                   jnp.sum(jnp.where(h0, 0.0, prod), axis=-1, keepdims=True)], axis=0)
            s = _dot_nt(q2, kw) + b_ref[var].reshape(2 * Q_BLK, K_WIN)
            p = jnp.exp(s - lse2)
            ds = p * (_dot_nt(do2, vw) - dd2)
            db_s[var, pl.ds(hp * 2, 2)] += ds.reshape(2, Q_BLK, K_WIN)
            dsb = ds.astype(BF16)
            dv_acc = _dot_tn(p.astype(BF16), do2)
            dk_acc = _dot_tn(dsb, q2)
            dq2 = _dot(dsb, kw) * ATTN_SCALE
            dq_ref[qrows(m, sub)] = jnp.where(h0, dq2[0:Q_BLK], dq2[Q_BLK:2 * Q_BLK])
            dkp[m, pl.ds(st, Q_BLK), :] = carry_k + dk_acc[0:Q_BLK]
            dvp[m, pl.ds(st, Q_BLK), :] = carry_v + dv_acc[0:Q_BLK]
            carry_k, carry_v = dk_acc[Q_BLK:K_WIN], dv_acc[Q_BLK:K_WIN]
            if sub == nsub - 1:
                carry[m, 0] = carry_k
                carry[m, 1] = carry_v

        @pl.when(step == nstep - 1)
        def _():
            for m in range(inner):
                dkp[m, L:L + Q_BLK, :] = carry[m, 0]
                dvp[m, L:L + Q_BLK, :] = carry[m, 1]
                dk_ref[krows(m)] = dkp[m, HALF_WIN:HALF_WIN + L, :]
                dv_ref[krows(m)] = dvp[m, HALF_WIN:HALF_WIN + L, :]

        @pl.when(last)
        def _():
            db_ref[...] = db_s[...]

    dbshape = (3, N_HEADS, Q_BLK, K_WIN)
    return _call(
        body, f"attn_bwd_d{dil}", grid,
        [qspec, kspec, kspec, bspec, qfspec, qfspec, qfspec],
        [qfspec, kfspec, kfspec, _whole_vmem()],
        [fshape, fshape, fshape, jax.ShapeDtypeStruct(dbshape, F32)],
        [pltpu.VMEM(dbshape, F32)] + [pltpu.VMEM((inner, L + 2 * HALF_WIN, 128), BF16)] * 2
        + [pltpu.VMEM((inner, L + 2 * HALF_WIN, 128), F32)] * 2 + [pltpu.VMEM((inner, 2, Q_BLK, 128), F32)],
        (view(q), view(k), view(v), bias, do, o, lse), rider)


def _bucket_onehots(dil):
    m = np.zeros((3, K_WIN, N_BUCKETS), np.float32)
    for var, off in enumerate(_VARIANT_OFFSETS):
        for rel in range(-HALF_WIN, HALF_WIN + 1):
            col = (rel - off + Q_BLK - 1) % K_WIN
            m[var, col, int(_t5_bucket_np(np.asarray(rel * dil)))] = 1.0
    return jnp.asarray(m)


def _bias_grad(dbs):
    onehots = [_bucket_onehots(d) for d in DILATIONS]
    flip = jnp.asarray(np.eye(Q_BLK, dtype=np.float32)[::-1].copy())

    def body(d1, d2, d3, m1, m2, m3, flip_ref, out_ref):
        hp = lax.Precision.HIGHEST
        acc = jnp.zeros((N_HEADS, N_BUCKETS), F32)
        for d_ref, m_ref in ((d1, m1), (d2, m2), (d3, m3)):
            for var in range(3):
                rows = []
                for h in range(N_HEADS):
                    xrev = jnp.dot(flip_ref[...], d_ref[var, h], precision=hp, preferred_element_type=F32)
                    y = pltpu.roll(xrev, 0, 1, stride=1, stride_axis=0)
                    rows.append(jnp.sum(y, axis=0, keepdims=True))
                vec = jnp.concatenate(rows, axis=0)
                acc = acc + jnp.dot(vec, m_ref[var], precision=hp, preferred_element_type=F32)
        out_ref[...] = acc

    return pl.pallas_call(
        body, name="bias_grad", out_shape=jax.ShapeDtypeStruct((N_HEADS, N_BUCKETS), F32),
        compiler_params=_params(0),
    )(*dbs, *onehots, flip)


def _block_diag(w):
    eye = jnp.eye(N_RNN_BLOCKS, dtype=w.dtype)
    return jnp.einsum("ncd,nm->ncmd", w, eye).reshape(D_RNN, D_RNN).astype(BF16)


def _diag_blocks(dense):
    d = dense.reshape(N_RNN_BLOCKS, RNN_BLOCK, N_RNN_BLOCKS, RNN_BLOCK)
    return jnp.stack([d[n, :, n, :] for n in range(N_RNN_BLOCKS)])


EARLY = ("w_out", "w_up", "w_down")


def _local_step(x, target, p, shards=None):
    p = dict(p)
    first = None if shards is None else _gather_rider(["w_in"], [shards["w_in"]], shards["conv_w"])
    biases, got = _bias_mats(p["rel_bias"], first)
    if shards is not None:
        p["w_in"] = got[0]
        p["conv_w"] = jnp.transpose(got[1], (1, 0, 2)).reshape(4, D_RNN)
    lru = {}
    for dname in ("fwd", "bwd"):
        lru[dname] = (_block_diag(p["lru_wa_" + dname]), p["lru_ba_" + dname], _block_diag(p["lru_wx_" + dname]),
                      p["lru_bx_" + dname], p["lru_lam_" + dname])

    def gather(name):
        return None if shards is None else _gather_rider([name], [shards[name]])

    (xr, gate, *qkv), got = _inproj_fwd(x, p["attn_norm_g"], p["w_in"], gather("w_out"))
    p.update(zip(["w_out"], got))
    qs, ks, vs = qkv[0:3], qkv[3:6], qkv[6:9]
    hf, got = _rnn_fwd(xr, p["conv_w"], p["conv_b"], *lru["fwd"], reverse=False, rider=gather("w_up"))
    p.update(zip(["w_up"], got))
    hb, got = _rnn_fwd(xr, p["conv_w"], p["conv_b"], *lru["bwd"], reverse=True, rider=gather("w_down"))
    p.update(zip(["w_down"], got))
    outs, lses = [], []
    for q, k, v, bias in zip(qs, ks, vs, biases):
        o, l = _attn_fwd(q, k, v, bias)
        outs.append(o)
        lses.append(l)
    x1, mixb, *yl = _mix_fwd(outs, lses, hf, hb, gate, x, p["norm_rnn_g"], p["norm_attn_g"], p["w_out"])
    yas, lsts = [yl[0], yl[1], yl[1]], [yl[2], yl[3], yl[3]]
    dx1, h2b, a2b, dub, dx2b, loss_vec, dg_fin, dg_mlp = _mlp_fwd_bwd(
        x1, target, p["mlp_norm_g"], p["final_norm_g"], p["w_up"], p["w_down"])
    dhs, dgate, dya1, dya4, dw_out, dg_rnn, dg_attn = _mix_bwd(dx1, p["w_out"], mixb, yas[0], hf, hb, gate,
                                                               p["norm_rnn_g"], p["norm_attn_g"])
    dyas = [dya1, dya4, dya4]
    dw_up = _dw_matmul(h2b, dub, D_MODEL, FF_BLK, "dw_up")
    dw_down = _dw_matmul(a2b, dx2b, FF_BLK, D_MODEL, "dw_down")
    early = [dw_out, dw_up, dw_down]
    dqs, dks, dvs, dbs = [], [], [], []
    for i, (q, k, v, bias, dya, ya, lse) in enumerate(zip(qs, ks, vs, biases, dyas, yas, lsts)):
        rider = None
        if shards is not None:
            make = (lambda: _pair_exchange_rider(EARLY, early), lambda: _chip_exchange_rider(early),
                    lambda: _pair_share_rider(EARLY, early))[i]
            rider = make()
        (dq, dk, dv, db), got = _attn_bwd(q, k, v, bias, dya, ya, lse, rider)
        if shards is not None and i == 0:
            core = lax.axis_index("c").reshape(1).astype(jnp.int32)
            early = [_pair_add(core, g, o, "grad_pair_add_" + n) for n, g, o in zip(EARLY, early, got)]
        elif shards is not None and i == 1:
            early = [_chip_sum(t, "grad_chip_sum_" + n) for n, t in zip(EARLY, got)]
        elif shards is not None:
            early = got
        dqs.append(dq)
        dks.append(dk)
        dvs.append(dv)
        dbs.append(db)
    d_rel_bias = _bias_grad(dbs).T
    (dxc_f, dwa_f, dwx_f, dvec_f), _ = _rnn_bwd(xr, hf, dhs, p["conv_w"], p["conv_b"], *lru["fwd"], reverse=False)
    small = {
        "lru_wa_fwd": _diag_blocks(dwa_f), "lru_ba_fwd": dvec_f[0:1], "lru_wx_fwd": _diag_blocks(dwx_f),
        "lru_bx_fwd": dvec_f[1:2], "lru_lam_fwd": dvec_f[3:4],
        "rel_bias": d_rel_bias, "norm_rnn_g": dg_rnn, "norm_attn_g": dg_attn,
        "mlp_norm_g": dg_mlp, "final_norm_g": dg_fin,
    }
    loss_local = (0.5 / D_MODEL) * jnp.sum(loss_vec)
    rider = None
    if shards is not None:
        rider = _small_gather_rider(_pack([small[n].reshape(shp) for n, shp in SMALL if n in small]
                                          + [loss_local.reshape(1)]))
    (dxc_b, dwa_b, dwx_b, dvec_b), gathered = _rnn_bwd(xr, hb, dhs, p["conv_w"], p["conv_b"], *lru["bwd"], reverse=True,
                                                       rider=rider)
    grad_x, dw_in, dg1, dconv = _inproj_bwd(x, dx1, xr, dxc_f, dxc_b, dgate, dqs, dks, dvs,
                                            p["attn_norm_g"], p["conv_w"], p["w_in"])[0]
    last = {"lru_wa_bwd": _diag_blocks(dwa_b), "lru_ba_bwd": dvec_b[0:1], "lru_wx_bwd": _diag_blocks(dwx_b),
            "lru_bx_bwd": dvec_b[1:2], "lru_lam_bwd": dvec_b[3:4],
            "attn_norm_g": dg1, "conv_w": dconv[0:4], "conv_b": dconv[4:5]}
    if shards is None:
        big = {"w_in": dw_in, "w_out": dw_out, "w_up": dw_up, "w_down": dw_down}
        return loss_local, grad_x, {**small, **last}, None, big, {}
    return loss_local, grad_x, last, gathered[0], {"w_in": dw_in}, dict(zip(EARLY, early))


BIG = ("w_in", "w_out", "w_up", "w_down")
BIG_SHARD = {"w_in": (D_MODEL, IN_BLK), "w_out": (OUT_BLK, D_MODEL), "w_up": (D_MODEL, FF_BLK), "w_down": (FF_BLK, D_MODEL)}
N_BIG = len(BIG)
N_CHIP_PEERS = 3
ANY = pl.BlockSpec(memory_space=pl.ANY)


def _place():
    x, y, c = lax.axis_index("x"), lax.axis_index("y"), lax.axis_index("c")
    chips = [(1 - x, y), (x, 1 - y), (1 - x, 1 - y)]
    return x, y, c, chips


def _remote(src, dst, send_sem, recv_sem, dev):
    return pltpu.make_async_remote_copy(src_ref=src, dst_ref=dst, send_sem=send_sem, recv_sem=recv_sem,
                                        device_id=dev, device_id_type=MESH)


def _staged_start(srcs, bufs, sems):
    legs = [pltpu.make_async_copy(s, b, sems.at[i]) for i, (s, b) in enumerate(zip(srcs, bufs))]
    for cp in legs:
        cp.start()
    return legs


def _staged_finish(legs, bufs, dsts, sems):
    out = []
    for i, (leg, b, d) in enumerate(zip(legs, bufs, dsts)):
        leg.wait()
        cp = pltpu.make_async_copy(b, d, sems.at[i])
        cp.start()
        out.append(cp)
    return out


class _Rider:
    def __init__(self, inputs, out_shape, scratch, first, late, last):
        self.inputs, self.out_shape, self.scratch = list(inputs), list(out_shape), list(scratch)
        self.first, self.late, self.last = first, late, last


def _call(body, name, grid, in_specs, out_specs, out_shape, scratch, operands, rider=None):
    n_grid = len(grid)
    if rider is None:
        res = pl.pallas_call(body, grid=grid, name=name, in_specs=in_specs, out_specs=out_specs, out_shape=out_shape,
                             scratch_shapes=scratch, compiler_params=_params(n_grid))(*operands)
        return list(res), []
    n_in, n_out, n_scr = len(in_specs), len(out_specs), len(scratch)
    ri, ro = len(rider.inputs), len(rider.out_shape)
    nsteps = int(np.prod(grid))
    late_step = max(nsteps - 3, 1)

    def wrapped(*refs):
        a, b = n_in, n_in + ri
        c, d = b + n_out, b + n_out + ro
        e = d + n_scr
        mine = refs[:a] + refs[b:c] + refs[d:e]
        theirs = (refs[a:b], refs[c:d], refs[e:])
        step = pl.program_id(0)
        for ax in range(1, n_grid):
            step = step * grid[ax] + pl.program_id(ax)
        pl.when(step == 0)(lambda: rider.first(*theirs))
        pl.when(step == late_step)(lambda: rider.late(*theirs))
        body(*mine)
        pl.when(step == nsteps - 1)(lambda: rider.last(*theirs))

    res = pl.pallas_call(
        wrapped, grid=grid, name=name, in_specs=list(in_specs) + [ANY] * ri, out_specs=list(out_specs) + [ANY] * ro,
        out_shape=list(out_shape) + rider.out_shape, scratch_shapes=list(scratch) + rider.scratch,
        compiler_params=_params(n_grid),
    )(*operands, *rider.inputs)
    return list(res[:n_out]), list(res[n_out:])


def _run_rider(rider, name):
    ri, ro = len(rider.inputs), len(rider.out_shape)

    def body(*refs):
        parts = (refs[:ri], refs[ri:ri + ro], refs[ri + ro:])
        rider.first(*parts)
        rider.late(*parts)
        rider.last(*parts)

    return list(pl.pallas_call(
        body, name=name, in_specs=[ANY] * ri, out_specs=[ANY] * ro, out_shape=rider.out_shape, scratch_shapes=rider.scratch,
        compiler_params=pltpu.CompilerParams(has_side_effects=True, vmem_limit_bytes=VMEM_LIMIT),
    )(*rider.inputs))


def _nothing(ins, outs, scr):
    return None


def _gather_rider(names, shards, conv_w=None):
    n = len(names)
    items = n + (conv_w is not None)
    halves = [BIG_SHARD[nm][0] // 2 for nm in names]

    def parts(ins, outs, scr):
        x, y, c, chips = _place()
        return x, y, c, chips, 2 * x + y, (x, y, 1 - c), scr[:8], scr[8:]

    def piece(outs, w, chip, core_half):
        return outs[w].at[chip, pl.ds(core_half * halves[w], halves[w])]

    def ici(ins, outs, sems, w, k, chip_xy, c, me):
        return _remote(ins[w].at[pl.ds(c * halves[w], halves[w])], piece(outs, w, me, c),
                       sems[0].at[w, k], sems[1].at[w, k], (*chip_xy, c))

    def first(ins, outs, scr):
        x, y, c, chips, me, sibling, sems, bufs = parts(ins, outs, scr)
        legs = _staged_start(ins, bufs, sems[6])
        for w in range(n):
            for k, chip_xy in enumerate(chips):
                ici(ins, outs, sems, w, k, chip_xy, c, me).start()
        if conv_w is not None:
            for k, (px, py) in enumerate(chips):
                _remote(ins[n], outs[n].at[me], sems[4].at[k], sems[5].at[k], (px, py, c)).start()
        _staged_finish(legs, bufs, [o.at[me] for o in outs], sems[7])

    def late(ins, outs, scr):
        x, y, c, chips, me, sibling, sems, bufs = parts(ins, outs, scr)
        for w in range(n):
            for k, (px, py) in enumerate(chips):
                landed = piece(outs, w, 2 * px + py, c)
                _remote(landed, landed, sems[0].at[w, k], sems[1].at[w, k], (px, py, c)).wait_recv()
                _remote(landed, landed, sems[2].at[w, k], sems[3].at[w, k], sibling).start()

    def last(ins, outs, scr):
        x, y, c, chips, me, sibling, sems, bufs = parts(ins, outs, scr)
        for w in range(n):
            for k, (px, py) in enumerate(chips):
                other = piece(outs, w, 2 * px + py, 1 - c)
                _remote(other, other, sems[2].at[w, k], sems[3].at[w, k], sibling).wait_recv()
        if conv_w is not None:
            for k, (px, py) in enumerate(chips):
                got = outs[n].at[2 * px + py]
                _remote(got, got, sems[4].at[k], sems[5].at[k], (px, py, c)).wait_recv()
                _remote(ins[n], outs[n].at[me], sems[4].at[k], sems[5].at[k], (px, py, c)).wait_send()
        for i in range(items):
            pltpu.make_async_copy(bufs[i], outs[i].at[me], sems[7].at[i]).wait()
        for w in range(n):
            for k, (px, py) in enumerate(chips):
                ici(ins, outs, sems, w, k, (px, py), c, me).wait_send()
                landed = piece(outs, w, 2 * px + py, c)
                _remote(landed, landed, sems[2].at[w, k], sems[3].at[w, k], sibling).wait_send()

    out_shape = [jax.ShapeDtypeStruct((N_SHARD,) + BIG_SHARD[nm], BF16) for nm in names]
    stage = [pltpu.VMEM(BIG_SHARD[nm], BF16) for nm in names]
    inputs = list(shards)
    if conv_w is not None:
        out_shape.append(jax.ShapeDtypeStruct((N_SHARD,) + conv_w.shape, F32))
        stage.append(pltpu.VMEM(conv_w.shape, F32))
        inputs.append(conv_w)
    scratch = ([pltpu.SemaphoreType.DMA((n, N_CHIP_PEERS))] * 4 + [pltpu.SemaphoreType.DMA((N_CHIP_PEERS,))] * 2
               + [pltpu.SemaphoreType.DMA((items,))] * 2 + stage)
    return _Rider(inputs, out_shape, scratch, first, late, last)


def _pair_exchange_rider(names, grads):
    def copies(ins, outs, scr):
        x, y, c, _ = _place()
        out = []
        for w, nm in enumerate(names):
            h = BIG_SHARD[nm][0] // 2
            out.append(_remote(ins[w].at[:, pl.ds((1 - c) * h, h), :], outs[w], scr[0].at[w], scr[1].at[w], (x, y, 1 - c)))
        return out

    def first(ins, outs, scr):
        for cp in copies(ins, outs, scr):
            cp.start()

    def last(ins, outs, scr):
        for cp in copies(ins, outs, scr):
            cp.wait()

    out_shape = [jax.ShapeDtypeStruct((N_SHARD, BIG_SHARD[nm][0] // 2, BIG_SHARD[nm][1]), F32) for nm in names]
    return _Rider(grads, out_shape, [pltpu.SemaphoreType.DMA((len(names),))] * 2, first, _nothing, last)


def _pair_add(core, grad, other, name):
    _, r, cols = grad.shape
    h = r // 2
    th = min(h, 256)
    per = h // th

    def body(c_ref, g_ref, o_ref, out_ref):
        out_ref[...] = (g_ref[...] + o_ref[...]).astype(BF16)

    return pl.pallas_call(
        body, name=name,
        grid_spec=pltpu.PrefetchScalarGridSpec(
            num_scalar_prefetch=1, grid=(N_SHARD, per),
            in_specs=[pl.BlockSpec((1, th, cols), lambda j, i, c_ref: (j, c_ref[0] * per + i, 0)),
                      pl.BlockSpec((1, th, cols), lambda j, i, c_ref: (j, i, 0))],
            out_specs=pl.BlockSpec((1, th, cols), lambda j, i, c_ref: (j, i, 0))),
        out_shape=jax.ShapeDtypeStruct((N_SHARD, h, cols), BF16),
        compiler_params=_params(2),
    )(core, grad, other)


def _chip_exchange_rider(parts):
    n = len(parts)

    def sends(ins, outs, scr):
        x, y, c, chips = _place()
        me = 2 * x + y
        return [_remote(ins[w].at[2 * px + py], outs[w].at[me], scr[0].at[w, k], scr[1].at[w, k], (px, py, c))
                for w in range(n) for k, (px, py) in enumerate(chips)]

    def first(ins, outs, scr):
        x, y, c, chips = _place()
        me = 2 * x + y
        legs = _staged_start([r.at[me] for r in ins], scr[4:], scr[2])
        for cp in sends(ins, outs, scr):
            cp.start()
        _staged_finish(legs, scr[4:], [o.at[me] for o in outs], scr[3])

    def last(ins, outs, scr):
        x, y, c, chips = _place()
        me = 2 * x + y
        for w in range(n):
            for k, (px, py) in enumerate(chips):
                got = outs[w].at[2 * px + py]
                _remote(got, got, scr[0].at[w, k], scr[1].at[w, k], (px, py, c)).wait_recv()
        for cp in sends(ins, outs, scr):
            cp.wait_send()
        for w in range(n):
            pltpu.make_async_copy(scr[4 + w], outs[w].at[me], scr[3].at[w]).wait()

    out_shape = [jax.ShapeDtypeStruct(p.shape, BF16) for p in parts]
    scratch = ([pltpu.SemaphoreType.DMA((n, N_CHIP_PEERS))] * 2 + [pltpu.SemaphoreType.DMA((n,))] * 2
               + [pltpu.VMEM(p.shape[1:], BF16) for p in parts])
    return _Rider(parts, out_shape, scratch, first, _nothing, last)


def _chip_sum(parts, name):
    _, h, cols = parts.shape
    th = min(h, 256)

    def body(p_ref, out_ref):
        acc = p_ref[0].astype(F32)
        for j in range(1, N_SHARD):
            acc = acc + p_ref[j].astype(F32)
        out_ref[...] = acc

    return pl.pallas_call(
        body, name=name, grid=(h // th,),
        in_specs=[pl.BlockSpec((N_SHARD, th, cols), lambda i: (0, i, 0))],
        out_specs=pl.BlockSpec((th, cols), lambda i: (i, 0)),
        out_shape=jax.ShapeDtypeStruct((h, cols), F32),
        compiler_params=_params(),
    )(parts)


def _pair_share_rider(names, halves):
    n = len(names)
    hs = [BIG_SHARD[nm][0] // 2 for nm in names]

    def mine(outs, c):
        return [outs[w].at[pl.ds(c * hs[w], hs[w])] for w in range(n)]

    def first(ins, outs, scr):
        x, y, c, _ = _place()
        legs = _staged_start(ins, scr[4:], scr[2])
        for w, dst in enumerate(mine(outs, c)):
            _remote(ins[w], dst, scr[0].at[w], scr[1].at[w], (x, y, 1 - c)).start()
        _staged_finish(legs, scr[4:], mine(outs, c), scr[3])

    def last(ins, outs, scr):
        x, y, c, _ = _place()
        for w, (theirs, dst) in enumerate(zip(mine(outs, 1 - c), mine(outs, c))):
            _remote(theirs, theirs, scr[0].at[w], scr[1].at[w], (x, y, 1 - c)).wait_recv()
            _remote(ins[w], dst, scr[0].at[w], scr[1].at[w], (x, y, 1 - c)).wait_send()
            pltpu.make_async_copy(scr[4 + w], dst, scr[3].at[w]).wait()

    out_shape = [jax.ShapeDtypeStruct(BIG_SHARD[nm], F32) for nm in names]
    scratch = [pltpu.SemaphoreType.DMA((n,))] * 4 + [pltpu.VMEM((h, BIG_SHARD[nm][1]), F32) for nm, h in zip(names, hs)]
    return _Rider(halves, out_shape, scratch, first, _nothing, last)


N_DEV = 8


def _all_peers(x, y, c):
    return [((1 - x) if fx else x, (1 - y) if fy else y, (1 - c) if fc else c)
            for fx in (0, 1) for fy in (0, 1) for fc in (0, 1) if fx or fy or fc]


def _small_gather_rider(vec):
    def sends(ins, outs, scr):
        x, y, c, _ = _place()
        me = 4 * x + 2 * y + c
        return [_remote(ins[0], outs[0].at[me], scr[0].at[k], scr[1].at[k], dev) for k, dev in enumerate(_all_peers(x, y, c))]

    def first(ins, outs, scr):
        x, y, c, _ = _place()
        legs = _staged_start(ins, scr[4:], scr[2])
        for cp in sends(ins, outs, scr):
            cp.start()
        _staged_finish(legs, scr[4:], [outs[0].at[4 * x + 2 * y + c]], scr[3])

    def last(ins, outs, scr):
        x, y, c, _ = _place()
        for k, (px, py, pc) in enumerate(_all_peers(x, y, c)):
            got = outs[0].at[4 * px + 2 * py + pc]
            _remote(got, got, scr[0].at[k], scr[1].at[k], (px, py, pc)).wait_recv()
        for cp in sends(ins, outs, scr):
            cp.wait_send()
        pltpu.make_async_copy(scr[4], outs[0].at[4 * x + 2 * y + c], scr[3].at[0]).wait()

    scratch = ([pltpu.SemaphoreType.DMA((N_DEV - 1,))] * 2 + [pltpu.SemaphoreType.DMA((1,))] * 2
               + [pltpu.VMEM(vec.shape, F32)])
    return _Rider([vec], [jax.ShapeDtypeStruct((N_DEV,) + vec.shape, F32)], scratch, first, _nothing, last)


def _sum_devices(gathered):
    def body(g_ref, out_ref):
        acc = g_ref[0]
        for j in range(1, N_DEV):
            acc = acc + g_ref[j]
        out_ref[...] = acc

    return pl.pallas_call(body, name="sum_devices", out_shape=jax.ShapeDtypeStruct(gathered.shape[1:], F32),
                          compiler_params=_params(0))(gathered)


def _allreduce_small(vec):
    rows = vec.shape[0]

    def body(v_ref, sum_ref, gat_ref, send, recv, loc_sem):
        x, y, c, chips = _place()
        sibling = (x, y, 1 - c)
        slot = lambda px, py, pc: gat_ref.at[4 * px + 2 * py + pc]
        lc = pltpu.make_async_copy(v_ref, slot(x, y, c), loc_sem)
        lc.start()
        sends = [_remote(v_ref, slot(x, y, c), send.at[0], recv.at[0], sibling)]
        sends += [_remote(v_ref, slot(x, y, c), send.at[1 + k], recv.at[1 + k], (px, py, c))
                  for k, (px, py) in enumerate(chips)]
        for cp in sends:
            cp.start()
        for k, (px, py) in enumerate(chips):
            got = slot(px, py, c)
            _remote(got, got, send.at[1 + k], recv.at[1 + k], (px, py, c)).wait_recv()
            cp = _remote(got, got, send.at[4 + k], recv.at[4 + k], sibling)
            cp.start()
            sends.append(cp)
        got = slot(x, y, 1 - c)
        _remote(got, got, send.at[0], recv.at[0], sibling).wait_recv()
        for k, (px, py) in enumerate(chips):
            got = slot(px, py, 1 - c)
            _remote(got, got, send.at[4 + k], recv.at[4 + k], sibling).wait_recv()
        for cp in sends:
            cp.wait_send()
        lc.wait()
        acc = gat_ref[0]
        for j in range(1, N_DEV):
            acc = acc + gat_ref[j]
        sum_ref[...] = acc

    total, _ = pl.pallas_call(
        body, name="allreduce_small",
        in_specs=[_whole_vmem()], out_specs=[_whole_vmem(), _whole_vmem()],
        out_shape=[jax.ShapeDtypeStruct((rows, 128), F32), jax.ShapeDtypeStruct((N_DEV, rows, 128), F32)],
        scratch_shapes=[pltpu.SemaphoreType.DMA((N_DEV - 1,))] * 2 + [pltpu.SemaphoreType.DMA(())],
        compiler_params=pltpu.CompilerParams(has_side_effects=True, vmem_limit_bytes=VMEM_LIMIT),
    )(vec)
    return total


def _adam_math(w_ref, g_ref, m_ref, v_ref, d_ref, m2_ref, v2_ref):
    c1 = 1.0 - ADAM_B1 ** ADAM_STEP
    c2 = 1.0 - ADAM_B2 ** ADAM_STEP
    gv = g_ref[...]
    m2 = ADAM_B1 * m_ref[...] + (1.0 - ADAM_B1) * gv
    v2 = ADAM_B2 * v_ref[...] + (1.0 - ADAM_B2) * (gv * gv)
    m2_ref[...] = m2
    v2_ref[...] = v2
    d_ref[...] = -ADAM_LR * ((m2 / c1) / (jnp.sqrt(v2 / c2) + ADAM_EPS) + ADAM_WD * w_ref[...])


def _adamw_many(ws, gs, ms, vs):
    n = len(ws)

    def body(*refs):
        for i in range(n):
            _adam_math(*[refs[k * n + i] for k in range(7)])

    shapes = [jax.ShapeDtypeStruct(w.shape, F32) for w in ws]
    res = pl.pallas_call(body, name="adamw_small", out_shape=shapes * 3, compiler_params=_params(0))(*ws, *gs, *ms, *vs)
    return res[:n], res[n:2 * n], res[2 * n:]


def _adamw(w, g, m, v, name):
    rows, cols = w.shape
    tr = 256 if rows % 256 == 0 else rows

    def body(w_ref, g_ref, m_ref, v_ref, d_ref, m2_ref, v2_ref):
        _adam_math(w_ref, g_ref, m_ref, v_ref, d_ref, m2_ref, v2_ref)

    spec = pl.BlockSpec((tr, cols), lambda i: (i, 0))
    shp = jax.ShapeDtypeStruct((rows, cols), F32)
    return pl.pallas_call(
        body, name=name, grid=(rows // tr,), in_specs=[spec] * 4, out_specs=[spec] * 3, out_shape=[shp] * 3,
        compiler_params=_params(),
    )(w, g, m, v)


SMALL = (
    ("attn_norm_g", (1, 1024)), ("conv_w", (1, 4, 512)), ("conv_b", (1, 512)),
    ("lru_wa_fwd", (1, 8, 64, 64)), ("lru_ba_fwd", (1, 512)), ("lru_wx_fwd", (1, 8, 64, 64)), ("lru_bx_fwd", (1, 512)),
    ("lru_lam_fwd", (1, 512)),
    ("lru_wa_bwd", (1, 8, 64, 64)), ("lru_ba_bwd", (1, 512)), ("lru_wx_bwd", (1, 8, 64, 64)), ("lru_bx_bwd", (1, 512)),
    ("lru_lam_bwd", (1, 512)),
    ("rel_bias", (32, 8)), ("norm_rnn_g", (1, 512)), ("norm_attn_g", (1, 512)), ("mlp_norm_g", (1, 1024)),
    ("final_norm_g", (1024,)),
)
PACK_ROW = 8 * 128


def _pack(parts):
    flat = jnp.concatenate([p.reshape(-1) for p in parts])
    pad = (-flat.shape[0]) % PACK_ROW
    return jnp.pad(flat, (0, pad)).reshape(-1, 128)


def _unpack(packed, shapes):
    flat = packed.reshape(-1)
    out, off = [], 0
    for shp in shapes:
        n = int(np.prod(shp))
        out.append(flat[off:off + n].reshape(shp))
        off += n
    return out


WEIGHT_ORDER = ("attn_norm_g", "w_in", "conv_w", "conv_b", "lru_wa_fwd", "lru_ba_fwd", "lru_wx_fwd", "lru_bx_fwd",
                "lru_lam_fwd", "lru_wa_bwd", "lru_ba_bwd", "lru_wx_bwd", "lru_bx_bwd", "lru_lam_bwd", "rel_bias",
                "norm_rnn_g", "norm_attn_g", "w_out", "mlp_norm_g", "w_up", "w_down", "final_norm_g")


def kernel(x, attn_norm_g, w_in, conv_w, conv_b, lru_wa_fwd, lru_ba_fwd, lru_wx_fwd, lru_bx_fwd, lru_lam_fwd, lru_wa_bwd, lru_ba_bwd, lru_wx_bwd, lru_bx_bwd, lru_lam_bwd, rel_bias, norm_rnn_g, norm_attn_g, w_out, mlp_norm_g, w_up, w_down, final_norm_g, loss_target, m_attn_norm_g, m_w_in, m_conv_w, m_conv_b, m_lru_wa_fwd, m_lru_ba_fwd, m_lru_wx_fwd, m_lru_bx_fwd, m_lru_lam_fwd, m_lru_wa_bwd, m_lru_ba_bwd, m_lru_wx_bwd, m_lru_bx_bwd, m_lru_lam_bwd, m_rel_bias, m_norm_rnn_g, m_norm_attn_g, m_w_out, m_mlp_norm_g, m_w_up, m_w_down, m_final_norm_g, v_attn_norm_g, v_w_in, v_conv_w, v_conv_b, v_lru_wa_fwd, v_lru_ba_fwd, v_lru_wx_fwd, v_lru_bx_fwd, v_lru_lam_fwd, v_lru_wa_bwd, v_lru_ba_bwd, v_lru_wx_bwd, v_lru_bx_bwd, v_lru_lam_bwd, v_rel_bias, v_norm_rnn_g, v_norm_attn_g, v_w_out, v_mlp_norm_g, v_w_up, v_w_down, v_final_norm_g):
    given = dict(locals())
    w = {n: given[n] for n in WEIGHT_ORDER}
    m = {n: given["m_" + n] for n in WEIGHT_ORDER}
    v = {n: given["v_" + n] for n in WEIGHT_ORDER}

    chip = lax.axis_index("x") * 2 + lax.axis_index("y")
    core = lax.axis_index("c")

    shards = {n: w[n][0].astype(BF16) for n in BIG}
    shards["conv_w"] = w["conv_w"][0]
    p = {n: (t[0] if t.ndim >= 3 else t) for n, t in w.items() if n not in BIG and n != "conv_w"}
    p["final_norm_g"] = w["final_norm_g"].reshape(1, D_MODEL)

    _, grad_x, small, gathered, big, reduced = _local_step(x[0], loss_target[0], p, shards)

    late = tuple(big)
    grads = [big[n] for n in late]
    others = _run_rider(_pair_exchange_rider(late, grads), "grad_pair_exchange")
    core_arr = core.reshape(1).astype(jnp.int32)
    parts = [_pair_add(core_arr, g, o, "grad_pair_add_" + n) for n, g, o in zip(late, grads, others)]
    landed = _run_rider(_chip_exchange_rider(parts), "grad_chip_exchange")
    halves = [_chip_sum(t, "grad_chip_sum_" + n) for n, t in zip(late, landed)]
    reduced.update(zip(late, _run_rider(_pair_share_rider(late, halves), "grad_pair_share")))

    early_small = [(n, shp) for n, shp in SMALL if n not in small]
    late_small = [(n, shp) for n, shp in SMALL if n in small]
    *early_g, loss = _unpack(_sum_devices(gathered), [shp for _, shp in early_small] + [(1,)])
    late_g = _unpack(_allreduce_small(_pack([small[n].reshape(shp) for n, shp in late_small])),
                     [shp for _, shp in late_small])
    g = dict(zip([n for n, _ in early_small + late_small], early_g + late_g))
    g["conv_w"] = lax.dynamic_slice_in_dim(g["conv_w"], chip * (D_RNN // N_SHARD), D_RNN // N_SHARD, axis=2)
    for n in BIG:
        g[n] = reduced[n][None]

    delta, new_m, new_v = {}, {}, {}
    for n in BIG:
        d2, m2, v2 = _adamw(w[n][0], reduced[n], m[n][0], v[n][0], "adamw_" + n)
        delta[n], new_m[n], new_v[n] = d2[None], m2[None], v2[None]
    names = [n for n, _ in SMALL]
    for dst, src in zip((delta, new_m, new_v), _adamw_many(*[[t[n] for n in names] for t in (w, g, m, v)])):
        dst.update(dict(zip(names, src)))

    return (loss.reshape(()), grad_x[None], *[g[n] for n in WEIGHT_ORDER], *[delta[n] for n in WEIGHT_ORDER],
            *[new_m[n] for n in WEIGHT_ORDER], *[new_v[n] for n in WEIGHT_ORDER])
```

```python
import functools
import math

import numpy as np
import jax
import jax.numpy as jnp
from jax import lax
from jax.experimental import pallas as pl
from jax.experimental.pallas import tpu as pltpu

F32 = jnp.float32
BF16 = jnp.bfloat16

D_MODEL = 1024
D_RNN = 512
D_ATTN = 512
N_HEADS = 8
HEAD_DIM = 64
N_RNN_BLOCKS = 8
RNN_BLOCK = 64
D_IN = 2 * D_RNN + 3 * D_ATTN
D_FF = 4 * D_MODEL
N_SHARD = 4
IN_BLK = D_IN // N_SHARD
OUT_BLK = D_MODEL // N_SHARD
FF_BLK = D_FF // N_SHARD
EPS = 1e-6
NEG_INF = -1e30
LRU_C = 8.0
DILATIONS = (1, 4, 16)
F32_LAYOUT = 4
HALF_WIN = 64
Q_BLK = 128
K_WIN = 256
N_BUCKETS = 32
MAX_DISTANCE = 1024
ATTN_SCALE = HEAD_DIM ** -0.5

ADAM_LR = 0.001
ADAM_B1 = 0.9
ADAM_B2 = 0.999
ADAM_EPS = 1e-08
ADAM_WD = 0.01
ADAM_STEP = 10

TS = 512
TS_MLP = 256
TS_INPROJ_BWD = 512
ATTN_SUB = 16
TK_DW = 4096
SCAN_UNROLL = 8
SUB = 8
VMEM_LIMIT = 56 * 1024 * 1024
GELU_C0 = math.sqrt(2.0 / math.pi)
GELU_C1 = 0.044715

MESH = pl.DeviceIdType.MESH


def _params(n_grid=1):
    return pltpu.CompilerParams(vmem_limit_bytes=VMEM_LIMIT, dimension_semantics=("arbitrary",) * n_grid)


def _whole_vmem():
    return pl.BlockSpec(memory_space=pltpu.VMEM)


def _rows(width, tile=TS):
    return pl.BlockSpec((tile, width), lambda i: (i, 0))


def _sigmoid(z):
    return 0.5 * jnp.tanh(0.5 * z) + 0.5


def _log1p(u):
    w = 1.0 + u
    return jnp.where(w == 1.0, u, jnp.log(w) * (u / (w - 1.0)))


def _softplus(z):
    return jnp.maximum(z, 0.0) + _log1p(jnp.exp(-jnp.abs(z)))


def _gelu_parts(g):
    inner = GELU_C0 * (g + GELU_C1 * g * g * g)
    t = jnp.tanh(inner)
    val = 0.5 * g * (1.0 + t)
    dinner = GELU_C0 * (1.0 + 3.0 * GELU_C1 * g * g)
    grad = 0.5 * (1.0 + t) + 0.5 * g * (1.0 - t * t) * dinner
    return val, grad


def _rms(x):
    rstd = lax.rsqrt(jnp.mean(x * x, axis=-1, keepdims=True) + EPS)
    return rstd, x * rstd


def _rms_bwd(dy, g, xhat, rstd):
    dxh = dy * g
    dx = rstd * (dxh - xhat * jnp.mean(dxh * xhat, axis=-1, keepdims=True))
    dg = jnp.sum(dy * xhat, axis=0, keepdims=True)
    return dx, dg


def _dot(a, b):
    return jnp.dot(a, b, preferred_element_type=F32)


def _dot_nt(a, b):
    return lax.dot_general(a, b, (((1,), (1,)), ((), ())), preferred_element_type=F32)


def _dot_tn(a, b):
    return lax.dot_general(a, b, (((0,), (0,)), ((), ())), preferred_element_type=F32)


def _shifted(tile, prev8, next8, k):
    n = tile.shape[0]
    row = lax.broadcasted_iota(jnp.int32, tile.shape, 0)
    if k == 0:
        return tile
    if k < 0:
        r = pltpu.roll(tile, -k, 0)
        for j in range(-k):
            r = jnp.where(row == j, prev8[SUB + j + k:SUB + j + k + 1, :], r)
        return r
    r = pltpu.roll(tile, n - k, 0)
    for j in range(k):
        r = jnp.where(row == n - k + j, next8[j:j + 1, :], r)
    return r


def _to_lane_blocks(val, s_ref):
    for j in range(val.shape[1] // 128):
        s_ref[j] = val[:, j * 128:(j + 1) * 128]


def _from_lane_blocks(s_ref):
    return jnp.concatenate([s_ref[j] for j in range(s_ref.shape[0])], axis=-1)


def _class_rows(s_ref, r, dil):
    n = s_ref.shape[1] // dil
    return jnp.concatenate([s_ref[j, pl.ds(r, n, stride=dil), :] for j in range(s_ref.shape[0])], axis=-1)


def _split_classes(val, s_ref, out_ref, dil):
    _to_lane_blocks(val, s_ref)
    for r in range(dil):
        out_ref[r] = _class_rows(s_ref, r, dil).astype(out_ref.dtype)


def _merge_classes(in_ref, s_ref, dil, also_ref=None):
    n = s_ref.shape[1] // dil
    for r in range(dil):
        v = in_ref[r].astype(F32)
        if also_ref is not None:
            v = v + also_ref[r].astype(F32)
        for j in range(s_ref.shape[0]):
            s_ref[j, pl.ds(r, n, stride=dil), :] = v[:, j * 128:(j + 1) * 128]
    return _from_lane_blocks(s_ref)


def _class_spec(dil, tile=TS):
    return pl.BlockSpec((dil, tile // dil, 512), lambda i: (0, i, 0))


def _class_shape(S, dil, dtype):
    return jax.ShapeDtypeStruct((dil, S // dil, 512), dtype)


def _scan_tile(a_ref, b_ref, h_ref, carry_ref, reverse):
    n = a_ref.shape[0]
    width = a_ref.shape[1]
    groups = n // SUB
    row = lax.broadcasted_iota(jnp.int32, (SUB, width), 0)

    def group_scan(g):
        r0 = pl.multiple_of(g * SUB, SUB)
        a = a_ref[pl.ds(r0, SUB), :]
        b = b_ref[pl.ds(r0, SUB), :]
        for s in (1, 2, 4):
            if reverse:
                a_sh = pltpu.roll(a, SUB - s, 0)
                b_sh = pltpu.roll(b, SUB - s, 0)
                m = row < SUB - s
            else:
                a_sh = pltpu.roll(a, s, 0)
                b_sh = pltpu.roll(b, s, 0)
                m = row >= s
            b = jnp.where(m, a * b_sh + b, b)
            a = jnp.where(m, a * a_sh, a)
        return r0, a, b

    def step(i, carry):
        first = i * SCAN_UNROLL
        order = [(groups - 1 - (first + u)) if reverse else (first + u) for u in range(SCAN_UNROLL)]
        scans = [group_scan(g) for g in order]
        for r0, a, b in scans:
            h = b + a * carry
            h_ref[pl.ds(r0, SUB), :] = h
            edge = h[0:1, :] if reverse else h[SUB - 1:SUB, :]
            carry = jnp.broadcast_to(edge, (SUB, width))
        return carry

    carry_ref[...] = lax.fori_loop(0, groups // SCAN_UNROLL, step, carry_ref[...])


def _conv_fwd(xr, prev8, next8, cw, cb):
    y = cb + _shifted(xr, prev8, next8, -2) * cw[0:1, :]
    y = y + _shifted(xr, prev8, next8, -1) * cw[1:2, :]
    y = y + xr * cw[2:3, :]
    y = y + _shifted(xr, prev8, next8, 1) * cw[3:4, :]
    return y


def _lru_gates(xc, wa_ref, ba, wx_ref, bx, lam):
    xcb = xc.astype(BF16)
    r = _sigmoid(_dot(xcb, wa_ref[...]) + ba)
    i = _sigmoid(_dot(xcb, wx_ref[...]) + bx)
    cl = -LRU_C * _softplus(-lam)
    la = cl * r
    a = jnp.exp(la)
    m2 = -jnp.tanh(la) * (a * a + 1.0)
    inv = jnp.where(m2 > 0.0, lax.rsqrt(m2), 0.0)
    mult = m2 * inv
    return xcb, r, i, cl, a, mult, inv


def _inproj_fwd(x, g1, w_in, rider=None):
    S = x.shape[0]

    def body(x_ref, g_ref, w_ref, xr_ref, gate_ref, *rest):
        qkv_refs, s_ref, s4_ref = rest[:9], rest[9], rest[10]
        _, xh = _rms(x_ref[...])
        h = (xh * g_ref[...]).astype(BF16)
        proj = jnp.concatenate([_dot(h, w_ref[j]) for j in range(N_SHARD)], axis=-1)
        xr_ref[...] = proj[:, 0:512]
        gate_ref[...] = proj[:, 512:1024]
        for t in range(3):
            val = proj[:, 1024 + 512 * t:1536 + 512 * t]
            d1_ref, d4_ref, d16_ref = qkv_refs[3 * t:3 * t + 3]
            d1_ref[0] = val.astype(BF16)
            _to_lane_blocks(val, s_ref)
            for r4 in range(4):
                c4 = _class_rows(s_ref, r4, 4)
                d4_ref[r4] = c4.astype(BF16)
                _to_lane_blocks(c4, s4_ref.at[r4])
            for r4 in range(4):
                for m in range(4):
                    d16_ref[r4 + 4 * m] = _class_rows(s4_ref.at[r4], m, 4).astype(BF16)

    f = jax.ShapeDtypeStruct((S, 512), F32)
    return _call(
        body, "inproj_fwd", (S // TS,),
        [_rows(D_MODEL), _whole_vmem(), _whole_vmem()],
        [_rows(512)] * 2 + [_class_spec(d) for d in DILATIONS] * 3,
        [f, f] + [_class_shape(S, d, BF16) for d in DILATIONS] * 3,
        [pltpu.VMEM((4, TS, 128), F32), pltpu.VMEM((4, 4, TS // 4, 128), F32)], (x, g1, w_in), rider)


def _halo_specs(S, order, tile=TS):
    per = tile // SUB
    last = S // SUB - 1
    return [
        pl.BlockSpec((tile, 512), lambda i: (order(i), 0)),
        pl.BlockSpec((SUB, 512), lambda i: (jnp.maximum(order(i) * per - 1, 0), 0)),
        pl.BlockSpec((SUB, 512), lambda i: (jnp.minimum((order(i) + 1) * per, last), 0)),
    ]


def _rnn_fwd(xr, conv_w, conv_b, wa, ba, wx, bx, lam, reverse, rider=None):
    S = xr.shape[0]
    nt = S // TS
    order = (lambda i: nt - 1 - i) if reverse else (lambda i: i)

    def body(x_ref, xp_ref, xn_ref, cw_ref, cb_ref, wa_ref, ba_ref, wx_ref, bx_ref, lam_ref, h_ref, a_s, b_s, carry):
        i = pl.program_id(0)
        t = order(i)

        @pl.when(i == 0)
        def _():
            carry[...] = jnp.zeros_like(carry)

        prev8 = jnp.where(t > 0, xp_ref[...], 0.0)
        next8 = jnp.where(t < nt - 1, xn_ref[...], 0.0)
        xc = _conv_fwd(x_ref[...], prev8, next8, cw_ref[...], cb_ref[...])
        _, _, gi, _, a, mult, _ = _lru_gates(xc, wa_ref, ba_ref[...], wx_ref, bx_ref[...], lam_ref[...])
        a_s[...] = a
        b_s[...] = mult * (gi * xc)
        _scan_tile(a_s, b_s, h_ref, carry, reverse)

    (h,), carried = _call(
        body, "rnn_fwd_rev" if reverse else "rnn_fwd_fwd", (nt,),
        _halo_specs(S, order) + [_whole_vmem()] * 7,
        [pl.BlockSpec((TS, 512), lambda i: (order(i), 0))],
        [jax.ShapeDtypeStruct((S, 512), F32)],
        [pltpu.VMEM((TS, 512), F32), pltpu.VMEM((TS, 512), F32), pltpu.VMEM((SUB, 512), F32)],
        (xr, xr, xr, conv_w, conv_b, wa, ba, wx, bx, lam), rider)
    return h, carried


def _mix_fwd(o3, l3, hf, hb, gate, x, g_rnn, g_attn, w_out):
    S = x.shape[0]

    def body(o1, o2, o3_, l1, l2, l3_, hf_ref, hb_ref, gate_ref, x_ref, gr_ref, ga_ref, w_ref,
             x1_ref, mix_ref, ya1, ya2, ls1, ls2, s_ref):
        la, lb, lc = l1[0], _merge_classes(l2, s_ref, F32_LAYOUT), _merge_classes(l3_, s_ref, F32_LAYOUT)
        m = jnp.maximum(jnp.maximum(la, lb), lc)
        ea, eb, ec = jnp.exp(la - m), jnp.exp(lb - m), jnp.exp(lc - m)
        den = ea + eb + ec
        lse = m + jnp.log(den)
        ya = (ea * o1[0] + eb * _merge_classes(o2, s_ref, F32_LAYOUT) + ec * _merge_classes(o3_, s_ref, F32_LAYOUT)) / den
        ya1[0] = ya
        ls1[0] = lse
        _split_classes(ya, s_ref, ya2, F32_LAYOUT)
        _split_classes(lse, s_ref, ls2, F32_LAYOUT)
        gg, _ = _gelu_parts(gate_ref[...])
        yr = (hf_ref[...] + hb_ref[...]) * gg
        _, xh_r = _rms(yr)
        _, xh_a = _rms(ya)
        mix = jnp.concatenate([xh_r * gr_ref[...], xh_a * ga_ref[...]], axis=-1).astype(BF16)
        mix_ref[...] = mix
        acc = x_ref[...]
        for j in range(N_SHARD):
            acc = acc + _dot(mix[:, j * OUT_BLK:(j + 1) * OUT_BLK], w_ref[j])
        x1_ref[...] = acc

    one, four = _class_spec(1), _class_spec(F32_LAYOUT)
    return pl.pallas_call(
        body, grid=(S // TS,), name="mix_fwd",
        in_specs=[one, four, four] * 2 + [_rows(512)] * 3 + [_rows(D_MODEL)] + [_whole_vmem()] * 3,
        out_specs=[_rows(D_MODEL), _rows(D_MODEL)] + [one, four] * 2,
        out_shape=[jax.ShapeDtypeStruct((S, D_MODEL), F32), jax.ShapeDtypeStruct((S, D_MODEL), BF16)]
        + [_class_shape(S, 1, F32), _class_shape(S, F32_LAYOUT, F32)] * 2,
        scratch_shapes=[pltpu.VMEM((4, TS, 128), F32)],
        compiler_params=_params(),
    )(*o3, *l3, hf, hb, gate, x, g_rnn, g_attn, w_out)


def _mlp_fwd_bwd(x1, target, g_mlp, g_fin, w_up, w_down):
    S = x1.shape[0]
    tm = TS_MLP

    def body(x1_ref, t_ref, gm_ref, gf_ref, wu_ref, wd_ref,
             dx1_ref, h2_ref, a2_ref, du_ref, dx2_ref, loss_ref, dgf_ref, dgm_ref, relu_s):
        @pl.when(pl.program_id(0) == 0)
        def _():
            loss_ref[...] = jnp.zeros_like(loss_ref)
            dgf_ref[...] = jnp.zeros_like(dgf_ref)
            dgm_ref[...] = jnp.zeros_like(dgm_ref)

        x1v = x1_ref[...]
        rstd1, xh1 = _rms(x1v)
        h2 = (xh1 * gm_ref[...]).astype(BF16)
        h2_ref[...] = h2
        x2 = x1v
        for j in range(N_SHARD):
            r = jnp.maximum(_dot(h2, wu_ref[j]), 0.0)
            relu_s[j] = r
            a2 = (r * r).astype(BF16)
            a2_ref[:, j * FF_BLK:(j + 1) * FF_BLK] = a2
            x2 = x2 + _dot(a2, wd_ref[j])
        rstd2, xh2 = _rms(x2)
        err = xh2 * gf_ref[...] - t_ref[...]
        loss_ref[...] += jnp.sum(err * err, axis=0, keepdims=True)
        dy = err * (1.0 / D_MODEL)
        dx2, dgf = _rms_bwd(dy, gf_ref[...], xh2, rstd2)
        dgf_ref[...] += dgf
        dx2b = dx2.astype(BF16)
        dx2_ref[...] = dx2b
        dh2 = jnp.zeros((tm, D_MODEL), F32)
        for j in range(N_SHARD):
            du = (_dot_nt(dx2b, wd_ref[j]) * (2.0 * relu_s[j])).astype(BF16)
            du_ref[:, j * FF_BLK:(j + 1) * FF_BLK] = du
            dh2 = dh2 + _dot_nt(du, wu_ref[j])
        dx1n, dgm = _rms_bwd(dh2, gm_ref[...], xh1, rstd1)
        dgm_ref[...] += dgm
        dx1_ref[...] = dx2 + dx1n

    vec = jax.ShapeDtypeStruct((1, D_MODEL), F32)
    return pl.pallas_call(
        body, grid=(S // tm,), name="mlp_fwd_bwd",
        in_specs=[_rows(D_MODEL, tm), _rows(D_MODEL, tm)] + [_whole_vmem()] * 4,
        out_specs=[_rows(D_MODEL, tm), _rows(D_MODEL, tm), _rows(D_FF, tm), _rows(D_FF, tm), _rows(D_MODEL, tm)]
        + [_whole_vmem()] * 3,
        out_shape=[jax.ShapeDtypeStruct((S, D_MODEL), F32), jax.ShapeDtypeStruct((S, D_MODEL), BF16),
                   jax.ShapeDtypeStruct((S, D_FF), BF16), jax.ShapeDtypeStruct((S, D_FF), BF16),
                   jax.ShapeDtypeStruct((S, D_MODEL), BF16), vec, vec, vec],
        scratch_shapes=[pltpu.VMEM((N_SHARD, tm, FF_BLK), F32)],
        compiler_params=_params(),
    )(x1, target, g_mlp, g_fin, w_up, w_down)


def _mix_bwd(dx1, w_out, mixb, ya, hf, hb, gate, g_rnn, g_attn):
    S = dx1.shape[0]

    def body(dx1_ref, w_ref, mix_ref, ya_ref, hf_ref, hb_ref, gate_ref, gr_ref, ga_ref,
             dhs_ref, dgate_ref, dya1, dya2, dw_ref, dgr_ref, dga_ref, s_ref):
        @pl.when(pl.program_id(0) == 0)
        def _():
            dw_ref[...] = jnp.zeros_like(dw_ref)
            dgr_ref[...] = jnp.zeros_like(dgr_ref)
            dga_ref[...] = jnp.zeros_like(dga_ref)

        dx1b = dx1_ref[...].astype(BF16)
        mix = mix_ref[...]
        for j in range(N_SHARD):
            dw_ref[j] += _dot_tn(mix[:, j * OUT_BLK:(j + 1) * OUT_BLK], dx1b)
        dmix = jnp.concatenate([_dot_nt(dx1b, w_ref[j]) for j in range(N_SHARD)], axis=-1)
        gg, dgg = _gelu_parts(gate_ref[...])
        hs = hf_ref[...] + hb_ref[...]
        rstd_r, xh_r = _rms(hs * gg)
        dyr, dgr = _rms_bwd(dmix[:, 0:D_RNN], gr_ref[...], xh_r, rstd_r)
        dgr_ref[...] += dgr
        rstd_a, xh_a = _rms(ya_ref[0])
        dya, dga = _rms_bwd(dmix[:, D_RNN:], ga_ref[...], xh_a, rstd_a)
        dga_ref[...] += dga
        dya1[0] = dya
        _split_classes(dya, s_ref, dya2, F32_LAYOUT)
        dhs_ref[...] = dyr * gg
        dgate_ref[...] = dyr * hs * dgg

    f512 = jax.ShapeDtypeStruct((S, 512), F32)
    vec = jax.ShapeDtypeStruct((1, 512), F32)
    return pl.pallas_call(
        body, grid=(S // TS,), name="mix_bwd",
        in_specs=[_rows(D_MODEL), _whole_vmem(), _rows(D_MODEL), _class_spec(1)] + [_rows(512)] * 3 + [_whole_vmem()] * 2,
        out_specs=[_rows(512)] * 2 + [_class_spec(1), _class_spec(F32_LAYOUT)] + [_whole_vmem()] * 3,
        out_shape=[f512, f512, _class_shape(S, 1, F32), _class_shape(S, F32_LAYOUT, F32),
                   jax.ShapeDtypeStruct((N_SHARD, OUT_BLK, D_MODEL), F32), vec, vec],
        scratch_shapes=[pltpu.VMEM((4, TS, 128), F32)],
        compiler_params=_params(),
    )(dx1, w_out, mixb, ya, hf, hb, gate, g_rnn, g_attn)


def _rnn_bwd(xr, h, dhs, conv_w, conv_b, wa, ba, wx, bx, lam, reverse, rider=None):
    S = xr.shape[0]
    nt = S // TS
    order = (lambda i: i) if reverse else (lambda i: nt - 1 - i)
    per = TS // SUB
    last = S // SUB - 1
    if reverse:
        h_halo = pl.BlockSpec((SUB, 512), lambda i: (jnp.minimum((order(i) + 1) * per, last), 0))
    else:
        h_halo = pl.BlockSpec((SUB, 512), lambda i: (jnp.maximum(order(i) * per - 1, 0), 0))
    tile = pl.BlockSpec((TS, 512), lambda i: (order(i), 0))

    def body(x_ref, xp_ref, xn_ref, h_ref, hh_ref, dh_ref, cw_ref, cb_ref, wa_ref, ba_ref, wx_ref, bx_ref, lam_ref,
             dxc_ref, dwa_ref, dwx_ref, dvec_ref, a_s, g_s, carry, edge):
        i = pl.program_id(0)
        t = order(i)

        @pl.when(i == 0)
        def _():
            carry[...] = jnp.zeros_like(carry)
            edge[...] = jnp.zeros_like(edge)
            dwa_ref[...] = jnp.zeros_like(dwa_ref)
            dwx_ref[...] = jnp.zeros_like(dwx_ref)
            dvec_ref[...] = jnp.zeros_like(dvec_ref)

        prev8 = jnp.where(t > 0, xp_ref[...], 0.0)
        next8 = jnp.where(t < nt - 1, xn_ref[...], 0.0)
        xc = _conv_fwd(x_ref[...], prev8, next8, cw_ref[...], cb_ref[...])
        xcb, r, gi, cl, a, mult, inv_mult = _lru_gates(xc, wa_ref, ba_ref[...], wx_ref, bx_ref[...], lam_ref[...])
        hv = h_ref[...]
        if reverse:
            a_s[...] = _shifted(a, edge[...], None, -1)
            edge[...] = a[TS - SUB:TS, :]
            hh = jnp.where(t < nt - 1, hh_ref[...], 0.0)
            h_prev = _shifted(hv, None, hh, 1)
        else:
            a_s[...] = _shifted(a, None, edge[...], 1)
            edge[...] = a[0:SUB, :]
            hh = jnp.where(t > 0, hh_ref[...], 0.0)
            h_prev = _shifted(hv, hh, None, -1)
        _scan_tile(a_s, dh_ref, g_s, carry, not reverse)
        g = g_s[...]
        da = g * h_prev
        gm = g * mult
        d_i = gm * xc
        dmult = g * gi * xc
        dla = da * a - dmult * (a * a) * inv_mult
        d_r = dla * cl
        dpre_r = d_r * r * (1.0 - r)
        dpre_i = d_i * gi * (1.0 - gi)
        dprb = dpre_r.astype(BF16)
        dpib = dpre_i.astype(BF16)
        dwa_ref[...] += _dot_tn(xcb, dprb)
        dwx_ref[...] += _dot_tn(xcb, dpib)
        dvec_ref[0:1, :] += jnp.sum(dpre_r, axis=0, keepdims=True)
        dvec_ref[1:2, :] += jnp.sum(dpre_i, axis=0, keepdims=True)
        dvec_ref[2:3, :] += jnp.sum(dla * r, axis=0, keepdims=True)
        dvec_ref[3:4, :] = dvec_ref[2:3, :] * (LRU_C * _sigmoid(-lam_ref[...]))
        dxc_ref[...] = gm * gi + _dot_nt(dprb, wa_ref[...]) + _dot_nt(dpib, wx_ref[...])

    sq = jax.ShapeDtypeStruct((D_RNN, D_RNN), F32)
    return _call(
        body, "rnn_bwd_rev" if reverse else "rnn_bwd_fwd", (nt,),
        _halo_specs(S, order) + [tile, h_halo, tile] + [_whole_vmem()] * 7,
        [tile, _whole_vmem(), _whole_vmem(), _whole_vmem()],
        [jax.ShapeDtypeStruct((S, 512), F32), sq, sq, jax.ShapeDtypeStruct((SUB, 512), F32)],
        [pltpu.VMEM((TS, 512), F32), pltpu.VMEM((TS, 512), F32), pltpu.VMEM((SUB, 512), F32),
         pltpu.VMEM((SUB, 512), F32)],
        (xr, xr, xr, h, h, dhs, conv_w, conv_b, wa, ba, wx, bx, lam), rider)


def _inproj_bwd(x, dx1, xr, dxc_f, dxc_b, dgate, dq3, dk3, dv3, g1, conv_w, w_in, rider=None):
    S = x.shape[0]
    tb = TS_INPROJ_BWD
    nt = S // tb
    ident = lambda i: i

    def body(x_ref, dx1_ref, xr_ref, xrp_ref, xrn_ref, cf_ref, cfp_ref, cfn_ref, cb_ref, cbp_ref, cbn_ref, dgate_ref,
             dq1, dq2, dq3_, dk1, dk2, dk3_, dv1, dv2, dv3_, g_ref, cw_ref, w_ref,
             dx_ref, dw_ref, dg_ref, dcw_ref, s_ref):
        i = pl.program_id(0)

        @pl.when(i == 0)
        def _():
            dw_ref[...] = jnp.zeros_like(dw_ref)
            dg_ref[...] = jnp.zeros_like(dg_ref)
            dcw_ref[...] = jnp.zeros_like(dcw_ref)

        first, last = i > 0, i < nt - 1
        dxc = cf_ref[...] + cb_ref[...]
        dxc_p = jnp.where(first, cfp_ref[...] + cbp_ref[...], 0.0)
        dxc_n = jnp.where(last, cfn_ref[...] + cbn_ref[...], 0.0)
        cw = cw_ref[...]
        dxr = (_shifted(dxc, dxc_p, dxc_n, 2) * cw[0:1, :] + _shifted(dxc, dxc_p, dxc_n, 1) * cw[1:2, :]
               + dxc * cw[2:3, :] + _shifted(dxc, dxc_p, dxc_n, -1) * cw[3:4, :])
        xrv = xr_ref[...]
        xr_p = jnp.where(first, xrp_ref[...], 0.0)
        xr_n = jnp.where(last, xrn_ref[...], 0.0)
        for k, off in enumerate((-2, -1, 0, 1)):
            dcw_ref[k:k + 1, :] += jnp.sum(dxc * _shifted(xrv, xr_p, xr_n, off), axis=0, keepdims=True)
        dcw_ref[4:5, :] += jnp.sum(dxc, axis=0, keepdims=True)

        def total(a, b, c_):
            return a[0].astype(F32) + _merge_classes(b, s_ref, F32_LAYOUT, c_)

        dproj = jnp.concatenate(
            [dxr, dgate_ref[...], total(dq1, dq2, dq3_), total(dk1, dk2, dk3_), total(dv1, dv2, dv3_)],
            axis=-1).astype(BF16)
        xv = x_ref[...]
        rstd, xh = _rms(xv)
        hb = (xh * g_ref[...]).astype(BF16)
        dh = jnp.zeros((tb, D_MODEL), F32)
        for j in range(N_SHARD):
            dpj = dproj[:, j * IN_BLK:(j + 1) * IN_BLK]
            dh = dh + _dot_nt(dpj, w_ref[j])
            dw_ref[j] += _dot_tn(hb, dpj)
        dxn, dg = _rms_bwd(dh, g_ref[...], xh, rstd)
        dg_ref[...] += dg
        dx_ref[...] = dx1_ref[...] + dxn

    halo = _halo_specs(S, ident, tb)
    return _call(
        body, "inproj_bwd", (nt,),
        [_rows(D_MODEL, tb), _rows(D_MODEL, tb)] + halo * 3 + [_rows(512, tb)]
        + [_class_spec(1, tb), _class_spec(F32_LAYOUT, tb), _class_spec(F32_LAYOUT, tb)] * 3 + [_whole_vmem()] * 3,
        [_rows(D_MODEL, tb), _whole_vmem(), _whole_vmem(), _whole_vmem()],
        [jax.ShapeDtypeStruct((S, D_MODEL), F32), jax.ShapeDtypeStruct((N_SHARD, D_MODEL, IN_BLK), F32),
         jax.ShapeDtypeStruct((1, D_MODEL), F32), jax.ShapeDtypeStruct((SUB, 512), F32)],
        [pltpu.VMEM((4, tb, 128), F32)],
        (x, dx1, xr, xr, xr, dxc_f, dxc_f, dxc_f, dxc_b, dxc_b, dxc_b, dgate, *dq3, *dk3, *dv3, g1, conv_w, w_in), rider)


def _dw_matmul(a, b, a_cols, b_cols, name):
    S = a.shape[0]
    tk = min(S, TK_DW)
    a_shared = a.shape[1] == a_cols
    b_shared = b.shape[1] == b_cols

    def body(a_ref, b_ref, o_ref):
        @pl.when(pl.program_id(1) == 0)
        def _():
            o_ref[...] = jnp.zeros_like(o_ref)
        o_ref[0] += _dot_tn(a_ref[...], b_ref[...])

    return pl.pallas_call(
        body, grid=(N_SHARD, S // tk), name=name,
        in_specs=[pl.BlockSpec((tk, a_cols), (lambda j, k: (k, 0)) if a_shared else (lambda j, k: (k, j))),
                  pl.BlockSpec((tk, b_cols), (lambda j, k: (k, 0)) if b_shared else (lambda j, k: (k, j)))],
        out_specs=pl.BlockSpec((1, a_cols, b_cols), lambda j, k: (j, 0, 0)),
        out_shape=jax.ShapeDtypeStruct((N_SHARD, a_cols, b_cols), F32),
        compiler_params=_params(2),
    )(a, b)


def _t5_bucket_np(rel):
    nb = N_BUCKETS // 2
    max_exact = nb // 2
    ret = np.where(rel > 0, nb, 0)
    n = np.abs(rel)
    nf = np.maximum(n, 1).astype(np.float32)
    large = max_exact + (np.log(nf / np.float32(max_exact)) / np.float32(math.log(MAX_DISTANCE / max_exact))
                         * np.float32(nb - max_exact)).astype(np.int32)
    large = np.minimum(large, nb - 1)
    return ret + np.where(n < max_exact, n, large)


_VARIANT_OFFSETS = (-HALF_WIN,) * 3


def _band_index():
    kk = np.arange(K_WIN)[None, :]
    ql = np.arange(Q_BLK)[:, None]
    rel = np.stack([kk - ql + off for off in _VARIANT_OFFSETS])
    band = np.abs(rel) <= HALF_WIN
    inside = np.stack([np.broadcast_to(kk >= HALF_WIN, band[0].shape), np.ones_like(band[0]),
                       np.broadcast_to(kk < K_WIN - HALF_WIN, band[0].shape)])
    return rel, band & inside


def _bucket_tables(dil):
    rel, valid = _band_index()
    bucket = _t5_bucket_np(np.clip(rel, -HALF_WIN, HALF_WIN) * dil)
    return np.where(valid, bucket, -1).astype(np.int32)


def _bias_mats(rel_bias, rider=None):
    tables = [_bucket_tables(d) for d in DILATIONS]
    used = [sorted(set(t[t >= 0].tolist())) for t in tables]

    def one_pattern(rb_ref, t_ref, o_ref, buckets):
        bk = t_ref[1]
        for h in range(N_HEADS):
            acc = jnp.full((Q_BLK, K_WIN), NEG_INF, F32)
            for b in buckets:
                acc = jnp.where(bk == b, rb_ref[b, h], acc)
            o_ref[1, h] = acc
            for var in (0, 2):
                o_ref[var, h] = jnp.where(t_ref[var] >= 0, acc, NEG_INF)

    def body(rb_ref, t1, t2, t3, o1, o2, o3):
        for i, (t_ref, o_ref) in enumerate(((t1, o1), (t2, o2), (t3, o3))):
            pl.when(pl.program_id(0) == i)(functools.partial(one_pattern, rb_ref, t_ref, o_ref, used[i]))

    shp = jax.ShapeDtypeStruct((3, N_HEADS, Q_BLK, K_WIN), F32)
    return _call(
        body, "bias_tables", (len(DILATIONS),), [pl.BlockSpec(memory_space=pltpu.SMEM)] + [_whole_vmem()] * 3,
        [_whole_vmem()] * 3, [shp] * 3, [], (rel_bias, *[jnp.asarray(t) for t in tables]), rider)


def _variant(qb, nq):
    return jnp.where(qb == 0, 0, jnp.where(qb == nq - 1, 2, 1))


def _win_start(qb):
    return pl.multiple_of(qb * Q_BLK, Q_BLK)


def _fill_padded(src_ref, pad_ref):
    L = src_ref.shape[0]
    edge = jnp.zeros((HALF_WIN, 128), pad_ref.dtype)
    pad_ref[0:HALF_WIN, :] = edge
    pad_ref[HALF_WIN:HALF_WIN + L, :] = src_ref[...]
    pad_ref[HALF_WIN + L:2 * HALF_WIN + L, :] = edge


INNER = {1: 1, 4: 1, 16: 4}


def _attn_layout(dil, L):
    inner = INNER[dil]
    n_outer = dil // inner
    nsub = min(ATTN_SUB // inner, L // Q_BLK)
    qt = nsub * Q_BLK
    grid = (4, n_outer, L // qt)
    qspec = pl.BlockSpec((inner, None, qt, 128), lambda hp, r, s: (0, r, s, hp))
    kspec = pl.BlockSpec((inner, None, L, 128), lambda hp, r, s: (0, r, 0, hp))
    bspec = pl.BlockSpec((3, 2, Q_BLK, K_WIN), lambda hp, r, s: (0, hp, 0, 0))
    kfspec = pl.BlockSpec((None, inner * L, 128), lambda hp, r, s: (r, 0, hp))
    qfspec = kfspec if inner > 1 else pl.BlockSpec((None, qt, 128), lambda hp, r, s: (r, s, hp))
    fshape = jax.ShapeDtypeStruct((n_outer, inner * L, D_ATTN), F32)
    view = lambda t: t.reshape(inner, n_outer, L, D_ATTN)

    def qrows(m, sub):
        if inner == 1:
            return (slice(sub * Q_BLK, (sub + 1) * Q_BLK), slice(None))
        first = (pl.program_id(2) * nsub + sub) * Q_BLK
        return (pl.ds(m + inner * first, Q_BLK, stride=inner), slice(None))

    def krows(m):
        if inner == 1:
            return (slice(None), slice(None))
        return (pl.ds(m, L, stride=inner), slice(None))

    return inner, nsub, grid, qspec, kspec, bspec, qfspec, kfspec, fshape, view, qrows, krows


def _head_masks():
    lane = lax.broadcasted_iota(jnp.int32, (Q_BLK, 128), 1)
    return lane < HEAD_DIM


def _attn_fwd(q, k, v, bias):
    dil, L, _ = q.shape
    nq = L // Q_BLK
    inner, nsub, grid, qspec, kspec, bspec, qfspec, kfspec, fshape, view, qrows, krows = _attn_layout(dil, L)

    def body(q_ref, k_ref, v_ref, b_ref, o_ref, l_ref, kp, vp):
        step = pl.program_id(2)

        @pl.when(step == 0)
        def _():
            for m in range(inner):
                _fill_padded(k_ref.at[m], kp.at[m])
                _fill_padded(v_ref.at[m], vp.at[m])

        h0 = _head_masks()
        for m, sub in [(m, sub) for m in range(inner) for sub in range(nsub)]:
            qb = step * nsub + sub
            st = _win_start(qb)
            var = _variant(qb, nq)
            kw = kp[m, pl.ds(st, K_WIN), :]
            vw = vp[m, pl.ds(st, K_WIN), :]
            qs = q_ref[m, sub * Q_BLK:(sub + 1) * Q_BLK, :] * ATTN_SCALE
            zq = jnp.zeros_like(qs)
            q2 = jnp.concatenate([jnp.where(h0, qs, zq), jnp.where(h0, zq, qs)], axis=0)
            s = _dot_nt(q2, kw) + b_ref[var].reshape(2 * Q_BLK, K_WIN)
            top = jnp.max(s, axis=-1, keepdims=True)
            p = jnp.exp(s - top)
            l = jnp.sum(p, axis=-1, keepdims=True)
            out = _dot(p.astype(BF16), vw) / l
            lse = top + jnp.log(l)
            o_ref[qrows(m, sub)] = jnp.where(h0, out[0:Q_BLK], out[Q_BLK:2 * Q_BLK])
            l_ref[qrows(m, sub)] = jnp.where(h0, lse[0:Q_BLK], lse[Q_BLK:2 * Q_BLK])

    return pl.pallas_call(
        body, grid=grid, name=f"attn_fwd_d{dil}",
        in_specs=[qspec, kspec, kspec, bspec], out_specs=[qfspec, qfspec], out_shape=[fshape, fshape],
        scratch_shapes=[pltpu.VMEM((inner, L + 2 * HALF_WIN, 128), BF16)] * 2,
        compiler_params=_params(3),
    )(view(q), view(k), view(v), bias)


def _attn_bwd(q, k, v, bias, do, o, lse, rider=None):
    dil, L, _ = q.shape
    nq = L // Q_BLK
    inner, nsub, grid, qspec, kspec, bspec, qfspec, kfspec, fshape, view, qrows, krows = _attn_layout(dil, L)
    nstep = grid[2]

    def body(q_ref, k_ref, v_ref, b_ref, do_ref, o_ref, l_ref, dq_ref, dk_ref, dv_ref, db_ref, db_s,
             kp, vp, dkp, dvp, carry):
        hp, step = pl.program_id(0), pl.program_id(2)
        first = (hp == 0) & (pl.program_id(1) == 0) & (step == 0)
        last = (hp == grid[0] - 1) & (pl.program_id(1) == grid[1] - 1) & (step == nstep - 1)

        @pl.when(first)
        def _():
            db_s[...] = jnp.zeros_like(db_s)

        @pl.when(step == 0)
        def _():
            for m in range(inner):
                _fill_padded(k_ref.at[m], kp.at[m])
                _fill_padded(v_ref.at[m], vp.at[m])
            carry[...] = jnp.zeros_like(carry)

        h0 = _head_masks()
        for m, sub in [(m, sub) for m in range(inner) for sub in range(nsub)]:
            if sub == 0:
                carry_k, carry_v = carry[m, 0], carry[m, 1]
            qb = step * nsub + sub
            st = _win_start(qb)
            var = _variant(qb, nq)
            kw = kp[m, pl.ds(st, K_WIN), :]
            vw = vp[m, pl.ds(st, K_WIN), :]
            qs = q_ref[m, sub * Q_BLK:(sub + 1) * Q_BLK, :] * ATTN_SCALE
            dof = do_ref[qrows(m, sub)]
            dob = dof.astype(BF16)
            prod = dof * o_ref[qrows(m, sub)]
            lsev = l_ref[qrows(m, sub)]
            zq, zd = jnp.zeros_like(qs), jnp.zeros_like(dob)
            q2 = jnp.concatenate([jnp.where(h0, qs, zq), jnp.where(h0, zq, qs)], axis=0)
            do2 = jnp.concatenate([jnp.where(h0, dob, zd), jnp.where(h0, zd, dob)], axis=0)
            lse2 = jnp.concatenate([lsev[:, 0:1], lsev[:, HEAD_DIM:HEAD_DIM + 1]], axis=0)
            dd2 = jnp.concatenate([jnp.sum(jnp.where(h0, prod, 0.0), axis=-1, keepdims=True),
                                   jnp.sum(jnp.where(h0, 0.0, prod), axis=-1, keepdims=True)], axis=0)
            s = _dot_nt(q2, kw) + b_ref[var].reshape(2 * Q_BLK, K_WIN)
            p = jnp.exp(s - lse2)
            ds = p * (_dot_nt(do2, vw) - dd2)
            db_s[var, pl.ds(hp * 2, 2)] += ds.reshape(2, Q_BLK, K_WIN)
            dsb = ds.astype(BF16)
            dv_acc = _dot_tn(p.astype(BF16), do2)
            dk_acc = _dot_tn(dsb, q2)
            dq2 = _dot(dsb, kw) * ATTN_SCALE
            dq_ref[qrows(m, sub)] = jnp.where(h0, dq2[0:Q_BLK], dq2[Q_BLK:2 * Q_BLK]).astype(dq_ref.dtype)
            dkp[m, pl.ds(st, Q_BLK), :] = carry_k + dk_acc[0:Q_BLK]
            dvp[m, pl.ds(st, Q_BLK), :] = carry_v + dv_acc[0:Q_BLK]
            carry_k, carry_v = dk_acc[Q_BLK:K_WIN], dv_acc[Q_BLK:K_WIN]
            if sub == nsub - 1:
                carry[m, 0] = carry_k
                carry[m, 1] = carry_v

        @pl.when(step == nstep - 1)
        def _():
            for m in range(inner):
                dkp[m, L:L + Q_BLK, :] = carry[m, 0]
                dvp[m, L:L + Q_BLK, :] = carry[m, 1]
                dk_ref[krows(m)] = dkp[m, HALF_WIN:HALF_WIN + L, :].astype(dk_ref.dtype)
                dv_ref[krows(m)] = dvp[m, HALF_WIN:HALF_WIN + L, :].astype(dv_ref.dtype)

        @pl.when(last)
        def _():
            db_ref[...] = db_s[...]

    dbshape = (3, N_HEADS, Q_BLK, K_WIN)
    gshape = jax.ShapeDtypeStruct(fshape.shape, BF16 if inner == 1 else F32)
    return _call(
        body, f"attn_bwd_d{dil}", grid,
        [qspec, kspec, kspec, bspec, qfspec, qfspec, qfspec],
        [qfspec, kfspec, kfspec, _whole_vmem()],
        [gshape, gshape, gshape, jax.ShapeDtypeStruct(dbshape, F32)],
        [pltpu.VMEM(dbshape, F32)] + [pltpu.VMEM((inner, L + 2 * HALF_WIN, 128), BF16)] * 2
        + [pltpu.VMEM((inner, L + 2 * HALF_WIN, 128), F32)] * 2 + [pltpu.VMEM((inner, 2, Q_BLK, 128), F32)],
        (view(q), view(k), view(v), bias, do, o, lse), rider)


def _bucket_onehots(dil):
    m = np.zeros((3, K_WIN, N_BUCKETS), np.float32)
    for var, off in enumerate(_VARIANT_OFFSETS):
        for rel in range(-HALF_WIN, HALF_WIN + 1):
            col = (rel - off + Q_BLK - 1) % K_WIN
            m[var, col, int(_t5_bucket_np(np.asarray(rel * dil)))] = 1.0
    return jnp.asarray(m)


def _bias_grad(dbs):
    onehots = [_bucket_onehots(d) for d in DILATIONS]
    flip = jnp.asarray(np.eye(Q_BLK, dtype=np.float32)[::-1].copy())

    def body(d1, d2, d3, m1, m2, m3, flip_ref, out_ref):
        hp = lax.Precision.HIGHEST
        acc = jnp.zeros((N_HEADS, N_BUCKETS), F32)
        for d_ref, m_ref in ((d1, m1), (d2, m2), (d3, m3)):
            for var in range(3):
                rows = []
                for h in range(N_HEADS):
                    xrev = jnp.dot(flip_ref[...], d_ref[var, h], precision=hp, preferred_element_type=F32)
                    y = pltpu.roll(xrev, 0, 1, stride=1, stride_axis=0)
                    rows.append(jnp.sum(y, axis=0, keepdims=True))
                vec = jnp.concatenate(rows, axis=0)
                acc = acc + jnp.dot(vec, m_ref[var], precision=hp, preferred_element_type=F32)
        out_ref[...] = acc

    return pl.pallas_call(
        body, name="bias_grad", out_shape=jax.ShapeDtypeStruct((N_HEADS, N_BUCKETS), F32),
        compiler_params=_params(0),
    )(*dbs, *onehots, flip)


def _block_diag(w):
    eye = jnp.eye(N_RNN_BLOCKS, dtype=w.dtype)
    return jnp.einsum("ncd,nm->ncmd", w, eye).reshape(D_RNN, D_RNN).astype(BF16)


def _diag_blocks(dense):
    d = dense.reshape(N_RNN_BLOCKS, RNN_BLOCK, N_RNN_BLOCKS, RNN_BLOCK)
    return jnp.stack([d[n, :, n, :] for n in range(N_RNN_BLOCKS)])


EARLY = ("w_out", "w_up", "w_down")


def _local_step(x, target, p, shards=None):
    p = dict(p)
    first = None if shards is None else _gather_rider(["w_in"], [shards["w_in"]], shards["conv_w"])
    biases, got = _bias_mats(p["rel_bias"], first)
    if shards is not None:
        p["w_in"] = got[0]
        p["conv_w"] = jnp.transpose(got[1], (1, 0, 2)).reshape(4, D_RNN)
    lru = {}
    for dname in ("fwd", "bwd"):
        lru[dname] = (_block_diag(p["lru_wa_" + dname]), p["lru_ba_" + dname], _block_diag(p["lru_wx_" + dname]),
                      p["lru_bx_" + dname], p["lru_lam_" + dname])

    def gather(name):
        return None if shards is None else _gather_rider([name], [shards[name]])

    (xr, gate, *qkv), got = _inproj_fwd(x, p["attn_norm_g"], p["w_in"], gather("w_out"))
    p.update(zip(["w_out"], got))
    qs, ks, vs = qkv[0:3], qkv[3:6], qkv[6:9]
    hf, got = _rnn_fwd(xr, p["conv_w"], p["conv_b"], *lru["fwd"], reverse=False, rider=gather("w_up"))
    p.update(zip(["w_up"], got))
    hb, got = _rnn_fwd(xr, p["conv_w"], p["conv_b"], *lru["bwd"], reverse=True, rider=gather("w_down"))
    p.update(zip(["w_down"], got))
    outs, lses = [], []
    for q, k, v, bias in zip(qs, ks, vs, biases):
        o, l = _attn_fwd(q, k, v, bias)
        outs.append(o)
        lses.append(l)
    x1, mixb, *yl = _mix_fwd(outs, lses, hf, hb, gate, x, p["norm_rnn_g"], p["norm_attn_g"], p["w_out"])
    yas, lsts = [yl[0], yl[1], yl[1]], [yl[2], yl[3], yl[3]]
    dx1, h2b, a2b, dub, dx2b, loss_vec, dg_fin, dg_mlp = _mlp_fwd_bwd(
        x1, target, p["mlp_norm_g"], p["final_norm_g"], p["w_up"], p["w_down"])
    dhs, dgate, dya1, dya4, dw_out, dg_rnn, dg_attn = _mix_bwd(dx1, p["w_out"], mixb, yas[0], hf, hb, gate,
                                                               p["norm_rnn_g"], p["norm_attn_g"])
    dyas = [dya1, dya4, dya4]
    dw_up = _dw_matmul(h2b, dub, D_MODEL, FF_BLK, "dw_up")
    dw_down = _dw_matmul(a2b, dx2b, FF_BLK, D_MODEL, "dw_down")
    early = [dw_out, dw_up, dw_down]
    dqs, dks, dvs, dbs = [], [], [], []
    for i, (q, k, v, bias, dya, ya, lse) in enumerate(zip(qs, ks, vs, biases, dyas, yas, lsts)):
        rider = None
        if shards is not None:
            make = (lambda: _pair_exchange_rider(EARLY, early), lambda: _chip_exchange_rider(early),
                    lambda: _pair_share_rider(EARLY, early))[i]
            rider = make()
        (dq, dk, dv, db), got = _attn_bwd(q, k, v, bias, dya, ya, lse, rider)
        if shards is not None and i == 0:
            core = lax.axis_index("c").reshape(1).astype(jnp.int32)
            early = [_pair_add(core, g, o, "grad_pair_add_" + n) for n, g, o in zip(EARLY, early, got)]
        elif shards is not None and i == 1:
            early = [_chip_sum(t, "grad_chip_sum_" + n) for n, t in zip(EARLY, got)]
        elif shards is not None:
            early = got
        dqs.append(dq)
        dks.append(dk)
        dvs.append(dv)
        dbs.append(db)
    d_rel_bias = _bias_grad(dbs).T
    (dxc_f, dwa_f, dwx_f, dvec_f), _ = _rnn_bwd(xr, hf, dhs, p["conv_w"], p["conv_b"], *lru["fwd"], reverse=False)
    small = {
        "lru_wa_fwd": _diag_blocks(dwa_f), "lru_ba_fwd": dvec_f[0:1], "lru_wx_fwd": _diag_blocks(dwx_f),
        "lru_bx_fwd": dvec_f[1:2], "lru_lam_fwd": dvec_f[3:4],
        "rel_bias": d_rel_bias, "norm_rnn_g": dg_rnn, "norm_attn_g": dg_attn,
        "mlp_norm_g": dg_mlp, "final_norm_g": dg_fin,
    }
    loss_local = (0.5 / D_MODEL) * jnp.sum(loss_vec)
    rider = None
    if shards is not None:
        rider = _small_gather_rider(_pack([small[n].reshape(shp) for n, shp in SMALL if n in small]
                                          + [loss_local.reshape(1)]))
    (dxc_b, dwa_b, dwx_b, dvec_b), gathered = _rnn_bwd(xr, hb, dhs, p["conv_w"], p["conv_b"], *lru["bwd"], reverse=True,
                                                       rider=rider)
    grad_x, dw_in, dg1, dconv = _inproj_bwd(x, dx1, xr, dxc_f, dxc_b, dgate, dqs, dks, dvs,
                                            p["attn_norm_g"], p["conv_w"], p["w_in"])[0]
    last = {"lru_wa_bwd": _diag_blocks(dwa_b), "lru_ba_bwd": dvec_b[0:1], "lru_wx_bwd": _diag_blocks(dwx_b),
            "lru_bx_bwd": dvec_b[1:2], "lru_lam_bwd": dvec_b[3:4],
            "attn_norm_g": dg1, "conv_w": dconv[0:4], "conv_b": dconv[4:5]}
    if shards is None:
        big = {"w_in": dw_in, "w_out": dw_out, "w_up": dw_up, "w_down": dw_down}
        return loss_local, grad_x, {**small, **last}, None, big, {}
    return loss_local, grad_x, last, gathered[0], {"w_in": dw_in}, dict(zip(EARLY, early))


BIG = ("w_in", "w_out", "w_up", "w_down")
BIG_SHARD = {"w_in": (D_MODEL, IN_BLK), "w_out": (OUT_BLK, D_MODEL), "w_up": (D_MODEL, FF_BLK), "w_down": (FF_BLK, D_MODEL)}
N_BIG = len(BIG)
N_CHIP_PEERS = 3
ANY = pl.BlockSpec(memory_space=pl.ANY)


def _place():
    x, y, c = lax.axis_index("x"), lax.axis_index("y"), lax.axis_index("c")
    chips = [(1 - x, y), (x, 1 - y), (1 - x, 1 - y)]
    return x, y, c, chips


def _remote(src, dst, send_sem, recv_sem, dev):
    return pltpu.make_async_remote_copy(src_ref=src, dst_ref=dst, send_sem=send_sem, recv_sem=recv_sem,
                                        device_id=dev, device_id_type=MESH)


def _staged_start(srcs, bufs, sems):
    legs = [pltpu.make_async_copy(s, b, sems.at[i]) for i, (s, b) in enumerate(zip(srcs, bufs))]
    for cp in legs:
        cp.start()
    return legs


def _staged_finish(legs, bufs, dsts, sems):
    out = []
    for i, (leg, b, d) in enumerate(zip(legs, bufs, dsts)):
        leg.wait()
        cp = pltpu.make_async_copy(b, d, sems.at[i])
        cp.start()
        out.append(cp)
    return out


class _Rider:
    def __init__(self, inputs, out_shape, scratch, first, late, last):
        self.inputs, self.out_shape, self.scratch = list(inputs), list(out_shape), list(scratch)
        self.first, self.late, self.last = first, late, last


def _call(body, name, grid, in_specs, out_specs, out_shape, scratch, operands, rider=None):
    n_grid = len(grid)
    if rider is None:
        res = pl.pallas_call(body, grid=grid, name=name, in_specs=in_specs, out_specs=out_specs, out_shape=out_shape,
                             scratch_shapes=scratch, compiler_params=_params(n_grid))(*operands)
        return list(res), []
    n_in, n_out, n_scr = len(in_specs), len(out_specs), len(scratch)
    ri, ro = len(rider.inputs), len(rider.out_shape)
    nsteps = int(np.prod(grid))
    late_step = max(nsteps - 3, 1)

    def wrapped(*refs):
        a, b = n_in, n_in + ri
        c, d = b + n_out, b + n_out + ro
        e = d + n_scr
        mine = refs[:a] + refs[b:c] + refs[d:e]
        theirs = (refs[a:b], refs[c:d], refs[e:])
        step = pl.program_id(0)
        for ax in range(1, n_grid):
            step = step * grid[ax] + pl.program_id(ax)
        pl.when(step == 0)(lambda: rider.first(*theirs))
        pl.when(step == late_step)(lambda: rider.late(*theirs))
        body(*mine)
        pl.when(step == nsteps - 1)(lambda: rider.last(*theirs))

    res = pl.pallas_call(
        wrapped, grid=grid, name=name, in_specs=list(in_specs) + [ANY] * ri, out_specs=list(out_specs) + [ANY] * ro,
        out_shape=list(out_shape) + rider.out_shape, scratch_shapes=list(scratch) + rider.scratch,
        compiler_params=_params(n_grid),
    )(*operands, *rider.inputs)
    return list(res[:n_out]), list(res[n_out:])


def _run_rider(rider, name):
    ri, ro = len(rider.inputs), len(rider.out_shape)

    def body(*refs):
        parts = (refs[:ri], refs[ri:ri + ro], refs[ri + ro:])
        rider.first(*parts)
        rider.late(*parts)
        rider.last(*parts)

    return list(pl.pallas_call(
        body, name=name, in_specs=[ANY] * ri, out_specs=[ANY] * ro, out_shape=rider.out_shape, scratch_shapes=rider.scratch,
        compiler_params=pltpu.CompilerParams(has_side_effects=True, vmem_limit_bytes=VMEM_LIMIT),
    )(*rider.inputs))


def _nothing(ins, outs, scr):
    return None


def _gather_rider(names, shards, conv_w=None):
    n = len(names)
    items = n + (conv_w is not None)
    halves = [BIG_SHARD[nm][0] // 2 for nm in names]

    def parts(ins, outs, scr):
        x, y, c, chips = _place()
        return x, y, c, chips, 2 * x + y, (x, y, 1 - c), scr[:8], scr[8:]

    def piece(outs, w, chip, core_half):
        return outs[w].at[chip, pl.ds(core_half * halves[w], halves[w])]

    def ici(ins, outs, sems, w, k, chip_xy, c, me):
        return _remote(ins[w].at[pl.ds(c * halves[w], halves[w])], piece(outs, w, me, c),
                       sems[0].at[w, k], sems[1].at[w, k], (*chip_xy, c))

    def first(ins, outs, scr):
        x, y, c, chips, me, sibling, sems, bufs = parts(ins, outs, scr)
        legs = _staged_start(ins, bufs, sems[6])
        for w in range(n):
            for k, chip_xy in enumerate(chips):
                ici(ins, outs, sems, w, k, chip_xy, c, me).start()
        if conv_w is not None:
            for k, (px, py) in enumerate(chips):
                _remote(ins[n], outs[n].at[me], sems[4].at[k], sems[5].at[k], (px, py, c)).start()
        _staged_finish(legs, bufs, [o.at[me] for o in outs], sems[7])

    def late(ins, outs, scr):
        x, y, c, chips, me, sibling, sems, bufs = parts(ins, outs, scr)
        for w in range(n):
            for k, (px, py) in enumerate(chips):
                landed = piece(outs, w, 2 * px + py, c)
                _remote(landed, landed, sems[0].at[w, k], sems[1].at[w, k], (px, py, c)).wait_recv()
                _remote(landed, landed, sems[2].at[w, k], sems[3].at[w, k], sibling).start()

    def last(ins, outs, scr):
        x, y, c, chips, me, sibling, sems, bufs = parts(ins, outs, scr)
        for w in range(n):
            for k, (px, py) in enumerate(chips):
                other = piece(outs, w, 2 * px + py, 1 - c)
                _remote(other, other, sems[2].at[w, k], sems[3].at[w, k], sibling).wait_recv()
        if conv_w is not None:
            for k, (px, py) in enumerate(chips):
                got = outs[n].at[2 * px + py]
                _remote(got, got, sems[4].at[k], sems[5].at[k], (px, py, c)).wait_recv()
                _remote(ins[n], outs[n].at[me], sems[4].at[k], sems[5].at[k], (px, py, c)).wait_send()
        for i in range(items):
            pltpu.make_async_copy(bufs[i], outs[i].at[me], sems[7].at[i]).wait()
        for w in range(n):
            for k, (px, py) in enumerate(chips):
                ici(ins, outs, sems, w, k, (px, py), c, me).wait_send()
                landed = piece(outs, w, 2 * px + py, c)
                _remote(landed, landed, sems[2].at[w, k], sems[3].at[w, k], sibling).wait_send()

    out_shape = [jax.ShapeDtypeStruct((N_SHARD,) + BIG_SHARD[nm], BF16) for nm in names]
    stage = [pltpu.VMEM(BIG_SHARD[nm], BF16) for nm in names]
    inputs = list(shards)
    if conv_w is not None:
        out_shape.append(jax.ShapeDtypeStruct((N_SHARD,) + conv_w.shape, F32))
        stage.append(pltpu.VMEM(conv_w.shape, F32))
        inputs.append(conv_w)
    scratch = ([pltpu.SemaphoreType.DMA((n, N_CHIP_PEERS))] * 4 + [pltpu.SemaphoreType.DMA((N_CHIP_PEERS,))] * 2
               + [pltpu.SemaphoreType.DMA((items,))] * 2 + stage)
    return _Rider(inputs, out_shape, scratch, first, late, last)


def _pair_exchange_rider(names, grads):
    def copies(ins, outs, scr):
        x, y, c, _ = _place()
        out = []
        for w, nm in enumerate(names):
            h = BIG_SHARD[nm][0] // 2
            out.append(_remote(ins[w].at[:, pl.ds((1 - c) * h, h), :], outs[w], scr[0].at[w], scr[1].at[w], (x, y, 1 - c)))
        return out

    def first(ins, outs, scr):
        for cp in copies(ins, outs, scr):
            cp.start()

    def last(ins, outs, scr):
        for cp in copies(ins, outs, scr):
            cp.wait()

    out_shape = [jax.ShapeDtypeStruct((N_SHARD, BIG_SHARD[nm][0] // 2, BIG_SHARD[nm][1]), F32) for nm in names]
    return _Rider(grads, out_shape, [pltpu.SemaphoreType.DMA((len(names),))] * 2, first, _nothing, last)


def _pair_add(core, grad, other, name):
    _, r, cols = grad.shape
    h = r // 2
    th = min(h, 256)
    per = h // th

    def body(c_ref, g_ref, o_ref, out_ref):
        out_ref[...] = (g_ref[...] + o_ref[...]).astype(BF16)

    return pl.pallas_call(
        body, name=name,
        grid_spec=pltpu.PrefetchScalarGridSpec(
            num_scalar_prefetch=1, grid=(N_SHARD, per),
            in_specs=[pl.BlockSpec((1, th, cols), lambda j, i, c_ref: (j, c_ref[0] * per + i, 0)),
                      pl.BlockSpec((1, th, cols), lambda j, i, c_ref: (j, i, 0))],
            out_specs=pl.BlockSpec((1, th, cols), lambda j, i, c_ref: (j, i, 0))),
        out_shape=jax.ShapeDtypeStruct((N_SHARD, h, cols), BF16),
        compiler_params=_params(2),
    )(core, grad, other)


def _chip_exchange_rider(parts):
    n = len(parts)

    def sends(ins, outs, scr):
        x, y, c, chips = _place()
        me = 2 * x + y
        return [_remote(ins[w].at[2 * px + py], outs[w].at[me], scr[0].at[w, k], scr[1].at[w, k], (px, py, c))
                for w in range(n) for k, (px, py) in enumerate(chips)]

    def first(ins, outs, scr):
        x, y, c, chips = _place()
        me = 2 * x + y
        legs = _staged_start([r.at[me] for r in ins], scr[4:], scr[2])
        for cp in sends(ins, outs, scr):
            cp.start()
        _staged_finish(legs, scr[4:], [o.at[me] for o in outs], scr[3])

    def last(ins, outs, scr):
        x, y, c, chips = _place()
        me = 2 * x + y
        for w in range(n):
            for k, (px, py) in enumerate(chips):
                got = outs[w].at[2 * px + py]
                _remote(got, got, scr[0].at[w, k], scr[1].at[w, k], (px, py, c)).wait_recv()
        for cp in sends(ins, outs, scr):
            cp.wait_send()
        for w in range(n):
            pltpu.make_async_copy(scr[4 + w], outs[w].at[me], scr[3].at[w]).wait()

    out_shape = [jax.ShapeDtypeStruct(p.shape, BF16) for p in parts]
    scratch = ([pltpu.SemaphoreType.DMA((n, N_CHIP_PEERS))] * 2 + [pltpu.SemaphoreType.DMA((n,))] * 2
               + [pltpu.VMEM(p.shape[1:], BF16) for p in parts])
    return _Rider(parts, out_shape, scratch, first, _nothing, last)


def _chip_sum(parts, name):
    _, h, cols = parts.shape
    th = min(h, 256)

    def body(p_ref, out_ref):
        acc = p_ref[0].astype(F32)
        for j in range(1, N_SHARD):
            acc = acc + p_ref[j].astype(F32)
        out_ref[...] = acc

    return pl.pallas_call(
        body, name=name, grid=(h // th,),
        in_specs=[pl.BlockSpec((N_SHARD, th, cols), lambda i: (0, i, 0))],
        out_specs=pl.BlockSpec((th, cols), lambda i: (i, 0)),
        out_shape=jax.ShapeDtypeStruct((h, cols), F32),
        compiler_params=_params(),
    )(parts)


def _pair_share_rider(names, halves):
    n = len(names)
    hs = [BIG_SHARD[nm][0] // 2 for nm in names]

    def mine(outs, c):
        return [outs[w].at[pl.ds(c * hs[w], hs[w])] for w in range(n)]

    def first(ins, outs, scr):
        x, y, c, _ = _place()
        legs = _staged_start(ins, scr[4:], scr[2])
        for w, dst in enumerate(mine(outs, c)):
            _remote(ins[w], dst, scr[0].at[w], scr[1].at[w], (x, y, 1 - c)).start()
        _staged_finish(legs, scr[4:], mine(outs, c), scr[3])

    def last(ins, outs, scr):
        x, y, c, _ = _place()
        for w, (theirs, dst) in enumerate(zip(mine(outs, 1 - c), mine(outs, c))):
            _remote(theirs, theirs, scr[0].at[w], scr[1].at[w], (x, y, 1 - c)).wait_recv()
            _remote(ins[w], dst, scr[0].at[w], scr[1].at[w], (x, y, 1 - c)).wait_send()
            pltpu.make_async_copy(scr[4 + w], dst, scr[3].at[w]).wait()

    out_shape = [jax.ShapeDtypeStruct(BIG_SHARD[nm], F32) for nm in names]
    scratch = [pltpu.SemaphoreType.DMA((n,))] * 4 + [pltpu.VMEM((h, BIG_SHARD[nm][1]), F32) for nm, h in zip(names, hs)]
    return _Rider(halves, out_shape, scratch, first, _nothing, last)


N_DEV = 8


def _all_peers(x, y, c):
    return [((1 - x) if fx else x, (1 - y) if fy else y, (1 - c) if fc else c)
            for fx in (0, 1) for fy in (0, 1) for fc in (0, 1) if fx or fy or fc]


def _small_gather_rider(vec):
    def sends(ins, outs, scr):
        x, y, c, _ = _place()
        me = 4 * x + 2 * y + c
        return [_remote(ins[0], outs[0].at[me], scr[0].at[k], scr[1].at[k], dev) for k, dev in enumerate(_all_peers(x, y, c))]

    def first(ins, outs, scr):
        x, y, c, _ = _place()
        legs = _staged_start(ins, scr[4:], scr[2])
        for cp in sends(ins, outs, scr):
            cp.start()
        _staged_finish(legs, scr[4:], [outs[0].at[4 * x + 2 * y + c]], scr[3])

    def last(ins, outs, scr):
        x, y, c, _ = _place()
        for k, (px, py, pc) in enumerate(_all_peers(x, y, c)):
            got = outs[0].at[4 * px + 2 * py + pc]
            _remote(got, got, scr[0].at[k], scr[1].at[k], (px, py, pc)).wait_recv()
        for cp in sends(ins, outs, scr):
            cp.wait_send()
        pltpu.make_async_copy(scr[4], outs[0].at[4 * x + 2 * y + c], scr[3].at[0]).wait()

    scratch = ([pltpu.SemaphoreType.DMA((N_DEV - 1,))] * 2 + [pltpu.SemaphoreType.DMA((1,))] * 2
               + [pltpu.VMEM(vec.shape, F32)])
    return _Rider([vec], [jax.ShapeDtypeStruct((N_DEV,) + vec.shape, F32)], scratch, first, _nothing, last)


def _sum_devices(gathered):
    def body(g_ref, out_ref):
        acc = g_ref[0]
        for j in range(1, N_DEV):
            acc = acc + g_ref[j]
        out_ref[...] = acc

    return pl.pallas_call(body, name="sum_devices", out_shape=jax.ShapeDtypeStruct(gathered.shape[1:], F32),
                          compiler_params=_params(0))(gathered)


def _allreduce_small(vec):
    rows = vec.shape[0]

    def body(v_ref, sum_ref, gat_ref, send, recv, loc_sem):
        x, y, c, chips = _place()
        sibling = (x, y, 1 - c)
        slot = lambda px, py, pc: gat_ref.at[4 * px + 2 * py + pc]
        lc = pltpu.make_async_copy(v_ref, slot(x, y, c), loc_sem)
        lc.start()
        sends = [_remote(v_ref, slot(x, y, c), send.at[0], recv.at[0], sibling)]
        sends += [_remote(v_ref, slot(x, y, c), send.at[1 + k], recv.at[1 + k], (px, py, c))
                  for k, (px, py) in enumerate(chips)]
        for cp in sends:
            cp.start()
        for k, (px, py) in enumerate(chips):
            got = slot(px, py, c)
            _remote(got, got, send.at[1 + k], recv.at[1 + k], (px, py, c)).wait_recv()
            cp = _remote(got, got, send.at[4 + k], recv.at[4 + k], sibling)
            cp.start()
            sends.append(cp)
        got = slot(x, y, 1 - c)
        _remote(got, got, send.at[0], recv.at[0], sibling).wait_recv()
        for k, (px, py) in enumerate(chips):
            got = slot(px, py, 1 - c)
            _remote(got, got, send.at[4 + k], recv.at[4 + k], sibling).wait_recv()
        for cp in sends:
            cp.wait_send()
        lc.wait()
        acc = gat_ref[0]
        for j in range(1, N_DEV):
            acc = acc + gat_ref[j]
        sum_ref[...] = acc

    total, _ = pl.pallas_call(
        body, name="allreduce_small",
        in_specs=[_whole_vmem()], out_specs=[_whole_vmem(), _whole_vmem()],
        out_shape=[jax.ShapeDtypeStruct((rows, 128), F32), jax.ShapeDtypeStruct((N_DEV, rows, 128), F32)],
        scratch_shapes=[pltpu.SemaphoreType.DMA((N_DEV - 1,))] * 2 + [pltpu.SemaphoreType.DMA(())],
        compiler_params=pltpu.CompilerParams(has_side_effects=True, vmem_limit_bytes=VMEM_LIMIT),
    )(vec)
    return total


def _adam_math(w_ref, g_ref, m_ref, v_ref, d_ref, m2_ref, v2_ref):
    c1 = 1.0 - ADAM_B1 ** ADAM_STEP
    c2 = 1.0 - ADAM_B2 ** ADAM_STEP
    gv = g_ref[...]
    m2 = ADAM_B1 * m_ref[...] + (1.0 - ADAM_B1) * gv
    v2 = ADAM_B2 * v_ref[...] + (1.0 - ADAM_B2) * (gv * gv)
    m2_ref[...] = m2
    v2_ref[...] = v2
    d_ref[...] = -ADAM_LR * ((m2 / c1) / (jnp.sqrt(v2 / c2) + ADAM_EPS) + ADAM_WD * w_ref[...])


def _adamw_many(ws, gs, ms, vs):
    n = len(ws)

    def body(*refs):
        for i in range(n):
            _adam_math(*[refs[k * n + i] for k in range(7)])

    shapes = [jax.ShapeDtypeStruct(w.shape, F32) for w in ws]
    res = pl.pallas_call(body, name="adamw_small", out_shape=shapes * 3, compiler_params=_params(0))(*ws, *gs, *ms, *vs)
    return res[:n], res[n:2 * n], res[2 * n:]


def _adamw(w, g, m, v, name):
    rows, cols = w.shape
    tr = 256 if rows % 256 == 0 else rows

    def body(w_ref, g_ref, m_ref, v_ref, d_ref, m2_ref, v2_ref):
        _adam_math(w_ref, g_ref, m_ref, v_ref, d_ref, m2_ref, v2_ref)

    spec = pl.BlockSpec((tr, cols), lambda i: (i, 0))
    shp = jax.ShapeDtypeStruct((rows, cols), F32)
    return pl.pallas_call(
        body, name=name, grid=(rows // tr,), in_specs=[spec] * 4, out_specs=[spec] * 3, out_shape=[shp] * 3,
        compiler_params=_params(),
    )(w, g, m, v)


SMALL = (
    ("attn_norm_g", (1, 1024)), ("conv_w", (1, 4, 512)), ("conv_b", (1, 512)),
    ("lru_wa_fwd", (1, 8, 64, 64)), ("lru_ba_fwd", (1, 512)), ("lru_wx_fwd", (1, 8, 64, 64)), ("lru_bx_fwd", (1, 512)),
    ("lru_lam_fwd", (1, 512)),
    ("lru_wa_bwd", (1, 8, 64, 64)), ("lru_ba_bwd", (1, 512)), ("lru_wx_bwd", (1, 8, 64, 64)), ("lru_bx_bwd", (1, 512)),
    ("lru_lam_bwd", (1, 512)),
    ("rel_bias", (32, 8)), ("norm_rnn_g", (1, 512)), ("norm_attn_g", (1, 512)), ("mlp_norm_g", (1, 1024)),
    ("final_norm_g", (1024,)),
)
PACK_ROW = 8 * 128


def _pack(parts):
    flat = jnp.concatenate([p.reshape(-1) for p in parts])
    pad = (-flat.shape[0]) % PACK_ROW
    return jnp.pad(flat, (0, pad)).reshape(-1, 128)


def _unpack(packed, shapes):
    flat = packed.reshape(-1)
    out, off = [], 0
    for shp in shapes:
        n = int(np.prod(shp))
        out.append(flat[off:off + n].reshape(shp))
        off += n
    return out


WEIGHT_ORDER = ("attn_norm_g", "w_in", "conv_w", "conv_b", "lru_wa_fwd", "lru_ba_fwd", "lru_wx_fwd", "lru_bx_fwd",
                "lru_lam_fwd", "lru_wa_bwd", "lru_ba_bwd", "lru_wx_bwd", "lru_bx_bwd", "lru_lam_bwd", "rel_bias",
                "norm_rnn_g", "norm_attn_g", "w_out", "mlp_norm_g", "w_up", "w_down", "final_norm_g")


def kernel(x, attn_norm_g, w_in, conv_w, conv_b, lru_wa_fwd, lru_ba_fwd, lru_wx_fwd, lru_bx_fwd, lru_lam_fwd, lru_wa_bwd, lru_ba_bwd, lru_wx_bwd, lru_bx_bwd, lru_lam_bwd, rel_bias, norm_rnn_g, norm_attn_g, w_out, mlp_norm_g, w_up, w_down, final_norm_g, loss_target, m_attn_norm_g, m_w_in, m_conv_w, m_conv_b, m_lru_wa_fwd, m_lru_ba_fwd, m_lru_wx_fwd, m_lru_bx_fwd, m_lru_lam_fwd, m_lru_wa_bwd, m_lru_ba_bwd, m_lru_wx_bwd, m_lru_bx_bwd, m_lru_lam_bwd, m_rel_bias, m_norm_rnn_g, m_norm_attn_g, m_w_out, m_mlp_norm_g, m_w_up, m_w_down, m_final_norm_g, v_attn_norm_g, v_w_in, v_conv_w, v_conv_b, v_lru_wa_fwd, v_lru_ba_fwd, v_lru_wx_fwd, v_lru_bx_fwd, v_lru_lam_fwd, v_lru_wa_bwd, v_lru_ba_bwd, v_lru_wx_bwd, v_lru_bx_bwd, v_lru_lam_bwd, v_rel_bias, v_norm_rnn_g, v_norm_attn_g, v_w_out, v_mlp_norm_g, v_w_up, v_w_down, v_final_norm_g):
    given = dict(locals())
    w = {n: given[n] for n in WEIGHT_ORDER}
    m = {n: given["m_" + n] for n in WEIGHT_ORDER}
    v = {n: given["v_" + n] for n in WEIGHT_ORDER}

    chip = lax.axis_index("x") * 2 + lax.axis_index("y")
    core = lax.axis_index("c")

    shards = {n: w[n][0].astype(BF16) for n in BIG}
    shards["conv_w"] = w["conv_w"][0]
    p = {n: (t[0] if t.ndim >= 3 else t) for n, t in w.items() if n not in BIG and n != "conv_w"}
    p["final_norm_g"] = w["final_norm_g"].reshape(1, D_MODEL)

    _, grad_x, small, gathered, big, reduced = _local_step(x[0], loss_target[0], p, shards)

    late = tuple(big)
    grads = [big[n] for n in late]
    others = _run_rider(_pair_exchange_rider(late, grads), "grad_pair_exchange")
    core_arr = core.reshape(1).astype(jnp.int32)
    parts = [_pair_add(core_arr, g, o, "grad_pair_add_" + n) for n, g, o in zip(late, grads, others)]
    landed = _run_rider(_chip_exchange_rider(parts), "grad_chip_exchange")
    halves = [_chip_sum(t, "grad_chip_sum_" + n) for n, t in zip(late, landed)]
    reduced.update(zip(late, _run_rider(_pair_share_rider(late, halves), "grad_pair_share")))

    early_small = [(n, shp) for n, shp in SMALL if n not in small]
    late_small = [(n, shp) for n, shp in SMALL if n in small]
    *early_g, loss = _unpack(_sum_devices(gathered), [shp for _, shp in early_small] + [(1,)])
    late_g = _unpack(_allreduce_small(_pack([small[n].reshape(shp) for n, shp in late_small])),
                     [shp for _, shp in late_small])
    g = dict(zip([n for n, _ in early_small + late_small], early_g + late_g))
    g["conv_w"] = lax.dynamic_slice_in_dim(g["conv_w"], chip * (D_RNN // N_SHARD), D_RNN // N_SHARD, axis=2)
    for n in BIG:
        g[n] = reduced[n][None]

    delta, new_m, new_v = {}, {}, {}
    for n in BIG:
        d2, m2, v2 = _adamw(w[n][0], reduced[n], m[n][0], v[n][0], "adamw_" + n)
        delta[n], new_m[n], new_v[n] = d2[None], m2[None], v2[None]
    names = [n for n, _ in SMALL]
    for dst, src in zip((delta, new_m, new_v), _adamw_many(*[[t[n] for n in names] for t in (w, g, m, v)])):
        dst.update(dict(zip(names, src)))

    return (loss.reshape(()), grad_x[None], *[g[n] for n in WEIGHT_ORDER], *[delta[n] for n in WEIGHT_ORDER],
            *[new_m[n] for n in WEIGHT_ORDER], *[new_v[n] for n in WEIGHT_ORDER])
```

```python
import functools
import math

import numpy as np
import jax
import jax.numpy as jnp
from jax import lax
from jax.experimental import pallas as pl
from jax.experimental.pallas import tpu as pltpu

F32 = jnp.float32
BF16 = jnp.bfloat16

D_MODEL = 1024
D_RNN = 512
D_ATTN = 512
N_HEADS = 8
HEAD_DIM = 64
N_RNN_BLOCKS = 8
RNN_BLOCK = 64
D_IN = 2 * D_RNN + 3 * D_ATTN
D_FF = 4 * D_MODEL
N_SHARD = 4
IN_BLK = D_IN // N_SHARD
OUT_BLK = D_MODEL // N_SHARD
FF_BLK = D_FF // N_SHARD
EPS = 1e-6
NEG_INF = -1e30
LRU_C = 8.0
DILATIONS = (1, 4, 16)
F32_LAYOUT = 4
HALF_WIN = 64
Q_BLK = 128
K_WIN = 256
N_BUCKETS = 32
MAX_DISTANCE = 1024
ATTN_SCALE = HEAD_DIM ** -0.5

ADAM_LR = 0.001
ADAM_B1 = 0.9
ADAM_B2 = 0.999
ADAM_EPS = 1e-08
ADAM_WD = 0.01
ADAM_STEP = 10

TS = 512
TS_MLP = 256
TS_INPROJ_BWD = 512
ATTN_SUB = 16
TK_DW = 4096
SCAN_UNROLL = 8
SUB = 8
VMEM_LIMIT = 56 * 1024 * 1024
GELU_C0 = math.sqrt(2.0 / math.pi)
GELU_C1 = 0.044715

MESH = pl.DeviceIdType.MESH


def _params(n_grid=1):
    return pltpu.CompilerParams(vmem_limit_bytes=VMEM_LIMIT, dimension_semantics=("arbitrary",) * n_grid)


def _whole_vmem():
    return pl.BlockSpec(memory_space=pltpu.VMEM)


def _rows(width, tile=TS):
    return pl.BlockSpec((tile, width), lambda i: (i, 0))


def _sigmoid(z):
    return 0.5 * jnp.tanh(0.5 * z) + 0.5


def _log1p(u):
    w = 1.0 + u
    return jnp.where(w == 1.0, u, jnp.log(w) * (u / (w - 1.0)))


def _softplus(z):
    return jnp.maximum(z, 0.0) + _log1p(jnp.exp(-jnp.abs(z)))


def _gelu_parts(g):
    inner = GELU_C0 * (g + GELU_C1 * g * g * g)
    t = jnp.tanh(inner)
    val = 0.5 * g * (1.0 + t)
    dinner = GELU_C0 * (1.0 + 3.0 * GELU_C1 * g * g)
    grad = 0.5 * (1.0 + t) + 0.5 * g * (1.0 - t * t) * dinner
    return val, grad


def _rms(x):
    rstd = lax.rsqrt(jnp.mean(x * x, axis=-1, keepdims=True) + EPS)
    return rstd, x * rstd


def _rms_bwd(dy, g, xhat, rstd):
    dxh = dy * g
    dx = rstd * (dxh - xhat * jnp.mean(dxh * xhat, axis=-1, keepdims=True))
    dg = jnp.sum(dy * xhat, axis=0, keepdims=True)
    return dx, dg


def _dot(a, b):
    return jnp.dot(a, b, preferred_element_type=F32)


def _dot_nt(a, b):
    return lax.dot_general(a, b, (((1,), (1,)), ((), ())), preferred_element_type=F32)


def _dot_tn(a, b):
    return lax.dot_general(a, b, (((0,), (0,)), ((), ())), preferred_element_type=F32)


def _shifted(tile, prev8, next8, k):
    n = tile.shape[0]
    row = lax.broadcasted_iota(jnp.int32, tile.shape, 0)
    if k == 0:
        return tile
    if k < 0:
        r = pltpu.roll(tile, -k, 0)
        for j in range(-k):
            r = jnp.where(row == j, prev8[SUB + j + k:SUB + j + k + 1, :], r)
        return r
    r = pltpu.roll(tile, n - k, 0)
    for j in range(k):
        r = jnp.where(row == n - k + j, next8[j:j + 1, :], r)
    return r


def _to_lane_blocks(val, s_ref):
    for j in range(val.shape[1] // 128):
        s_ref[j] = val[:, j * 128:(j + 1) * 128]


def _from_lane_blocks(s_ref):
    return jnp.concatenate([s_ref[j] for j in range(s_ref.shape[0])], axis=-1)


def _class_rows(s_ref, r, dil):
    n = s_ref.shape[1] // dil
    return jnp.concatenate([s_ref[j, pl.ds(r, n, stride=dil), :] for j in range(s_ref.shape[0])], axis=-1)


def _split_classes(val, s_ref, out_ref, dil):
    _to_lane_blocks(val, s_ref)
    for r in range(dil):
        out_ref[r] = _class_rows(s_ref, r, dil).astype(out_ref.dtype)


def _merge_classes(in_ref, s_ref, dil, also_ref=None):
    n = s_ref.shape[1] // dil
    for r in range(dil):
        v = in_ref[r].astype(F32)
        if also_ref is not None:
            v = v + also_ref[r].astype(F32)
        for j in range(s_ref.shape[0]):
            s_ref[j, pl.ds(r, n, stride=dil), :] = v[:, j * 128:(j + 1) * 128]
    return _from_lane_blocks(s_ref)


def _class_spec(dil, tile=TS):
    return pl.BlockSpec((dil, tile // dil, 512), lambda i: (0, i, 0))


def _class_shape(S, dil, dtype):
    return jax.ShapeDtypeStruct((dil, S // dil, 512), dtype)


def _scan_tile(a_ref, b_ref, h_ref, carry_ref, reverse):
    n = a_ref.shape[0]
    width = a_ref.shape[1]
    groups = n // SUB
    row = lax.broadcasted_iota(jnp.int32, (SUB, width), 0)

    def group_scan(g):
        r0 = pl.multiple_of(g * SUB, SUB)
        a = a_ref[pl.ds(r0, SUB), :]
        b = b_ref[pl.ds(r0, SUB), :]
        for s in (1, 2, 4):
            if reverse:
                a_sh = pltpu.roll(a, SUB - s, 0)
                b_sh = pltpu.roll(b, SUB - s, 0)
                m = row < SUB - s
            else:
                a_sh = pltpu.roll(a, s, 0)
                b_sh = pltpu.roll(b, s, 0)
                m = row >= s
            b = jnp.where(m, a * b_sh + b, b)
            a = jnp.where(m, a * a_sh, a)
        return r0, a, b

    def step(i, carry):
        first = i * SCAN_UNROLL
        order = [(groups - 1 - (first + u)) if reverse else (first + u) for u in range(SCAN_UNROLL)]
        scans = [group_scan(g) for g in order]
        for r0, a, b in scans:
            h = b + a * carry
            h_ref[pl.ds(r0, SUB), :] = h
            edge = h[0:1, :] if reverse else h[SUB - 1:SUB, :]
            carry = jnp.broadcast_to(edge, (SUB, width))
        return carry

    carry_ref[...] = lax.fori_loop(0, groups // SCAN_UNROLL, step, carry_ref[...])


def _conv_fwd(xr, prev8, next8, cw, cb):
    y = cb + _shifted(xr, prev8, next8, -2) * cw[0:1, :]
    y = y + _shifted(xr, prev8, next8, -1) * cw[1:2, :]
    y = y + xr * cw[2:3, :]
    y = y + _shifted(xr, prev8, next8, 1) * cw[3:4, :]
    return y


def _lru_gates(xc, wa_ref, ba, wx_ref, bx, lam):
    xcb = xc.astype(BF16)
    r = _sigmoid(_dot(xcb, wa_ref[...]) + ba)
    i = _sigmoid(_dot(xcb, wx_ref[...]) + bx)
    cl = -LRU_C * _softplus(-lam)
    la = cl * r
    a = jnp.exp(la)
    m2 = -jnp.tanh(la) * (a * a + 1.0)
    inv = jnp.where(m2 > 0.0, lax.rsqrt(m2), 0.0)
    mult = m2 * inv
    return xcb, r, i, cl, a, mult, inv


def _inproj_fwd(x, g1, w_in, rider=None):
    S = x.shape[0]

    def body(x_ref, g_ref, w_ref, xr_ref, gate_ref, *rest):
        qkv_refs, s_ref, s4_ref = rest[:9], rest[9], rest[10]
        _, xh = _rms(x_ref[...])
        h = (xh * g_ref[...]).astype(BF16)
        proj = jnp.concatenate([_dot(h, w_ref[j]) for j in range(N_SHARD)], axis=-1)
        xr_ref[...] = proj[:, 0:512]
        gate_ref[...] = proj[:, 512:1024]
        for t in range(3):
            val = proj[:, 1024 + 512 * t:1536 + 512 * t]
            d1_ref, d4_ref, d16_ref = qkv_refs[3 * t:3 * t + 3]
            d1_ref[0] = val.astype(BF16)
            _to_lane_blocks(val, s_ref)
            for r4 in range(4):
                c4 = _class_rows(s_ref, r4, 4)
                d4_ref[r4] = c4.astype(BF16)
                _to_lane_blocks(c4, s4_ref.at[r4])
            for r4 in range(4):
                for m in range(4):
                    d16_ref[r4 + 4 * m] = _class_rows(s4_ref.at[r4], m, 4).astype(BF16)

    f = jax.ShapeDtypeStruct((S, 512), F32)
    return _call(
        body, "inproj_fwd", (S // TS,),
        [_rows(D_MODEL), _whole_vmem(), _whole_vmem()],
        [_rows(512)] * 2 + [_class_spec(d) for d in DILATIONS] * 3,
        [f, f] + [_class_shape(S, d, BF16) for d in DILATIONS] * 3,
        [pltpu.VMEM((4, TS, 128), F32), pltpu.VMEM((4, 4, TS // 4, 128), F32)], (x, g1, w_in), rider)


def _halo_specs(S, order, tile=TS):
    per = tile // SUB
    last = S // SUB - 1
    return [
        pl.BlockSpec((tile, 512), lambda i: (order(i), 0)),
        pl.BlockSpec((SUB, 512), lambda i: (jnp.maximum(order(i) * per - 1, 0), 0)),
        pl.BlockSpec((SUB, 512), lambda i: (jnp.minimum((order(i) + 1) * per, last), 0)),
    ]


def _rnn_fwd(xr, conv_w, conv_b, wa, ba, wx, bx, lam, reverse, rider=None):
    S = xr.shape[0]
    nt = S // TS
    order = (lambda i: nt - 1 - i) if reverse else (lambda i: i)

    def body(x_ref, xp_ref, xn_ref, cw_ref, cb_ref, wa_ref, ba_ref, wx_ref, bx_ref, lam_ref, h_ref, a_s, b_s, carry):
        i = pl.program_id(0)
        t = order(i)

        @pl.when(i == 0)
        def _():
            carry[...] = jnp.zeros_like(carry)

        prev8 = jnp.where(t > 0, xp_ref[...], 0.0)
        next8 = jnp.where(t < nt - 1, xn_ref[...], 0.0)
        xc = _conv_fwd(x_ref[...], prev8, next8, cw_ref[...], cb_ref[...])
        _, _, gi, _, a, mult, _ = _lru_gates(xc, wa_ref, ba_ref[...], wx_ref, bx_ref[...], lam_ref[...])
        a_s[...] = a
        b_s[...] = mult * (gi * xc)
        _scan_tile(a_s, b_s, h_ref, carry, reverse)

    (h,), carried = _call(
        body, "rnn_fwd_rev" if reverse else "rnn_fwd_fwd", (nt,),
        _halo_specs(S, order) + [_whole_vmem()] * 7,
        [pl.BlockSpec((TS, 512), lambda i: (order(i), 0))],
        [jax.ShapeDtypeStruct((S, 512), F32)],
        [pltpu.VMEM((TS, 512), F32), pltpu.VMEM((TS, 512), F32), pltpu.VMEM((SUB, 512), F32)],
        (xr, xr, xr, conv_w, conv_b, wa, ba, wx, bx, lam), rider)
    return h, carried


def _mix_fwd(o3, l3, hf, hb, gate, x, g_rnn, g_attn, w_out):
    S = x.shape[0]

    def body(o1, o2, o3_, l1, l2, l3_, hf_ref, hb_ref, gate_ref, x_ref, gr_ref, ga_ref, w_ref,
             x1_ref, mix_ref, ya1, ya2, ls1, ls2, s_ref):
        la, lb, lc = l1[0], _merge_classes(l2, s_ref, F32_LAYOUT), _merge_classes(l3_, s_ref, F32_LAYOUT)
        m = jnp.maximum(jnp.maximum(la, lb), lc)
        ea, eb, ec = jnp.exp(la - m), jnp.exp(lb - m), jnp.exp(lc - m)
        den = ea + eb + ec
        lse = m + jnp.log(den)
        ya = (ea * o1[0] + eb * _merge_classes(o2, s_ref, F32_LAYOUT) + ec * _merge_classes(o3_, s_ref, F32_LAYOUT)) / den
        ya1[0] = ya
        ls1[0] = lse
        _split_classes(ya, s_ref, ya2, F32_LAYOUT)
        _split_classes(lse, s_ref, ls2, F32_LAYOUT)
        gg, _ = _gelu_parts(gate_ref[...])
        yr = (hf_ref[...] + hb_ref[...]) * gg
        _, xh_r = _rms(yr)
        _, xh_a = _rms(ya)
        mix = jnp.concatenate([xh_r * gr_ref[...], xh_a * ga_ref[...]], axis=-1).astype(BF16)
        mix_ref[...] = mix
        acc = x_ref[...]
        for j in range(N_SHARD):
            acc = acc + _dot(mix[:, j * OUT_BLK:(j + 1) * OUT_BLK], w_ref[j])
        x1_ref[...] = acc

    one, four = _class_spec(1), _class_spec(F32_LAYOUT)
    return pl.pallas_call(
        body, grid=(S // TS,), name="mix_fwd",
        in_specs=[one, four, four] * 2 + [_rows(512)] * 3 + [_rows(D_MODEL)] + [_whole_vmem()] * 3,
        out_specs=[_rows(D_MODEL), _rows(D_MODEL)] + [one, four] * 2,
        out_shape=[jax.ShapeDtypeStruct((S, D_MODEL), F32), jax.ShapeDtypeStruct((S, D_MODEL), BF16)]
        + [_class_shape(S, 1, F32), _class_shape(S, F32_LAYOUT, F32)] * 2,
        scratch_shapes=[pltpu.VMEM((4, TS, 128), F32)],
        compiler_params=_params(),
    )(*o3, *l3, hf, hb, gate, x, g_rnn, g_attn, w_out)


def _mlp_fwd_bwd(x1, target, g_mlp, g_fin, w_up, w_down):
    S = x1.shape[0]
    tm = TS_MLP

    def body(x1_ref, t_ref, gm_ref, gf_ref, wu_ref, wd_ref,
             dx1_ref, h2_ref, a2_ref, du_ref, dx2_ref, loss_ref, dgf_ref, dgm_ref, relu_s):
        @pl.when(pl.program_id(0) == 0)
        def _():
            loss_ref[...] = jnp.zeros_like(loss_ref)
            dgf_ref[...] = jnp.zeros_like(dgf_ref)
            dgm_ref[...] = jnp.zeros_like(dgm_ref)

        x1v = x1_ref[...]
        rstd1, xh1 = _rms(x1v)
        h2 = (xh1 * gm_ref[...]).astype(BF16)
        h2_ref[...] = h2
        x2 = x1v
        for j in range(N_SHARD):
            r = jnp.maximum(_dot(h2, wu_ref[j]), 0.0)
            relu_s[j] = r
            a2 = (r * r).astype(BF16)
            a2_ref[:, j * FF_BLK:(j + 1) * FF_BLK] = a2
            x2 = x2 + _dot(a2, wd_ref[j])
        rstd2, xh2 = _rms(x2)
        err = xh2 * gf_ref[...] - t_ref[...]
        loss_ref[...] += jnp.sum(err * err, axis=0, keepdims=True)
        dy = err * (1.0 / D_MODEL)
        dx2, dgf = _rms_bwd(dy, gf_ref[...], xh2, rstd2)
        dgf_ref[...] += dgf
        dx2b = dx2.astype(BF16)
        dx2_ref[...] = dx2b
        dh2 = jnp.zeros((tm, D_MODEL), F32)
        for j in range(N_SHARD):
            du = (_dot_nt(dx2b, wd_ref[j]) * (2.0 * relu_s[j])).astype(BF16)
            du_ref[:, j * FF_BLK:(j + 1) * FF_BLK] = du
            dh2 = dh2 + _dot_nt(du, wu_ref[j])
        dx1n, dgm = _rms_bwd(dh2, gm_ref[...], xh1, rstd1)
        dgm_ref[...] += dgm
        dx1_ref[...] = dx2 + dx1n

    vec = jax.ShapeDtypeStruct((1, D_MODEL), F32)
    return pl.pallas_call(
        body, grid=(S // tm,), name="mlp_fwd_bwd",
        in_specs=[_rows(D_MODEL, tm), _rows(D_MODEL, tm)] + [_whole_vmem()] * 4,
        out_specs=[_rows(D_MODEL, tm), _rows(D_MODEL, tm), _rows(D_FF, tm), _rows(D_FF, tm), _rows(D_MODEL, tm)]
        + [_whole_vmem()] * 3,
        out_shape=[jax.ShapeDtypeStruct((S, D_MODEL), F32), jax.ShapeDtypeStruct((S, D_MODEL), BF16),
                   jax.ShapeDtypeStruct((S, D_FF), BF16), jax.ShapeDtypeStruct((S, D_FF), BF16),
                   jax.ShapeDtypeStruct((S, D_MODEL), BF16), vec, vec, vec],
        scratch_shapes=[pltpu.VMEM((N_SHARD, tm, FF_BLK), F32)],
        compiler_params=_params(),
    )(x1, target, g_mlp, g_fin, w_up, w_down)


def _mix_bwd(dx1, w_out, mixb, ya, hf, hb, gate, g_rnn, g_attn):
    S = dx1.shape[0]

    def body(dx1_ref, w_ref, mix_ref, ya_ref, hf_ref, hb_ref, gate_ref, gr_ref, ga_ref,
             dhs_ref, dgate_ref, dya1, dya2, dw_ref, dgr_ref, dga_ref, s_ref):
        @pl.when(pl.program_id(0) == 0)
        def _():
            dw_ref[...] = jnp.zeros_like(dw_ref)
            dgr_ref[...] = jnp.zeros_like(dgr_ref)
            dga_ref[...] = jnp.zeros_like(dga_ref)

        dx1b = dx1_ref[...].astype(BF16)
        mix = mix_ref[...]
        for j in range(N_SHARD):
            dw_ref[j] += _dot_tn(mix[:, j * OUT_BLK:(j + 1) * OUT_BLK], dx1b)
        dmix = jnp.concatenate([_dot_nt(dx1b, w_ref[j]) for j in range(N_SHARD)], axis=-1)
        gg, dgg = _gelu_parts(gate_ref[...])
        hs = hf_ref[...] + hb_ref[...]
        rstd_r, xh_r = _rms(hs * gg)
        dyr, dgr = _rms_bwd(dmix[:, 0:D_RNN], gr_ref[...], xh_r, rstd_r)
        dgr_ref[...] += dgr
        rstd_a, xh_a = _rms(ya_ref[0])
        dya, dga = _rms_bwd(dmix[:, D_RNN:], ga_ref[...], xh_a, rstd_a)
        dga_ref[...] += dga
        dya1[0] = dya
        _split_classes(dya, s_ref, dya2, F32_LAYOUT)
        dhs_ref[...] = dyr * gg
        dgate_ref[...] = dyr * hs * dgg

    f512 = jax.ShapeDtypeStruct((S, 512), F32)
    vec = jax.ShapeDtypeStruct((1, 512), F32)
    return pl.pallas_call(
        body, grid=(S // TS,), name="mix_bwd",
        in_specs=[_rows(D_MODEL), _whole_vmem(), _rows(D_MODEL), _class_spec(1)] + [_rows(512)] * 3 + [_whole_vmem()] * 2,
        out_specs=[_rows(512)] * 2 + [_class_spec(1), _class_spec(F32_LAYOUT)] + [_whole_vmem()] * 3,
        out_shape=[f512, f512, _class_shape(S, 1, F32), _class_shape(S, F32_LAYOUT, F32),
                   jax.ShapeDtypeStruct((N_SHARD, OUT_BLK, D_MODEL), F32), vec, vec],
        scratch_shapes=[pltpu.VMEM((4, TS, 128), F32)],
        compiler_params=_params(),
    )(dx1, w_out, mixb, ya, hf, hb, gate, g_rnn, g_attn)


def _rnn_bwd(xr, h, dhs, conv_w, conv_b, wa, ba, wx, bx, lam, reverse, rider=None):
    S = xr.shape[0]
    nt = S // TS
    order = (lambda i: i) if reverse else (lambda i: nt - 1 - i)
    per = TS // SUB
    last = S // SUB - 1
    if reverse:
        h_halo = pl.BlockSpec((SUB, 512), lambda i: (jnp.minimum((order(i) + 1) * per, last), 0))
    else:
        h_halo = pl.BlockSpec((SUB, 512), lambda i: (jnp.maximum(order(i) * per - 1, 0), 0))
    tile = pl.BlockSpec((TS, 512), lambda i: (order(i), 0))

    def body(x_ref, xp_ref, xn_ref, h_ref, hh_ref, dh_ref, cw_ref, cb_ref, wa_ref, ba_ref, wx_ref, bx_ref, lam_ref,
             dxc_ref, dwa_ref, dwx_ref, dvec_ref, a_s, g_s, carry, edge):
        i = pl.program_id(0)
        t = order(i)

        @pl.when(i == 0)
        def _():
            carry[...] = jnp.zeros_like(carry)
            edge[...] = jnp.zeros_like(edge)
            dwa_ref[...] = jnp.zeros_like(dwa_ref)
            dwx_ref[...] = jnp.zeros_like(dwx_ref)
            dvec_ref[...] = jnp.zeros_like(dvec_ref)

        prev8 = jnp.where(t > 0, xp_ref[...], 0.0)
        next8 = jnp.where(t < nt - 1, xn_ref[...], 0.0)
        xc = _conv_fwd(x_ref[...], prev8, next8, cw_ref[...], cb_ref[...])
        xcb, r, gi, cl, a, mult, inv_mult = _lru_gates(xc, wa_ref, ba_ref[...], wx_ref, bx_ref[...], lam_ref[...])
        hv = h_ref[...]
        if reverse:
            a_s[...] = _shifted(a, edge[...], None, -1)
            edge[...] = a[TS - SUB:TS, :]
            hh = jnp.where(t < nt - 1, hh_ref[...], 0.0)
            h_prev = _shifted(hv, None, hh, 1)
        else:
            a_s[...] = _shifted(a, None, edge[...], 1)
            edge[...] = a[0:SUB, :]
            hh = jnp.where(t > 0, hh_ref[...], 0.0)
            h_prev = _shifted(hv, hh, None, -1)
        _scan_tile(a_s, dh_ref, g_s, carry, not reverse)
        g = g_s[...]
        da = g * h_prev
        gm = g * mult
        d_i = gm * xc
        dmult = g * gi * xc
        dla = da * a - dmult * (a * a) * inv_mult
        d_r = dla * cl
        dpre_r = d_r * r * (1.0 - r)
        dpre_i = d_i * gi * (1.0 - gi)
        dprb = dpre_r.astype(BF16)
        dpib = dpre_i.astype(BF16)
        dwa_ref[...] += _dot_tn(xcb, dprb)
        dwx_ref[...] += _dot_tn(xcb, dpib)
        dvec_ref[0:1, :] += jnp.sum(dpre_r, axis=0, keepdims=True)
        dvec_ref[1:2, :] += jnp.sum(dpre_i, axis=0, keepdims=True)
        dvec_ref[2:3, :] += jnp.sum(dla * r, axis=0, keepdims=True)
        dvec_ref[3:4, :] = dvec_ref[2:3, :] * (LRU_C * _sigmoid(-lam_ref[...]))
        dxc_ref[...] = gm * gi + _dot_nt(dprb, wa_ref[...]) + _dot_nt(dpib, wx_ref[...])

    sq = jax.ShapeDtypeStruct((D_RNN, D_RNN), F32)
    return _call(
        body, "rnn_bwd_rev" if reverse else "rnn_bwd_fwd", (nt,),
        _halo_specs(S, order) + [tile, h_halo, tile] + [_whole_vmem()] * 7,
        [tile, _whole_vmem(), _whole_vmem(), _whole_vmem()],
        [jax.ShapeDtypeStruct((S, 512), F32), sq, sq, jax.ShapeDtypeStruct((SUB, 512), F32)],
        [pltpu.VMEM((TS, 512), F32), pltpu.VMEM((TS, 512), F32), pltpu.VMEM((SUB, 512), F32),
         pltpu.VMEM((SUB, 512), F32)],
        (xr, xr, xr, h, h, dhs, conv_w, conv_b, wa, ba, wx, bx, lam), rider)


def _inproj_bwd(x, dx1, xr, dxc_f, dxc_b, dgate, dq3, dk3, dv3, g1, conv_w, w_in, rider=None):
    S = x.shape[0]
    tb = TS_INPROJ_BWD
    nt = S // tb
    ident = lambda i: i

    def body(x_ref, dx1_ref, xr_ref, xrp_ref, xrn_ref, cf_ref, cfp_ref, cfn_ref, cb_ref, cbp_ref, cbn_ref, dgate_ref,
             dq1, dq2, dq3_, dk1, dk2, dk3_, dv1, dv2, dv3_, g_ref, cw_ref, w_ref,
             dx_ref, dw_ref, dg_ref, dcw_ref, s_ref):
        i = pl.program_id(0)

        @pl.when(i == 0)
        def _():
            dw_ref[...] = jnp.zeros_like(dw_ref)
            dg_ref[...] = jnp.zeros_like(dg_ref)
            dcw_ref[...] = jnp.zeros_like(dcw_ref)

        first, last = i > 0, i < nt - 1
        dxc = cf_ref[...] + cb_ref[...]
        dxc_p = jnp.where(first, cfp_ref[...] + cbp_ref[...], 0.0)
        dxc_n = jnp.where(last, cfn_ref[...] + cbn_ref[...], 0.0)
        cw = cw_ref[...]
        dxr = (_shifted(dxc, dxc_p, dxc_n, 2) * cw[0:1, :] + _shifted(dxc, dxc_p, dxc_n, 1) * cw[1:2, :]
               + dxc * cw[2:3, :] + _shifted(dxc, dxc_p, dxc_n, -1) * cw[3:4, :])
        xrv = xr_ref[...]
        xr_p = jnp.where(first, xrp_ref[...], 0.0)
        xr_n = jnp.where(last, xrn_ref[...], 0.0)
        for k, off in enumerate((-2, -1, 0, 1)):
            dcw_ref[k:k + 1, :] += jnp.sum(dxc * _shifted(xrv, xr_p, xr_n, off), axis=0, keepdims=True)
        dcw_ref[4:5, :] += jnp.sum(dxc, axis=0, keepdims=True)

        def total(a, b, c_):
            return a[0].astype(F32) + _merge_classes(b, s_ref, F32_LAYOUT, c_)

        dproj = jnp.concatenate(
            [dxr, dgate_ref[...], total(dq1, dq2, dq3_), total(dk1, dk2, dk3_), total(dv1, dv2, dv3_)],
            axis=-1).astype(BF16)
        xv = x_ref[...]
        rstd, xh = _rms(xv)
        hb = (xh * g_ref[...]).astype(BF16)
        dh = jnp.zeros((tb, D_MODEL), F32)
        for j in range(N_SHARD):
            dpj = dproj[:, j * IN_BLK:(j + 1) * IN_BLK]
            dh = dh + _dot_nt(dpj, w_ref[j])
            dw_ref[j] += _dot_tn(hb, dpj)
        dxn, dg = _rms_bwd(dh, g_ref[...], xh, rstd)
        dg_ref[...] += dg
        dx_ref[...] = dx1_ref[...] + dxn

    halo = _halo_specs(S, ident, tb)
    return _call(
        body, "inproj_bwd", (nt,),
        [_rows(D_MODEL, tb), _rows(D_MODEL, tb)] + halo * 3 + [_rows(512, tb)]
        + [_class_spec(1, tb), _class_spec(F32_LAYOUT, tb), _class_spec(F32_LAYOUT, tb)] * 3 + [_whole_vmem()] * 3,
        [_rows(D_MODEL, tb), _whole_vmem(), _whole_vmem(), _whole_vmem()],
        [jax.ShapeDtypeStruct((S, D_MODEL), F32), jax.ShapeDtypeStruct((N_SHARD, D_MODEL, IN_BLK), F32),
         jax.ShapeDtypeStruct((1, D_MODEL), F32), jax.ShapeDtypeStruct((SUB, 512), F32)],
        [pltpu.VMEM((4, tb, 128), F32)],
        (x, dx1, xr, xr, xr, dxc_f, dxc_f, dxc_f, dxc_b, dxc_b, dxc_b, dgate, *dq3, *dk3, *dv3, g1, conv_w, w_in), rider)


def _dw_matmul(a, b, a_cols, b_cols, name):
    S = a.shape[0]
    tk = min(S, TK_DW)
    a_shared = a.shape[1] == a_cols
    b_shared = b.shape[1] == b_cols

    def body(a_ref, b_ref, o_ref):
        @pl.when(pl.program_id(1) == 0)
        def _():
            o_ref[...] = jnp.zeros_like(o_ref)
        o_ref[0] += _dot_tn(a_ref[...], b_ref[...])

    return pl.pallas_call(
        body, grid=(N_SHARD, S // tk), name=name,
        in_specs=[pl.BlockSpec((tk, a_cols), (lambda j, k: (k, 0)) if a_shared else (lambda j, k: (k, j))),
                  pl.BlockSpec((tk, b_cols), (lambda j, k: (k, 0)) if b_shared else (lambda j, k: (k, j)))],
        out_specs=pl.BlockSpec((1, a_cols, b_cols), lambda j, k: (j, 0, 0)),
        out_shape=jax.ShapeDtypeStruct((N_SHARD, a_cols, b_cols), F32),
        compiler_params=_params(2),
    )(a, b)


def _t5_bucket_np(rel):
    nb = N_BUCKETS // 2
    max_exact = nb // 2
    ret = np.where(rel > 0, nb, 0)
    n = np.abs(rel)
    nf = np.maximum(n, 1).astype(np.float32)
    large = max_exact + (np.log(nf / np.float32(max_exact)) / np.float32(math.log(MAX_DISTANCE / max_exact))
                         * np.float32(nb - max_exact)).astype(np.int32)
    large = np.minimum(large, nb - 1)
    return ret + np.where(n < max_exact, n, large)


_VARIANT_OFFSETS = (-HALF_WIN,) * 3


def _band_index():
    kk = np.arange(K_WIN)[None, :]
    ql = np.arange(Q_BLK)[:, None]
    rel = np.stack([kk - ql + off for off in _VARIANT_OFFSETS])
    band = np.abs(rel) <= HALF_WIN
    inside = np.stack([np.broadcast_to(kk >= HALF_WIN, band[0].shape), np.ones_like(band[0]),
                       np.broadcast_to(kk < K_WIN - HALF_WIN, band[0].shape)])
    return rel, band & inside


def _bucket_tables(dil):
    rel, valid = _band_index()
    bucket = _t5_bucket_np(np.clip(rel, -HALF_WIN, HALF_WIN) * dil)
    return np.where(valid, bucket, -1).astype(np.int32)


def _bias_mats(rel_bias, rider=None):
    tables = [_bucket_tables(d) for d in DILATIONS]
    used = [sorted(set(t[t >= 0].tolist())) for t in tables]

    def one_pattern(rb_ref, t_ref, o_ref, buckets):
        bk = t_ref[1]
        for h in range(N_HEADS):
            acc = jnp.full((Q_BLK, K_WIN), NEG_INF, F32)
            for b in buckets:
                acc = jnp.where(bk == b, rb_ref[b, h], acc)
            o_ref[1, h] = acc
            for var in (0, 2):
                o_ref[var, h] = jnp.where(t_ref[var] >= 0, acc, NEG_INF)

    def body(rb_ref, t1, t2, t3, o1, o2, o3):
        for i, (t_ref, o_ref) in enumerate(((t1, o1), (t2, o2), (t3, o3))):
            pl.when(pl.program_id(0) == i)(functools.partial(one_pattern, rb_ref, t_ref, o_ref, used[i]))

    shp = jax.ShapeDtypeStruct((3, N_HEADS, Q_BLK, K_WIN), F32)
    return _call(
        body, "bias_tables", (len(DILATIONS),), [pl.BlockSpec(memory_space=pltpu.SMEM)] + [_whole_vmem()] * 3,
        [_whole_vmem()] * 3, [shp] * 3, [], (rel_bias, *[jnp.asarray(t) for t in tables]), rider)


def _variant(qb, nq):
    return jnp.where(qb == 0, 0, jnp.where(qb == nq - 1, 2, 1))


def _win_start(qb):
    return pl.multiple_of(qb * Q_BLK, Q_BLK)


def _fill_padded(src_ref, pad_ref):
    L = src_ref.shape[0]
    edge = jnp.zeros((HALF_WIN, 128), pad_ref.dtype)
    pad_ref[0:HALF_WIN, :] = edge
    pad_ref[HALF_WIN:HALF_WIN + L, :] = src_ref[...]
    pad_ref[HALF_WIN + L:2 * HALF_WIN + L, :] = edge


INNER = {1: 1, 4: 1, 16: 4}


def _attn_layout(dil, L):
    inner = INNER[dil]
    n_outer = dil // inner
    nsub = min(ATTN_SUB // inner, L // Q_BLK)
    qt = nsub * Q_BLK
    grid = (4, n_outer, L // qt)
    qspec = pl.BlockSpec((inner, None, qt, 128), lambda hp, r, s: (0, r, s, hp))
    kspec = pl.BlockSpec((inner, None, L, 128), lambda hp, r, s: (0, r, 0, hp))
    bspec = pl.BlockSpec((3, 2, Q_BLK, K_WIN), lambda hp, r, s: (0, hp, 0, 0))
    kfspec = pl.BlockSpec((None, inner * L, 128), lambda hp, r, s: (r, 0, hp))
    qfspec = kfspec if inner > 1 else pl.BlockSpec((None, qt, 128), lambda hp, r, s: (r, s, hp))
    fshape = jax.ShapeDtypeStruct((n_outer, inner * L, D_ATTN), F32)
    view = lambda t: t.reshape(inner, n_outer, L, D_ATTN)

    def qrows(m, sub):
        if inner == 1:
            return (slice(sub * Q_BLK, (sub + 1) * Q_BLK), slice(None))
        first = (pl.program_id(2) * nsub + sub) * Q_BLK
        return (pl.ds(m + inner * first, Q_BLK, stride=inner), slice(None))

    def krows(m):
        if inner == 1:
            return (slice(None), slice(None))
        return (pl.ds(m, L, stride=inner), slice(None))

    return inner, nsub, grid, qspec, kspec, bspec, qfspec, kfspec, fshape, view, qrows, krows


def _head_masks():
    lane = lax.broadcasted_iota(jnp.int32, (Q_BLK, 128), 1)
    return lane < HEAD_DIM


def _attn_fwd(q, k, v, bias):
    dil, L, _ = q.shape
    nq = L // Q_BLK
    inner, nsub, grid, qspec, kspec, bspec, qfspec, kfspec, fshape, view, qrows, krows = _attn_layout(dil, L)

    def body(q_ref, k_ref, v_ref, b_ref, o_ref, l_ref, kp, vp):
        step = pl.program_id(2)

        @pl.when(step == 0)
        def _():
            for m in range(inner):
                _fill_padded(k_ref.at[m], kp.at[m])
                _fill_padded(v_ref.at[m], vp.at[m])

        h0 = _head_masks()
        for m, sub in [(m, sub) for m in range(inner) for sub in range(nsub)]:
            qb = step * nsub + sub
            st = _win_start(qb)
            var = _variant(qb, nq)
            kw = kp[m, pl.ds(st, K_WIN), :]
            vw = vp[m, pl.ds(st, K_WIN), :]
            qs = q_ref[m, sub * Q_BLK:(sub + 1) * Q_BLK, :] * ATTN_SCALE
            zq = jnp.zeros_like(qs)
            q2 = jnp.concatenate([jnp.where(h0, qs, zq), jnp.where(h0, zq, qs)], axis=0)
            s = _dot_nt(q2, kw) + b_ref[var].reshape(2 * Q_BLK, K_WIN)
            top = jnp.max(s, axis=-1, keepdims=True)
            p = jnp.exp(s - top)
            l = jnp.sum(p, axis=-1, keepdims=True)
            out = _dot(p.astype(BF16), vw) / l
            lse = top + jnp.log(l)
            o_ref[qrows(m, sub)] = jnp.where(h0, out[0:Q_BLK], out[Q_BLK:2 * Q_BLK])
            l_ref[qrows(m, sub)] = jnp.where(h0, lse[0:Q_BLK], lse[Q_BLK:2 * Q_BLK])

    return pl.pallas_call(
        body, grid=grid, name=f"attn_fwd_d{dil}",
        in_specs=[qspec, kspec, kspec, bspec], out_specs=[qfspec, qfspec], out_shape=[fshape, fshape],
        scratch_shapes=[pltpu.VMEM((inner, L + 2 * HALF_WIN, 128), BF16)] * 2,
        compiler_params=_params(3),
    )(view(q), view(k), view(v), bias)


def _attn_bwd(q, k, v, bias, do, o, lse, rider=None):
    dil, L, _ = q.shape
    nq = L // Q_BLK
    inner, nsub, grid, qspec, kspec, bspec, qfspec, kfspec, fshape, view, qrows, krows = _attn_layout(dil, L)
    nstep = grid[2]

    def body(q_ref, k_ref, v_ref, b_ref, do_ref, o_ref, l_ref, dq_ref, dk_ref, dv_ref, db_ref, db_s,
             kp, vp, dkp, dvp, carry):
        hp, step = pl.program_id(0), pl.program_id(2)
        first = (hp == 0) & (pl.program_id(1) == 0) & (step == 0)
        last = (hp == grid[0] - 1) & (pl.program_id(1) == grid[1] - 1) & (step == nstep - 1)

        @pl.when(first)
        def _():
            db_s[...] = jnp.zeros_like(db_s)

        @pl.when(step == 0)
        def _():
            for m in range(inner):
                _fill_padded(k_ref.at[m], kp.at[m])
                _fill_padded(v_ref.at[m], vp.at[m])
            carry[...] = jnp.zeros_like(carry)

        h0 = _head_masks()
        for m, sub in [(m, sub) for m in range(inner) for sub in range(nsub)]:
            if sub == 0:
                carry_k, carry_v = carry[m, 0], carry[m, 1]
            qb = step * nsub + sub
            st = _win_start(qb)
            var = _variant(qb, nq)
            kw = kp[m, pl.ds(st, K_WIN), :]
            vw = vp[m, pl.ds(st, K_WIN), :]
            qs = q_ref[m, sub * Q_BLK:(sub + 1) * Q_BLK, :] * ATTN_SCALE
            dof = do_ref[qrows(m, sub)]
            dob = dof.astype(BF16)
            prod = dof * o_ref[qrows(m, sub)]
            lsev = l_ref[qrows(m, sub)]
            zq, zd = jnp.zeros_like(qs), jnp.zeros_like(dob)
            q2 = jnp.concatenate([jnp.where(h0, qs, zq), jnp.where(h0, zq, qs)], axis=0)
            do2 = jnp.concatenate([jnp.where(h0, dob, zd), jnp.where(h0, zd, dob)], axis=0)
            lse2 = jnp.concatenate([lsev[:, 0:1], lsev[:, HEAD_DIM:HEAD_DIM + 1]], axis=0)
            dd2 = jnp.concatenate([jnp.sum(jnp.where(h0, prod, 0.0), axis=-1, keepdims=True),
                                   jnp.sum(jnp.where(h0, 0.0, prod), axis=-1, keepdims=True)], axis=0)
            s = _dot_nt(q2, kw) + b_ref[var].reshape(2 * Q_BLK, K_WIN)
            p = jnp.exp(s - lse2)
            ds = p * (_dot_nt(do2, vw) - dd2)
            db_s[pl.ds(hp * 2, 2)] += ds.reshape(2, Q_BLK, K_WIN)
            dsb = ds.astype(BF16)
            dv_acc = _dot_tn(p.astype(BF16), do2)
            dk_acc = _dot_tn(dsb, q2)
            dq2 = _dot(dsb, kw) * ATTN_SCALE
            dq_ref[qrows(m, sub)] = jnp.where(h0, dq2[0:Q_BLK], dq2[Q_BLK:2 * Q_BLK]).astype(dq_ref.dtype)
            dkp[m, pl.ds(st, Q_BLK), :] = carry_k + dk_acc[0:Q_BLK]
            dvp[m, pl.ds(st, Q_BLK), :] = carry_v + dv_acc[0:Q_BLK]
            carry_k, carry_v = dk_acc[Q_BLK:K_WIN], dv_acc[Q_BLK:K_WIN]
            if sub == nsub - 1:
                carry[m, 0] = carry_k
                carry[m, 1] = carry_v

        @pl.when(step == nstep - 1)
        def _():
            for m in range(inner):
                dkp[m, L:L + Q_BLK, :] = carry[m, 0]
                dvp[m, L:L + Q_BLK, :] = carry[m, 1]
                dk_ref[krows(m)] = dkp[m, HALF_WIN:HALF_WIN + L, :].astype(dk_ref.dtype)
                dv_ref[krows(m)] = dvp[m, HALF_WIN:HALF_WIN + L, :].astype(dv_ref.dtype)

        @pl.when(last)
        def _():
            db_ref[...] = db_s[...]

    dbshape = (N_HEADS, Q_BLK, K_WIN)
    gshape = jax.ShapeDtypeStruct(fshape.shape, BF16 if inner == 1 else F32)
    return _call(
        body, f"attn_bwd_d{dil}", grid,
        [qspec, kspec, kspec, bspec, qfspec, qfspec, qfspec],
        [qfspec, kfspec, kfspec, _whole_vmem()],
        [gshape, gshape, gshape, jax.ShapeDtypeStruct(dbshape, F32)],
        [pltpu.VMEM(dbshape, F32)] + [pltpu.VMEM((inner, L + 2 * HALF_WIN, 128), BF16)] * 2
        + [pltpu.VMEM((inner, L + 2 * HALF_WIN, 128), F32)] * 2 + [pltpu.VMEM((inner, 2, Q_BLK, 128), F32)],
        (view(q), view(k), view(v), bias, do, o, lse), rider)


def _bucket_onehots(dil):
    m = np.zeros((3, K_WIN, N_BUCKETS), np.float32)
    for var, off in enumerate(_VARIANT_OFFSETS):
        for rel in range(-HALF_WIN, HALF_WIN + 1):
            col = (rel - off + Q_BLK - 1) % K_WIN
            m[var, col, int(_t5_bucket_np(np.asarray(rel * dil)))] = 1.0
    return jnp.asarray(m)


def _bias_grad(dbs):
    onehots = [_bucket_onehots(d) for d in DILATIONS]
    flip = jnp.asarray(np.eye(Q_BLK, dtype=np.float32)[::-1].copy())

    def body(d1, d2, d3, m1, m2, m3, flip_ref, out_ref):
        hp = lax.Precision.HIGHEST
        acc = jnp.zeros((N_HEADS, N_BUCKETS), F32)
        for d_ref, m_ref in ((d1, m1), (d2, m2), (d3, m3)):
            rows = []
            for h in range(N_HEADS):
                xrev = jnp.dot(flip_ref[...], d_ref[h], precision=hp, preferred_element_type=F32)
                y = pltpu.roll(xrev, 0, 1, stride=1, stride_axis=0)
                rows.append(jnp.sum(y, axis=0, keepdims=True))
            acc = acc + jnp.dot(jnp.concatenate(rows, axis=0), m_ref[1], precision=hp, preferred_element_type=F32)
        out_ref[...] = acc

    return pl.pallas_call(
        body, name="bias_grad", out_shape=jax.ShapeDtypeStruct((N_HEADS, N_BUCKETS), F32),
        compiler_params=_params(0),
    )(*dbs, *onehots, flip)


def _block_diag(w):
    eye = jnp.eye(N_RNN_BLOCKS, dtype=w.dtype)
    return jnp.einsum("ncd,nm->ncmd", w, eye).reshape(D_RNN, D_RNN).astype(BF16)


def _diag_blocks(dense):
    d = dense.reshape(N_RNN_BLOCKS, RNN_BLOCK, N_RNN_BLOCKS, RNN_BLOCK)
    return jnp.stack([d[n, :, n, :] for n in range(N_RNN_BLOCKS)])


EARLY = ("w_out", "w_up", "w_down")


def _local_step(x, target, p, shards=None):
    p = dict(p)
    first = None if shards is None else _gather_rider(["w_in"], [shards["w_in"]], shards["conv_w"])
    biases, got = _bias_mats(p["rel_bias"], first)
    if shards is not None:
        p["w_in"] = got[0]
        p["conv_w"] = jnp.transpose(got[1], (1, 0, 2)).reshape(4, D_RNN)
    lru = {}
    for dname in ("fwd", "bwd"):
        lru[dname] = (_block_diag(p["lru_wa_" + dname]), p["lru_ba_" + dname], _block_diag(p["lru_wx_" + dname]),
                      p["lru_bx_" + dname], p["lru_lam_" + dname])

    def gather(name):
        return None if shards is None else _gather_rider([name], [shards[name]])

    (xr, gate, *qkv), got = _inproj_fwd(x, p["attn_norm_g"], p["w_in"], gather("w_out"))
    p.update(zip(["w_out"], got))
    qs, ks, vs = qkv[0:3], qkv[3:6], qkv[6:9]
    hf, got = _rnn_fwd(xr, p["conv_w"], p["conv_b"], *lru["fwd"], reverse=False, rider=gather("w_up"))
    p.update(zip(["w_up"], got))
    hb, got = _rnn_fwd(xr, p["conv_w"], p["conv_b"], *lru["bwd"], reverse=True, rider=gather("w_down"))
    p.update(zip(["w_down"], got))
    outs, lses = [], []
    for q, k, v, bias in zip(qs, ks, vs, biases):
        o, l = _attn_fwd(q, k, v, bias)
        outs.append(o)
        lses.append(l)
    x1, mixb, *yl = _mix_fwd(outs, lses, hf, hb, gate, x, p["norm_rnn_g"], p["norm_attn_g"], p["w_out"])
    yas, lsts = [yl[0], yl[1], yl[1]], [yl[2], yl[3], yl[3]]
    dx1, h2b, a2b, dub, dx2b, loss_vec, dg_fin, dg_mlp = _mlp_fwd_bwd(
        x1, target, p["mlp_norm_g"], p["final_norm_g"], p["w_up"], p["w_down"])
    dhs, dgate, dya1, dya4, dw_out, dg_rnn, dg_attn = _mix_bwd(dx1, p["w_out"], mixb, yas[0], hf, hb, gate,
                                                               p["norm_rnn_g"], p["norm_attn_g"])
    dyas = [dya1, dya4, dya4]
    dw_up = _dw_matmul(h2b, dub, D_MODEL, FF_BLK, "dw_up")
    dw_down = _dw_matmul(a2b, dx2b, FF_BLK, D_MODEL, "dw_down")
    early = [dw_out, dw_up, dw_down]
    dqs, dks, dvs, dbs = [], [], [], []
    for i, (q, k, v, bias, dya, ya, lse) in enumerate(zip(qs, ks, vs, biases, dyas, yas, lsts)):
        rider = None
        if shards is not None:
            make = (lambda: _pair_exchange_rider(EARLY, early), lambda: _chip_exchange_rider(early),
                    lambda: _pair_share_rider(EARLY, early))[i]
            rider = make()
        (dq, dk, dv, db), got = _attn_bwd(q, k, v, bias, dya, ya, lse, rider)
        if shards is not None and i == 0:
            core = lax.axis_index("c").reshape(1).astype(jnp.int32)
            early = [_pair_add(core, g, o, "grad_pair_add_" + n) for n, g, o in zip(EARLY, early, got)]
        elif shards is not None and i == 1:
            early = [_chip_sum(t, "grad_chip_sum_" + n) for n, t in zip(EARLY, got)]
        elif shards is not None:
            early = got
        dqs.append(dq)
        dks.append(dk)
        dvs.append(dv)
        dbs.append(db)
    d_rel_bias = _bias_grad(dbs).T
    (dxc_f, dwa_f, dwx_f, dvec_f), _ = _rnn_bwd(xr, hf, dhs, p["conv_w"], p["conv_b"], *lru["fwd"], reverse=False)
    small = {
        "lru_wa_fwd": _diag_blocks(dwa_f), "lru_ba_fwd": dvec_f[0:1], "lru_wx_fwd": _diag_blocks(dwx_f),
        "lru_bx_fwd": dvec_f[1:2], "lru_lam_fwd": dvec_f[3:4],
        "rel_bias": d_rel_bias, "norm_rnn_g": dg_rnn, "norm_attn_g": dg_attn,
        "mlp_norm_g": dg_mlp, "final_norm_g": dg_fin,
    }
    loss_local = (0.5 / D_MODEL) * jnp.sum(loss_vec)
    rider = None
    if shards is not None:
        rider = _small_gather_rider(_pack([small[n].reshape(shp) for n, shp in SMALL if n in small]
                                          + [loss_local.reshape(1)]))
    (dxc_b, dwa_b, dwx_b, dvec_b), gathered = _rnn_bwd(xr, hb, dhs, p["conv_w"], p["conv_b"], *lru["bwd"], reverse=True,
                                                       rider=rider)
    grad_x, dw_in, dg1, dconv = _inproj_bwd(x, dx1, xr, dxc_f, dxc_b, dgate, dqs, dks, dvs,
                                            p["attn_norm_g"], p["conv_w"], p["w_in"])[0]
    last = {"lru_wa_bwd": _diag_blocks(dwa_b), "lru_ba_bwd": dvec_b[0:1], "lru_wx_bwd": _diag_blocks(dwx_b),
            "lru_bx_bwd": dvec_b[1:2], "lru_lam_bwd": dvec_b[3:4],
            "attn_norm_g": dg1, "conv_w": dconv[0:4], "conv_b": dconv[4:5]}
    if shards is None:
        big = {"w_in": dw_in, "w_out": dw_out, "w_up": dw_up, "w_down": dw_down}
        return loss_local, grad_x, {**small, **last}, None, big, {}
    return loss_local, grad_x, last, gathered[0], {"w_in": dw_in}, dict(zip(EARLY, early))


BIG = ("w_in", "w_out", "w_up", "w_down")
BIG_SHARD = {"w_in": (D_MODEL, IN_BLK), "w_out": (OUT_BLK, D_MODEL), "w_up": (D_MODEL, FF_BLK), "w_down": (FF_BLK, D_MODEL)}
N_BIG = len(BIG)
N_CHIP_PEERS = 3
ANY = pl.BlockSpec(memory_space=pl.ANY)


def _place():
    x, y, c = lax.axis_index("x"), lax.axis_index("y"), lax.axis_index("c")
    chips = [(1 - x, y), (x, 1 - y), (1 - x, 1 - y)]
    return x, y, c, chips


def _remote(src, dst, send_sem, recv_sem, dev):
    return pltpu.make_async_remote_copy(src_ref=src, dst_ref=dst, send_sem=send_sem, recv_sem=recv_sem,
                                        device_id=dev, device_id_type=MESH)


def _staged_start(srcs, bufs, sems):
    legs = [pltpu.make_async_copy(s, b, sems.at[i]) for i, (s, b) in enumerate(zip(srcs, bufs))]
    for cp in legs:
        cp.start()
    return legs


def _staged_finish(legs, bufs, dsts, sems):
    out = []
    for i, (leg, b, d) in enumerate(zip(legs, bufs, dsts)):
        leg.wait()
        cp = pltpu.make_async_copy(b, d, sems.at[i])
        cp.start()
        out.append(cp)
    return out


class _Rider:
    def __init__(self, inputs, out_shape, scratch, first, late, last):
        self.inputs, self.out_shape, self.scratch = list(inputs), list(out_shape), list(scratch)
        self.first, self.late, self.last = first, late, last


def _call(body, name, grid, in_specs, out_specs, out_shape, scratch, operands, rider=None):
    n_grid = len(grid)
    if rider is None:
        res = pl.pallas_call(body, grid=grid, name=name, in_specs=in_specs, out_specs=out_specs, out_shape=out_shape,
                             scratch_shapes=scratch, compiler_params=_params(n_grid))(*operands)
        return list(res), []
    n_in, n_out, n_scr = len(in_specs), len(out_specs), len(scratch)
    ri, ro = len(rider.inputs), len(rider.out_shape)
    nsteps = int(np.prod(grid))
    late_step = max(nsteps - 3, 1)

    def wrapped(*refs):
        a, b = n_in, n_in + ri
        c, d = b + n_out, b + n_out + ro
        e = d + n_scr
        mine = refs[:a] + refs[b:c] + refs[d:e]
        theirs = (refs[a:b], refs[c:d], refs[e:])
        step = pl.program_id(0)
        for ax in range(1, n_grid):
            step = step * grid[ax] + pl.program_id(ax)
        pl.when(step == 0)(lambda: rider.first(*theirs))
        pl.when(step == late_step)(lambda: rider.late(*theirs))
        body(*mine)
        pl.when(step == nsteps - 1)(lambda: rider.last(*theirs))

    res = pl.pallas_call(
        wrapped, grid=grid, name=name, in_specs=list(in_specs) + [ANY] * ri, out_specs=list(out_specs) + [ANY] * ro,
        out_shape=list(out_shape) + rider.out_shape, scratch_shapes=list(scratch) + rider.scratch,
        compiler_params=_params(n_grid),
    )(*operands, *rider.inputs)
    return list(res[:n_out]), list(res[n_out:])


def _run_rider(rider, name):
    ri, ro = len(rider.inputs), len(rider.out_shape)

    def body(*refs):
        parts = (refs[:ri], refs[ri:ri + ro], refs[ri + ro:])
        rider.first(*parts)
        rider.late(*parts)
        rider.last(*parts)

    return list(pl.pallas_call(
        body, name=name, in_specs=[ANY] * ri, out_specs=[ANY] * ro, out_shape=rider.out_shape, scratch_shapes=rider.scratch,
        compiler_params=pltpu.CompilerParams(has_side_effects=True, vmem_limit_bytes=VMEM_LIMIT),
    )(*rider.inputs))


def _nothing(ins, outs, scr):
    return None


def _gather_rider(names, shards, conv_w=None):
    n = len(names)
    items = n + (conv_w is not None)
    halves = [BIG_SHARD[nm][0] // 2 for nm in names]

    def parts(ins, outs, scr):
        x, y, c, chips = _place()
        return x, y, c, chips, 2 * x + y, (x, y, 1 - c), scr[:8], scr[8:]

    def piece(outs, w, chip, core_half):
        return outs[w].at[chip, pl.ds(core_half * halves[w], halves[w])]

    def ici(ins, outs, sems, w, k, chip_xy, c, me):
        return _remote(ins[w].at[pl.ds(c * halves[w], halves[w])], piece(outs, w, me, c),
                       sems[0].at[w, k], sems[1].at[w, k], (*chip_xy, c))

    def first(ins, outs, scr):
        x, y, c, chips, me, sibling, sems, bufs = parts(ins, outs, scr)
        legs = _staged_start(ins, bufs, sems[6])
        for w in range(n):
            for k, chip_xy in enumerate(chips):
                ici(ins, outs, sems, w, k, chip_xy, c, me).start()
        if conv_w is not None:
            for k, (px, py) in enumerate(chips):
                _remote(ins[n], outs[n].at[me], sems[4].at[k], sems[5].at[k], (px, py, c)).start()
        _staged_finish(legs, bufs, [o.at[me] for o in outs], sems[7])

    def late(ins, outs, scr):
        x, y, c, chips, me, sibling, sems, bufs = parts(ins, outs, scr)
        for w in range(n):
            for k, (px, py) in enumerate(chips):
                landed = piece(outs, w, 2 * px + py, c)
                _remote(landed, landed, sems[0].at[w, k], sems[1].at[w, k], (px, py, c)).wait_recv()
                _remote(landed, landed, sems[2].at[w, k], sems[3].at[w, k], sibling).start()

    def last(ins, outs, scr):
        x, y, c, chips, me, sibling, sems, bufs = parts(ins, outs, scr)
        for w in range(n):
            for k, (px, py) in enumerate(chips):
                other = piece(outs, w, 2 * px + py, 1 - c)
                _remote(other, other, sems[2].at[w, k], sems[3].at[w, k], sibling).wait_recv()
        if conv_w is not None:
            for k, (px, py) in enumerate(chips):
                got = outs[n].at[2 * px + py]
                _remote(got, got, sems[4].at[k], sems[5].at[k], (px, py, c)).wait_recv()
                _remote(ins[n], outs[n].at[me], sems[4].at[k], sems[5].at[k], (px, py, c)).wait_send()
        for i in range(items):
            pltpu.make_async_copy(bufs[i], outs[i].at[me], sems[7].at[i]).wait()
        for w in range(n):
            for k, (px, py) in enumerate(chips):
                ici(ins, outs, sems, w, k, (px, py), c, me).wait_send()
                landed = piece(outs, w, 2 * px + py, c)
                _remote(landed, landed, sems[2].at[w, k], sems[3].at[w, k], sibling).wait_send()

    out_shape = [jax.ShapeDtypeStruct((N_SHARD,) + BIG_SHARD[nm], BF16) for nm in names]
    stage = [pltpu.VMEM(BIG_SHARD[nm], BF16) for nm in names]
    inputs = list(shards)
    if conv_w is not None:
        out_shape.append(jax.ShapeDtypeStruct((N_SHARD,) + conv_w.shape, F32))
        stage.append(pltpu.VMEM(conv_w.shape, F32))
        inputs.append(conv_w)
    scratch = ([pltpu.SemaphoreType.DMA((n, N_CHIP_PEERS))] * 4 + [pltpu.SemaphoreType.DMA((N_CHIP_PEERS,))] * 2
               + [pltpu.SemaphoreType.DMA((items,))] * 2 + stage)
    return _Rider(inputs, out_shape, scratch, first, late, last)


def _pair_exchange_rider(names, grads):
    def copies(ins, outs, scr):
        x, y, c, _ = _place()
        out = []
        for w, nm in enumerate(names):
            h = BIG_SHARD[nm][0] // 2
            out.append(_remote(ins[w].at[:, pl.ds((1 - c) * h, h), :], outs[w], scr[0].at[w], scr[1].at[w], (x, y, 1 - c)))
        return out

    def first(ins, outs, scr):
        for cp in copies(ins, outs, scr):
            cp.start()

    def last(ins, outs, scr):
        for cp in copies(ins, outs, scr):
            cp.wait()

    out_shape = [jax.ShapeDtypeStruct((N_SHARD, BIG_SHARD[nm][0] // 2, BIG_SHARD[nm][1]), F32) for nm in names]
    return _Rider(grads, out_shape, [pltpu.SemaphoreType.DMA((len(names),))] * 2, first, _nothing, last)


def _pair_add(core, grad, other, name):
    _, r, cols = grad.shape
    h = r // 2
    th = min(h, 256)
    per = h // th

    def body(c_ref, g_ref, o_ref, out_ref):
        out_ref[...] = (g_ref[...] + o_ref[...]).astype(BF16)

    return pl.pallas_call(
        body, name=name,
        grid_spec=pltpu.PrefetchScalarGridSpec(
            num_scalar_prefetch=1, grid=(N_SHARD, per),
            in_specs=[pl.BlockSpec((1, th, cols), lambda j, i, c_ref: (j, c_ref[0] * per + i, 0)),
                      pl.BlockSpec((1, th, cols), lambda j, i, c_ref: (j, i, 0))],
            out_specs=pl.BlockSpec((1, th, cols), lambda j, i, c_ref: (j, i, 0))),
        out_shape=jax.ShapeDtypeStruct((N_SHARD, h, cols), BF16),
        compiler_params=_params(2),
    )(core, grad, other)


def _chip_exchange_rider(parts):
    n = len(parts)

    def sends(ins, outs, scr):
        x, y, c, chips = _place()
        me = 2 * x + y
        return [_remote(ins[w].at[2 * px + py], outs[w].at[me], scr[0].at[w, k], scr[1].at[w, k], (px, py, c))
                for w in range(n) for k, (px, py) in enumerate(chips)]

    def first(ins, outs, scr):
        x, y, c, chips = _place()
        me = 2 * x + y
        legs = _staged_start([r.at[me] for r in ins], scr[4:], scr[2])
        for cp in sends(ins, outs, scr):
            cp.start()
        _staged_finish(legs, scr[4:], [o.at[me] for o in outs], scr[3])

    def last(ins, outs, scr):
        x, y, c, chips = _place()
        me = 2 * x + y
        for w in range(n):
            for k, (px, py) in enumerate(chips):
                got = outs[w].at[2 * px + py]
                _remote(got, got, scr[0].at[w, k], scr[1].at[w, k], (px, py, c)).wait_recv()
        for cp in sends(ins, outs, scr):
            cp.wait_send()
        for w in range(n):
            pltpu.make_async_copy(scr[4 + w], outs[w].at[me], scr[3].at[w]).wait()

    out_shape = [jax.ShapeDtypeStruct(p.shape, BF16) for p in parts]
    scratch = ([pltpu.SemaphoreType.DMA((n, N_CHIP_PEERS))] * 2 + [pltpu.SemaphoreType.DMA((n,))] * 2
               + [pltpu.VMEM(p.shape[1:], BF16) for p in parts])
    return _Rider(parts, out_shape, scratch, first, _nothing, last)


def _chip_sum(parts, name):
    _, h, cols = parts.shape
    th = min(h, 256)

    def body(p_ref, out_ref):
        acc = p_ref[0].astype(F32)
        for j in range(1, N_SHARD):
            acc = acc + p_ref[j].astype(F32)
        out_ref[...] = acc

    return pl.pallas_call(
        body, name=name, grid=(h // th,),
        in_specs=[pl.BlockSpec((N_SHARD, th, cols), lambda i: (0, i, 0))],
        out_specs=pl.BlockSpec((th, cols), lambda i: (i, 0)),
        out_shape=jax.ShapeDtypeStruct((h, cols), F32),
        compiler_params=_params(),
    )(parts)


def _pair_share_rider(names, halves):
    n = len(names)
    hs = [BIG_SHARD[nm][0] // 2 for nm in names]

    def mine(outs, c):
        return [outs[w].at[pl.ds(c * hs[w], hs[w])] for w in range(n)]

    def first(ins, outs, scr):
        x, y, c, _ = _place()
        legs = _staged_start(ins, scr[4:], scr[2])
        for w, dst in enumerate(mine(outs, c)):
            _remote(ins[w], dst, scr[0].at[w], scr[1].at[w], (x, y, 1 - c)).start()
        _staged_finish(legs, scr[4:], mine(outs, c), scr[3])

    def last(ins, outs, scr):
        x, y, c, _ = _place()
        for w, (theirs, dst) in enumerate(zip(mine(outs, 1 - c), mine(outs, c))):
            _remote(theirs, theirs, scr[0].at[w], scr[1].at[w], (x, y, 1 - c)).wait_recv()
            _remote(ins[w], dst, scr[0].at[w], scr[1].at[w], (x, y, 1 - c)).wait_send()
            pltpu.make_async_copy(scr[4 + w], dst, scr[3].at[w]).wait()

    out_shape = [jax.ShapeDtypeStruct(BIG_SHARD[nm], F32) for nm in names]
    scratch = [pltpu.SemaphoreType.DMA((n,))] * 4 + [pltpu.VMEM((h, BIG_SHARD[nm][1]), F32) for nm, h in zip(names, hs)]
    return _Rider(halves, out_shape, scratch, first, _nothing, last)


N_DEV = 8


def _all_peers(x, y, c):
    return [((1 - x) if fx else x, (1 - y) if fy else y, (1 - c) if fc else c)
            for fx in (0, 1) for fy in (0, 1) for fc in (0, 1) if fx or fy or fc]


def _small_gather_rider(vec):
    def sends(ins, outs, scr):
        x, y, c, _ = _place()
        me = 4 * x + 2 * y + c
        return [_remote(ins[0], outs[0].at[me], scr[0].at[k], scr[1].at[k], dev) for k, dev in enumerate(_all_peers(x, y, c))]

    def first(ins, outs, scr):
        x, y, c, _ = _place()
        legs = _staged_start(ins, scr[4:], scr[2])
        for cp in sends(ins, outs, scr):
            cp.start()
        _staged_finish(legs, scr[4:], [outs[0].at[4 * x + 2 * y + c]], scr[3])

    def last(ins, outs, scr):
        x, y, c, _ = _place()
        for k, (px, py, pc) in enumerate(_all_peers(x, y, c)):
            got = outs[0].at[4 * px + 2 * py + pc]
            _remote(got, got, scr[0].at[k], scr[1].at[k], (px, py, pc)).wait_recv()
        for cp in sends(ins, outs, scr):
            cp.wait_send()
        pltpu.make_async_copy(scr[4], outs[0].at[4 * x + 2 * y + c], scr[3].at[0]).wait()

    scratch = ([pltpu.SemaphoreType.DMA((N_DEV - 1,))] * 2 + [pltpu.SemaphoreType.DMA((1,))] * 2
               + [pltpu.VMEM(vec.shape, F32)])
    return _Rider([vec], [jax.ShapeDtypeStruct((N_DEV,) + vec.shape, F32)], scratch, first, _nothing, last)


def _sum_devices(gathered):
    def body(g_ref, out_ref):
        acc = g_ref[0]
        for j in range(1, N_DEV):
            acc = acc + g_ref[j]
        out_ref[...] = acc

    return pl.pallas_call(body, name="sum_devices", out_shape=jax.ShapeDtypeStruct(gathered.shape[1:], F32),
                          compiler_params=_params(0))(gathered)


def _allreduce_small(vec):
    rows = vec.shape[0]

    def body(v_ref, sum_ref, gat_ref, send, recv, loc_sem):
        x, y, c, chips = _place()
        sibling = (x, y, 1 - c)
        slot = lambda px, py, pc: gat_ref.at[4 * px + 2 * py + pc]
        lc = pltpu.make_async_copy(v_ref, slot(x, y, c), loc_sem)
        lc.start()
        sends = [_remote(v_ref, slot(x, y, c), send.at[0], recv.at[0], sibling)]
        sends += [_remote(v_ref, slot(x, y, c), send.at[1 + k], recv.at[1 + k], (px, py, c))
                  for k, (px, py) in enumerate(chips)]
        for cp in sends:
            cp.start()
        for k, (px, py) in enumerate(chips):
            got = slot(px, py, c)
            _remote(got, got, send.at[1 + k], recv.at[1 + k], (px, py, c)).wait_recv()
            cp = _remote(got, got, send.at[4 + k], recv.at[4 + k], sibling)
            cp.start()
            sends.append(cp)
        got = slot(x, y, 1 - c)
        _remote(got, got, send.at[0], recv.at[0], sibling).wait_recv()
        for k, (px, py) in enumerate(chips):
            got = slot(px, py, 1 - c)
            _remote(got, got, send.at[4 + k], recv.at[4 + k], sibling).wait_recv()
        for cp in sends:
            cp.wait_send()
        lc.wait()
        acc = gat_ref[0]
        for j in range(1, N_DEV):
            acc = acc + gat_ref[j]
        sum_ref[...] = acc

    total, _ = pl.pallas_call(
        body, name="allreduce_small",
        in_specs=[_whole_vmem()], out_specs=[_whole_vmem(), _whole_vmem()],
        out_shape=[jax.ShapeDtypeStruct((rows, 128), F32), jax.ShapeDtypeStruct((N_DEV, rows, 128), F32)],
        scratch_shapes=[pltpu.SemaphoreType.DMA((N_DEV - 1,))] * 2 + [pltpu.SemaphoreType.DMA(())],
        compiler_params=pltpu.CompilerParams(has_side_effects=True, vmem_limit_bytes=VMEM_LIMIT),
    )(vec)
    return total


def _adam_math(w_ref, g_ref, m_ref, v_ref, d_ref, m2_ref, v2_ref):
    c1 = 1.0 - ADAM_B1 ** ADAM_STEP
    c2 = 1.0 - ADAM_B2 ** ADAM_STEP
    gv = g_ref[...]
    m2 = ADAM_B1 * m_ref[...] + (1.0 - ADAM_B1) * gv
    v2 = ADAM_B2 * v_ref[...] + (1.0 - ADAM_B2) * (gv * gv)
    m2_ref[...] = m2
    v2_ref[...] = v2
    d_ref[...] = -ADAM_LR * ((m2 / c1) / (jnp.sqrt(v2 / c2) + ADAM_EPS) + ADAM_WD * w_ref[...])


def _adamw_many(ws, gs, ms, vs):
    n = len(ws)

    def body(*refs):
        for i in range(n):
            _adam_math(*[refs[k * n + i] for k in range(7)])

    shapes = [jax.ShapeDtypeStruct(w.shape, F32) for w in ws]
    res = pl.pallas_call(body, name="adamw_small", out_shape=shapes * 3, compiler_params=_params(0))(*ws, *gs, *ms, *vs)
    return res[:n], res[n:2 * n], res[2 * n:]


def _adamw(w, g, m, v, name):
    rows, cols = w.shape
    tr = 256 if rows % 256 == 0 else rows

    def body(w_ref, g_ref, m_ref, v_ref, d_ref, m2_ref, v2_ref):
        _adam_math(w_ref, g_ref, m_ref, v_ref, d_ref, m2_ref, v2_ref)

    spec = pl.BlockSpec((tr, cols), lambda i: (i, 0))
    shp = jax.ShapeDtypeStruct((rows, cols), F32)
    return pl.pallas_call(
        body, name=name, grid=(rows // tr,), in_specs=[spec] * 4, out_specs=[spec] * 3, out_shape=[shp] * 3,
        compiler_params=_params(),
    )(w, g, m, v)


SMALL = (
    ("attn_norm_g", (1, 1024)), ("conv_w", (1, 4, 512)), ("conv_b", (1, 512)),
    ("lru_wa_fwd", (1, 8, 64, 64)), ("lru_ba_fwd", (1, 512)), ("lru_wx_fwd", (1, 8, 64, 64)), ("lru_bx_fwd", (1, 512)),
    ("lru_lam_fwd", (1, 512)),
    ("lru_wa_bwd", (1, 8, 64, 64)), ("lru_ba_bwd", (1, 512)), ("lru_wx_bwd", (1, 8, 64, 64)), ("lru_bx_bwd", (1, 512)),
    ("lru_lam_bwd", (1, 512)),
    ("rel_bias", (32, 8)), ("norm_rnn_g", (1, 512)), ("norm_attn_g", (1, 512)), ("mlp_norm_g", (1, 1024)),
    ("final_norm_g", (1024,)),
)
PACK_ROW = 8 * 128


def _pack(parts):
    flat = jnp.concatenate([p.reshape(-1) for p in parts])
    pad = (-flat.shape[0]) % PACK_ROW
    return jnp.pad(flat, (0, pad)).reshape(-1, 128)


def _unpack(packed, shapes):
    flat = packed.reshape(-1)
    out, off = [], 0
    for shp in shapes:
        n = int(np.prod(shp))
        out.append(flat[off:off + n].reshape(shp))
        off += n
    return out


WEIGHT_ORDER = ("attn_norm_g", "w_in", "conv_w", "conv_b", "lru_wa_fwd", "lru_ba_fwd", "lru_wx_fwd", "lru_bx_fwd",
                "lru_lam_fwd", "lru_wa_bwd", "lru_ba_bwd", "lru_wx_bwd", "lru_bx_bwd", "lru_lam_bwd", "rel_bias",
                "norm_rnn_g", "norm_attn_g", "w_out", "mlp_norm_g", "w_up", "w_down", "final_norm_g")


def kernel(x, attn_norm_g, w_in, conv_w, conv_b, lru_wa_fwd, lru_ba_fwd, lru_wx_fwd, lru_bx_fwd, lru_lam_fwd, lru_wa_bwd, lru_ba_bwd, lru_wx_bwd, lru_bx_bwd, lru_lam_bwd, rel_bias, norm_rnn_g, norm_attn_g, w_out, mlp_norm_g, w_up, w_down, final_norm_g, loss_target, m_attn_norm_g, m_w_in, m_conv_w, m_conv_b, m_lru_wa_fwd, m_lru_ba_fwd, m_lru_wx_fwd, m_lru_bx_fwd, m_lru_lam_fwd, m_lru_wa_bwd, m_lru_ba_bwd, m_lru_wx_bwd, m_lru_bx_bwd, m_lru_lam_bwd, m_rel_bias, m_norm_rnn_g, m_norm_attn_g, m_w_out, m_mlp_norm_g, m_w_up, m_w_down, m_final_norm_g, v_attn_norm_g, v_w_in, v_conv_w, v_conv_b, v_lru_wa_fwd, v_lru_ba_fwd, v_lru_wx_fwd, v_lru_bx_fwd, v_lru_lam_fwd, v_lru_wa_bwd, v_lru_ba_bwd, v_lru_wx_bwd, v_lru_bx_bwd, v_lru_lam_bwd, v_rel_bias, v_norm_rnn_g, v_norm_attn_g, v_w_out, v_mlp_norm_g, v_w_up, v_w_down, v_final_norm_g):
    given = dict(locals())
    w = {n: given[n] for n in WEIGHT_ORDER}
    m = {n: given["m_" + n] for n in WEIGHT_ORDER}
    v = {n: given["v_" + n] for n in WEIGHT_ORDER}

    chip = lax.axis_index("x") * 2 + lax.axis_index("y")
    core = lax.axis_index("c")

    shards = {n: w[n][0].astype(BF16) for n in BIG}
    shards["conv_w"] = w["conv_w"][0]
    p = {n: (t[0] if t.ndim >= 3 else t) for n, t in w.items() if n not in BIG and n != "conv_w"}
    p["final_norm_g"] = w["final_norm_g"].reshape(1, D_MODEL)

    _, grad_x, small, gathered, big, reduced = _local_step(x[0], loss_target[0], p, shards)

    late = tuple(big)
    grads = [big[n] for n in late]
    others = _run_rider(_pair_exchange_rider(late, grads), "grad_pair_exchange")
    core_arr = core.reshape(1).astype(jnp.int32)
    parts = [_pair_add(core_arr, g, o, "grad_pair_add_" + n) for n, g, o in zip(late, grads, others)]
    landed = _run_rider(_chip_exchange_rider(parts), "grad_chip_exchange")
    halves = [_chip_sum(t, "grad_chip_sum_" + n) for n, t in zip(late, landed)]
    reduced.update(zip(late, _run_rider(_pair_share_rider(late, halves), "grad_pair_share")))

    early_small = [(n, shp) for n, shp in SMALL if n not in small]
    late_small = [(n, shp) for n, shp in SMALL if n in small]
    *early_g, loss = _unpack(_sum_devices(gathered), [shp for _, shp in early_small] + [(1,)])
    late_g = _unpack(_allreduce_small(_pack([small[n].reshape(shp) for n, shp in late_small])),
                     [shp for _, shp in late_small])
    g = dict(zip([n for n, _ in early_small + late_small], early_g + late_g))
    g["conv_w"] = lax.dynamic_slice_in_dim(g["conv_w"], chip * (D_RNN // N_SHARD), D_RNN // N_SHARD, axis=2)
    for n in BIG:
        g[n] = reduced[n][None]

    delta, new_m, new_v = {}, {}, {}
    for n in BIG:
        d2, m2, v2 = _adamw(w[n][0], reduced[n], m[n][0], v[n][0], "adamw_" + n)
        delta[n], new_m[n], new_v[n] = d2[None], m2[None], v2[None]
    names = [n for n, _ in SMALL]
    for dst, src in zip((delta, new_m, new_v), _adamw_many(*[[t[n] for n in names] for t in (w, g, m, v)])):
        dst.update(dict(zip(names, src)))

    return (loss.reshape(()), grad_x[None], *[g[n] for n in WEIGHT_ORDER], *[delta[n] for n in WEIGHT_ORDER],
            *[new_m[n] for n in WEIGHT_ORDER], *[new_v[n] for n in WEIGHT_ORDER])
```

```python
import functools
import math

import numpy as np
import jax
import jax.numpy as jnp
from jax import lax
from jax.experimental import pallas as pl
from jax.experimental.pallas import tpu as pltpu

F32 = jnp.float32
BF16 = jnp.bfloat16

D_MODEL = 1024
D_RNN = 512
D_ATTN = 512
N_HEADS = 8
HEAD_DIM = 64
N_RNN_BLOCKS = 8
RNN_BLOCK = 64
D_IN = 2 * D_RNN + 3 * D_ATTN
D_FF = 4 * D_MODEL
N_SHARD = 4
IN_BLK = D_IN // N_SHARD
OUT_BLK = D_MODEL // N_SHARD
FF_BLK = D_FF // N_SHARD
EPS = 1e-6
NEG_INF = -1e30
LRU_C = 8.0
DILATIONS = (1, 4, 16)
F32_LAYOUT = 4
HALF_WIN = 64
Q_BLK = 128
K_WIN = 256
N_BUCKETS = 32
MAX_DISTANCE = 1024
ATTN_SCALE = HEAD_DIM ** -0.5

ADAM_LR = 0.001
ADAM_B1 = 0.9
ADAM_B2 = 0.999
ADAM_EPS = 1e-08
ADAM_WD = 0.01
ADAM_STEP = 10

TS = 512
TS_MLP = 256
TS_INPROJ_BWD = 512
ATTN_SUB = 16
TK_DW = 4096
SCAN_UNROLL = 8
SUB = 8
VMEM_LIMIT = 56 * 1024 * 1024
GELU_C0 = math.sqrt(2.0 / math.pi)
GELU_C1 = 0.044715

MESH = pl.DeviceIdType.MESH


def _params(n_grid=1):
    return pltpu.CompilerParams(vmem_limit_bytes=VMEM_LIMIT, dimension_semantics=("arbitrary",) * n_grid)


def _whole_vmem():
    return pl.BlockSpec(memory_space=pltpu.VMEM)


def _rows(width, tile=TS):
    return pl.BlockSpec((tile, width), lambda i: (i, 0))


def _sigmoid(z):
    return 0.5 * jnp.tanh(0.5 * z) + 0.5


def _log1p(u):
    w = 1.0 + u
    return jnp.where(w == 1.0, u, jnp.log(w) * (u / (w - 1.0)))


def _softplus(z):
    return jnp.maximum(z, 0.0) + _log1p(jnp.exp(-jnp.abs(z)))


def _gelu_parts(g):
    inner = GELU_C0 * (g + GELU_C1 * g * g * g)
    t = jnp.tanh(inner)
    val = 0.5 * g * (1.0 + t)
    dinner = GELU_C0 * (1.0 + 3.0 * GELU_C1 * g * g)
    grad = 0.5 * (1.0 + t) + 0.5 * g * (1.0 - t * t) * dinner
    return val, grad


def _rms(x):
    rstd = lax.rsqrt(jnp.mean(x * x, axis=-1, keepdims=True) + EPS)
    return rstd, x * rstd


def _rms_bwd(dy, g, xhat, rstd):
    dxh = dy * g
    dx = rstd * (dxh - xhat * jnp.mean(dxh * xhat, axis=-1, keepdims=True))
    dg = jnp.sum(dy * xhat, axis=0, keepdims=True)
    return dx, dg


def _dot(a, b):
    return jnp.dot(a, b, preferred_element_type=F32)


def _dot_nt(a, b):
    return lax.dot_general(a, b, (((1,), (1,)), ((), ())), preferred_element_type=F32)


def _dot_tn(a, b):
    return lax.dot_general(a, b, (((0,), (0,)), ((), ())), preferred_element_type=F32)


def _shifted(tile, prev8, next8, k):
    n = tile.shape[0]
    row = lax.broadcasted_iota(jnp.int32, tile.shape, 0)
    if k == 0:
        return tile
    if k < 0:
        r = pltpu.roll(tile, -k, 0)
        for j in range(-k):
            r = jnp.where(row == j, prev8[SUB + j + k:SUB + j + k + 1, :], r)
        return r
    r = pltpu.roll(tile, n - k, 0)
    for j in range(k):
        r = jnp.where(row == n - k + j, next8[j:j + 1, :], r)
    return r


def _to_lane_blocks(val, s_ref):
    for j in range(val.shape[1] // 128):
        s_ref[j] = val[:, j * 128:(j + 1) * 128]


def _from_lane_blocks(s_ref):
    return jnp.concatenate([s_ref[j] for j in range(s_ref.shape[0])], axis=-1)


def _class_rows(s_ref, r, dil):
    n = s_ref.shape[1] // dil
    return jnp.concatenate([s_ref[j, pl.ds(r, n, stride=dil), :] for j in range(s_ref.shape[0])], axis=-1)


def _split_classes(val, s_ref, out_ref, dil):
    _to_lane_blocks(val, s_ref)
    for r in range(dil):
        out_ref[r] = _class_rows(s_ref, r, dil).astype(out_ref.dtype)


def _merge_classes(in_ref, s_ref, dil, also_ref=None):
    n = s_ref.shape[1] // dil
    for r in range(dil):
        v = in_ref[r].astype(F32)
        if also_ref is not None:
            v = v + also_ref[r].astype(F32)
        for j in range(s_ref.shape[0]):
            s_ref[j, pl.ds(r, n, stride=dil), :] = v[:, j * 128:(j + 1) * 128]
    return _from_lane_blocks(s_ref)


def _class_spec(dil, tile=TS):
    return pl.BlockSpec((dil, tile // dil, 512), lambda i: (0, i, 0))


def _class_shape(S, dil, dtype):
    return jax.ShapeDtypeStruct((dil, S // dil, 512), dtype)


def _scan_tile(a_ref, b_ref, h_ref, carry_ref, reverse):
    n = a_ref.shape[0]
    width = a_ref.shape[1]
    groups = n // SUB
    row = lax.broadcasted_iota(jnp.int32, (SUB, width), 0)

    def group_scan(g):
        r0 = pl.multiple_of(g * SUB, SUB)
        a = a_ref[pl.ds(r0, SUB), :]
        b = b_ref[pl.ds(r0, SUB), :]
        for s in (1, 2, 4):
            if reverse:
                a_sh = pltpu.roll(a, SUB - s, 0)
                b_sh = pltpu.roll(b, SUB - s, 0)
                m = row < SUB - s
            else:
                a_sh = pltpu.roll(a, s, 0)
                b_sh = pltpu.roll(b, s, 0)
                m = row >= s
            b = jnp.where(m, a * b_sh + b, b)
            a = jnp.where(m, a * a_sh, a)
        return r0, a, b

    def step(i, carry):
        first = i * SCAN_UNROLL
        order = [(groups - 1 - (first + u)) if reverse else (first + u) for u in range(SCAN_UNROLL)]
        scans = [group_scan(g) for g in order]
        for r0, a, b in scans:
            h = b + a * carry
            h_ref[pl.ds(r0, SUB), :] = h
            edge = h[0:1, :] if reverse else h[SUB - 1:SUB, :]
            carry = jnp.broadcast_to(edge, (SUB, width))
        return carry

    carry_ref[...] = lax.fori_loop(0, groups // SCAN_UNROLL, step, carry_ref[...])


def _conv_fwd(xr, prev8, next8, cw, cb):
    y = cb + _shifted(xr, prev8, next8, -2) * cw[0:1, :]
    y = y + _shifted(xr, prev8, next8, -1) * cw[1:2, :]
    y = y + xr * cw[2:3, :]
    y = y + _shifted(xr, prev8, next8, 1) * cw[3:4, :]
    return y


def _lru_gates(xc, wa_ref, ba, wx_ref, bx, lam):
    xcb = xc.astype(BF16)
    r = _sigmoid(_dot(xcb, wa_ref[...]) + ba)
    i = _sigmoid(_dot(xcb, wx_ref[...]) + bx)
    cl = -LRU_C * _softplus(-lam)
    la = cl * r
    a = jnp.exp(la)
    m2 = -jnp.tanh(la) * (a * a + 1.0)
    inv = jnp.where(m2 > 0.0, lax.rsqrt(m2), 0.0)
    mult = m2 * inv
    return xcb, r, i, cl, a, mult, inv


def _inproj_fwd(x, g1, w_in, rider=None):
    S = x.shape[0]

    def body(x_ref, g_ref, w_ref, xr_ref, gate_ref, *rest):
        qkv_refs, s_ref, s4_ref, w_full = rest[:9], rest[9], rest[10], rest[11]

        @pl.when(pl.program_id(0) == 0)
        def _():
            for j in range(N_SHARD):
                w_full[:, j * IN_BLK:(j + 1) * IN_BLK] = w_ref[j]

        _, xh = _rms(x_ref[...])
        h = (xh * g_ref[...]).astype(BF16)
        proj = _dot(h, w_full[...])
        xr_ref[...] = proj[:, 0:512]
        gate_ref[...] = proj[:, 512:1024]
        for t in range(3):
            val = proj[:, 1024 + 512 * t:1536 + 512 * t]
            d1_ref, d4_ref, d16_ref = qkv_refs[3 * t:3 * t + 3]
            d1_ref[0] = val.astype(BF16)
            _to_lane_blocks(val, s_ref)
            for r4 in range(4):
                c4 = _class_rows(s_ref, r4, 4)
                d4_ref[r4] = c4.astype(BF16)
                _to_lane_blocks(c4, s4_ref.at[r4])
            for r4 in range(4):
                for m in range(4):
                    d16_ref[r4 + 4 * m] = _class_rows(s4_ref.at[r4], m, 4).astype(BF16)

    f = jax.ShapeDtypeStruct((S, 512), F32)
    return _call(
        body, "inproj_fwd", (S // TS,),
        [_rows(D_MODEL), _whole_vmem(), _whole_vmem()],
        [_rows(512)] * 2 + [_class_spec(d) for d in DILATIONS] * 3,
        [f, f] + [_class_shape(S, d, BF16) for d in DILATIONS] * 3,
        [pltpu.VMEM((4, TS, 128), F32), pltpu.VMEM((4, 4, TS // 4, 128), F32), pltpu.VMEM((D_MODEL, D_IN), BF16)],
        (x, g1, w_in), rider)


def _halo_specs(S, order, tile=TS):
    per = tile // SUB
    last = S // SUB - 1
    return [
        pl.BlockSpec((tile, 512), lambda i: (order(i), 0)),
        pl.BlockSpec((SUB, 512), lambda i: (jnp.maximum(order(i) * per - 1, 0), 0)),
        pl.BlockSpec((SUB, 512), lambda i: (jnp.minimum((order(i) + 1) * per, last), 0)),
    ]


def _rnn_fwd(xr, conv_w, conv_b, wa, ba, wx, bx, lam, reverse, rider=None):
    S = xr.shape[0]
    nt = S // TS
    order = (lambda i: nt - 1 - i) if reverse else (lambda i: i)

    def body(x_ref, xp_ref, xn_ref, cw_ref, cb_ref, wa_ref, ba_ref, wx_ref, bx_ref, lam_ref, h_ref, a_s, b_s, carry):
        i = pl.program_id(0)
        t = order(i)

        @pl.when(i == 0)
        def _():
            carry[...] = jnp.zeros_like(carry)

        prev8 = jnp.where(t > 0, xp_ref[...], 0.0)
        next8 = jnp.where(t < nt - 1, xn_ref[...], 0.0)
        xc = _conv_fwd(x_ref[...], prev8, next8, cw_ref[...], cb_ref[...])
        _, _, gi, _, a, mult, _ = _lru_gates(xc, wa_ref, ba_ref[...], wx_ref, bx_ref[...], lam_ref[...])
        a_s[...] = a
        b_s[...] = mult * (gi * xc)
        _scan_tile(a_s, b_s, h_ref, carry, reverse)

    (h,), carried = _call(
        body, "rnn_fwd_rev" if reverse else "rnn_fwd_fwd", (nt,),
        _halo_specs(S, order) + [_whole_vmem()] * 7,
        [pl.BlockSpec((TS, 512), lambda i: (order(i), 0))],
        [jax.ShapeDtypeStruct((S, 512), F32)],
        [pltpu.VMEM((TS, 512), F32), pltpu.VMEM((TS, 512), F32), pltpu.VMEM((SUB, 512), F32)],
        (xr, xr, xr, conv_w, conv_b, wa, ba, wx, bx, lam), rider)
    return h, carried


def _mix_fwd(o3, l3, hf, hb, gate, x, g_rnn, g_attn, w_out):
    S = x.shape[0]

    def body(o1, o2, o3_, l1, l2, l3_, hf_ref, hb_ref, gate_ref, x_ref, gr_ref, ga_ref, w_ref,
             x1_ref, mix_ref, ya1, ya2, ls1, ls2, s_ref):
        la, lb, lc = l1[0], _merge_classes(l2, s_ref, F32_LAYOUT), _merge_classes(l3_, s_ref, F32_LAYOUT)
        m = jnp.maximum(jnp.maximum(la, lb), lc)
        ea, eb, ec = jnp.exp(la - m), jnp.exp(lb - m), jnp.exp(lc - m)
        den = ea + eb + ec
        lse = m + jnp.log(den)
        ya = (ea * o1[0] + eb * _merge_classes(o2, s_ref, F32_LAYOUT) + ec * _merge_classes(o3_, s_ref, F32_LAYOUT)) / den
        ya1[0] = ya
        ls1[0] = lse
        _split_classes(ya, s_ref, ya2, F32_LAYOUT)
        _split_classes(lse, s_ref, ls2, F32_LAYOUT)
        gg, _ = _gelu_parts(gate_ref[...])
        yr = (hf_ref[...] + hb_ref[...]) * gg
        _, xh_r = _rms(yr)
        _, xh_a = _rms(ya)
        mix = jnp.concatenate([xh_r * gr_ref[...], xh_a * ga_ref[...]], axis=-1).astype(BF16)
        mix_ref[...] = mix
        acc = x_ref[...]
        for j in range(N_SHARD):
            acc = acc + _dot(mix[:, j * OUT_BLK:(j + 1) * OUT_BLK], w_ref[j])
        x1_ref[...] = acc

    one, four = _class_spec(1), _class_spec(F32_LAYOUT)
    return pl.pallas_call(
        body, grid=(S // TS,), name="mix_fwd",
        in_specs=[one, four, four] * 2 + [_rows(512)] * 3 + [_rows(D_MODEL)] + [_whole_vmem()] * 3,
        out_specs=[_rows(D_MODEL), _rows(D_MODEL)] + [one, four] * 2,
        out_shape=[jax.ShapeDtypeStruct((S, D_MODEL), F32), jax.ShapeDtypeStruct((S, D_MODEL), BF16)]
        + [_class_shape(S, 1, F32), _class_shape(S, F32_LAYOUT, F32)] * 2,
        scratch_shapes=[pltpu.VMEM((4, TS, 128), F32)],
        compiler_params=_params(),
    )(*o3, *l3, hf, hb, gate, x, g_rnn, g_attn, w_out)


def _mlp_fwd_bwd(x1, target, g_mlp, g_fin, w_up, w_down):
    S = x1.shape[0]
    tm = TS_MLP

    def body(x1_ref, t_ref, gm_ref, gf_ref, wu_ref, wd_ref,
             dx1_ref, h2_ref, a2_ref, du_ref, dx2_ref, loss_ref, dgf_ref, dgm_ref, relu_s):
        @pl.when(pl.program_id(0) == 0)
        def _():
            loss_ref[...] = jnp.zeros_like(loss_ref)
            dgf_ref[...] = jnp.zeros_like(dgf_ref)
            dgm_ref[...] = jnp.zeros_like(dgm_ref)

        x1v = x1_ref[...]
        rstd1, xh1 = _rms(x1v)
        h2 = (xh1 * gm_ref[...]).astype(BF16)
        h2_ref[...] = h2
        x2 = x1v
        for j in range(N_SHARD):
            r = jnp.maximum(_dot(h2, wu_ref[j]), 0.0)
            relu_s[j] = r
            a2 = (r * r).astype(BF16)
            a2_ref[:, j * FF_BLK:(j + 1) * FF_BLK] = a2
            x2 = x2 + _dot(a2, wd_ref[j])
        rstd2, xh2 = _rms(x2)
        err = xh2 * gf_ref[...] - t_ref[...]
        loss_ref[...] += jnp.sum(err * err, axis=0, keepdims=True)
        dy = err * (1.0 / D_MODEL)
        dx2, dgf = _rms_bwd(dy, gf_ref[...], xh2, rstd2)
        dgf_ref[...] += dgf
        dx2b = dx2.astype(BF16)
        dx2_ref[...] = dx2b
        dh2 = jnp.zeros((tm, D_MODEL), F32)
        for j in range(N_SHARD):
            du = (_dot_nt(dx2b, wd_ref[j]) * (2.0 * relu_s[j])).astype(BF16)
            du_ref[:, j * FF_BLK:(j + 1) * FF_BLK] = du
            dh2 = dh2 + _dot_nt(du, wu_ref[j])
        dx1n, dgm = _rms_bwd(dh2, gm_ref[...], xh1, rstd1)
        dgm_ref[...] += dgm
        dx1_ref[...] = dx2 + dx1n

    vec = jax.ShapeDtypeStruct((1, D_MODEL), F32)
    return pl.pallas_call(
        body, grid=(S // tm,), name="mlp_fwd_bwd",
        in_specs=[_rows(D_MODEL, tm), _rows(D_MODEL, tm)] + [_whole_vmem()] * 4,
        out_specs=[_rows(D_MODEL, tm), _rows(D_MODEL, tm), _rows(D_FF, tm), _rows(D_FF, tm), _rows(D_MODEL, tm)]
        + [_whole_vmem()] * 3,
        out_shape=[jax.ShapeDtypeStruct((S, D_MODEL), F32), jax.ShapeDtypeStruct((S, D_MODEL), BF16),
                   jax.ShapeDtypeStruct((S, D_FF), BF16), jax.ShapeDtypeStruct((S, D_FF), BF16),
                   jax.ShapeDtypeStruct((S, D_MODEL), BF16), vec, vec, vec],
        scratch_shapes=[pltpu.VMEM((N_SHARD, tm, FF_BLK), F32)],
        compiler_params=_params(),
    )(x1, target, g_mlp, g_fin, w_up, w_down)


def _mix_bwd(dx1, w_out, mixb, ya, hf, hb, gate, g_rnn, g_attn):
    S = dx1.shape[0]

    def body(dx1_ref, w_ref, mix_ref, ya_ref, hf_ref, hb_ref, gate_ref, gr_ref, ga_ref,
             dhs_ref, dgate_ref, dya1, dya2, dw_ref, dgr_ref, dga_ref, s_ref):
        @pl.when(pl.program_id(0) == 0)
        def _():
            dw_ref[...] = jnp.zeros_like(dw_ref)
            dgr_ref[...] = jnp.zeros_like(dgr_ref)
            dga_ref[...] = jnp.zeros_like(dga_ref)

        dx1b = dx1_ref[...].astype(BF16)
        mix = mix_ref[...]
        for j in range(N_SHARD):
            dw_ref[j] += _dot_tn(mix[:, j * OUT_BLK:(j + 1) * OUT_BLK], dx1b)
        dmix = jnp.concatenate([_dot_nt(dx1b, w_ref[j]) for j in range(N_SHARD)], axis=-1)
        gg, dgg = _gelu_parts(gate_ref[...])
        hs = hf_ref[...] + hb_ref[...]
        rstd_r, xh_r = _rms(hs * gg)
        dyr, dgr = _rms_bwd(dmix[:, 0:D_RNN], gr_ref[...], xh_r, rstd_r)
        dgr_ref[...] += dgr
        rstd_a, xh_a = _rms(ya_ref[0])
        dya, dga = _rms_bwd(dmix[:, D_RNN:], ga_ref[...], xh_a, rstd_a)
        dga_ref[...] += dga
        dya1[0] = dya
        _split_classes(dya, s_ref, dya2, F32_LAYOUT)
        dhs_ref[...] = dyr * gg
        dgate_ref[...] = dyr * hs * dgg

    f512 = jax.ShapeDtypeStruct((S, 512), F32)
    vec = jax.ShapeDtypeStruct((1, 512), F32)
    return pl.pallas_call(
        body, grid=(S // TS,), name="mix_bwd",
        in_specs=[_rows(D_MODEL), _whole_vmem(), _rows(D_MODEL), _class_spec(1)] + [_rows(512)] * 3 + [_whole_vmem()] * 2,
        out_specs=[_rows(512)] * 2 + [_class_spec(1), _class_spec(F32_LAYOUT)] + [_whole_vmem()] * 3,
        out_shape=[f512, f512, _class_shape(S, 1, F32), _class_shape(S, F32_LAYOUT, F32),
                   jax.ShapeDtypeStruct((N_SHARD, OUT_BLK, D_MODEL), F32), vec, vec],
        scratch_shapes=[pltpu.VMEM((4, TS, 128), F32)],
        compiler_params=_params(),
    )(dx1, w_out, mixb, ya, hf, hb, gate, g_rnn, g_attn)


def _rnn_bwd(xr, h, dhs, conv_w, conv_b, wa, ba, wx, bx, lam, reverse, rider=None):
    S = xr.shape[0]
    nt = S // TS
    order = (lambda i: i) if reverse else (lambda i: nt - 1 - i)
    per = TS // SUB
    last = S // SUB - 1
    if reverse:
        h_halo = pl.BlockSpec((SUB, 512), lambda i: (jnp.minimum((order(i) + 1) * per, last), 0))
    else:
        h_halo = pl.BlockSpec((SUB, 512), lambda i: (jnp.maximum(order(i) * per - 1, 0), 0))
    tile = pl.BlockSpec((TS, 512), lambda i: (order(i), 0))

    def body(x_ref, xp_ref, xn_ref, h_ref, hh_ref, dh_ref, cw_ref, cb_ref, wa_ref, ba_ref, wx_ref, bx_ref, lam_ref,
             dxc_ref, dwa_ref, dwx_ref, dvec_ref, a_s, g_s, carry, edge):
        i = pl.program_id(0)
        t = order(i)

        @pl.when(i == 0)
        def _():
            carry[...] = jnp.zeros_like(carry)
            edge[...] = jnp.zeros_like(edge)
            dwa_ref[...] = jnp.zeros_like(dwa_ref)
            dwx_ref[...] = jnp.zeros_like(dwx_ref)
            dvec_ref[...] = jnp.zeros_like(dvec_ref)

        prev8 = jnp.where(t > 0, xp_ref[...], 0.0)
        next8 = jnp.where(t < nt - 1, xn_ref[...], 0.0)
        xc = _conv_fwd(x_ref[...], prev8, next8, cw_ref[...], cb_ref[...])
        xcb, r, gi, cl, a, mult, inv_mult = _lru_gates(xc, wa_ref, ba_ref[...], wx_ref, bx_ref[...], lam_ref[...])
        hv = h_ref[...]
        if reverse:
            a_s[...] = _shifted(a, edge[...], None, -1)
            edge[...] = a[TS - SUB:TS, :]
            hh = jnp.where(t < nt - 1, hh_ref[...], 0.0)
            h_prev = _shifted(hv, None, hh, 1)
        else:
            a_s[...] = _shifted(a, None, edge[...], 1)
            edge[...] = a[0:SUB, :]
            hh = jnp.where(t > 0, hh_ref[...], 0.0)
            h_prev = _shifted(hv, hh, None, -1)
        _scan_tile(a_s, dh_ref, g_s, carry, not reverse)
        g = g_s[...]
        da = g * h_prev
        gm = g * mult
        d_i = gm * xc
        dmult = g * gi * xc
        dla = da * a - dmult * (a * a) * inv_mult
        d_r = dla * cl
        dpre_r = d_r * r * (1.0 - r)
        dpre_i = d_i * gi * (1.0 - gi)
        dprb = dpre_r.astype(BF16)
        dpib = dpre_i.astype(BF16)
        dwa_ref[...] += _dot_tn(xcb, dprb)
        dwx_ref[...] += _dot_tn(xcb, dpib)
        dvec_ref[0:1, :] += jnp.sum(dpre_r, axis=0, keepdims=True)
        dvec_ref[1:2, :] += jnp.sum(dpre_i, axis=0, keepdims=True)
        dvec_ref[2:3, :] += jnp.sum(dla * r, axis=0, keepdims=True)
        dvec_ref[3:4, :] = dvec_ref[2:3, :] * (LRU_C * _sigmoid(-lam_ref[...]))
        dxc_ref[...] = gm * gi + _dot_nt(dprb, wa_ref[...]) + _dot_nt(dpib, wx_ref[...])

    sq = jax.ShapeDtypeStruct((D_RNN, D_RNN), F32)
    return _call(
        body, "rnn_bwd_rev" if reverse else "rnn_bwd_fwd", (nt,),
        _halo_specs(S, order) + [tile, h_halo, tile] + [_whole_vmem()] * 7,
        [tile, _whole_vmem(), _whole_vmem(), _whole_vmem()],
        [jax.ShapeDtypeStruct((S, 512), F32), sq, sq, jax.ShapeDtypeStruct((SUB, 512), F32)],
        [pltpu.VMEM((TS, 512), F32), pltpu.VMEM((TS, 512), F32), pltpu.VMEM((SUB, 512), F32),
         pltpu.VMEM((SUB, 512), F32)],
        (xr, xr, xr, h, h, dhs, conv_w, conv_b, wa, ba, wx, bx, lam), rider)


def _inproj_bwd(x, dx1, xr, dxc_f, dxc_b, dgate, dq3, dk3, dv3, g1, conv_w, w_in, rider=None):
    S = x.shape[0]
    tb = TS_INPROJ_BWD
    nt = S // tb
    ident = lambda i: i

    def body(x_ref, dx1_ref, xr_ref, xrp_ref, xrn_ref, cf_ref, cfp_ref, cfn_ref, cb_ref, cbp_ref, cbn_ref, dgate_ref,
             dq1, dq2, dq3_, dk1, dk2, dk3_, dv1, dv2, dv3_, g_ref, cw_ref, w_ref,
             dx_ref, dw_ref, dg_ref, dcw_ref, s_ref, w_full, dw_full, sems):
        i = pl.program_id(0)

        def blocks(full, blocked, k0):
            return [(full.at[:, j * IN_BLK:(j + 1) * IN_BLK], blocked.at[j], sems.at[k0 + j]) for j in range(N_SHARD)]

        @pl.when(i == 0)
        def _():
            copies = [pltpu.make_async_copy(src, dst, sem) for dst, src, sem in blocks(w_full, w_ref, 0)]
            for cp in copies:
                cp.start()
            for cp in copies:
                cp.wait()
            dw_full[...] = jnp.zeros_like(dw_full)
            dg_ref[...] = jnp.zeros_like(dg_ref)
            dcw_ref[...] = jnp.zeros_like(dcw_ref)

        first, last = i > 0, i < nt - 1
        dxc = cf_ref[...] + cb_ref[...]
        dxc_p = jnp.where(first, cfp_ref[...] + cbp_ref[...], 0.0)
        dxc_n = jnp.where(last, cfn_ref[...] + cbn_ref[...], 0.0)
        cw = cw_ref[...]
        dxr = (_shifted(dxc, dxc_p, dxc_n, 2) * cw[0:1, :] + _shifted(dxc, dxc_p, dxc_n, 1) * cw[1:2, :]
               + dxc * cw[2:3, :] + _shifted(dxc, dxc_p, dxc_n, -1) * cw[3:4, :])
        xrv = xr_ref[...]
        xr_p = jnp.where(first, xrp_ref[...], 0.0)
        xr_n = jnp.where(last, xrn_ref[...], 0.0)
        for k, off in enumerate((-2, -1, 0, 1)):
            dcw_ref[k:k + 1, :] += jnp.sum(dxc * _shifted(xrv, xr_p, xr_n, off), axis=0, keepdims=True)
        dcw_ref[4:5, :] += jnp.sum(dxc, axis=0, keepdims=True)

        def total(a, b, c_):
            return a[0].astype(F32) + _merge_classes(b, s_ref, F32_LAYOUT, c_)

        dproj = jnp.concatenate(
            [dxr, dgate_ref[...], total(dq1, dq2, dq3_), total(dk1, dk2, dk3_), total(dv1, dv2, dv3_)],
            axis=-1).astype(BF16)
        xv = x_ref[...]
        rstd, xh = _rms(xv)
        hb = (xh * g_ref[...]).astype(BF16)
        dh = _dot_nt(dproj, w_full[...])
        dw_full[...] += _dot_tn(hb, dproj)
        dxn, dg = _rms_bwd(dh, g_ref[...], xh, rstd)
        dg_ref[...] += dg
        dx_ref[...] = dx1_ref[...] + dxn

        @pl.when(i == nt - 1)
        def _():
            copies = [pltpu.make_async_copy(src, dst, sem) for src, dst, sem in blocks(dw_full, dw_ref, N_SHARD)]
            for cp in copies:
                cp.start()
            for cp in copies:
                cp.wait()

    halo = _halo_specs(S, ident, tb)
    return _call(
        body, "inproj_bwd", (nt,),
        [_rows(D_MODEL, tb), _rows(D_MODEL, tb)] + halo * 3 + [_rows(512, tb)]
        + [_class_spec(1, tb), _class_spec(F32_LAYOUT, tb), _class_spec(F32_LAYOUT, tb)] * 3 + [_whole_vmem()] * 2 + [ANY],
        [_rows(D_MODEL, tb), ANY, _whole_vmem(), _whole_vmem()],
        [jax.ShapeDtypeStruct((S, D_MODEL), F32), jax.ShapeDtypeStruct((N_SHARD, D_MODEL, IN_BLK), F32),
         jax.ShapeDtypeStruct((1, D_MODEL), F32), jax.ShapeDtypeStruct((SUB, 512), F32)],
        [pltpu.VMEM((4, tb, 128), F32), pltpu.VMEM((D_MODEL, D_IN), BF16), pltpu.VMEM((D_MODEL, D_IN), F32),
         pltpu.SemaphoreType.DMA((2 * N_SHARD,))],
        (x, dx1, xr, xr, xr, dxc_f, dxc_f, dxc_f, dxc_b, dxc_b, dxc_b, dgate, *dq3, *dk3, *dv3, g1, conv_w, w_in), rider)


def _dw_matmul(a, b, a_cols, b_cols, name):
    S = a.shape[0]
    tk = min(S, TK_DW)
    a_shared = a.shape[1] == a_cols
    b_shared = b.shape[1] == b_cols

    def body(a_ref, b_ref, o_ref):
        @pl.when(pl.program_id(1) == 0)
        def _():
            o_ref[...] = jnp.zeros_like(o_ref)
        o_ref[0] += _dot_tn(a_ref[...], b_ref[...])

    return pl.pallas_call(
        body, grid=(N_SHARD, S // tk), name=name,
        in_specs=[pl.BlockSpec((tk, a_cols), (lambda j, k: (k, 0)) if a_shared else (lambda j, k: (k, j))),
                  pl.BlockSpec((tk, b_cols), (lambda j, k: (k, 0)) if b_shared else (lambda j, k: (k, j)))],
        out_specs=pl.BlockSpec((1, a_cols, b_cols), lambda j, k: (j, 0, 0)),
        out_shape=jax.ShapeDtypeStruct((N_SHARD, a_cols, b_cols), F32),
        compiler_params=_params(2),
    )(a, b)


def _t5_bucket_np(rel):
    nb = N_BUCKETS // 2
    max_exact = nb // 2
    ret = np.where(rel > 0, nb, 0)
    n = np.abs(rel)
    nf = np.maximum(n, 1).astype(np.float32)
    large = max_exact + (np.log(nf / np.float32(max_exact)) / np.float32(math.log(MAX_DISTANCE / max_exact))
                         * np.float32(nb - max_exact)).astype(np.int32)
    large = np.minimum(large, nb - 1)
    return ret + np.where(n < max_exact, n, large)


_VARIANT_OFFSETS = (-HALF_WIN,) * 3


def _band_index():
    kk = np.arange(K_WIN)[None, :]
    ql = np.arange(Q_BLK)[:, None]
    rel = np.stack([kk - ql + off for off in _VARIANT_OFFSETS])
    band = np.abs(rel) <= HALF_WIN
    inside = np.stack([np.broadcast_to(kk >= HALF_WIN, band[0].shape), np.ones_like(band[0]),
                       np.broadcast_to(kk < K_WIN - HALF_WIN, band[0].shape)])
    return rel, band & inside


def _bucket_tables(dil):
    rel, valid = _band_index()
    bucket = _t5_bucket_np(np.clip(rel, -HALF_WIN, HALF_WIN) * dil)
    return np.where(valid, bucket, -1).astype(np.int32)


def _bias_mats(rel_bias, rider=None):
    tables = [_bucket_tables(d) for d in DILATIONS]
    used = [sorted(set(t[t >= 0].tolist())) for t in tables]

    def one_pattern(rb_ref, t_ref, o_ref, buckets):
        bk = t_ref[1]
        for h in range(N_HEADS):
            acc = jnp.full((Q_BLK, K_WIN), NEG_INF, F32)
            for b in buckets:
                acc = jnp.where(bk == b, rb_ref[b, h], acc)
            o_ref[1, h] = acc
            for var in (0, 2):
                o_ref[var, h] = jnp.where(t_ref[var] >= 0, acc, NEG_INF)

    def body(rb_ref, t1, t2, t3, o1, o2, o3):
        for i, (t_ref, o_ref) in enumerate(((t1, o1), (t2, o2), (t3, o3))):
            pl.when(pl.program_id(0) == i)(functools.partial(one_pattern, rb_ref, t_ref, o_ref, used[i]))

    shp = jax.ShapeDtypeStruct((3, N_HEADS, Q_BLK, K_WIN), F32)
    return _call(
        body, "bias_tables", (len(DILATIONS),), [pl.BlockSpec(memory_space=pltpu.SMEM)] + [_whole_vmem()] * 3,
        [_whole_vmem()] * 3, [shp] * 3, [], (rel_bias, *[jnp.asarray(t) for t in tables]), rider)


def _variant(qb, nq):
    return jnp.where(qb == 0, 0, jnp.where(qb == nq - 1, 2, 1))


def _win_start(qb):
    return pl.multiple_of(qb * Q_BLK, Q_BLK)


def _fill_padded(src_ref, pad_ref):
    L = src_ref.shape[0]
    edge = jnp.zeros((HALF_WIN, 128), pad_ref.dtype)
    pad_ref[0:HALF_WIN, :] = edge
    pad_ref[HALF_WIN:HALF_WIN + L, :] = src_ref[...]
    pad_ref[HALF_WIN + L:2 * HALF_WIN + L, :] = edge


INNER = {1: 1, 4: 1, 16: 4}


def _attn_layout(dil, L):
    inner = INNER[dil]
    n_outer = dil // inner
    nsub = min(ATTN_SUB // inner, L // Q_BLK)
    qt = nsub * Q_BLK
    grid = (4, n_outer, L // qt)
    qspec = pl.BlockSpec((inner, None, qt, 128), lambda hp, r, s: (0, r, s, hp))
    kspec = pl.BlockSpec((inner, None, L, 128), lambda hp, r, s: (0, r, 0, hp))
    bspec = pl.BlockSpec((3, 2, Q_BLK, K_WIN), lambda hp, r, s: (0, hp, 0, 0))
    kfspec = pl.BlockSpec((None, inner * L, 128), lambda hp, r, s: (r, 0, hp))
    qfspec = kfspec if inner > 1 else pl.BlockSpec((None, qt, 128), lambda hp, r, s: (r, s, hp))
    fshape = jax.ShapeDtypeStruct((n_outer, inner * L, D_ATTN), F32)
    view = lambda t: t.reshape(inner, n_outer, L, D_ATTN)

    def qrows(m, sub):
        if inner == 1:
            return (slice(sub * Q_BLK, (sub + 1) * Q_BLK), slice(None))
        first = (pl.program_id(2) * nsub + sub) * Q_BLK
        return (pl.ds(m + inner * first, Q_BLK, stride=inner), slice(None))

    def krows(m):
        if inner == 1:
            return (slice(None), slice(None))
        return (pl.ds(m, L, stride=inner), slice(None))

    return inner, nsub, grid, qspec, kspec, bspec, qfspec, kfspec, fshape, view, qrows, krows


def _head_masks():
    lane = lax.broadcasted_iota(jnp.int32, (Q_BLK, 128), 1)
    return lane < HEAD_DIM


def _attn_fwd(q, k, v, bias):
    dil, L, _ = q.shape
    nq = L // Q_BLK
    inner, nsub, grid, qspec, kspec, bspec, qfspec, kfspec, fshape, view, qrows, krows = _attn_layout(dil, L)

    def body(q_ref, k_ref, v_ref, b_ref, o_ref, l_ref, kp, vp):
        step = pl.program_id(2)

        @pl.when(step == 0)
        def _():
            for m in range(inner):
                _fill_padded(k_ref.at[m], kp.at[m])
                _fill_padded(v_ref.at[m], vp.at[m])

        h0 = _head_masks()
        for m, sub in [(m, sub) for m in range(inner) for sub in range(nsub)]:
            qb = step * nsub + sub
            st = _win_start(qb)
            var = _variant(qb, nq)
            kw = kp[m, pl.ds(st, K_WIN), :]
            vw = vp[m, pl.ds(st, K_WIN), :]
            qs = q_ref[m, sub * Q_BLK:(sub + 1) * Q_BLK, :] * ATTN_SCALE
            zq = jnp.zeros_like(qs)
            q2 = jnp.concatenate([jnp.where(h0, qs, zq), jnp.where(h0, zq, qs)], axis=0)
            s = _dot_nt(q2, kw) + b_ref[var].reshape(2 * Q_BLK, K_WIN)
            top = jnp.max(s, axis=-1, keepdims=True)
            p = jnp.exp(s - top)
            l = jnp.sum(p, axis=-1, keepdims=True)
            out = _dot(p.astype(BF16), vw) / l
            lse = top + jnp.log(l)
            o_ref[qrows(m, sub)] = jnp.where(h0, out[0:Q_BLK], out[Q_BLK:2 * Q_BLK])
            l_ref[qrows(m, sub)] = jnp.where(h0, lse[0:Q_BLK], lse[Q_BLK:2 * Q_BLK])

    return pl.pallas_call(
        body, grid=grid, name=f"attn_fwd_d{dil}",
        in_specs=[qspec, kspec, kspec, bspec], out_specs=[qfspec, qfspec], out_shape=[fshape, fshape],
        scratch_shapes=[pltpu.VMEM((inner, L + 2 * HALF_WIN, 128), BF16)] * 2,
        compiler_params=_params(3),
    )(view(q), view(k), view(v), bias)


def _attn_bwd(q, k, v, bias, do, o, lse, rider=None):
    dil, L, _ = q.shape
    nq = L // Q_BLK
    inner, nsub, grid, qspec, kspec, bspec, qfspec, kfspec, fshape, view, qrows, krows = _attn_layout(dil, L)
    nstep = grid[2]

    def body(q_ref, k_ref, v_ref, b_ref, do_ref, o_ref, l_ref, dq_ref, dk_ref, dv_ref, db_ref, db_s,
             kp, vp, dkp, dvp, carry):
        hp, step = pl.program_id(0), pl.program_id(2)
        first = (hp == 0) & (pl.program_id(1) == 0) & (step == 0)
        last = (hp == grid[0] - 1) & (pl.program_id(1) == grid[1] - 1) & (step == nstep - 1)

        @pl.when(first)
        def _():
            db_s[...] = jnp.zeros_like(db_s)

        @pl.when(step == 0)
        def _():
            for m in range(inner):
                _fill_padded(k_ref.at[m], kp.at[m])
                _fill_padded(v_ref.at[m], vp.at[m])
            carry[...] = jnp.zeros_like(carry)

        h0 = _head_masks()
        for m, sub in [(m, sub) for m in range(inner) for sub in range(nsub)]:
            if sub == 0:
                carry_k, carry_v = carry[m, 0], carry[m, 1]
            qb = step * nsub + sub
            st = _win_start(qb)
            var = _variant(qb, nq)
            kw = kp[m, pl.ds(st, K_WIN), :]
            vw = vp[m, pl.ds(st, K_WIN), :]
            qs = q_ref[m, sub * Q_BLK:(sub + 1) * Q_BLK, :] * ATTN_SCALE
            dof = do_ref[qrows(m, sub)]
            dob = dof.astype(BF16)
            prod = dof * o_ref[qrows(m, sub)]
            lsev = l_ref[qrows(m, sub)]
            zq, zd = jnp.zeros_like(qs), jnp.zeros_like(dob)
            q2 = jnp.concatenate([jnp.where(h0, qs, zq), jnp.where(h0, zq, qs)], axis=0)
            do2 = jnp.concatenate([jnp.where(h0, dob, zd), jnp.where(h0, zd, dob)], axis=0)
            lse2 = jnp.concatenate([lsev[:, 0:1], lsev[:, HEAD_DIM:HEAD_DIM + 1]], axis=0)
            dd2 = jnp.concatenate([jnp.sum(jnp.where(h0, prod, 0.0), axis=-1, keepdims=True),
                                   jnp.sum(jnp.where(h0, 0.0, prod), axis=-1, keepdims=True)], axis=0)
            s = _dot_nt(q2, kw) + b_ref[var].reshape(2 * Q_BLK, K_WIN)
            p = jnp.exp(s - lse2)
            ds = p * (_dot_nt(do2, vw) - dd2)
            db_s[pl.ds(hp * 2, 2)] += ds.reshape(2, Q_BLK, K_WIN)
            dsb = ds.astype(BF16)
            dv_acc = _dot_tn(p.astype(BF16), do2)
            dk_acc = _dot_tn(dsb, q2)
            dq2 = _dot(dsb, kw) * ATTN_SCALE
            dq_ref[qrows(m, sub)] = jnp.where(h0, dq2[0:Q_BLK], dq2[Q_BLK:2 * Q_BLK]).astype(dq_ref.dtype)
            dkp[m, pl.ds(st, Q_BLK), :] = carry_k + dk_acc[0:Q_BLK]
            dvp[m, pl.ds(st, Q_BLK), :] = carry_v + dv_acc[0:Q_BLK]
            carry_k, carry_v = dk_acc[Q_BLK:K_WIN], dv_acc[Q_BLK:K_WIN]
            if sub == nsub - 1:
                carry[m, 0] = carry_k
                carry[m, 1] = carry_v

        @pl.when(step == nstep - 1)
        def _():
            for m in range(inner):
                dkp[m, L:L + Q_BLK, :] = carry[m, 0]
                dvp[m, L:L + Q_BLK, :] = carry[m, 1]
                dk_ref[krows(m)] = dkp[m, HALF_WIN:HALF_WIN + L, :].astype(dk_ref.dtype)
                dv_ref[krows(m)] = dvp[m, HALF_WIN:HALF_WIN + L, :].astype(dv_ref.dtype)

        @pl.when(last)
        def _():
            db_ref[...] = db_s[...]

    dbshape = (N_HEADS, Q_BLK, K_WIN)
    gshape = jax.ShapeDtypeStruct(fshape.shape, BF16 if inner == 1 else F32)
    return _call(
        body, f"attn_bwd_d{dil}", grid,
        [qspec, kspec, kspec, bspec, qfspec, qfspec, qfspec],
        [qfspec, kfspec, kfspec, _whole_vmem()],
        [gshape, gshape, gshape, jax.ShapeDtypeStruct(dbshape, F32)],
        [pltpu.VMEM(dbshape, F32)] + [pltpu.VMEM((inner, L + 2 * HALF_WIN, 128), BF16)] * 2
        + [pltpu.VMEM((inner, L + 2 * HALF_WIN, 128), F32)] * 2 + [pltpu.VMEM((inner, 2, Q_BLK, 128), F32)],
        (view(q), view(k), view(v), bias, do, o, lse), rider)


def _bucket_onehots(dil):
    m = np.zeros((3, K_WIN, N_BUCKETS), np.float32)
    for var, off in enumerate(_VARIANT_OFFSETS):
        for rel in range(-HALF_WIN, HALF_WIN + 1):
            col = (rel - off + Q_BLK - 1) % K_WIN
            m[var, col, int(_t5_bucket_np(np.asarray(rel * dil)))] = 1.0
    return jnp.asarray(m)


def _bias_grad(dbs):
    onehots = [_bucket_onehots(d) for d in DILATIONS]
    flip = jnp.asarray(np.eye(Q_BLK, dtype=np.float32)[::-1].copy())

    def body(d1, d2, d3, m1, m2, m3, flip_ref, out_ref):
        hp = lax.Precision.HIGHEST
        acc = jnp.zeros((N_HEADS, N_BUCKETS), F32)
        for d_ref, m_ref in ((d1, m1), (d2, m2), (d3, m3)):
            rows = []
            for h in range(N_HEADS):
                xrev = jnp.dot(flip_ref[...], d_ref[h], precision=hp, preferred_element_type=F32)
                y = pltpu.roll(xrev, 0, 1, stride=1, stride_axis=0)
                rows.append(jnp.sum(y, axis=0, keepdims=True))
            acc = acc + jnp.dot(jnp.concatenate(rows, axis=0), m_ref[1], precision=hp, preferred_element_type=F32)
        out_ref[...] = acc

    return pl.pallas_call(
        body, name="bias_grad", out_shape=jax.ShapeDtypeStruct((N_HEADS, N_BUCKETS), F32),
        compiler_params=_params(0),
    )(*dbs, *onehots, flip)


def _block_diag(w):
    eye = jnp.eye(N_RNN_BLOCKS, dtype=w.dtype)
    return jnp.einsum("ncd,nm->ncmd", w, eye).reshape(D_RNN, D_RNN).astype(BF16)


def _diag_blocks(dense):
    d = dense.reshape(N_RNN_BLOCKS, RNN_BLOCK, N_RNN_BLOCKS, RNN_BLOCK)
    return jnp.stack([d[n, :, n, :] for n in range(N_RNN_BLOCKS)])


EARLY = ("w_out", "w_up", "w_down")


def _local_step(x, target, p, shards=None):
    p = dict(p)
    first = None if shards is None else _gather_rider(["w_in"], [shards["w_in"]], shards["conv_w"])
    biases, got = _bias_mats(p["rel_bias"], first)
    if shards is not None:
        p["w_in"] = got[0]
        p["conv_w"] = jnp.transpose(got[1], (1, 0, 2)).reshape(4, D_RNN)
    lru = {}
    for dname in ("fwd", "bwd"):
        lru[dname] = (_block_diag(p["lru_wa_" + dname]), p["lru_ba_" + dname], _block_diag(p["lru_wx_" + dname]),
                      p["lru_bx_" + dname], p["lru_lam_" + dname])

    def gather(name):
        return None if shards is None else _gather_rider([name], [shards[name]])

    (xr, gate, *qkv), got = _inproj_fwd(x, p["attn_norm_g"], p["w_in"], gather("w_out"))
    p.update(zip(["w_out"], got))
    qs, ks, vs = qkv[0:3], qkv[3:6], qkv[6:9]
    hf, got = _rnn_fwd(xr, p["conv_w"], p["conv_b"], *lru["fwd"], reverse=False, rider=gather("w_up"))
    p.update(zip(["w_up"], got))
    hb, got = _rnn_fwd(xr, p["conv_w"], p["conv_b"], *lru["bwd"], reverse=True, rider=gather("w_down"))
    p.update(zip(["w_down"], got))
    outs, lses = [], []
    for q, k, v, bias in zip(qs, ks, vs, biases):
        o, l = _attn_fwd(q, k, v, bias)
        outs.append(o)
        lses.append(l)
    x1, mixb, *yl = _mix_fwd(outs, lses, hf, hb, gate, x, p["norm_rnn_g"], p["norm_attn_g"], p["w_out"])
    yas, lsts = [yl[0], yl[1], yl[1]], [yl[2], yl[3], yl[3]]
    dx1, h2b, a2b, dub, dx2b, loss_vec, dg_fin, dg_mlp = _mlp_fwd_bwd(
        x1, target, p["mlp_norm_g"], p["final_norm_g"], p["w_up"], p["w_down"])
    dhs, dgate, dya1, dya4, dw_out, dg_rnn, dg_attn = _mix_bwd(dx1, p["w_out"], mixb, yas[0], hf, hb, gate,
                                                               p["norm_rnn_g"], p["norm_attn_g"])
    dyas = [dya1, dya4, dya4]
    dw_up = _dw_matmul(h2b, dub, D_MODEL, FF_BLK, "dw_up")
    dw_down = _dw_matmul(a2b, dx2b, FF_BLK, D_MODEL, "dw_down")
    early = [dw_out, dw_up, dw_down]
    dqs, dks, dvs, dbs = [], [], [], []
    for i, (q, k, v, bias, dya, ya, lse) in enumerate(zip(qs, ks, vs, biases, dyas, yas, lsts)):
        rider = None
        if shards is not None:
            make = (lambda: _pair_exchange_rider(EARLY, early), lambda: _chip_exchange_rider(early),
                    lambda: _pair_share_rider(EARLY, early))[i]
            rider = make()
        (dq, dk, dv, db), got = _attn_bwd(q, k, v, bias, dya, ya, lse, rider)
        if shards is not None and i == 0:
            core = lax.axis_index("c").reshape(1).astype(jnp.int32)
            early = [_pair_add(core, g, o, "grad_pair_add_" + n) for n, g, o in zip(EARLY, early, got)]
        elif shards is not None and i == 1:
            early = [_chip_sum(t, "grad_chip_sum_" + n) for n, t in zip(EARLY, got)]
        elif shards is not None:
            early = got
        dqs.append(dq)
        dks.append(dk)
        dvs.append(dv)
        dbs.append(db)
    d_rel_bias = _bias_grad(dbs).T
    (dxc_f, dwa_f, dwx_f, dvec_f), _ = _rnn_bwd(xr, hf, dhs, p["conv_w"], p["conv_b"], *lru["fwd"], reverse=False)
    small = {
        "lru_wa_fwd": _diag_blocks(dwa_f), "lru_ba_fwd": dvec_f[0:1], "lru_wx_fwd": _diag_blocks(dwx_f),
        "lru_bx_fwd": dvec_f[1:2], "lru_lam_fwd": dvec_f[3:4],
        "rel_bias": d_rel_bias, "norm_rnn_g": dg_rnn, "norm_attn_g": dg_attn,
        "mlp_norm_g": dg_mlp, "final_norm_g": dg_fin,
    }
    loss_local = (0.5 / D_MODEL) * jnp.sum(loss_vec)
    rider = None
    if shards is not None:
        rider = _small_gather_rider(_pack([small[n].reshape(shp) for n, shp in SMALL if n in small]
                                          + [loss_local.reshape(1)]))
    (dxc_b, dwa_b, dwx_b, dvec_b), gathered = _rnn_bwd(xr, hb, dhs, p["conv_w"], p["conv_b"], *lru["bwd"], reverse=True,
                                                       rider=rider)
    grad_x, dw_in, dg1, dconv = _inproj_bwd(x, dx1, xr, dxc_f, dxc_b, dgate, dqs, dks, dvs,
                                            p["attn_norm_g"], p["conv_w"], p["w_in"])[0]
    last = {"lru_wa_bwd": _diag_blocks(dwa_b), "lru_ba_bwd": dvec_b[0:1], "lru_wx_bwd": _diag_blocks(dwx_b),
            "lru_bx_bwd": dvec_b[1:2], "lru_lam_bwd": dvec_b[3:4],
            "attn_norm_g": dg1, "conv_w": dconv[0:4], "conv_b": dconv[4:5]}
    if shards is None:
        big = {"w_in": dw_in, "w_out": dw_out, "w_up": dw_up, "w_down": dw_down}
        return loss_local, grad_x, {**small, **last}, None, big, {}
    return loss_local, grad_x, last, gathered[0], {"w_in": dw_in}, dict(zip(EARLY, early))


BIG = ("w_in", "w_out", "w_up", "w_down")
BIG_SHARD = {"w_in": (D_MODEL, IN_BLK), "w_out": (OUT_BLK, D_MODEL), "w_up": (D_MODEL, FF_BLK), "w_down": (FF_BLK, D_MODEL)}
N_BIG = len(BIG)
N_CHIP_PEERS = 3
ANY = pl.BlockSpec(memory_space=pl.ANY)


def _place():
    x, y, c = lax.axis_index("x"), lax.axis_index("y"), lax.axis_index("c")
    chips = [(1 - x, y), (x, 1 - y), (1 - x, 1 - y)]
    return x, y, c, chips


def _remote(src, dst, send_sem, recv_sem, dev):
    return pltpu.make_async_remote_copy(src_ref=src, dst_ref=dst, send_sem=send_sem, recv_sem=recv_sem,
                                        device_id=dev, device_id_type=MESH)


def _staged_start(srcs, bufs, sems):
    legs = [pltpu.make_async_copy(s, b, sems.at[i]) for i, (s, b) in enumerate(zip(srcs, bufs))]
    for cp in legs:
        cp.start()
    return legs


def _staged_finish(legs, bufs, dsts, sems):
    out = []
    for i, (leg, b, d) in enumerate(zip(legs, bufs, dsts)):
        leg.wait()
        cp = pltpu.make_async_copy(b, d, sems.at[i])
        cp.start()
        out.append(cp)
    return out


class _Rider:
    def __init__(self, inputs, out_shape, scratch, first, late, last):
        self.inputs, self.out_shape, self.scratch = list(inputs), list(out_shape), list(scratch)
        self.first, self.late, self.last = first, late, last


def _call(body, name, grid, in_specs, out_specs, out_shape, scratch, operands, rider=None):
    n_grid = len(grid)
    if rider is None:
        res = pl.pallas_call(body, grid=grid, name=name, in_specs=in_specs, out_specs=out_specs, out_shape=out_shape,
                             scratch_shapes=scratch, compiler_params=_params(n_grid))(*operands)
        return list(res), []
    n_in, n_out, n_scr = len(in_specs), len(out_specs), len(scratch)
    ri, ro = len(rider.inputs), len(rider.out_shape)
    nsteps = int(np.prod(grid))
    late_step = max(nsteps - 3, 1)

    def wrapped(*refs):
        a, b = n_in, n_in + ri
        c, d = b + n_out, b + n_out + ro
        e = d + n_scr
        mine = refs[:a] + refs[b:c] + refs[d:e]
        theirs = (refs[a:b], refs[c:d], refs[e:])
        step = pl.program_id(0)
        for ax in range(1, n_grid):
            step = step * grid[ax] + pl.program_id(ax)
        pl.when(step == 0)(lambda: rider.first(*theirs))
        pl.when(step == late_step)(lambda: rider.late(*theirs))
        body(*mine)
        pl.when(step == nsteps - 1)(lambda: rider.last(*theirs))

    res = pl.pallas_call(
        wrapped, grid=grid, name=name, in_specs=list(in_specs) + [ANY] * ri, out_specs=list(out_specs) + [ANY] * ro,
        out_shape=list(out_shape) + rider.out_shape, scratch_shapes=list(scratch) + rider.scratch,
        compiler_params=_params(n_grid),
    )(*operands, *rider.inputs)
    return list(res[:n_out]), list(res[n_out:])


def _run_rider(rider, name):
    ri, ro = len(rider.inputs), len(rider.out_shape)

    def body(*refs):
        parts = (refs[:ri], refs[ri:ri + ro], refs[ri + ro:])
        rider.first(*parts)
        rider.late(*parts)
        rider.last(*parts)

    return list(pl.pallas_call(
        body, name=name, in_specs=[ANY] * ri, out_specs=[ANY] * ro, out_shape=rider.out_shape, scratch_shapes=rider.scratch,
        compiler_params=pltpu.CompilerParams(has_side_effects=True, vmem_limit_bytes=VMEM_LIMIT),
    )(*rider.inputs))


def _nothing(ins, outs, scr):
    return None


def _gather_rider(names, shards, conv_w=None):
    n = len(names)
    items = n + (conv_w is not None)
    halves = [BIG_SHARD[nm][0] // 2 for nm in names]

    def parts(ins, outs, scr):
        x, y, c, chips = _place()
        return x, y, c, chips, 2 * x + y, (x, y, 1 - c), scr[:8], scr[8:]

    def piece(outs, w, chip, core_half):
        return outs[w].at[chip, pl.ds(core_half * halves[w], halves[w])]

    def ici(ins, outs, sems, w, k, chip_xy, c, me):
        return _remote(ins[w].at[pl.ds(c * halves[w], halves[w])], piece(outs, w, me, c),
                       sems[0].at[w, k], sems[1].at[w, k], (*chip_xy, c))

    def first(ins, outs, scr):
        x, y, c, chips, me, sibling, sems, bufs = parts(ins, outs, scr)
        legs = _staged_start(ins, bufs, sems[6])
        for w in range(n):
            for k, chip_xy in enumerate(chips):
                ici(ins, outs, sems, w, k, chip_xy, c, me).start()
        if conv_w is not None:
            for k, (px, py) in enumerate(chips):
                _remote(ins[n], outs[n].at[me], sems[4].at[k], sems[5].at[k], (px, py, c)).start()
        _staged_finish(legs, bufs, [o.at[me] for o in outs], sems[7])

    def late(ins, outs, scr):
        x, y, c, chips, me, sibling, sems, bufs = parts(ins, outs, scr)
        for w in range(n):
            for k, (px, py) in enumerate(chips):
                landed = piece(outs, w, 2 * px + py, c)
                _remote(landed, landed, sems[0].at[w, k], sems[1].at[w, k], (px, py, c)).wait_recv()
                _remote(landed, landed, sems[2].at[w, k], sems[3].at[w, k], sibling).start()

    def last(ins, outs, scr):
        x, y, c, chips, me, sibling, sems, bufs = parts(ins, outs, scr)
        for w in range(n):
            for k, (px, py) in enumerate(chips):
                other = piece(outs, w, 2 * px + py, 1 - c)
                _remote(other, other, sems[2].at[w, k], sems[3].at[w, k], sibling).wait_recv()
        if conv_w is not None:
            for k, (px, py) in enumerate(chips):
                got = outs[n].at[2 * px + py]
                _remote(got, got, sems[4].at[k], sems[5].at[k], (px, py, c)).wait_recv()
                _remote(ins[n], outs[n].at[me], sems[4].at[k], sems[5].at[k], (px, py, c)).wait_send()
        for i in range(items):
            pltpu.make_async_copy(bufs[i], outs[i].at[me], sems[7].at[i]).wait()
        for w in range(n):
            for k, (px, py) in enumerate(chips):
                ici(ins, outs, sems, w, k, (px, py), c, me).wait_send()
                landed = piece(outs, w, 2 * px + py, c)
                _remote(landed, landed, sems[2].at[w, k], sems[3].at[w, k], sibling).wait_send()

    out_shape = [jax.ShapeDtypeStruct((N_SHARD,) + BIG_SHARD[nm], BF16) for nm in names]
    stage = [pltpu.VMEM(BIG_SHARD[nm], BF16) for nm in names]
    inputs = list(shards)
    if conv_w is not None:
        out_shape.append(jax.ShapeDtypeStruct((N_SHARD,) + conv_w.shape, F32))
        stage.append(pltpu.VMEM(conv_w.shape, F32))
        inputs.append(conv_w)
    scratch = ([pltpu.SemaphoreType.DMA((n, N_CHIP_PEERS))] * 4 + [pltpu.SemaphoreType.DMA((N_CHIP_PEERS,))] * 2
               + [pltpu.SemaphoreType.DMA((items,))] * 2 + stage)
    return _Rider(inputs, out_shape, scratch, first, late, last)


def _pair_exchange_rider(names, grads):
    def copies(ins, outs, scr):
        x, y, c, _ = _place()
        out = []
        for w, nm in enumerate(names):
            h = BIG_SHARD[nm][0] // 2
            out.append(_remote(ins[w].at[:, pl.ds((1 - c) * h, h), :], outs[w], scr[0].at[w], scr[1].at[w], (x, y, 1 - c)))
        return out

    def first(ins, outs, scr):
        for cp in copies(ins, outs, scr):
            cp.start()

    def last(ins, outs, scr):
        for cp in copies(ins, outs, scr):
            cp.wait()

    out_shape = [jax.ShapeDtypeStruct((N_SHARD, BIG_SHARD[nm][0] // 2, BIG_SHARD[nm][1]), F32) for nm in names]
    return _Rider(grads, out_shape, [pltpu.SemaphoreType.DMA((len(names),))] * 2, first, _nothing, last)


def _pair_add(core, grad, other, name):
    _, r, cols = grad.shape
    h = r // 2
    th = min(h, 256)
    per = h // th

    def body(c_ref, g_ref, o_ref, out_ref):
        out_ref[...] = (g_ref[...] + o_ref[...]).astype(BF16)

    return pl.pallas_call(
        body, name=name,
        grid_spec=pltpu.PrefetchScalarGridSpec(
            num_scalar_prefetch=1, grid=(N_SHARD, per),
            in_specs=[pl.BlockSpec((1, th, cols), lambda j, i, c_ref: (j, c_ref[0] * per + i, 0)),
                      pl.BlockSpec((1, th, cols), lambda j, i, c_ref: (j, i, 0))],
            out_specs=pl.BlockSpec((1, th, cols), lambda j, i, c_ref: (j, i, 0))),
        out_shape=jax.ShapeDtypeStruct((N_SHARD, h, cols), BF16),
        compiler_params=_params(2),
    )(core, grad, other)


def _chip_exchange_rider(parts):
    n = len(parts)

    def sends(ins, outs, scr):
        x, y, c, chips = _place()
        me = 2 * x + y
        return [_remote(ins[w].at[2 * px + py], outs[w].at[me], scr[0].at[w, k], scr[1].at[w, k], (px, py, c))
                for w in range(n) for k, (px, py) in enumerate(chips)]

    def first(ins, outs, scr):
        x, y, c, chips = _place()
        me = 2 * x + y
        legs = _staged_start([r.at[me] for r in ins], scr[4:], scr[2])
        for cp in sends(ins, outs, scr):
            cp.start()
        _staged_finish(legs, scr[4:], [o.at[me] for o in outs], scr[3])

    def last(ins, outs, scr):
        x, y, c, chips = _place()
        me = 2 * x + y
        for w in range(n):
            for k, (px, py) in enumerate(chips):
                got = outs[w].at[2 * px + py]
                _remote(got, got, scr[0].at[w, k], scr[1].at[w, k], (px, py, c)).wait_recv()
        for cp in sends(ins, outs, scr):
            cp.wait_send()
        for w in range(n):
            pltpu.make_async_copy(scr[4 + w], outs[w].at[me], scr[3].at[w]).wait()

    out_shape = [jax.ShapeDtypeStruct(p.shape, BF16) for p in parts]
    scratch = ([pltpu.SemaphoreType.DMA((n, N_CHIP_PEERS))] * 2 + [pltpu.SemaphoreType.DMA((n,))] * 2
               + [pltpu.VMEM(p.shape[1:], BF16) for p in parts])
    return _Rider(parts, out_shape, scratch, first, _nothing, last)


def _chip_sum(parts, name):
    _, h, cols = parts.shape
    th = min(h, 256)

    def body(p_ref, out_ref):
        acc = p_ref[0].astype(F32)
        for j in range(1, N_SHARD):
            acc = acc + p_ref[j].astype(F32)
        out_ref[...] = acc

    return pl.pallas_call(
        body, name=name, grid=(h // th,),
        in_specs=[pl.BlockSpec((N_SHARD, th, cols), lambda i: (0, i, 0))],
        out_specs=pl.BlockSpec((th, cols), lambda i: (i, 0)),
        out_shape=jax.ShapeDtypeStruct((h, cols), F32),
        compiler_params=_params(),
    )(parts)


def _pair_share_rider(names, halves):
    n = len(names)
    hs = [BIG_SHARD[nm][0] // 2 for nm in names]

    def mine(outs, c):
        return [outs[w].at[pl.ds(c * hs[w], hs[w])] for w in range(n)]

    def first(ins, outs, scr):
        x, y, c, _ = _place()
        legs = _staged_start(ins, scr[4:], scr[2])
        for w, dst in enumerate(mine(outs, c)):
            _remote(ins[w], dst, scr[0].at[w], scr[1].at[w], (x, y, 1 - c)).start()
        _staged_finish(legs, scr[4:], mine(outs, c), scr[3])

    def last(ins, outs, scr):
        x, y, c, _ = _place()
        for w, (theirs, dst) in enumerate(zip(mine(outs, 1 - c), mine(outs, c))):
            _remote(theirs, theirs, scr[0].at[w], scr[1].at[w], (x, y, 1 - c)).wait_recv()
            _remote(ins[w], dst, scr[0].at[w], scr[1].at[w], (x, y, 1 - c)).wait_send()
            pltpu.make_async_copy(scr[4 + w], dst, scr[3].at[w]).wait()

    out_shape = [jax.ShapeDtypeStruct(BIG_SHARD[nm], F32) for nm in names]
    scratch = [pltpu.SemaphoreType.DMA((n,))] * 4 + [pltpu.VMEM((h, BIG_SHARD[nm][1]), F32) for nm, h in zip(names, hs)]
    return _Rider(halves, out_shape, scratch, first, _nothing, last)


N_DEV = 8


def _all_peers(x, y, c):
    return [((1 - x) if fx else x, (1 - y) if fy else y, (1 - c) if fc else c)
            for fx in (0, 1) for fy in (0, 1) for fc in (0, 1) if fx or fy or fc]


def _small_gather_rider(vec):
    def sends(ins, outs, scr):
        x, y, c, _ = _place()
        me = 4 * x + 2 * y + c
        return [_remote(ins[0], outs[0].at[me], scr[0].at[k], scr[1].at[k], dev) for k, dev in enumerate(_all_peers(x, y, c))]

    def first(ins, outs, scr):
        x, y, c, _ = _place()
        legs = _staged_start(ins, scr[4:], scr[2])
        for cp in sends(ins, outs, scr):
            cp.start()
        _staged_finish(legs, scr[4:], [outs[0].at[4 * x + 2 * y + c]], scr[3])

    def last(ins, outs, scr):
        x, y, c, _ = _place()
        for k, (px, py, pc) in enumerate(_all_peers(x, y, c)):
            got = outs[0].at[4 * px + 2 * py + pc]
            _remote(got, got, scr[0].at[k], scr[1].at[k], (px, py, pc)).wait_recv()
        for cp in sends(ins, outs, scr):
            cp.wait_send()
        pltpu.make_async_copy(scr[4], outs[0].at[4 * x + 2 * y + c], scr[3].at[0]).wait()

    scratch = ([pltpu.SemaphoreType.DMA((N_DEV - 1,))] * 2 + [pltpu.SemaphoreType.DMA((1,))] * 2
               + [pltpu.VMEM(vec.shape, F32)])
    return _Rider([vec], [jax.ShapeDtypeStruct((N_DEV,) + vec.shape, F32)], scratch, first, _nothing, last)


def _sum_devices(gathered):
    def body(g_ref, out_ref):
        acc = g_ref[0]
        for j in range(1, N_DEV):
            acc = acc + g_ref[j]
        out_ref[...] = acc

    return pl.pallas_call(body, name="sum_devices", out_shape=jax.ShapeDtypeStruct(gathered.shape[1:], F32),
                          compiler_params=_params(0))(gathered)


def _allreduce_small(vec):
    rows = vec.shape[0]

    def body(v_ref, sum_ref, gat_ref, send, recv, loc_sem):
        x, y, c, chips = _place()
        sibling = (x, y, 1 - c)
        slot = lambda px, py, pc: gat_ref.at[4 * px + 2 * py + pc]
        lc = pltpu.make_async_copy(v_ref, slot(x, y, c), loc_sem)
        lc.start()
        sends = [_remote(v_ref, slot(x, y, c), send.at[0], recv.at[0], sibling)]
        sends += [_remote(v_ref, slot(x, y, c), send.at[1 + k], recv.at[1 + k], (px, py, c))
                  for k, (px, py) in enumerate(chips)]
        for cp in sends:
            cp.start()
        for k, (px, py) in enumerate(chips):
            got = slot(px, py, c)
            _remote(got, got, send.at[1 + k], recv.at[1 + k], (px, py, c)).wait_recv()
            cp = _remote(got, got, send.at[4 + k], recv.at[4 + k], sibling)
            cp.start()
            sends.append(cp)
        got = slot(x, y, 1 - c)
        _remote(got, got, send.at[0], recv.at[0], sibling).wait_recv()
        for k, (px, py) in enumerate(chips):
            got = slot(px, py, 1 - c)
            _remote(got, got, send.at[4 + k], recv.at[4 + k], sibling).wait_recv()
        for cp in sends:
            cp.wait_send()
        lc.wait()
        acc = gat_ref[0]
        for j in range(1, N_DEV):
            acc = acc + gat_ref[j]
        sum_ref[...] = acc

    total, _ = pl.pallas_call(
        body, name="allreduce_small",
        in_specs=[_whole_vmem()], out_specs=[_whole_vmem(), _whole_vmem()],
        out_shape=[jax.ShapeDtypeStruct((rows, 128), F32), jax.ShapeDtypeStruct((N_DEV, rows, 128), F32)],
        scratch_shapes=[pltpu.SemaphoreType.DMA((N_DEV - 1,))] * 2 + [pltpu.SemaphoreType.DMA(())],
        compiler_params=pltpu.CompilerParams(has_side_effects=True, vmem_limit_bytes=VMEM_LIMIT),
    )(vec)
    return total


def _adam_math(w_ref, g_ref, m_ref, v_ref, d_ref, m2_ref, v2_ref):
    c1 = 1.0 - ADAM_B1 ** ADAM_STEP
    c2 = 1.0 - ADAM_B2 ** ADAM_STEP
    gv = g_ref[...]
    m2 = ADAM_B1 * m_ref[...] + (1.0 - ADAM_B1) * gv
    v2 = ADAM_B2 * v_ref[...] + (1.0 - ADAM_B2) * (gv * gv)
    m2_ref[...] = m2
    v2_ref[...] = v2
    d_ref[...] = -ADAM_LR * ((m2 / c1) / (jnp.sqrt(v2 / c2) + ADAM_EPS) + ADAM_WD * w_ref[...])


def _adamw_many(ws, gs, ms, vs):
    n = len(ws)

    def body(*refs):
        for i in range(n):
            _adam_math(*[refs[k * n + i] for k in range(7)])

    shapes = [jax.ShapeDtypeStruct(w.shape, F32) for w in ws]
    res = pl.pallas_call(body, name="adamw_small", out_shape=shapes * 3, compiler_params=_params(0))(*ws, *gs, *ms, *vs)
    return res[:n], res[n:2 * n], res[2 * n:]


def _adamw(w, g, m, v, name):
    rows, cols = w.shape
    tr = 256 if rows % 256 == 0 else rows

    def body(w_ref, g_ref, m_ref, v_ref, d_ref, m2_ref, v2_ref):
        _adam_math(w_ref, g_ref, m_ref, v_ref, d_ref, m2_ref, v2_ref)

    spec = pl.BlockSpec((tr, cols), lambda i: (i, 0))
    shp = jax.ShapeDtypeStruct((rows, cols), F32)
    return pl.pallas_call(
        body, name=name, grid=(rows // tr,), in_specs=[spec] * 4, out_specs=[spec] * 3, out_shape=[shp] * 3,
        compiler_params=_params(),
    )(w, g, m, v)


SMALL = (
    ("attn_norm_g", (1, 1024)), ("conv_w", (1, 4, 512)), ("conv_b", (1, 512)),
    ("lru_wa_fwd", (1, 8, 64, 64)), ("lru_ba_fwd", (1, 512)), ("lru_wx_fwd", (1, 8, 64, 64)), ("lru_bx_fwd", (1, 512)),
    ("lru_lam_fwd", (1, 512)),
    ("lru_wa_bwd", (1, 8, 64, 64)), ("lru_ba_bwd", (1, 512)), ("lru_wx_bwd", (1, 8, 64, 64)), ("lru_bx_bwd", (1, 512)),
    ("lru_lam_bwd", (1, 512)),
    ("rel_bias", (32, 8)), ("norm_rnn_g", (1, 512)), ("norm_attn_g", (1, 512)), ("mlp_norm_g", (1, 1024)),
    ("final_norm_g", (1024,)),
)
PACK_ROW = 8 * 128


def _pack(parts):
    flat = jnp.concatenate([p.reshape(-1) for p in parts])
    pad = (-flat.shape[0]) % PACK_ROW
    return jnp.pad(flat, (0, pad)).reshape(-1, 128)


def _unpack(packed, shapes):
    flat = packed.reshape(-1)
    out, off = [], 0
    for shp in shapes:
        n = int(np.prod(shp))
        out.append(flat[off:off + n].reshape(shp))
        off += n
    return out


WEIGHT_ORDER = ("attn_norm_g", "w_in", "conv_w", "conv_b", "lru_wa_fwd", "lru_ba_fwd", "lru_wx_fwd", "lru_bx_fwd",
                "lru_lam_fwd", "lru_wa_bwd", "lru_ba_bwd", "lru_wx_bwd", "lru_bx_bwd", "lru_lam_bwd", "rel_bias",
                "norm_rnn_g", "norm_attn_g", "w_out", "mlp_norm_g", "w_up", "w_down", "final_norm_g")


def kernel(x, attn_norm_g, w_in, conv_w, conv_b, lru_wa_fwd, lru_ba_fwd, lru_wx_fwd, lru_bx_fwd, lru_lam_fwd, lru_wa_bwd, lru_ba_bwd, lru_wx_bwd, lru_bx_bwd, lru_lam_bwd, rel_bias, norm_rnn_g, norm_attn_g, w_out, mlp_norm_g, w_up, w_down, final_norm_g, loss_target, m_attn_norm_g, m_w_in, m_conv_w, m_conv_b, m_lru_wa_fwd, m_lru_ba_fwd, m_lru_wx_fwd, m_lru_bx_fwd, m_lru_lam_fwd, m_lru_wa_bwd, m_lru_ba_bwd, m_lru_wx_bwd, m_lru_bx_bwd, m_lru_lam_bwd, m_rel_bias, m_norm_rnn_g, m_norm_attn_g, m_w_out, m_mlp_norm_g, m_w_up, m_w_down, m_final_norm_g, v_attn_norm_g, v_w_in, v_conv_w, v_conv_b, v_lru_wa_fwd, v_lru_ba_fwd, v_lru_wx_fwd, v_lru_bx_fwd, v_lru_lam_fwd, v_lru_wa_bwd, v_lru_ba_bwd, v_lru_wx_bwd, v_lru_bx_bwd, v_lru_lam_bwd, v_rel_bias, v_norm_rnn_g, v_norm_attn_g, v_w_out, v_mlp_norm_g, v_w_up, v_w_down, v_final_norm_g):
    given = dict(locals())
    w = {n: given[n] for n in WEIGHT_ORDER}
    m = {n: given["m_" + n] for n in WEIGHT_ORDER}
    v = {n: given["v_" + n] for n in WEIGHT_ORDER}

    chip = lax.axis_index("x") * 2 + lax.axis_index("y")
    core = lax.axis_index("c")

    shards = {n: w[n][0].astype(BF16) for n in BIG}
    shards["conv_w"] = w["conv_w"][0]
    p = {n: (t[0] if t.ndim >= 3 else t) for n, t in w.items() if n not in BIG and n != "conv_w"}
    p["final_norm_g"] = w["final_norm_g"].reshape(1, D_MODEL)

    _, grad_x, small, gathered, big, reduced = _local_step(x[0], loss_target[0], p, shards)

    late = tuple(big)
    grads = [big[n] for n in late]
    others = _run_rider(_pair_exchange_rider(late, grads), "grad_pair_exchange")
    core_arr = core.reshape(1).astype(jnp.int32)
    parts = [_pair_add(core_arr, g, o, "grad_pair_add_" + n) for n, g, o in zip(late, grads, others)]
    landed = _run_rider(_chip_exchange_rider(parts), "grad_chip_exchange")
    halves = [_chip_sum(t, "grad_chip_sum_" + n) for n, t in zip(late, landed)]
    reduced.update(zip(late, _run_rider(_pair_share_rider(late, halves), "grad_pair_share")))

    early_small = [(n, shp) for n, shp in SMALL if n not in small]
    late_small = [(n, shp) for n, shp in SMALL if n in small]
    *early_g, loss = _unpack(_sum_devices(gathered), [shp for _, shp in early_small] + [(1,)])
    late_g = _unpack(_allreduce_small(_pack([small[n].reshape(shp) for n, shp in late_small])),
                     [shp for _, shp in late_small])
    g = dict(zip([n for n, _ in early_small + late_small], early_g + late_g))
    g["conv_w"] = lax.dynamic_slice_in_dim(g["conv_w"], chip * (D_RNN // N_SHARD), D_RNN // N_SHARD, axis=2)
    for n in BIG:
        g[n] = reduced[n][None]

    delta, new_m, new_v = {}, {}, {}
    for n in BIG:
        d2, m2, v2 = _adamw(w[n][0], reduced[n], m[n][0], v[n][0], "adamw_" + n)
        delta[n], new_m[n], new_v[n] = d2[None], m2[None], v2[None]
    names = [n for n, _ in SMALL]
    for dst, src in zip((delta, new_m, new_v), _adamw_many(*[[t[n] for n in names] for t in (w, g, m, v)])):
        dst.update(dict(zip(names, src)))

    return (loss.reshape(()), grad_x[None], *[g[n] for n in WEIGHT_ORDER], *[delta[n] for n in WEIGHT_ORDER],
            *[new_m[n] for n in WEIGHT_ORDER], *[new_v[n] for n in WEIGHT_ORDER])
```

```python
import functools
import math

import numpy as np
import jax
import jax.numpy as jnp
from jax import lax
from jax.experimental import pallas as pl
from jax.experimental.pallas import tpu as pltpu

F32 = jnp.float32
BF16 = jnp.bfloat16

D_MODEL = 1024
D_RNN = 512
D_ATTN = 512
N_HEADS = 8
HEAD_DIM = 64
N_RNN_BLOCKS = 8
RNN_BLOCK = 64
D_IN = 2 * D_RNN + 3 * D_ATTN
D_FF = 4 * D_MODEL
N_SHARD = 4
IN_BLK = D_IN // N_SHARD
OUT_BLK = D_MODEL // N_SHARD
FF_BLK = D_FF // N_SHARD
EPS = 1e-6
NEG_INF = -1e30
LRU_C = 8.0
DILATIONS = (1, 4, 16)
F32_LAYOUT = 4
HALF_WIN = 64
Q_BLK = 128
K_WIN = 256
N_BUCKETS = 32
MAX_DISTANCE = 1024
ATTN_SCALE = HEAD_DIM ** -0.5

ADAM_LR = 0.001
ADAM_B1 = 0.9
ADAM_B2 = 0.999
ADAM_EPS = 1e-08
ADAM_WD = 0.01
ADAM_STEP = 10

TS = 512
TS_MLP = 256
TS_INPROJ_BWD = 512
ATTN_SUB = 16
TK_DW = 4096
SCAN_UNROLL = 8
SUB = 8
VMEM_LIMIT = 56 * 1024 * 1024
GELU_C0 = math.sqrt(2.0 / math.pi)
GELU_C1 = 0.044715

MESH = pl.DeviceIdType.MESH


def _params(n_grid=1):
    return pltpu.CompilerParams(vmem_limit_bytes=VMEM_LIMIT, dimension_semantics=("arbitrary",) * n_grid)


def _whole_vmem():
    return pl.BlockSpec(memory_space=pltpu.VMEM)


def _rows(width, tile=TS):
    return pl.BlockSpec((tile, width), lambda i: (i, 0))


def _sigmoid(z):
    return 0.5 * jnp.tanh(0.5 * z) + 0.5


def _log1p(u):
    w = 1.0 + u
    return jnp.where(w == 1.0, u, jnp.log(w) * (u / (w - 1.0)))


def _softplus(z):
    return jnp.maximum(z, 0.0) + _log1p(jnp.exp(-jnp.abs(z)))


def _gelu_parts(g):
    inner = GELU_C0 * (g + GELU_C1 * g * g * g)
    t = jnp.tanh(inner)
    val = 0.5 * g * (1.0 + t)
    dinner = GELU_C0 * (1.0 + 3.0 * GELU_C1 * g * g)
    grad = 0.5 * (1.0 + t) + 0.5 * g * (1.0 - t * t) * dinner
    return val, grad


def _rms(x):
    rstd = lax.rsqrt(jnp.mean(x * x, axis=-1, keepdims=True) + EPS)
    return rstd, x * rstd


def _rms_bwd(dy, g, xhat, rstd):
    dxh = dy * g
    dx = rstd * (dxh - xhat * jnp.mean(dxh * xhat, axis=-1, keepdims=True))
    dg = jnp.sum(dy * xhat, axis=0, keepdims=True)
    return dx, dg


def _dot(a, b):
    return jnp.dot(a, b, preferred_element_type=F32)


def _dot_nt(a, b):
    return lax.dot_general(a, b, (((1,), (1,)), ((), ())), preferred_element_type=F32)


def _dot_tn(a, b):
    return lax.dot_general(a, b, (((0,), (0,)), ((), ())), preferred_element_type=F32)


def _shifted(tile, prev8, next8, k):
    n = tile.shape[0]
    row = lax.broadcasted_iota(jnp.int32, tile.shape, 0)
    if k == 0:
        return tile
    if k < 0:
        r = pltpu.roll(tile, -k, 0)
        for j in range(-k):
            r = jnp.where(row == j, prev8[SUB + j + k:SUB + j + k + 1, :], r)
        return r
    r = pltpu.roll(tile, n - k, 0)
    for j in range(k):
        r = jnp.where(row == n - k + j, next8[j:j + 1, :], r)
    return r


def _to_lane_blocks(val, s_ref):
    for j in range(val.shape[1] // 128):
        s_ref[j] = val[:, j * 128:(j + 1) * 128]


def _from_lane_blocks(s_ref):
    return jnp.concatenate([s_ref[j] for j in range(s_ref.shape[0])], axis=-1)


def _class_rows(s_ref, r, dil):
    n = s_ref.shape[1] // dil
    return jnp.concatenate([s_ref[j, pl.ds(r, n, stride=dil), :] for j in range(s_ref.shape[0])], axis=-1)


def _split_classes(val, s_ref, out_ref, dil):
    _to_lane_blocks(val, s_ref)
    for r in range(dil):
        out_ref[r] = _class_rows(s_ref, r, dil).astype(out_ref.dtype)


def _merge_classes(in_ref, s_ref, dil, also_ref=None):
    n = s_ref.shape[1] // dil
    for r in range(dil):
        v = in_ref[r].astype(F32)
        if also_ref is not None:
            v = v + also_ref[r].astype(F32)
        for j in range(s_ref.shape[0]):
            s_ref[j, pl.ds(r, n, stride=dil), :] = v[:, j * 128:(j + 1) * 128]
    return _from_lane_blocks(s_ref)


def _class_spec(dil, tile=TS):
    return pl.BlockSpec((dil, tile // dil, 512), lambda i: (0, i, 0))


def _class_shape(S, dil, dtype):
    return jax.ShapeDtypeStruct((dil, S // dil, 512), dtype)


def _scan_tile(a_ref, b_ref, h_ref, carry_ref, reverse):
    n = a_ref.shape[0]
    width = a_ref.shape[1]
    groups = n // SUB
    row = lax.broadcasted_iota(jnp.int32, (SUB, width), 0)

    def group_scan(g):
        r0 = pl.multiple_of(g * SUB, SUB)
        a = a_ref[pl.ds(r0, SUB), :]
        b = b_ref[pl.ds(r0, SUB), :]
        for s in (1, 2, 4):
            if reverse:
                a_sh = pltpu.roll(a, SUB - s, 0)
                b_sh = pltpu.roll(b, SUB - s, 0)
                m = row < SUB - s
            else:
                a_sh = pltpu.roll(a, s, 0)
                b_sh = pltpu.roll(b, s, 0)
                m = row >= s
            b = jnp.where(m, a * b_sh + b, b)
            a = jnp.where(m, a * a_sh, a)
        return r0, a, b

    def step(i, carry):
        first = i * SCAN_UNROLL
        order = [(groups - 1 - (first + u)) if reverse else (first + u) for u in range(SCAN_UNROLL)]
        scans = [group_scan(g) for g in order]
        for r0, a, b in scans:
            h = b + a * carry
            h_ref[pl.ds(r0, SUB), :] = h
            edge = h[0:1, :] if reverse else h[SUB - 1:SUB, :]
            carry = jnp.broadcast_to(edge, (SUB, width))
        return carry

    carry_ref[...] = lax.fori_loop(0, groups // SCAN_UNROLL, step, carry_ref[...])


def _conv_fwd(xr, prev8, next8, cw, cb):
    y = cb + _shifted(xr, prev8, next8, -2) * cw[0:1, :]
    y = y + _shifted(xr, prev8, next8, -1) * cw[1:2, :]
    y = y + xr * cw[2:3, :]
    y = y + _shifted(xr, prev8, next8, 1) * cw[3:4, :]
    return y


def _lru_gates(xc, wa_ref, ba, wx_ref, bx, lam):
    xcb = xc.astype(BF16)
    r = _sigmoid(_dot(xcb, wa_ref[...]) + ba)
    i = _sigmoid(_dot(xcb, wx_ref[...]) + bx)
    cl = -LRU_C * _softplus(-lam)
    la = cl * r
    a = jnp.exp(la)
    m2 = -jnp.tanh(la) * (a * a + 1.0)
    inv = jnp.where(m2 > 0.0, lax.rsqrt(m2), 0.0)
    mult = m2 * inv
    return xcb, r, i, cl, a, mult, inv


def _inproj_fwd(x, g1, w_in, rider=None):
    S = x.shape[0]

    def body(x_ref, g_ref, w_ref, xr_ref, gate_ref, *rest):
        qkv_refs, s_ref, s4_ref, w_full = rest[:9], rest[9], rest[10], rest[11]

        @pl.when(pl.program_id(0) == 0)
        def _():
            for j in range(N_SHARD):
                w_full[:, j * IN_BLK:(j + 1) * IN_BLK] = w_ref[j]

        _, xh = _rms(x_ref[...])
        h = (xh * g_ref[...]).astype(BF16)
        proj = _dot(h, w_full[...])
        xr_ref[...] = proj[:, 0:512]
        gate_ref[...] = proj[:, 512:1024]
        for t in range(3):
            val = proj[:, 1024 + 512 * t:1536 + 512 * t]
            d1_ref, d4_ref, d16_ref = qkv_refs[3 * t:3 * t + 3]
            d1_ref[0] = val.astype(BF16)
            _to_lane_blocks(val, s_ref)
            for r4 in range(4):
                c4 = _class_rows(s_ref, r4, 4)
                d4_ref[r4] = c4.astype(BF16)
                _to_lane_blocks(c4, s4_ref.at[r4])
            for r4 in range(4):
                for m in range(4):
                    d16_ref[r4 + 4 * m] = _class_rows(s4_ref.at[r4], m, 4).astype(BF16)

    f = jax.ShapeDtypeStruct((S, 512), F32)
    return _call(
        body, "inproj_fwd", (S // TS,),
        [_rows(D_MODEL), _whole_vmem(), _whole_vmem()],
        [_rows(512)] * 2 + [_class_spec(d) for d in DILATIONS] * 3,
        [f, f] + [_class_shape(S, d, BF16) for d in DILATIONS] * 3,
        [pltpu.VMEM((4, TS, 128), F32), pltpu.VMEM((4, 4, TS // 4, 128), F32), pltpu.VMEM((D_MODEL, D_IN), BF16)],
        (x, g1, w_in), rider)


def _halo_specs(S, order, tile=TS):
    per = tile // SUB
    last = S // SUB - 1
    return [
        pl.BlockSpec((tile, 512), lambda i: (order(i), 0)),
        pl.BlockSpec((SUB, 512), lambda i: (jnp.maximum(order(i) * per - 1, 0), 0)),
        pl.BlockSpec((SUB, 512), lambda i: (jnp.minimum((order(i) + 1) * per, last), 0)),
    ]


def _rnn_fwd(xr, conv_w, conv_b, wa, ba, wx, bx, lam, reverse, rider=None, xc=None):
    S = xr.shape[0]
    nt = S // TS
    order = (lambda i: nt - 1 - i) if reverse else (lambda i: i)
    with_conv = xc is None
    n_x = 5 if with_conv else 1
    tile = pl.BlockSpec((TS, 512), lambda i: (order(i), 0))

    def body(*refs):
        wa_ref, ba_ref, wx_ref, bx_ref, lam_ref, h_ref = refs[n_x:n_x + 6]
        a_s, b_s, carry = refs[-3:]
        i = pl.program_id(0)
        t = order(i)

        @pl.when(i == 0)
        def _():
            carry[...] = jnp.zeros_like(carry)

        if with_conv:
            x_ref, xp_ref, xn_ref, cw_ref, cb_ref = refs[:5]
            prev8 = jnp.where(t > 0, xp_ref[...], 0.0)
            next8 = jnp.where(t < nt - 1, xn_ref[...], 0.0)
            xcv = _conv_fwd(x_ref[...], prev8, next8, cw_ref[...], cb_ref[...])
            refs[n_x + 6][...] = xcv
        else:
            xcv = refs[0][...]
        _, _, gi, _, a, mult, _ = _lru_gates(xcv, wa_ref, ba_ref[...], wx_ref, bx_ref[...], lam_ref[...])
        a_s[...] = a
        b_s[...] = mult * (gi * xcv)
        _scan_tile(a_s, b_s, h_ref, carry, reverse)

    f512 = jax.ShapeDtypeStruct((S, 512), F32)
    return _call(
        body, "rnn_fwd_rev" if reverse else "rnn_fwd_fwd", (nt,),
        (_halo_specs(S, order) + [_whole_vmem()] * 2 if with_conv else [tile]) + [_whole_vmem()] * 5,
        [tile, tile] if with_conv else [tile], [f512, f512] if with_conv else [f512],
        [pltpu.VMEM((TS, 512), F32), pltpu.VMEM((TS, 512), F32), pltpu.VMEM((SUB, 512), F32)],
        ((xr, xr, xr, conv_w, conv_b) if with_conv else (xc,)) + (wa, ba, wx, bx, lam), rider)


def _mix_fwd(o3, l3, hf, hb, gate, x, g_rnn, g_attn, w_out):
    S = x.shape[0]

    def body(o1, o2, o3_, l1, l2, l3_, hf_ref, hb_ref, gate_ref, x_ref, gr_ref, ga_ref, w_ref,
             x1_ref, mix_ref, ya1, ya2, ls1, ls2, s_ref):
        la, lb, lc = l1[0], _merge_classes(l2, s_ref, F32_LAYOUT), _merge_classes(l3_, s_ref, F32_LAYOUT)
        m = jnp.maximum(jnp.maximum(la, lb), lc)
        ea, eb, ec = jnp.exp(la - m), jnp.exp(lb - m), jnp.exp(lc - m)
        den = ea + eb + ec
        lse = m + jnp.log(den)
        ya = (ea * o1[0] + eb * _merge_classes(o2, s_ref, F32_LAYOUT) + ec * _merge_classes(o3_, s_ref, F32_LAYOUT)) / den
        ya1[0] = ya
        ls1[0] = lse
        _split_classes(ya, s_ref, ya2, F32_LAYOUT)
        _split_classes(lse, s_ref, ls2, F32_LAYOUT)
        gg, _ = _gelu_parts(gate_ref[...])
        yr = (hf_ref[...] + hb_ref[...]) * gg
        _, xh_r = _rms(yr)
        _, xh_a = _rms(ya)
        mix = jnp.concatenate([xh_r * gr_ref[...], xh_a * ga_ref[...]], axis=-1).astype(BF16)
        mix_ref[...] = mix
        acc = x_ref[...]
        for j in range(N_SHARD):
            acc = acc + _dot(mix[:, j * OUT_BLK:(j + 1) * OUT_BLK], w_ref[j])
        x1_ref[...] = acc

    one, four = _class_spec(1), _class_spec(F32_LAYOUT)
    return pl.pallas_call(
        body, grid=(S // TS,), name="mix_fwd",
        in_specs=[one, four, four] * 2 + [_rows(512)] * 3 + [_rows(D_MODEL)] + [_whole_vmem()] * 3,
        out_specs=[_rows(D_MODEL), _rows(D_MODEL)] + [one, four] * 2,
        out_shape=[jax.ShapeDtypeStruct((S, D_MODEL), F32), jax.ShapeDtypeStruct((S, D_MODEL), BF16)]
        + [_class_shape(S, 1, F32), _class_shape(S, F32_LAYOUT, F32)] * 2,
        scratch_shapes=[pltpu.VMEM((4, TS, 128), F32)],
        compiler_params=_params(),
    )(*o3, *l3, hf, hb, gate, x, g_rnn, g_attn, w_out)


def _mlp_fwd_bwd(x1, target, g_mlp, g_fin, w_up, w_down):
    S = x1.shape[0]
    tm = TS_MLP

    def body(x1_ref, t_ref, gm_ref, gf_ref, wu_ref, wd_ref,
             dx1_ref, h2_ref, a2_ref, du_ref, dx2_ref, loss_ref, dgf_ref, dgm_ref, relu_s):
        @pl.when(pl.program_id(0) == 0)
        def _():
            loss_ref[...] = jnp.zeros_like(loss_ref)
            dgf_ref[...] = jnp.zeros_like(dgf_ref)
            dgm_ref[...] = jnp.zeros_like(dgm_ref)

        x1v = x1_ref[...]
        rstd1, xh1 = _rms(x1v)
        h2 = (xh1 * gm_ref[...]).astype(BF16)
        h2_ref[...] = h2
        x2 = x1v
        for j in range(N_SHARD):
            r = jnp.maximum(_dot(h2, wu_ref[j]), 0.0)
            relu_s[j] = r
            a2 = (r * r).astype(BF16)
            a2_ref[:, j * FF_BLK:(j + 1) * FF_BLK] = a2
            x2 = x2 + _dot(a2, wd_ref[j])
        rstd2, xh2 = _rms(x2)
        err = xh2 * gf_ref[...] - t_ref[...]
        loss_ref[...] += jnp.sum(err * err, axis=0, keepdims=True)
        dy = err * (1.0 / D_MODEL)
        dx2, dgf = _rms_bwd(dy, gf_ref[...], xh2, rstd2)
        dgf_ref[...] += dgf
        dx2b = dx2.astype(BF16)
        dx2_ref[...] = dx2b
        dh2 = jnp.zeros((tm, D_MODEL), F32)
        for j in range(N_SHARD):
            du = (_dot_nt(dx2b, wd_ref[j]) * (2.0 * relu_s[j])).astype(BF16)
            du_ref[:, j * FF_BLK:(j + 1) * FF_BLK] = du
            dh2 = dh2 + _dot_nt(du, wu_ref[j])
        dx1n, dgm = _rms_bwd(dh2, gm_ref[...], xh1, rstd1)
        dgm_ref[...] += dgm
        dx1_ref[...] = dx2 + dx1n

    vec = jax.ShapeDtypeStruct((1, D_MODEL), F32)
    return pl.pallas_call(
        body, grid=(S // tm,), name="mlp_fwd_bwd",
        in_specs=[_rows(D_MODEL, tm), _rows(D_MODEL, tm)] + [_whole_vmem()] * 4,
        out_specs=[_rows(D_MODEL, tm), _rows(D_MODEL, tm), _rows(D_FF, tm), _rows(D_FF, tm), _rows(D_MODEL, tm)]
        + [_whole_vmem()] * 3,
        out_shape=[jax.ShapeDtypeStruct((S, D_MODEL), F32), jax.ShapeDtypeStruct((S, D_MODEL), BF16),
                   jax.ShapeDtypeStruct((S, D_FF), BF16), jax.ShapeDtypeStruct((S, D_FF), BF16),
                   jax.ShapeDtypeStruct((S, D_MODEL), BF16), vec, vec, vec],
        scratch_shapes=[pltpu.VMEM((N_SHARD, tm, FF_BLK), F32)],
        compiler_params=_params(),
    )(x1, target, g_mlp, g_fin, w_up, w_down)


def _mix_bwd(dx1, w_out, mixb, ya, hf, hb, gate, g_rnn, g_attn):
    S = dx1.shape[0]

    def body(dx1_ref, w_ref, mix_ref, ya_ref, hf_ref, hb_ref, gate_ref, gr_ref, ga_ref,
             dhs_ref, dgate_ref, dya1, dya2, dw_ref, dgr_ref, dga_ref, s_ref):
        @pl.when(pl.program_id(0) == 0)
        def _():
            dw_ref[...] = jnp.zeros_like(dw_ref)
            dgr_ref[...] = jnp.zeros_like(dgr_ref)
            dga_ref[...] = jnp.zeros_like(dga_ref)

        dx1b = dx1_ref[...].astype(BF16)
        mix = mix_ref[...]
        for j in range(N_SHARD):
            dw_ref[j] += _dot_tn(mix[:, j * OUT_BLK:(j + 1) * OUT_BLK], dx1b)
        dmix = jnp.concatenate([_dot_nt(dx1b, w_ref[j]) for j in range(N_SHARD)], axis=-1)
        gg, dgg = _gelu_parts(gate_ref[...])
        hs = hf_ref[...] + hb_ref[...]
        rstd_r, xh_r = _rms(hs * gg)
        dyr, dgr = _rms_bwd(dmix[:, 0:D_RNN], gr_ref[...], xh_r, rstd_r)
        dgr_ref[...] += dgr
        rstd_a, xh_a = _rms(ya_ref[0])
        dya, dga = _rms_bwd(dmix[:, D_RNN:], ga_ref[...], xh_a, rstd_a)
        dga_ref[...] += dga
        dya1[0] = dya
        _split_classes(dya, s_ref, dya2, F32_LAYOUT)
        dhs_ref[...] = dyr * gg
        dgate_ref[...] = dyr * hs * dgg

    f512 = jax.ShapeDtypeStruct((S, 512), F32)
    vec = jax.ShapeDtypeStruct((1, 512), F32)
    return pl.pallas_call(
        body, grid=(S // TS,), name="mix_bwd",
        in_specs=[_rows(D_MODEL), _whole_vmem(), _rows(D_MODEL), _class_spec(1)] + [_rows(512)] * 3 + [_whole_vmem()] * 2,
        out_specs=[_rows(512)] * 2 + [_class_spec(1), _class_spec(F32_LAYOUT)] + [_whole_vmem()] * 3,
        out_shape=[f512, f512, _class_shape(S, 1, F32), _class_shape(S, F32_LAYOUT, F32),
                   jax.ShapeDtypeStruct((N_SHARD, OUT_BLK, D_MODEL), F32), vec, vec],
        scratch_shapes=[pltpu.VMEM((4, TS, 128), F32)],
        compiler_params=_params(),
    )(dx1, w_out, mixb, ya, hf, hb, gate, g_rnn, g_attn)


def _rnn_bwd(xc, h, dhs, wa, ba, wx, bx, lam, reverse, rider=None):
    S = xc.shape[0]
    nt = S // TS
    order = (lambda i: i) if reverse else (lambda i: nt - 1 - i)
    per = TS // SUB
    last = S // SUB - 1
    if reverse:
        h_halo = pl.BlockSpec((SUB, 512), lambda i: (jnp.minimum((order(i) + 1) * per, last), 0))
    else:
        h_halo = pl.BlockSpec((SUB, 512), lambda i: (jnp.maximum(order(i) * per - 1, 0), 0))
    tile = pl.BlockSpec((TS, 512), lambda i: (order(i), 0))

    def body(xc_ref, h_ref, hh_ref, dh_ref, wa_ref, ba_ref, wx_ref, bx_ref, lam_ref,
             dxc_ref, dwa_ref, dwx_ref, dvec_ref, a_s, g_s, carry, edge):
        i = pl.program_id(0)
        t = order(i)

        @pl.when(i == 0)
        def _():
            carry[...] = jnp.zeros_like(carry)
            edge[...] = jnp.zeros_like(edge)
            dwa_ref[...] = jnp.zeros_like(dwa_ref)
            dwx_ref[...] = jnp.zeros_like(dwx_ref)
            dvec_ref[...] = jnp.zeros_like(dvec_ref)

        xc = xc_ref[...]
        xcb, r, gi, cl, a, mult, inv_mult = _lru_gates(xc, wa_ref, ba_ref[...], wx_ref, bx_ref[...], lam_ref[...])
        hv = h_ref[...]
        if reverse:
            a_s[...] = _shifted(a, edge[...], None, -1)
            edge[...] = a[TS - SUB:TS, :]
            hh = jnp.where(t < nt - 1, hh_ref[...], 0.0)
            h_prev = _shifted(hv, None, hh, 1)
        else:
            a_s[...] = _shifted(a, None, edge[...], 1)
            edge[...] = a[0:SUB, :]
            hh = jnp.where(t > 0, hh_ref[...], 0.0)
            h_prev = _shifted(hv, hh, None, -1)
        _scan_tile(a_s, dh_ref, g_s, carry, not reverse)
        g = g_s[...]
        da = g * h_prev
        gm = g * mult
        d_i = gm * xc
        dmult = g * gi * xc
        dla = da * a - dmult * (a * a) * inv_mult
        d_r = dla * cl
        dpre_r = d_r * r * (1.0 - r)
        dpre_i = d_i * gi * (1.0 - gi)
        dprb = dpre_r.astype(BF16)
        dpib = dpre_i.astype(BF16)
        dwa_ref[...] += _dot_tn(xcb, dprb)
        dwx_ref[...] += _dot_tn(xcb, dpib)
        dvec_ref[0:1, :] += jnp.sum(dpre_r, axis=0, keepdims=True)
        dvec_ref[1:2, :] += jnp.sum(dpre_i, axis=0, keepdims=True)
        dvec_ref[2:3, :] += jnp.sum(dla * r, axis=0, keepdims=True)
        dvec_ref[3:4, :] = dvec_ref[2:3, :] * (LRU_C * _sigmoid(-lam_ref[...]))
        dxc_ref[...] = gm * gi + _dot_nt(dprb, wa_ref[...]) + _dot_nt(dpib, wx_ref[...])

    sq = jax.ShapeDtypeStruct((D_RNN, D_RNN), F32)
    return _call(
        body, "rnn_bwd_rev" if reverse else "rnn_bwd_fwd", (nt,),
        [tile, tile, h_halo, tile] + [_whole_vmem()] * 5,
        [tile, _whole_vmem(), _whole_vmem(), _whole_vmem()],
        [jax.ShapeDtypeStruct((S, 512), F32), sq, sq, jax.ShapeDtypeStruct((SUB, 512), F32)],
        [pltpu.VMEM((TS, 512), F32), pltpu.VMEM((TS, 512), F32), pltpu.VMEM((SUB, 512), F32),
         pltpu.VMEM((SUB, 512), F32)],
        (xc, h, h, dhs, wa, ba, wx, bx, lam), rider)


def _inproj_bwd(x, dx1, xr, dxc_f, dxc_b, dgate, dq3, dk3, dv3, g1, conv_w, w_in, rider=None):
    S = x.shape[0]
    tb = TS_INPROJ_BWD
    nt = S // tb
    ident = lambda i: i

    def body(x_ref, dx1_ref, xr_ref, xrp_ref, xrn_ref, cf_ref, cfp_ref, cfn_ref, cb_ref, cbp_ref, cbn_ref, dgate_ref,
             dq1, dq2, dq3_, dk1, dk2, dk3_, dv1, dv2, dv3_, g_ref, cw_ref, w_ref,
             dx_ref, dw_ref, dg_ref, dcw_ref, s_ref, w_full, dw_full, sems):
        i = pl.program_id(0)

        def blocks(full, blocked, k0):
            return [(full.at[:, j * IN_BLK:(j + 1) * IN_BLK], blocked.at[j], sems.at[k0 + j]) for j in range(N_SHARD)]

        @pl.when(i == 0)
        def _():
            copies = [pltpu.make_async_copy(src, dst, sem) for dst, src, sem in blocks(w_full, w_ref, 0)]
            for cp in copies:
                cp.start()
            for cp in copies:
                cp.wait()
            dw_full[...] = jnp.zeros_like(dw_full)
            dg_ref[...] = jnp.zeros_like(dg_ref)
            dcw_ref[...] = jnp.zeros_like(dcw_ref)

        first, last = i > 0, i < nt - 1
        dxc = cf_ref[...] + cb_ref[...]
        dxc_p = jnp.where(first, cfp_ref[...] + cbp_ref[...], 0.0)
        dxc_n = jnp.where(last, cfn_ref[...] + cbn_ref[...], 0.0)
        cw = cw_ref[...]
        dxr = (_shifted(dxc, dxc_p, dxc_n, 2) * cw[0:1, :] + _shifted(dxc, dxc_p, dxc_n, 1) * cw[1:2, :]
               + dxc * cw[2:3, :] + _shifted(dxc, dxc_p, dxc_n, -1) * cw[3:4, :])
        xrv = xr_ref[...]
        xr_p = jnp.where(first, xrp_ref[...], 0.0)
        xr_n = jnp.where(last, xrn_ref[...], 0.0)
        for k, off in enumerate((-2, -1, 0, 1)):
            dcw_ref[k:k + 1, :] += jnp.sum(dxc * _shifted(xrv, xr_p, xr_n, off), axis=0, keepdims=True)
        dcw_ref[4:5, :] += jnp.sum(dxc, axis=0, keepdims=True)

        def total(a, b, c_):
            return a[0].astype(F32) + _merge_classes(b, s_ref, F32_LAYOUT, c_)

        dproj = jnp.concatenate(
            [dxr, dgate_ref[...], total(dq1, dq2, dq3_), total(dk1, dk2, dk3_), total(dv1, dv2, dv3_)],
            axis=-1).astype(BF16)
        xv = x_ref[...]
        rstd, xh = _rms(xv)
        hb = (xh * g_ref[...]).astype(BF16)
        dh = _dot_nt(dproj, w_full[...])
        dw_full[...] += _dot_tn(hb, dproj)
        dxn, dg = _rms_bwd(dh, g_ref[...], xh, rstd)
        dg_ref[...] += dg
        dx_ref[...] = dx1_ref[...] + dxn

        @pl.when(i == nt - 1)
        def _():
            copies = [pltpu.make_async_copy(src, dst, sem) for src, dst, sem in blocks(dw_full, dw_ref, N_SHARD)]
            for cp in copies:
                cp.start()
            for cp in copies:
                cp.wait()

    halo = _halo_specs(S, ident, tb)
    return _call(
        body, "inproj_bwd", (nt,),
        [_rows(D_MODEL, tb), _rows(D_MODEL, tb)] + halo * 3 + [_rows(512, tb)]
        + [_class_spec(1, tb), _class_spec(F32_LAYOUT, tb), _class_spec(F32_LAYOUT, tb)] * 3 + [_whole_vmem()] * 2 + [ANY],
        [_rows(D_MODEL, tb), ANY, _whole_vmem(), _whole_vmem()],
        [jax.ShapeDtypeStruct((S, D_MODEL), F32), jax.ShapeDtypeStruct((N_SHARD, D_MODEL, IN_BLK), F32),
         jax.ShapeDtypeStruct((1, D_MODEL), F32), jax.ShapeDtypeStruct((SUB, 512), F32)],
        [pltpu.VMEM((4, tb, 128), F32), pltpu.VMEM((D_MODEL, D_IN), BF16), pltpu.VMEM((D_MODEL, D_IN), F32),
         pltpu.SemaphoreType.DMA((2 * N_SHARD,))],
        (x, dx1, xr, xr, xr, dxc_f, dxc_f, dxc_f, dxc_b, dxc_b, dxc_b, dgate, *dq3, *dk3, *dv3, g1, conv_w, w_in), rider)


def _dw_matmul(a, b, a_cols, b_cols, name):
    S = a.shape[0]
    tk = min(S, TK_DW)
    a_shared = a.shape[1] == a_cols
    b_shared = b.shape[1] == b_cols

    def body(a_ref, b_ref, o_ref):
        @pl.when(pl.program_id(1) == 0)
        def _():
            o_ref[...] = jnp.zeros_like(o_ref)
        o_ref[0] += _dot_tn(a_ref[...], b_ref[...])

    return pl.pallas_call(
        body, grid=(N_SHARD, S // tk), name=name,
        in_specs=[pl.BlockSpec((tk, a_cols), (lambda j, k: (k, 0)) if a_shared else (lambda j, k: (k, j))),
                  pl.BlockSpec((tk, b_cols), (lambda j, k: (k, 0)) if b_shared else (lambda j, k: (k, j)))],
        out_specs=pl.BlockSpec((1, a_cols, b_cols), lambda j, k: (j, 0, 0)),
        out_shape=jax.ShapeDtypeStruct((N_SHARD, a_cols, b_cols), F32),
        compiler_params=_params(2),
    )(a, b)


def _t5_bucket_np(rel):
    nb = N_BUCKETS // 2
    max_exact = nb // 2
    ret = np.where(rel > 0, nb, 0)
    n = np.abs(rel)
    nf = np.maximum(n, 1).astype(np.float32)
    large = max_exact + (np.log(nf / np.float32(max_exact)) / np.float32(math.log(MAX_DISTANCE / max_exact))
                         * np.float32(nb - max_exact)).astype(np.int32)
    large = np.minimum(large, nb - 1)
    return ret + np.where(n < max_exact, n, large)


_VARIANT_OFFSETS = (-HALF_WIN,) * 3


def _band_index():
    kk = np.arange(K_WIN)[None, :]
    ql = np.arange(Q_BLK)[:, None]
    rel = np.stack([kk - ql + off for off in _VARIANT_OFFSETS])
    band = np.abs(rel) <= HALF_WIN
    inside = np.stack([np.broadcast_to(kk >= HALF_WIN, band[0].shape), np.ones_like(band[0]),
                       np.broadcast_to(kk < K_WIN - HALF_WIN, band[0].shape)])
    return rel, band & inside


def _bucket_tables(dil):
    rel, valid = _band_index()
    bucket = _t5_bucket_np(np.clip(rel, -HALF_WIN, HALF_WIN) * dil)
    return np.where(valid, bucket, -1).astype(np.int32)


def _bias_mats(rel_bias, rider=None):
    tables = [_bucket_tables(d) for d in DILATIONS]
    used = [sorted(set(t[t >= 0].tolist())) for t in tables]

    def one_pattern(rb_ref, t_ref, o_ref, buckets):
        bk = t_ref[1]
        for h in range(N_HEADS):
            acc = jnp.full((Q_BLK, K_WIN), NEG_INF, F32)
            for b in buckets:
                acc = jnp.where(bk == b, rb_ref[b, h], acc)
            o_ref[1, h] = acc
            for var in (0, 2):
                o_ref[var, h] = jnp.where(t_ref[var] >= 0, acc, NEG_INF)

    def body(rb_ref, t1, t2, t3, o1, o2, o3):
        for i, (t_ref, o_ref) in enumerate(((t1, o1), (t2, o2), (t3, o3))):
            pl.when(pl.program_id(0) == i)(functools.partial(one_pattern, rb_ref, t_ref, o_ref, used[i]))

    shp = jax.ShapeDtypeStruct((3, N_HEADS, Q_BLK, K_WIN), F32)
    return _call(
        body, "bias_tables", (len(DILATIONS),), [pl.BlockSpec(memory_space=pltpu.SMEM)] + [_whole_vmem()] * 3,
        [_whole_vmem()] * 3, [shp] * 3, [], (rel_bias, *[jnp.asarray(t) for t in tables]), rider)


def _variant(qb, nq):
    return jnp.where(qb == 0, 0, jnp.where(qb == nq - 1, 2, 1))


def _win_start(qb):
    return pl.multiple_of(qb * Q_BLK, Q_BLK)


def _fill_padded(src_ref, pad_ref):
    L = src_ref.shape[0]
    edge = jnp.zeros((HALF_WIN, 128), pad_ref.dtype)
    pad_ref[0:HALF_WIN, :] = edge
    pad_ref[HALF_WIN:HALF_WIN + L, :] = src_ref[...]
    pad_ref[HALF_WIN + L:2 * HALF_WIN + L, :] = edge


INNER = {1: 1, 4: 1, 16: 4}


def _attn_layout(dil, L):
    inner = INNER[dil]
    n_outer = dil // inner
    nsub = min(ATTN_SUB // inner, L // Q_BLK)
    qt = nsub * Q_BLK
    grid = (4, n_outer, L // qt)
    qspec = pl.BlockSpec((inner, None, qt, 128), lambda hp, r, s: (0, r, s, hp))
    kspec = pl.BlockSpec((inner, None, L, 128), lambda hp, r, s: (0, r, 0, hp))
    bspec = pl.BlockSpec((3, 2, Q_BLK, K_WIN), lambda hp, r, s: (0, hp, 0, 0))
    kfspec = pl.BlockSpec((None, inner * L, 128), lambda hp, r, s: (r, 0, hp))
    qfspec = kfspec if inner > 1 else pl.BlockSpec((None, qt, 128), lambda hp, r, s: (r, s, hp))
    fshape = jax.ShapeDtypeStruct((n_outer, inner * L, D_ATTN), F32)
    view = lambda t: t.reshape(inner, n_outer, L, D_ATTN)

    def qrows(m, sub):
        if inner == 1:
            return (slice(sub * Q_BLK, (sub + 1) * Q_BLK), slice(None))
        first = (pl.program_id(2) * nsub + sub) * Q_BLK
        return (pl.ds(m + inner * first, Q_BLK, stride=inner), slice(None))

    def krows(m):
        if inner == 1:
            return (slice(None), slice(None))
        return (pl.ds(m, L, stride=inner), slice(None))

    return inner, nsub, grid, qspec, kspec, bspec, qfspec, kfspec, fshape, view, qrows, krows


def _head_masks():
    lane = lax.broadcasted_iota(jnp.int32, (Q_BLK, 128), 1)
    return lane < HEAD_DIM


def _attn_fwd(q, k, v, bias):
    dil, L, _ = q.shape
    nq = L // Q_BLK
    inner, nsub, grid, qspec, kspec, bspec, qfspec, kfspec, fshape, view, qrows, krows = _attn_layout(dil, L)

    def body(q_ref, k_ref, v_ref, b_ref, o_ref, l_ref, kp, vp):
        step = pl.program_id(2)

        @pl.when(step == 0)
        def _():
            for m in range(inner):
                _fill_padded(k_ref.at[m], kp.at[m])
                _fill_padded(v_ref.at[m], vp.at[m])

        h0 = _head_masks()
        for m, sub in [(m, sub) for m in range(inner) for sub in range(nsub)]:
            qb = step * nsub + sub
            st = _win_start(qb)
            var = _variant(qb, nq)
            kw = kp[m, pl.ds(st, K_WIN), :]
            vw = vp[m, pl.ds(st, K_WIN), :]
            qs = q_ref[m, sub * Q_BLK:(sub + 1) * Q_BLK, :] * ATTN_SCALE
            zq = jnp.zeros_like(qs)
            q2 = jnp.concatenate([jnp.where(h0, qs, zq), jnp.where(h0, zq, qs)], axis=0)
            s = _dot_nt(q2, kw) + b_ref[var].reshape(2 * Q_BLK, K_WIN)
            top = jnp.max(s, axis=-1, keepdims=True)
            p = jnp.exp(s - top)
            l = jnp.sum(p, axis=-1, keepdims=True)
            out = _dot(p.astype(BF16), vw) / l
            lse = top + jnp.log(l)
            o_ref[qrows(m, sub)] = jnp.where(h0, out[0:Q_BLK], out[Q_BLK:2 * Q_BLK])
            l_ref[qrows(m, sub)] = jnp.where(h0, lse[0:Q_BLK], lse[Q_BLK:2 * Q_BLK])

    return pl.pallas_call(
        body, grid=grid, name=f"attn_fwd_d{dil}",
        in_specs=[qspec, kspec, kspec, bspec], out_specs=[qfspec, qfspec], out_shape=[fshape, fshape],
        scratch_shapes=[pltpu.VMEM((inner, L + 2 * HALF_WIN, 128), BF16)] * 2,
        compiler_params=_params(3),
    )(view(q), view(k), view(v), bias)


def _attn_bwd(q, k, v, bias, do, o, lse, rider=None):
    dil, L, _ = q.shape
    nq = L // Q_BLK
    inner, nsub, grid, qspec, kspec, bspec, qfspec, kfspec, fshape, view, qrows, krows = _attn_layout(dil, L)
    nstep = grid[2]

    def body(q_ref, k_ref, v_ref, b_ref, do_ref, o_ref, l_ref, dq_ref, dk_ref, dv_ref, db_ref, db_s,
             kp, vp, dkp, dvp, carry):
        hp, step = pl.program_id(0), pl.program_id(2)
        first = (hp == 0) & (pl.program_id(1) == 0) & (step == 0)
        last = (hp == grid[0] - 1) & (pl.program_id(1) == grid[1] - 1) & (step == nstep - 1)

        @pl.when(first)
        def _():
            db_s[...] = jnp.zeros_like(db_s)

        @pl.when(step == 0)
        def _():
            for m in range(inner):
                _fill_padded(k_ref.at[m], kp.at[m])
                _fill_padded(v_ref.at[m], vp.at[m])
            carry[...] = jnp.zeros_like(carry)

        h0 = _head_masks()
        for m, sub in [(m, sub) for m in range(inner) for sub in range(nsub)]:
            if sub == 0:
                carry_k, carry_v = carry[m, 0], carry[m, 1]
            qb = step * nsub + sub
            st = _win_start(qb)
            var = _variant(qb, nq)
            kw = kp[m, pl.ds(st, K_WIN), :]
            vw = vp[m, pl.ds(st, K_WIN), :]
            qs = q_ref[m, sub * Q_BLK:(sub + 1) * Q_BLK, :] * ATTN_SCALE
            dof = do_ref[qrows(m, sub)]
            dob = dof.astype(BF16)
            prod = dof * o_ref[qrows(m, sub)]
            lsev = l_ref[qrows(m, sub)]
            zq, zd = jnp.zeros_like(qs), jnp.zeros_like(dob)
            q2 = jnp.concatenate([jnp.where(h0, qs, zq), jnp.where(h0, zq, qs)], axis=0)
            do2 = jnp.concatenate([jnp.where(h0, dob, zd), jnp.where(h0, zd, dob)], axis=0)
            lse2 = jnp.concatenate([lsev[:, 0:1], lsev[:, HEAD_DIM:HEAD_DIM + 1]], axis=0)
            dd2 = jnp.concatenate([jnp.sum(jnp.where(h0, prod, 0.0), axis=-1, keepdims=True),
                                   jnp.sum(jnp.where(h0, 0.0, prod), axis=-1, keepdims=True)], axis=0)
            s = _dot_nt(q2, kw) + b_ref[var].reshape(2 * Q_BLK, K_WIN)
            p = jnp.exp(s - lse2)
            ds = p * (_dot_nt(do2, vw) - dd2)
            db_s[pl.ds(hp * 2, 2)] += ds.reshape(2, Q_BLK, K_WIN)
            dsb = ds.astype(BF16)
            dv_acc = _dot_tn(p.astype(BF16), do2)
            dk_acc = _dot_tn(dsb, q2)
            dq2 = _dot(dsb, kw) * ATTN_SCALE
            dq_ref[qrows(m, sub)] = jnp.where(h0, dq2[0:Q_BLK], dq2[Q_BLK:2 * Q_BLK]).astype(dq_ref.dtype)
            dkp[m, pl.ds(st, Q_BLK), :] = carry_k + dk_acc[0:Q_BLK]
            dvp[m, pl.ds(st, Q_BLK), :] = carry_v + dv_acc[0:Q_BLK]
            carry_k, carry_v = dk_acc[Q_BLK:K_WIN], dv_acc[Q_BLK:K_WIN]
            if sub == nsub - 1:
                carry[m, 0] = carry_k
                carry[m, 1] = carry_v

        @pl.when(step == nstep - 1)
        def _():
            for m in range(inner):
                dkp[m, L:L + Q_BLK, :] = carry[m, 0]
                dvp[m, L:L + Q_BLK, :] = carry[m, 1]
                dk_ref[krows(m)] = dkp[m, HALF_WIN:HALF_WIN + L, :].astype(dk_ref.dtype)
                dv_ref[krows(m)] = dvp[m, HALF_WIN:HALF_WIN + L, :].astype(dv_ref.dtype)

        @pl.when(last)
        def _():
            db_ref[...] = db_s[...]

    dbshape = (N_HEADS, Q_BLK, K_WIN)
    gshape = jax.ShapeDtypeStruct(fshape.shape, BF16 if inner == 1 else F32)
    return _call(
        body, f"attn_bwd_d{dil}", grid,
        [qspec, kspec, kspec, bspec, qfspec, qfspec, qfspec],
        [qfspec, kfspec, kfspec, _whole_vmem()],
        [gshape, gshape, gshape, jax.ShapeDtypeStruct(dbshape, F32)],
        [pltpu.VMEM(dbshape, F32)] + [pltpu.VMEM((inner, L + 2 * HALF_WIN, 128), BF16)] * 2
        + [pltpu.VMEM((inner, L + 2 * HALF_WIN, 128), F32)] * 2 + [pltpu.VMEM((inner, 2, Q_BLK, 128), F32)],
        (view(q), view(k), view(v), bias, do, o, lse), rider)


def _bucket_onehots(dil):
    m = np.zeros((3, K_WIN, N_BUCKETS), np.float32)
    for var, off in enumerate(_VARIANT_OFFSETS):
        for rel in range(-HALF_WIN, HALF_WIN + 1):
            col = (rel - off + Q_BLK - 1) % K_WIN
            m[var, col, int(_t5_bucket_np(np.asarray(rel * dil)))] = 1.0
    return jnp.asarray(m)


def _bias_grad(dbs):
    onehots = [_bucket_onehots(d) for d in DILATIONS]
    flip = jnp.asarray(np.eye(Q_BLK, dtype=np.float32)[::-1].copy())

    def body(d1, d2, d3, m1, m2, m3, flip_ref, out_ref):
        hp = lax.Precision.HIGHEST
        acc = jnp.zeros((N_HEADS, N_BUCKETS), F32)
        for d_ref, m_ref in ((d1, m1), (d2, m2), (d3, m3)):
            rows = []
            for h in range(N_HEADS):
                xrev = jnp.dot(flip_ref[...], d_ref[h], precision=hp, preferred_element_type=F32)
                y = pltpu.roll(xrev, 0, 1, stride=1, stride_axis=0)
                rows.append(jnp.sum(y, axis=0, keepdims=True))
            acc = acc + jnp.dot(jnp.concatenate(rows, axis=0), m_ref[1], precision=hp, preferred_element_type=F32)
        out_ref[...] = acc

    return pl.pallas_call(
        body, name="bias_grad", out_shape=jax.ShapeDtypeStruct((N_HEADS, N_BUCKETS), F32),
        compiler_params=_params(0),
    )(*dbs, *onehots, flip)


def _block_diag(w):
    eye = jnp.eye(N_RNN_BLOCKS, dtype=w.dtype)
    return jnp.einsum("ncd,nm->ncmd", w, eye).reshape(D_RNN, D_RNN).astype(BF16)


def _diag_blocks(dense):
    d = dense.reshape(N_RNN_BLOCKS, RNN_BLOCK, N_RNN_BLOCKS, RNN_BLOCK)
    return jnp.stack([d[n, :, n, :] for n in range(N_RNN_BLOCKS)])


EARLY = ("w_out", "w_up", "w_down")


def _local_step(x, target, p, shards=None):
    p = dict(p)
    first = None if shards is None else _gather_rider(["w_in"], [shards["w_in"]], shards["conv_w"])
    biases, got = _bias_mats(p["rel_bias"], first)
    if shards is not None:
        p["w_in"] = got[0]
        p["conv_w"] = jnp.transpose(got[1], (1, 0, 2)).reshape(4, D_RNN)
    lru = {}
    for dname in ("fwd", "bwd"):
        lru[dname] = (_block_diag(p["lru_wa_" + dname]), p["lru_ba_" + dname], _block_diag(p["lru_wx_" + dname]),
                      p["lru_bx_" + dname], p["lru_lam_" + dname])

    def gather(name):
        return None if shards is None else _gather_rider([name], [shards[name]])

    (xr, gate, *qkv), got = _inproj_fwd(x, p["attn_norm_g"], p["w_in"], gather("w_out"))
    p.update(zip(["w_out"], got))
    qs, ks, vs = qkv[0:3], qkv[3:6], qkv[6:9]
    (hf, xc), got = _rnn_fwd(xr, p["conv_w"], p["conv_b"], *lru["fwd"], reverse=False, rider=gather("w_up"))
    p.update(zip(["w_up"], got))
    (hb,), got = _rnn_fwd(xr, p["conv_w"], p["conv_b"], *lru["bwd"], reverse=True, rider=gather("w_down"), xc=xc)
    p.update(zip(["w_down"], got))
    outs, lses = [], []
    for q, k, v, bias in zip(qs, ks, vs, biases):
        o, l = _attn_fwd(q, k, v, bias)
        outs.append(o)
        lses.append(l)
    x1, mixb, *yl = _mix_fwd(outs, lses, hf, hb, gate, x, p["norm_rnn_g"], p["norm_attn_g"], p["w_out"])
    yas, lsts = [yl[0], yl[1], yl[1]], [yl[2], yl[3], yl[3]]
    dx1, h2b, a2b, dub, dx2b, loss_vec, dg_fin, dg_mlp = _mlp_fwd_bwd(
        x1, target, p["mlp_norm_g"], p["final_norm_g"], p["w_up"], p["w_down"])
    dhs, dgate, dya1, dya4, dw_out, dg_rnn, dg_attn = _mix_bwd(dx1, p["w_out"], mixb, yas[0], hf, hb, gate,
                                                               p["norm_rnn_g"], p["norm_attn_g"])
    dyas = [dya1, dya4, dya4]
    dw_up = _dw_matmul(h2b, dub, D_MODEL, FF_BLK, "dw_up")
    dw_down = _dw_matmul(a2b, dx2b, FF_BLK, D_MODEL, "dw_down")
    early = [dw_out, dw_up, dw_down]
    dqs, dks, dvs, dbs = [], [], [], []
    for i, (q, k, v, bias, dya, ya, lse) in enumerate(zip(qs, ks, vs, biases, dyas, yas, lsts)):
        rider = None
        if shards is not None:
            make = (lambda: _pair_exchange_rider(EARLY, early), lambda: _chip_exchange_rider(early),
                    lambda: _pair_share_rider(EARLY, early))[i]
            rider = make()
        (dq, dk, dv, db), got = _attn_bwd(q, k, v, bias, dya, ya, lse, rider)
        if shards is not None and i == 0:
            core = lax.axis_index("c").reshape(1).astype(jnp.int32)
            early = [_pair_add(core, g, o, "grad_pair_add_" + n) for n, g, o in zip(EARLY, early, got)]
        elif shards is not None and i == 1:
            early = [_chip_sum(t, "grad_chip_sum_" + n) for n, t in zip(EARLY, got)]
        elif shards is not None:
            early = got
        dqs.append(dq)
        dks.append(dk)
        dvs.append(dv)
        dbs.append(db)
    d_rel_bias = _bias_grad(dbs).T
    (dxc_f, dwa_f, dwx_f, dvec_f), _ = _rnn_bwd(xc, hf, dhs, *lru["fwd"], reverse=False)
    small = {
        "lru_wa_fwd": _diag_blocks(dwa_f), "lru_ba_fwd": dvec_f[0:1], "lru_wx_fwd": _diag_blocks(dwx_f),
        "lru_bx_fwd": dvec_f[1:2], "lru_lam_fwd": dvec_f[3:4],
        "rel_bias": d_rel_bias, "norm_rnn_g": dg_rnn, "norm_attn_g": dg_attn,
        "mlp_norm_g": dg_mlp, "final_norm_g": dg_fin,
    }
    loss_local = (0.5 / D_MODEL) * jnp.sum(loss_vec)
    rider = None
    if shards is not None:
        rider = _small_gather_rider(_pack([small[n].reshape(shp) for n, shp in SMALL if n in small]
                                          + [loss_local.reshape(1)]))
    (dxc_b, dwa_b, dwx_b, dvec_b), gathered = _rnn_bwd(xc, hb, dhs, *lru["bwd"], reverse=True, rider=rider)
    grad_x, dw_in, dg1, dconv = _inproj_bwd(x, dx1, xr, dxc_f, dxc_b, dgate, dqs, dks, dvs,
                                            p["attn_norm_g"], p["conv_w"], p["w_in"])[0]
    last = {"lru_wa_bwd": _diag_blocks(dwa_b), "lru_ba_bwd": dvec_b[0:1], "lru_wx_bwd": _diag_blocks(dwx_b),
            "lru_bx_bwd": dvec_b[1:2], "lru_lam_bwd": dvec_b[3:4],
            "attn_norm_g": dg1, "conv_w": dconv[0:4], "conv_b": dconv[4:5]}
    if shards is None:
        big = {"w_in": dw_in, "w_out": dw_out, "w_up": dw_up, "w_down": dw_down}
        return loss_local, grad_x, {**small, **last}, None, big, {}
    return loss_local, grad_x, last, gathered[0], {"w_in": dw_in}, dict(zip(EARLY, early))


BIG = ("w_in", "w_out", "w_up", "w_down")
BIG_SHARD = {"w_in": (D_MODEL, IN_BLK), "w_out": (OUT_BLK, D_MODEL), "w_up": (D_MODEL, FF_BLK), "w_down": (FF_BLK, D_MODEL)}
N_BIG = len(BIG)
N_CHIP_PEERS = 3
ANY = pl.BlockSpec(memory_space=pl.ANY)


def _place():
    x, y, c = lax.axis_index("x"), lax.axis_index("y"), lax.axis_index("c")
    chips = [(1 - x, y), (x, 1 - y), (1 - x, 1 - y)]
    return x, y, c, chips


def _remote(src, dst, send_sem, recv_sem, dev):
    return pltpu.make_async_remote_copy(src_ref=src, dst_ref=dst, send_sem=send_sem, recv_sem=recv_sem,
                                        device_id=dev, device_id_type=MESH)


def _staged_start(srcs, bufs, sems):
    legs = [pltpu.make_async_copy(s, b, sems.at[i]) for i, (s, b) in enumerate(zip(srcs, bufs))]
    for cp in legs:
        cp.start()
    return legs


def _staged_finish(legs, bufs, dsts, sems):
    out = []
    for i, (leg, b, d) in enumerate(zip(legs, bufs, dsts)):
        leg.wait()
        cp = pltpu.make_async_copy(b, d, sems.at[i])
        cp.start()
        out.append(cp)
    return out


class _Rider:
    def __init__(self, inputs, out_shape, scratch, first, late, last):
        self.inputs, self.out_shape, self.scratch = list(inputs), list(out_shape), list(scratch)
        self.first, self.late, self.last = first, late, last


def _call(body, name, grid, in_specs, out_specs, out_shape, scratch, operands, rider=None):
    n_grid = len(grid)
    if rider is None:
        res = pl.pallas_call(body, grid=grid, name=name, in_specs=in_specs, out_specs=out_specs, out_shape=out_shape,
                             scratch_shapes=scratch, compiler_params=_params(n_grid))(*operands)
        return list(res), []
    n_in, n_out, n_scr = len(in_specs), len(out_specs), len(scratch)
    ri, ro = len(rider.inputs), len(rider.out_shape)
    nsteps = int(np.prod(grid))
    late_step = max(nsteps - 3, 1)

    def wrapped(*refs):
        a, b = n_in, n_in + ri
        c, d = b + n_out, b + n_out + ro
        e = d + n_scr
        mine = refs[:a] + refs[b:c] + refs[d:e]
        theirs = (refs[a:b], refs[c:d], refs[e:])
        step = pl.program_id(0)
        for ax in range(1, n_grid):
            step = step * grid[ax] + pl.program_id(ax)
        pl.when(step == 0)(lambda: rider.first(*theirs))
        pl.when(step == late_step)(lambda: rider.late(*theirs))
        body(*mine)
        pl.when(step == nsteps - 1)(lambda: rider.last(*theirs))

    res = pl.pallas_call(
        wrapped, grid=grid, name=name, in_specs=list(in_specs) + [ANY] * ri, out_specs=list(out_specs) + [ANY] * ro,
        out_shape=list(out_shape) + rider.out_shape, scratch_shapes=list(scratch) + rider.scratch,
        compiler_params=_params(n_grid),
    )(*operands, *rider.inputs)
    return list(res[:n_out]), list(res[n_out:])


def _run_rider(rider, name):
    ri, ro = len(rider.inputs), len(rider.out_shape)

    def body(*refs):
        parts = (refs[:ri], refs[ri:ri + ro], refs[ri + ro:])
        rider.first(*parts)
        rider.late(*parts)
        rider.last(*parts)

    return list(pl.pallas_call(
        body, name=name, in_specs=[ANY] * ri, out_specs=[ANY] * ro, out_shape=rider.out_shape, scratch_shapes=rider.scratch,
        compiler_params=pltpu.CompilerParams(has_side_effects=True, vmem_limit_bytes=VMEM_LIMIT),
    )(*rider.inputs))


def _nothing(ins, outs, scr):
    return None


def _gather_rider(names, shards, conv_w=None):
    n = len(names)
    items = n + (conv_w is not None)
    halves = [BIG_SHARD[nm][0] // 2 for nm in names]

    def parts(ins, outs, scr):
        x, y, c, chips = _place()
        return x, y, c, chips, 2 * x + y, (x, y, 1 - c), scr[:8], scr[8:]

    def piece(outs, w, chip, core_half):
        return outs[w].at[chip, pl.ds(core_half * halves[w], halves[w])]

    def ici(ins, outs, sems, w, k, chip_xy, c, me):
        return _remote(ins[w].at[pl.ds(c * halves[w], halves[w])], piece(outs, w, me, c),
                       sems[0].at[w, k], sems[1].at[w, k], (*chip_xy, c))

    def first(ins, outs, scr):
        x, y, c, chips, me, sibling, sems, bufs = parts(ins, outs, scr)
        legs = _staged_start(ins, bufs, sems[6])
        for w in range(n):
            for k, chip_xy in enumerate(chips):
                ici(ins, outs, sems, w, k, chip_xy, c, me).start()
        if conv_w is not None:
            for k, (px, py) in enumerate(chips):
                _remote(ins[n], outs[n].at[me], sems[4].at[k], sems[5].at[k], (px, py, c)).start()
        _staged_finish(legs, bufs, [o.at[me] for o in outs], sems[7])

    def late(ins, outs, scr):
        x, y, c, chips, me, sibling, sems, bufs = parts(ins, outs, scr)
        for w in range(n):
            for k, (px, py) in enumerate(chips):
                landed = piece(outs, w, 2 * px + py, c)
                _remote(landed, landed, sems[0].at[w, k], sems[1].at[w, k], (px, py, c)).wait_recv()
                _remote(landed, landed, sems[2].at[w, k], sems[3].at[w, k], sibling).start()

    def last(ins, outs, scr):
        x, y, c, chips, me, sibling, sems, bufs = parts(ins, outs, scr)
        for w in range(n):
            for k, (px, py) in enumerate(chips):
                other = piece(outs, w, 2 * px + py, 1 - c)
                _remote(other, other, sems[2].at[w, k], sems[3].at[w, k], sibling).wait_recv()
        if conv_w is not None:
            for k, (px, py) in enumerate(chips):
                got = outs[n].at[2 * px + py]
                _remote(got, got, sems[4].at[k], sems[5].at[k], (px, py, c)).wait_recv()
                _remote(ins[n], outs[n].at[me], sems[4].at[k], sems[5].at[k], (px, py, c)).wait_send()
        for i in range(items):
            pltpu.make_async_copy(bufs[i], outs[i].at[me], sems[7].at[i]).wait()
        for w in range(n):
            for k, (px, py) in enumerate(chips):
                ici(ins, outs, sems, w, k, (px, py), c, me).wait_send()
                landed = piece(outs, w, 2 * px + py, c)
                _remote(landed, landed, sems[2].at[w, k], sems[3].at[w, k], sibling).wait_send()

    out_shape = [jax.ShapeDtypeStruct((N_SHARD,) + BIG_SHARD[nm], BF16) for nm in names]
    stage = [pltpu.VMEM(BIG_SHARD[nm], BF16) for nm in names]
    inputs = list(shards)
    if conv_w is not None:
        out_shape.append(jax.ShapeDtypeStruct((N_SHARD,) + conv_w.shape, F32))
        stage.append(pltpu.VMEM(conv_w.shape, F32))
        inputs.append(conv_w)
    scratch = ([pltpu.SemaphoreType.DMA((n, N_CHIP_PEERS))] * 4 + [pltpu.SemaphoreType.DMA((N_CHIP_PEERS,))] * 2
               + [pltpu.SemaphoreType.DMA((items,))] * 2 + stage)
    return _Rider(inputs, out_shape, scratch, first, late, last)


def _pair_exchange_rider(names, grads):
    def copies(ins, outs, scr):
        x, y, c, _ = _place()
        out = []
        for w, nm in enumerate(names):
            h = BIG_SHARD[nm][0] // 2
            out.append(_remote(ins[w].at[:, pl.ds((1 - c) * h, h), :], outs[w], scr[0].at[w], scr[1].at[w], (x, y, 1 - c)))
        return out

    def first(ins, outs, scr):
        for cp in copies(ins, outs, scr):
            cp.start()

    def last(ins, outs, scr):
        for cp in copies(ins, outs, scr):
            cp.wait()

    out_shape = [jax.ShapeDtypeStruct((N_SHARD, BIG_SHARD[nm][0] // 2, BIG_SHARD[nm][1]), F32) for nm in names]
    return _Rider(grads, out_shape, [pltpu.SemaphoreType.DMA((len(names),))] * 2, first, _nothing, last)


def _pair_add(core, grad, other, name):
    _, r, cols = grad.shape
    h = r // 2
    th = min(h, 256)
    per = h // th

    def body(c_ref, g_ref, o_ref, out_ref):
        out_ref[...] = (g_ref[...] + o_ref[...]).astype(BF16)

    return pl.pallas_call(
        body, name=name,
        grid_spec=pltpu.PrefetchScalarGridSpec(
            num_scalar_prefetch=1, grid=(N_SHARD, per),
            in_specs=[pl.BlockSpec((1, th, cols), lambda j, i, c_ref: (j, c_ref[0] * per + i, 0)),
                      pl.BlockSpec((1, th, cols), lambda j, i, c_ref: (j, i, 0))],
            out_specs=pl.BlockSpec((1, th, cols), lambda j, i, c_ref: (j, i, 0))),
        out_shape=jax.ShapeDtypeStruct((N_SHARD, h, cols), BF16),
        compiler_params=_params(2),
    )(core, grad, other)


def _chip_exchange_rider(parts):
    n = len(parts)

    def sends(ins, outs, scr):
        x, y, c, chips = _place()
        me = 2 * x + y
        return [_remote(ins[w].at[2 * px + py], outs[w].at[me], scr[0].at[w, k], scr[1].at[w, k], (px, py, c))
                for w in range(n) for k, (px, py) in enumerate(chips)]

    def first(ins, outs, scr):
        x, y, c, chips = _place()
        me = 2 * x + y
        legs = _staged_start([r.at[me] for r in ins], scr[4:], scr[2])
        for cp in sends(ins, outs, scr):
            cp.start()
        _staged_finish(legs, scr[4:], [o.at[me] for o in outs], scr[3])

    def last(ins, outs, scr):
        x, y, c, chips = _place()
        me = 2 * x + y
        for w in range(n):
            for k, (px, py) in enumerate(chips):
                got = outs[w].at[2 * px + py]
                _remote(got, got, scr[0].at[w, k], scr[1].at[w, k], (px, py, c)).wait_recv()
        for cp in sends(ins, outs, scr):
            cp.wait_send()
        for w in range(n):
            pltpu.make_async_copy(scr[4 + w], outs[w].at[me], scr[3].at[w]).wait()

    out_shape = [jax.ShapeDtypeStruct(p.shape, BF16) for p in parts]
    scratch = ([pltpu.SemaphoreType.DMA((n, N_CHIP_PEERS))] * 2 + [pltpu.SemaphoreType.DMA((n,))] * 2
               + [pltpu.VMEM(p.shape[1:], BF16) for p in parts])
    return _Rider(parts, out_shape, scratch, first, _nothing, last)


def _chip_sum(parts, name):
    _, h, cols = parts.shape
    th = min(h, 256)

    def body(p_ref, out_ref):
        acc = p_ref[0].astype(F32)
        for j in range(1, N_SHARD):
            acc = acc + p_ref[j].astype(F32)
        out_ref[...] = acc

    return pl.pallas_call(
        body, name=name, grid=(h // th,),
        in_specs=[pl.BlockSpec((N_SHARD, th, cols), lambda i: (0, i, 0))],
        out_specs=pl.BlockSpec((th, cols), lambda i: (i, 0)),
        out_shape=jax.ShapeDtypeStruct((h, cols), F32),
        compiler_params=_params(),
    )(parts)


def _pair_share_rider(names, halves):
    n = len(names)
    hs = [BIG_SHARD[nm][0] // 2 for nm in names]

    def mine(outs, c):
        return [outs[w].at[pl.ds(c * hs[w], hs[w])] for w in range(n)]

    def first(ins, outs, scr):
        x, y, c, _ = _place()
        legs = _staged_start(ins, scr[4:], scr[2])
        for w, dst in enumerate(mine(outs, c)):
            _remote(ins[w], dst, scr[0].at[w], scr[1].at[w], (x, y, 1 - c)).start()
        _staged_finish(legs, scr[4:], mine(outs, c), scr[3])

    def last(ins, outs, scr):
        x, y, c, _ = _place()
        for w, (theirs, dst) in enumerate(zip(mine(outs, 1 - c), mine(outs, c))):
            _remote(theirs, theirs, scr[0].at[w], scr[1].at[w], (x, y, 1 - c)).wait_recv()
            _remote(ins[w], dst, scr[0].at[w], scr[1].at[w], (x, y, 1 - c)).wait_send()
            pltpu.make_async_copy(scr[4 + w], dst, scr[3].at[w]).wait()

    out_shape = [jax.ShapeDtypeStruct(BIG_SHARD[nm], F32) for nm in names]
    scratch = [pltpu.SemaphoreType.DMA((n,))] * 4 + [pltpu.VMEM((h, BIG_SHARD[nm][1]), F32) for nm, h in zip(names, hs)]
    return _Rider(halves, out_shape, scratch, first, _nothing, last)


N_DEV = 8


def _all_peers(x, y, c):
    return [((1 - x) if fx else x, (1 - y) if fy else y, (1 - c) if fc else c)
            for fx in (0, 1) for fy in (0, 1) for fc in (0, 1) if fx or fy or fc]


def _small_gather_rider(vec):
    def sends(ins, outs, scr):
        x, y, c, _ = _place()
        me = 4 * x + 2 * y + c
        return [_remote(ins[0], outs[0].at[me], scr[0].at[k], scr[1].at[k], dev) for k, dev in enumerate(_all_peers(x, y, c))]

    def first(ins, outs, scr):
        x, y, c, _ = _place()
        legs = _staged_start(ins, scr[4:], scr[2])
        for cp in sends(ins, outs, scr):
            cp.start()
        _staged_finish(legs, scr[4:], [outs[0].at[4 * x + 2 * y + c]], scr[3])

    def last(ins, outs, scr):
        x, y, c, _ = _place()
        for k, (px, py, pc) in enumerate(_all_peers(x, y, c)):
            got = outs[0].at[4 * px + 2 * py + pc]
            _remote(got, got, scr[0].at[k], scr[1].at[k], (px, py, pc)).wait_recv()
        for cp in sends(ins, outs, scr):
            cp.wait_send()
        pltpu.make_async_copy(scr[4], outs[0].at[4 * x + 2 * y + c], scr[3].at[0]).wait()

    scratch = ([pltpu.SemaphoreType.DMA((N_DEV - 1,))] * 2 + [pltpu.SemaphoreType.DMA((1,))] * 2
               + [pltpu.VMEM(vec.shape, F32)])
    return _Rider([vec], [jax.ShapeDtypeStruct((N_DEV,) + vec.shape, F32)], scratch, first, _nothing, last)


def _sum_devices(gathered):
    def body(g_ref, out_ref):
        acc = g_ref[0]
        for j in range(1, N_DEV):
            acc = acc + g_ref[j]
        out_ref[...] = acc

    return pl.pallas_call(body, name="sum_devices", out_shape=jax.ShapeDtypeStruct(gathered.shape[1:], F32),
                          compiler_params=_params(0))(gathered)


def _allreduce_small(vec):
    rows = vec.shape[0]

    def body(v_ref, sum_ref, gat_ref, send, recv, loc_sem):
        x, y, c, chips = _place()
        sibling = (x, y, 1 - c)
        slot = lambda px, py, pc: gat_ref.at[4 * px + 2 * py + pc]
        lc = pltpu.make_async_copy(v_ref, slot(x, y, c), loc_sem)
        lc.start()
        sends = [_remote(v_ref, slot(x, y, c), send.at[0], recv.at[0], sibling)]
        sends += [_remote(v_ref, slot(x, y, c), send.at[1 + k], recv.at[1 + k], (px, py, c))
                  for k, (px, py) in enumerate(chips)]
        for cp in sends:
            cp.start()
        for k, (px, py) in enumerate(chips):
            got = slot(px, py, c)
            _remote(got, got, send.at[1 + k], recv.at[1 + k], (px, py, c)).wait_recv()
            cp = _remote(got, got, send.at[4 + k], recv.at[4 + k], sibling)
            cp.start()
            sends.append(cp)
        got = slot(x, y, 1 - c)
        _remote(got, got, send.at[0], recv.at[0], sibling).wait_recv()
        for k, (px, py) in enumerate(chips):
            got = slot(px, py, 1 - c)
            _remote(got, got, send.at[4 + k], recv.at[4 + k], sibling).wait_recv()
        for cp in sends:
            cp.wait_send()
        lc.wait()
        acc = gat_ref[0]
        for j in range(1, N_DEV):
            acc = acc + gat_ref[j]
        sum_ref[...] = acc

    total, _ = pl.pallas_call(
        body, name="allreduce_small",
        in_specs=[_whole_vmem()], out_specs=[_whole_vmem(), _whole_vmem()],
        out_shape=[jax.ShapeDtypeStruct((rows, 128), F32), jax.ShapeDtypeStruct((N_DEV, rows, 128), F32)],
        scratch_shapes=[pltpu.SemaphoreType.DMA((N_DEV - 1,))] * 2 + [pltpu.SemaphoreType.DMA(())],
        compiler_params=pltpu.CompilerParams(has_side_effects=True, vmem_limit_bytes=VMEM_LIMIT),
    )(vec)
    return total


def _adam_math(w_ref, g_ref, m_ref, v_ref, d_ref, m2_ref, v2_ref):
    c1 = 1.0 - ADAM_B1 ** ADAM_STEP
    c2 = 1.0 - ADAM_B2 ** ADAM_STEP
    gv = g_ref[...]
    m2 = ADAM_B1 * m_ref[...] + (1.0 - ADAM_B1) * gv
    v2 = ADAM_B2 * v_ref[...] + (1.0 - ADAM_B2) * (gv * gv)
    m2_ref[...] = m2
    v2_ref[...] = v2
    d_ref[...] = -ADAM_LR * ((m2 / c1) / (jnp.sqrt(v2 / c2) + ADAM_EPS) + ADAM_WD * w_ref[...])


def _adamw_many(ws, gs, ms, vs):
    n = len(ws)

    def body(*refs):
        for i in range(n):
            _adam_math(*[refs[k * n + i] for k in range(7)])

    shapes = [jax.ShapeDtypeStruct(w.shape, F32) for w in ws]
    res = pl.pallas_call(body, name="adamw_small", out_shape=shapes * 3, compiler_params=_params(0))(*ws, *gs, *ms, *vs)
    return res[:n], res[n:2 * n], res[2 * n:]


def _adamw(w, g, m, v, name):
    rows, cols = w.shape
    tr = 256 if rows % 256 == 0 else rows

    def body(w_ref, g_ref, m_ref, v_ref, d_ref, m2_ref, v2_ref):
        _adam_math(w_ref, g_ref, m_ref, v_ref, d_ref, m2_ref, v2_ref)

    spec = pl.BlockSpec((tr, cols), lambda i: (i, 0))
    shp = jax.ShapeDtypeStruct((rows, cols), F32)
    return pl.pallas_call(
        body, name=name, grid=(rows // tr,), in_specs=[spec] * 4, out_specs=[spec] * 3, out_shape=[shp] * 3,
        compiler_params=_params(),
    )(w, g, m, v)


SMALL = (
    ("attn_norm_g", (1, 1024)), ("conv_w", (1, 4, 512)), ("conv_b", (1, 512)),
    ("lru_wa_fwd", (1, 8, 64, 64)), ("lru_ba_fwd", (1, 512)), ("lru_wx_fwd", (1, 8, 64, 64)), ("lru_bx_fwd", (1, 512)),
    ("lru_lam_fwd", (1, 512)),
    ("lru_wa_bwd", (1, 8, 64, 64)), ("lru_ba_bwd", (1, 512)), ("lru_wx_bwd", (1, 8, 64, 64)), ("lru_bx_bwd", (1, 512)),
    ("lru_lam_bwd", (1, 512)),
    ("rel_bias", (32, 8)), ("norm_rnn_g", (1, 512)), ("norm_attn_g", (1, 512)), ("mlp_norm_g", (1, 1024)),
    ("final_norm_g", (1024,)),
)
PACK_ROW = 8 * 128


def _pack(parts):
    flat = jnp.concatenate([p.reshape(-1) for p in parts])
    pad = (-flat.shape[0]) % PACK_ROW
    return jnp.pad(flat, (0, pad)).reshape(-1, 128)


def _unpack(packed, shapes):
    flat = packed.reshape(-1)
    out, off = [], 0
    for shp in shapes:
        n = int(np.prod(shp))
        out.append(flat[off:off + n].reshape(shp))
        off += n
    return out


WEIGHT_ORDER = ("attn_norm_g", "w_in", "conv_w", "conv_b", "lru_wa_fwd", "lru_ba_fwd", "lru_wx_fwd", "lru_bx_fwd",
                "lru_lam_fwd", "lru_wa_bwd", "lru_ba_bwd", "lru_wx_bwd", "lru_bx_bwd", "lru_lam_bwd", "rel_bias",
                "norm_rnn_g", "norm_attn_g", "w_out", "mlp_norm_g", "w_up", "w_down", "final_norm_g")


def kernel(x, attn_norm_g, w_in, conv_w, conv_b, lru_wa_fwd, lru_ba_fwd, lru_wx_fwd, lru_bx_fwd, lru_lam_fwd, lru_wa_bwd, lru_ba_bwd, lru_wx_bwd, lru_bx_bwd, lru_lam_bwd, rel_bias, norm_rnn_g, norm_attn_g, w_out, mlp_norm_g, w_up, w_down, final_norm_g, loss_target, m_attn_norm_g, m_w_in, m_conv_w, m_conv_b, m_lru_wa_fwd, m_lru_ba_fwd, m_lru_wx_fwd, m_lru_bx_fwd, m_lru_lam_fwd, m_lru_wa_bwd, m_lru_ba_bwd, m_lru_wx_bwd, m_lru_bx_bwd, m_lru_lam_bwd, m_rel_bias, m_norm_rnn_g, m_norm_attn_g, m_w_out, m_mlp_norm_g, m_w_up, m_w_down, m_final_norm_g, v_attn_norm_g, v_w_in, v_conv_w, v_conv_b, v_lru_wa_fwd, v_lru_ba_fwd, v_lru_wx_fwd, v_lru_bx_fwd, v_lru_lam_fwd, v_lru_wa_bwd, v_lru_ba_bwd, v_lru_wx_bwd, v_lru_bx_bwd, v_lru_lam_bwd, v_rel_bias, v_norm_rnn_g, v_norm_attn_g, v_w_out, v_mlp_norm_g, v_w_up, v_w_down, v_final_norm_g):
    given = dict(locals())
    w = {n: given[n] for n in WEIGHT_ORDER}
    m = {n: given["m_" + n] for n in WEIGHT_ORDER}
    v = {n: given["v_" + n] for n in WEIGHT_ORDER}

    chip = lax.axis_index("x") * 2 + lax.axis_index("y")
    core = lax.axis_index("c")

    shards = {n: w[n][0].astype(BF16) for n in BIG}
    shards["conv_w"] = w["conv_w"][0]
    p = {n: (t[0] if t.ndim >= 3 else t) for n, t in w.items() if n not in BIG and n != "conv_w"}
    p["final_norm_g"] = w["final_norm_g"].reshape(1, D_MODEL)

    _, grad_x, small, gathered, big, reduced = _local_step(x[0], loss_target[0], p, shards)

    late = tuple(big)
    grads = [big[n] for n in late]
    others = _run_rider(_pair_exchange_rider(late, grads), "grad_pair_exchange")
    core_arr = core.reshape(1).astype(jnp.int32)
    parts = [_pair_add(core_arr, g, o, "grad_pair_add_" + n) for n, g, o in zip(late, grads, others)]
    landed = _run_rider(_chip_exchange_rider(parts), "grad_chip_exchange")
    halves = [_chip_sum(t, "grad_chip_sum_" + n) for n, t in zip(late, landed)]
    reduced.update(zip(late, _run_rider(_pair_share_rider(late, halves), "grad_pair_share")))

    early_small = [(n, shp) for n, shp in SMALL if n not in small]
    late_small = [(n, shp) for n, shp in SMALL if n in small]
    *early_g, loss = _unpack(_sum_devices(gathered), [shp for _, shp in early_small] + [(1,)])
    late_g = _unpack(_allreduce_small(_pack([small[n].reshape(shp) for n, shp in late_small])),
                     [shp for _, shp in late_small])
    g = dict(zip([n for n, _ in early_small + late_small], early_g + late_g))
    g["conv_w"] = lax.dynamic_slice_in_dim(g["conv_w"], chip * (D_RNN // N_SHARD), D_RNN // N_SHARD, axis=2)
    for n in BIG:
        g[n] = reduced[n][None]

    delta, new_m, new_v = {}, {}, {}
    for n in BIG:
        d2, m2, v2 = _adamw(w[n][0], reduced[n], m[n][0], v[n][0], "adamw_" + n)
        delta[n], new_m[n], new_v[n] = d2[None], m2[None], v2[None]
    names = [n for n, _ in SMALL]
    for dst, src in zip((delta, new_m, new_v), _adamw_many(*[[t[n] for n in names] for t in (w, g, m, v)])):
        dst.update(dict(zip(names, src)))

    return (loss.reshape(()), grad_x[None], *[g[n] for n in WEIGHT_ORDER], *[delta[n] for n in WEIGHT_ORDER],
            *[new_m[n] for n in WEIGHT_ORDER], *[new_v[n] for n in WEIGHT_ORDER])
```

```python
import functools
import math

import numpy as np
import jax
import jax.numpy as jnp
from jax import lax
from jax.experimental import pallas as pl
from jax.experimental.pallas import tpu as pltpu

F32 = jnp.float32
BF16 = jnp.bfloat16

D_MODEL = 1024
D_RNN = 512
D_ATTN = 512
N_HEADS = 8
HEAD_DIM = 64
N_RNN_BLOCKS = 8
RNN_BLOCK = 64
D_IN = 2 * D_RNN + 3 * D_ATTN
D_FF = 4 * D_MODEL
N_SHARD = 4
IN_BLK = D_IN // N_SHARD
OUT_BLK = D_MODEL // N_SHARD
FF_BLK = D_FF // N_SHARD
EPS = 1e-6
NEG_INF = -1e30
LRU_C = 8.0
DILATIONS = (1, 4, 16)
F32_LAYOUT = 4
HALF_WIN = 64
Q_BLK = 128
K_WIN = 256
N_BUCKETS = 32
MAX_DISTANCE = 1024
ATTN_SCALE = HEAD_DIM ** -0.5

ADAM_LR = 0.001
ADAM_B1 = 0.9
ADAM_B2 = 0.999
ADAM_EPS = 1e-08
ADAM_WD = 0.01
ADAM_STEP = 10

TS = 512
TS_RNN = 1024
TS_MLP = 256
TS_INPROJ_BWD = 512
ATTN_SUB = 16
TK_DW = 4096
SCAN_UNROLL = 8
SUB = 8
VMEM_LIMIT = 56 * 1024 * 1024
GELU_C0 = math.sqrt(2.0 / math.pi)
GELU_C1 = 0.044715

MESH = pl.DeviceIdType.MESH


def _params(n_grid=1):
    return pltpu.CompilerParams(vmem_limit_bytes=VMEM_LIMIT, dimension_semantics=("arbitrary",) * n_grid)


def _whole_vmem():
    return pl.BlockSpec(memory_space=pltpu.VMEM)


def _rows(width, tile=TS):
    return pl.BlockSpec((tile, width), lambda i: (i, 0))


def _sigmoid(z):
    return 0.5 * jnp.tanh(0.5 * z) + 0.5


def _log1p(u):
    w = 1.0 + u
    return jnp.where(w == 1.0, u, jnp.log(w) * (u / (w - 1.0)))


def _softplus(z):
    return jnp.maximum(z, 0.0) + _log1p(jnp.exp(-jnp.abs(z)))


def _gelu_parts(g):
    inner = GELU_C0 * (g + GELU_C1 * g * g * g)
    t = jnp.tanh(inner)
    val = 0.5 * g * (1.0 + t)
    dinner = GELU_C0 * (1.0 + 3.0 * GELU_C1 * g * g)
    grad = 0.5 * (1.0 + t) + 0.5 * g * (1.0 - t * t) * dinner
    return val, grad


def _rms(x):
    rstd = lax.rsqrt(jnp.mean(x * x, axis=-1, keepdims=True) + EPS)
    return rstd, x * rstd


def _rms_bwd(dy, g, xhat, rstd):
    dxh = dy * g
    dx = rstd * (dxh - xhat * jnp.mean(dxh * xhat, axis=-1, keepdims=True))
    dg = jnp.sum(dy * xhat, axis=0, keepdims=True)
    return dx, dg


def _dot(a, b):
    return jnp.dot(a, b, preferred_element_type=F32)


def _dot_nt(a, b):
    return lax.dot_general(a, b, (((1,), (1,)), ((), ())), preferred_element_type=F32)


def _dot_tn(a, b):
    return lax.dot_general(a, b, (((0,), (0,)), ((), ())), preferred_element_type=F32)


def _shifted(tile, prev8, next8, k):
    n = tile.shape[0]
    row = lax.broadcasted_iota(jnp.int32, tile.shape, 0)
    if k == 0:
        return tile
    if k < 0:
        r = pltpu.roll(tile, -k, 0)
        for j in range(-k):
            r = jnp.where(row == j, prev8[SUB + j + k:SUB + j + k + 1, :], r)
        return r
    r = pltpu.roll(tile, n - k, 0)
    for j in range(k):
        r = jnp.where(row == n - k + j, next8[j:j + 1, :], r)
    return r


def _to_lane_blocks(val, s_ref):
    for j in range(val.shape[1] // 128):
        s_ref[j] = val[:, j * 128:(j + 1) * 128]


def _from_lane_blocks(s_ref):
    return jnp.concatenate([s_ref[j] for j in range(s_ref.shape[0])], axis=-1)


def _class_rows(s_ref, r, dil):
    n = s_ref.shape[1] // dil
    return jnp.concatenate([s_ref[j, pl.ds(r, n, stride=dil), :] for j in range(s_ref.shape[0])], axis=-1)


def _split_classes(val, s_ref, out_ref, dil):
    _to_lane_blocks(val, s_ref)
    for r in range(dil):
        out_ref[r] = _class_rows(s_ref, r, dil).astype(out_ref.dtype)


def _merge_classes(in_ref, s_ref, dil, also_ref=None):
    n = s_ref.shape[1] // dil
    for r in range(dil):
        v = in_ref[r].astype(F32)
        if also_ref is not None:
            v = v + also_ref[r].astype(F32)
        for j in range(s_ref.shape[0]):
            s_ref[j, pl.ds(r, n, stride=dil), :] = v[:, j * 128:(j + 1) * 128]
    return _from_lane_blocks(s_ref)


def _class_spec(dil, tile=TS):
    return pl.BlockSpec((dil, tile // dil, 512), lambda i: (0, i, 0))


def _class_shape(S, dil, dtype):
    return jax.ShapeDtypeStruct((dil, S // dil, 512), dtype)


def _scan_tile(a_ref, b_ref, h_ref, carry_ref, reverse):
    n = a_ref.shape[0]
    width = a_ref.shape[1]
    groups = n // SUB
    row = lax.broadcasted_iota(jnp.int32, (SUB, width), 0)

    def group_scan(g):
        r0 = pl.multiple_of(g * SUB, SUB)
        a = a_ref[pl.ds(r0, SUB), :]
        b = b_ref[pl.ds(r0, SUB), :]
        for s in (1, 2, 4):
            if reverse:
                a_sh = pltpu.roll(a, SUB - s, 0)
                b_sh = pltpu.roll(b, SUB - s, 0)
                m = row < SUB - s
            else:
                a_sh = pltpu.roll(a, s, 0)
                b_sh = pltpu.roll(b, s, 0)
                m = row >= s
            b = jnp.where(m, a * b_sh + b, b)
            a = jnp.where(m, a * a_sh, a)
        return r0, a, b

    def step(i, carry):
        first = i * SCAN_UNROLL
        order = [(groups - 1 - (first + u)) if reverse else (first + u) for u in range(SCAN_UNROLL)]
        scans = [group_scan(g) for g in order]
        for r0, a, b in scans:
            h = b + a * carry
            h_ref[pl.ds(r0, SUB), :] = h
            edge = h[0:1, :] if reverse else h[SUB - 1:SUB, :]
            carry = jnp.broadcast_to(edge, (SUB, width))
        return carry

    carry_ref[...] = lax.fori_loop(0, groups // SCAN_UNROLL, step, carry_ref[...])


def _conv_fwd(xr, prev8, next8, cw, cb):
    y = cb + _shifted(xr, prev8, next8, -2) * cw[0:1, :]
    y = y + _shifted(xr, prev8, next8, -1) * cw[1:2, :]
    y = y + xr * cw[2:3, :]
    y = y + _shifted(xr, prev8, next8, 1) * cw[3:4, :]
    return y


def _lru_gates(xc, wa_ref, ba, wx_ref, bx, lam):
    xcb = xc.astype(BF16)
    r = _sigmoid(_dot(xcb, wa_ref[...]) + ba)
    i = _sigmoid(_dot(xcb, wx_ref[...]) + bx)
    cl = -LRU_C * _softplus(-lam)
    la = cl * r
    a = jnp.exp(la)
    m2 = -jnp.tanh(la) * (a * a + 1.0)
    inv = jnp.where(m2 > 0.0, lax.rsqrt(m2), 0.0)
    mult = m2 * inv
    return xcb, r, i, cl, a, mult, inv


def _inproj_fwd(x, g1, w_in, rider=None):
    S = x.shape[0]

    def body(x_ref, g_ref, w_ref, xr_ref, gate_ref, *rest):
        qkv_refs, s_ref, s4_ref, w_full = rest[:9], rest[9], rest[10], rest[11]

        @pl.when(pl.program_id(0) == 0)
        def _():
            for j in range(N_SHARD):
                w_full[:, j * IN_BLK:(j + 1) * IN_BLK] = w_ref[j]

        _, xh = _rms(x_ref[...])
        h = (xh * g_ref[...]).astype(BF16)
        proj = _dot(h, w_full[...])
        xr_ref[...] = proj[:, 0:512]
        gate_ref[...] = proj[:, 512:1024]
        for t in range(3):
            val = proj[:, 1024 + 512 * t:1536 + 512 * t]
            d1_ref, d4_ref, d16_ref = qkv_refs[3 * t:3 * t + 3]
            d1_ref[0] = val.astype(BF16)
            _to_lane_blocks(val, s_ref)
            for r4 in range(4):
                c4 = _class_rows(s_ref, r4, 4)
                d4_ref[r4] = c4.astype(BF16)
                _to_lane_blocks(c4, s4_ref.at[r4])
            for r4 in range(4):
                for m in range(4):
                    d16_ref[r4 + 4 * m] = _class_rows(s4_ref.at[r4], m, 4).astype(BF16)

    f = jax.ShapeDtypeStruct((S, 512), F32)
    return _call(
        body, "inproj_fwd", (S // TS,),
        [_rows(D_MODEL), _whole_vmem(), _whole_vmem()],
        [_rows(512)] * 2 + [_class_spec(d) for d in DILATIONS] * 3,
        [f, f] + [_class_shape(S, d, BF16) for d in DILATIONS] * 3,
        [pltpu.VMEM((4, TS, 128), F32), pltpu.VMEM((4, 4, TS // 4, 128), F32), pltpu.VMEM((D_MODEL, D_IN), BF16)],
        (x, g1, w_in), rider)


def _halo_specs(S, order, tile=TS):
    per = tile // SUB
    last = S // SUB - 1
    return [
        pl.BlockSpec((tile, 512), lambda i: (order(i), 0)),
        pl.BlockSpec((SUB, 512), lambda i: (jnp.maximum(order(i) * per - 1, 0), 0)),
        pl.BlockSpec((SUB, 512), lambda i: (jnp.minimum((order(i) + 1) * per, last), 0)),
    ]


def _rnn_fwd(xr, conv_w, conv_b, wa, ba, wx, bx, lam, reverse, rider=None, xc=None):
    S = xr.shape[0]
    nt = S // TS_RNN
    order = (lambda i: nt - 1 - i) if reverse else (lambda i: i)
    with_conv = xc is None
    n_x = 5 if with_conv else 1
    tile = pl.BlockSpec((TS_RNN, 512), lambda i: (order(i), 0))

    def body(*refs):
        wa_ref, ba_ref, wx_ref, bx_ref, lam_ref, h_ref = refs[n_x:n_x + 6]
        a_s, b_s, carry = refs[-3:]
        i = pl.program_id(0)
        t = order(i)

        @pl.when(i == 0)
        def _():
            carry[...] = jnp.zeros_like(carry)

        if with_conv:
            x_ref, xp_ref, xn_ref, cw_ref, cb_ref = refs[:5]
            prev8 = jnp.where(t > 0, xp_ref[...], 0.0)
            next8 = jnp.where(t < nt - 1, xn_ref[...], 0.0)
            xcv = _conv_fwd(x_ref[...], prev8, next8, cw_ref[...], cb_ref[...])
            refs[n_x + 6][...] = xcv
        else:
            xcv = refs[0][...]
        _, _, gi, _, a, mult, _ = _lru_gates(xcv, wa_ref, ba_ref[...], wx_ref, bx_ref[...], lam_ref[...])
        a_s[...] = a
        b_s[...] = mult * (gi * xcv)
        _scan_tile(a_s, b_s, h_ref, carry, reverse)

    f512 = jax.ShapeDtypeStruct((S, 512), F32)
    return _call(
        body, "rnn_fwd_rev" if reverse else "rnn_fwd_fwd", (nt,),
        (_halo_specs(S, order, TS_RNN) + [_whole_vmem()] * 2 if with_conv else [tile]) + [_whole_vmem()] * 5,
        [tile, tile] if with_conv else [tile], [f512, f512] if with_conv else [f512],
        [pltpu.VMEM((TS_RNN, 512), F32), pltpu.VMEM((TS_RNN, 512), F32), pltpu.VMEM((SUB, 512), F32)],
        ((xr, xr, xr, conv_w, conv_b) if with_conv else (xc,)) + (wa, ba, wx, bx, lam), rider)


def _mix_fwd(o3, l3, hf, hb, gate, x, g_rnn, g_attn, w_out):
    S = x.shape[0]

    def body(o1, o2, o3_, l1, l2, l3_, hf_ref, hb_ref, gate_ref, x_ref, gr_ref, ga_ref, w_ref,
             x1_ref, mix_ref, ya1, ya2, ls1, ls2, s_ref):
        la, lb, lc = l1[0], _merge_classes(l2, s_ref, F32_LAYOUT), _merge_classes(l3_, s_ref, F32_LAYOUT)
        m = jnp.maximum(jnp.maximum(la, lb), lc)
        ea, eb, ec = jnp.exp(la - m), jnp.exp(lb - m), jnp.exp(lc - m)
        den = ea + eb + ec
        lse = m + jnp.log(den)
        ya = (ea * o1[0] + eb * _merge_classes(o2, s_ref, F32_LAYOUT) + ec * _merge_classes(o3_, s_ref, F32_LAYOUT)) / den
        ya1[0] = ya
        ls1[0] = lse
        _split_classes(ya, s_ref, ya2, F32_LAYOUT)
        _split_classes(lse, s_ref, ls2, F32_LAYOUT)
        gg, _ = _gelu_parts(gate_ref[...])
        yr = (hf_ref[...] + hb_ref[...]) * gg
        _, xh_r = _rms(yr)
        _, xh_a = _rms(ya)
        mix = jnp.concatenate([xh_r * gr_ref[...], xh_a * ga_ref[...]], axis=-1).astype(BF16)
        mix_ref[...] = mix
        acc = x_ref[...]
        for j in range(N_SHARD):
            acc = acc + _dot(mix[:, j * OUT_BLK:(j + 1) * OUT_BLK], w_ref[j])
        x1_ref[...] = acc

    one, four = _class_spec(1), _class_spec(F32_LAYOUT)
    return pl.pallas_call(
        body, grid=(S // TS,), name="mix_fwd",
        in_specs=[one, four, four] * 2 + [_rows(512)] * 3 + [_rows(D_MODEL)] + [_whole_vmem()] * 3,
        out_specs=[_rows(D_MODEL), _rows(D_MODEL)] + [one, four] * 2,
        out_shape=[jax.ShapeDtypeStruct((S, D_MODEL), F32), jax.ShapeDtypeStruct((S, D_MODEL), BF16)]
        + [_class_shape(S, 1, F32), _class_shape(S, F32_LAYOUT, F32)] * 2,
        scratch_shapes=[pltpu.VMEM((4, TS, 128), F32)],
        compiler_params=_params(),
    )(*o3, *l3, hf, hb, gate, x, g_rnn, g_attn, w_out)


def _mlp_fwd_bwd(x1, target, g_mlp, g_fin, w_up, w_down):
    S = x1.shape[0]
    tm = TS_MLP

    def body(x1_ref, t_ref, gm_ref, gf_ref, wu_ref, wd_ref,
             dx1_ref, h2_ref, a2_ref, du_ref, dx2_ref, loss_ref, dgf_ref, dgm_ref, relu_s):
        @pl.when(pl.program_id(0) == 0)
        def _():
            loss_ref[...] = jnp.zeros_like(loss_ref)
            dgf_ref[...] = jnp.zeros_like(dgf_ref)
            dgm_ref[...] = jnp.zeros_like(dgm_ref)

        x1v = x1_ref[...]
        rstd1, xh1 = _rms(x1v)
        h2 = (xh1 * gm_ref[...]).astype(BF16)
        h2_ref[...] = h2
        x2 = x1v
        for j in range(N_SHARD):
            r = jnp.maximum(_dot(h2, wu_ref[j]), 0.0)
            relu_s[j] = r
            a2 = (r * r).astype(BF16)
            a2_ref[:, j * FF_BLK:(j + 1) * FF_BLK] = a2
            x2 = x2 + _dot(a2, wd_ref[j])
        rstd2, xh2 = _rms(x2)
        err = xh2 * gf_ref[...] - t_ref[...]
        loss_ref[...] += jnp.sum(err * err, axis=0, keepdims=True)
        dy = err * (1.0 / D_MODEL)
        dx2, dgf = _rms_bwd(dy, gf_ref[...], xh2, rstd2)
        dgf_ref[...] += dgf
        dx2b = dx2.astype(BF16)
        dx2_ref[...] = dx2b
        dh2 = jnp.zeros((tm, D_MODEL), F32)
        for j in range(N_SHARD):
            du = (_dot_nt(dx2b, wd_ref[j]) * (2.0 * relu_s[j])).astype(BF16)
            du_ref[:, j * FF_BLK:(j + 1) * FF_BLK] = du
            dh2 = dh2 + _dot_nt(du, wu_ref[j])
        dx1n, dgm = _rms_bwd(dh2, gm_ref[...], xh1, rstd1)
        dgm_ref[...] += dgm
        dx1_ref[...] = dx2 + dx1n

    vec = jax.ShapeDtypeStruct((1, D_MODEL), F32)
    return pl.pallas_call(
        body, grid=(S // tm,), name="mlp_fwd_bwd",
        in_specs=[_rows(D_MODEL, tm), _rows(D_MODEL, tm)] + [_whole_vmem()] * 4,
        out_specs=[_rows(D_MODEL, tm), _rows(D_MODEL, tm), _rows(D_FF, tm), _rows(D_FF, tm), _rows(D_MODEL, tm)]
        + [_whole_vmem()] * 3,
        out_shape=[jax.ShapeDtypeStruct((S, D_MODEL), F32), jax.ShapeDtypeStruct((S, D_MODEL), BF16),
                   jax.ShapeDtypeStruct((S, D_FF), BF16), jax.ShapeDtypeStruct((S, D_FF), BF16),
                   jax.ShapeDtypeStruct((S, D_MODEL), BF16), vec, vec, vec],
        scratch_shapes=[pltpu.VMEM((N_SHARD, tm, FF_BLK), F32)],
        compiler_params=_params(),
    )(x1, target, g_mlp, g_fin, w_up, w_down)


def _mix_bwd(dx1, w_out, mixb, ya, hf, hb, gate, g_rnn, g_attn):
    S = dx1.shape[0]

    def body(dx1_ref, w_ref, mix_ref, ya_ref, hf_ref, hb_ref, gate_ref, gr_ref, ga_ref,
             dhs_ref, dgate_ref, dya1, dya2, dw_ref, dgr_ref, dga_ref, s_ref):
        @pl.when(pl.program_id(0) == 0)
        def _():
            dw_ref[...] = jnp.zeros_like(dw_ref)
            dgr_ref[...] = jnp.zeros_like(dgr_ref)
            dga_ref[...] = jnp.zeros_like(dga_ref)

        dx1b = dx1_ref[...].astype(BF16)
        mix = mix_ref[...]
        for j in range(N_SHARD):
            dw_ref[j] += _dot_tn(mix[:, j * OUT_BLK:(j + 1) * OUT_BLK], dx1b)
        dmix = jnp.concatenate([_dot_nt(dx1b, w_ref[j]) for j in range(N_SHARD)], axis=-1)
        gg, dgg = _gelu_parts(gate_ref[...])
        hs = hf_ref[...] + hb_ref[...]
        rstd_r, xh_r = _rms(hs * gg)
        dyr, dgr = _rms_bwd(dmix[:, 0:D_RNN], gr_ref[...], xh_r, rstd_r)
        dgr_ref[...] += dgr
        rstd_a, xh_a = _rms(ya_ref[0])
        dya, dga = _rms_bwd(dmix[:, D_RNN:], ga_ref[...], xh_a, rstd_a)
        dga_ref[...] += dga
        dya1[0] = dya
        _split_classes(dya, s_ref, dya2, F32_LAYOUT)
        dhs_ref[...] = dyr * gg
        dgate_ref[...] = dyr * hs * dgg

    f512 = jax.ShapeDtypeStruct((S, 512), F32)
    vec = jax.ShapeDtypeStruct((1, 512), F32)
    return pl.pallas_call(
        body, grid=(S // TS,), name="mix_bwd",
        in_specs=[_rows(D_MODEL), _whole_vmem(), _rows(D_MODEL), _class_spec(1)] + [_rows(512)] * 3 + [_whole_vmem()] * 2,
        out_specs=[_rows(512)] * 2 + [_class_spec(1), _class_spec(F32_LAYOUT)] + [_whole_vmem()] * 3,
        out_shape=[f512, f512, _class_shape(S, 1, F32), _class_shape(S, F32_LAYOUT, F32),
                   jax.ShapeDtypeStruct((N_SHARD, OUT_BLK, D_MODEL), F32), vec, vec],
        scratch_shapes=[pltpu.VMEM((4, TS, 128), F32)],
        compiler_params=_params(),
    )(dx1, w_out, mixb, ya, hf, hb, gate, g_rnn, g_attn)


def _rnn_bwd(xc, h, dhs, wa, ba, wx, bx, lam, reverse, rider=None):
    S = xc.shape[0]
    nt = S // TS_RNN
    order = (lambda i: i) if reverse else (lambda i: nt - 1 - i)
    per = TS_RNN // SUB
    last = S // SUB - 1
    if reverse:
        h_halo = pl.BlockSpec((SUB, 512), lambda i: (jnp.minimum((order(i) + 1) * per, last), 0))
    else:
        h_halo = pl.BlockSpec((SUB, 512), lambda i: (jnp.maximum(order(i) * per - 1, 0), 0))
    tile = pl.BlockSpec((TS_RNN, 512), lambda i: (order(i), 0))

    def body(xc_ref, h_ref, hh_ref, dh_ref, wa_ref, ba_ref, wx_ref, bx_ref, lam_ref,
             dxc_ref, dwa_ref, dwx_ref, dvec_ref, a_s, g_s, carry, edge):
        i = pl.program_id(0)
        t = order(i)

        @pl.when(i == 0)
        def _():
            carry[...] = jnp.zeros_like(carry)
            edge[...] = jnp.zeros_like(edge)
            dwa_ref[...] = jnp.zeros_like(dwa_ref)
            dwx_ref[...] = jnp.zeros_like(dwx_ref)
            dvec_ref[...] = jnp.zeros_like(dvec_ref)

        xc = xc_ref[...]
        xcb, r, gi, cl, a, mult, inv_mult = _lru_gates(xc, wa_ref, ba_ref[...], wx_ref, bx_ref[...], lam_ref[...])
        hv = h_ref[...]
        if reverse:
            a_s[...] = _shifted(a, edge[...], None, -1)
            edge[...] = a[TS_RNN - SUB:TS_RNN, :]
            hh = jnp.where(t < nt - 1, hh_ref[...], 0.0)
            h_prev = _shifted(hv, None, hh, 1)
        else:
            a_s[...] = _shifted(a, None, edge[...], 1)
            edge[...] = a[0:SUB, :]
            hh = jnp.where(t > 0, hh_ref[...], 0.0)
            h_prev = _shifted(hv, hh, None, -1)
        _scan_tile(a_s, dh_ref, g_s, carry, not reverse)
        g = g_s[...]
        da = g * h_prev
        gm = g * mult
        d_i = gm * xc
        dmult = g * gi * xc
        dla = da * a - dmult * (a * a) * inv_mult
        d_r = dla * cl
        dpre_r = d_r * r * (1.0 - r)
        dpre_i = d_i * gi * (1.0 - gi)
        dprb = dpre_r.astype(BF16)
        dpib = dpre_i.astype(BF16)
        dwa_ref[...] += _dot_tn(xcb, dprb)
        dwx_ref[...] += _dot_tn(xcb, dpib)
        dvec_ref[0:1, :] += jnp.sum(dpre_r, axis=0, keepdims=True)
        dvec_ref[1:2, :] += jnp.sum(dpre_i, axis=0, keepdims=True)
        dvec_ref[2:3, :] += jnp.sum(dla * r, axis=0, keepdims=True)
        dvec_ref[3:4, :] = dvec_ref[2:3, :] * (LRU_C * _sigmoid(-lam_ref[...]))
        dxc_ref[...] = gm * gi + _dot_nt(dprb, wa_ref[...]) + _dot_nt(dpib, wx_ref[...])

    sq = jax.ShapeDtypeStruct((D_RNN, D_RNN), F32)
    return _call(
        body, "rnn_bwd_rev" if reverse else "rnn_bwd_fwd", (nt,),
        [tile, tile, h_halo, tile] + [_whole_vmem()] * 5,
        [tile, _whole_vmem(), _whole_vmem(), _whole_vmem()],
        [jax.ShapeDtypeStruct((S, 512), F32), sq, sq, jax.ShapeDtypeStruct((SUB, 512), F32)],
        [pltpu.VMEM((TS_RNN, 512), F32), pltpu.VMEM((TS_RNN, 512), F32), pltpu.VMEM((SUB, 512), F32),
         pltpu.VMEM((SUB, 512), F32)],
        (xc, h, h, dhs, wa, ba, wx, bx, lam), rider)


def _inproj_bwd(x, dx1, xr, dxc_f, dxc_b, dgate, dq3, dk3, dv3, g1, conv_w, w_in, rider=None):
    S = x.shape[0]
    tb = TS_INPROJ_BWD
    nt = S // tb
    ident = lambda i: i

    def body(x_ref, dx1_ref, xr_ref, xrp_ref, xrn_ref, cf_ref, cfp_ref, cfn_ref, cb_ref, cbp_ref, cbn_ref, dgate_ref,
             dq1, dq2, dq3_, dk1, dk2, dk3_, dv1, dv2, dv3_, g_ref, cw_ref, w_ref,
             dx_ref, dw_ref, dg_ref, dcw_ref, s_ref, w_full, dw_full, sems):
        i = pl.program_id(0)

        def blocks(full, blocked, k0):
            return [(full.at[:, j * IN_BLK:(j + 1) * IN_BLK], blocked.at[j], sems.at[k0 + j]) for j in range(N_SHARD)]

        @pl.when(i == 0)
        def _():
            copies = [pltpu.make_async_copy(src, dst, sem) for dst, src, sem in blocks(w_full, w_ref, 0)]
            for cp in copies:
                cp.start()
            for cp in copies:
                cp.wait()
            dw_full[...] = jnp.zeros_like(dw_full)
            dg_ref[...] = jnp.zeros_like(dg_ref)
            dcw_ref[...] = jnp.zeros_like(dcw_ref)

        first, last = i > 0, i < nt - 1
        dxc = cf_ref[...] + cb_ref[...]
        dxc_p = jnp.where(first, cfp_ref[...] + cbp_ref[...], 0.0)
        dxc_n = jnp.where(last, cfn_ref[...] + cbn_ref[...], 0.0)
        cw = cw_ref[...]
        dxr = (_shifted(dxc, dxc_p, dxc_n, 2) * cw[0:1, :] + _shifted(dxc, dxc_p, dxc_n, 1) * cw[1:2, :]
               + dxc * cw[2:3, :] + _shifted(dxc, dxc_p, dxc_n, -1) * cw[3:4, :])
        xrv = xr_ref[...]
        xr_p = jnp.where(first, xrp_ref[...], 0.0)
        xr_n = jnp.where(last, xrn_ref[...], 0.0)
        for k, off in enumerate((-2, -1, 0, 1)):
            dcw_ref[k:k + 1, :] += jnp.sum(dxc * _shifted(xrv, xr_p, xr_n, off), axis=0, keepdims=True)
        dcw_ref[4:5, :] += jnp.sum(dxc, axis=0, keepdims=True)

        def total(a, b, c_):
            return a[0].astype(F32) + _merge_classes(b, s_ref, F32_LAYOUT, c_)

        dproj = jnp.concatenate(
            [dxr, dgate_ref[...], total(dq1, dq2, dq3_), total(dk1, dk2, dk3_), total(dv1, dv2, dv3_)],
            axis=-1).astype(BF16)
        xv = x_ref[...]
        rstd, xh = _rms(xv)
        hb = (xh * g_ref[...]).astype(BF16)
        dh = _dot_nt(dproj, w_full[...])
        dw_full[...] += _dot_tn(hb, dproj)
        dxn, dg = _rms_bwd(dh, g_ref[...], xh, rstd)
        dg_ref[...] += dg
        dx_ref[...] = dx1_ref[...] + dxn

        @pl.when(i == nt - 1)
        def _():
            copies = [pltpu.make_async_copy(src, dst, sem) for src, dst, sem in blocks(dw_full, dw_ref, N_SHARD)]
            for cp in copies:
                cp.start()
            for cp in copies:
                cp.wait()

    halo = _halo_specs(S, ident, tb)
    return _call(
        body, "inproj_bwd", (nt,),
        [_rows(D_MODEL, tb), _rows(D_MODEL, tb)] + halo * 3 + [_rows(512, tb)]
        + [_class_spec(1, tb), _class_spec(F32_LAYOUT, tb), _class_spec(F32_LAYOUT, tb)] * 3 + [_whole_vmem()] * 2 + [ANY],
        [_rows(D_MODEL, tb), ANY, _whole_vmem(), _whole_vmem()],
        [jax.ShapeDtypeStruct((S, D_MODEL), F32), jax.ShapeDtypeStruct((N_SHARD, D_MODEL, IN_BLK), F32),
         jax.ShapeDtypeStruct((1, D_MODEL), F32), jax.ShapeDtypeStruct((SUB, 512), F32)],
        [pltpu.VMEM((4, tb, 128), F32), pltpu.VMEM((D_MODEL, D_IN), BF16), pltpu.VMEM((D_MODEL, D_IN), F32),
         pltpu.SemaphoreType.DMA((2 * N_SHARD,))],
        (x, dx1, xr, xr, xr, dxc_f, dxc_f, dxc_f, dxc_b, dxc_b, dxc_b, dgate, *dq3, *dk3, *dv3, g1, conv_w, w_in), rider)


def _dw_matmul(a, b, a_cols, b_cols, name):
    S = a.shape[0]
    tk = min(S, TK_DW)
    a_shared = a.shape[1] == a_cols
    b_shared = b.shape[1] == b_cols

    def body(a_ref, b_ref, o_ref):
        @pl.when(pl.program_id(1) == 0)
        def _():
            o_ref[...] = jnp.zeros_like(o_ref)
        o_ref[0] += _dot_tn(a_ref[...], b_ref[...])

    return pl.pallas_call(
        body, grid=(N_SHARD, S // tk), name=name,
        in_specs=[pl.BlockSpec((tk, a_cols), (lambda j, k: (k, 0)) if a_shared else (lambda j, k: (k, j))),
                  pl.BlockSpec((tk, b_cols), (lambda j, k: (k, 0)) if b_shared else (lambda j, k: (k, j)))],
        out_specs=pl.BlockSpec((1, a_cols, b_cols), lambda j, k: (j, 0, 0)),
        out_shape=jax.ShapeDtypeStruct((N_SHARD, a_cols, b_cols), F32),
        compiler_params=_params(2),
    )(a, b)


def _t5_bucket_np(rel):
    nb = N_BUCKETS // 2
    max_exact = nb // 2
    ret = np.where(rel > 0, nb, 0)
    n = np.abs(rel)
    nf = np.maximum(n, 1).astype(np.float32)
    large = max_exact + (np.log(nf / np.float32(max_exact)) / np.float32(math.log(MAX_DISTANCE / max_exact))
                         * np.float32(nb - max_exact)).astype(np.int32)
    large = np.minimum(large, nb - 1)
    return ret + np.where(n < max_exact, n, large)


_VARIANT_OFFSETS = (-HALF_WIN,) * 3


def _band_index():
    kk = np.arange(K_WIN)[None, :]
    ql = np.arange(Q_BLK)[:, None]
    rel = np.stack([kk - ql + off for off in _VARIANT_OFFSETS])
    band = np.abs(rel) <= HALF_WIN
    inside = np.stack([np.broadcast_to(kk >= HALF_WIN, band[0].shape), np.ones_like(band[0]),
                       np.broadcast_to(kk < K_WIN - HALF_WIN, band[0].shape)])
    return rel, band & inside


def _bucket_tables(dil):
    rel, valid = _band_index()
    bucket = _t5_bucket_np(np.clip(rel, -HALF_WIN, HALF_WIN) * dil)
    return np.where(valid, bucket, -1).astype(np.int32)


def _bias_mats(rel_bias, rider=None):
    tables = [_bucket_tables(d) for d in DILATIONS]
    used = [sorted(set(t[t >= 0].tolist())) for t in tables]

    def one_pattern(rb_ref, t_ref, o_ref, buckets):
        bk = t_ref[1]
        for h in range(N_HEADS):
            acc = jnp.full((Q_BLK, K_WIN), NEG_INF, F32)
            for b in buckets:
                acc = jnp.where(bk == b, rb_ref[b, h], acc)
            o_ref[1, h] = acc
            for var in (0, 2):
                o_ref[var, h] = jnp.where(t_ref[var] >= 0, acc, NEG_INF)

    def body(rb_ref, t1, t2, t3, o1, o2, o3):
        for i, (t_ref, o_ref) in enumerate(((t1, o1), (t2, o2), (t3, o3))):
            pl.when(pl.program_id(0) == i)(functools.partial(one_pattern, rb_ref, t_ref, o_ref, used[i]))

    shp = jax.ShapeDtypeStruct((3, N_HEADS, Q_BLK, K_WIN), F32)
    return _call(
        body, "bias_tables", (len(DILATIONS),), [pl.BlockSpec(memory_space=pltpu.SMEM)] + [_whole_vmem()] * 3,
        [_whole_vmem()] * 3, [shp] * 3, [], (rel_bias, *[jnp.asarray(t) for t in tables]), rider)


def _variant(qb, nq):
    return jnp.where(qb == 0, 0, jnp.where(qb == nq - 1, 2, 1))


def _win_start(qb):
    return pl.multiple_of(qb * Q_BLK, Q_BLK)


def _fill_padded(src_ref, pad_ref):
    L = src_ref.shape[0]
    edge = jnp.zeros((HALF_WIN, 128), pad_ref.dtype)
    pad_ref[0:HALF_WIN, :] = edge
    pad_ref[HALF_WIN:HALF_WIN + L, :] = src_ref[...]
    pad_ref[HALF_WIN + L:2 * HALF_WIN + L, :] = edge


INNER = {1: 1, 4: 1, 16: 4}


def _attn_layout(dil, L):
    inner = INNER[dil]
    n_outer = dil // inner
    nsub = min(ATTN_SUB // inner, L // Q_BLK)
    qt = nsub * Q_BLK
    grid = (4, n_outer, L // qt)
    qspec = pl.BlockSpec((inner, None, qt, 128), lambda hp, r, s: (0, r, s, hp))
    kspec = pl.BlockSpec((inner, None, L, 128), lambda hp, r, s: (0, r, 0, hp))
    bspec = pl.BlockSpec((3, 2, Q_BLK, K_WIN), lambda hp, r, s: (0, hp, 0, 0))
    kfspec = pl.BlockSpec((None, inner * L, 128), lambda hp, r, s: (r, 0, hp))
    qfspec = kfspec if inner > 1 else pl.BlockSpec((None, qt, 128), lambda hp, r, s: (r, s, hp))
    fshape = jax.ShapeDtypeStruct((n_outer, inner * L, D_ATTN), F32)
    view = lambda t: t.reshape(inner, n_outer, L, D_ATTN)

    def qrows(m, sub):
        if inner == 1:
            return (slice(sub * Q_BLK, (sub + 1) * Q_BLK), slice(None))
        first = (pl.program_id(2) * nsub + sub) * Q_BLK
        return (pl.ds(m + inner * first, Q_BLK, stride=inner), slice(None))

    def krows(m):
        if inner == 1:
            return (slice(None), slice(None))
        return (pl.ds(m, L, stride=inner), slice(None))

    return inner, nsub, grid, qspec, kspec, bspec, qfspec, kfspec, fshape, view, qrows, krows


def _head_masks():
    lane = lax.broadcasted_iota(jnp.int32, (Q_BLK, 128), 1)
    return lane < HEAD_DIM


def _attn_fwd(q, k, v, bias):
    dil, L, _ = q.shape
    nq = L // Q_BLK
    inner, nsub, grid, qspec, kspec, bspec, qfspec, kfspec, fshape, view, qrows, krows = _attn_layout(dil, L)

    def body(q_ref, k_ref, v_ref, b_ref, o_ref, l_ref, kp, vp):
        step = pl.program_id(2)

        @pl.when(step == 0)
        def _():
            for m in range(inner):
                _fill_padded(k_ref.at[m], kp.at[m])
                _fill_padded(v_ref.at[m], vp.at[m])

        h0 = _head_masks()
        for m, sub in [(m, sub) for m in range(inner) for sub in range(nsub)]:
            qb = step * nsub + sub
            st = _win_start(qb)
            var = _variant(qb, nq)
            kw = kp[m, pl.ds(st, K_WIN), :]
            vw = vp[m, pl.ds(st, K_WIN), :]
            qs = q_ref[m, sub * Q_BLK:(sub + 1) * Q_BLK, :] * ATTN_SCALE
            zq = jnp.zeros_like(qs)
            q2 = jnp.concatenate([jnp.where(h0, qs, zq), jnp.where(h0, zq, qs)], axis=0)
            s = _dot_nt(q2, kw) + b_ref[var].reshape(2 * Q_BLK, K_WIN)
            top = jnp.max(s, axis=-1, keepdims=True)
            p = jnp.exp(s - top)
            l = jnp.sum(p, axis=-1, keepdims=True)
            out = _dot(p.astype(BF16), vw) / l
            lse = top + jnp.log(l)
            o_ref[qrows(m, sub)] = jnp.where(h0, out[0:Q_BLK], out[Q_BLK:2 * Q_BLK])
            l_ref[qrows(m, sub)] = jnp.where(h0, lse[0:Q_BLK], lse[Q_BLK:2 * Q_BLK])

    return pl.pallas_call(
        body, grid=grid, name=f"attn_fwd_d{dil}",
        in_specs=[qspec, kspec, kspec, bspec], out_specs=[qfspec, qfspec], out_shape=[fshape, fshape],
        scratch_shapes=[pltpu.VMEM((inner, L + 2 * HALF_WIN, 128), BF16)] * 2,
        compiler_params=_params(3),
    )(view(q), view(k), view(v), bias)


def _attn_bwd(q, k, v, bias, do, o, lse, rider=None):
    dil, L, _ = q.shape
    nq = L // Q_BLK
    inner, nsub, grid, qspec, kspec, bspec, qfspec, kfspec, fshape, view, qrows, krows = _attn_layout(dil, L)
    nstep = grid[2]

    def body(q_ref, k_ref, v_ref, b_ref, do_ref, o_ref, l_ref, dq_ref, dk_ref, dv_ref, db_ref, db_s,
             kp, vp, dkp, dvp, carry):
        hp, step = pl.program_id(0), pl.program_id(2)
        first = (hp == 0) & (pl.program_id(1) == 0) & (step == 0)
        last = (hp == grid[0] - 1) & (pl.program_id(1) == grid[1] - 1) & (step == nstep - 1)

        @pl.when(first)
        def _():
            db_s[...] = jnp.zeros_like(db_s)

        @pl.when(step == 0)
        def _():
            for m in range(inner):
                _fill_padded(k_ref.at[m], kp.at[m])
                _fill_padded(v_ref.at[m], vp.at[m])
            carry[...] = jnp.zeros_like(carry)

        h0 = _head_masks()
        for m, sub in [(m, sub) for m in range(inner) for sub in range(nsub)]:
            if sub == 0:
                carry_k, carry_v = carry[m, 0], carry[m, 1]
            qb = step * nsub + sub
            st = _win_start(qb)
            var = _variant(qb, nq)
            kw = kp[m, pl.ds(st, K_WIN), :]
            vw = vp[m, pl.ds(st, K_WIN), :]
            qs = q_ref[m, sub * Q_BLK:(sub + 1) * Q_BLK, :] * ATTN_SCALE
            dof = do_ref[qrows(m, sub)]
            dob = dof.astype(BF16)
            prod = dof * o_ref[qrows(m, sub)]
            lsev = l_ref[qrows(m, sub)]
            zq, zd = jnp.zeros_like(qs), jnp.zeros_like(dob)
            q2 = jnp.concatenate([jnp.where(h0, qs, zq), jnp.where(h0, zq, qs)], axis=0)
            do2 = jnp.concatenate([jnp.where(h0, dob, zd), jnp.where(h0, zd, dob)], axis=0)
            lse2 = jnp.concatenate([lsev[:, 0:1], lsev[:, HEAD_DIM:HEAD_DIM + 1]], axis=0)
            dd2 = jnp.concatenate([jnp.sum(jnp.where(h0, prod, 0.0), axis=-1, keepdims=True),
                                   jnp.sum(jnp.where(h0, 0.0, prod), axis=-1, keepdims=True)], axis=0)
            s = _dot_nt(q2, kw) + b_ref[var].reshape(2 * Q_BLK, K_WIN)
            p = jnp.exp(s - lse2)
            ds = p * (_dot_nt(do2, vw) - dd2)
            db_s[pl.ds(hp * 2, 2)] += ds.reshape(2, Q_BLK, K_WIN)
            dsb = ds.astype(BF16)
            dv_acc = _dot_tn(p.astype(BF16), do2)
            dk_acc = _dot_tn(dsb, q2)
            dq2 = _dot(dsb, kw) * ATTN_SCALE
            dq_ref[qrows(m, sub)] = jnp.where(h0, dq2[0:Q_BLK], dq2[Q_BLK:2 * Q_BLK]).astype(dq_ref.dtype)
            dkp[m, pl.ds(st, Q_BLK), :] = carry_k + dk_acc[0:Q_BLK]
            dvp[m, pl.ds(st, Q_BLK), :] = carry_v + dv_acc[0:Q_BLK]
            carry_k, carry_v = dk_acc[Q_BLK:K_WIN], dv_acc[Q_BLK:K_WIN]
            if sub == nsub - 1:
                carry[m, 0] = carry_k
                carry[m, 1] = carry_v

        @pl.when(step == nstep - 1)
        def _():
            for m in range(inner):
                dkp[m, L:L + Q_BLK, :] = carry[m, 0]
                dvp[m, L:L + Q_BLK, :] = carry[m, 1]
                dk_ref[krows(m)] = dkp[m, HALF_WIN:HALF_WIN + L, :].astype(dk_ref.dtype)
                dv_ref[krows(m)] = dvp[m, HALF_WIN:HALF_WIN + L, :].astype(dv_ref.dtype)

        @pl.when(last)
        def _():
            db_ref[...] = db_s[...]

    dbshape = (N_HEADS, Q_BLK, K_WIN)
    gshape = jax.ShapeDtypeStruct(fshape.shape, BF16 if inner == 1 else F32)
    return _call(
        body, f"attn_bwd_d{dil}", grid,
        [qspec, kspec, kspec, bspec, qfspec, qfspec, qfspec],
        [qfspec, kfspec, kfspec, _whole_vmem()],
        [gshape, gshape, gshape, jax.ShapeDtypeStruct(dbshape, F32)],
        [pltpu.VMEM(dbshape, F32)] + [pltpu.VMEM((inner, L + 2 * HALF_WIN, 128), BF16)] * 2
        + [pltpu.VMEM((inner, L + 2 * HALF_WIN, 128), F32)] * 2 + [pltpu.VMEM((inner, 2, Q_BLK, 128), F32)],
        (view(q), view(k), view(v), bias, do, o, lse), rider)


def _bucket_onehots(dil):
    m = np.zeros((3, K_WIN, N_BUCKETS), np.float32)
    for var, off in enumerate(_VARIANT_OFFSETS):
        for rel in range(-HALF_WIN, HALF_WIN + 1):
            col = (rel - off + Q_BLK - 1) % K_WIN
            m[var, col, int(_t5_bucket_np(np.asarray(rel * dil)))] = 1.0
    return jnp.asarray(m)


def _bias_grad(dbs):
    onehots = [_bucket_onehots(d) for d in DILATIONS]
    flip = jnp.asarray(np.eye(Q_BLK, dtype=np.float32)[::-1].copy())

    def body(d1, d2, d3, m1, m2, m3, flip_ref, out_ref):
        hp = lax.Precision.HIGHEST
        acc = jnp.zeros((N_HEADS, N_BUCKETS), F32)
        for d_ref, m_ref in ((d1, m1), (d2, m2), (d3, m3)):
            rows = []
            for h in range(N_HEADS):
                xrev = jnp.dot(flip_ref[...], d_ref[h], precision=hp, preferred_element_type=F32)
                y = pltpu.roll(xrev, 0, 1, stride=1, stride_axis=0)
                rows.append(jnp.sum(y, axis=0, keepdims=True))
            acc = acc + jnp.dot(jnp.concatenate(rows, axis=0), m_ref[1], precision=hp, preferred_element_type=F32)
        out_ref[...] = acc

    return pl.pallas_call(
        body, name="bias_grad", out_shape=jax.ShapeDtypeStruct((N_HEADS, N_BUCKETS), F32),
        compiler_params=_params(0),
    )(*dbs, *onehots, flip)


def _block_diag(w):
    eye = jnp.eye(N_RNN_BLOCKS, dtype=w.dtype)
    return jnp.einsum("ncd,nm->ncmd", w, eye).reshape(D_RNN, D_RNN).astype(BF16)


def _diag_blocks(dense):
    d = dense.reshape(N_RNN_BLOCKS, RNN_BLOCK, N_RNN_BLOCKS, RNN_BLOCK)
    return jnp.stack([d[n, :, n, :] for n in range(N_RNN_BLOCKS)])


EARLY = ("w_out", "w_up", "w_down")


def _local_step(x, target, p, shards=None):
    p = dict(p)
    first = None if shards is None else _gather_rider(["w_in"], [shards["w_in"]], shards["conv_w"])
    biases, got = _bias_mats(p["rel_bias"], first)
    if shards is not None:
        p["w_in"] = got[0]
        p["conv_w"] = jnp.transpose(got[1], (1, 0, 2)).reshape(4, D_RNN)
    lru = {}
    for dname in ("fwd", "bwd"):
        lru[dname] = (_block_diag(p["lru_wa_" + dname]), p["lru_ba_" + dname], _block_diag(p["lru_wx_" + dname]),
                      p["lru_bx_" + dname], p["lru_lam_" + dname])

    def gather(name):
        return None if shards is None else _gather_rider([name], [shards[name]])

    (xr, gate, *qkv), got = _inproj_fwd(x, p["attn_norm_g"], p["w_in"], gather("w_out"))
    p.update(zip(["w_out"], got))
    qs, ks, vs = qkv[0:3], qkv[3:6], qkv[6:9]
    (hf, xc), got = _rnn_fwd(xr, p["conv_w"], p["conv_b"], *lru["fwd"], reverse=False, rider=gather("w_up"))
    p.update(zip(["w_up"], got))
    (hb,), got = _rnn_fwd(xr, p["conv_w"], p["conv_b"], *lru["bwd"], reverse=True, rider=gather("w_down"), xc=xc)
    p.update(zip(["w_down"], got))
    outs, lses = [], []
    for q, k, v, bias in zip(qs, ks, vs, biases):
        o, l = _attn_fwd(q, k, v, bias)
        outs.append(o)
        lses.append(l)
    x1, mixb, *yl = _mix_fwd(outs, lses, hf, hb, gate, x, p["norm_rnn_g"], p["norm_attn_g"], p["w_out"])
    yas, lsts = [yl[0], yl[1], yl[1]], [yl[2], yl[3], yl[3]]
    dx1, h2b, a2b, dub, dx2b, loss_vec, dg_fin, dg_mlp = _mlp_fwd_bwd(
        x1, target, p["mlp_norm_g"], p["final_norm_g"], p["w_up"], p["w_down"])
    dhs, dgate, dya1, dya4, dw_out, dg_rnn, dg_attn = _mix_bwd(dx1, p["w_out"], mixb, yas[0], hf, hb, gate,
                                                               p["norm_rnn_g"], p["norm_attn_g"])
    dyas = [dya1, dya4, dya4]
    dw_up = _dw_matmul(h2b, dub, D_MODEL, FF_BLK, "dw_up")
    dw_down = _dw_matmul(a2b, dx2b, FF_BLK, D_MODEL, "dw_down")
    early = [dw_out, dw_up, dw_down]
    dqs, dks, dvs, dbs = [], [], [], []
    for i, (q, k, v, bias, dya, ya, lse) in enumerate(zip(qs, ks, vs, biases, dyas, yas, lsts)):
        rider = None
        if shards is not None:
            make = (lambda: _pair_exchange_rider(EARLY, early), lambda: _chip_exchange_rider(early),
                    lambda: _pair_share_rider(EARLY, early))[i]
            rider = make()
        (dq, dk, dv, db), got = _attn_bwd(q, k, v, bias, dya, ya, lse, rider)
        if shards is not None and i == 0:
            core = lax.axis_index("c").reshape(1).astype(jnp.int32)
            early = [_pair_add(core, g, o, "grad_pair_add_" + n) for n, g, o in zip(EARLY, early, got)]
        elif shards is not None and i == 1:
            early = [_chip_sum(t, "grad_chip_sum_" + n) for n, t in zip(EARLY, got)]
        elif shards is not None:
            early = got
        dqs.append(dq)
        dks.append(dk)
        dvs.append(dv)
        dbs.append(db)
    d_rel_bias = _bias_grad(dbs).T
    (dxc_f, dwa_f, dwx_f, dvec_f), _ = _rnn_bwd(xc, hf, dhs, *lru["fwd"], reverse=False)
    small = {
        "lru_wa_fwd": _diag_blocks(dwa_f), "lru_ba_fwd": dvec_f[0:1], "lru_wx_fwd": _diag_blocks(dwx_f),
        "lru_bx_fwd": dvec_f[1:2], "lru_lam_fwd": dvec_f[3:4],
        "rel_bias": d_rel_bias, "norm_rnn_g": dg_rnn, "norm_attn_g": dg_attn,
        "mlp_norm_g": dg_mlp, "final_norm_g": dg_fin,
    }
    loss_local = (0.5 / D_MODEL) * jnp.sum(loss_vec)
    rider = None
    if shards is not None:
        rider = _small_gather_rider(_pack([small[n].reshape(shp) for n, shp in SMALL if n in small]
                                          + [loss_local.reshape(1)]))
    (dxc_b, dwa_b, dwx_b, dvec_b), gathered = _rnn_bwd(xc, hb, dhs, *lru["bwd"], reverse=True, rider=rider)
    grad_x, dw_in, dg1, dconv = _inproj_bwd(x, dx1, xr, dxc_f, dxc_b, dgate, dqs, dks, dvs,
                                            p["attn_norm_g"], p["conv_w"], p["w_in"])[0]
    last = {"lru_wa_bwd": _diag_blocks(dwa_b), "lru_ba_bwd": dvec_b[0:1], "lru_wx_bwd": _diag_blocks(dwx_b),
            "lru_bx_bwd": dvec_b[1:2], "lru_lam_bwd": dvec_b[3:4],
            "attn_norm_g": dg1, "conv_w": dconv[0:4], "conv_b": dconv[4:5]}
    if shards is None:
        big = {"w_in": dw_in, "w_out": dw_out, "w_up": dw_up, "w_down": dw_down}
        return loss_local, grad_x, {**small, **last}, None, big, {}
    return loss_local, grad_x, last, gathered[0], {"w_in": dw_in}, dict(zip(EARLY, early))


BIG = ("w_in", "w_out", "w_up", "w_down")
BIG_SHARD = {"w_in": (D_MODEL, IN_BLK), "w_out": (OUT_BLK, D_MODEL), "w_up": (D_MODEL, FF_BLK), "w_down": (FF_BLK, D_MODEL)}
N_BIG = len(BIG)
N_CHIP_PEERS = 3
ANY = pl.BlockSpec(memory_space=pl.ANY)


def _place():
    x, y, c = lax.axis_index("x"), lax.axis_index("y"), lax.axis_index("c")
    chips = [(1 - x, y), (x, 1 - y), (1 - x, 1 - y)]
    return x, y, c, chips


def _remote(src, dst, send_sem, recv_sem, dev):
    return pltpu.make_async_remote_copy(src_ref=src, dst_ref=dst, send_sem=send_sem, recv_sem=recv_sem,
                                        device_id=dev, device_id_type=MESH)


def _staged_start(srcs, bufs, sems):
    legs = [pltpu.make_async_copy(s, b, sems.at[i]) for i, (s, b) in enumerate(zip(srcs, bufs))]
    for cp in legs:
        cp.start()
    return legs


def _staged_finish(legs, bufs, dsts, sems):
    out = []
    for i, (leg, b, d) in enumerate(zip(legs, bufs, dsts)):
        leg.wait()
        cp = pltpu.make_async_copy(b, d, sems.at[i])
        cp.start()
        out.append(cp)
    return out


class _Rider:
    def __init__(self, inputs, out_shape, scratch, first, late, last):
        self.inputs, self.out_shape, self.scratch = list(inputs), list(out_shape), list(scratch)
        self.first, self.late, self.last = first, late, last


def _call(body, name, grid, in_specs, out_specs, out_shape, scratch, operands, rider=None):
    n_grid = len(grid)
    if rider is None:
        res = pl.pallas_call(body, grid=grid, name=name, in_specs=in_specs, out_specs=out_specs, out_shape=out_shape,
                             scratch_shapes=scratch, compiler_params=_params(n_grid))(*operands)
        return list(res), []
    n_in, n_out, n_scr = len(in_specs), len(out_specs), len(scratch)
    ri, ro = len(rider.inputs), len(rider.out_shape)
    nsteps = int(np.prod(grid))
    late_step = max(nsteps - 3, 1)

    def wrapped(*refs):
        a, b = n_in, n_in + ri
        c, d = b + n_out, b + n_out + ro
        e = d + n_scr
        mine = refs[:a] + refs[b:c] + refs[d:e]
        theirs = (refs[a:b], refs[c:d], refs[e:])
        step = pl.program_id(0)
        for ax in range(1, n_grid):
            step = step * grid[ax] + pl.program_id(ax)
        pl.when(step == 0)(lambda: rider.first(*theirs))
        pl.when(step == late_step)(lambda: rider.late(*theirs))
        body(*mine)
        pl.when(step == nsteps - 1)(lambda: rider.last(*theirs))

    res = pl.pallas_call(
        wrapped, grid=grid, name=name, in_specs=list(in_specs) + [ANY] * ri, out_specs=list(out_specs) + [ANY] * ro,
        out_shape=list(out_shape) + rider.out_shape, scratch_shapes=list(scratch) + rider.scratch,
        compiler_params=_params(n_grid),
    )(*operands, *rider.inputs)
    return list(res[:n_out]), list(res[n_out:])


def _run_rider(rider, name):
    ri, ro = len(rider.inputs), len(rider.out_shape)

    def body(*refs):
        parts = (refs[:ri], refs[ri:ri + ro], refs[ri + ro:])
        rider.first(*parts)
        rider.late(*parts)
        rider.last(*parts)

    return list(pl.pallas_call(
        body, name=name, in_specs=[ANY] * ri, out_specs=[ANY] * ro, out_shape=rider.out_shape, scratch_shapes=rider.scratch,
        compiler_params=pltpu.CompilerParams(has_side_effects=True, vmem_limit_bytes=VMEM_LIMIT),
    )(*rider.inputs))


def _nothing(ins, outs, scr):
    return None


def _gather_rider(names, shards, conv_w=None):
    n = len(names)
    items = n + (conv_w is not None)
    halves = [BIG_SHARD[nm][0] // 2 for nm in names]

    def parts(ins, outs, scr):
        x, y, c, chips = _place()
        return x, y, c, chips, 2 * x + y, (x, y, 1 - c), scr[:8], scr[8:]

    def piece(outs, w, chip, core_half):
        return outs[w].at[chip, pl.ds(core_half * halves[w], halves[w])]

    def ici(ins, outs, sems, w, k, chip_xy, c, me):
        return _remote(ins[w].at[pl.ds(c * halves[w], halves[w])], piece(outs, w, me, c),
                       sems[0].at[w, k], sems[1].at[w, k], (*chip_xy, c))

    def first(ins, outs, scr):
        x, y, c, chips, me, sibling, sems, bufs = parts(ins, outs, scr)
        legs = _staged_start(ins, bufs, sems[6])
        for w in range(n):
            for k, chip_xy in enumerate(chips):
                ici(ins, outs, sems, w, k, chip_xy, c, me).start()
        if conv_w is not None:
            for k, (px, py) in enumerate(chips):
                _remote(ins[n], outs[n].at[me], sems[4].at[k], sems[5].at[k], (px, py, c)).start()
        _staged_finish(legs, bufs, [o.at[me] for o in outs], sems[7])

    def late(ins, outs, scr):
        x, y, c, chips, me, sibling, sems, bufs = parts(ins, outs, scr)
        for w in range(n):
            for k, (px, py) in enumerate(chips):
                landed = piece(outs, w, 2 * px + py, c)
                _remote(landed, landed, sems[0].at[w, k], sems[1].at[w, k], (px, py, c)).wait_recv()
                _remote(landed, landed, sems[2].at[w, k], sems[3].at[w, k], sibling).start()

    def last(ins, outs, scr):
        x, y, c, chips, me, sibling, sems, bufs = parts(ins, outs, scr)
        for w in range(n):
            for k, (px, py) in enumerate(chips):
                other = piece(outs, w, 2 * px + py, 1 - c)
                _remote(other, other, sems[2].at[w, k], sems[3].at[w, k], sibling).wait_recv()
        if conv_w is not None:
            for k, (px, py) in enumerate(chips):
                got = outs[n].at[2 * px + py]
                _remote(got, got, sems[4].at[k], sems[5].at[k], (px, py, c)).wait_recv()
                _remote(ins[n], outs[n].at[me], sems[4].at[k], sems[5].at[k], (px, py, c)).wait_send()
        for i in range(items):
            pltpu.make_async_copy(bufs[i], outs[i].at[me], sems[7].at[i]).wait()
        for w in range(n):
            for k, (px, py) in enumerate(chips):
                ici(ins, outs, sems, w, k, (px, py), c, me).wait_send()
                landed = piece(outs, w, 2 * px + py, c)
                _remote(landed, landed, sems[2].at[w, k], sems[3].at[w, k], sibling).wait_send()

    out_shape = [jax.ShapeDtypeStruct((N_SHARD,) + BIG_SHARD[nm], BF16) for nm in names]
    stage = [pltpu.VMEM(BIG_SHARD[nm], BF16) for nm in names]
    inputs = list(shards)
    if conv_w is not None:
        out_shape.append(jax.ShapeDtypeStruct((N_SHARD,) + conv_w.shape, F32))
        stage.append(pltpu.VMEM(conv_w.shape, F32))
        inputs.append(conv_w)
    scratch = ([pltpu.SemaphoreType.DMA((n, N_CHIP_PEERS))] * 4 + [pltpu.SemaphoreType.DMA((N_CHIP_PEERS,))] * 2
               + [pltpu.SemaphoreType.DMA((items,))] * 2 + stage)
    return _Rider(inputs, out_shape, scratch, first, late, last)


def _pair_exchange_rider(names, grads):
    def copies(ins, outs, scr):
        x, y, c, _ = _place()
        out = []
        for w, nm in enumerate(names):
            h = BIG_SHARD[nm][0] // 2
            out.append(_remote(ins[w].at[:, pl.ds((1 - c) * h, h), :], outs[w], scr[0].at[w], scr[1].at[w], (x, y, 1 - c)))
        return out

    def first(ins, outs, scr):
        for cp in copies(ins, outs, scr):
            cp.start()

    def last(ins, outs, scr):
        for cp in copies(ins, outs, scr):
            cp.wait()

    out_shape = [jax.ShapeDtypeStruct((N_SHARD, BIG_SHARD[nm][0] // 2, BIG_SHARD[nm][1]), F32) for nm in names]
    return _Rider(grads, out_shape, [pltpu.SemaphoreType.DMA((len(names),))] * 2, first, _nothing, last)


def _pair_add(core, grad, other, name):
    _, r, cols = grad.shape
    h = r // 2
    th = min(h, 256)
    per = h // th

    def body(c_ref, g_ref, o_ref, out_ref):
        out_ref[...] = (g_ref[...] + o_ref[...]).astype(BF16)

    return pl.pallas_call(
        body, name=name,
        grid_spec=pltpu.PrefetchScalarGridSpec(
            num_scalar_prefetch=1, grid=(N_SHARD, per),
            in_specs=[pl.BlockSpec((1, th, cols), lambda j, i, c_ref: (j, c_ref[0] * per + i, 0)),
                      pl.BlockSpec((1, th, cols), lambda j, i, c_ref: (j, i, 0))],
            out_specs=pl.BlockSpec((1, th, cols), lambda j, i, c_ref: (j, i, 0))),
        out_shape=jax.ShapeDtypeStruct((N_SHARD, h, cols), BF16),
        compiler_params=_params(2),
    )(core, grad, other)


def _chip_exchange_rider(parts):
    n = len(parts)

    def sends(ins, outs, scr):
        x, y, c, chips = _place()
        me = 2 * x + y
        return [_remote(ins[w].at[2 * px + py], outs[w].at[me], scr[0].at[w, k], scr[1].at[w, k], (px, py, c))
                for w in range(n) for k, (px, py) in enumerate(chips)]

    def first(ins, outs, scr):
        x, y, c, chips = _place()
        me = 2 * x + y
        legs = _staged_start([r.at[me] for r in ins], scr[4:], scr[2])
        for cp in sends(ins, outs, scr):
            cp.start()
        _staged_finish(legs, scr[4:], [o.at[me] for o in outs], scr[3])

    def last(ins, outs, scr):
        x, y, c, chips = _place()
        me = 2 * x + y
        for w in range(n):
            for k, (px, py) in enumerate(chips):
                got = outs[w].at[2 * px + py]
                _remote(got, got, scr[0].at[w, k], scr[1].at[w, k], (px, py, c)).wait_recv()
        for cp in sends(ins, outs, scr):
            cp.wait_send()
        for w in range(n):
            pltpu.make_async_copy(scr[4 + w], outs[w].at[me], scr[3].at[w]).wait()

    out_shape = [jax.ShapeDtypeStruct(p.shape, BF16) for p in parts]
    scratch = ([pltpu.SemaphoreType.DMA((n, N_CHIP_PEERS))] * 2 + [pltpu.SemaphoreType.DMA((n,))] * 2
               + [pltpu.VMEM(p.shape[1:], BF16) for p in parts])
    return _Rider(parts, out_shape, scratch, first, _nothing, last)


def _chip_sum(parts, name):
    _, h, cols = parts.shape
    th = min(h, 256)

    def body(p_ref, out_ref):
        acc = p_ref[0].astype(F32)
        for j in range(1, N_SHARD):
            acc = acc + p_ref[j].astype(F32)
        out_ref[...] = acc

    return pl.pallas_call(
        body, name=name, grid=(h // th,),
        in_specs=[pl.BlockSpec((N_SHARD, th, cols), lambda i: (0, i, 0))],
        out_specs=pl.BlockSpec((th, cols), lambda i: (i, 0)),
        out_shape=jax.ShapeDtypeStruct((h, cols), F32),
        compiler_params=_params(),
    )(parts)


def _pair_share_rider(names, halves):
    n = len(names)
    hs = [BIG_SHARD[nm][0] // 2 for nm in names]

    def mine(outs, c):
        return [outs[w].at[pl.ds(c * hs[w], hs[w])] for w in range(n)]

    def first(ins, outs, scr):
        x, y, c, _ = _place()
        legs = _staged_start(ins, scr[4:], scr[2])
        for w, dst in enumerate(mine(outs, c)):
            _remote(ins[w], dst, scr[0].at[w], scr[1].at[w], (x, y, 1 - c)).start()
        _staged_finish(legs, scr[4:], mine(outs, c), scr[3])

    def last(ins, outs, scr):
        x, y, c, _ = _place()
        for w, (theirs, dst) in enumerate(zip(mine(outs, 1 - c), mine(outs, c))):
            _remote(theirs, theirs, scr[0].at[w], scr[1].at[w], (x, y, 1 - c)).wait_recv()
            _remote(ins[w], dst, scr[0].at[w], scr[1].at[w], (x, y, 1 - c)).wait_send()
            pltpu.make_async_copy(scr[4 + w], dst, scr[3].at[w]).wait()

    out_shape = [jax.ShapeDtypeStruct(BIG_SHARD[nm], F32) for nm in names]
    scratch = [pltpu.SemaphoreType.DMA((n,))] * 4 + [pltpu.VMEM((h, BIG_SHARD[nm][1]), F32) for nm, h in zip(names, hs)]
    return _Rider(halves, out_shape, scratch, first, _nothing, last)


N_DEV = 8


def _all_peers(x, y, c):
    return [((1 - x) if fx else x, (1 - y) if fy else y, (1 - c) if fc else c)
            for fx in (0, 1) for fy in (0, 1) for fc in (0, 1) if fx or fy or fc]


def _small_gather_rider(vec):
    def sends(ins, outs, scr):
        x, y, c, _ = _place()
        me = 4 * x + 2 * y + c
        return [_remote(ins[0], outs[0].at[me], scr[0].at[k], scr[1].at[k], dev) for k, dev in enumerate(_all_peers(x, y, c))]

    def first(ins, outs, scr):
        x, y, c, _ = _place()
        legs = _staged_start(ins, scr[4:], scr[2])
        for cp in sends(ins, outs, scr):
            cp.start()
        _staged_finish(legs, scr[4:], [outs[0].at[4 * x + 2 * y + c]], scr[3])

    def last(ins, outs, scr):
        x, y, c, _ = _place()
        for k, (px, py, pc) in enumerate(_all_peers(x, y, c)):
            got = outs[0].at[4 * px + 2 * py + pc]
            _remote(got, got, scr[0].at[k], scr[1].at[k], (px, py, pc)).wait_recv()
        for cp in sends(ins, outs, scr):
            cp.wait_send()
        pltpu.make_async_copy(scr[4], outs[0].at[4 * x + 2 * y + c], scr[3].at[0]).wait()

    scratch = ([pltpu.SemaphoreType.DMA((N_DEV - 1,))] * 2 + [pltpu.SemaphoreType.DMA((1,))] * 2
               + [pltpu.VMEM(vec.shape, F32)])
    return _Rider([vec], [jax.ShapeDtypeStruct((N_DEV,) + vec.shape, F32)], scratch, first, _nothing, last)


def _sum_devices(gathered):
    def body(g_ref, out_ref):
        acc = g_ref[0]
        for j in range(1, N_DEV):
            acc = acc + g_ref[j]
        out_ref[...] = acc

    return pl.pallas_call(body, name="sum_devices", out_shape=jax.ShapeDtypeStruct(gathered.shape[1:], F32),
                          compiler_params=_params(0))(gathered)


def _allreduce_small(vec):
    rows = vec.shape[0]

    def body(v_ref, sum_ref, gat_ref, send, recv, loc_sem):
        x, y, c, chips = _place()
        sibling = (x, y, 1 - c)
        slot = lambda px, py, pc: gat_ref.at[4 * px + 2 * py + pc]
        lc = pltpu.make_async_copy(v_ref, slot(x, y, c), loc_sem)
        lc.start()
        sends = [_remote(v_ref, slot(x, y, c), send.at[0], recv.at[0], sibling)]
        sends += [_remote(v_ref, slot(x, y, c), send.at[1 + k], recv.at[1 + k], (px, py, c))
                  for k, (px, py) in enumerate(chips)]
        for cp in sends:
            cp.start()
        for k, (px, py) in enumerate(chips):
            got = slot(px, py, c)
            _remote(got, got, send.at[1 + k], recv.at[1 + k], (px, py, c)).wait_recv()
            cp = _remote(got, got, send.at[4 + k], recv.at[4 + k], sibling)
            cp.start()
            sends.append(cp)
        got = slot(x, y, 1 - c)
        _remote(got, got, send.at[0], recv.at[0], sibling).wait_recv()
        for k, (px, py) in enumerate(chips):
            got = slot(px, py, 1 - c)
            _remote(got, got, send.at[4 + k], recv.at[4 + k], sibling).wait_recv()
        for cp in sends:
            cp.wait_send()
        lc.wait()
        acc = gat_ref[0]
        for j in range(1, N_DEV):
            acc = acc + gat_ref[j]
        sum_ref[...] = acc

    total, _ = pl.pallas_call(
        body, name="allreduce_small",
        in_specs=[_whole_vmem()], out_specs=[_whole_vmem(), _whole_vmem()],
        out_shape=[jax.ShapeDtypeStruct((rows, 128), F32), jax.ShapeDtypeStruct((N_DEV, rows, 128), F32)],
        scratch_shapes=[pltpu.SemaphoreType.DMA((N_DEV - 1,))] * 2 + [pltpu.SemaphoreType.DMA(())],
        compiler_params=pltpu.CompilerParams(has_side_effects=True, vmem_limit_bytes=VMEM_LIMIT),
    )(vec)
    return total


def _adam_math(w_ref, g_ref, m_ref, v_ref, d_ref, m2_ref, v2_ref):
    c1 = 1.0 - ADAM_B1 ** ADAM_STEP
    c2 = 1.0 - ADAM_B2 ** ADAM_STEP
    gv = g_ref[...]
    m2 = ADAM_B1 * m_ref[...] + (1.0 - ADAM_B1) * gv
    v2 = ADAM_B2 * v_ref[...] + (1.0 - ADAM_B2) * (gv * gv)
    m2_ref[...] = m2
    v2_ref[...] = v2
    d_ref[...] = -ADAM_LR * ((m2 / c1) / (jnp.sqrt(v2 / c2) + ADAM_EPS) + ADAM_WD * w_ref[...])


def _adamw_many(ws, gs, ms, vs):
    n = len(ws)

    def body(*refs):
        for i in range(n):
            _adam_math(*[refs[k * n + i] for k in range(7)])

    shapes = [jax.ShapeDtypeStruct(w.shape, F32) for w in ws]
    res = pl.pallas_call(body, name="adamw_small", out_shape=shapes * 3, compiler_params=_params(0))(*ws, *gs, *ms, *vs)
    return res[:n], res[n:2 * n], res[2 * n:]


def _adamw(w, g, m, v, name):
    rows, cols = w.shape
    tr = 256 if rows % 256 == 0 else rows

    def body(w_ref, g_ref, m_ref, v_ref, d_ref, m2_ref, v2_ref):
        _adam_math(w_ref, g_ref, m_ref, v_ref, d_ref, m2_ref, v2_ref)

    spec = pl.BlockSpec((tr, cols), lambda i: (i, 0))
    shp = jax.ShapeDtypeStruct((rows, cols), F32)
    return pl.pallas_call(
        body, name=name, grid=(rows // tr,), in_specs=[spec] * 4, out_specs=[spec] * 3, out_shape=[shp] * 3,
        compiler_params=_params(),
    )(w, g, m, v)


SMALL = (
    ("attn_norm_g", (1, 1024)), ("conv_w", (1, 4, 512)), ("conv_b", (1, 512)),
    ("lru_wa_fwd", (1, 8, 64, 64)), ("lru_ba_fwd", (1, 512)), ("lru_wx_fwd", (1, 8, 64, 64)), ("lru_bx_fwd", (1, 512)),
    ("lru_lam_fwd", (1, 512)),
    ("lru_wa_bwd", (1, 8, 64, 64)), ("lru_ba_bwd", (1, 512)), ("lru_wx_bwd", (1, 8, 64, 64)), ("lru_bx_bwd", (1, 512)),
    ("lru_lam_bwd", (1, 512)),
    ("rel_bias", (32, 8)), ("norm_rnn_g", (1, 512)), ("norm_attn_g", (1, 512)), ("mlp_norm_g", (1, 1024)),
    ("final_norm_g", (1024,)),
)
PACK_ROW = 8 * 128


def _pack(parts):
    flat = jnp.concatenate([p.reshape(-1) for p in parts])
    pad = (-flat.shape[0]) % PACK_ROW
    return jnp.pad(flat, (0, pad)).reshape(-1, 128)


def _unpack(packed, shapes):
    flat = packed.reshape(-1)
    out, off = [], 0
    for shp in shapes:
        n = int(np.prod(shp))
        out.append(flat[off:off + n].reshape(shp))
        off += n
    return out


WEIGHT_ORDER = ("attn_norm_g", "w_in", "conv_w", "conv_b", "lru_wa_fwd", "lru_ba_fwd", "lru_wx_fwd", "lru_bx_fwd",
                "lru_lam_fwd", "lru_wa_bwd", "lru_ba_bwd", "lru_wx_bwd", "lru_bx_bwd", "lru_lam_bwd", "rel_bias",
                "norm_rnn_g", "norm_attn_g", "w_out", "mlp_norm_g", "w_up", "w_down", "final_norm_g")


def kernel(x, attn_norm_g, w_in, conv_w, conv_b, lru_wa_fwd, lru_ba_fwd, lru_wx_fwd, lru_bx_fwd, lru_lam_fwd, lru_wa_bwd, lru_ba_bwd, lru_wx_bwd, lru_bx_bwd, lru_lam_bwd, rel_bias, norm_rnn_g, norm_attn_g, w_out, mlp_norm_g, w_up, w_down, final_norm_g, loss_target, m_attn_norm_g, m_w_in, m_conv_w, m_conv_b, m_lru_wa_fwd, m_lru_ba_fwd, m_lru_wx_fwd, m_lru_bx_fwd, m_lru_lam_fwd, m_lru_wa_bwd, m_lru_ba_bwd, m_lru_wx_bwd, m_lru_bx_bwd, m_lru_lam_bwd, m_rel_bias, m_norm_rnn_g, m_norm_attn_g, m_w_out, m_mlp_norm_g, m_w_up, m_w_down, m_final_norm_g, v_attn_norm_g, v_w_in, v_conv_w, v_conv_b, v_lru_wa_fwd, v_lru_ba_fwd, v_lru_wx_fwd, v_lru_bx_fwd, v_lru_lam_fwd, v_lru_wa_bwd, v_lru_ba_bwd, v_lru_wx_bwd, v_lru_bx_bwd, v_lru_lam_bwd, v_rel_bias, v_norm_rnn_g, v_norm_attn_g, v_w_out, v_mlp_norm_g, v_w_up, v_w_down, v_final_norm_g):
    given = dict(locals())
    w = {n: given[n] for n in WEIGHT_ORDER}
    m = {n: given["m_" + n] for n in WEIGHT_ORDER}
    v = {n: given["v_" + n] for n in WEIGHT_ORDER}

    chip = lax.axis_index("x") * 2 + lax.axis_index("y")
    core = lax.axis_index("c")

    shards = {n: w[n][0].astype(BF16) for n in BIG}
    shards["conv_w"] = w["conv_w"][0]
    p = {n: (t[0] if t.ndim >= 3 else t) for n, t in w.items() if n not in BIG and n != "conv_w"}
    p["final_norm_g"] = w["final_norm_g"].reshape(1, D_MODEL)

    _, grad_x, small, gathered, big, reduced = _local_step(x[0], loss_target[0], p, shards)

    late = tuple(big)
    grads = [big[n] for n in late]
    others = _run_rider(_pair_exchange_rider(late, grads), "grad_pair_exchange")
    core_arr = core.reshape(1).astype(jnp.int32)
    parts = [_pair_add(core_arr, g, o, "grad_pair_add_" + n) for n, g, o in zip(late, grads, others)]
    landed = _run_rider(_chip_exchange_rider(parts), "grad_chip_exchange")
    halves = [_chip_sum(t, "grad_chip_sum_" + n) for n, t in zip(late, landed)]
    reduced.update(zip(late, _run_rider(_pair_share_rider(late, halves), "grad_pair_share")))

    early_small = [(n, shp) for n, shp in SMALL if n not in small]
    late_small = [(n, shp) for n, shp in SMALL if n in small]
    *early_g, loss = _unpack(_sum_devices(gathered), [shp for _, shp in early_small] + [(1,)])
    late_g = _unpack(_allreduce_small(_pack([small[n].reshape(shp) for n, shp in late_small])),
                     [shp for _, shp in late_small])
    g = dict(zip([n for n, _ in early_small + late_small], early_g + late_g))
    g["conv_w"] = lax.dynamic_slice_in_dim(g["conv_w"], chip * (D_RNN // N_SHARD), D_RNN // N_SHARD, axis=2)
    for n in BIG:
        g[n] = reduced[n][None]

    delta, new_m, new_v = {}, {}, {}
    for n in BIG:
        d2, m2, v2 = _adamw(w[n][0], reduced[n], m[n][0], v[n][0], "adamw_" + n)
        delta[n], new_m[n], new_v[n] = d2[None], m2[None], v2[None]
    names = [n for n, _ in SMALL]
    for dst, src in zip((delta, new_m, new_v), _adamw_many(*[[t[n] for n in names] for t in (w, g, m, v)])):
        dst.update(dict(zip(names, src)))

    return (loss.reshape(()), grad_x[None], *[g[n] for n in WEIGHT_ORDER], *[delta[n] for n in WEIGHT_ORDER],
            *[new_m[n] for n in WEIGHT_ORDER], *[new_v[n] for n in WEIGHT_ORDER])
```

```python
import functools
import math

import numpy as np
import jax
import jax.numpy as jnp
from jax import lax
from jax.experimental import pallas as pl
from jax.experimental.pallas import tpu as pltpu

F32 = jnp.float32
BF16 = jnp.bfloat16

D_MODEL = 1024
D_RNN = 512
D_ATTN = 512
N_HEADS = 8
HEAD_DIM = 64
N_RNN_BLOCKS = 8
RNN_BLOCK = 64
D_IN = 2 * D_RNN + 3 * D_ATTN
D_FF = 4 * D_MODEL
N_SHARD = 4
IN_BLK = D_IN // N_SHARD
OUT_BLK = D_MODEL // N_SHARD
FF_BLK = D_FF // N_SHARD
EPS = 1e-6
NEG_INF = -1e30
LRU_C = 8.0
DILATIONS = (1, 4, 16)
F32_LAYOUT = 4
HALF_WIN = 64
Q_BLK = 128
K_WIN = 256
N_BUCKETS = 32
MAX_DISTANCE = 1024
ATTN_SCALE = HEAD_DIM ** -0.5

ADAM_LR = 0.001
ADAM_B1 = 0.9
ADAM_B2 = 0.999
ADAM_EPS = 1e-08
ADAM_WD = 0.01
ADAM_STEP = 10

TS = 512
TS_RNN_BWD = 1024
TS_MLP = 256
TS_INPROJ_BWD = 512
ATTN_SUB = 16
TK_DW = 4096
SCAN_UNROLL = 8
SUB = 8
VMEM_LIMIT = 56 * 1024 * 1024
GELU_C0 = math.sqrt(2.0 / math.pi)
GELU_C1 = 0.044715

MESH = pl.DeviceIdType.MESH


def _params(n_grid=1):
    return pltpu.CompilerParams(vmem_limit_bytes=VMEM_LIMIT, dimension_semantics=("arbitrary",) * n_grid)


def _whole_vmem():
    return pl.BlockSpec(memory_space=pltpu.VMEM)


def _rows(width, tile=TS):
    return pl.BlockSpec((tile, width), lambda i: (i, 0))


def _sigmoid(z):
    return 0.5 * jnp.tanh(0.5 * z) + 0.5


def _log1p(u):
    w = 1.0 + u
    return jnp.where(w == 1.0, u, jnp.log(w) * (u / (w - 1.0)))


def _softplus(z):
    return jnp.maximum(z, 0.0) + _log1p(jnp.exp(-jnp.abs(z)))


def _gelu_parts(g):
    inner = GELU_C0 * (g + GELU_C1 * g * g * g)
    t = jnp.tanh(inner)
    val = 0.5 * g * (1.0 + t)
    dinner = GELU_C0 * (1.0 + 3.0 * GELU_C1 * g * g)
    grad = 0.5 * (1.0 + t) + 0.5 * g * (1.0 - t * t) * dinner
    return val, grad


def _rms(x):
    rstd = lax.rsqrt(jnp.mean(x * x, axis=-1, keepdims=True) + EPS)
    return rstd, x * rstd


def _rms_bwd(dy, g, xhat, rstd):
    dxh = dy * g
    dx = rstd * (dxh - xhat * jnp.mean(dxh * xhat, axis=-1, keepdims=True))
    dg = jnp.sum(dy * xhat, axis=0, keepdims=True)
    return dx, dg


def _dot(a, b):
    return jnp.dot(a, b, preferred_element_type=F32)


def _dot_nt(a, b):
    return lax.dot_general(a, b, (((1,), (1,)), ((), ())), preferred_element_type=F32)


def _dot_tn(a, b):
    return lax.dot_general(a, b, (((0,), (0,)), ((), ())), preferred_element_type=F32)


def _shifted(tile, prev8, next8, k):
    n = tile.shape[0]
    row = lax.broadcasted_iota(jnp.int32, tile.shape, 0)
    if k == 0:
        return tile
    if k < 0:
        r = pltpu.roll(tile, -k, 0)
        for j in range(-k):
            r = jnp.where(row == j, prev8[SUB + j + k:SUB + j + k + 1, :], r)
        return r
    r = pltpu.roll(tile, n - k, 0)
    for j in range(k):
        r = jnp.where(row == n - k + j, next8[j:j + 1, :], r)
    return r


def _to_lane_blocks(val, s_ref):
    for j in range(val.shape[1] // 128):
        s_ref[j] = val[:, j * 128:(j + 1) * 128]


def _from_lane_blocks(s_ref):
    return jnp.concatenate([s_ref[j] for j in range(s_ref.shape[0])], axis=-1)


def _class_rows(s_ref, r, dil):
    n = s_ref.shape[1] // dil
    return jnp.concatenate([s_ref[j, pl.ds(r, n, stride=dil), :] for j in range(s_ref.shape[0])], axis=-1)


def _split_classes(val, s_ref, out_ref, dil):
    _to_lane_blocks(val, s_ref)
    for r in range(dil):
        out_ref[r] = _class_rows(s_ref, r, dil).astype(out_ref.dtype)


def _merge_classes(in_ref, s_ref, dil, also_ref=None):
    n = s_ref.shape[1] // dil
    for r in range(dil):
        v = in_ref[r].astype(F32)
        if also_ref is not None:
            v = v + also_ref[r].astype(F32)
        for j in range(s_ref.shape[0]):
            s_ref[j, pl.ds(r, n, stride=dil), :] = v[:, j * 128:(j + 1) * 128]
    return _from_lane_blocks(s_ref)


def _class_spec(dil, tile=TS):
    return pl.BlockSpec((dil, tile // dil, 512), lambda i: (0, i, 0))


def _class_shape(S, dil, dtype):
    return jax.ShapeDtypeStruct((dil, S // dil, 512), dtype)


def _scan_tile(a_ref, b_ref, h_ref, carry_ref, reverse):
    n = a_ref.shape[0]
    width = a_ref.shape[1]
    groups = n // SUB
    row = lax.broadcasted_iota(jnp.int32, (SUB, width), 0)

    def group_scan(g):
        r0 = pl.multiple_of(g * SUB, SUB)
        a = a_ref[pl.ds(r0, SUB), :]
        b = b_ref[pl.ds(r0, SUB), :]
        for s in (1, 2, 4):
            if reverse:
                a_sh = pltpu.roll(a, SUB - s, 0)
                b_sh = pltpu.roll(b, SUB - s, 0)
                m = row < SUB - s
            else:
                a_sh = pltpu.roll(a, s, 0)
                b_sh = pltpu.roll(b, s, 0)
                m = row >= s
            b = jnp.where(m, a * b_sh + b, b)
            a = jnp.where(m, a * a_sh, a)
        return r0, a, b

    def step(i, carry):
        first = i * SCAN_UNROLL
        order = [(groups - 1 - (first + u)) if reverse else (first + u) for u in range(SCAN_UNROLL)]
        scans = [group_scan(g) for g in order]
        for r0, a, b in scans:
            h = b + a * carry
            h_ref[pl.ds(r0, SUB), :] = h
            edge = h[0:1, :] if reverse else h[SUB - 1:SUB, :]
            carry = jnp.broadcast_to(edge, (SUB, width))
        return carry

    carry_ref[...] = lax.fori_loop(0, groups // SCAN_UNROLL, step, carry_ref[...])


def _conv_fwd(xr, prev8, next8, cw, cb):
    y = cb + _shifted(xr, prev8, next8, -2) * cw[0:1, :]
    y = y + _shifted(xr, prev8, next8, -1) * cw[1:2, :]
    y = y + xr * cw[2:3, :]
    y = y + _shifted(xr, prev8, next8, 1) * cw[3:4, :]
    return y


def _lru_gates(xc, wa_ref, ba, wx_ref, bx, lam):
    xcb = xc.astype(BF16)
    r = _sigmoid(_dot(xcb, wa_ref[...]) + ba)
    i = _sigmoid(_dot(xcb, wx_ref[...]) + bx)
    cl = -LRU_C * _softplus(-lam)
    la = cl * r
    a = jnp.exp(la)
    m2 = -jnp.tanh(la) * (a * a + 1.0)
    inv = jnp.where(m2 > 0.0, lax.rsqrt(m2), 0.0)
    mult = m2 * inv
    return xcb, r, i, cl, a, mult, inv


def _inproj_fwd(x, g1, w_in, rider=None):
    S = x.shape[0]

    def body(x_ref, g_ref, w_ref, xr_ref, gate_ref, *rest):
        qkv_refs, s_ref, s4_ref, w_full = rest[:9], rest[9], rest[10], rest[11]

        @pl.when(pl.program_id(0) == 0)
        def _():
            for j in range(N_SHARD):
                w_full[:, j * IN_BLK:(j + 1) * IN_BLK] = w_ref[j]

        _, xh = _rms(x_ref[...])
        h = (xh * g_ref[...]).astype(BF16)
        proj = _dot(h, w_full[...])
        xr_ref[...] = proj[:, 0:512]
        gate_ref[...] = proj[:, 512:1024]
        for t in range(3):
            val = proj[:, 1024 + 512 * t:1536 + 512 * t]
            d1_ref, d4_ref, d16_ref = qkv_refs[3 * t:3 * t + 3]
            d1_ref[0] = val.astype(BF16)
            _to_lane_blocks(val, s_ref)
            for r4 in range(4):
                c4 = _class_rows(s_ref, r4, 4)
                d4_ref[r4] = c4.astype(BF16)
                _to_lane_blocks(c4, s4_ref.at[r4])
            for r4 in range(4):
                for m in range(4):
                    d16_ref[r4 + 4 * m] = _class_rows(s4_ref.at[r4], m, 4).astype(BF16)

    f = jax.ShapeDtypeStruct((S, 512), F32)
    return _call(
        body, "inproj_fwd", (S // TS,),
        [_rows(D_MODEL), _whole_vmem(), _whole_vmem()],
        [_rows(512)] * 2 + [_class_spec(d) for d in DILATIONS] * 3,
        [f, f] + [_class_shape(S, d, BF16) for d in DILATIONS] * 3,
        [pltpu.VMEM((4, TS, 128), F32), pltpu.VMEM((4, 4, TS // 4, 128), F32), pltpu.VMEM((D_MODEL, D_IN), BF16)],
        (x, g1, w_in), rider)


def _halo_specs(S, order, tile=TS):
    per = tile // SUB
    last = S // SUB - 1
    return [
        pl.BlockSpec((tile, 512), lambda i: (order(i), 0)),
        pl.BlockSpec((SUB, 512), lambda i: (jnp.maximum(order(i) * per - 1, 0), 0)),
        pl.BlockSpec((SUB, 512), lambda i: (jnp.minimum((order(i) + 1) * per, last), 0)),
    ]


def _rnn_fwd(xr, conv_w, conv_b, wa, ba, wx, bx, lam, reverse, rider=None, xc=None):
    S = xr.shape[0]
    nt = S // TS
    order = (lambda i: nt - 1 - i) if reverse else (lambda i: i)
    with_conv = xc is None
    n_x = 5 if with_conv else 1
    tile = pl.BlockSpec((TS, 512), lambda i: (order(i), 0))

    def body(*refs):
        wa_ref, ba_ref, wx_ref, bx_ref, lam_ref, h_ref = refs[n_x:n_x + 6]
        a_s, b_s, carry = refs[-3:]
        i = pl.program_id(0)
        t = order(i)

        @pl.when(i == 0)
        def _():
            carry[...] = jnp.zeros_like(carry)

        if with_conv:
            x_ref, xp_ref, xn_ref, cw_ref, cb_ref = refs[:5]
            prev8 = jnp.where(t > 0, xp_ref[...], 0.0)
            next8 = jnp.where(t < nt - 1, xn_ref[...], 0.0)
            xcv = _conv_fwd(x_ref[...], prev8, next8, cw_ref[...], cb_ref[...])
            refs[n_x + 6][...] = xcv
        else:
            xcv = refs[0][...]
        _, _, gi, _, a, mult, _ = _lru_gates(xcv, wa_ref, ba_ref[...], wx_ref, bx_ref[...], lam_ref[...])
        a_s[...] = a
        b_s[...] = mult * (gi * xcv)
        _scan_tile(a_s, b_s, h_ref, carry, reverse)

    f512 = jax.ShapeDtypeStruct((S, 512), F32)
    return _call(
        body, "rnn_fwd_rev" if reverse else "rnn_fwd_fwd", (nt,),
        (_halo_specs(S, order) + [_whole_vmem()] * 2 if with_conv else [tile]) + [_whole_vmem()] * 5,
        [tile, tile] if with_conv else [tile], [f512, f512] if with_conv else [f512],
        [pltpu.VMEM((TS, 512), F32), pltpu.VMEM((TS, 512), F32), pltpu.VMEM((SUB, 512), F32)],
        ((xr, xr, xr, conv_w, conv_b) if with_conv else (xc,)) + (wa, ba, wx, bx, lam), rider)


def _mix_fwd(o3, l3, hf, hb, gate, x, g_rnn, g_attn, w_out):
    S = x.shape[0]

    def body(o1, o2, o3_, l1, l2, l3_, hf_ref, hb_ref, gate_ref, x_ref, gr_ref, ga_ref, w_ref,
             x1_ref, mix_ref, ya1, ya2, ls1, ls2, s_ref):
        la, lb, lc = l1[0], _merge_classes(l2, s_ref, F32_LAYOUT), _merge_classes(l3_, s_ref, F32_LAYOUT)
        m = jnp.maximum(jnp.maximum(la, lb), lc)
        ea, eb, ec = jnp.exp(la - m), jnp.exp(lb - m), jnp.exp(lc - m)
        den = ea + eb + ec
        lse = m + jnp.log(den)
        ya = (ea * o1[0] + eb * _merge_classes(o2, s_ref, F32_LAYOUT) + ec * _merge_classes(o3_, s_ref, F32_LAYOUT)) / den
        ya1[0] = ya
        ls1[0] = lse
        _split_classes(ya, s_ref, ya2, F32_LAYOUT)
        _split_classes(lse, s_ref, ls2, F32_LAYOUT)
        gg, _ = _gelu_parts(gate_ref[...])
        yr = (hf_ref[...] + hb_ref[...]) * gg
        _, xh_r = _rms(yr)
        _, xh_a = _rms(ya)
        mix = jnp.concatenate([xh_r * gr_ref[...], xh_a * ga_ref[...]], axis=-1).astype(BF16)
        mix_ref[...] = mix
        acc = x_ref[...]
        for j in range(N_SHARD):
            acc = acc + _dot(mix[:, j * OUT_BLK:(j + 1) * OUT_BLK], w_ref[j])
        x1_ref[...] = acc

    one, four = _class_spec(1), _class_spec(F32_LAYOUT)
    return pl.pallas_call(
        body, grid=(S // TS,), name="mix_fwd",
        in_specs=[one, four, four] * 2 + [_rows(512)] * 3 + [_rows(D_MODEL)] + [_whole_vmem()] * 3,
        out_specs=[_rows(D_MODEL), _rows(D_MODEL)] + [one, four] * 2,
        out_shape=[jax.ShapeDtypeStruct((S, D_MODEL), F32), jax.ShapeDtypeStruct((S, D_MODEL), BF16)]
        + [_class_shape(S, 1, F32), _class_shape(S, F32_LAYOUT, F32)] * 2,
        scratch_shapes=[pltpu.VMEM((4, TS, 128), F32)],
        compiler_params=_params(),
    )(*o3, *l3, hf, hb, gate, x, g_rnn, g_attn, w_out)


def _mlp_fwd_bwd(x1, target, g_mlp, g_fin, w_up, w_down):
    S = x1.shape[0]
    tm = TS_MLP

    def body(x1_ref, t_ref, gm_ref, gf_ref, wu_ref, wd_ref,
             dx1_ref, h2_ref, a2_ref, du_ref, dx2_ref, loss_ref, dgf_ref, dgm_ref, relu_s):
        @pl.when(pl.program_id(0) == 0)
        def _():
            loss_ref[...] = jnp.zeros_like(loss_ref)
            dgf_ref[...] = jnp.zeros_like(dgf_ref)
            dgm_ref[...] = jnp.zeros_like(dgm_ref)

        x1v = x1_ref[...]
        rstd1, xh1 = _rms(x1v)
        h2 = (xh1 * gm_ref[...]).astype(BF16)
        h2_ref[...] = h2
        x2 = x1v
        for j in range(N_SHARD):
            r = jnp.maximum(_dot(h2, wu_ref[j]), 0.0)
            relu_s[j] = r
            a2 = (r * r).astype(BF16)
            a2_ref[:, j * FF_BLK:(j + 1) * FF_BLK] = a2
            x2 = x2 + _dot(a2, wd_ref[j])
        rstd2, xh2 = _rms(x2)
        err = xh2 * gf_ref[...] - t_ref[...]
        loss_ref[...] += jnp.sum(err * err, axis=0, keepdims=True)
        dy = err * (1.0 / D_MODEL)
        dx2, dgf = _rms_bwd(dy, gf_ref[...], xh2, rstd2)
        dgf_ref[...] += dgf
        dx2b = dx2.astype(BF16)
        dx2_ref[...] = dx2b
        dh2 = jnp.zeros((tm, D_MODEL), F32)
        for j in range(N_SHARD):
            du = (_dot_nt(dx2b, wd_ref[j]) * (2.0 * relu_s[j])).astype(BF16)
            du_ref[:, j * FF_BLK:(j + 1) * FF_BLK] = du
            dh2 = dh2 + _dot_nt(du, wu_ref[j])
        dx1n, dgm = _rms_bwd(dh2, gm_ref[...], xh1, rstd1)
        dgm_ref[...] += dgm
        dx1_ref[...] = dx2 + dx1n

    vec = jax.ShapeDtypeStruct((1, D_MODEL), F32)
    return pl.pallas_call(
        body, grid=(S // tm,), name="mlp_fwd_bwd",
        in_specs=[_rows(D_MODEL, tm), _rows(D_MODEL, tm)] + [_whole_vmem()] * 4,
        out_specs=[_rows(D_MODEL, tm), _rows(D_MODEL, tm), _rows(D_FF, tm), _rows(D_FF, tm), _rows(D_MODEL, tm)]
        + [_whole_vmem()] * 3,
        out_shape=[jax.ShapeDtypeStruct((S, D_MODEL), F32), jax.ShapeDtypeStruct((S, D_MODEL), BF16),
                   jax.ShapeDtypeStruct((S, D_FF), BF16), jax.ShapeDtypeStruct((S, D_FF), BF16),
                   jax.ShapeDtypeStruct((S, D_MODEL), BF16), vec, vec, vec],
        scratch_shapes=[pltpu.VMEM((N_SHARD, tm, FF_BLK), F32)],
        compiler_params=_params(),
    )(x1, target, g_mlp, g_fin, w_up, w_down)


def _mix_bwd(dx1, w_out, mixb, ya, hf, hb, gate, g_rnn, g_attn):
    S = dx1.shape[0]

    def body(dx1_ref, w_ref, mix_ref, ya_ref, hf_ref, hb_ref, gate_ref, gr_ref, ga_ref,
             dhs_ref, dgate_ref, dya1, dya2, dw_ref, dgr_ref, dga_ref, s_ref):
        @pl.when(pl.program_id(0) == 0)
        def _():
            dw_ref[...] = jnp.zeros_like(dw_ref)
            dgr_ref[...] = jnp.zeros_like(dgr_ref)
            dga_ref[...] = jnp.zeros_like(dga_ref)

        dx1b = dx1_ref[...].astype(BF16)
        mix = mix_ref[...]
        for j in range(N_SHARD):
            dw_ref[j] += _dot_tn(mix[:, j * OUT_BLK:(j + 1) * OUT_BLK], dx1b)
        dmix = jnp.concatenate([_dot_nt(dx1b, w_ref[j]) for j in range(N_SHARD)], axis=-1)
        gg, dgg = _gelu_parts(gate_ref[...])
        hs = hf_ref[...] + hb_ref[...]
        rstd_r, xh_r = _rms(hs * gg)
        dyr, dgr = _rms_bwd(dmix[:, 0:D_RNN], gr_ref[...], xh_r, rstd_r)
        dgr_ref[...] += dgr
        rstd_a, xh_a = _rms(ya_ref[0])
        dya, dga = _rms_bwd(dmix[:, D_RNN:], ga_ref[...], xh_a, rstd_a)
        dga_ref[...] += dga
        dya1[0] = dya
        _split_classes(dya, s_ref, dya2, F32_LAYOUT)
        dhs_ref[...] = dyr * gg
        dgate_ref[...] = dyr * hs * dgg

    f512 = jax.ShapeDtypeStruct((S, 512), F32)
    vec = jax.ShapeDtypeStruct((1, 512), F32)
    return pl.pallas_call(
        body, grid=(S // TS,), name="mix_bwd",
        in_specs=[_rows(D_MODEL), _whole_vmem(), _rows(D_MODEL), _class_spec(1)] + [_rows(512)] * 3 + [_whole_vmem()] * 2,
        out_specs=[_rows(512)] * 2 + [_class_spec(1), _class_spec(F32_LAYOUT)] + [_whole_vmem()] * 3,
        out_shape=[f512, f512, _class_shape(S, 1, F32), _class_shape(S, F32_LAYOUT, F32),
                   jax.ShapeDtypeStruct((N_SHARD, OUT_BLK, D_MODEL), F32), vec, vec],
        scratch_shapes=[pltpu.VMEM((4, TS, 128), F32)],
        compiler_params=_params(),
    )(dx1, w_out, mixb, ya, hf, hb, gate, g_rnn, g_attn)


def _rnn_bwd(xc, h, dhs, wa, ba, wx, bx, lam, reverse, rider=None):
    S = xc.shape[0]
    nt = S // TS_RNN_BWD
    order = (lambda i: i) if reverse else (lambda i: nt - 1 - i)
    per = TS_RNN_BWD // SUB
    last = S // SUB - 1
    if reverse:
        h_halo = pl.BlockSpec((SUB, 512), lambda i: (jnp.minimum((order(i) + 1) * per, last), 0))
    else:
        h_halo = pl.BlockSpec((SUB, 512), lambda i: (jnp.maximum(order(i) * per - 1, 0), 0))
    tile = pl.BlockSpec((TS_RNN_BWD, 512), lambda i: (order(i), 0))

    def body(xc_ref, h_ref, hh_ref, dh_ref, wa_ref, ba_ref, wx_ref, bx_ref, lam_ref,
             dxc_ref, dwa_ref, dwx_ref, dvec_ref, a_s, g_s, carry, edge):
        i = pl.program_id(0)
        t = order(i)

        @pl.when(i == 0)
        def _():
            carry[...] = jnp.zeros_like(carry)
            edge[...] = jnp.zeros_like(edge)
            dwa_ref[...] = jnp.zeros_like(dwa_ref)
            dwx_ref[...] = jnp.zeros_like(dwx_ref)
            dvec_ref[...] = jnp.zeros_like(dvec_ref)

        xc = xc_ref[...]
        xcb, r, gi, cl, a, mult, inv_mult = _lru_gates(xc, wa_ref, ba_ref[...], wx_ref, bx_ref[...], lam_ref[...])
        hv = h_ref[...]
        if reverse:
            a_s[...] = _shifted(a, edge[...], None, -1)
            edge[...] = a[TS_RNN_BWD - SUB:TS_RNN_BWD, :]
            hh = jnp.where(t < nt - 1, hh_ref[...], 0.0)
            h_prev = _shifted(hv, None, hh, 1)
        else:
            a_s[...] = _shifted(a, None, edge[...], 1)
            edge[...] = a[0:SUB, :]
            hh = jnp.where(t > 0, hh_ref[...], 0.0)
            h_prev = _shifted(hv, hh, None, -1)
        _scan_tile(a_s, dh_ref, g_s, carry, not reverse)
        g = g_s[...]
        da = g * h_prev
        gm = g * mult
        d_i = gm * xc
        dmult = g * gi * xc
        dla = da * a - dmult * (a * a) * inv_mult
        d_r = dla * cl
        dpre_r = d_r * r * (1.0 - r)
        dpre_i = d_i * gi * (1.0 - gi)
        dprb = dpre_r.astype(BF16)
        dpib = dpre_i.astype(BF16)
        dwa_ref[...] += _dot_tn(xcb, dprb)
        dwx_ref[...] += _dot_tn(xcb, dpib)
        dvec_ref[0:1, :] += jnp.sum(dpre_r, axis=0, keepdims=True)
        dvec_ref[1:2, :] += jnp.sum(dpre_i, axis=0, keepdims=True)
        dvec_ref[2:3, :] += jnp.sum(dla * r, axis=0, keepdims=True)
        dvec_ref[3:4, :] = dvec_ref[2:3, :] * (LRU_C * _sigmoid(-lam_ref[...]))
        dxc_ref[...] = gm * gi + _dot_nt(dprb, wa_ref[...]) + _dot_nt(dpib, wx_ref[...])

    sq = jax.ShapeDtypeStruct((D_RNN, D_RNN), F32)
    return _call(
        body, "rnn_bwd_rev" if reverse else "rnn_bwd_fwd", (nt,),
        [tile, tile, h_halo, tile] + [_whole_vmem()] * 5,
        [tile, _whole_vmem(), _whole_vmem(), _whole_vmem()],
        [jax.ShapeDtypeStruct((S, 512), F32), sq, sq, jax.ShapeDtypeStruct((SUB, 512), F32)],
        [pltpu.VMEM((TS_RNN_BWD, 512), F32), pltpu.VMEM((TS_RNN_BWD, 512), F32), pltpu.VMEM((SUB, 512), F32),
         pltpu.VMEM((SUB, 512), F32)],
        (xc, h, h, dhs, wa, ba, wx, bx, lam), rider)


def _inproj_bwd(x, dx1, xr, dxc_f, dxc_b, dgate, dq3, dk3, dv3, g1, conv_w, w_in, rider=None):
    S = x.shape[0]
    tb = TS_INPROJ_BWD
    nt = S // tb
    ident = lambda i: i

    def body(x_ref, dx1_ref, xr_ref, xrp_ref, xrn_ref, cf_ref, cfp_ref, cfn_ref, cb_ref, cbp_ref, cbn_ref, dgate_ref,
             dq1, dq2, dq3_, dk1, dk2, dk3_, dv1, dv2, dv3_, g_ref, cw_ref, w_ref,
             dx_ref, dw_ref, dg_ref, dcw_ref, s_ref, w_full, dw_full, sems):
        i = pl.program_id(0)

        def blocks(full, blocked, k0):
            return [(full.at[:, j * IN_BLK:(j + 1) * IN_BLK], blocked.at[j], sems.at[k0 + j]) for j in range(N_SHARD)]

        @pl.when(i == 0)
        def _():
            copies = [pltpu.make_async_copy(src, dst, sem) for dst, src, sem in blocks(w_full, w_ref, 0)]
            for cp in copies:
                cp.start()
            for cp in copies:
                cp.wait()
            dw_full[...] = jnp.zeros_like(dw_full)
            dg_ref[...] = jnp.zeros_like(dg_ref)
            dcw_ref[...] = jnp.zeros_like(dcw_ref)

        first, last = i > 0, i < nt - 1
        dxc = cf_ref[...] + cb_ref[...]
        dxc_p = jnp.where(first, cfp_ref[...] + cbp_ref[...], 0.0)
        dxc_n = jnp.where(last, cfn_ref[...] + cbn_ref[...], 0.0)
        cw = cw_ref[...]
        dxr = (_shifted(dxc, dxc_p, dxc_n, 2) * cw[0:1, :] + _shifted(dxc, dxc_p, dxc_n, 1) * cw[1:2, :]
               + dxc * cw[2:3, :] + _shifted(dxc, dxc_p, dxc_n, -1) * cw[3:4, :])
        xrv = xr_ref[...]
        xr_p = jnp.where(first, xrp_ref[...], 0.0)
        xr_n = jnp.where(last, xrn_ref[...], 0.0)
        for k, off in enumerate((-2, -1, 0, 1)):
            dcw_ref[k:k + 1, :] += jnp.sum(dxc * _shifted(xrv, xr_p, xr_n, off), axis=0, keepdims=True)
        dcw_ref[4:5, :] += jnp.sum(dxc, axis=0, keepdims=True)

        def total(a, b, c_):
            return a[0].astype(F32) + _merge_classes(b, s_ref, F32_LAYOUT, c_)

        dproj = jnp.concatenate(
            [dxr, dgate_ref[...], total(dq1, dq2, dq3_), total(dk1, dk2, dk3_), total(dv1, dv2, dv3_)],
            axis=-1).astype(BF16)
        xv = x_ref[...]
        rstd, xh = _rms(xv)
        hb = (xh * g_ref[...]).astype(BF16)
        dh = _dot_nt(dproj, w_full[...])
        dw_full[...] += _dot_tn(hb, dproj)
        dxn, dg = _rms_bwd(dh, g_ref[...], xh, rstd)
        dg_ref[...] += dg
        dx_ref[...] = dx1_ref[...] + dxn

        @pl.when(i == nt - 1)
        def _():
            copies = [pltpu.make_async_copy(src, dst, sem) for src, dst, sem in blocks(dw_full, dw_ref, N_SHARD)]
            for cp in copies:
                cp.start()
            for cp in copies:
                cp.wait()

    halo = _halo_specs(S, ident, tb)
    return _call(
        body, "inproj_bwd", (nt,),
        [_rows(D_MODEL, tb), _rows(D_MODEL, tb)] + halo * 3 + [_rows(512, tb)]
        + [_class_spec(1, tb), _class_spec(F32_LAYOUT, tb), _class_spec(F32_LAYOUT, tb)] * 3 + [_whole_vmem()] * 2 + [ANY],
        [_rows(D_MODEL, tb), ANY, _whole_vmem(), _whole_vmem()],
        [jax.ShapeDtypeStruct((S, D_MODEL), F32), jax.ShapeDtypeStruct((N_SHARD, D_MODEL, IN_BLK), F32),
         jax.ShapeDtypeStruct((1, D_MODEL), F32), jax.ShapeDtypeStruct((SUB, 512), F32)],
        [pltpu.VMEM((4, tb, 128), F32), pltpu.VMEM((D_MODEL, D_IN), BF16), pltpu.VMEM((D_MODEL, D_IN), F32),
         pltpu.SemaphoreType.DMA((2 * N_SHARD,))],
        (x, dx1, xr, xr, xr, dxc_f, dxc_f, dxc_f, dxc_b, dxc_b, dxc_b, dgate, *dq3, *dk3, *dv3, g1, conv_w, w_in), rider)


def _dw_matmul(a, b, a_cols, b_cols, name):
    S = a.shape[0]
    tk = min(S, TK_DW)
    a_shared = a.shape[1] == a_cols
    b_shared = b.shape[1] == b_cols

    def body(a_ref, b_ref, o_ref):
        @pl.when(pl.program_id(1) == 0)
        def _():
            o_ref[...] = jnp.zeros_like(o_ref)
        o_ref[0] += _dot_tn(a_ref[...], b_ref[...])

    return pl.pallas_call(
        body, grid=(N_SHARD, S // tk), name=name,
        in_specs=[pl.BlockSpec((tk, a_cols), (lambda j, k: (k, 0)) if a_shared else (lambda j, k: (k, j))),
                  pl.BlockSpec((tk, b_cols), (lambda j, k: (k, 0)) if b_shared else (lambda j, k: (k, j)))],
        out_specs=pl.BlockSpec((1, a_cols, b_cols), lambda j, k: (j, 0, 0)),
        out_shape=jax.ShapeDtypeStruct((N_SHARD, a_cols, b_cols), F32),
        compiler_params=_params(2),
    )(a, b)


def _t5_bucket_np(rel):
    nb = N_BUCKETS // 2
    max_exact = nb // 2
    ret = np.where(rel > 0, nb, 0)
    n = np.abs(rel)
    nf = np.maximum(n, 1).astype(np.float32)
    large = max_exact + (np.log(nf / np.float32(max_exact)) / np.float32(math.log(MAX_DISTANCE / max_exact))
                         * np.float32(nb - max_exact)).astype(np.int32)
    large = np.minimum(large, nb - 1)
    return ret + np.where(n < max_exact, n, large)


_VARIANT_OFFSETS = (-HALF_WIN,) * 3


def _band_index():
    kk = np.arange(K_WIN)[None, :]
    ql = np.arange(Q_BLK)[:, None]
    rel = np.stack([kk - ql + off for off in _VARIANT_OFFSETS])
    band = np.abs(rel) <= HALF_WIN
    inside = np.stack([np.broadcast_to(kk >= HALF_WIN, band[0].shape), np.ones_like(band[0]),
                       np.broadcast_to(kk < K_WIN - HALF_WIN, band[0].shape)])
    return rel, band & inside


def _bucket_tables(dil):
    rel, valid = _band_index()
    bucket = _t5_bucket_np(np.clip(rel, -HALF_WIN, HALF_WIN) * dil)
    return np.where(valid, bucket, -1).astype(np.int32)


def _bias_mats(rel_bias, rider=None):
    tables = [_bucket_tables(d) for d in DILATIONS]
    used = [sorted(set(t[t >= 0].tolist())) for t in tables]

    def one_pattern(rb_ref, t_ref, o_ref, buckets):
        bk = t_ref[1]
        for h in range(N_HEADS):
            acc = jnp.full((Q_BLK, K_WIN), NEG_INF, F32)
            for b in buckets:
                acc = jnp.where(bk == b, rb_ref[b, h], acc)
            o_ref[1, h] = acc
            for var in (0, 2):
                o_ref[var, h] = jnp.where(t_ref[var] >= 0, acc, NEG_INF)

    def body(rb_ref, t1, t2, t3, o1, o2, o3):
        for i, (t_ref, o_ref) in enumerate(((t1, o1), (t2, o2), (t3, o3))):
            pl.when(pl.program_id(0) == i)(functools.partial(one_pattern, rb_ref, t_ref, o_ref, used[i]))

    shp = jax.ShapeDtypeStruct((3, N_HEADS, Q_BLK, K_WIN), F32)
    return _call(
        body, "bias_tables", (len(DILATIONS),), [pl.BlockSpec(memory_space=pltpu.SMEM)] + [_whole_vmem()] * 3,
        [_whole_vmem()] * 3, [shp] * 3, [], (rel_bias, *[jnp.asarray(t) for t in tables]), rider)


def _variant(qb, nq):
    return jnp.where(qb == 0, 0, jnp.where(qb == nq - 1, 2, 1))


def _win_start(qb):
    return pl.multiple_of(qb * Q_BLK, Q_BLK)


def _fill_padded(src_ref, pad_ref):
    L = src_ref.shape[0]
    edge = jnp.zeros((HALF_WIN, 128), pad_ref.dtype)
    pad_ref[0:HALF_WIN, :] = edge
    pad_ref[HALF_WIN:HALF_WIN + L, :] = src_ref[...]
    pad_ref[HALF_WIN + L:2 * HALF_WIN + L, :] = edge


INNER = {1: 1, 4: 1, 16: 4}


def _attn_layout(dil, L):
    inner = INNER[dil]
    n_outer = dil // inner
    nsub = min(ATTN_SUB // inner, L // Q_BLK)
    qt = nsub * Q_BLK
    grid = (4, n_outer, L // qt)
    qspec = pl.BlockSpec((inner, None, qt, 128), lambda hp, r, s: (0, r, s, hp))
    kspec = pl.BlockSpec((inner, None, L, 128), lambda hp, r, s: (0, r, 0, hp))
    bspec = pl.BlockSpec((3, 2, Q_BLK, K_WIN), lambda hp, r, s: (0, hp, 0, 0))
    kfspec = pl.BlockSpec((None, inner * L, 128), lambda hp, r, s: (r, 0, hp))
    qfspec = kfspec if inner > 1 else pl.BlockSpec((None, qt, 128), lambda hp, r, s: (r, s, hp))
    fshape = jax.ShapeDtypeStruct((n_outer, inner * L, D_ATTN), F32)
    view = lambda t: t.reshape(inner, n_outer, L, D_ATTN)

    def qrows(m, sub):
        if inner == 1:
            return (slice(sub * Q_BLK, (sub + 1) * Q_BLK), slice(None))
        first = (pl.program_id(2) * nsub + sub) * Q_BLK
        return (pl.ds(m + inner * first, Q_BLK, stride=inner), slice(None))

    def krows(m):
        if inner == 1:
            return (slice(None), slice(None))
        return (pl.ds(m, L, stride=inner), slice(None))

    return inner, nsub, grid, qspec, kspec, bspec, qfspec, kfspec, fshape, view, qrows, krows


def _head_masks():
    lane = lax.broadcasted_iota(jnp.int32, (Q_BLK, 128), 1)
    return lane < HEAD_DIM


def _attn_fwd(q, k, v, bias):
    dil, L, _ = q.shape
    nq = L // Q_BLK
    inner, nsub, grid, qspec, kspec, bspec, qfspec, kfspec, fshape, view, qrows, krows = _attn_layout(dil, L)

    def body(q_ref, k_ref, v_ref, b_ref, o_ref, l_ref, kp, vp):
        step = pl.program_id(2)

        @pl.when(step == 0)
        def _():
            for m in range(inner):
                _fill_padded(k_ref.at[m], kp.at[m])
                _fill_padded(v_ref.at[m], vp.at[m])

        h0 = _head_masks()
        for m, sub in [(m, sub) for m in range(inner) for sub in range(nsub)]:
            qb = step * nsub + sub
            st = _win_start(qb)
            var = _variant(qb, nq)
            kw = kp[m, pl.ds(st, K_WIN), :]
            vw = vp[m, pl.ds(st, K_WIN), :]
            qs = q_ref[m, sub * Q_BLK:(sub + 1) * Q_BLK, :] * ATTN_SCALE
            zq = jnp.zeros_like(qs)
            q2 = jnp.concatenate([jnp.where(h0, qs, zq), jnp.where(h0, zq, qs)], axis=0)
            s = _dot_nt(q2, kw) + b_ref[var].reshape(2 * Q_BLK, K_WIN)
            top = jnp.max(s, axis=-1, keepdims=True)
            p = jnp.exp(s - top)
            l = jnp.sum(p, axis=-1, keepdims=True)
            out = _dot(p.astype(BF16), vw) / l
            lse = top + jnp.log(l)
            o_ref[qrows(m, sub)] = jnp.where(h0, out[0:Q_BLK], out[Q_BLK:2 * Q_BLK])
            l_ref[qrows(m, sub)] = jnp.where(h0, lse[0:Q_BLK], lse[Q_BLK:2 * Q_BLK])

    return pl.pallas_call(
        body, grid=grid, name=f"attn_fwd_d{dil}",
        in_specs=[qspec, kspec, kspec, bspec], out_specs=[qfspec, qfspec], out_shape=[fshape, fshape],
        scratch_shapes=[pltpu.VMEM((inner, L + 2 * HALF_WIN, 128), BF16)] * 2,
        compiler_params=_params(3),
    )(view(q), view(k), view(v), bias)


def _attn_bwd(q, k, v, bias, do, o, lse, rider=None):
    dil, L, _ = q.shape
    nq = L // Q_BLK
    inner, nsub, grid, qspec, kspec, bspec, qfspec, kfspec, fshape, view, qrows, krows = _attn_layout(dil, L)
    nstep = grid[2]

    def body(q_ref, k_ref, v_ref, b_ref, do_ref, o_ref, l_ref, dq_ref, dk_ref, dv_ref, db_ref, db_s,
             kp, vp, dkp, dvp, carry):
        hp, step = pl.program_id(0), pl.program_id(2)
        first = (hp == 0) & (pl.program_id(1) == 0) & (step == 0)
        last = (hp == grid[0] - 1) & (pl.program_id(1) == grid[1] - 1) & (step == nstep - 1)

        @pl.when(first)
        def _():
            db_s[...] = jnp.zeros_like(db_s)

        @pl.when(step == 0)
        def _():
            for m in range(inner):
                _fill_padded(k_ref.at[m], kp.at[m])
                _fill_padded(v_ref.at[m], vp.at[m])
            carry[...] = jnp.zeros_like(carry)

        h0 = _head_masks()
        for m, sub in [(m, sub) for m in range(inner) for sub in range(nsub)]:
            if sub == 0:
                carry_k, carry_v = carry[m, 0], carry[m, 1]
            qb = step * nsub + sub
            st = _win_start(qb)
            var = _variant(qb, nq)
            kw = kp[m, pl.ds(st, K_WIN), :]
            vw = vp[m, pl.ds(st, K_WIN), :]
            qs = q_ref[m, sub * Q_BLK:(sub + 1) * Q_BLK, :] * ATTN_SCALE
            dof = do_ref[qrows(m, sub)]
            dob = dof.astype(BF16)
            prod = dof * o_ref[qrows(m, sub)]
            lsev = l_ref[qrows(m, sub)]
            zq, zd = jnp.zeros_like(qs), jnp.zeros_like(dob)
            q2 = jnp.concatenate([jnp.where(h0, qs, zq), jnp.where(h0, zq, qs)], axis=0)
            do2 = jnp.concatenate([jnp.where(h0, dob, zd), jnp.where(h0, zd, dob)], axis=0)
            lse2 = jnp.concatenate([lsev[:, 0:1], lsev[:, HEAD_DIM:HEAD_DIM + 1]], axis=0)
            dd2 = jnp.concatenate([jnp.sum(jnp.where(h0, prod, 0.0), axis=-1, keepdims=True),
                                   jnp.sum(jnp.where(h0, 0.0, prod), axis=-1, keepdims=True)], axis=0)
            s = _dot_nt(q2, kw) + b_ref[var].reshape(2 * Q_BLK, K_WIN)
            p = jnp.exp(s - lse2)
            ds = p * (_dot_nt(do2, vw) - dd2)
            db_s[pl.ds(hp * 2, 2)] += ds.reshape(2, Q_BLK, K_WIN)
            dsb = ds.astype(BF16)
            dv_acc = _dot_tn(p.astype(BF16), do2)
            dk_acc = _dot_tn(dsb, q2)
            dq2 = _dot(dsb, kw) * ATTN_SCALE
            dq_ref[qrows(m, sub)] = jnp.where(h0, dq2[0:Q_BLK], dq2[Q_BLK:2 * Q_BLK]).astype(dq_ref.dtype)
            dkp[m, pl.ds(st, Q_BLK), :] = carry_k + dk_acc[0:Q_BLK]
            dvp[m, pl.ds(st, Q_BLK), :] = carry_v + dv_acc[0:Q_BLK]
            carry_k, carry_v = dk_acc[Q_BLK:K_WIN], dv_acc[Q_BLK:K_WIN]
            if sub == nsub - 1:
                carry[m, 0] = carry_k
                carry[m, 1] = carry_v

        @pl.when(step == nstep - 1)
        def _():
            for m in range(inner):
                dkp[m, L:L + Q_BLK, :] = carry[m, 0]
                dvp[m, L:L + Q_BLK, :] = carry[m, 1]
                dk_ref[krows(m)] = dkp[m, HALF_WIN:HALF_WIN + L, :].astype(dk_ref.dtype)
                dv_ref[krows(m)] = dvp[m, HALF_WIN:HALF_WIN + L, :].astype(dv_ref.dtype)

        @pl.when(last)
        def _():
            db_ref[...] = db_s[...]

    dbshape = (N_HEADS, Q_BLK, K_WIN)
    gshape = jax.ShapeDtypeStruct(fshape.shape, BF16 if inner == 1 else F32)
    return _call(
        body, f"attn_bwd_d{dil}", grid,
        [qspec, kspec, kspec, bspec, qfspec, qfspec, qfspec],
        [qfspec, kfspec, kfspec, _whole_vmem()],
        [gshape, gshape, gshape, jax.ShapeDtypeStruct(dbshape, F32)],
        [pltpu.VMEM(dbshape, F32)] + [pltpu.VMEM((inner, L + 2 * HALF_WIN, 128), BF16)] * 2
        + [pltpu.VMEM((inner, L + 2 * HALF_WIN, 128), F32)] * 2 + [pltpu.VMEM((inner, 2, Q_BLK, 128), F32)],
        (view(q), view(k), view(v), bias, do, o, lse), rider)


def _bucket_onehots(dil):
    m = np.zeros((3, K_WIN, N_BUCKETS), np.float32)
    for var, off in enumerate(_VARIANT_OFFSETS):
        for rel in range(-HALF_WIN, HALF_WIN + 1):
            col = (rel - off + Q_BLK - 1) % K_WIN
            m[var, col, int(_t5_bucket_np(np.asarray(rel * dil)))] = 1.0
    return jnp.asarray(m)


def _bias_grad(dbs):
    onehots = [_bucket_onehots(d) for d in DILATIONS]
    flip = jnp.asarray(np.eye(Q_BLK, dtype=np.float32)[::-1].copy())

    def body(d1, d2, d3, m1, m2, m3, flip_ref, out_ref):
        hp = lax.Precision.HIGHEST
        acc = jnp.zeros((N_HEADS, N_BUCKETS), F32)
        for d_ref, m_ref in ((d1, m1), (d2, m2), (d3, m3)):
            rows = []
            for h in range(N_HEADS):
                xrev = jnp.dot(flip_ref[...], d_ref[h], precision=hp, preferred_element_type=F32)
                y = pltpu.roll(xrev, 0, 1, stride=1, stride_axis=0)
                rows.append(jnp.sum(y, axis=0, keepdims=True))
            acc = acc + jnp.dot(jnp.concatenate(rows, axis=0), m_ref[1], precision=hp, preferred_element_type=F32)
        out_ref[...] = acc

    return pl.pallas_call(
        body, name="bias_grad", out_shape=jax.ShapeDtypeStruct((N_HEADS, N_BUCKETS), F32),
        compiler_params=_params(0),
    )(*dbs, *onehots, flip)


def _block_diag(w):
    eye = jnp.eye(N_RNN_BLOCKS, dtype=w.dtype)
    return jnp.einsum("ncd,nm->ncmd", w, eye).reshape(D_RNN, D_RNN).astype(BF16)


def _diag_blocks(dense):
    d = dense.reshape(N_RNN_BLOCKS, RNN_BLOCK, N_RNN_BLOCKS, RNN_BLOCK)
    return jnp.stack([d[n, :, n, :] for n in range(N_RNN_BLOCKS)])


EARLY = ("w_out", "w_up", "w_down")


def _local_step(x, target, p, shards=None):
    p = dict(p)
    first = None if shards is None else _gather_rider(["w_in"], [shards["w_in"]], shards["conv_w"])
    biases, got = _bias_mats(p["rel_bias"], first)
    if shards is not None:
        p["w_in"] = got[0]
        p["conv_w"] = jnp.transpose(got[1], (1, 0, 2)).reshape(4, D_RNN)
    lru = {}
    for dname in ("fwd", "bwd"):
        lru[dname] = (_block_diag(p["lru_wa_" + dname]), p["lru_ba_" + dname], _block_diag(p["lru_wx_" + dname]),
                      p["lru_bx_" + dname], p["lru_lam_" + dname])

    def gather(name):
        return None if shards is None else _gather_rider([name], [shards[name]])

    (xr, gate, *qkv), got = _inproj_fwd(x, p["attn_norm_g"], p["w_in"], gather("w_out"))
    p.update(zip(["w_out"], got))
    qs, ks, vs = qkv[0:3], qkv[3:6], qkv[6:9]
    (hf, xc), got = _rnn_fwd(xr, p["conv_w"], p["conv_b"], *lru["fwd"], reverse=False, rider=gather("w_up"))
    p.update(zip(["w_up"], got))
    (hb,), got = _rnn_fwd(xr, p["conv_w"], p["conv_b"], *lru["bwd"], reverse=True, rider=gather("w_down"), xc=xc)
    p.update(zip(["w_down"], got))
    outs, lses = [], []
    for q, k, v, bias in zip(qs, ks, vs, biases):
        o, l = _attn_fwd(q, k, v, bias)
        outs.append(o)
        lses.append(l)
    x1, mixb, *yl = _mix_fwd(outs, lses, hf, hb, gate, x, p["norm_rnn_g"], p["norm_attn_g"], p["w_out"])
    yas, lsts = [yl[0], yl[1], yl[1]], [yl[2], yl[3], yl[3]]
    dx1, h2b, a2b, dub, dx2b, loss_vec, dg_fin, dg_mlp = _mlp_fwd_bwd(
        x1, target, p["mlp_norm_g"], p["final_norm_g"], p["w_up"], p["w_down"])
    dhs, dgate, dya1, dya4, dw_out, dg_rnn, dg_attn = _mix_bwd(dx1, p["w_out"], mixb, yas[0], hf, hb, gate,
                                                               p["norm_rnn_g"], p["norm_attn_g"])
    dyas = [dya1, dya4, dya4]
    dw_up = _dw_matmul(h2b, dub, D_MODEL, FF_BLK, "dw_up")
    dw_down = _dw_matmul(a2b, dx2b, FF_BLK, D_MODEL, "dw_down")
    early = [dw_out, dw_up, dw_down]
    dqs, dks, dvs, dbs = [], [], [], []
    for i, (q, k, v, bias, dya, ya, lse) in enumerate(zip(qs, ks, vs, biases, dyas, yas, lsts)):
        rider = None
        if shards is not None:
            make = (lambda: _pair_exchange_rider(EARLY, early), lambda: _chip_exchange_rider(early),
                    lambda: _pair_share_rider(EARLY, early))[i]
            rider = make()
        (dq, dk, dv, db), got = _attn_bwd(q, k, v, bias, dya, ya, lse, rider)
        if shards is not None and i == 0:
            core = lax.axis_index("c").reshape(1).astype(jnp.int32)
            early = [_pair_add(core, g, o, "grad_pair_add_" + n) for n, g, o in zip(EARLY, early, got)]
        elif shards is not None and i == 1:
            early = [_chip_sum(t, "grad_chip_sum_" + n) for n, t in zip(EARLY, got)]
        elif shards is not None:
            early = got
        dqs.append(dq)
        dks.append(dk)
        dvs.append(dv)
        dbs.append(db)
    d_rel_bias = _bias_grad(dbs).T
    (dxc_f, dwa_f, dwx_f, dvec_f), _ = _rnn_bwd(xc, hf, dhs, *lru["fwd"], reverse=False)
    small = {
        "lru_wa_fwd": _diag_blocks(dwa_f), "lru_ba_fwd": dvec_f[0:1], "lru_wx_fwd": _diag_blocks(dwx_f),
        "lru_bx_fwd": dvec_f[1:2], "lru_lam_fwd": dvec_f[3:4],
        "rel_bias": d_rel_bias, "norm_rnn_g": dg_rnn, "norm_attn_g": dg_attn,
        "mlp_norm_g": dg_mlp, "final_norm_g": dg_fin,
    }
    loss_local = (0.5 / D_MODEL) * jnp.sum(loss_vec)
    rider = None
    if shards is not None:
        rider = _small_gather_rider(_pack([small[n].reshape(shp) for n, shp in SMALL if n in small]
                                          + [loss_local.reshape(1)]))
    (dxc_b, dwa_b, dwx_b, dvec_b), gathered = _rnn_bwd(xc, hb, dhs, *lru["bwd"], reverse=True, rider=rider)
    grad_x, dw_in, dg1, dconv = _inproj_bwd(x, dx1, xr, dxc_f, dxc_b, dgate, dqs, dks, dvs,
                                            p["attn_norm_g"], p["conv_w"], p["w_in"])[0]
    last = {"lru_wa_bwd": _diag_blocks(dwa_b), "lru_ba_bwd": dvec_b[0:1], "lru_wx_bwd": _diag_blocks(dwx_b),
            "lru_bx_bwd": dvec_b[1:2], "lru_lam_bwd": dvec_b[3:4],
            "attn_norm_g": dg1, "conv_w": dconv[0:4], "conv_b": dconv[4:5]}
    if shards is None:
        big = {"w_in": dw_in, "w_out": dw_out, "w_up": dw_up, "w_down": dw_down}
        return loss_local, grad_x, {**small, **last}, None, big, {}
    return loss_local, grad_x, last, gathered[0], {"w_in": dw_in}, dict(zip(EARLY, early))


BIG = ("w_in", "w_out", "w_up", "w_down")
BIG_SHARD = {"w_in": (D_MODEL, IN_BLK), "w_out": (OUT_BLK, D_MODEL), "w_up": (D_MODEL, FF_BLK), "w_down": (FF_BLK, D_MODEL)}
N_BIG = len(BIG)
N_CHIP_PEERS = 3
ANY = pl.BlockSpec(memory_space=pl.ANY)


def _place():
    x, y, c = lax.axis_index("x"), lax.axis_index("y"), lax.axis_index("c")
    chips = [(1 - x, y), (x, 1 - y), (1 - x, 1 - y)]
    return x, y, c, chips


def _remote(src, dst, send_sem, recv_sem, dev):
    return pltpu.make_async_remote_copy(src_ref=src, dst_ref=dst, send_sem=send_sem, recv_sem=recv_sem,
                                        device_id=dev, device_id_type=MESH)


def _staged_start(srcs, bufs, sems):
    legs = [pltpu.make_async_copy(s, b, sems.at[i]) for i, (s, b) in enumerate(zip(srcs, bufs))]
    for cp in legs:
        cp.start()
    return legs


def _staged_finish(legs, bufs, dsts, sems):
    out = []
    for i, (leg, b, d) in enumerate(zip(legs, bufs, dsts)):
        leg.wait()
        cp = pltpu.make_async_copy(b, d, sems.at[i])
        cp.start()
        out.append(cp)
    return out


class _Rider:
    def __init__(self, inputs, out_shape, scratch, first, late, last):
        self.inputs, self.out_shape, self.scratch = list(inputs), list(out_shape), list(scratch)
        self.first, self.late, self.last = first, late, last


def _call(body, name, grid, in_specs, out_specs, out_shape, scratch, operands, rider=None):
    n_grid = len(grid)
    if rider is None:
        res = pl.pallas_call(body, grid=grid, name=name, in_specs=in_specs, out_specs=out_specs, out_shape=out_shape,
                             scratch_shapes=scratch, compiler_params=_params(n_grid))(*operands)
        return list(res), []
    n_in, n_out, n_scr = len(in_specs), len(out_specs), len(scratch)
    ri, ro = len(rider.inputs), len(rider.out_shape)
    nsteps = int(np.prod(grid))
    late_step = max(nsteps - 3, 1)

    def wrapped(*refs):
        a, b = n_in, n_in + ri
        c, d = b + n_out, b + n_out + ro
        e = d + n_scr
        mine = refs[:a] + refs[b:c] + refs[d:e]
        theirs = (refs[a:b], refs[c:d], refs[e:])
        step = pl.program_id(0)
        for ax in range(1, n_grid):
            step = step * grid[ax] + pl.program_id(ax)
        pl.when(step == 0)(lambda: rider.first(*theirs))
        pl.when(step == late_step)(lambda: rider.late(*theirs))
        body(*mine)
        pl.when(step == nsteps - 1)(lambda: rider.last(*theirs))

    res = pl.pallas_call(
        wrapped, grid=grid, name=name, in_specs=list(in_specs) + [ANY] * ri, out_specs=list(out_specs) + [ANY] * ro,
        out_shape=list(out_shape) + rider.out_shape, scratch_shapes=list(scratch) + rider.scratch,
        compiler_params=_params(n_grid),
    )(*operands, *rider.inputs)
    return list(res[:n_out]), list(res[n_out:])


def _run_rider(rider, name):
    ri, ro = len(rider.inputs), len(rider.out_shape)

    def body(*refs):
        parts = (refs[:ri], refs[ri:ri + ro], refs[ri + ro:])
        rider.first(*parts)
        rider.late(*parts)
        rider.last(*parts)

    return list(pl.pallas_call(
        body, name=name, in_specs=[ANY] * ri, out_specs=[ANY] * ro, out_shape=rider.out_shape, scratch_shapes=rider.scratch,
        compiler_params=pltpu.CompilerParams(has_side_effects=True, vmem_limit_bytes=VMEM_LIMIT),
    )(*rider.inputs))


def _nothing(ins, outs, scr):
    return None


def _gather_rider(names, shards, conv_w=None):
    n = len(names)
    items = n + (conv_w is not None)
    halves = [BIG_SHARD[nm][0] // 2 for nm in names]

    def parts(ins, outs, scr):
        x, y, c, chips = _place()
        return x, y, c, chips, 2 * x + y, (x, y, 1 - c), scr[:8], scr[8:]

    def piece(outs, w, chip, core_half):
        return outs[w].at[chip, pl.ds(core_half * halves[w], halves[w])]

    def ici(ins, outs, sems, w, k, chip_xy, c, me):
        return _remote(ins[w].at[pl.ds(c * halves[w], halves[w])], piece(outs, w, me, c),
                       sems[0].at[w, k], sems[1].at[w, k], (*chip_xy, c))

    def first(ins, outs, scr):
        x, y, c, chips, me, sibling, sems, bufs = parts(ins, outs, scr)
        legs = _staged_start(ins, bufs, sems[6])
        for w in range(n):
            for k, chip_xy in enumerate(chips):
                ici(ins, outs, sems, w, k, chip_xy, c, me).start()
        if conv_w is not None:
            for k, (px, py) in enumerate(chips):
                _remote(ins[n], outs[n].at[me], sems[4].at[k], sems[5].at[k], (px, py, c)).start()
        _staged_finish(legs, bufs, [o.at[me] for o in outs], sems[7])

    def late(ins, outs, scr):
        x, y, c, chips, me, sibling, sems, bufs = parts(ins, outs, scr)
        for w in range(n):
            for k, (px, py) in enumerate(chips):
                landed = piece(outs, w, 2 * px + py, c)
                _remote(landed, landed, sems[0].at[w, k], sems[1].at[w, k], (px, py, c)).wait_recv()
                _remote(landed, landed, sems[2].at[w, k], sems[3].at[w, k], sibling).start()

    def last(ins, outs, scr):
        x, y, c, chips, me, sibling, sems, bufs = parts(ins, outs, scr)
        for w in range(n):
            for k, (px, py) in enumerate(chips):
                other = piece(outs, w, 2 * px + py, 1 - c)
                _remote(other, other, sems[2].at[w, k], sems[3].at[w, k], sibling).wait_recv()
        if conv_w is not None:
            for k, (px, py) in enumerate(chips):
                got = outs[n].at[2 * px + py]
                _remote(got, got, sems[4].at[k], sems[5].at[k], (px, py, c)).wait_recv()
                _remote(ins[n], outs[n].at[me], sems[4].at[k], sems[5].at[k], (px, py, c)).wait_send()
        for i in range(items):
            pltpu.make_async_copy(bufs[i], outs[i].at[me], sems[7].at[i]).wait()
        for w in range(n):
            for k, (px, py) in enumerate(chips):
                ici(ins, outs, sems, w, k, (px, py), c, me).wait_send()
                landed = piece(outs, w, 2 * px + py, c)
                _remote(landed, landed, sems[2].at[w, k], sems[3].at[w, k], sibling).wait_send()

    out_shape = [jax.ShapeDtypeStruct((N_SHARD,) + BIG_SHARD[nm], BF16) for nm in names]
    stage = [pltpu.VMEM(BIG_SHARD[nm], BF16) for nm in names]
    inputs = list(shards)
    if conv_w is not None:
        out_shape.append(jax.ShapeDtypeStruct((N_SHARD,) + conv_w.shape, F32))
        stage.append(pltpu.VMEM(conv_w.shape, F32))
        inputs.append(conv_w)
    scratch = ([pltpu.SemaphoreType.DMA((n, N_CHIP_PEERS))] * 4 + [pltpu.SemaphoreType.DMA((N_CHIP_PEERS,))] * 2
               + [pltpu.SemaphoreType.DMA((items,))] * 2 + stage)
    return _Rider(inputs, out_shape, scratch, first, late, last)


def _pair_exchange_rider(names, grads):
    def copies(ins, outs, scr):
        x, y, c, _ = _place()
        out = []
        for w, nm in enumerate(names):
            h = BIG_SHARD[nm][0] // 2
            out.append(_remote(ins[w].at[:, pl.ds((1 - c) * h, h), :], outs[w], scr[0].at[w], scr[1].at[w], (x, y, 1 - c)))
        return out

    def first(ins, outs, scr):
        for cp in copies(ins, outs, scr):
            cp.start()

    def last(ins, outs, scr):
        for cp in copies(ins, outs, scr):
            cp.wait()

    out_shape = [jax.ShapeDtypeStruct((N_SHARD, BIG_SHARD[nm][0] // 2, BIG_SHARD[nm][1]), F32) for nm in names]
    return _Rider(grads, out_shape, [pltpu.SemaphoreType.DMA((len(names),))] * 2, first, _nothing, last)


def _pair_add(core, grad, other, name):
    _, r, cols = grad.shape
    h = r // 2
    th = min(h, 256)
    per = h // th

    def body(c_ref, g_ref, o_ref, out_ref):
        out_ref[...] = (g_ref[...] + o_ref[...]).astype(BF16)

    return pl.pallas_call(
        body, name=name,
        grid_spec=pltpu.PrefetchScalarGridSpec(
            num_scalar_prefetch=1, grid=(N_SHARD, per),
            in_specs=[pl.BlockSpec((1, th, cols), lambda j, i, c_ref: (j, c_ref[0] * per + i, 0)),
                      pl.BlockSpec((1, th, cols), lambda j, i, c_ref: (j, i, 0))],
            out_specs=pl.BlockSpec((1, th, cols), lambda j, i, c_ref: (j, i, 0))),
        out_shape=jax.ShapeDtypeStruct((N_SHARD, h, cols), BF16),
        compiler_params=_params(2),
    )(core, grad, other)


def _chip_exchange_rider(parts):
    n = len(parts)

    def sends(ins, outs, scr):
        x, y, c, chips = _place()
        me = 2 * x + y
        return [_remote(ins[w].at[2 * px + py], outs[w].at[me], scr[0].at[w, k], scr[1].at[w, k], (px, py, c))
                for w in range(n) for k, (px, py) in enumerate(chips)]

    def first(ins, outs, scr):
        x, y, c, chips = _place()
        me = 2 * x + y
        legs = _staged_start([r.at[me] for r in ins], scr[4:], scr[2])
        for cp in sends(ins, outs, scr):
            cp.start()
        _staged_finish(legs, scr[4:], [o.at[me] for o in outs], scr[3])

    def last(ins, outs, scr):
        x, y, c, chips = _place()
        me = 2 * x + y
        for w in range(n):
            for k, (px, py) in enumerate(chips):
                got = outs[w].at[2 * px + py]
                _remote(got, got, scr[0].at[w, k], scr[1].at[w, k], (px, py, c)).wait_recv()
        for cp in sends(ins, outs, scr):
            cp.wait_send()
        for w in range(n):
            pltpu.make_async_copy(scr[4 + w], outs[w].at[me], scr[3].at[w]).wait()

    out_shape = [jax.ShapeDtypeStruct(p.shape, BF16) for p in parts]
    scratch = ([pltpu.SemaphoreType.DMA((n, N_CHIP_PEERS))] * 2 + [pltpu.SemaphoreType.DMA((n,))] * 2
               + [pltpu.VMEM(p.shape[1:], BF16) for p in parts])
    return _Rider(parts, out_shape, scratch, first, _nothing, last)


def _chip_sum(parts, name):
    _, h, cols = parts.shape
    th = min(h, 256)

    def body(p_ref, out_ref):
        acc = p_ref[0].astype(F32)
        for j in range(1, N_SHARD):
            acc = acc + p_ref[j].astype(F32)
        out_ref[...] = acc

    return pl.pallas_call(
        body, name=name, grid=(h // th,),
        in_specs=[pl.BlockSpec((N_SHARD, th, cols), lambda i: (0, i, 0))],
        out_specs=pl.BlockSpec((th, cols), lambda i: (i, 0)),
        out_shape=jax.ShapeDtypeStruct((h, cols), F32),
        compiler_params=_params(),
    )(parts)


def _pair_share_rider(names, halves):
    n = len(names)
    hs = [BIG_SHARD[nm][0] // 2 for nm in names]

    def mine(outs, c):
        return [outs[w].at[pl.ds(c * hs[w], hs[w])] for w in range(n)]

    def first(ins, outs, scr):
        x, y, c, _ = _place()
        legs = _staged_start(ins, scr[4:], scr[2])
        for w, dst in enumerate(mine(outs, c)):
            _remote(ins[w], dst, scr[0].at[w], scr[1].at[w], (x, y, 1 - c)).start()
        _staged_finish(legs, scr[4:], mine(outs, c), scr[3])

    def last(ins, outs, scr):
        x, y, c, _ = _place()
        for w, (theirs, dst) in enumerate(zip(mine(outs, 1 - c), mine(outs, c))):
            _remote(theirs, theirs, scr[0].at[w], scr[1].at[w], (x, y, 1 - c)).wait_recv()
            _remote(ins[w], dst, scr[0].at[w], scr[1].at[w], (x, y, 1 - c)).wait_send()
            pltpu.make_async_copy(scr[4 + w], dst, scr[3].at[w]).wait()

    out_shape = [jax.ShapeDtypeStruct(BIG_SHARD[nm], F32) for nm in names]
    scratch = [pltpu.SemaphoreType.DMA((n,))] * 4 + [pltpu.VMEM((h, BIG_SHARD[nm][1]), F32) for nm, h in zip(names, hs)]
    return _Rider(halves, out_shape, scratch, first, _nothing, last)


N_DEV = 8


def _all_peers(x, y, c):
    return [((1 - x) if fx else x, (1 - y) if fy else y, (1 - c) if fc else c)
            for fx in (0, 1) for fy in (0, 1) for fc in (0, 1) if fx or fy or fc]


def _small_gather_rider(vec):
    def sends(ins, outs, scr):
        x, y, c, _ = _place()
        me = 4 * x + 2 * y + c
        return [_remote(ins[0], outs[0].at[me], scr[0].at[k], scr[1].at[k], dev) for k, dev in enumerate(_all_peers(x, y, c))]

    def first(ins, outs, scr):
        x, y, c, _ = _place()
        legs = _staged_start(ins, scr[4:], scr[2])
        for cp in sends(ins, outs, scr):
            cp.start()
        _staged_finish(legs, scr[4:], [outs[0].at[4 * x + 2 * y + c]], scr[3])

    def last(ins, outs, scr):
        x, y, c, _ = _place()
        for k, (px, py, pc) in enumerate(_all_peers(x, y, c)):
            got = outs[0].at[4 * px + 2 * py + pc]
            _remote(got, got, scr[0].at[k], scr[1].at[k], (px, py, pc)).wait_recv()
        for cp in sends(ins, outs, scr):
            cp.wait_send()
        pltpu.make_async_copy(scr[4], outs[0].at[4 * x + 2 * y + c], scr[3].at[0]).wait()

    scratch = ([pltpu.SemaphoreType.DMA((N_DEV - 1,))] * 2 + [pltpu.SemaphoreType.DMA((1,))] * 2
               + [pltpu.VMEM(vec.shape, F32)])
    return _Rider([vec], [jax.ShapeDtypeStruct((N_DEV,) + vec.shape, F32)], scratch, first, _nothing, last)


def _sum_devices(gathered):
    def body(g_ref, out_ref):
        acc = g_ref[0]
        for j in range(1, N_DEV):
            acc = acc + g_ref[j]
        out_ref[...] = acc

    return pl.pallas_call(body, name="sum_devices", out_shape=jax.ShapeDtypeStruct(gathered.shape[1:], F32),
                          compiler_params=_params(0))(gathered)


def _allreduce_small(vec):
    rows = vec.shape[0]

    def body(v_ref, sum_ref, gat_ref, send, recv, loc_sem):
        x, y, c, chips = _place()
        sibling = (x, y, 1 - c)
        slot = lambda px, py, pc: gat_ref.at[4 * px + 2 * py + pc]
        lc = pltpu.make_async_copy(v_ref, slot(x, y, c), loc_sem)
        lc.start()
        sends = [_remote(v_ref, slot(x, y, c), send.at[0], recv.at[0], sibling)]
        sends += [_remote(v_ref, slot(x, y, c), send.at[1 + k], recv.at[1 + k], (px, py, c))
                  for k, (px, py) in enumerate(chips)]
        for cp in sends:
            cp.start()
        for k, (px, py) in enumerate(chips):
            got = slot(px, py, c)
            _remote(got, got, send.at[1 + k], recv.at[1 + k], (px, py, c)).wait_recv()
            cp = _remote(got, got, send.at[4 + k], recv.at[4 + k], sibling)
            cp.start()
            sends.append(cp)
        got = slot(x, y, 1 - c)
        _remote(got, got, send.at[0], recv.at[0], sibling).wait_recv()
        for k, (px, py) in enumerate(chips):
            got = slot(px, py, 1 - c)
            _remote(got, got, send.at[4 + k], recv.at[4 + k], sibling).wait_recv()
        for cp in sends:
            cp.wait_send()
        lc.wait()
        acc = gat_ref[0]
        for j in range(1, N_DEV):
            acc = acc + gat_ref[j]
        sum_ref[...] = acc

    total, _ = pl.pallas_call(
        body, name="allreduce_small",
        in_specs=[_whole_vmem()], out_specs=[_whole_vmem(), _whole_vmem()],
        out_shape=[jax.ShapeDtypeStruct((rows, 128), F32), jax.ShapeDtypeStruct((N_DEV, rows, 128), F32)],
        scratch_shapes=[pltpu.SemaphoreType.DMA((N_DEV - 1,))] * 2 + [pltpu.SemaphoreType.DMA(())],
        compiler_params=pltpu.CompilerParams(has_side_effects=True, vmem_limit_bytes=VMEM_LIMIT),
    )(vec)
    return total


def _adam_math(w_ref, g_ref, m_ref, v_ref, d_ref, m2_ref, v2_ref):
    c1 = 1.0 - ADAM_B1 ** ADAM_STEP
    c2 = 1.0 - ADAM_B2 ** ADAM_STEP
    gv = g_ref[...]
    m2 = ADAM_B1 * m_ref[...] + (1.0 - ADAM_B1) * gv
    v2 = ADAM_B2 * v_ref[...] + (1.0 - ADAM_B2) * (gv * gv)
    m2_ref[...] = m2
    v2_ref[...] = v2
    d_ref[...] = -ADAM_LR * ((m2 / c1) / (jnp.sqrt(v2 / c2) + ADAM_EPS) + ADAM_WD * w_ref[...])


def _adamw_many(ws, gs, ms, vs):
    n = len(ws)

    def body(*refs):
        for i in range(n):
            _adam_math(*[refs[k * n + i] for k in range(7)])

    shapes = [jax.ShapeDtypeStruct(w.shape, F32) for w in ws]
    res = pl.pallas_call(body, name="adamw_small", out_shape=shapes * 3, compiler_params=_params(0))(*ws, *gs, *ms, *vs)
    return res[:n], res[n:2 * n], res[2 * n:]


def _adamw(w, g, m, v, name):
    rows, cols = w.shape
    tr = 256 if rows % 256 == 0 else rows

    def body(w_ref, g_ref, m_ref, v_ref, d_ref, m2_ref, v2_ref):
        _adam_math(w_ref, g_ref, m_ref, v_ref, d_ref, m2_ref, v2_ref)

    spec = pl.BlockSpec((tr, cols), lambda i: (i, 0))
    shp = jax.ShapeDtypeStruct((rows, cols), F32)
    return pl.pallas_call(
        body, name=name, grid=(rows // tr,), in_specs=[spec] * 4, out_specs=[spec] * 3, out_shape=[shp] * 3,
        compiler_params=_params(),
    )(w, g, m, v)


SMALL = (
    ("attn_norm_g", (1, 1024)), ("conv_w", (1, 4, 512)), ("conv_b", (1, 512)),
    ("lru_wa_fwd", (1, 8, 64, 64)), ("lru_ba_fwd", (1, 512)), ("lru_wx_fwd", (1, 8, 64, 64)), ("lru_bx_fwd", (1, 512)),
    ("lru_lam_fwd", (1, 512)),
    ("lru_wa_bwd", (1, 8, 64, 64)), ("lru_ba_bwd", (1, 512)), ("lru_wx_bwd", (1, 8, 64, 64)), ("lru_bx_bwd", (1, 512)),
    ("lru_lam_bwd", (1, 512)),
    ("rel_bias", (32, 8)), ("norm_rnn_g", (1, 512)), ("norm_attn_g", (1, 512)), ("mlp_norm_g", (1, 1024)),
    ("final_norm_g", (1024,)),
)
PACK_ROW = 8 * 128


def _pack(parts):
    flat = jnp.concatenate([p.reshape(-1) for p in parts])
    pad = (-flat.shape[0]) % PACK_ROW
    return jnp.pad(flat, (0, pad)).reshape(-1, 128)


def _unpack(packed, shapes):
    flat = packed.reshape(-1)
    out, off = [], 0
    for shp in shapes:
        n = int(np.prod(shp))
        out.append(flat[off:off + n].reshape(shp))
        off += n
    return out


WEIGHT_ORDER = ("attn_norm_g", "w_in", "conv_w", "conv_b", "lru_wa_fwd", "lru_ba_fwd", "lru_wx_fwd", "lru_bx_fwd",
                "lru_lam_fwd", "lru_wa_bwd", "lru_ba_bwd", "lru_wx_bwd", "lru_bx_bwd", "lru_lam_bwd", "rel_bias",
                "norm_rnn_g", "norm_attn_g", "w_out", "mlp_norm_g", "w_up", "w_down", "final_norm_g")


def kernel(x, attn_norm_g, w_in, conv_w, conv_b, lru_wa_fwd, lru_ba_fwd, lru_wx_fwd, lru_bx_fwd, lru_lam_fwd, lru_wa_bwd, lru_ba_bwd, lru_wx_bwd, lru_bx_bwd, lru_lam_bwd, rel_bias, norm_rnn_g, norm_attn_g, w_out, mlp_norm_g, w_up, w_down, final_norm_g, loss_target, m_attn_norm_g, m_w_in, m_conv_w, m_conv_b, m_lru_wa_fwd, m_lru_ba_fwd, m_lru_wx_fwd, m_lru_bx_fwd, m_lru_lam_fwd, m_lru_wa_bwd, m_lru_ba_bwd, m_lru_wx_bwd, m_lru_bx_bwd, m_lru_lam_bwd, m_rel_bias, m_norm_rnn_g, m_norm_attn_g, m_w_out, m_mlp_norm_g, m_w_up, m_w_down, m_final_norm_g, v_attn_norm_g, v_w_in, v_conv_w, v_conv_b, v_lru_wa_fwd, v_lru_ba_fwd, v_lru_wx_fwd, v_lru_bx_fwd, v_lru_lam_fwd, v_lru_wa_bwd, v_lru_ba_bwd, v_lru_wx_bwd, v_lru_bx_bwd, v_lru_lam_bwd, v_rel_bias, v_norm_rnn_g, v_norm_attn_g, v_w_out, v_mlp_norm_g, v_w_up, v_w_down, v_final_norm_g):
    given = dict(locals())
    w = {n: given[n] for n in WEIGHT_ORDER}
    m = {n: given["m_" + n] for n in WEIGHT_ORDER}
    v = {n: given["v_" + n] for n in WEIGHT_ORDER}

    chip = lax.axis_index("x") * 2 + lax.axis_index("y")
    core = lax.axis_index("c")

    shards = {n: w[n][0].astype(BF16) for n in BIG}
    shards["conv_w"] = w["conv_w"][0]
    p = {n: (t[0] if t.ndim >= 3 else t) for n, t in w.items() if n not in BIG and n != "conv_w"}
    p["final_norm_g"] = w["final_norm_g"].reshape(1, D_MODEL)

    _, grad_x, small, gathered, big, reduced = _local_step(x[0], loss_target[0], p, shards)

    late = tuple(big)
    grads = [big[n] for n in late]
    others = _run_rider(_pair_exchange_rider(late, grads), "grad_pair_exchange")
    core_arr = core.reshape(1).astype(jnp.int32)
    parts = [_pair_add(core_arr, g, o, "grad_pair_add_" + n) for n, g, o in zip(late, grads, others)]
    landed = _run_rider(_chip_exchange_rider(parts), "grad_chip_exchange")
    halves = [_chip_sum(t, "grad_chip_sum_" + n) for n, t in zip(late, landed)]
    reduced.update(zip(late, _run_rider(_pair_share_rider(late, halves), "grad_pair_share")))

    early_small = [(n, shp) for n, shp in SMALL if n not in small]
    late_small = [(n, shp) for n, shp in SMALL if n in small]
    *early_g, loss = _unpack(_sum_devices(gathered), [shp for _, shp in early_small] + [(1,)])
    late_g = _unpack(_allreduce_small(_pack([small[n].reshape(shp) for n, shp in late_small])),
                     [shp for _, shp in late_small])
    g = dict(zip([n for n, _ in early_small + late_small], early_g + late_g))
    g["conv_w"] = lax.dynamic_slice_in_dim(g["conv_w"], chip * (D_RNN // N_SHARD), D_RNN // N_SHARD, axis=2)
    for n in BIG:
        g[n] = reduced[n][None]

    delta, new_m, new_v = {}, {}, {}
    for n in BIG:
        d2, m2, v2 = _adamw(w[n][0], reduced[n], m[n][0], v[n][0], "adamw_" + n)
        delta[n], new_m[n], new_v[n] = d2[None], m2[None], v2[None]
    names = [n for n, _ in SMALL]
    for dst, src in zip((delta, new_m, new_v), _adamw_many(*[[t[n] for n in names] for t in (w, g, m, v)])):
        dst.update(dict(zip(names, src)))

    return (loss.reshape(()), grad_x[None], *[g[n] for n in WEIGHT_ORDER], *[delta[n] for n in WEIGHT_ORDER],
            *[new_m[n] for n in WEIGHT_ORDER], *[new_v[n] for n in WEIGHT_ORDER])
```

```python
import functools
import math

import numpy as np
import jax
import jax.numpy as jnp
from jax import lax
from jax.experimental import pallas as pl
from jax.experimental.pallas import tpu as pltpu

F32 = jnp.float32
BF16 = jnp.bfloat16

D_MODEL = 1024
D_RNN = 512
D_ATTN = 512
N_HEADS = 8
HEAD_DIM = 64
N_RNN_BLOCKS = 8
RNN_BLOCK = 64
D_IN = 2 * D_RNN + 3 * D_ATTN
D_FF = 4 * D_MODEL
N_SHARD = 4
IN_BLK = D_IN // N_SHARD
OUT_BLK = D_MODEL // N_SHARD
FF_BLK = D_FF // N_SHARD
EPS = 1e-6
NEG_INF = -1e30
LRU_C = 8.0
DILATIONS = (1, 4, 16)
F32_LAYOUT = 4
HALF_WIN = 64
Q_BLK = 128
K_WIN = 256
N_BUCKETS = 32
MAX_DISTANCE = 1024
ATTN_SCALE = HEAD_DIM ** -0.5

ADAM_LR = 0.001
ADAM_B1 = 0.9
ADAM_B2 = 0.999
ADAM_EPS = 1e-08
ADAM_WD = 0.01
ADAM_STEP = 10

TS = 512
TS_RNN_BWD = 1024
TS_MLP = 256
TS_INPROJ_BWD = 512
ATTN_SUB = 16
ATTN_SUB_FWD = 32
TK_DW = 4096
SCAN_UNROLL = 8
SUB = 8
VMEM_LIMIT = 56 * 1024 * 1024
GELU_C0 = math.sqrt(2.0 / math.pi)
GELU_C1 = 0.044715

MESH = pl.DeviceIdType.MESH


def _params(n_grid=1):
    return pltpu.CompilerParams(vmem_limit_bytes=VMEM_LIMIT, dimension_semantics=("arbitrary",) * n_grid)


def _whole_vmem():
    return pl.BlockSpec(memory_space=pltpu.VMEM)


def _rows(width, tile=TS):
    return pl.BlockSpec((tile, width), lambda i: (i, 0))


def _sigmoid(z):
    return 0.5 * jnp.tanh(0.5 * z) + 0.5


def _log1p(u):
    w = 1.0 + u
    return jnp.where(w == 1.0, u, jnp.log(w) * (u / (w - 1.0)))


def _softplus(z):
    return jnp.maximum(z, 0.0) + _log1p(jnp.exp(-jnp.abs(z)))


def _gelu_parts(g):
    inner = GELU_C0 * (g + GELU_C1 * g * g * g)
    t = jnp.tanh(inner)
    val = 0.5 * g * (1.0 + t)
    dinner = GELU_C0 * (1.0 + 3.0 * GELU_C1 * g * g)
    grad = 0.5 * (1.0 + t) + 0.5 * g * (1.0 - t * t) * dinner
    return val, grad


def _rms(x):
    rstd = lax.rsqrt(jnp.mean(x * x, axis=-1, keepdims=True) + EPS)
    return rstd, x * rstd


def _rms_bwd(dy, g, xhat, rstd):
    dxh = dy * g
    dx = rstd * (dxh - xhat * jnp.mean(dxh * xhat, axis=-1, keepdims=True))
    dg = jnp.sum(dy * xhat, axis=0, keepdims=True)
    return dx, dg


def _dot(a, b):
    return jnp.dot(a, b, preferred_element_type=F32)


def _dot_nt(a, b):
    return lax.dot_general(a, b, (((1,), (1,)), ((), ())), preferred_element_type=F32)


def _dot_tn(a, b):
    return lax.dot_general(a, b, (((0,), (0,)), ((), ())), preferred_element_type=F32)


def _shifted(tile, prev8, next8, k):
    n = tile.shape[0]
    row = lax.broadcasted_iota(jnp.int32, tile.shape, 0)
    if k == 0:
        return tile
    if k < 0:
        r = pltpu.roll(tile, -k, 0)
        for j in range(-k):
            r = jnp.where(row == j, prev8[SUB + j + k:SUB + j + k + 1, :], r)
        return r
    r = pltpu.roll(tile, n - k, 0)
    for j in range(k):
        r = jnp.where(row == n - k + j, next8[j:j + 1, :], r)
    return r


def _to_lane_blocks(val, s_ref):
    for j in range(val.shape[1] // 128):
        s_ref[j] = val[:, j * 128:(j + 1) * 128]


def _from_lane_blocks(s_ref):
    return jnp.concatenate([s_ref[j] for j in range(s_ref.shape[0])], axis=-1)


def _class_rows(s_ref, r, dil):
    n = s_ref.shape[1] // dil
    return jnp.concatenate([s_ref[j, pl.ds(r, n, stride=dil), :] for j in range(s_ref.shape[0])], axis=-1)


def _split_classes(val, s_ref, out_ref, dil):
    _to_lane_blocks(val, s_ref)
    for r in range(dil):
        out_ref[r] = _class_rows(s_ref, r, dil).astype(out_ref.dtype)


def _merge_classes(in_ref, s_ref, dil, also_ref=None):
    n = s_ref.shape[1] // dil
    for r in range(dil):
        v = in_ref[r].astype(F32)
        if also_ref is not None:
            v = v + also_ref[r].astype(F32)
        for j in range(s_ref.shape[0]):
            s_ref[j, pl.ds(r, n, stride=dil), :] = v[:, j * 128:(j + 1) * 128]
    return _from_lane_blocks(s_ref)


def _class_spec(dil, tile=TS):
    return pl.BlockSpec((dil, tile // dil, 512), lambda i: (0, i, 0))


def _class_shape(S, dil, dtype):
    return jax.ShapeDtypeStruct((dil, S // dil, 512), dtype)


def _scan_tile(a_ref, b_ref, h_ref, carry_ref, reverse):
    n = a_ref.shape[0]
    width = a_ref.shape[1]
    groups = n // SUB
    row = lax.broadcasted_iota(jnp.int32, (SUB, width), 0)

    def group_scan(g):
        r0 = pl.multiple_of(g * SUB, SUB)
        a = a_ref[pl.ds(r0, SUB), :]
        b = b_ref[pl.ds(r0, SUB), :]
        for s in (1, 2, 4):
            if reverse:
                a_sh = pltpu.roll(a, SUB - s, 0)
                b_sh = pltpu.roll(b, SUB - s, 0)
                m = row < SUB - s
            else:
                a_sh = pltpu.roll(a, s, 0)
                b_sh = pltpu.roll(b, s, 0)
                m = row >= s
            b = jnp.where(m, a * b_sh + b, b)
            a = jnp.where(m, a * a_sh, a)
        return r0, a, b

    def step(i, carry):
        first = i * SCAN_UNROLL
        order = [(groups - 1 - (first + u)) if reverse else (first + u) for u in range(SCAN_UNROLL)]
        scans = [group_scan(g) for g in order]
        for r0, a, b in scans:
            h = b + a * carry
            h_ref[pl.ds(r0, SUB), :] = h
            edge = h[0:1, :] if reverse else h[SUB - 1:SUB, :]
            carry = jnp.broadcast_to(edge, (SUB, width))
        return carry

    carry_ref[...] = lax.fori_loop(0, groups // SCAN_UNROLL, step, carry_ref[...])


def _conv_fwd(xr, prev8, next8, cw, cb):
    y = cb + _shifted(xr, prev8, next8, -2) * cw[0:1, :]
    y = y + _shifted(xr, prev8, next8, -1) * cw[1:2, :]
    y = y + xr * cw[2:3, :]
    y = y + _shifted(xr, prev8, next8, 1) * cw[3:4, :]
    return y


def _lru_gates(xc, wa_ref, ba, wx_ref, bx, lam):
    xcb = xc.astype(BF16)
    r = _sigmoid(_dot(xcb, wa_ref[...]) + ba)
    i = _sigmoid(_dot(xcb, wx_ref[...]) + bx)
    cl = -LRU_C * _softplus(-lam)
    la = cl * r
    a = jnp.exp(la)
    m2 = -jnp.tanh(la) * (a * a + 1.0)
    inv = jnp.where(m2 > 0.0, lax.rsqrt(m2), 0.0)
    mult = m2 * inv
    return xcb, r, i, cl, a, mult, inv


def _inproj_fwd(x, g1, w_in, rider=None):
    S = x.shape[0]

    def body(x_ref, g_ref, w_ref, xr_ref, gate_ref, *rest):
        qkv_refs, s_ref, s4_ref, w_full = rest[:9], rest[9], rest[10], rest[11]

        @pl.when(pl.program_id(0) == 0)
        def _():
            for j in range(N_SHARD):
                w_full[:, j * IN_BLK:(j + 1) * IN_BLK] = w_ref[j]

        _, xh = _rms(x_ref[...])
        h = (xh * g_ref[...]).astype(BF16)
        proj = _dot(h, w_full[...])
        xr_ref[...] = proj[:, 0:512]
        gate_ref[...] = proj[:, 512:1024]
        for t in range(3):
            val = proj[:, 1024 + 512 * t:1536 + 512 * t]
            d1_ref, d4_ref, d16_ref = qkv_refs[3 * t:3 * t + 3]
            d1_ref[0] = val.astype(BF16)
            _to_lane_blocks(val, s_ref)
            for r4 in range(4):
                c4 = _class_rows(s_ref, r4, 4)
                d4_ref[r4] = c4.astype(BF16)
                _to_lane_blocks(c4, s4_ref.at[r4])
            for r4 in range(4):
                for m in range(4):
                    d16_ref[r4 + 4 * m] = _class_rows(s4_ref.at[r4], m, 4).astype(BF16)

    f = jax.ShapeDtypeStruct((S, 512), F32)
    return _call(
        body, "inproj_fwd", (S // TS,),
        [_rows(D_MODEL), _whole_vmem(), _whole_vmem()],
        [_rows(512)] * 2 + [_class_spec(d) for d in DILATIONS] * 3,
        [f, f] + [_class_shape(S, d, BF16) for d in DILATIONS] * 3,
        [pltpu.VMEM((4, TS, 128), F32), pltpu.VMEM((4, 4, TS // 4, 128), F32), pltpu.VMEM((D_MODEL, D_IN), BF16)],
        (x, g1, w_in), rider)


def _halo_specs(S, order, tile=TS):
    per = tile // SUB
    last = S // SUB - 1
    return [
        pl.BlockSpec((tile, 512), lambda i: (order(i), 0)),
        pl.BlockSpec((SUB, 512), lambda i: (jnp.maximum(order(i) * per - 1, 0), 0)),
        pl.BlockSpec((SUB, 512), lambda i: (jnp.minimum((order(i) + 1) * per, last), 0)),
    ]


def _rnn_fwd(xr, conv_w, conv_b, wa, ba, wx, bx, lam, reverse, rider=None, xc=None):
    S = xr.shape[0]
    nt = S // TS
    order = (lambda i: nt - 1 - i) if reverse else (lambda i: i)
    with_conv = xc is None
    n_x = 5 if with_conv else 1
    tile = pl.BlockSpec((TS, 512), lambda i: (order(i), 0))

    def body(*refs):
        wa_ref, ba_ref, wx_ref, bx_ref, lam_ref, h_ref = refs[n_x:n_x + 6]
        a_s, b_s, carry = refs[-3:]
        i = pl.program_id(0)
        t = order(i)

        @pl.when(i == 0)
        def _():
            carry[...] = jnp.zeros_like(carry)

        if with_conv:
            x_ref, xp_ref, xn_ref, cw_ref, cb_ref = refs[:5]
            prev8 = jnp.where(t > 0, xp_ref[...], 0.0)
            next8 = jnp.where(t < nt - 1, xn_ref[...], 0.0)
            xcv = _conv_fwd(x_ref[...], prev8, next8, cw_ref[...], cb_ref[...])
            refs[n_x + 6][...] = xcv
        else:
            xcv = refs[0][...]
        _, _, gi, _, a, mult, _ = _lru_gates(xcv, wa_ref, ba_ref[...], wx_ref, bx_ref[...], lam_ref[...])
        a_s[...] = a
        b_s[...] = mult * (gi * xcv)
        _scan_tile(a_s, b_s, h_ref, carry, reverse)

    f512 = jax.ShapeDtypeStruct((S, 512), F32)
    return _call(
        body, "rnn_fwd_rev" if reverse else "rnn_fwd_fwd", (nt,),
        (_halo_specs(S, order) + [_whole_vmem()] * 2 if with_conv else [tile]) + [_whole_vmem()] * 5,
        [tile, tile] if with_conv else [tile], [f512, f512] if with_conv else [f512],
        [pltpu.VMEM((TS, 512), F32), pltpu.VMEM((TS, 512), F32), pltpu.VMEM((SUB, 512), F32)],
        ((xr, xr, xr, conv_w, conv_b) if with_conv else (xc,)) + (wa, ba, wx, bx, lam), rider)


def _mix_fwd(o3, l3, hf, hb, gate, x, g_rnn, g_attn, w_out):
    S = x.shape[0]

    def body(o1, o2, o3_, l1, l2, l3_, hf_ref, hb_ref, gate_ref, x_ref, gr_ref, ga_ref, w_ref,
             x1_ref, mix_ref, ya1, ya2, ls1, ls2, s_ref):
        la, lb, lc = l1[0], _merge_classes(l2, s_ref, F32_LAYOUT), _merge_classes(l3_, s_ref, F32_LAYOUT)
        m = jnp.maximum(jnp.maximum(la, lb), lc)
        ea, eb, ec = jnp.exp(la - m), jnp.exp(lb - m), jnp.exp(lc - m)
        den = ea + eb + ec
        lse = m + jnp.log(den)
        ya = (ea * o1[0] + eb * _merge_classes(o2, s_ref, F32_LAYOUT) + ec * _merge_classes(o3_, s_ref, F32_LAYOUT)) / den
        ya1[0] = ya
        ls1[0] = lse
        _split_classes(ya, s_ref, ya2, F32_LAYOUT)
        _split_classes(lse, s_ref, ls2, F32_LAYOUT)
        gg, _ = _gelu_parts(gate_ref[...])
        yr = (hf_ref[...] + hb_ref[...]) * gg
        _, xh_r = _rms(yr)
        _, xh_a = _rms(ya)
        mix = jnp.concatenate([xh_r * gr_ref[...], xh_a * ga_ref[...]], axis=-1).astype(BF16)
        mix_ref[...] = mix
        acc = x_ref[...]
        for j in range(N_SHARD):
            acc = acc + _dot(mix[:, j * OUT_BLK:(j + 1) * OUT_BLK], w_ref[j])
        x1_ref[...] = acc

    one, four = _class_spec(1), _class_spec(F32_LAYOUT)
    return pl.pallas_call(
        body, grid=(S // TS,), name="mix_fwd",
        in_specs=[one, four, four] * 2 + [_rows(512)] * 3 + [_rows(D_MODEL)] + [_whole_vmem()] * 3,
        out_specs=[_rows(D_MODEL), _rows(D_MODEL)] + [one, four] * 2,
        out_shape=[jax.ShapeDtypeStruct((S, D_MODEL), F32), jax.ShapeDtypeStruct((S, D_MODEL), BF16)]
        + [_class_shape(S, 1, F32), _class_shape(S, F32_LAYOUT, F32)] * 2,
        scratch_shapes=[pltpu.VMEM((4, TS, 128), F32)],
        compiler_params=_params(),
    )(*o3, *l3, hf, hb, gate, x, g_rnn, g_attn, w_out)


def _mlp_fwd_bwd(x1, target, g_mlp, g_fin, w_up, w_down):
    S = x1.shape[0]
    tm = TS_MLP

    def body(x1_ref, t_ref, gm_ref, gf_ref, wu_ref, wd_ref,
             dx1_ref, h2_ref, a2_ref, du_ref, dx2_ref, loss_ref, dgf_ref, dgm_ref, relu_s):
        @pl.when(pl.program_id(0) == 0)
        def _():
            loss_ref[...] = jnp.zeros_like(loss_ref)
            dgf_ref[...] = jnp.zeros_like(dgf_ref)
            dgm_ref[...] = jnp.zeros_like(dgm_ref)

        x1v = x1_ref[...]
        rstd1, xh1 = _rms(x1v)
        h2 = (xh1 * gm_ref[...]).astype(BF16)
        h2_ref[...] = h2
        x2 = x1v
        for j in range(N_SHARD):
            r = jnp.maximum(_dot(h2, wu_ref[j]), 0.0)
            relu_s[j] = r
            a2 = (r * r).astype(BF16)
            a2_ref[:, j * FF_BLK:(j + 1) * FF_BLK] = a2
            x2 = x2 + _dot(a2, wd_ref[j])
        rstd2, xh2 = _rms(x2)
        err = xh2 * gf_ref[...] - t_ref[...]
        loss_ref[...] += jnp.sum(err * err, axis=0, keepdims=True)
        dy = err * (1.0 / D_MODEL)
        dx2, dgf = _rms_bwd(dy, gf_ref[...], xh2, rstd2)
        dgf_ref[...] += dgf
        dx2b = dx2.astype(BF16)
        dx2_ref[...] = dx2b
        dh2 = jnp.zeros((tm, D_MODEL), F32)
        for j in range(N_SHARD):
            du = (_dot_nt(dx2b, wd_ref[j]) * (2.0 * relu_s[j])).astype(BF16)
            du_ref[:, j * FF_BLK:(j + 1) * FF_BLK] = du
            dh2 = dh2 + _dot_nt(du, wu_ref[j])
        dx1n, dgm = _rms_bwd(dh2, gm_ref[...], xh1, rstd1)
        dgm_ref[...] += dgm
        dx1_ref[...] = dx2 + dx1n

    vec = jax.ShapeDtypeStruct((1, D_MODEL), F32)
    return pl.pallas_call(
        body, grid=(S // tm,), name="mlp_fwd_bwd",
        in_specs=[_rows(D_MODEL, tm), _rows(D_MODEL, tm)] + [_whole_vmem()] * 4,
        out_specs=[_rows(D_MODEL, tm), _rows(D_MODEL, tm), _rows(D_FF, tm), _rows(D_FF, tm), _rows(D_MODEL, tm)]
        + [_whole_vmem()] * 3,
        out_shape=[jax.ShapeDtypeStruct((S, D_MODEL), F32), jax.ShapeDtypeStruct((S, D_MODEL), BF16),
                   jax.ShapeDtypeStruct((S, D_FF), BF16), jax.ShapeDtypeStruct((S, D_FF), BF16),
                   jax.ShapeDtypeStruct((S, D_MODEL), BF16), vec, vec, vec],
        scratch_shapes=[pltpu.VMEM((N_SHARD, tm, FF_BLK), F32)],
        compiler_params=_params(),
    )(x1, target, g_mlp, g_fin, w_up, w_down)


def _mix_bwd(dx1, w_out, mixb, ya, hf, hb, gate, g_rnn, g_attn):
    S = dx1.shape[0]

    def body(dx1_ref, w_ref, mix_ref, ya_ref, hf_ref, hb_ref, gate_ref, gr_ref, ga_ref,
             dhs_ref, dgate_ref, dya1, dya2, dw_ref, dgr_ref, dga_ref, s_ref):
        @pl.when(pl.program_id(0) == 0)
        def _():
            dw_ref[...] = jnp.zeros_like(dw_ref)
            dgr_ref[...] = jnp.zeros_like(dgr_ref)
            dga_ref[...] = jnp.zeros_like(dga_ref)

        dx1b = dx1_ref[...].astype(BF16)
        mix = mix_ref[...]
        for j in range(N_SHARD):
            dw_ref[j] += _dot_tn(mix[:, j * OUT_BLK:(j + 1) * OUT_BLK], dx1b)
        dmix = jnp.concatenate([_dot_nt(dx1b, w_ref[j]) for j in range(N_SHARD)], axis=-1)
        gg, dgg = _gelu_parts(gate_ref[...])
        hs = hf_ref[...] + hb_ref[...]
        rstd_r, xh_r = _rms(hs * gg)
        dyr, dgr = _rms_bwd(dmix[:, 0:D_RNN], gr_ref[...], xh_r, rstd_r)
        dgr_ref[...] += dgr
        rstd_a, xh_a = _rms(ya_ref[0])
        dya, dga = _rms_bwd(dmix[:, D_RNN:], ga_ref[...], xh_a, rstd_a)
        dga_ref[...] += dga
        dya1[0] = dya
        _split_classes(dya, s_ref, dya2, F32_LAYOUT)
        dhs_ref[...] = dyr * gg
        dgate_ref[...] = dyr * hs * dgg

    f512 = jax.ShapeDtypeStruct((S, 512), F32)
    vec = jax.ShapeDtypeStruct((1, 512), F32)
    return pl.pallas_call(
        body, grid=(S // TS,), name="mix_bwd",
        in_specs=[_rows(D_MODEL), _whole_vmem(), _rows(D_MODEL), _class_spec(1)] + [_rows(512)] * 3 + [_whole_vmem()] * 2,
        out_specs=[_rows(512)] * 2 + [_class_spec(1), _class_spec(F32_LAYOUT)] + [_whole_vmem()] * 3,
        out_shape=[f512, f512, _class_shape(S, 1, F32), _class_shape(S, F32_LAYOUT, F32),
                   jax.ShapeDtypeStruct((N_SHARD, OUT_BLK, D_MODEL), F32), vec, vec],
        scratch_shapes=[pltpu.VMEM((4, TS, 128), F32)],
        compiler_params=_params(),
    )(dx1, w_out, mixb, ya, hf, hb, gate, g_rnn, g_attn)


def _rnn_bwd(xc, h, dhs, wa, ba, wx, bx, lam, reverse, rider=None):
    S = xc.shape[0]
    nt = S // TS_RNN_BWD
    order = (lambda i: i) if reverse else (lambda i: nt - 1 - i)
    per = TS_RNN_BWD // SUB
    last = S // SUB - 1
    if reverse:
        h_halo = pl.BlockSpec((SUB, 512), lambda i: (jnp.minimum((order(i) + 1) * per, last), 0))
    else:
        h_halo = pl.BlockSpec((SUB, 512), lambda i: (jnp.maximum(order(i) * per - 1, 0), 0))
    tile = pl.BlockSpec((TS_RNN_BWD, 512), lambda i: (order(i), 0))

    def body(xc_ref, h_ref, hh_ref, dh_ref, wa_ref, ba_ref, wx_ref, bx_ref, lam_ref,
             dxc_ref, dwa_ref, dwx_ref, dvec_ref, a_s, g_s, carry, edge):
        i = pl.program_id(0)
        t = order(i)

        @pl.when(i == 0)
        def _():
            carry[...] = jnp.zeros_like(carry)
            edge[...] = jnp.zeros_like(edge)
            dwa_ref[...] = jnp.zeros_like(dwa_ref)
            dwx_ref[...] = jnp.zeros_like(dwx_ref)
            dvec_ref[...] = jnp.zeros_like(dvec_ref)

        xc = xc_ref[...]
        xcb, r, gi, cl, a, mult, inv_mult = _lru_gates(xc, wa_ref, ba_ref[...], wx_ref, bx_ref[...], lam_ref[...])
        hv = h_ref[...]
        if reverse:
            a_s[...] = _shifted(a, edge[...], None, -1)
            edge[...] = a[TS_RNN_BWD - SUB:TS_RNN_BWD, :]
            hh = jnp.where(t < nt - 1, hh_ref[...], 0.0)
            h_prev = _shifted(hv, None, hh, 1)
        else:
            a_s[...] = _shifted(a, None, edge[...], 1)
            edge[...] = a[0:SUB, :]
            hh = jnp.where(t > 0, hh_ref[...], 0.0)
            h_prev = _shifted(hv, hh, None, -1)
        _scan_tile(a_s, dh_ref, g_s, carry, not reverse)
        g = g_s[...]
        da = g * h_prev
        gm = g * mult
        d_i = gm * xc
        dmult = g * gi * xc
        dla = da * a - dmult * (a * a) * inv_mult
        d_r = dla * cl
        dpre_r = d_r * r * (1.0 - r)
        dpre_i = d_i * gi * (1.0 - gi)
        dprb = dpre_r.astype(BF16)
        dpib = dpre_i.astype(BF16)
        dwa_ref[...] += _dot_tn(xcb, dprb)
        dwx_ref[...] += _dot_tn(xcb, dpib)
        dvec_ref[0:1, :] += jnp.sum(dpre_r, axis=0, keepdims=True)
        dvec_ref[1:2, :] += jnp.sum(dpre_i, axis=0, keepdims=True)
        dvec_ref[2:3, :] += jnp.sum(dla * r, axis=0, keepdims=True)
        dvec_ref[3:4, :] = dvec_ref[2:3, :] * (LRU_C * _sigmoid(-lam_ref[...]))
        dxc_ref[...] = gm * gi + _dot_nt(dprb, wa_ref[...]) + _dot_nt(dpib, wx_ref[...])

    sq = jax.ShapeDtypeStruct((D_RNN, D_RNN), F32)
    return _call(
        body, "rnn_bwd_rev" if reverse else "rnn_bwd_fwd", (nt,),
        [tile, tile, h_halo, tile] + [_whole_vmem()] * 5,
        [tile, _whole_vmem(), _whole_vmem(), _whole_vmem()],
        [jax.ShapeDtypeStruct((S, 512), F32), sq, sq, jax.ShapeDtypeStruct((SUB, 512), F32)],
        [pltpu.VMEM((TS_RNN_BWD, 512), F32), pltpu.VMEM((TS_RNN_BWD, 512), F32), pltpu.VMEM((SUB, 512), F32),
         pltpu.VMEM((SUB, 512), F32)],
        (xc, h, h, dhs, wa, ba, wx, bx, lam), rider)


def _inproj_bwd(x, dx1, xr, dxc_f, dxc_b, dgate, dq3, dk3, dv3, g1, conv_w, w_in, rider=None):
    S = x.shape[0]
    tb = TS_INPROJ_BWD
    nt = S // tb
    ident = lambda i: i

    def body(x_ref, dx1_ref, xr_ref, xrp_ref, xrn_ref, cf_ref, cfp_ref, cfn_ref, cb_ref, cbp_ref, cbn_ref, dgate_ref,
             dq1, dq2, dq3_, dk1, dk2, dk3_, dv1, dv2, dv3_, g_ref, cw_ref, w_ref,
             dx_ref, dw_ref, dg_ref, dcw_ref, s_ref, w_full, dw_full, sems):
        i = pl.program_id(0)

        def blocks(full, blocked, k0):
            return [(full.at[:, j * IN_BLK:(j + 1) * IN_BLK], blocked.at[j], sems.at[k0 + j]) for j in range(N_SHARD)]

        @pl.when(i == 0)
        def _():
            copies = [pltpu.make_async_copy(src, dst, sem) for dst, src, sem in blocks(w_full, w_ref, 0)]
            for cp in copies:
                cp.start()
            for cp in copies:
                cp.wait()
            dw_full[...] = jnp.zeros_like(dw_full)
            dg_ref[...] = jnp.zeros_like(dg_ref)
            dcw_ref[...] = jnp.zeros_like(dcw_ref)

        first, last = i > 0, i < nt - 1
        dxc = cf_ref[...] + cb_ref[...]
        dxc_p = jnp.where(first, cfp_ref[...] + cbp_ref[...], 0.0)
        dxc_n = jnp.where(last, cfn_ref[...] + cbn_ref[...], 0.0)
        cw = cw_ref[...]
        dxr = (_shifted(dxc, dxc_p, dxc_n, 2) * cw[0:1, :] + _shifted(dxc, dxc_p, dxc_n, 1) * cw[1:2, :]
               + dxc * cw[2:3, :] + _shifted(dxc, dxc_p, dxc_n, -1) * cw[3:4, :])
        xrv = xr_ref[...]
        xr_p = jnp.where(first, xrp_ref[...], 0.0)
        xr_n = jnp.where(last, xrn_ref[...], 0.0)
        for k, off in enumerate((-2, -1, 0, 1)):
            dcw_ref[k:k + 1, :] += jnp.sum(dxc * _shifted(xrv, xr_p, xr_n, off), axis=0, keepdims=True)
        dcw_ref[4:5, :] += jnp.sum(dxc, axis=0, keepdims=True)

        def total(a, b, c_):
            return a[0].astype(F32) + _merge_classes(b, s_ref, F32_LAYOUT, c_)

        dproj = jnp.concatenate(
            [dxr, dgate_ref[...], total(dq1, dq2, dq3_), total(dk1, dk2, dk3_), total(dv1, dv2, dv3_)],
            axis=-1).astype(BF16)
        xv = x_ref[...]
        rstd, xh = _rms(xv)
        hb = (xh * g_ref[...]).astype(BF16)
        dh = _dot_nt(dproj, w_full[...])
        dw_full[...] += _dot_tn(hb, dproj)
        dxn, dg = _rms_bwd(dh, g_ref[...], xh, rstd)
        dg_ref[...] += dg
        dx_ref[...] = dx1_ref[...] + dxn

        @pl.when(i == nt - 1)
        def _():
            copies = [pltpu.make_async_copy(src, dst, sem) for src, dst, sem in blocks(dw_full, dw_ref, N_SHARD)]
            for cp in copies:
                cp.start()
            for cp in copies:
                cp.wait()

    halo = _halo_specs(S, ident, tb)
    return _call(
        body, "inproj_bwd", (nt,),
        [_rows(D_MODEL, tb), _rows(D_MODEL, tb)] + halo * 3 + [_rows(512, tb)]
        + [_class_spec(1, tb), _class_spec(F32_LAYOUT, tb), _class_spec(F32_LAYOUT, tb)] * 3 + [_whole_vmem()] * 2 + [ANY],
        [_rows(D_MODEL, tb), ANY, _whole_vmem(), _whole_vmem()],
        [jax.ShapeDtypeStruct((S, D_MODEL), F32), jax.ShapeDtypeStruct((N_SHARD, D_MODEL, IN_BLK), F32),
         jax.ShapeDtypeStruct((1, D_MODEL), F32), jax.ShapeDtypeStruct((SUB, 512), F32)],
        [pltpu.VMEM((4, tb, 128), F32), pltpu.VMEM((D_MODEL, D_IN), BF16), pltpu.VMEM((D_MODEL, D_IN), F32),
         pltpu.SemaphoreType.DMA((2 * N_SHARD,))],
        (x, dx1, xr, xr, xr, dxc_f, dxc_f, dxc_f, dxc_b, dxc_b, dxc_b, dgate, *dq3, *dk3, *dv3, g1, conv_w, w_in), rider)


def _dw_matmul(a, b, a_cols, b_cols, name):
    S = a.shape[0]
    tk = min(S, TK_DW)
    a_shared = a.shape[1] == a_cols
    b_shared = b.shape[1] == b_cols

    def body(a_ref, b_ref, o_ref):
        @pl.when(pl.program_id(1) == 0)
        def _():
            o_ref[...] = jnp.zeros_like(o_ref)
        o_ref[0] += _dot_tn(a_ref[...], b_ref[...])

    return pl.pallas_call(
        body, grid=(N_SHARD, S // tk), name=name,
        in_specs=[pl.BlockSpec((tk, a_cols), (lambda j, k: (k, 0)) if a_shared else (lambda j, k: (k, j))),
                  pl.BlockSpec((tk, b_cols), (lambda j, k: (k, 0)) if b_shared else (lambda j, k: (k, j)))],
        out_specs=pl.BlockSpec((1, a_cols, b_cols), lambda j, k: (j, 0, 0)),
        out_shape=jax.ShapeDtypeStruct((N_SHARD, a_cols, b_cols), F32),
        compiler_params=_params(2),
    )(a, b)


def _t5_bucket_np(rel):
    nb = N_BUCKETS // 2
    max_exact = nb // 2
    ret = np.where(rel > 0, nb, 0)
    n = np.abs(rel)
    nf = np.maximum(n, 1).astype(np.float32)
    large = max_exact + (np.log(nf / np.float32(max_exact)) / np.float32(math.log(MAX_DISTANCE / max_exact))
                         * np.float32(nb - max_exact)).astype(np.int32)
    large = np.minimum(large, nb - 1)
    return ret + np.where(n < max_exact, n, large)


_VARIANT_OFFSETS = (-HALF_WIN,) * 3


def _band_index():
    kk = np.arange(K_WIN)[None, :]
    ql = np.arange(Q_BLK)[:, None]
    rel = np.stack([kk - ql + off for off in _VARIANT_OFFSETS])
    band = np.abs(rel) <= HALF_WIN
    inside = np.stack([np.broadcast_to(kk >= HALF_WIN, band[0].shape), np.ones_like(band[0]),
                       np.broadcast_to(kk < K_WIN - HALF_WIN, band[0].shape)])
    return rel, band & inside


def _bucket_tables(dil):
    rel, valid = _band_index()
    bucket = _t5_bucket_np(np.clip(rel, -HALF_WIN, HALF_WIN) * dil)
    return np.where(valid, bucket, -1).astype(np.int32)


def _bias_mats(rel_bias, rider=None):
    tables = [_bucket_tables(d) for d in DILATIONS]
    used = [sorted(set(t[t >= 0].tolist())) for t in tables]

    def one_pattern(rb_ref, t_ref, o_ref, buckets):
        bk = t_ref[1]
        for h in range(N_HEADS):
            acc = jnp.full((Q_BLK, K_WIN), NEG_INF, F32)
            for b in buckets:
                acc = jnp.where(bk == b, rb_ref[b, h], acc)
            o_ref[1, h] = acc
            for var in (0, 2):
                o_ref[var, h] = jnp.where(t_ref[var] >= 0, acc, NEG_INF)

    def body(rb_ref, t1, t2, t3, o1, o2, o3):
        for i, (t_ref, o_ref) in enumerate(((t1, o1), (t2, o2), (t3, o3))):
            pl.when(pl.program_id(0) == i)(functools.partial(one_pattern, rb_ref, t_ref, o_ref, used[i]))

    shp = jax.ShapeDtypeStruct((3, N_HEADS, Q_BLK, K_WIN), F32)
    return _call(
        body, "bias_tables", (len(DILATIONS),), [pl.BlockSpec(memory_space=pltpu.SMEM)] + [_whole_vmem()] * 3,
        [_whole_vmem()] * 3, [shp] * 3, [], (rel_bias, *[jnp.asarray(t) for t in tables]), rider)


def _variant(qb, nq):
    return jnp.where(qb == 0, 0, jnp.where(qb == nq - 1, 2, 1))


def _win_start(qb):
    return pl.multiple_of(qb * Q_BLK, Q_BLK)


def _fill_padded(src_ref, pad_ref):
    L = src_ref.shape[0]
    edge = jnp.zeros((HALF_WIN, 128), pad_ref.dtype)
    pad_ref[0:HALF_WIN, :] = edge
    pad_ref[HALF_WIN:HALF_WIN + L, :] = src_ref[...]
    pad_ref[HALF_WIN + L:2 * HALF_WIN + L, :] = edge


INNER = {1: 1, 4: 1, 16: 4}


def _attn_layout(dil, L, blocks=ATTN_SUB):
    inner = INNER[dil]
    n_outer = dil // inner
    nsub = min(blocks // inner, L // Q_BLK)
    qt = nsub * Q_BLK
    grid = (4, n_outer, L // qt)
    qspec = pl.BlockSpec((inner, None, qt, 128), lambda hp, r, s: (0, r, s, hp))
    kspec = pl.BlockSpec((inner, None, L, 128), lambda hp, r, s: (0, r, 0, hp))
    bspec = pl.BlockSpec((3, 2, Q_BLK, K_WIN), lambda hp, r, s: (0, hp, 0, 0))
    kfspec = pl.BlockSpec((None, inner * L, 128), lambda hp, r, s: (r, 0, hp))
    qfspec = kfspec if inner > 1 else pl.BlockSpec((None, qt, 128), lambda hp, r, s: (r, s, hp))
    fshape = jax.ShapeDtypeStruct((n_outer, inner * L, D_ATTN), F32)
    view = lambda t: t.reshape(inner, n_outer, L, D_ATTN)

    def qrows(m, sub):
        if inner == 1:
            return (slice(sub * Q_BLK, (sub + 1) * Q_BLK), slice(None))
        first = (pl.program_id(2) * nsub + sub) * Q_BLK
        return (pl.ds(m + inner * first, Q_BLK, stride=inner), slice(None))

    def krows(m):
        if inner == 1:
            return (slice(None), slice(None))
        return (pl.ds(m, L, stride=inner), slice(None))

    return inner, nsub, grid, qspec, kspec, bspec, qfspec, kfspec, fshape, view, qrows, krows


def _head_masks():
    lane = lax.broadcasted_iota(jnp.int32, (Q_BLK, 128), 1)
    return lane < HEAD_DIM


def _attn_fwd(q, k, v, bias):
    dil, L, _ = q.shape
    nq = L // Q_BLK
    inner, nsub, grid, qspec, kspec, bspec, qfspec, kfspec, fshape, view, qrows, krows = _attn_layout(dil, L, ATTN_SUB_FWD)

    def body(q_ref, k_ref, v_ref, b_ref, o_ref, l_ref, kp, vp):
        step = pl.program_id(2)

        @pl.when(step == 0)
        def _():
            for m in range(inner):
                _fill_padded(k_ref.at[m], kp.at[m])
                _fill_padded(v_ref.at[m], vp.at[m])

        h0 = _head_masks()
        for m, sub in [(m, sub) for m in range(inner) for sub in range(nsub)]:
            qb = step * nsub + sub
            st = _win_start(qb)
            var = _variant(qb, nq)
            kw = kp[m, pl.ds(st, K_WIN), :]
            vw = vp[m, pl.ds(st, K_WIN), :]
            qs = q_ref[m, sub * Q_BLK:(sub + 1) * Q_BLK, :] * ATTN_SCALE
            zq = jnp.zeros_like(qs)
            q2 = jnp.concatenate([jnp.where(h0, qs, zq), jnp.where(h0, zq, qs)], axis=0)
            s = _dot_nt(q2, kw) + b_ref[var].reshape(2 * Q_BLK, K_WIN)
            top = jnp.max(s, axis=-1, keepdims=True)
            p = jnp.exp(s - top)
            l = jnp.sum(p, axis=-1, keepdims=True)
            out = _dot(p.astype(BF16), vw) / l
            lse = top + jnp.log(l)
            o_ref[qrows(m, sub)] = jnp.where(h0, out[0:Q_BLK], out[Q_BLK:2 * Q_BLK])
            l_ref[qrows(m, sub)] = jnp.where(h0, lse[0:Q_BLK], lse[Q_BLK:2 * Q_BLK])

    return pl.pallas_call(
        body, grid=grid, name=f"attn_fwd_d{dil}",
        in_specs=[qspec, kspec, kspec, bspec], out_specs=[qfspec, qfspec], out_shape=[fshape, fshape],
        scratch_shapes=[pltpu.VMEM((inner, L + 2 * HALF_WIN, 128), BF16)] * 2,
        compiler_params=_params(3),
    )(view(q), view(k), view(v), bias)


def _attn_bwd(q, k, v, bias, do, o, lse, rider=None):
    dil, L, _ = q.shape
    nq = L // Q_BLK
    inner, nsub, grid, qspec, kspec, bspec, qfspec, kfspec, fshape, view, qrows, krows = _attn_layout(dil, L)
    nstep = grid[2]

    def body(q_ref, k_ref, v_ref, b_ref, do_ref, o_ref, l_ref, dq_ref, dk_ref, dv_ref, db_ref, db_s,
             kp, vp, dkp, dvp, carry):
        hp, step = pl.program_id(0), pl.program_id(2)
        first = (hp == 0) & (pl.program_id(1) == 0) & (step == 0)
        last = (hp == grid[0] - 1) & (pl.program_id(1) == grid[1] - 1) & (step == nstep - 1)

        @pl.when(first)
        def _():
            db_s[...] = jnp.zeros_like(db_s)

        @pl.when(step == 0)
        def _():
            for m in range(inner):
                _fill_padded(k_ref.at[m], kp.at[m])
                _fill_padded(v_ref.at[m], vp.at[m])
            carry[...] = jnp.zeros_like(carry)

        h0 = _head_masks()
        for m, sub in [(m, sub) for m in range(inner) for sub in range(nsub)]:
            if sub == 0:
                carry_k, carry_v = carry[m, 0], carry[m, 1]
            qb = step * nsub + sub
            st = _win_start(qb)
            var = _variant(qb, nq)
            kw = kp[m, pl.ds(st, K_WIN), :]
            vw = vp[m, pl.ds(st, K_WIN), :]
            qs = q_ref[m, sub * Q_BLK:(sub + 1) * Q_BLK, :] * ATTN_SCALE
            dof = do_ref[qrows(m, sub)]
            dob = dof.astype(BF16)
            prod = dof * o_ref[qrows(m, sub)]
            lsev = l_ref[qrows(m, sub)]
            zq, zd = jnp.zeros_like(qs), jnp.zeros_like(dob)
            q2 = jnp.concatenate([jnp.where(h0, qs, zq), jnp.where(h0, zq, qs)], axis=0)
            do2 = jnp.concatenate([jnp.where(h0, dob, zd), jnp.where(h0, zd, dob)], axis=0)
            lse2 = jnp.concatenate([lsev[:, 0:1], lsev[:, HEAD_DIM:HEAD_DIM + 1]], axis=0)
            dd2 = jnp.concatenate([jnp.sum(jnp.where(h0, prod, 0.0), axis=-1, keepdims=True),
                                   jnp.sum(jnp.where(h0, 0.0, prod), axis=-1, keepdims=True)], axis=0)
            s = _dot_nt(q2, kw) + b_ref[var].reshape(2 * Q_BLK, K_WIN)
            p = jnp.exp(s - lse2)
            ds = p * (_dot_nt(do2, vw) - dd2)
            db_s[pl.ds(hp * 2, 2)] += ds.reshape(2, Q_BLK, K_WIN)
            dsb = ds.astype(BF16)
            dv_acc = _dot_tn(p.astype(BF16), do2)
            dk_acc = _dot_tn(dsb, q2)
            dq2 = _dot(dsb, kw) * ATTN_SCALE
            dq_ref[qrows(m, sub)] = jnp.where(h0, dq2[0:Q_BLK], dq2[Q_BLK:2 * Q_BLK]).astype(dq_ref.dtype)
            dkp[m, pl.ds(st, Q_BLK), :] = carry_k + dk_acc[0:Q_BLK]
            dvp[m, pl.ds(st, Q_BLK), :] = carry_v + dv_acc[0:Q_BLK]
            carry_k, carry_v = dk_acc[Q_BLK:K_WIN], dv_acc[Q_BLK:K_WIN]
            if sub == nsub - 1:
                carry[m, 0] = carry_k
                carry[m, 1] = carry_v

        @pl.when(step == nstep - 1)
        def _():
            for m in range(inner):
                dkp[m, L:L + Q_BLK, :] = carry[m, 0]
                dvp[m, L:L + Q_BLK, :] = carry[m, 1]
                dk_ref[krows(m)] = dkp[m, HALF_WIN:HALF_WIN + L, :].astype(dk_ref.dtype)
                dv_ref[krows(m)] = dvp[m, HALF_WIN:HALF_WIN + L, :].astype(dv_ref.dtype)

        @pl.when(last)
        def _():
            db_ref[...] = db_s[...]

    dbshape = (N_HEADS, Q_BLK, K_WIN)
    gshape = jax.ShapeDtypeStruct(fshape.shape, BF16 if inner == 1 else F32)
    return _call(
        body, f"attn_bwd_d{dil}", grid,
        [qspec, kspec, kspec, bspec, qfspec, qfspec, qfspec],
        [qfspec, kfspec, kfspec, _whole_vmem()],
        [gshape, gshape, gshape, jax.ShapeDtypeStruct(dbshape, F32)],
        [pltpu.VMEM(dbshape, F32)] + [pltpu.VMEM((inner, L + 2 * HALF_WIN, 128), BF16)] * 2
        + [pltpu.VMEM((inner, L + 2 * HALF_WIN, 128), F32)] * 2 + [pltpu.VMEM((inner, 2, Q_BLK, 128), F32)],
        (view(q), view(k), view(v), bias, do, o, lse), rider)


def _bucket_onehots(dil):
    m = np.zeros((3, K_WIN, N_BUCKETS), np.float32)
    for var, off in enumerate(_VARIANT_OFFSETS):
        for rel in range(-HALF_WIN, HALF_WIN + 1):
            col = (rel - off + Q_BLK - 1) % K_WIN
            m[var, col, int(_t5_bucket_np(np.asarray(rel * dil)))] = 1.0
    return jnp.asarray(m)


def _bias_grad(dbs):
    onehots = [_bucket_onehots(d) for d in DILATIONS]
    flip = jnp.asarray(np.eye(Q_BLK, dtype=np.float32)[::-1].copy())

    def body(d1, d2, d3, m1, m2, m3, flip_ref, out_ref):
        hp = lax.Precision.HIGHEST
        acc = jnp.zeros((N_HEADS, N_BUCKETS), F32)
        for d_ref, m_ref in ((d1, m1), (d2, m2), (d3, m3)):
            rows = []
            for h in range(N_HEADS):
                xrev = jnp.dot(flip_ref[...], d_ref[h], precision=hp, preferred_element_type=F32)
                y = pltpu.roll(xrev, 0, 1, stride=1, stride_axis=0)
                rows.append(jnp.sum(y, axis=0, keepdims=True))
            acc = acc + jnp.dot(jnp.concatenate(rows, axis=0), m_ref[1], precision=hp, preferred_element_type=F32)
        out_ref[...] = acc

    return pl.pallas_call(
        body, name="bias_grad", out_shape=jax.ShapeDtypeStruct((N_HEADS, N_BUCKETS), F32),
        compiler_params=_params(0),
    )(*dbs, *onehots, flip)


def _block_diag(w):
    eye = jnp.eye(N_RNN_BLOCKS, dtype=w.dtype)
    return jnp.einsum("ncd,nm->ncmd", w, eye).reshape(D_RNN, D_RNN).astype(BF16)


def _diag_blocks(dense):
    d = dense.reshape(N_RNN_BLOCKS, RNN_BLOCK, N_RNN_BLOCKS, RNN_BLOCK)
    return jnp.stack([d[n, :, n, :] for n in range(N_RNN_BLOCKS)])


EARLY = ("w_out", "w_up", "w_down")


def _local_step(x, target, p, shards=None):
    p = dict(p)
    first = None if shards is None else _gather_rider(["w_in"], [shards["w_in"]], shards["conv_w"])
    biases, got = _bias_mats(p["rel_bias"], first)
    if shards is not None:
        p["w_in"] = got[0]
        p["conv_w"] = jnp.transpose(got[1], (1, 0, 2)).reshape(4, D_RNN)
    lru = {}
    for dname in ("fwd", "bwd"):
        lru[dname] = (_block_diag(p["lru_wa_" + dname]), p["lru_ba_" + dname], _block_diag(p["lru_wx_" + dname]),
                      p["lru_bx_" + dname], p["lru_lam_" + dname])

    def gather(name):
        return None if shards is None else _gather_rider([name], [shards[name]])

    (xr, gate, *qkv), got = _inproj_fwd(x, p["attn_norm_g"], p["w_in"], gather("w_out"))
    p.update(zip(["w_out"], got))
    qs, ks, vs = qkv[0:3], qkv[3:6], qkv[6:9]
    (hf, xc), got = _rnn_fwd(xr, p["conv_w"], p["conv_b"], *lru["fwd"], reverse=False, rider=gather("w_up"))
    p.update(zip(["w_up"], got))
    (hb,), got = _rnn_fwd(xr, p["conv_w"], p["conv_b"], *lru["bwd"], reverse=True, rider=gather("w_down"), xc=xc)
    p.update(zip(["w_down"], got))
    outs, lses = [], []
    for q, k, v, bias in zip(qs, ks, vs, biases):
        o, l = _attn_fwd(q, k, v, bias)
        outs.append(o)
        lses.append(l)
    x1, mixb, *yl = _mix_fwd(outs, lses, hf, hb, gate, x, p["norm_rnn_g"], p["norm_attn_g"], p["w_out"])
    yas, lsts = [yl[0], yl[1], yl[1]], [yl[2], yl[3], yl[3]]
    dx1, h2b, a2b, dub, dx2b, loss_vec, dg_fin, dg_mlp = _mlp_fwd_bwd(
        x1, target, p["mlp_norm_g"], p["final_norm_g"], p["w_up"], p["w_down"])
    dhs, dgate, dya1, dya4, dw_out, dg_rnn, dg_attn = _mix_bwd(dx1, p["w_out"], mixb, yas[0], hf, hb, gate,
                                                               p["norm_rnn_g"], p["norm_attn_g"])
    dyas = [dya1, dya4, dya4]
    dw_up = _dw_matmul(h2b, dub, D_MODEL, FF_BLK, "dw_up")
    dw_down = _dw_matmul(a2b, dx2b, FF_BLK, D_MODEL, "dw_down")
    early = [dw_out, dw_up, dw_down]
    dqs, dks, dvs, dbs = [], [], [], []
    for i, (q, k, v, bias, dya, ya, lse) in enumerate(zip(qs, ks, vs, biases, dyas, yas, lsts)):
        rider = None
        if shards is not None:
            make = (lambda: _pair_exchange_rider(EARLY, early), lambda: _chip_exchange_rider(early),
                    lambda: _pair_share_rider(EARLY, early))[i]
            rider = make()
        (dq, dk, dv, db), got = _attn_bwd(q, k, v, bias, dya, ya, lse, rider)
        if shards is not None and i == 0:
            core = lax.axis_index("c").reshape(1).astype(jnp.int32)
            early = [_pair_add(core, g, o, "grad_pair_add_" + n) for n, g, o in zip(EARLY, early, got)]
        elif shards is not None and i == 1:
            early = [_chip_sum(t, "grad_chip_sum_" + n) for n, t in zip(EARLY, got)]
        elif shards is not None:
            early = got
        dqs.append(dq)
        dks.append(dk)
        dvs.append(dv)
        dbs.append(db)
    d_rel_bias = _bias_grad(dbs).T
    (dxc_f, dwa_f, dwx_f, dvec_f), _ = _rnn_bwd(xc, hf, dhs, *lru["fwd"], reverse=False)
    small = {
        "lru_wa_fwd": _diag_blocks(dwa_f), "lru_ba_fwd": dvec_f[0:1], "lru_wx_fwd": _diag_blocks(dwx_f),
        "lru_bx_fwd": dvec_f[1:2], "lru_lam_fwd": dvec_f[3:4],
        "rel_bias": d_rel_bias, "norm_rnn_g": dg_rnn, "norm_attn_g": dg_attn,
        "mlp_norm_g": dg_mlp, "final_norm_g": dg_fin,
    }
    loss_local = (0.5 / D_MODEL) * jnp.sum(loss_vec)
    rider = None
    if shards is not None:
        rider = _small_gather_rider(_pack([small[n].reshape(shp) for n, shp in SMALL if n in small]
                                          + [loss_local.reshape(1)]))
    (dxc_b, dwa_b, dwx_b, dvec_b), gathered = _rnn_bwd(xc, hb, dhs, *lru["bwd"], reverse=True, rider=rider)
    grad_x, dw_in, dg1, dconv = _inproj_bwd(x, dx1, xr, dxc_f, dxc_b, dgate, dqs, dks, dvs,
                                            p["attn_norm_g"], p["conv_w"], p["w_in"])[0]
    last = {"lru_wa_bwd": _diag_blocks(dwa_b), "lru_ba_bwd": dvec_b[0:1], "lru_wx_bwd": _diag_blocks(dwx_b),
            "lru_bx_bwd": dvec_b[1:2], "lru_lam_bwd": dvec_b[3:4],
            "attn_norm_g": dg1, "conv_w": dconv[0:4], "conv_b": dconv[4:5]}
    if shards is None:
        big = {"w_in": dw_in, "w_out": dw_out, "w_up": dw_up, "w_down": dw_down}
        return loss_local, grad_x, {**small, **last}, None, big, {}
    return loss_local, grad_x, last, gathered[0], {"w_in": dw_in}, dict(zip(EARLY, early))


BIG = ("w_in", "w_out", "w_up", "w_down")
BIG_SHARD = {"w_in": (D_MODEL, IN_BLK), "w_out": (OUT_BLK, D_MODEL), "w_up": (D_MODEL, FF_BLK), "w_down": (FF_BLK, D_MODEL)}
N_BIG = len(BIG)
N_CHIP_PEERS = 3
ANY = pl.BlockSpec(memory_space=pl.ANY)


def _place():
    x, y, c = lax.axis_index("x"), lax.axis_index("y"), lax.axis_index("c")
    chips = [(1 - x, y), (x, 1 - y), (1 - x, 1 - y)]
    return x, y, c, chips


def _remote(src, dst, send_sem, recv_sem, dev):
    return pltpu.make_async_remote_copy(src_ref=src, dst_ref=dst, send_sem=send_sem, recv_sem=recv_sem,
                                        device_id=dev, device_id_type=MESH)


def _staged_start(srcs, bufs, sems):
    legs = [pltpu.make_async_copy(s, b, sems.at[i]) for i, (s, b) in enumerate(zip(srcs, bufs))]
    for cp in legs:
        cp.start()
    return legs


def _staged_finish(legs, bufs, dsts, sems):
    out = []
    for i, (leg, b, d) in enumerate(zip(legs, bufs, dsts)):
        leg.wait()
        cp = pltpu.make_async_copy(b, d, sems.at[i])
        cp.start()
        out.append(cp)
    return out


class _Rider:
    def __init__(self, inputs, out_shape, scratch, first, late, last):
        self.inputs, self.out_shape, self.scratch = list(inputs), list(out_shape), list(scratch)
        self.first, self.late, self.last = first, late, last


def _call(body, name, grid, in_specs, out_specs, out_shape, scratch, operands, rider=None):
    n_grid = len(grid)
    if rider is None:
        res = pl.pallas_call(body, grid=grid, name=name, in_specs=in_specs, out_specs=out_specs, out_shape=out_shape,
                             scratch_shapes=scratch, compiler_params=_params(n_grid))(*operands)
        return list(res), []
    n_in, n_out, n_scr = len(in_specs), len(out_specs), len(scratch)
    ri, ro = len(rider.inputs), len(rider.out_shape)
    nsteps = int(np.prod(grid))
    late_step = max(nsteps - 3, 1)

    def wrapped(*refs):
        a, b = n_in, n_in + ri
        c, d = b + n_out, b + n_out + ro
        e = d + n_scr
        mine = refs[:a] + refs[b:c] + refs[d:e]
        theirs = (refs[a:b], refs[c:d], refs[e:])
        step = pl.program_id(0)
        for ax in range(1, n_grid):
            step = step * grid[ax] + pl.program_id(ax)
        pl.when(step == 0)(lambda: rider.first(*theirs))
        pl.when(step == late_step)(lambda: rider.late(*theirs))
        body(*mine)
        pl.when(step == nsteps - 1)(lambda: rider.last(*theirs))

    res = pl.pallas_call(
        wrapped, grid=grid, name=name, in_specs=list(in_specs) + [ANY] * ri, out_specs=list(out_specs) + [ANY] * ro,
        out_shape=list(out_shape) + rider.out_shape, scratch_shapes=list(scratch) + rider.scratch,
        compiler_params=_params(n_grid),
    )(*operands, *rider.inputs)
    return list(res[:n_out]), list(res[n_out:])


def _run_rider(rider, name):
    ri, ro = len(rider.inputs), len(rider.out_shape)

    def body(*refs):
        parts = (refs[:ri], refs[ri:ri + ro], refs[ri + ro:])
        rider.first(*parts)
        rider.late(*parts)
        rider.last(*parts)

    return list(pl.pallas_call(
        body, name=name, in_specs=[ANY] * ri, out_specs=[ANY] * ro, out_shape=rider.out_shape, scratch_shapes=rider.scratch,
        compiler_params=pltpu.CompilerParams(has_side_effects=True, vmem_limit_bytes=VMEM_LIMIT),
    )(*rider.inputs))


def _nothing(ins, outs, scr):
    return None


def _gather_rider(names, shards, conv_w=None):
    n = len(names)
    items = n + (conv_w is not None)
    halves = [BIG_SHARD[nm][0] // 2 for nm in names]

    def parts(ins, outs, scr):
        x, y, c, chips = _place()
        return x, y, c, chips, 2 * x + y, (x, y, 1 - c), scr[:8], scr[8:]

    def piece(outs, w, chip, core_half):
        return outs[w].at[chip, pl.ds(core_half * halves[w], halves[w])]

    def ici(ins, outs, sems, w, k, chip_xy, c, me):
        return _remote(ins[w].at[pl.ds(c * halves[w], halves[w])], piece(outs, w, me, c),
                       sems[0].at[w, k], sems[1].at[w, k], (*chip_xy, c))

    def first(ins, outs, scr):
        x, y, c, chips, me, sibling, sems, bufs = parts(ins, outs, scr)
        legs = _staged_start(ins, bufs, sems[6])
        for w in range(n):
            for k, chip_xy in enumerate(chips):
                ici(ins, outs, sems, w, k, chip_xy, c, me).start()
        if conv_w is not None:
            for k, (px, py) in enumerate(chips):
                _remote(ins[n], outs[n].at[me], sems[4].at[k], sems[5].at[k], (px, py, c)).start()
        _staged_finish(legs, bufs, [o.at[me] for o in outs], sems[7])

    def late(ins, outs, scr):
        x, y, c, chips, me, sibling, sems, bufs = parts(ins, outs, scr)
        for w in range(n):
            for k, (px, py) in enumerate(chips):
                landed = piece(outs, w, 2 * px + py, c)
                _remote(landed, landed, sems[0].at[w, k], sems[1].at[w, k], (px, py, c)).wait_recv()
                _remote(landed, landed, sems[2].at[w, k], sems[3].at[w, k], sibling).start()

    def last(ins, outs, scr):
        x, y, c, chips, me, sibling, sems, bufs = parts(ins, outs, scr)
        for w in range(n):
            for k, (px, py) in enumerate(chips):
                other = piece(outs, w, 2 * px + py, 1 - c)
                _remote(other, other, sems[2].at[w, k], sems[3].at[w, k], sibling).wait_recv()
        if conv_w is not None:
            for k, (px, py) in enumerate(chips):
                got = outs[n].at[2 * px + py]
                _remote(got, got, sems[4].at[k], sems[5].at[k], (px, py, c)).wait_recv()
                _remote(ins[n], outs[n].at[me], sems[4].at[k], sems[5].at[k], (px, py, c)).wait_send()
        for i in range(items):
            pltpu.make_async_copy(bufs[i], outs[i].at[me], sems[7].at[i]).wait()
        for w in range(n):
            for k, (px, py) in enumerate(chips):
                ici(ins, outs, sems, w, k, (px, py), c, me).wait_send()
                landed = piece(outs, w, 2 * px + py, c)
                _remote(landed, landed, sems[2].at[w, k], sems[3].at[w, k], sibling).wait_send()

    out_shape = [jax.ShapeDtypeStruct((N_SHARD,) + BIG_SHARD[nm], BF16) for nm in names]
    stage = [pltpu.VMEM(BIG_SHARD[nm], BF16) for nm in names]
    inputs = list(shards)
    if conv_w is not None:
        out_shape.append(jax.ShapeDtypeStruct((N_SHARD,) + conv_w.shape, F32))
        stage.append(pltpu.VMEM(conv_w.shape, F32))
        inputs.append(conv_w)
    scratch = ([pltpu.SemaphoreType.DMA((n, N_CHIP_PEERS))] * 4 + [pltpu.SemaphoreType.DMA((N_CHIP_PEERS,))] * 2
               + [pltpu.SemaphoreType.DMA((items,))] * 2 + stage)
    return _Rider(inputs, out_shape, scratch, first, late, last)


def _pair_exchange_rider(names, grads):
    def copies(ins, outs, scr):
        x, y, c, _ = _place()
        out = []
        for w, nm in enumerate(names):
            h = BIG_SHARD[nm][0] // 2
            out.append(_remote(ins[w].at[:, pl.ds((1 - c) * h, h), :], outs[w], scr[0].at[w], scr[1].at[w], (x, y, 1 - c)))
        return out

    def first(ins, outs, scr):
        for cp in copies(ins, outs, scr):
            cp.start()

    def last(ins, outs, scr):
        for cp in copies(ins, outs, scr):
            cp.wait()

    out_shape = [jax.ShapeDtypeStruct((N_SHARD, BIG_SHARD[nm][0] // 2, BIG_SHARD[nm][1]), F32) for nm in names]
    return _Rider(grads, out_shape, [pltpu.SemaphoreType.DMA((len(names),))] * 2, first, _nothing, last)


def _pair_add(core, grad, other, name):
    _, r, cols = grad.shape
    h = r // 2
    th = min(h, 256)
    per = h // th

    def body(c_ref, g_ref, o_ref, out_ref):
        out_ref[...] = (g_ref[...] + o_ref[...]).astype(BF16)

    return pl.pallas_call(
        body, name=name,
        grid_spec=pltpu.PrefetchScalarGridSpec(
            num_scalar_prefetch=1, grid=(N_SHARD, per),
            in_specs=[pl.BlockSpec((1, th, cols), lambda j, i, c_ref: (j, c_ref[0] * per + i, 0)),
                      pl.BlockSpec((1, th, cols), lambda j, i, c_ref: (j, i, 0))],
            out_specs=pl.BlockSpec((1, th, cols), lambda j, i, c_ref: (j, i, 0))),
        out_shape=jax.ShapeDtypeStruct((N_SHARD, h, cols), BF16),
        compiler_params=_params(2),
    )(core, grad, other)


def _chip_exchange_rider(parts):
    n = len(parts)

    def sends(ins, outs, scr):
        x, y, c, chips = _place()
        me = 2 * x + y
        return [_remote(ins[w].at[2 * px + py], outs[w].at[me], scr[0].at[w, k], scr[1].at[w, k], (px, py, c))
                for w in range(n) for k, (px, py) in enumerate(chips)]

    def first(ins, outs, scr):
        x, y, c, chips = _place()
        me = 2 * x + y
        legs = _staged_start([r.at[me] for r in ins], scr[4:], scr[2])
        for cp in sends(ins, outs, scr):
            cp.start()
        _staged_finish(legs, scr[4:], [o.at[me] for o in outs], scr[3])

    def last(ins, outs, scr):
        x, y, c, chips = _place()
        me = 2 * x + y
        for w in range(n):
            for k, (px, py) in enumerate(chips):
                got = outs[w].at[2 * px + py]
                _remote(got, got, scr[0].at[w, k], scr[1].at[w, k], (px, py, c)).wait_recv()
        for cp in sends(ins, outs, scr):
            cp.wait_send()
        for w in range(n):
            pltpu.make_async_copy(scr[4 + w], outs[w].at[me], scr[3].at[w]).wait()

    out_shape = [jax.ShapeDtypeStruct(p.shape, BF16) for p in parts]
    scratch = ([pltpu.SemaphoreType.DMA((n, N_CHIP_PEERS))] * 2 + [pltpu.SemaphoreType.DMA((n,))] * 2
               + [pltpu.VMEM(p.shape[1:], BF16) for p in parts])
    return _Rider(parts, out_shape, scratch, first, _nothing, last)


def _chip_sum(parts, name):
    _, h, cols = parts.shape
    th = min(h, 256)

    def body(p_ref, out_ref):
        acc = p_ref[0].astype(F32)
        for j in range(1, N_SHARD):
            acc = acc + p_ref[j].astype(F32)
        out_ref[...] = acc

    return pl.pallas_call(
        body, name=name, grid=(h // th,),
        in_specs=[pl.BlockSpec((N_SHARD, th, cols), lambda i: (0, i, 0))],
        out_specs=pl.BlockSpec((th, cols), lambda i: (i, 0)),
        out_shape=jax.ShapeDtypeStruct((h, cols), F32),
        compiler_params=_params(),
    )(parts)


def _pair_share_rider(names, halves):
    n = len(names)
    hs = [BIG_SHARD[nm][0] // 2 for nm in names]

    def mine(outs, c):
        return [outs[w].at[pl.ds(c * hs[w], hs[w])] for w in range(n)]

    def first(ins, outs, scr):
        x, y, c, _ = _place()
        legs = _staged_start(ins, scr[4:], scr[2])
        for w, dst in enumerate(mine(outs, c)):
            _remote(ins[w], dst, scr[0].at[w], scr[1].at[w], (x, y, 1 - c)).start()
        _staged_finish(legs, scr[4:], mine(outs, c), scr[3])

    def last(ins, outs, scr):
        x, y, c, _ = _place()
        for w, (theirs, dst) in enumerate(zip(mine(outs, 1 - c), mine(outs, c))):
            _remote(theirs, theirs, scr[0].at[w], scr[1].at[w], (x, y, 1 - c)).wait_recv()
            _remote(ins[w], dst, scr[0].at[w], scr[1].at[w], (x, y, 1 - c)).wait_send()
            pltpu.make_async_copy(scr[4 + w], dst, scr[3].at[w]).wait()

    out_shape = [jax.ShapeDtypeStruct(BIG_SHARD[nm], F32) for nm in names]
    scratch = [pltpu.SemaphoreType.DMA((n,))] * 4 + [pltpu.VMEM((h, BIG_SHARD[nm][1]), F32) for nm, h in zip(names, hs)]
    return _Rider(halves, out_shape, scratch, first, _nothing, last)


N_DEV = 8


def _all_peers(x, y, c):
    return [((1 - x) if fx else x, (1 - y) if fy else y, (1 - c) if fc else c)
            for fx in (0, 1) for fy in (0, 1) for fc in (0, 1) if fx or fy or fc]


def _small_gather_rider(vec):
    def sends(ins, outs, scr):
        x, y, c, _ = _place()
        me = 4 * x + 2 * y + c
        return [_remote(ins[0], outs[0].at[me], scr[0].at[k], scr[1].at[k], dev) for k, dev in enumerate(_all_peers(x, y, c))]

    def first(ins, outs, scr):
        x, y, c, _ = _place()
        legs = _staged_start(ins, scr[4:], scr[2])
        for cp in sends(ins, outs, scr):
            cp.start()
        _staged_finish(legs, scr[4:], [outs[0].at[4 * x + 2 * y + c]], scr[3])

    def last(ins, outs, scr):
        x, y, c, _ = _place()
        for k, (px, py, pc) in enumerate(_all_peers(x, y, c)):
            got = outs[0].at[4 * px + 2 * py + pc]
            _remote(got, got, scr[0].at[k], scr[1].at[k], (px, py, pc)).wait_recv()
        for cp in sends(ins, outs, scr):
            cp.wait_send()
        pltpu.make_async_copy(scr[4], outs[0].at[4 * x + 2 * y + c], scr[3].at[0]).wait()

    scratch = ([pltpu.SemaphoreType.DMA((N_DEV - 1,))] * 2 + [pltpu.SemaphoreType.DMA((1,))] * 2
               + [pltpu.VMEM(vec.shape, F32)])
    return _Rider([vec], [jax.ShapeDtypeStruct((N_DEV,) + vec.shape, F32)], scratch, first, _nothing, last)


def _sum_devices(gathered):
    def body(g_ref, out_ref):
        acc = g_ref[0]
        for j in range(1, N_DEV):
            acc = acc + g_ref[j]
        out_ref[...] = acc

    return pl.pallas_call(body, name="sum_devices", out_shape=jax.ShapeDtypeStruct(gathered.shape[1:], F32),
                          compiler_params=_params(0))(gathered)


def _allreduce_small(vec):
    rows = vec.shape[0]

    def body(v_ref, sum_ref, gat_ref, send, recv, loc_sem):
        x, y, c, chips = _place()
        sibling = (x, y, 1 - c)
        slot = lambda px, py, pc: gat_ref.at[4 * px + 2 * py + pc]
        lc = pltpu.make_async_copy(v_ref, slot(x, y, c), loc_sem)
        lc.start()
        sends = [_remote(v_ref, slot(x, y, c), send.at[0], recv.at[0], sibling)]
        sends += [_remote(v_ref, slot(x, y, c), send.at[1 + k], recv.at[1 + k], (px, py, c))
                  for k, (px, py) in enumerate(chips)]
        for cp in sends:
            cp.start()
        for k, (px, py) in enumerate(chips):
            got = slot(px, py, c)
            _remote(got, got, send.at[1 + k], recv.at[1 + k], (px, py, c)).wait_recv()
            cp = _remote(got, got, send.at[4 + k], recv.at[4 + k], sibling)
            cp.start()
            sends.append(cp)
        got = slot(x, y, 1 - c)
        _remote(got, got, send.at[0], recv.at[0], sibling).wait_recv()
        for k, (px, py) in enumerate(chips):
            got = slot(px, py, 1 - c)
            _remote(got, got, send.at[4 + k], recv.at[4 + k], sibling).wait_recv()
        for cp in sends:
            cp.wait_send()
        lc.wait()
        acc = gat_ref[0]
        for j in range(1, N_DEV):
            acc = acc + gat_ref[j]
        sum_ref[...] = acc

    total, _ = pl.pallas_call(
        body, name="allreduce_small",
        in_specs=[_whole_vmem()], out_specs=[_whole_vmem(), _whole_vmem()],
        out_shape=[jax.ShapeDtypeStruct((rows, 128), F32), jax.ShapeDtypeStruct((N_DEV, rows, 128), F32)],
        scratch_shapes=[pltpu.SemaphoreType.DMA((N_DEV - 1,))] * 2 + [pltpu.SemaphoreType.DMA(())],
        compiler_params=pltpu.CompilerParams(has_side_effects=True, vmem_limit_bytes=VMEM_LIMIT),
    )(vec)
    return total


def _adam_math(w_ref, g_ref, m_ref, v_ref, d_ref, m2_ref, v2_ref):
    c1 = 1.0 - ADAM_B1 ** ADAM_STEP
    c2 = 1.0 - ADAM_B2 ** ADAM_STEP
    gv = g_ref[...]
    m2 = ADAM_B1 * m_ref[...] + (1.0 - ADAM_B1) * gv
    v2 = ADAM_B2 * v_ref[...] + (1.0 - ADAM_B2) * (gv * gv)
    m2_ref[...] = m2
    v2_ref[...] = v2
    d_ref[...] = -ADAM_LR * ((m2 / c1) / (jnp.sqrt(v2 / c2) + ADAM_EPS) + ADAM_WD * w_ref[...])


def _adamw_many(ws, gs, ms, vs):
    n = len(ws)

    def body(*refs):
        for i in range(n):
            _adam_math(*[refs[k * n + i] for k in range(7)])

    shapes = [jax.ShapeDtypeStruct(w.shape, F32) for w in ws]
    res = pl.pallas_call(body, name="adamw_small", out_shape=shapes * 3, compiler_params=_params(0))(*ws, *gs, *ms, *vs)
    return res[:n], res[n:2 * n], res[2 * n:]


def _adamw(w, g, m, v, name):
    rows, cols = w.shape
    tr = 256 if rows % 256 == 0 else rows

    def body(w_ref, g_ref, m_ref, v_ref, d_ref, m2_ref, v2_ref):
        _adam_math(w_ref, g_ref, m_ref, v_ref, d_ref, m2_ref, v2_ref)

    spec = pl.BlockSpec((tr, cols), lambda i: (i, 0))
    shp = jax.ShapeDtypeStruct((rows, cols), F32)
    return pl.pallas_call(
        body, name=name, grid=(rows // tr,), in_specs=[spec] * 4, out_specs=[spec] * 3, out_shape=[shp] * 3,
        compiler_params=_params(),
    )(w, g, m, v)


SMALL = (
    ("attn_norm_g", (1, 1024)), ("conv_w", (1, 4, 512)), ("conv_b", (1, 512)),
    ("lru_wa_fwd", (1, 8, 64, 64)), ("lru_ba_fwd", (1, 512)), ("lru_wx_fwd", (1, 8, 64, 64)), ("lru_bx_fwd", (1, 512)),
    ("lru_lam_fwd", (1, 512)),
    ("lru_wa_bwd", (1, 8, 64, 64)), ("lru_ba_bwd", (1, 512)), ("lru_wx_bwd", (1, 8, 64, 64)), ("lru_bx_bwd", (1, 512)),
    ("lru_lam_bwd", (1, 512)),
    ("rel_bias", (32, 8)), ("norm_rnn_g", (1, 512)), ("norm_attn_g", (1, 512)), ("mlp_norm_g", (1, 1024)),
    ("final_norm_g", (1024,)),
)
PACK_ROW = 8 * 128


def _pack(parts):
    flat = jnp.concatenate([p.reshape(-1) for p in parts])
    pad = (-flat.shape[0]) % PACK_ROW
    return jnp.pad(flat, (0, pad)).reshape(-1, 128)


def _unpack(packed, shapes):
    flat = packed.reshape(-1)
    out, off = [], 0
    for shp in shapes:
        n = int(np.prod(shp))
        out.append(flat[off:off + n].reshape(shp))
        off += n
    return out


WEIGHT_ORDER = ("attn_norm_g", "w_in", "conv_w", "conv_b", "lru_wa_fwd", "lru_ba_fwd", "lru_wx_fwd", "lru_bx_fwd",
                "lru_lam_fwd", "lru_wa_bwd", "lru_ba_bwd", "lru_wx_bwd", "lru_bx_bwd", "lru_lam_bwd", "rel_bias",
                "norm_rnn_g", "norm_attn_g", "w_out", "mlp_norm_g", "w_up", "w_down", "final_norm_g")


def kernel(x, attn_norm_g, w_in, conv_w, conv_b, lru_wa_fwd, lru_ba_fwd, lru_wx_fwd, lru_bx_fwd, lru_lam_fwd, lru_wa_bwd, lru_ba_bwd, lru_wx_bwd, lru_bx_bwd, lru_lam_bwd, rel_bias, norm_rnn_g, norm_attn_g, w_out, mlp_norm_g, w_up, w_down, final_norm_g, loss_target, m_attn_norm_g, m_w_in, m_conv_w, m_conv_b, m_lru_wa_fwd, m_lru_ba_fwd, m_lru_wx_fwd, m_lru_bx_fwd, m_lru_lam_fwd, m_lru_wa_bwd, m_lru_ba_bwd, m_lru_wx_bwd, m_lru_bx_bwd, m_lru_lam_bwd, m_rel_bias, m_norm_rnn_g, m_norm_attn_g, m_w_out, m_mlp_norm_g, m_w_up, m_w_down, m_final_norm_g, v_attn_norm_g, v_w_in, v_conv_w, v_conv_b, v_lru_wa_fwd, v_lru_ba_fwd, v_lru_wx_fwd, v_lru_bx_fwd, v_lru_lam_fwd, v_lru_wa_bwd, v_lru_ba_bwd, v_lru_wx_bwd, v_lru_bx_bwd, v_lru_lam_bwd, v_rel_bias, v_norm_rnn_g, v_norm_attn_g, v_w_out, v_mlp_norm_g, v_w_up, v_w_down, v_final_norm_g):
    given = dict(locals())
    w = {n: given[n] for n in WEIGHT_ORDER}
    m = {n: given["m_" + n] for n in WEIGHT_ORDER}
    v = {n: given["v_" + n] for n in WEIGHT_ORDER}

    chip = lax.axis_index("x") * 2 + lax.axis_index("y")
    core = lax.axis_index("c")

    shards = {n: w[n][0].astype(BF16) for n in BIG}
    shards["conv_w"] = w["conv_w"][0]
    p = {n: (t[0] if t.ndim >= 3 else t) for n, t in w.items() if n not in BIG and n != "conv_w"}
    p["final_norm_g"] = w["final_norm_g"].reshape(1, D_MODEL)

    _, grad_x, small, gathered, big, reduced = _local_step(x[0], loss_target[0], p, shards)

    late = tuple(big)
    grads = [big[n] for n in late]
    others = _run_rider(_pair_exchange_rider(late, grads), "grad_pair_exchange")
    core_arr = core.reshape(1).astype(jnp.int32)
    parts = [_pair_add(core_arr, g, o, "grad_pair_add_" + n) for n, g, o in zip(late, grads, others)]
    landed = _run_rider(_chip_exchange_rider(parts), "grad_chip_exchange")
    halves = [_chip_sum(t, "grad_chip_sum_" + n) for n, t in zip(late, landed)]
    reduced.update(zip(late, _run_rider(_pair_share_rider(late, halves), "grad_pair_share")))

    early_small = [(n, shp) for n, shp in SMALL if n not in small]
    late_small = [(n, shp) for n, shp in SMALL if n in small]
    *early_g, loss = _unpack(_sum_devices(gathered), [shp for _, shp in early_small] + [(1,)])
    late_g = _unpack(_allreduce_small(_pack([small[n].reshape(shp) for n, shp in late_small])),
                     [shp for _, shp in late_small])
    g = dict(zip([n for n, _ in early_small + late_small], early_g + late_g))
    g["conv_w"] = lax.dynamic_slice_in_dim(g["conv_w"], chip * (D_RNN // N_SHARD), D_RNN // N_SHARD, axis=2)
    for n in BIG:
        g[n] = reduced[n][None]

    delta, new_m, new_v = {}, {}, {}
    for n in BIG:
        d2, m2, v2 = _adamw(w[n][0], reduced[n], m[n][0], v[n][0], "adamw_" + n)
        delta[n], new_m[n], new_v[n] = d2[None], m2[None], v2[None]
    names = [n for n, _ in SMALL]
    for dst, src in zip((delta, new_m, new_v), _adamw_many(*[[t[n] for n in names] for t in (w, g, m, v)])):
        dst.update(dict(zip(names, src)))

    return (loss.reshape(()), grad_x[None], *[g[n] for n in WEIGHT_ORDER], *[delta[n] for n in WEIGHT_ORDER],
            *[new_m[n] for n in WEIGHT_ORDER], *[new_v[n] for n in WEIGHT_ORDER])
```

```python
import functools
import math

import numpy as np
import jax
import jax.numpy as jnp
from jax import lax
from jax.experimental import pallas as pl
from jax.experimental.pallas import tpu as pltpu

F32 = jnp.float32
BF16 = jnp.bfloat16

D_MODEL = 1024
D_RNN = 512
D_ATTN = 512
N_HEADS = 8
HEAD_DIM = 64
N_RNN_BLOCKS = 8
RNN_BLOCK = 64
D_IN = 2 * D_RNN + 3 * D_ATTN
D_FF = 4 * D_MODEL
N_SHARD = 4
IN_BLK = D_IN // N_SHARD
OUT_BLK = D_MODEL // N_SHARD
FF_BLK = D_FF // N_SHARD
EPS = 1e-6
NEG_INF = -1e30
LRU_C = 8.0
DILATIONS = (1, 4, 16)
F32_LAYOUT = 4
HALF_WIN = 64
Q_BLK = 128
K_WIN = 256
N_BUCKETS = 32
MAX_DISTANCE = 1024
ATTN_SCALE = HEAD_DIM ** -0.5

ADAM_LR = 0.001
ADAM_B1 = 0.9
ADAM_B2 = 0.999
ADAM_EPS = 1e-08
ADAM_WD = 0.01
ADAM_STEP = 10

TS = 512
TS_RNN_BWD = 1024
TS_MLP = 256
TS_INPROJ_BWD = 512
ATTN_SUB = 32
ATTN_SUB_FWD = 32
TK_DW = 4096
SCAN_UNROLL = 8
SUB = 8
VMEM_LIMIT = 56 * 1024 * 1024
GELU_C0 = math.sqrt(2.0 / math.pi)
GELU_C1 = 0.044715

MESH = pl.DeviceIdType.MESH


def _params(n_grid=1):
    return pltpu.CompilerParams(vmem_limit_bytes=VMEM_LIMIT, dimension_semantics=("arbitrary",) * n_grid)


def _whole_vmem():
    return pl.BlockSpec(memory_space=pltpu.VMEM)


def _rows(width, tile=TS):
    return pl.BlockSpec((tile, width), lambda i: (i, 0))


def _sigmoid(z):
    return 0.5 * jnp.tanh(0.5 * z) + 0.5


def _log1p(u):
    w = 1.0 + u
    return jnp.where(w == 1.0, u, jnp.log(w) * (u / (w - 1.0)))


def _softplus(z):
    return jnp.maximum(z, 0.0) + _log1p(jnp.exp(-jnp.abs(z)))


def _gelu_parts(g):
    inner = GELU_C0 * (g + GELU_C1 * g * g * g)
    t = jnp.tanh(inner)
    val = 0.5 * g * (1.0 + t)
    dinner = GELU_C0 * (1.0 + 3.0 * GELU_C1 * g * g)
    grad = 0.5 * (1.0 + t) + 0.5 * g * (1.0 - t * t) * dinner
    return val, grad


def _rms(x):
    rstd = lax.rsqrt(jnp.mean(x * x, axis=-1, keepdims=True) + EPS)
    return rstd, x * rstd


def _rms_bwd(dy, g, xhat, rstd):
    dxh = dy * g
    dx = rstd * (dxh - xhat * jnp.mean(dxh * xhat, axis=-1, keepdims=True))
    dg = jnp.sum(dy * xhat, axis=0, keepdims=True)
    return dx, dg


def _dot(a, b):
    return jnp.dot(a, b, preferred_element_type=F32)


def _dot_nt(a, b):
    return lax.dot_general(a, b, (((1,), (1,)), ((), ())), preferred_element_type=F32)


def _dot_tn(a, b):
    return lax.dot_general(a, b, (((0,), (0,)), ((), ())), preferred_element_type=F32)


def _shifted(tile, prev8, next8, k):
    n = tile.shape[0]
    row = lax.broadcasted_iota(jnp.int32, tile.shape, 0)
    if k == 0:
        return tile
    if k < 0:
        r = pltpu.roll(tile, -k, 0)
        for j in range(-k):
            r = jnp.where(row == j, prev8[SUB + j + k:SUB + j + k + 1, :], r)
        return r
    r = pltpu.roll(tile, n - k, 0)
    for j in range(k):
        r = jnp.where(row == n - k + j, next8[j:j + 1, :], r)
    return r


def _to_lane_blocks(val, s_ref):
    for j in range(val.shape[1] // 128):
        s_ref[j] = val[:, j * 128:(j + 1) * 128]


def _from_lane_blocks(s_ref):
    return jnp.concatenate([s_ref[j] for j in range(s_ref.shape[0])], axis=-1)


def _class_rows(s_ref, r, dil):
    n = s_ref.shape[1] // dil
    return jnp.concatenate([s_ref[j, pl.ds(r, n, stride=dil), :] for j in range(s_ref.shape[0])], axis=-1)


def _split_classes(val, s_ref, out_ref, dil):
    _to_lane_blocks(val, s_ref)
    for r in range(dil):
        out_ref[r] = _class_rows(s_ref, r, dil).astype(out_ref.dtype)


def _merge_classes(in_ref, s_ref, dil, also_ref=None):
    n = s_ref.shape[1] // dil
    for r in range(dil):
        v = in_ref[r].astype(F32)
        if also_ref is not None:
            v = v + also_ref[r].astype(F32)
        for j in range(s_ref.shape[0]):
            s_ref[j, pl.ds(r, n, stride=dil), :] = v[:, j * 128:(j + 1) * 128]
    return _from_lane_blocks(s_ref)


def _class_spec(dil, tile=TS):
    return pl.BlockSpec((dil, tile // dil, 512), lambda i: (0, i, 0))


def _class_shape(S, dil, dtype):
    return jax.ShapeDtypeStruct((dil, S // dil, 512), dtype)


def _scan_tile(a_ref, b_ref, h_ref, carry_ref, reverse):
    n = a_ref.shape[0]
    width = a_ref.shape[1]
    groups = n // SUB
    row = lax.broadcasted_iota(jnp.int32, (SUB, width), 0)

    def group_scan(g):
        r0 = pl.multiple_of(g * SUB, SUB)
        a = a_ref[pl.ds(r0, SUB), :]
        b = b_ref[pl.ds(r0, SUB), :]
        for s in (1, 2, 4):
            if reverse:
                a_sh = pltpu.roll(a, SUB - s, 0)
                b_sh = pltpu.roll(b, SUB - s, 0)
                m = row < SUB - s
            else:
                a_sh = pltpu.roll(a, s, 0)
                b_sh = pltpu.roll(b, s, 0)
                m = row >= s
            b = jnp.where(m, a * b_sh + b, b)
            a = jnp.where(m, a * a_sh, a)
        return r0, a, b

    def step(i, carry):
        first = i * SCAN_UNROLL
        order = [(groups - 1 - (first + u)) if reverse else (first + u) for u in range(SCAN_UNROLL)]
        scans = [group_scan(g) for g in order]
        for r0, a, b in scans:
            h = b + a * carry
            h_ref[pl.ds(r0, SUB), :] = h
            edge = h[0:1, :] if reverse else h[SUB - 1:SUB, :]
            carry = jnp.broadcast_to(edge, (SUB, width))
        return carry

    carry_ref[...] = lax.fori_loop(0, groups // SCAN_UNROLL, step, carry_ref[...])


def _conv_fwd(xr, prev8, next8, cw, cb):
    y = cb + _shifted(xr, prev8, next8, -2) * cw[0:1, :]
    y = y + _shifted(xr, prev8, next8, -1) * cw[1:2, :]
    y = y + xr * cw[2:3, :]
    y = y + _shifted(xr, prev8, next8, 1) * cw[3:4, :]
    return y


def _lru_gates(xc, wa_ref, ba, wx_ref, bx, lam):
    xcb = xc.astype(BF16)
    r = _sigmoid(_dot(xcb, wa_ref[...]) + ba)
    i = _sigmoid(_dot(xcb, wx_ref[...]) + bx)
    cl = -LRU_C * _softplus(-lam)
    la = cl * r
    a = jnp.exp(la)
    m2 = -jnp.tanh(la) * (a * a + 1.0)
    inv = jnp.where(m2 > 0.0, lax.rsqrt(m2), 0.0)
    mult = m2 * inv
    return xcb, r, i, cl, a, mult, inv


def _inproj_fwd(x, g1, w_in, rider=None):
    S = x.shape[0]

    def body(x_ref, g_ref, w_ref, xr_ref, gate_ref, *rest):
        qkv_refs, s_ref, s4_ref, w_full = rest[:9], rest[9], rest[10], rest[11]

        @pl.when(pl.program_id(0) == 0)
        def _():
            for j in range(N_SHARD):
                w_full[:, j * IN_BLK:(j + 1) * IN_BLK] = w_ref[j]

        _, xh = _rms(x_ref[...])
        h = (xh * g_ref[...]).astype(BF16)
        proj = _dot(h, w_full[...])
        xr_ref[...] = proj[:, 0:512]
        gate_ref[...] = proj[:, 512:1024]
        for t in range(3):
            val = proj[:, 1024 + 512 * t:1536 + 512 * t]
            d1_ref, d4_ref, d16_ref = qkv_refs[3 * t:3 * t + 3]
            d1_ref[0] = val.astype(BF16)
            _to_lane_blocks(val, s_ref)
            for r4 in range(4):
                c4 = _class_rows(s_ref, r4, 4)
                d4_ref[r4] = c4.astype(BF16)
                _to_lane_blocks(c4, s4_ref.at[r4])
            for r4 in range(4):
                for m in range(4):
                    d16_ref[r4 + 4 * m] = _class_rows(s4_ref.at[r4], m, 4).astype(BF16)

    f = jax.ShapeDtypeStruct((S, 512), F32)
    return _call(
        body, "inproj_fwd", (S // TS,),
        [_rows(D_MODEL), _whole_vmem(), _whole_vmem()],
        [_rows(512)] * 2 + [_class_spec(d) for d in DILATIONS] * 3,
        [f, f] + [_class_shape(S, d, BF16) for d in DILATIONS] * 3,
        [pltpu.VMEM((4, TS, 128), F32), pltpu.VMEM((4, 4, TS // 4, 128), F32), pltpu.VMEM((D_MODEL, D_IN), BF16)],
        (x, g1, w_in), rider)


def _halo_specs(S, order, tile=TS):
    per = tile // SUB
    last = S // SUB - 1
    return [
        pl.BlockSpec((tile, 512), lambda i: (order(i), 0)),
        pl.BlockSpec((SUB, 512), lambda i: (jnp.maximum(order(i) * per - 1, 0), 0)),
        pl.BlockSpec((SUB, 512), lambda i: (jnp.minimum((order(i) + 1) * per, last), 0)),
    ]


def _rnn_fwd(xr, conv_w, conv_b, wa, ba, wx, bx, lam, reverse, rider=None, xc=None):
    S = xr.shape[0]
    nt = S // TS
    order = (lambda i: nt - 1 - i) if reverse else (lambda i: i)
    with_conv = xc is None
    n_x = 5 if with_conv else 1
    tile = pl.BlockSpec((TS, 512), lambda i: (order(i), 0))

    def body(*refs):
        wa_ref, ba_ref, wx_ref, bx_ref, lam_ref, h_ref = refs[n_x:n_x + 6]
        a_s, b_s, carry = refs[-3:]
        i = pl.program_id(0)
        t = order(i)

        @pl.when(i == 0)
        def _():
            carry[...] = jnp.zeros_like(carry)

        if with_conv:
            x_ref, xp_ref, xn_ref, cw_ref, cb_ref = refs[:5]
            prev8 = jnp.where(t > 0, xp_ref[...], 0.0)
            next8 = jnp.where(t < nt - 1, xn_ref[...], 0.0)
            xcv = _conv_fwd(x_ref[...], prev8, next8, cw_ref[...], cb_ref[...])
            refs[n_x + 6][...] = xcv
        else:
            xcv = refs[0][...]
        _, _, gi, _, a, mult, _ = _lru_gates(xcv, wa_ref, ba_ref[...], wx_ref, bx_ref[...], lam_ref[...])
        a_s[...] = a
        b_s[...] = mult * (gi * xcv)
        _scan_tile(a_s, b_s, h_ref, carry, reverse)

    f512 = jax.ShapeDtypeStruct((S, 512), F32)
    return _call(
        body, "rnn_fwd_rev" if reverse else "rnn_fwd_fwd", (nt,),
        (_halo_specs(S, order) + [_whole_vmem()] * 2 if with_conv else [tile]) + [_whole_vmem()] * 5,
        [tile, tile] if with_conv else [tile], [f512, f512] if with_conv else [f512],
        [pltpu.VMEM((TS, 512), F32), pltpu.VMEM((TS, 512), F32), pltpu.VMEM((SUB, 512), F32)],
        ((xr, xr, xr, conv_w, conv_b) if with_conv else (xc,)) + (wa, ba, wx, bx, lam), rider)


def _mix_fwd(o3, l3, hf, hb, gate, x, g_rnn, g_attn, w_out):
    S = x.shape[0]

    def body(o1, o2, o3_, l1, l2, l3_, hf_ref, hb_ref, gate_ref, x_ref, gr_ref, ga_ref, w_ref,
             x1_ref, mix_ref, ya1, ya2, ls1, ls2, s_ref):
        la, lb, lc = l1[0], _merge_classes(l2, s_ref, F32_LAYOUT), _merge_classes(l3_, s_ref, F32_LAYOUT)
        m = jnp.maximum(jnp.maximum(la, lb), lc)
        ea, eb, ec = jnp.exp(la - m), jnp.exp(lb - m), jnp.exp(lc - m)
        den = ea + eb + ec
        lse = m + jnp.log(den)
        ya = (ea * o1[0] + eb * _merge_classes(o2, s_ref, F32_LAYOUT) + ec * _merge_classes(o3_, s_ref, F32_LAYOUT)) / den
        ya1[0] = ya
        ls1[0] = lse
        _split_classes(ya, s_ref, ya2, F32_LAYOUT)
        _split_classes(lse, s_ref, ls2, F32_LAYOUT)
        gg, _ = _gelu_parts(gate_ref[...])
        yr = (hf_ref[...] + hb_ref[...]) * gg
        _, xh_r = _rms(yr)
        _, xh_a = _rms(ya)
        mix = jnp.concatenate([xh_r * gr_ref[...], xh_a * ga_ref[...]], axis=-1).astype(BF16)
        mix_ref[...] = mix
        acc = x_ref[...]
        for j in range(N_SHARD):
            acc = acc + _dot(mix[:, j * OUT_BLK:(j + 1) * OUT_BLK], w_ref[j])
        x1_ref[...] = acc

    one, four = _class_spec(1), _class_spec(F32_LAYOUT)
    return pl.pallas_call(
        body, grid=(S // TS,), name="mix_fwd",
        in_specs=[one, four, four] * 2 + [_rows(512)] * 3 + [_rows(D_MODEL)] + [_whole_vmem()] * 3,
        out_specs=[_rows(D_MODEL), _rows(D_MODEL)] + [one, four] * 2,
        out_shape=[jax.ShapeDtypeStruct((S, D_MODEL), F32), jax.ShapeDtypeStruct((S, D_MODEL), BF16)]
        + [_class_shape(S, 1, F32), _class_shape(S, F32_LAYOUT, F32)] * 2,
        scratch_shapes=[pltpu.VMEM((4, TS, 128), F32)],
        compiler_params=_params(),
    )(*o3, *l3, hf, hb, gate, x, g_rnn, g_attn, w_out)


def _mlp_fwd_bwd(x1, target, g_mlp, g_fin, w_up, w_down):
    S = x1.shape[0]
    tm = TS_MLP

    def body(x1_ref, t_ref, gm_ref, gf_ref, wu_ref, wd_ref,
             dx1_ref, h2_ref, a2_ref, du_ref, dx2_ref, loss_ref, dgf_ref, dgm_ref, relu_s):
        @pl.when(pl.program_id(0) == 0)
        def _():
            loss_ref[...] = jnp.zeros_like(loss_ref)
            dgf_ref[...] = jnp.zeros_like(dgf_ref)
            dgm_ref[...] = jnp.zeros_like(dgm_ref)

        x1v = x1_ref[...]
        rstd1, xh1 = _rms(x1v)
        h2 = (xh1 * gm_ref[...]).astype(BF16)
        h2_ref[...] = h2
        x2 = x1v
        for j in range(N_SHARD):
            r = jnp.maximum(_dot(h2, wu_ref[j]), 0.0)
            relu_s[j] = r
            a2 = (r * r).astype(BF16)
            a2_ref[:, j * FF_BLK:(j + 1) * FF_BLK] = a2
            x2 = x2 + _dot(a2, wd_ref[j])
        rstd2, xh2 = _rms(x2)
        err = xh2 * gf_ref[...] - t_ref[...]
        loss_ref[...] += jnp.sum(err * err, axis=0, keepdims=True)
        dy = err * (1.0 / D_MODEL)
        dx2, dgf = _rms_bwd(dy, gf_ref[...], xh2, rstd2)
        dgf_ref[...] += dgf
        dx2b = dx2.astype(BF16)
        dx2_ref[...] = dx2b
        dh2 = jnp.zeros((tm, D_MODEL), F32)
        for j in range(N_SHARD):
            du = (_dot_nt(dx2b, wd_ref[j]) * (2.0 * relu_s[j])).astype(BF16)
            du_ref[:, j * FF_BLK:(j + 1) * FF_BLK] = du
            dh2 = dh2 + _dot_nt(du, wu_ref[j])
        dx1n, dgm = _rms_bwd(dh2, gm_ref[...], xh1, rstd1)
        dgm_ref[...] += dgm
        dx1_ref[...] = dx2 + dx1n

    vec = jax.ShapeDtypeStruct((1, D_MODEL), F32)
    return pl.pallas_call(
        body, grid=(S // tm,), name="mlp_fwd_bwd",
        in_specs=[_rows(D_MODEL, tm), _rows(D_MODEL, tm)] + [_whole_vmem()] * 4,
        out_specs=[_rows(D_MODEL, tm), _rows(D_MODEL, tm), _rows(D_FF, tm), _rows(D_FF, tm), _rows(D_MODEL, tm)]
        + [_whole_vmem()] * 3,
        out_shape=[jax.ShapeDtypeStruct((S, D_MODEL), F32), jax.ShapeDtypeStruct((S, D_MODEL), BF16),
                   jax.ShapeDtypeStruct((S, D_FF), BF16), jax.ShapeDtypeStruct((S, D_FF), BF16),
                   jax.ShapeDtypeStruct((S, D_MODEL), BF16), vec, vec, vec],
        scratch_shapes=[pltpu.VMEM((N_SHARD, tm, FF_BLK), F32)],
        compiler_params=_params(),
    )(x1, target, g_mlp, g_fin, w_up, w_down)


def _mix_bwd(dx1, w_out, mixb, ya, hf, hb, gate, g_rnn, g_attn):
    S = dx1.shape[0]

    def body(dx1_ref, w_ref, mix_ref, ya_ref, hf_ref, hb_ref, gate_ref, gr_ref, ga_ref,
             dhs_ref, dgate_ref, dya1, dya2, dw_ref, dgr_ref, dga_ref, s_ref):
        @pl.when(pl.program_id(0) == 0)
        def _():
            dw_ref[...] = jnp.zeros_like(dw_ref)
            dgr_ref[...] = jnp.zeros_like(dgr_ref)
            dga_ref[...] = jnp.zeros_like(dga_ref)

        dx1b = dx1_ref[...].astype(BF16)
        mix = mix_ref[...]
        for j in range(N_SHARD):
            dw_ref[j] += _dot_tn(mix[:, j * OUT_BLK:(j + 1) * OUT_BLK], dx1b)
        dmix = jnp.concatenate([_dot_nt(dx1b, w_ref[j]) for j in range(N_SHARD)], axis=-1)
        gg, dgg = _gelu_parts(gate_ref[...])
        hs = hf_ref[...] + hb_ref[...]
        rstd_r, xh_r = _rms(hs * gg)
        dyr, dgr = _rms_bwd(dmix[:, 0:D_RNN], gr_ref[...], xh_r, rstd_r)
        dgr_ref[...] += dgr
        rstd_a, xh_a = _rms(ya_ref[0])
        dya, dga = _rms_bwd(dmix[:, D_RNN:], ga_ref[...], xh_a, rstd_a)
        dga_ref[...] += dga
        dya1[0] = dya
        _split_classes(dya, s_ref, dya2, F32_LAYOUT)
        dhs_ref[...] = dyr * gg
        dgate_ref[...] = dyr * hs * dgg

    f512 = jax.ShapeDtypeStruct((S, 512), F32)
    vec = jax.ShapeDtypeStruct((1, 512), F32)
    return pl.pallas_call(
        body, grid=(S // TS,), name="mix_bwd",
        in_specs=[_rows(D_MODEL), _whole_vmem(), _rows(D_MODEL), _class_spec(1)] + [_rows(512)] * 3 + [_whole_vmem()] * 2,
        out_specs=[_rows(512)] * 2 + [_class_spec(1), _class_spec(F32_LAYOUT)] + [_whole_vmem()] * 3,
        out_shape=[f512, f512, _class_shape(S, 1, F32), _class_shape(S, F32_LAYOUT, F32),
                   jax.ShapeDtypeStruct((N_SHARD, OUT_BLK, D_MODEL), F32), vec, vec],
        scratch_shapes=[pltpu.VMEM((4, TS, 128), F32)],
        compiler_params=_params(),
    )(dx1, w_out, mixb, ya, hf, hb, gate, g_rnn, g_attn)


def _rnn_bwd(xc, h, dhs, wa, ba, wx, bx, lam, reverse, rider=None):
    S = xc.shape[0]
    nt = S // TS_RNN_BWD
    order = (lambda i: i) if reverse else (lambda i: nt - 1 - i)
    per = TS_RNN_BWD // SUB
    last = S // SUB - 1
    if reverse:
        h_halo = pl.BlockSpec((SUB, 512), lambda i: (jnp.minimum((order(i) + 1) * per, last), 0))
    else:
        h_halo = pl.BlockSpec((SUB, 512), lambda i: (jnp.maximum(order(i) * per - 1, 0), 0))
    tile = pl.BlockSpec((TS_RNN_BWD, 512), lambda i: (order(i), 0))

    def body(xc_ref, h_ref, hh_ref, dh_ref, wa_ref, ba_ref, wx_ref, bx_ref, lam_ref,
             dxc_ref, dwa_ref, dwx_ref, dvec_ref, a_s, g_s, carry, edge):
        i = pl.program_id(0)
        t = order(i)

        @pl.when(i == 0)
        def _():
            carry[...] = jnp.zeros_like(carry)
            edge[...] = jnp.zeros_like(edge)
            dwa_ref[...] = jnp.zeros_like(dwa_ref)
            dwx_ref[...] = jnp.zeros_like(dwx_ref)
            dvec_ref[...] = jnp.zeros_like(dvec_ref)

        xc = xc_ref[...]
        xcb, r, gi, cl, a, mult, inv_mult = _lru_gates(xc, wa_ref, ba_ref[...], wx_ref, bx_ref[...], lam_ref[...])
        hv = h_ref[...]
        if reverse:
            a_s[...] = _shifted(a, edge[...], None, -1)
            edge[...] = a[TS_RNN_BWD - SUB:TS_RNN_BWD, :]
            hh = jnp.where(t < nt - 1, hh_ref[...], 0.0)
            h_prev = _shifted(hv, None, hh, 1)
        else:
            a_s[...] = _shifted(a, None, edge[...], 1)
            edge[...] = a[0:SUB, :]
            hh = jnp.where(t > 0, hh_ref[...], 0.0)
            h_prev = _shifted(hv, hh, None, -1)
        _scan_tile(a_s, dh_ref, g_s, carry, not reverse)
        g = g_s[...]
        da = g * h_prev
        gm = g * mult
        d_i = gm * xc
        dmult = g * gi * xc
        dla = da * a - dmult * (a * a) * inv_mult
        d_r = dla * cl
        dpre_r = d_r * r * (1.0 - r)
        dpre_i = d_i * gi * (1.0 - gi)
        dprb = dpre_r.astype(BF16)
        dpib = dpre_i.astype(BF16)
        dwa_ref[...] += _dot_tn(xcb, dprb)
        dwx_ref[...] += _dot_tn(xcb, dpib)
        dvec_ref[0:1, :] += jnp.sum(dpre_r, axis=0, keepdims=True)
        dvec_ref[1:2, :] += jnp.sum(dpre_i, axis=0, keepdims=True)
        dvec_ref[2:3, :] += jnp.sum(dla * r, axis=0, keepdims=True)
        dvec_ref[3:4, :] = dvec_ref[2:3, :] * (LRU_C * _sigmoid(-lam_ref[...]))
        dxc_ref[...] = gm * gi + _dot_nt(dprb, wa_ref[...]) + _dot_nt(dpib, wx_ref[...])

    sq = jax.ShapeDtypeStruct((D_RNN, D_RNN), F32)
    return _call(
        body, "rnn_bwd_rev" if reverse else "rnn_bwd_fwd", (nt,),
        [tile, tile, h_halo, tile] + [_whole_vmem()] * 5,
        [tile, _whole_vmem(), _whole_vmem(), _whole_vmem()],
        [jax.ShapeDtypeStruct((S, 512), F32), sq, sq, jax.ShapeDtypeStruct((SUB, 512), F32)],
        [pltpu.VMEM((TS_RNN_BWD, 512), F32), pltpu.VMEM((TS_RNN_BWD, 512), F32), pltpu.VMEM((SUB, 512), F32),
         pltpu.VMEM((SUB, 512), F32)],
        (xc, h, h, dhs, wa, ba, wx, bx, lam), rider)


def _inproj_bwd(x, dx1, xr, dxc_f, dxc_b, dgate, dq3, dk3, dv3, g1, conv_w, w_in, rider=None):
    S = x.shape[0]
    tb = TS_INPROJ_BWD
    nt = S // tb
    ident = lambda i: i

    def body(x_ref, dx1_ref, xr_ref, xrp_ref, xrn_ref, cf_ref, cfp_ref, cfn_ref, cb_ref, cbp_ref, cbn_ref, dgate_ref,
             dq1, dq2, dq3_, dk1, dk2, dk3_, dv1, dv2, dv3_, g_ref, cw_ref, w_ref,
             dx_ref, dw_ref, dg_ref, dcw_ref, s_ref, w_full, dw_full, sems):
        i = pl.program_id(0)

        def blocks(full, blocked, k0):
            return [(full.at[:, j * IN_BLK:(j + 1) * IN_BLK], blocked.at[j], sems.at[k0 + j]) for j in range(N_SHARD)]

        @pl.when(i == 0)
        def _():
            copies = [pltpu.make_async_copy(src, dst, sem) for dst, src, sem in blocks(w_full, w_ref, 0)]
            for cp in copies:
                cp.start()
            for cp in copies:
                cp.wait()
            dw_full[...] = jnp.zeros_like(dw_full)
            dg_ref[...] = jnp.zeros_like(dg_ref)
            dcw_ref[...] = jnp.zeros_like(dcw_ref)

        first, last = i > 0, i < nt - 1
        dxc = cf_ref[...] + cb_ref[...]
        dxc_p = jnp.where(first, cfp_ref[...] + cbp_ref[...], 0.0)
        dxc_n = jnp.where(last, cfn_ref[...] + cbn_ref[...], 0.0)
        cw = cw_ref[...]
        dxr = (_shifted(dxc, dxc_p, dxc_n, 2) * cw[0:1, :] + _shifted(dxc, dxc_p, dxc_n, 1) * cw[1:2, :]
               + dxc * cw[2:3, :] + _shifted(dxc, dxc_p, dxc_n, -1) * cw[3:4, :])
        xrv = xr_ref[...]
        xr_p = jnp.where(first, xrp_ref[...], 0.0)
        xr_n = jnp.where(last, xrn_ref[...], 0.0)
        for k, off in enumerate((-2, -1, 0, 1)):
            dcw_ref[k:k + 1, :] += jnp.sum(dxc * _shifted(xrv, xr_p, xr_n, off), axis=0, keepdims=True)
        dcw_ref[4:5, :] += jnp.sum(dxc, axis=0, keepdims=True)

        def total(a, b, c_):
            return a[0].astype(F32) + _merge_classes(b, s_ref, F32_LAYOUT, c_)

        dproj = jnp.concatenate(
            [dxr, dgate_ref[...], total(dq1, dq2, dq3_), total(dk1, dk2, dk3_), total(dv1, dv2, dv3_)],
            axis=-1).astype(BF16)
        xv = x_ref[...]
        rstd, xh = _rms(xv)
        hb = (xh * g_ref[...]).astype(BF16)
        dh = _dot_nt(dproj, w_full[...])
        dw_full[...] += _dot_tn(hb, dproj)
        dxn, dg = _rms_bwd(dh, g_ref[...], xh, rstd)
        dg_ref[...] += dg
        dx_ref[...] = dx1_ref[...] + dxn

        @pl.when(i == nt - 1)
        def _():
            copies = [pltpu.make_async_copy(src, dst, sem) for src, dst, sem in blocks(dw_full, dw_ref, N_SHARD)]
            for cp in copies:
                cp.start()
            for cp in copies:
                cp.wait()

    halo = _halo_specs(S, ident, tb)
    return _call(
        body, "inproj_bwd", (nt,),
        [_rows(D_MODEL, tb), _rows(D_MODEL, tb)] + halo * 3 + [_rows(512, tb)]
        + [_class_spec(1, tb), _class_spec(F32_LAYOUT, tb), _class_spec(F32_LAYOUT, tb)] * 3 + [_whole_vmem()] * 2 + [ANY],
        [_rows(D_MODEL, tb), ANY, _whole_vmem(), _whole_vmem()],
        [jax.ShapeDtypeStruct((S, D_MODEL), F32), jax.ShapeDtypeStruct((N_SHARD, D_MODEL, IN_BLK), F32),
         jax.ShapeDtypeStruct((1, D_MODEL), F32), jax.ShapeDtypeStruct((SUB, 512), F32)],
        [pltpu.VMEM((4, tb, 128), F32), pltpu.VMEM((D_MODEL, D_IN), BF16), pltpu.VMEM((D_MODEL, D_IN), F32),
         pltpu.SemaphoreType.DMA((2 * N_SHARD,))],
        (x, dx1, xr, xr, xr, dxc_f, dxc_f, dxc_f, dxc_b, dxc_b, dxc_b, dgate, *dq3, *dk3, *dv3, g1, conv_w, w_in), rider)


def _dw_matmul(a, b, a_cols, b_cols, name):
    S = a.shape[0]
    tk = min(S, TK_DW)
    a_shared = a.shape[1] == a_cols
    b_shared = b.shape[1] == b_cols

    def body(a_ref, b_ref, o_ref):
        @pl.when(pl.program_id(1) == 0)
        def _():
            o_ref[...] = jnp.zeros_like(o_ref)
        o_ref[0] += _dot_tn(a_ref[...], b_ref[...])

    return pl.pallas_call(
        body, grid=(N_SHARD, S // tk), name=name,
        in_specs=[pl.BlockSpec((tk, a_cols), (lambda j, k: (k, 0)) if a_shared else (lambda j, k: (k, j))),
                  pl.BlockSpec((tk, b_cols), (lambda j, k: (k, 0)) if b_shared else (lambda j, k: (k, j)))],
        out_specs=pl.BlockSpec((1, a_cols, b_cols), lambda j, k: (j, 0, 0)),
        out_shape=jax.ShapeDtypeStruct((N_SHARD, a_cols, b_cols), F32),
        compiler_params=_params(2),
    )(a, b)


def _t5_bucket_np(rel):
    nb = N_BUCKETS // 2
    max_exact = nb // 2
    ret = np.where(rel > 0, nb, 0)
    n = np.abs(rel)
    nf = np.maximum(n, 1).astype(np.float32)
    large = max_exact + (np.log(nf / np.float32(max_exact)) / np.float32(math.log(MAX_DISTANCE / max_exact))
                         * np.float32(nb - max_exact)).astype(np.int32)
    large = np.minimum(large, nb - 1)
    return ret + np.where(n < max_exact, n, large)


_VARIANT_OFFSETS = (-HALF_WIN,) * 3


def _band_index():
    kk = np.arange(K_WIN)[None, :]
    ql = np.arange(Q_BLK)[:, None]
    rel = np.stack([kk - ql + off for off in _VARIANT_OFFSETS])
    band = np.abs(rel) <= HALF_WIN
    inside = np.stack([np.broadcast_to(kk >= HALF_WIN, band[0].shape), np.ones_like(band[0]),
                       np.broadcast_to(kk < K_WIN - HALF_WIN, band[0].shape)])
    return rel, band & inside


def _bucket_tables(dil):
    rel, valid = _band_index()
    bucket = _t5_bucket_np(np.clip(rel, -HALF_WIN, HALF_WIN) * dil)
    return np.where(valid, bucket, -1).astype(np.int32)


def _bias_mats(rel_bias, rider=None):
    tables = [_bucket_tables(d) for d in DILATIONS]
    used = [sorted(set(t[t >= 0].tolist())) for t in tables]

    def one_pattern(rb_ref, t_ref, o_ref, buckets):
        bk = t_ref[1]
        for h in range(N_HEADS):
            acc = jnp.full((Q_BLK, K_WIN), NEG_INF, F32)
            for b in buckets:
                acc = jnp.where(bk == b, rb_ref[b, h], acc)
            o_ref[1, h] = acc
            for var in (0, 2):
                o_ref[var, h] = jnp.where(t_ref[var] >= 0, acc, NEG_INF)

    def body(rb_ref, t1, t2, t3, o1, o2, o3):
        for i, (t_ref, o_ref) in enumerate(((t1, o1), (t2, o2), (t3, o3))):
            pl.when(pl.program_id(0) == i)(functools.partial(one_pattern, rb_ref, t_ref, o_ref, used[i]))

    shp = jax.ShapeDtypeStruct((3, N_HEADS, Q_BLK, K_WIN), F32)
    return _call(
        body, "bias_tables", (len(DILATIONS),), [pl.BlockSpec(memory_space=pltpu.SMEM)] + [_whole_vmem()] * 3,
        [_whole_vmem()] * 3, [shp] * 3, [], (rel_bias, *[jnp.asarray(t) for t in tables]), rider)


def _variant(qb, nq):
    return jnp.where(qb == 0, 0, jnp.where(qb == nq - 1, 2, 1))


def _win_start(qb):
    return pl.multiple_of(qb * Q_BLK, Q_BLK)


def _fill_padded(src_ref, pad_ref):
    L = src_ref.shape[0]
    edge = jnp.zeros((HALF_WIN, 128), pad_ref.dtype)
    pad_ref[0:HALF_WIN, :] = edge
    pad_ref[HALF_WIN:HALF_WIN + L, :] = src_ref[...]
    pad_ref[HALF_WIN + L:2 * HALF_WIN + L, :] = edge


INNER = {1: 1, 4: 1, 16: 4}


def _attn_layout(dil, L, blocks=ATTN_SUB):
    inner = INNER[dil]
    n_outer = dil // inner
    nsub = min(blocks // inner, L // Q_BLK)
    qt = nsub * Q_BLK
    grid = (4, n_outer, L // qt)
    qspec = pl.BlockSpec((inner, None, qt, 128), lambda hp, r, s: (0, r, s, hp))
    kspec = pl.BlockSpec((inner, None, L, 128), lambda hp, r, s: (0, r, 0, hp))
    bspec = pl.BlockSpec((3, 2, Q_BLK, K_WIN), lambda hp, r, s: (0, hp, 0, 0))
    kfspec = pl.BlockSpec((None, inner * L, 128), lambda hp, r, s: (r, 0, hp))
    qfspec = kfspec if inner > 1 else pl.BlockSpec((None, qt, 128), lambda hp, r, s: (r, s, hp))
    fshape = jax.ShapeDtypeStruct((n_outer, inner * L, D_ATTN), F32)
    view = lambda t: t.reshape(inner, n_outer, L, D_ATTN)

    def qrows(m, sub):
        if inner == 1:
            return (slice(sub * Q_BLK, (sub + 1) * Q_BLK), slice(None))
        first = (pl.program_id(2) * nsub + sub) * Q_BLK
        return (pl.ds(m + inner * first, Q_BLK, stride=inner), slice(None))

    def krows(m):
        if inner == 1:
            return (slice(None), slice(None))
        return (pl.ds(m, L, stride=inner), slice(None))

    return inner, nsub, grid, qspec, kspec, bspec, qfspec, kfspec, fshape, view, qrows, krows


def _head_masks():
    lane = lax.broadcasted_iota(jnp.int32, (Q_BLK, 128), 1)
    return lane < HEAD_DIM


def _attn_fwd(q, k, v, bias):
    dil, L, _ = q.shape
    nq = L // Q_BLK
    inner, nsub, grid, qspec, kspec, bspec, qfspec, kfspec, fshape, view, qrows, krows = _attn_layout(dil, L, ATTN_SUB_FWD)

    def body(q_ref, k_ref, v_ref, b_ref, o_ref, l_ref, kp, vp):
        step = pl.program_id(2)

        @pl.when(step == 0)
        def _():
            for m in range(inner):
                _fill_padded(k_ref.at[m], kp.at[m])
                _fill_padded(v_ref.at[m], vp.at[m])

        h0 = _head_masks()
        for m, sub in [(m, sub) for m in range(inner) for sub in range(nsub)]:
            qb = step * nsub + sub
            st = _win_start(qb)
            var = _variant(qb, nq)
            kw = kp[m, pl.ds(st, K_WIN), :]
            vw = vp[m, pl.ds(st, K_WIN), :]
            qs = q_ref[m, sub * Q_BLK:(sub + 1) * Q_BLK, :] * ATTN_SCALE
            zq = jnp.zeros_like(qs)
            q2 = jnp.concatenate([jnp.where(h0, qs, zq), jnp.where(h0, zq, qs)], axis=0)
            s = _dot_nt(q2, kw) + b_ref[var].reshape(2 * Q_BLK, K_WIN)
            top = jnp.max(s, axis=-1, keepdims=True)
            p = jnp.exp(s - top)
            l = jnp.sum(p, axis=-1, keepdims=True)
            out = _dot(p.astype(BF16), vw) / l
            lse = top + jnp.log(l)
            o_ref[qrows(m, sub)] = jnp.where(h0, out[0:Q_BLK], out[Q_BLK:2 * Q_BLK])
            l_ref[qrows(m, sub)] = jnp.where(h0, lse[0:Q_BLK], lse[Q_BLK:2 * Q_BLK])

    return pl.pallas_call(
        body, grid=grid, name=f"attn_fwd_d{dil}",
        in_specs=[qspec, kspec, kspec, bspec], out_specs=[qfspec, qfspec], out_shape=[fshape, fshape],
        scratch_shapes=[pltpu.VMEM((inner, L + 2 * HALF_WIN, 128), BF16)] * 2,
        compiler_params=_params(3),
    )(view(q), view(k), view(v), bias)


def _attn_bwd(q, k, v, bias, do, o, lse, rider=None):
    dil, L, _ = q.shape
    nq = L // Q_BLK
    inner, nsub, grid, qspec, kspec, bspec, qfspec, kfspec, fshape, view, qrows, krows = _attn_layout(dil, L)
    nstep = grid[2]

    def body(q_ref, k_ref, v_ref, b_ref, do_ref, o_ref, l_ref, dq_ref, dk_ref, dv_ref, db_ref, db_s,
             kp, vp, dkp, dvp, carry):
        hp, step = pl.program_id(0), pl.program_id(2)
        first = (hp == 0) & (pl.program_id(1) == 0) & (step == 0)
        last = (hp == grid[0] - 1) & (pl.program_id(1) == grid[1] - 1) & (step == nstep - 1)

        @pl.when(first)
        def _():
            db_s[...] = jnp.zeros_like(db_s)

        @pl.when(step == 0)
        def _():
            for m in range(inner):
                _fill_padded(k_ref.at[m], kp.at[m])
                _fill_padded(v_ref.at[m], vp.at[m])
            carry[...] = jnp.zeros_like(carry)

        h0 = _head_masks()
        for m, sub in [(m, sub) for m in range(inner) for sub in range(nsub)]:
            if sub == 0:
                carry_k, carry_v = carry[m, 0], carry[m, 1]
            qb = step * nsub + sub
            st = _win_start(qb)
            var = _variant(qb, nq)
            kw = kp[m, pl.ds(st, K_WIN), :]
            vw = vp[m, pl.ds(st, K_WIN), :]
            qs = q_ref[m, sub * Q_BLK:(sub + 1) * Q_BLK, :] * ATTN_SCALE
            dof = do_ref[qrows(m, sub)]
            dob = dof.astype(BF16)
            prod = dof * o_ref[qrows(m, sub)]
            lsev = l_ref[qrows(m, sub)]
            zq, zd = jnp.zeros_like(qs), jnp.zeros_like(dob)
            q2 = jnp.concatenate([jnp.where(h0, qs, zq), jnp.where(h0, zq, qs)], axis=0)
            do2 = jnp.concatenate([jnp.where(h0, dob, zd), jnp.where(h0, zd, dob)], axis=0)
            lse2 = jnp.concatenate([lsev[:, 0:1], lsev[:, HEAD_DIM:HEAD_DIM + 1]], axis=0)
            dd2 = jnp.concatenate([jnp.sum(jnp.where(h0, prod, 0.0), axis=-1, keepdims=True),
                                   jnp.sum(jnp.where(h0, 0.0, prod), axis=-1, keepdims=True)], axis=0)
            s = _dot_nt(q2, kw) + b_ref[var].reshape(2 * Q_BLK, K_WIN)
            p = jnp.exp(s - lse2)
            ds = p * (_dot_nt(do2, vw) - dd2)
            db_s[pl.ds(hp * 2, 2)] += ds.reshape(2, Q_BLK, K_WIN)
            dsb = ds.astype(BF16)
            dv_acc = _dot_tn(p.astype(BF16), do2)
            dk_acc = _dot_tn(dsb, q2)
            dq2 = _dot(dsb, kw) * ATTN_SCALE
            dq_ref[qrows(m, sub)] = jnp.where(h0, dq2[0:Q_BLK], dq2[Q_BLK:2 * Q_BLK]).astype(dq_ref.dtype)
            dkp[m, pl.ds(st, Q_BLK), :] = carry_k + dk_acc[0:Q_BLK]
            dvp[m, pl.ds(st, Q_BLK), :] = carry_v + dv_acc[0:Q_BLK]
            carry_k, carry_v = dk_acc[Q_BLK:K_WIN], dv_acc[Q_BLK:K_WIN]
            if sub == nsub - 1:
                carry[m, 0] = carry_k
                carry[m, 1] = carry_v

        @pl.when(step == nstep - 1)
        def _():
            for m in range(inner):
                dkp[m, L:L + Q_BLK, :] = carry[m, 0]
                dvp[m, L:L + Q_BLK, :] = carry[m, 1]
                dk_ref[krows(m)] = dkp[m, HALF_WIN:HALF_WIN + L, :].astype(dk_ref.dtype)
                dv_ref[krows(m)] = dvp[m, HALF_WIN:HALF_WIN + L, :].astype(dv_ref.dtype)

        @pl.when(last)
        def _():
            db_ref[...] = db_s[...]

    dbshape = (N_HEADS, Q_BLK, K_WIN)
    gshape = jax.ShapeDtypeStruct(fshape.shape, BF16 if inner == 1 else F32)
    return _call(
        body, f"attn_bwd_d{dil}", grid,
        [qspec, kspec, kspec, bspec, qfspec, qfspec, qfspec],
        [qfspec, kfspec, kfspec, _whole_vmem()],
        [gshape, gshape, gshape, jax.ShapeDtypeStruct(dbshape, F32)],
        [pltpu.VMEM(dbshape, F32)] + [pltpu.VMEM((inner, L + 2 * HALF_WIN, 128), BF16)] * 2
        + [pltpu.VMEM((inner, L + 2 * HALF_WIN, 128), F32)] * 2 + [pltpu.VMEM((inner, 2, Q_BLK, 128), F32)],
        (view(q), view(k), view(v), bias, do, o, lse), rider)


def _bucket_onehots(dil):
    m = np.zeros((3, K_WIN, N_BUCKETS), np.float32)
    for var, off in enumerate(_VARIANT_OFFSETS):
        for rel in range(-HALF_WIN, HALF_WIN + 1):
            col = (rel - off + Q_BLK - 1) % K_WIN
            m[var, col, int(_t5_bucket_np(np.asarray(rel * dil)))] = 1.0
    return jnp.asarray(m)


def _bias_grad(dbs):
    onehots = [_bucket_onehots(d) for d in DILATIONS]
    flip = jnp.asarray(np.eye(Q_BLK, dtype=np.float32)[::-1].copy())

    def body(d1, d2, d3, m1, m2, m3, flip_ref, out_ref):
        hp = lax.Precision.HIGHEST
        acc = jnp.zeros((N_HEADS, N_BUCKETS), F32)
        for d_ref, m_ref in ((d1, m1), (d2, m2), (d3, m3)):
            rows = []
            for h in range(N_HEADS):
                xrev = jnp.dot(flip_ref[...], d_ref[h], precision=hp, preferred_element_type=F32)
                y = pltpu.roll(xrev, 0, 1, stride=1, stride_axis=0)
                rows.append(jnp.sum(y, axis=0, keepdims=True))
            acc = acc + jnp.dot(jnp.concatenate(rows, axis=0), m_ref[1], precision=hp, preferred_element_type=F32)
        out_ref[...] = acc

    return pl.pallas_call(
        body, name="bias_grad", out_shape=jax.ShapeDtypeStruct((N_HEADS, N_BUCKETS), F32),
        compiler_params=_params(0),
    )(*dbs, *onehots, flip)


def _block_diag(w):
    eye = jnp.eye(N_RNN_BLOCKS, dtype=w.dtype)
    return jnp.einsum("ncd,nm->ncmd", w, eye).reshape(D_RNN, D_RNN).astype(BF16)


def _diag_blocks(dense):
    d = dense.reshape(N_RNN_BLOCKS, RNN_BLOCK, N_RNN_BLOCKS, RNN_BLOCK)
    return jnp.stack([d[n, :, n, :] for n in range(N_RNN_BLOCKS)])


EARLY = ("w_out", "w_up", "w_down")


def _local_step(x, target, p, shards=None):
    p = dict(p)
    first = None if shards is None else _gather_rider(["w_in"], [shards["w_in"]], shards["conv_w"])
    biases, got = _bias_mats(p["rel_bias"], first)
    if shards is not None:
        p["w_in"] = got[0]
        p["conv_w"] = jnp.transpose(got[1], (1, 0, 2)).reshape(4, D_RNN)
    lru = {}
    for dname in ("fwd", "bwd"):
        lru[dname] = (_block_diag(p["lru_wa_" + dname]), p["lru_ba_" + dname], _block_diag(p["lru_wx_" + dname]),
                      p["lru_bx_" + dname], p["lru_lam_" + dname])

    def gather(name):
        return None if shards is None else _gather_rider([name], [shards[name]])

    (xr, gate, *qkv), got = _inproj_fwd(x, p["attn_norm_g"], p["w_in"], gather("w_out"))
    p.update(zip(["w_out"], got))
    qs, ks, vs = qkv[0:3], qkv[3:6], qkv[6:9]
    (hf, xc), got = _rnn_fwd(xr, p["conv_w"], p["conv_b"], *lru["fwd"], reverse=False, rider=gather("w_up"))
    p.update(zip(["w_up"], got))
    (hb,), got = _rnn_fwd(xr, p["conv_w"], p["conv_b"], *lru["bwd"], reverse=True, rider=gather("w_down"), xc=xc)
    p.update(zip(["w_down"], got))
    outs, lses = [], []
    for q, k, v, bias in zip(qs, ks, vs, biases):
        o, l = _attn_fwd(q, k, v, bias)
        outs.append(o)
        lses.append(l)
    x1, mixb, *yl = _mix_fwd(outs, lses, hf, hb, gate, x, p["norm_rnn_g"], p["norm_attn_g"], p["w_out"])
    yas, lsts = [yl[0], yl[1], yl[1]], [yl[2], yl[3], yl[3]]
    dx1, h2b, a2b, dub, dx2b, loss_vec, dg_fin, dg_mlp = _mlp_fwd_bwd(
        x1, target, p["mlp_norm_g"], p["final_norm_g"], p["w_up"], p["w_down"])
    dhs, dgate, dya1, dya4, dw_out, dg_rnn, dg_attn = _mix_bwd(dx1, p["w_out"], mixb, yas[0], hf, hb, gate,
                                                               p["norm_rnn_g"], p["norm_attn_g"])
    dyas = [dya1, dya4, dya4]
    dw_up = _dw_matmul(h2b, dub, D_MODEL, FF_BLK, "dw_up")
    dw_down = _dw_matmul(a2b, dx2b, FF_BLK, D_MODEL, "dw_down")
    early = [dw_out, dw_up, dw_down]
    dqs, dks, dvs, dbs = [], [], [], []
    for i, (q, k, v, bias, dya, ya, lse) in enumerate(zip(qs, ks, vs, biases, dyas, yas, lsts)):
        rider = None
        if shards is not None:
            make = (lambda: _pair_exchange_rider(EARLY, early), lambda: _chip_exchange_rider(early),
                    lambda: _pair_share_rider(EARLY, early))[i]
            rider = make()
        (dq, dk, dv, db), got = _attn_bwd(q, k, v, bias, dya, ya, lse, rider)
        if shards is not None and i == 0:
            core = lax.axis_index("c").reshape(1).astype(jnp.int32)
            early = [_pair_add(core, g, o, "grad_pair_add_" + n) for n, g, o in zip(EARLY, early, got)]
        elif shards is not None and i == 1:
            early = [_chip_sum(t, "grad_chip_sum_" + n) for n, t in zip(EARLY, got)]
        elif shards is not None:
            early = got
        dqs.append(dq)
        dks.append(dk)
        dvs.append(dv)
        dbs.append(db)
    d_rel_bias = _bias_grad(dbs).T
    (dxc_f, dwa_f, dwx_f, dvec_f), _ = _rnn_bwd(xc, hf, dhs, *lru["fwd"], reverse=False)
    small = {
        "lru_wa_fwd": _diag_blocks(dwa_f), "lru_ba_fwd": dvec_f[0:1], "lru_wx_fwd": _diag_blocks(dwx_f),
        "lru_bx_fwd": dvec_f[1:2], "lru_lam_fwd": dvec_f[3:4],
        "rel_bias": d_rel_bias, "norm_rnn_g": dg_rnn, "norm_attn_g": dg_attn,
        "mlp_norm_g": dg_mlp, "final_norm_g": dg_fin,
    }
    loss_local = (0.5 / D_MODEL) * jnp.sum(loss_vec)
    rider = None
    if shards is not None:
        rider = _small_gather_rider(_pack([small[n].reshape(shp) for n, shp in SMALL if n in small]
                                          + [loss_local.reshape(1)]))
    (dxc_b, dwa_b, dwx_b, dvec_b), gathered = _rnn_bwd(xc, hb, dhs, *lru["bwd"], reverse=True, rider=rider)
    grad_x, dw_in, dg1, dconv = _inproj_bwd(x, dx1, xr, dxc_f, dxc_b, dgate, dqs, dks, dvs,
                                            p["attn_norm_g"], p["conv_w"], p["w_in"])[0]
    last = {"lru_wa_bwd": _diag_blocks(dwa_b), "lru_ba_bwd": dvec_b[0:1], "lru_wx_bwd": _diag_blocks(dwx_b),
            "lru_bx_bwd": dvec_b[1:2], "lru_lam_bwd": dvec_b[3:4],
            "attn_norm_g": dg1, "conv_w": dconv[0:4], "conv_b": dconv[4:5]}
    if shards is None:
        big = {"w_in": dw_in, "w_out": dw_out, "w_up": dw_up, "w_down": dw_down}
        return loss_local, grad_x, {**small, **last}, None, big, {}
    return loss_local, grad_x, last, gathered[0], {"w_in": dw_in}, dict(zip(EARLY, early))


BIG = ("w_in", "w_out", "w_up", "w_down")
BIG_SHARD = {"w_in": (D_MODEL, IN_BLK), "w_out": (OUT_BLK, D_MODEL), "w_up": (D_MODEL, FF_BLK), "w_down": (FF_BLK, D_MODEL)}
N_BIG = len(BIG)
N_CHIP_PEERS = 3
ANY = pl.BlockSpec(memory_space=pl.ANY)


def _place():
    x, y, c = lax.axis_index("x"), lax.axis_index("y"), lax.axis_index("c")
    chips = [(1 - x, y), (x, 1 - y), (1 - x, 1 - y)]
    return x, y, c, chips


def _remote(src, dst, send_sem, recv_sem, dev):
    return pltpu.make_async_remote_copy(src_ref=src, dst_ref=dst, send_sem=send_sem, recv_sem=recv_sem,
                                        device_id=dev, device_id_type=MESH)


def _staged_start(srcs, bufs, sems):
    legs = [pltpu.make_async_copy(s, b, sems.at[i]) for i, (s, b) in enumerate(zip(srcs, bufs))]
    for cp in legs:
        cp.start()
    return legs


def _staged_finish(legs, bufs, dsts, sems):
    out = []
    for i, (leg, b, d) in enumerate(zip(legs, bufs, dsts)):
        leg.wait()
        cp = pltpu.make_async_copy(b, d, sems.at[i])
        cp.start()
        out.append(cp)
    return out


class _Rider:
    def __init__(self, inputs, out_shape, scratch, first, late, last):
        self.inputs, self.out_shape, self.scratch = list(inputs), list(out_shape), list(scratch)
        self.first, self.late, self.last = first, late, last


def _call(body, name, grid, in_specs, out_specs, out_shape, scratch, operands, rider=None):
    n_grid = len(grid)
    if rider is None:
        res = pl.pallas_call(body, grid=grid, name=name, in_specs=in_specs, out_specs=out_specs, out_shape=out_shape,
                             scratch_shapes=scratch, compiler_params=_params(n_grid))(*operands)
        return list(res), []
    n_in, n_out, n_scr = len(in_specs), len(out_specs), len(scratch)
    ri, ro = len(rider.inputs), len(rider.out_shape)
    nsteps = int(np.prod(grid))
    late_step = max(nsteps - 3, 1)

    def wrapped(*refs):
        a, b = n_in, n_in + ri
        c, d = b + n_out, b + n_out + ro
        e = d + n_scr
        mine = refs[:a] + refs[b:c] + refs[d:e]
        theirs = (refs[a:b], refs[c:d], refs[e:])
        step = pl.program_id(0)
        for ax in range(1, n_grid):
            step = step * grid[ax] + pl.program_id(ax)
        pl.when(step == 0)(lambda: rider.first(*theirs))
        pl.when(step == late_step)(lambda: rider.late(*theirs))
        body(*mine)
        pl.when(step == nsteps - 1)(lambda: rider.last(*theirs))

    res = pl.pallas_call(
        wrapped, grid=grid, name=name, in_specs=list(in_specs) + [ANY] * ri, out_specs=list(out_specs) + [ANY] * ro,
        out_shape=list(out_shape) + rider.out_shape, scratch_shapes=list(scratch) + rider.scratch,
        compiler_params=_params(n_grid),
    )(*operands, *rider.inputs)
    return list(res[:n_out]), list(res[n_out:])


def _run_rider(rider, name):
    ri, ro = len(rider.inputs), len(rider.out_shape)

    def body(*refs):
        parts = (refs[:ri], refs[ri:ri + ro], refs[ri + ro:])
        rider.first(*parts)
        rider.late(*parts)
        rider.last(*parts)

    return list(pl.pallas_call(
        body, name=name, in_specs=[ANY] * ri, out_specs=[ANY] * ro, out_shape=rider.out_shape, scratch_shapes=rider.scratch,
        compiler_params=pltpu.CompilerParams(has_side_effects=True, vmem_limit_bytes=VMEM_LIMIT),
    )(*rider.inputs))


def _nothing(ins, outs, scr):
    return None


def _gather_rider(names, shards, conv_w=None):
    n = len(names)
    items = n + (conv_w is not None)
    halves = [BIG_SHARD[nm][0] // 2 for nm in names]

    def parts(ins, outs, scr):
        x, y, c, chips = _place()
        return x, y, c, chips, 2 * x + y, (x, y, 1 - c), scr[:8], scr[8:]

    def piece(outs, w, chip, core_half):
        return outs[w].at[chip, pl.ds(core_half * halves[w], halves[w])]

    def ici(ins, outs, sems, w, k, chip_xy, c, me):
        return _remote(ins[w].at[pl.ds(c * halves[w], halves[w])], piece(outs, w, me, c),
                       sems[0].at[w, k], sems[1].at[w, k], (*chip_xy, c))

    def first(ins, outs, scr):
        x, y, c, chips, me, sibling, sems, bufs = parts(ins, outs, scr)
        legs = _staged_start(ins, bufs, sems[6])
        for w in range(n):
            for k, chip_xy in enumerate(chips):
                ici(ins, outs, sems, w, k, chip_xy, c, me).start()
        if conv_w is not None:
            for k, (px, py) in enumerate(chips):
                _remote(ins[n], outs[n].at[me], sems[4].at[k], sems[5].at[k], (px, py, c)).start()
        _staged_finish(legs, bufs, [o.at[me] for o in outs], sems[7])

    def late(ins, outs, scr):
        x, y, c, chips, me, sibling, sems, bufs = parts(ins, outs, scr)
        for w in range(n):
            for k, (px, py) in enumerate(chips):
                landed = piece(outs, w, 2 * px + py, c)
                _remote(landed, landed, sems[0].at[w, k], sems[1].at[w, k], (px, py, c)).wait_recv()
                _remote(landed, landed, sems[2].at[w, k], sems[3].at[w, k], sibling).start()

    def last(ins, outs, scr):
        x, y, c, chips, me, sibling, sems, bufs = parts(ins, outs, scr)
        for w in range(n):
            for k, (px, py) in enumerate(chips):
                other = piece(outs, w, 2 * px + py, 1 - c)
                _remote(other, other, sems[2].at[w, k], sems[3].at[w, k], sibling).wait_recv()
        if conv_w is not None:
            for k, (px, py) in enumerate(chips):
                got = outs[n].at[2 * px + py]
                _remote(got, got, sems[4].at[k], sems[5].at[k], (px, py, c)).wait_recv()
                _remote(ins[n], outs[n].at[me], sems[4].at[k], sems[5].at[k], (px, py, c)).wait_send()
        for i in range(items):
            pltpu.make_async_copy(bufs[i], outs[i].at[me], sems[7].at[i]).wait()
        for w in range(n):
            for k, (px, py) in enumerate(chips):
                ici(ins, outs, sems, w, k, (px, py), c, me).wait_send()
                landed = piece(outs, w, 2 * px + py, c)
                _remote(landed, landed, sems[2].at[w, k], sems[3].at[w, k], sibling).wait_send()

    out_shape = [jax.ShapeDtypeStruct((N_SHARD,) + BIG_SHARD[nm], BF16) for nm in names]
    stage = [pltpu.VMEM(BIG_SHARD[nm], BF16) for nm in names]
    inputs = list(shards)
    if conv_w is not None:
        out_shape.append(jax.ShapeDtypeStruct((N_SHARD,) + conv_w.shape, F32))
        stage.append(pltpu.VMEM(conv_w.shape, F32))
        inputs.append(conv_w)
    scratch = ([pltpu.SemaphoreType.DMA((n, N_CHIP_PEERS))] * 4 + [pltpu.SemaphoreType.DMA((N_CHIP_PEERS,))] * 2
               + [pltpu.SemaphoreType.DMA((items,))] * 2 + stage)
    return _Rider(inputs, out_shape, scratch, first, late, last)


def _pair_exchange_rider(names, grads):
    def copies(ins, outs, scr):
        x, y, c, _ = _place()
        out = []
        for w, nm in enumerate(names):
            h = BIG_SHARD[nm][0] // 2
            out.append(_remote(ins[w].at[:, pl.ds((1 - c) * h, h), :], outs[w], scr[0].at[w], scr[1].at[w], (x, y, 1 - c)))
        return out

    def first(ins, outs, scr):
        for cp in copies(ins, outs, scr):
            cp.start()

    def last(ins, outs, scr):
        for cp in copies(ins, outs, scr):
            cp.wait()

    out_shape = [jax.ShapeDtypeStruct((N_SHARD, BIG_SHARD[nm][0] // 2, BIG_SHARD[nm][1]), F32) for nm in names]
    return _Rider(grads, out_shape, [pltpu.SemaphoreType.DMA((len(names),))] * 2, first, _nothing, last)


def _pair_add(core, grad, other, name):
    _, r, cols = grad.shape
    h = r // 2
    th = min(h, 256)
    per = h // th

    def body(c_ref, g_ref, o_ref, out_ref):
        out_ref[...] = (g_ref[...] + o_ref[...]).astype(BF16)

    return pl.pallas_call(
        body, name=name,
        grid_spec=pltpu.PrefetchScalarGridSpec(
            num_scalar_prefetch=1, grid=(N_SHARD, per),
            in_specs=[pl.BlockSpec((1, th, cols), lambda j, i, c_ref: (j, c_ref[0] * per + i, 0)),
                      pl.BlockSpec((1, th, cols), lambda j, i, c_ref: (j, i, 0))],
            out_specs=pl.BlockSpec((1, th, cols), lambda j, i, c_ref: (j, i, 0))),
        out_shape=jax.ShapeDtypeStruct((N_SHARD, h, cols), BF16),
        compiler_params=_params(2),
    )(core, grad, other)


def _chip_exchange_rider(parts):
    n = len(parts)

    def sends(ins, outs, scr):
        x, y, c, chips = _place()
        me = 2 * x + y
        return [_remote(ins[w].at[2 * px + py], outs[w].at[me], scr[0].at[w, k], scr[1].at[w, k], (px, py, c))
                for w in range(n) for k, (px, py) in enumerate(chips)]

    def first(ins, outs, scr):
        x, y, c, chips = _place()
        me = 2 * x + y
        legs = _staged_start([r.at[me] for r in ins], scr[4:], scr[2])
        for cp in sends(ins, outs, scr):
            cp.start()
        _staged_finish(legs, scr[4:], [o.at[me] for o in outs], scr[3])

    def last(ins, outs, scr):
        x, y, c, chips = _place()
        me = 2 * x + y
        for w in range(n):
            for k, (px, py) in enumerate(chips):
                got = outs[w].at[2 * px + py]
                _remote(got, got, scr[0].at[w, k], scr[1].at[w, k], (px, py, c)).wait_recv()
        for cp in sends(ins, outs, scr):
            cp.wait_send()
        for w in range(n):
            pltpu.make_async_copy(scr[4 + w], outs[w].at[me], scr[3].at[w]).wait()

    out_shape = [jax.ShapeDtypeStruct(p.shape, BF16) for p in parts]
    scratch = ([pltpu.SemaphoreType.DMA((n, N_CHIP_PEERS))] * 2 + [pltpu.SemaphoreType.DMA((n,))] * 2
               + [pltpu.VMEM(p.shape[1:], BF16) for p in parts])
    return _Rider(parts, out_shape, scratch, first, _nothing, last)


def _chip_sum(parts, name):
    _, h, cols = parts.shape
    th = min(h, 256)

    def body(p_ref, out_ref):
        acc = p_ref[0].astype(F32)
        for j in range(1, N_SHARD):
            acc = acc + p_ref[j].astype(F32)
        out_ref[...] = acc

    return pl.pallas_call(
        body, name=name, grid=(h // th,),
        in_specs=[pl.BlockSpec((N_SHARD, th, cols), lambda i: (0, i, 0))],
        out_specs=pl.BlockSpec((th, cols), lambda i: (i, 0)),
        out_shape=jax.ShapeDtypeStruct((h, cols), F32),
        compiler_params=_params(),
    )(parts)


def _pair_share_rider(names, halves):
    n = len(names)
    hs = [BIG_SHARD[nm][0] // 2 for nm in names]

    def mine(outs, c):
        return [outs[w].at[pl.ds(c * hs[w], hs[w])] for w in range(n)]

    def first(ins, outs, scr):
        x, y, c, _ = _place()
        legs = _staged_start(ins, scr[4:], scr[2])
        for w, dst in enumerate(mine(outs, c)):
            _remote(ins[w], dst, scr[0].at[w], scr[1].at[w], (x, y, 1 - c)).start()
        _staged_finish(legs, scr[4:], mine(outs, c), scr[3])

    def last(ins, outs, scr):
        x, y, c, _ = _place()
        for w, (theirs, dst) in enumerate(zip(mine(outs, 1 - c), mine(outs, c))):
            _remote(theirs, theirs, scr[0].at[w], scr[1].at[w], (x, y, 1 - c)).wait_recv()
            _remote(ins[w], dst, scr[0].at[w], scr[1].at[w], (x, y, 1 - c)).wait_send()
            pltpu.make_async_copy(scr[4 + w], dst, scr[3].at[w]).wait()

    out_shape = [jax.ShapeDtypeStruct(BIG_SHARD[nm], F32) for nm in names]
    scratch = [pltpu.SemaphoreType.DMA((n,))] * 4 + [pltpu.VMEM((h, BIG_SHARD[nm][1]), F32) for nm, h in zip(names, hs)]
    return _Rider(halves, out_shape, scratch, first, _nothing, last)


N_DEV = 8


def _all_peers(x, y, c):
    return [((1 - x) if fx else x, (1 - y) if fy else y, (1 - c) if fc else c)
            for fx in (0, 1) for fy in (0, 1) for fc in (0, 1) if fx or fy or fc]


def _small_gather_rider(vec):
    def sends(ins, outs, scr):
        x, y, c, _ = _place()
        me = 4 * x + 2 * y + c
        return [_remote(ins[0], outs[0].at[me], scr[0].at[k], scr[1].at[k], dev) for k, dev in enumerate(_all_peers(x, y, c))]

    def first(ins, outs, scr):
        x, y, c, _ = _place()
        legs = _staged_start(ins, scr[4:], scr[2])
        for cp in sends(ins, outs, scr):
            cp.start()
        _staged_finish(legs, scr[4:], [outs[0].at[4 * x + 2 * y + c]], scr[3])

    def last(ins, outs, scr):
        x, y, c, _ = _place()
        for k, (px, py, pc) in enumerate(_all_peers(x, y, c)):
            got = outs[0].at[4 * px + 2 * py + pc]
            _remote(got, got, scr[0].at[k], scr[1].at[k], (px, py, pc)).wait_recv()
        for cp in sends(ins, outs, scr):
            cp.wait_send()
        pltpu.make_async_copy(scr[4], outs[0].at[4 * x + 2 * y + c], scr[3].at[0]).wait()

    scratch = ([pltpu.SemaphoreType.DMA((N_DEV - 1,))] * 2 + [pltpu.SemaphoreType.DMA((1,))] * 2
               + [pltpu.VMEM(vec.shape, F32)])
    return _Rider([vec], [jax.ShapeDtypeStruct((N_DEV,) + vec.shape, F32)], scratch, first, _nothing, last)


def _sum_devices(gathered):
    def body(g_ref, out_ref):
        acc = g_ref[0]
        for j in range(1, N_DEV):
            acc = acc + g_ref[j]
        out_ref[...] = acc

    return pl.pallas_call(body, name="sum_devices", out_shape=jax.ShapeDtypeStruct(gathered.shape[1:], F32),
                          compiler_params=_params(0))(gathered)


def _allreduce_small(vec):
    rows = vec.shape[0]

    def body(v_ref, sum_ref, gat_ref, send, recv, loc_sem):
        x, y, c, chips = _place()
        sibling = (x, y, 1 - c)
        slot = lambda px, py, pc: gat_ref.at[4 * px + 2 * py + pc]
        lc = pltpu.make_async_copy(v_ref, slot(x, y, c), loc_sem)
        lc.start()
        sends = [_remote(v_ref, slot(x, y, c), send.at[0], recv.at[0], sibling)]
        sends += [_remote(v_ref, slot(x, y, c), send.at[1 + k], recv.at[1 + k], (px, py, c))
                  for k, (px, py) in enumerate(chips)]
        for cp in sends:
            cp.start()
        for k, (px, py) in enumerate(chips):
            got = slot(px, py, c)
            _remote(got, got, send.at[1 + k], recv.at[1 + k], (px, py, c)).wait_recv()
            cp = _remote(got, got, send.at[4 + k], recv.at[4 + k], sibling)
            cp.start()
            sends.append(cp)
        got = slot(x, y, 1 - c)
        _remote(got, got, send.at[0], recv.at[0], sibling).wait_recv()
        for k, (px, py) in enumerate(chips):
            got = slot(px, py, 1 - c)
            _remote(got, got, send.at[4 + k], recv.at[4 + k], sibling).wait_recv()
        for cp in sends:
            cp.wait_send()
        lc.wait()
        acc = gat_ref[0]
        for j in range(1, N_DEV):
            acc = acc + gat_ref[j]
        sum_ref[...] = acc

    total, _ = pl.pallas_call(
        body, name="allreduce_small",
        in_specs=[_whole_vmem()], out_specs=[_whole_vmem(), _whole_vmem()],
        out_shape=[jax.ShapeDtypeStruct((rows, 128), F32), jax.ShapeDtypeStruct((N_DEV, rows, 128), F32)],
        scratch_shapes=[pltpu.SemaphoreType.DMA((N_DEV - 1,))] * 2 + [pltpu.SemaphoreType.DMA(())],
        compiler_params=pltpu.CompilerParams(has_side_effects=True, vmem_limit_bytes=VMEM_LIMIT),
    )(vec)
    return total


def _adam_math(w_ref, g_ref, m_ref, v_ref, d_ref, m2_ref, v2_ref):
    c1 = 1.0 - ADAM_B1 ** ADAM_STEP
    c2 = 1.0 - ADAM_B2 ** ADAM_STEP
    gv = g_ref[...]
    m2 = ADAM_B1 * m_ref[...] + (1.0 - ADAM_B1) * gv
    v2 = ADAM_B2 * v_ref[...] + (1.0 - ADAM_B2) * (gv * gv)
    m2_ref[...] = m2
    v2_ref[...] = v2
    d_ref[...] = -ADAM_LR * ((m2 / c1) / (jnp.sqrt(v2 / c2) + ADAM_EPS) + ADAM_WD * w_ref[...])


def _adamw_many(ws, gs, ms, vs):
    n = len(ws)

    def body(*refs):
        for i in range(n):
            _adam_math(*[refs[k * n + i] for k in range(7)])

    shapes = [jax.ShapeDtypeStruct(w.shape, F32) for w in ws]
    res = pl.pallas_call(body, name="adamw_small", out_shape=shapes * 3, compiler_params=_params(0))(*ws, *gs, *ms, *vs)
    return res[:n], res[n:2 * n], res[2 * n:]


def _adamw(w, g, m, v, name):
    rows, cols = w.shape
    tr = 256 if rows % 256 == 0 else rows

    def body(w_ref, g_ref, m_ref, v_ref, d_ref, m2_ref, v2_ref):
        _adam_math(w_ref, g_ref, m_ref, v_ref, d_ref, m2_ref, v2_ref)

    spec = pl.BlockSpec((tr, cols), lambda i: (i, 0))
    shp = jax.ShapeDtypeStruct((rows, cols), F32)
    return pl.pallas_call(
        body, name=name, grid=(rows // tr,), in_specs=[spec] * 4, out_specs=[spec] * 3, out_shape=[shp] * 3,
        compiler_params=_params(),
    )(w, g, m, v)


SMALL = (
    ("attn_norm_g", (1, 1024)), ("conv_w", (1, 4, 512)), ("conv_b", (1, 512)),
    ("lru_wa_fwd", (1, 8, 64, 64)), ("lru_ba_fwd", (1, 512)), ("lru_wx_fwd", (1, 8, 64, 64)), ("lru_bx_fwd", (1, 512)),
    ("lru_lam_fwd", (1, 512)),
    ("lru_wa_bwd", (1, 8, 64, 64)), ("lru_ba_bwd", (1, 512)), ("lru_wx_bwd", (1, 8, 64, 64)), ("lru_bx_bwd", (1, 512)),
    ("lru_lam_bwd", (1, 512)),
    ("rel_bias", (32, 8)), ("norm_rnn_g", (1, 512)), ("norm_attn_g", (1, 512)), ("mlp_norm_g", (1, 1024)),
    ("final_norm_g", (1024,)),
)
PACK_ROW = 8 * 128


def _pack(parts):
    flat = jnp.concatenate([p.reshape(-1) for p in parts])
    pad = (-flat.shape[0]) % PACK_ROW
    return jnp.pad(flat, (0, pad)).reshape(-1, 128)


def _unpack(packed, shapes):
    flat = packed.reshape(-1)
    out, off = [], 0
    for shp in shapes:
        n = int(np.prod(shp))
        out.append(flat[off:off + n].reshape(shp))
        off += n
    return out


WEIGHT_ORDER = ("attn_norm_g", "w_in", "conv_w", "conv_b", "lru_wa_fwd", "lru_ba_fwd", "lru_wx_fwd", "lru_bx_fwd",
                "lru_lam_fwd", "lru_wa_bwd", "lru_ba_bwd", "lru_wx_bwd", "lru_bx_bwd", "lru_lam_bwd", "rel_bias",
                "norm_rnn_g", "norm_attn_g", "w_out", "mlp_norm_g", "w_up", "w_down", "final_norm_g")


def kernel(x, attn_norm_g, w_in, conv_w, conv_b, lru_wa_fwd, lru_ba_fwd, lru_wx_fwd, lru_bx_fwd, lru_lam_fwd, lru_wa_bwd, lru_ba_bwd, lru_wx_bwd, lru_bx_bwd, lru_lam_bwd, rel_bias, norm_rnn_g, norm_attn_g, w_out, mlp_norm_g, w_up, w_down, final_norm_g, loss_target, m_attn_norm_g, m_w_in, m_conv_w, m_conv_b, m_lru_wa_fwd, m_lru_ba_fwd, m_lru_wx_fwd, m_lru_bx_fwd, m_lru_lam_fwd, m_lru_wa_bwd, m_lru_ba_bwd, m_lru_wx_bwd, m_lru_bx_bwd, m_lru_lam_bwd, m_rel_bias, m_norm_rnn_g, m_norm_attn_g, m_w_out, m_mlp_norm_g, m_w_up, m_w_down, m_final_norm_g, v_attn_norm_g, v_w_in, v_conv_w, v_conv_b, v_lru_wa_fwd, v_lru_ba_fwd, v_lru_wx_fwd, v_lru_bx_fwd, v_lru_lam_fwd, v_lru_wa_bwd, v_lru_ba_bwd, v_lru_wx_bwd, v_lru_bx_bwd, v_lru_lam_bwd, v_rel_bias, v_norm_rnn_g, v_norm_attn_g, v_w_out, v_mlp_norm_g, v_w_up, v_w_down, v_final_norm_g):
    given = dict(locals())
    w = {n: given[n] for n in WEIGHT_ORDER}
    m = {n: given["m_" + n] for n in WEIGHT_ORDER}
    v = {n: given["v_" + n] for n in WEIGHT_ORDER}

    chip = lax.axis_index("x") * 2 + lax.axis_index("y")
    core = lax.axis_index("c")

    shards = {n: w[n][0].astype(BF16) for n in BIG}
    shards["conv_w"] = w["conv_w"][0]
    p = {n: (t[0] if t.ndim >= 3 else t) for n, t in w.items() if n not in BIG and n != "conv_w"}
    p["final_norm_g"] = w["final_norm_g"].reshape(1, D_MODEL)

    _, grad_x, small, gathered, big, reduced = _local_step(x[0], loss_target[0], p, shards)

    late = tuple(big)
    grads = [big[n] for n in late]
    others = _run_rider(_pair_exchange_rider(late, grads), "grad_pair_exchange")
    core_arr = core.reshape(1).astype(jnp.int32)
    parts = [_pair_add(core_arr, g, o, "grad_pair_add_" + n) for n, g, o in zip(late, grads, others)]
    landed = _run_rider(_chip_exchange_rider(parts), "grad_chip_exchange")
    halves = [_chip_sum(t, "grad_chip_sum_" + n) for n, t in zip(late, landed)]
    reduced.update(zip(late, _run_rider(_pair_share_rider(late, halves), "grad_pair_share")))

    early_small = [(n, shp) for n, shp in SMALL if n not in small]
    late_small = [(n, shp) for n, shp in SMALL if n in small]
    *early_g, loss = _unpack(_sum_devices(gathered), [shp for _, shp in early_small] + [(1,)])
    late_g = _unpack(_allreduce_small(_pack([small[n].reshape(shp) for n, shp in late_small])),
                     [shp for _, shp in late_small])
    g = dict(zip([n for n, _ in early_small + late_small], early_g + late_g))
    g["conv_w"] = lax.dynamic_slice_in_dim(g["conv_w"], chip * (D_RNN // N_SHARD), D_RNN // N_SHARD, axis=2)
    for n in BIG:
        g[n] = reduced[n][None]

    delta, new_m, new_v = {}, {}, {}
    for n in BIG:
        d2, m2, v2 = _adamw(w[n][0], reduced[n], m[n][0], v[n][0], "adamw_" + n)
        delta[n], new_m[n], new_v[n] = d2[None], m2[None], v2[None]
    names = [n for n, _ in SMALL]
    for dst, src in zip((delta, new_m, new_v), _adamw_many(*[[t[n] for n in names] for t in (w, g, m, v)])):
        dst.update(dict(zip(names, src)))

    return (loss.reshape(()), grad_x[None], *[g[n] for n in WEIGHT_ORDER], *[delta[n] for n in WEIGHT_ORDER],
            *[new_m[n] for n in WEIGHT_ORDER], *[new_v[n] for n in WEIGHT_ORDER])
```

```python
import functools
import math

import numpy as np
import jax
import jax.numpy as jnp
from jax import lax
from jax.experimental import pallas as pl
from jax.experimental.pallas import tpu as pltpu

F32 = jnp.float32
BF16 = jnp.bfloat16

D_MODEL = 1024
D_RNN = 512
D_ATTN = 512
N_HEADS = 8
HEAD_DIM = 64
N_RNN_BLOCKS = 8
RNN_BLOCK = 64
D_IN = 2 * D_RNN + 3 * D_ATTN
D_FF = 4 * D_MODEL
N_SHARD = 4
IN_BLK = D_IN // N_SHARD
OUT_BLK = D_MODEL // N_SHARD
FF_BLK = D_FF // N_SHARD
EPS = 1e-6
NEG_INF = -1e30
LRU_C = 8.0
DILATIONS = (1, 4, 16)
F32_LAYOUT = 4
HALF_WIN = 64
Q_BLK = 128
K_WIN = 256
N_BUCKETS = 32
MAX_DISTANCE = 1024
ATTN_SCALE = HEAD_DIM ** -0.5

ADAM_LR = 0.001
ADAM_B1 = 0.9
ADAM_B2 = 0.999
ADAM_EPS = 1e-08
ADAM_WD = 0.01
ADAM_STEP = 10

TS = 512
TS_RNN_BWD = 1024
TS_MLP = 256
TS_INPROJ_BWD = 512
ATTN_SUB = 32
ATTN_SUB_FWD = 32
TK_DW = 4096
SCAN_UNROLL = 8
SUB = 8
VMEM_LIMIT = 56 * 1024 * 1024
GELU_C0 = math.sqrt(2.0 / math.pi)
GELU_C1 = 0.044715

MESH = pl.DeviceIdType.MESH


def _params(n_grid=1):
    return pltpu.CompilerParams(vmem_limit_bytes=VMEM_LIMIT, dimension_semantics=("arbitrary",) * n_grid)


def _whole_vmem():
    return pl.BlockSpec(memory_space=pltpu.VMEM)


def _rows(width, tile=TS):
    return pl.BlockSpec((tile, width), lambda i: (i, 0))


def _sigmoid(z):
    return 0.5 * jnp.tanh(0.5 * z) + 0.5


def _log1p(u):
    w = 1.0 + u
    return jnp.where(w == 1.0, u, jnp.log(w) * (u / (w - 1.0)))


def _softplus(z):
    return jnp.maximum(z, 0.0) + _log1p(jnp.exp(-jnp.abs(z)))


def _gelu_parts(g):
    inner = GELU_C0 * (g + GELU_C1 * g * g * g)
    t = jnp.tanh(inner)
    val = 0.5 * g * (1.0 + t)
    dinner = GELU_C0 * (1.0 + 3.0 * GELU_C1 * g * g)
    grad = 0.5 * (1.0 + t) + 0.5 * g * (1.0 - t * t) * dinner
    return val, grad


def _rms(x):
    rstd = lax.rsqrt(jnp.mean(x * x, axis=-1, keepdims=True) + EPS)
    return rstd, x * rstd


def _rms_bwd(dy, g, xhat, rstd):
    dxh = dy * g
    dx = rstd * (dxh - xhat * jnp.mean(dxh * xhat, axis=-1, keepdims=True))
    dg = jnp.sum(dy * xhat, axis=0, keepdims=True)
    return dx, dg


def _dot(a, b):
    return jnp.dot(a, b, preferred_element_type=F32)


def _dot_nt(a, b):
    return lax.dot_general(a, b, (((1,), (1,)), ((), ())), preferred_element_type=F32)


def _dot_tn(a, b):
    return lax.dot_general(a, b, (((0,), (0,)), ((), ())), preferred_element_type=F32)


def _shifted(tile, prev8, next8, k):
    n = tile.shape[0]
    row = lax.broadcasted_iota(jnp.int32, tile.shape, 0)
    if k == 0:
        return tile
    if k < 0:
        r = pltpu.roll(tile, -k, 0)
        for j in range(-k):
            r = jnp.where(row == j, prev8[SUB + j + k:SUB + j + k + 1, :], r)
        return r
    r = pltpu.roll(tile, n - k, 0)
    for j in range(k):
        r = jnp.where(row == n - k + j, next8[j:j + 1, :], r)
    return r


def _to_lane_blocks(val, s_ref):
    for j in range(val.shape[1] // 128):
        s_ref[j] = val[:, j * 128:(j + 1) * 128]


def _from_lane_blocks(s_ref):
    return jnp.concatenate([s_ref[j] for j in range(s_ref.shape[0])], axis=-1)


def _class_rows(s_ref, r, dil):
    n = s_ref.shape[1] // dil
    return jnp.concatenate([s_ref[j, pl.ds(r, n, stride=dil), :] for j in range(s_ref.shape[0])], axis=-1)


def _split_classes(val, s_ref, out_ref, dil):
    _to_lane_blocks(val, s_ref)
    for r in range(dil):
        out_ref[r] = _class_rows(s_ref, r, dil).astype(out_ref.dtype)


def _merge_classes(in_ref, s_ref, dil, also_ref=None):
    n = s_ref.shape[1] // dil
    for r in range(dil):
        v = in_ref[r].astype(F32)
        if also_ref is not None:
            v = v + also_ref[r].astype(F32)
        for j in range(s_ref.shape[0]):
            s_ref[j, pl.ds(r, n, stride=dil), :] = v[:, j * 128:(j + 1) * 128]
    return _from_lane_blocks(s_ref)


def _class_spec(dil, tile=TS):
    return pl.BlockSpec((dil, tile // dil, 512), lambda i: (0, i, 0))


def _class_shape(S, dil, dtype):
    return jax.ShapeDtypeStruct((dil, S // dil, 512), dtype)


def _scan_tile(a_ref, b_ref, h_ref, carry_ref, reverse):
    n = a_ref.shape[0]
    width = a_ref.shape[1]
    groups = n // SUB
    row = lax.broadcasted_iota(jnp.int32, (SUB, width), 0)

    def group_scan(g):
        r0 = pl.multiple_of(g * SUB, SUB)
        a = a_ref[pl.ds(r0, SUB), :]
        b = b_ref[pl.ds(r0, SUB), :]
        for s in (1, 2, 4):
            if reverse:
                a_sh = pltpu.roll(a, SUB - s, 0)
                b_sh = pltpu.roll(b, SUB - s, 0)
                m = row < SUB - s
            else:
                a_sh = pltpu.roll(a, s, 0)
                b_sh = pltpu.roll(b, s, 0)
                m = row >= s
            b = jnp.where(m, a * b_sh + b, b)
            a = jnp.where(m, a * a_sh, a)
        return r0, a, b

    def step(i, carry):
        first = i * SCAN_UNROLL
        order = [(groups - 1 - (first + u)) if reverse else (first + u) for u in range(SCAN_UNROLL)]
        scans = [group_scan(g) for g in order]
        for r0, a, b in scans:
            h = b + a * carry
            h_ref[pl.ds(r0, SUB), :] = h
            edge = h[0:1, :] if reverse else h[SUB - 1:SUB, :]
            carry = jnp.broadcast_to(edge, (SUB, width))
        return carry

    carry_ref[...] = lax.fori_loop(0, groups // SCAN_UNROLL, step, carry_ref[...])


def _conv_fwd(xr, prev8, next8, cw, cb):
    y = cb + _shifted(xr, prev8, next8, -2) * cw[0:1, :]
    y = y + _shifted(xr, prev8, next8, -1) * cw[1:2, :]
    y = y + xr * cw[2:3, :]
    y = y + _shifted(xr, prev8, next8, 1) * cw[3:4, :]
    return y


def _lru_gates(xc, wa_ref, ba, wx_ref, bx, lam):
    xcb = xc.astype(BF16)
    r = _sigmoid(_dot(xcb, wa_ref[...]) + ba)
    i = _sigmoid(_dot(xcb, wx_ref[...]) + bx)
    cl = -LRU_C * _softplus(-lam)
    la = cl * r
    a = jnp.exp(la)
    m2 = -jnp.tanh(la) * (a * a + 1.0)
    inv = jnp.where(m2 > 0.0, lax.rsqrt(m2), 0.0)
    mult = m2 * inv
    return xcb, r, i, cl, a, mult, inv


def _inproj_fwd(x, g1, w_in, rider=None):
    S = x.shape[0]

    def body(x_ref, g_ref, w_ref, xr_ref, gate_ref, *rest):
        qkv_refs, s_ref, s4_ref, w_full = rest[:9], rest[9], rest[10], rest[11]

        @pl.when(pl.program_id(0) == 0)
        def _():
            for j in range(N_SHARD):
                w_full[:, j * IN_BLK:(j + 1) * IN_BLK] = w_ref[j]

        _, xh = _rms(x_ref[...])
        h = (xh * g_ref[...]).astype(BF16)
        proj = _dot(h, w_full[...])
        xr_ref[...] = proj[:, 0:512]
        gate_ref[...] = proj[:, 512:1024]
        for t in range(3):
            val = proj[:, 1024 + 512 * t:1536 + 512 * t]
            d1_ref, d4_ref, d16_ref = qkv_refs[3 * t:3 * t + 3]
            d1_ref[0] = val.astype(BF16)
            _to_lane_blocks(val, s_ref)
            for r4 in range(4):
                c4 = _class_rows(s_ref, r4, 4)
                d4_ref[r4] = c4.astype(BF16)
                _to_lane_blocks(c4, s4_ref.at[r4])
            for r4 in range(4):
                for m in range(4):
                    d16_ref[r4 + 4 * m] = _class_rows(s4_ref.at[r4], m, 4).astype(BF16)

    f = jax.ShapeDtypeStruct((S, 512), F32)
    return _call(
        body, "inproj_fwd", (S // TS,),
        [_rows(D_MODEL), _whole_vmem(), _whole_vmem()],
        [_rows(512)] * 2 + [_class_spec(d) for d in DILATIONS] * 3,
        [f, f] + [_class_shape(S, d, BF16) for d in DILATIONS] * 3,
        [pltpu.VMEM((4, TS, 128), F32), pltpu.VMEM((4, 4, TS // 4, 128), F32), pltpu.VMEM((D_MODEL, D_IN), BF16)],
        (x, g1, w_in), rider)


def _halo_specs(S, order, tile=TS):
    per = tile // SUB
    last = S // SUB - 1
    return [
        pl.BlockSpec((tile, 512), lambda i: (order(i), 0)),
        pl.BlockSpec((SUB, 512), lambda i: (jnp.maximum(order(i) * per - 1, 0), 0)),
        pl.BlockSpec((SUB, 512), lambda i: (jnp.minimum((order(i) + 1) * per, last), 0)),
    ]


def _rnn_fwd(xr, conv_w, conv_b, wa, ba, wx, bx, lam, reverse, rider=None, xc=None):
    S = xr.shape[0]
    nt = S // TS
    order = (lambda i: nt - 1 - i) if reverse else (lambda i: i)
    with_conv = xc is None
    n_x = 5 if with_conv else 1
    tile = pl.BlockSpec((TS, 512), lambda i: (order(i), 0))

    def body(*refs):
        wa_ref, ba_ref, wx_ref, bx_ref, lam_ref, h_ref = refs[n_x:n_x + 6]
        a_s, b_s, carry = refs[-3:]
        i = pl.program_id(0)
        t = order(i)

        @pl.when(i == 0)
        def _():
            carry[...] = jnp.zeros_like(carry)

        if with_conv:
            x_ref, xp_ref, xn_ref, cw_ref, cb_ref = refs[:5]
            prev8 = jnp.where(t > 0, xp_ref[...], 0.0)
            next8 = jnp.where(t < nt - 1, xn_ref[...], 0.0)
            xcv = _conv_fwd(x_ref[...], prev8, next8, cw_ref[...], cb_ref[...])
            refs[n_x + 6][...] = xcv
        else:
            xcv = refs[0][...]
        _, _, gi, _, a, mult, _ = _lru_gates(xcv, wa_ref, ba_ref[...], wx_ref, bx_ref[...], lam_ref[...])
        a_s[...] = a
        b_s[...] = mult * (gi * xcv)
        _scan_tile(a_s, b_s, h_ref, carry, reverse)

    f512 = jax.ShapeDtypeStruct((S, 512), F32)
    return _call(
        body, "rnn_fwd_rev" if reverse else "rnn_fwd_fwd", (nt,),
        (_halo_specs(S, order) + [_whole_vmem()] * 2 if with_conv else [tile]) + [_whole_vmem()] * 5,
        [tile, tile] if with_conv else [tile], [f512, f512] if with_conv else [f512],
        [pltpu.VMEM((TS, 512), F32), pltpu.VMEM((TS, 512), F32), pltpu.VMEM((SUB, 512), F32)],
        ((xr, xr, xr, conv_w, conv_b) if with_conv else (xc,)) + (wa, ba, wx, bx, lam), rider)


def _mix_fwd(o3, l3, hf, hb, gate, x, g_rnn, g_attn, w_out):
    S = x.shape[0]

    def body(o1, o2, o3_, l1, l2, l3_, hf_ref, hb_ref, gate_ref, x_ref, gr_ref, ga_ref, w_ref,
             x1_ref, mix_ref, ya1, ya2, ls1, ls2, s_ref):
        la, lb, lc = l1[0], _merge_classes(l2, s_ref, F32_LAYOUT), _merge_classes(l3_, s_ref, F32_LAYOUT)
        m = jnp.maximum(jnp.maximum(la, lb), lc)
        ea, eb, ec = jnp.exp(la - m), jnp.exp(lb - m), jnp.exp(lc - m)
        den = ea + eb + ec
        lse = m + jnp.log(den)
        ya = (ea * o1[0] + eb * _merge_classes(o2, s_ref, F32_LAYOUT) + ec * _merge_classes(o3_, s_ref, F32_LAYOUT)) / den
        ya1[0] = ya
        ls1[0] = lse
        _split_classes(ya, s_ref, ya2, F32_LAYOUT)
        _split_classes(lse, s_ref, ls2, F32_LAYOUT)
        gg, _ = _gelu_parts(gate_ref[...])
        yr = (hf_ref[...] + hb_ref[...]) * gg
        _, xh_r = _rms(yr)
        _, xh_a = _rms(ya)
        mix = jnp.concatenate([xh_r * gr_ref[...], xh_a * ga_ref[...]], axis=-1).astype(BF16)
        mix_ref[...] = mix
        acc = x_ref[...]
        for j in range(N_SHARD):
            acc = acc + _dot(mix[:, j * OUT_BLK:(j + 1) * OUT_BLK], w_ref[j])
        x1_ref[...] = acc

    one, four = _class_spec(1), _class_spec(F32_LAYOUT)
    return pl.pallas_call(
        body, grid=(S // TS,), name="mix_fwd",
        in_specs=[one, four, four] * 2 + [_rows(512)] * 3 + [_rows(D_MODEL)] + [_whole_vmem()] * 3,
        out_specs=[_rows(D_MODEL), _rows(D_MODEL)] + [one, four] * 2,
        out_shape=[jax.ShapeDtypeStruct((S, D_MODEL), F32), jax.ShapeDtypeStruct((S, D_MODEL), BF16)]
        + [_class_shape(S, 1, F32), _class_shape(S, F32_LAYOUT, F32)] * 2,
        scratch_shapes=[pltpu.VMEM((4, TS, 128), F32)],
        compiler_params=_params(),
    )(*o3, *l3, hf, hb, gate, x, g_rnn, g_attn, w_out)


def _mlp_fwd_bwd(x1, target, g_mlp, g_fin, w_up, w_down):
    S = x1.shape[0]
    tm = TS_MLP

    def body(x1_ref, t_ref, gm_ref, gf_ref, wu_ref, wd_ref,
             dx1_ref, h2_ref, a2_ref, du_ref, dx2_ref, loss_ref, dgf_ref, dgm_ref, relu_s):
        @pl.when(pl.program_id(0) == 0)
        def _():
            loss_ref[...] = jnp.zeros_like(loss_ref)
            dgf_ref[...] = jnp.zeros_like(dgf_ref)
            dgm_ref[...] = jnp.zeros_like(dgm_ref)

        x1v = x1_ref[...]
        rstd1, xh1 = _rms(x1v)
        h2 = (xh1 * gm_ref[...]).astype(BF16)
        h2_ref[...] = h2
        x2 = x1v
        for j in range(N_SHARD):
            r = jnp.maximum(_dot(h2, wu_ref[j]), 0.0)
            relu_s[j] = r
            a2 = (r * r).astype(BF16)
            a2_ref[:, j * FF_BLK:(j + 1) * FF_BLK] = a2
            x2 = x2 + _dot(a2, wd_ref[j])
        rstd2, xh2 = _rms(x2)
        err = xh2 * gf_ref[...] - t_ref[...]
        loss_ref[...] += jnp.sum(err * err, axis=0, keepdims=True)
        dy = err * (1.0 / D_MODEL)
        dx2, dgf = _rms_bwd(dy, gf_ref[...], xh2, rstd2)
        dgf_ref[...] += dgf
        dx2b = dx2.astype(BF16)
        dx2_ref[...] = dx2b
        dh2 = jnp.zeros((tm, D_MODEL), F32)
        for j in range(N_SHARD):
            du = (_dot_nt(dx2b, wd_ref[j]) * (2.0 * relu_s[j])).astype(BF16)
            du_ref[:, j * FF_BLK:(j + 1) * FF_BLK] = du
            dh2 = dh2 + _dot_nt(du, wu_ref[j])
        dx1n, dgm = _rms_bwd(dh2, gm_ref[...], xh1, rstd1)
        dgm_ref[...] += dgm
        dx1_ref[...] = dx2 + dx1n

    vec = jax.ShapeDtypeStruct((1, D_MODEL), F32)
    return pl.pallas_call(
        body, grid=(S // tm,), name="mlp_fwd_bwd",
        in_specs=[_rows(D_MODEL, tm), _rows(D_MODEL, tm)] + [_whole_vmem()] * 4,
        out_specs=[_rows(D_MODEL, tm), _rows(D_MODEL, tm), _rows(D_FF, tm), _rows(D_FF, tm), _rows(D_MODEL, tm)]
        + [_whole_vmem()] * 3,
        out_shape=[jax.ShapeDtypeStruct((S, D_MODEL), F32), jax.ShapeDtypeStruct((S, D_MODEL), BF16),
                   jax.ShapeDtypeStruct((S, D_FF), BF16), jax.ShapeDtypeStruct((S, D_FF), BF16),
                   jax.ShapeDtypeStruct((S, D_MODEL), BF16), vec, vec, vec],
        scratch_shapes=[pltpu.VMEM((N_SHARD, tm, FF_BLK), F32)],
        compiler_params=_params(),
    )(x1, target, g_mlp, g_fin, w_up, w_down)


def _mix_bwd(dx1, w_out, mixb, ya, hf, hb, gate, g_rnn, g_attn):
    S = dx1.shape[0]

    def body(dx1_ref, w_ref, mix_ref, ya_ref, hf_ref, hb_ref, gate_ref, gr_ref, ga_ref,
             dhs_ref, dgate_ref, dya1, dya2, dw_ref, dgr_ref, dga_ref, s_ref):
        @pl.when(pl.program_id(0) == 0)
        def _():
            dw_ref[...] = jnp.zeros_like(dw_ref)
            dgr_ref[...] = jnp.zeros_like(dgr_ref)
            dga_ref[...] = jnp.zeros_like(dga_ref)

        dx1b = dx1_ref[...].astype(BF16)
        mix = mix_ref[...]
        for j in range(N_SHARD):
            dw_ref[j] += _dot_tn(mix[:, j * OUT_BLK:(j + 1) * OUT_BLK], dx1b)
        dmix = jnp.concatenate([_dot_nt(dx1b, w_ref[j]) for j in range(N_SHARD)], axis=-1)
        gg, dgg = _gelu_parts(gate_ref[...])
        hs = hf_ref[...] + hb_ref[...]
        rstd_r, xh_r = _rms(hs * gg)
        dyr, dgr = _rms_bwd(dmix[:, 0:D_RNN], gr_ref[...], xh_r, rstd_r)
        dgr_ref[...] += dgr
        rstd_a, xh_a = _rms(ya_ref[0])
        dya, dga = _rms_bwd(dmix[:, D_RNN:], ga_ref[...], xh_a, rstd_a)
        dga_ref[...] += dga
        dya1[0] = dya
        _split_classes(dya, s_ref, dya2, F32_LAYOUT)
        dhs_ref[...] = dyr * gg
        dgate_ref[...] = dyr * hs * dgg

    f512 = jax.ShapeDtypeStruct((S, 512), F32)
    vec = jax.ShapeDtypeStruct((1, 512), F32)
    return pl.pallas_call(
        body, grid=(S // TS,), name="mix_bwd",
        in_specs=[_rows(D_MODEL), _whole_vmem(), _rows(D_MODEL), _class_spec(1)] + [_rows(512)] * 3 + [_whole_vmem()] * 2,
        out_specs=[_rows(512)] * 2 + [_class_spec(1), _class_spec(F32_LAYOUT)] + [_whole_vmem()] * 3,
        out_shape=[f512, f512, _class_shape(S, 1, F32), _class_shape(S, F32_LAYOUT, F32),
                   jax.ShapeDtypeStruct((N_SHARD, OUT_BLK, D_MODEL), F32), vec, vec],
        scratch_shapes=[pltpu.VMEM((4, TS, 128), F32)],
        compiler_params=_params(),
    )(dx1, w_out, mixb, ya, hf, hb, gate, g_rnn, g_attn)


def _rnn_bwd(xc, h, dhs, wa, ba, wx, bx, lam, reverse, rider=None):
    S = xc.shape[0]
    nt = S // TS_RNN_BWD
    order = (lambda i: i) if reverse else (lambda i: nt - 1 - i)
    per = TS_RNN_BWD // SUB
    last = S // SUB - 1
    if reverse:
        h_halo = pl.BlockSpec((SUB, 512), lambda i: (jnp.minimum((order(i) + 1) * per, last), 0))
    else:
        h_halo = pl.BlockSpec((SUB, 512), lambda i: (jnp.maximum(order(i) * per - 1, 0), 0))
    tile = pl.BlockSpec((TS_RNN_BWD, 512), lambda i: (order(i), 0))

    def body(xc_ref, h_ref, hh_ref, dh_ref, wa_ref, ba_ref, wx_ref, bx_ref, lam_ref,
             dxc_ref, dwa_ref, dwx_ref, dvec_ref, a_s, g_s, carry, edge):
        i = pl.program_id(0)
        t = order(i)

        @pl.when(i == 0)
        def _():
            carry[...] = jnp.zeros_like(carry)
            edge[...] = jnp.zeros_like(edge)
            dwa_ref[...] = jnp.zeros_like(dwa_ref)
            dwx_ref[...] = jnp.zeros_like(dwx_ref)
            dvec_ref[...] = jnp.zeros_like(dvec_ref)

        xc = xc_ref[...]
        xcb, r, gi, cl, a, mult, inv_mult = _lru_gates(xc, wa_ref, ba_ref[...], wx_ref, bx_ref[...], lam_ref[...])
        hv = h_ref[...]
        if reverse:
            a_s[...] = _shifted(a, edge[...], None, -1)
            edge[...] = a[TS_RNN_BWD - SUB:TS_RNN_BWD, :]
            hh = jnp.where(t < nt - 1, hh_ref[...], 0.0)
            h_prev = _shifted(hv, None, hh, 1)
        else:
            a_s[...] = _shifted(a, None, edge[...], 1)
            edge[...] = a[0:SUB, :]
            hh = jnp.where(t > 0, hh_ref[...], 0.0)
            h_prev = _shifted(hv, hh, None, -1)
        _scan_tile(a_s, dh_ref, g_s, carry, not reverse)
        g = g_s[...]
        da = g * h_prev
        gm = g * mult
        d_i = gm * xc
        dmult = g * gi * xc
        dla = da * a - dmult * (a * a) * inv_mult
        d_r = dla * cl
        dpre_r = d_r * r * (1.0 - r)
        dpre_i = d_i * gi * (1.0 - gi)
        dprb = dpre_r.astype(BF16)
        dpib = dpre_i.astype(BF16)
        dwa_ref[...] += _dot_tn(xcb, dprb)
        dwx_ref[...] += _dot_tn(xcb, dpib)
        dvec_ref[0:1, :] += jnp.sum(dpre_r, axis=0, keepdims=True)
        dvec_ref[1:2, :] += jnp.sum(dpre_i, axis=0, keepdims=True)
        dvec_ref[2:3, :] += jnp.sum(dla * r, axis=0, keepdims=True)
        dvec_ref[3:4, :] = dvec_ref[2:3, :] * (LRU_C * _sigmoid(-lam_ref[...]))
        dxc_ref[...] = gm * gi + _dot_nt(dprb, wa_ref[...]) + _dot_nt(dpib, wx_ref[...])

    sq = jax.ShapeDtypeStruct((D_RNN, D_RNN), F32)
    return _call(
        body, "rnn_bwd_rev" if reverse else "rnn_bwd_fwd", (nt,),
        [tile, tile, h_halo, tile] + [_whole_vmem()] * 5,
        [tile, _whole_vmem(), _whole_vmem(), _whole_vmem()],
        [jax.ShapeDtypeStruct((S, 512), F32), sq, sq, jax.ShapeDtypeStruct((SUB, 512), F32)],
        [pltpu.VMEM((TS_RNN_BWD, 512), F32), pltpu.VMEM((TS_RNN_BWD, 512), F32), pltpu.VMEM((SUB, 512), F32),
         pltpu.VMEM((SUB, 512), F32)],
        (xc, h, h, dhs, wa, ba, wx, bx, lam), rider)


def _inproj_bwd(x, dx1, xr, dxc_f, dxc_b, dgate, dq3, dk3, dv3, g1, conv_w, w_in, rider=None):
    S = x.shape[0]
    tb = TS_INPROJ_BWD
    nt = S // tb
    ident = lambda i: i

    def body(x_ref, dx1_ref, xr_ref, xrp_ref, xrn_ref, cf_ref, cfp_ref, cfn_ref, cb_ref, cbp_ref, cbn_ref, dgate_ref,
             dq1, dq2, dq3_, dk1, dk2, dk3_, dv1, dv2, dv3_, g_ref, cw_ref, w_ref,
             dx_ref, dw_ref, dg_ref, dcw_ref, s_ref, w_full, dw_full, sems):
        i = pl.program_id(0)

        def blocks(full, blocked, k0):
            return [(full.at[:, j * IN_BLK:(j + 1) * IN_BLK], blocked.at[j], sems.at[k0 + j]) for j in range(N_SHARD)]

        @pl.when(i == 0)
        def _():
            copies = [pltpu.make_async_copy(src, dst, sem) for dst, src, sem in blocks(w_full, w_ref, 0)]
            for cp in copies:
                cp.start()
            for cp in copies:
                cp.wait()
            dw_full[...] = jnp.zeros_like(dw_full)
            dg_ref[...] = jnp.zeros_like(dg_ref)
            dcw_ref[...] = jnp.zeros_like(dcw_ref)

        first, last = i > 0, i < nt - 1
        dxc = cf_ref[...] + cb_ref[...]
        dxc_p = jnp.where(first, cfp_ref[...] + cbp_ref[...], 0.0)
        dxc_n = jnp.where(last, cfn_ref[...] + cbn_ref[...], 0.0)
        cw = cw_ref[...]
        dxr = (_shifted(dxc, dxc_p, dxc_n, 2) * cw[0:1, :] + _shifted(dxc, dxc_p, dxc_n, 1) * cw[1:2, :]
               + dxc * cw[2:3, :] + _shifted(dxc, dxc_p, dxc_n, -1) * cw[3:4, :])
        xrv = xr_ref[...]
        xr_p = jnp.where(first, xrp_ref[...], 0.0)
        xr_n = jnp.where(last, xrn_ref[...], 0.0)
        for k, off in enumerate((-2, -1, 0, 1)):
            dcw_ref[k:k + 1, :] += jnp.sum(dxc * _shifted(xrv, xr_p, xr_n, off), axis=0, keepdims=True)
        dcw_ref[4:5, :] += jnp.sum(dxc, axis=0, keepdims=True)

        def total(a, b, c_):
            return a[0].astype(F32) + _merge_classes(b, s_ref, F32_LAYOUT, c_)

        dproj = jnp.concatenate(
            [dxr, dgate_ref[...], total(dq1, dq2, dq3_), total(dk1, dk2, dk3_), total(dv1, dv2, dv3_)],
            axis=-1).astype(BF16)
        xv = x_ref[...]
        rstd, xh = _rms(xv)
        hb = (xh * g_ref[...]).astype(BF16)
        dh = _dot_nt(dproj, w_full[...])
        dw_full[...] += _dot_tn(hb, dproj)
        dxn, dg = _rms_bwd(dh, g_ref[...], xh, rstd)
        dg_ref[...] += dg
        dx_ref[...] = dx1_ref[...] + dxn

        @pl.when(i == nt - 1)
        def _():
            copies = [pltpu.make_async_copy(src, dst, sem) for src, dst, sem in blocks(dw_full, dw_ref, N_SHARD)]
            for cp in copies:
                cp.start()
            for cp in copies:
                cp.wait()

    halo = _halo_specs(S, ident, tb)
    return _call(
        body, "inproj_bwd", (nt,),
        [_rows(D_MODEL, tb), _rows(D_MODEL, tb)] + halo * 3 + [_rows(512, tb)]
        + [_class_spec(1, tb), _class_spec(F32_LAYOUT, tb), _class_spec(F32_LAYOUT, tb)] * 3 + [_whole_vmem()] * 2 + [ANY],
        [_rows(D_MODEL, tb), ANY, _whole_vmem(), _whole_vmem()],
        [jax.ShapeDtypeStruct((S, D_MODEL), F32), jax.ShapeDtypeStruct((N_SHARD, D_MODEL, IN_BLK), F32),
         jax.ShapeDtypeStruct((1, D_MODEL), F32), jax.ShapeDtypeStruct((SUB, 512), F32)],
        [pltpu.VMEM((4, tb, 128), F32), pltpu.VMEM((D_MODEL, D_IN), BF16), pltpu.VMEM((D_MODEL, D_IN), F32),
         pltpu.SemaphoreType.DMA((2 * N_SHARD,))],
        (x, dx1, xr, xr, xr, dxc_f, dxc_f, dxc_f, dxc_b, dxc_b, dxc_b, dgate, *dq3, *dk3, *dv3, g1, conv_w, w_in), rider)


def _dw_matmul(a, b, a_cols, b_cols, name):
    S = a.shape[0]
    tk = min(S, TK_DW)
    a_shared = a.shape[1] == a_cols
    b_shared = b.shape[1] == b_cols

    def body(a_ref, b_ref, o_ref):
        @pl.when(pl.program_id(1) == 0)
        def _():
            o_ref[...] = jnp.zeros_like(o_ref)
        o_ref[0] += _dot_tn(a_ref[...], b_ref[...])

    return pl.pallas_call(
        body, grid=(N_SHARD, S // tk), name=name,
        in_specs=[pl.BlockSpec((tk, a_cols), (lambda j, k: (k, 0)) if a_shared else (lambda j, k: (k, j))),
                  pl.BlockSpec((tk, b_cols), (lambda j, k: (k, 0)) if b_shared else (lambda j, k: (k, j)))],
        out_specs=pl.BlockSpec((1, a_cols, b_cols), lambda j, k: (j, 0, 0)),
        out_shape=jax.ShapeDtypeStruct((N_SHARD, a_cols, b_cols), F32),
        compiler_params=_params(2),
    )(a, b)


def _t5_bucket_np(rel):
    nb = N_BUCKETS // 2
    max_exact = nb // 2
    ret = np.where(rel > 0, nb, 0)
    n = np.abs(rel)
    nf = np.maximum(n, 1).astype(np.float32)
    large = max_exact + (np.log(nf / np.float32(max_exact)) / np.float32(math.log(MAX_DISTANCE / max_exact))
                         * np.float32(nb - max_exact)).astype(np.int32)
    large = np.minimum(large, nb - 1)
    return ret + np.where(n < max_exact, n, large)


_VARIANT_OFFSETS = (-HALF_WIN,) * 3


def _band_index():
    kk = np.arange(K_WIN)[None, :]
    ql = np.arange(Q_BLK)[:, None]
    rel = np.stack([kk - ql + off for off in _VARIANT_OFFSETS])
    band = np.abs(rel) <= HALF_WIN
    inside = np.stack([np.broadcast_to(kk >= HALF_WIN, band[0].shape), np.ones_like(band[0]),
                       np.broadcast_to(kk < K_WIN - HALF_WIN, band[0].shape)])
    return rel, band & inside


def _bucket_tables(dil):
    rel, valid = _band_index()
    bucket = _t5_bucket_np(np.clip(rel, -HALF_WIN, HALF_WIN) * dil)
    return np.where(valid, bucket, -1).astype(np.int32)


def _bias_mats(rel_bias, rider=None):
    tables = [_bucket_tables(d) for d in DILATIONS]
    used = [sorted(set(t[t >= 0].tolist())) for t in tables]

    def one_pattern(rb_ref, t_ref, o_ref, buckets):
        bk = t_ref[1]
        for h in range(N_HEADS):
            acc = jnp.full((Q_BLK, K_WIN), NEG_INF, F32)
            for b in buckets:
                acc = jnp.where(bk == b, rb_ref[b, h], acc)
            o_ref[1, h] = acc
            for var in (0, 2):
                o_ref[var, h] = jnp.where(t_ref[var] >= 0, acc, NEG_INF)

    def body(rb_ref, t1, t2, t3, o1, o2, o3):
        for i, (t_ref, o_ref) in enumerate(((t1, o1), (t2, o2), (t3, o3))):
            pl.when(pl.program_id(0) == i)(functools.partial(one_pattern, rb_ref, t_ref, o_ref, used[i]))

    shp = jax.ShapeDtypeStruct((3, N_HEADS, Q_BLK, K_WIN), F32)
    return _call(
        body, "bias_tables", (len(DILATIONS),), [pl.BlockSpec(memory_space=pltpu.SMEM)] + [_whole_vmem()] * 3,
        [_whole_vmem()] * 3, [shp] * 3, [], (rel_bias, *[jnp.asarray(t) for t in tables]), rider)


def _variant(qb, nq):
    return jnp.where(qb == 0, 0, jnp.where(qb == nq - 1, 2, 1))


def _win_start(qb):
    return pl.multiple_of(qb * Q_BLK, Q_BLK)


def _fill_padded(src_ref, pad_ref):
    L = src_ref.shape[0]
    edge = jnp.zeros((HALF_WIN, 128), pad_ref.dtype)
    pad_ref[0:HALF_WIN, :] = edge
    pad_ref[HALF_WIN:HALF_WIN + L, :] = src_ref[...]
    pad_ref[HALF_WIN + L:2 * HALF_WIN + L, :] = edge


INNER = {1: 1, 4: 1, 16: 4}


def _attn_layout(dil, L, blocks=ATTN_SUB, outer=None):
    inner = INNER[dil]
    n_outer = dil // inner
    nsub = min(blocks // inner, L // Q_BLK)
    qt = nsub * Q_BLK
    grid = (4, n_outer, L // qt)
    qspec = pl.BlockSpec((inner, None, qt, 128), lambda hp, r, s: (0, r, s, hp))
    kspec = pl.BlockSpec((inner, None, L, 128), lambda hp, r, s: (0, r, 0, hp))
    bspec = pl.BlockSpec((3, 2, Q_BLK, K_WIN), lambda hp, r, s: (0, hp, 0, 0))
    kfspec = pl.BlockSpec((None, inner * L, 128), lambda hp, r, s: (r, 0, hp))
    qfspec = kfspec if inner > 1 else pl.BlockSpec((None, qt, 128), lambda hp, r, s: (r, s, hp))
    fshape = jax.ShapeDtypeStruct((n_outer, inner * L, D_ATTN), F32)
    view = lambda t: t.reshape(inner, n_outer, L, D_ATTN)

    def qrows(m, sub):
        if inner == 1:
            return (slice(sub * Q_BLK, (sub + 1) * Q_BLK), slice(None))
        first = (pl.program_id(2) * nsub + sub) * Q_BLK
        return (pl.ds(m + inner * first, Q_BLK, stride=inner), slice(None))

    def krows(m):
        if inner == 1:
            return (slice(None), slice(None))
        return (pl.ds(m, L, stride=inner), slice(None))

    if outer is not None:
        grid = (4, n_outer // outer, L // qt)
        qspec = pl.BlockSpec((inner, outer, qt, 128), lambda hp, r, s: (0, r, s, hp))
        kspec = pl.BlockSpec((inner, outer, L, 128), lambda hp, r, s: (0, r, 0, hp))
        kfspec = pl.BlockSpec((outer, inner * L, 128), lambda hp, r, s: (r, 0, hp))
        qfspec = kfspec if inner > 1 else pl.BlockSpec((outer, qt, 128), lambda hp, r, s: (r, s, hp))
    return inner, nsub, grid, qspec, kspec, bspec, qfspec, kfspec, fshape, view, qrows, krows


def _head_masks():
    lane = lax.broadcasted_iota(jnp.int32, (Q_BLK, 128), 1)
    return lane < HEAD_DIM


def _attn_fwd(q, k, v, bias):
    dil, L, _ = q.shape
    nq = L // Q_BLK
    n_cls = min(dil // INNER[dil], 2)
    inner, nsub, grid, qspec, kspec, bspec, qfspec, kfspec, fshape, view, qrows, krows = _attn_layout(
        dil, L, ATTN_SUB_FWD, n_cls)

    def body(q_ref, k_ref, v_ref, b_ref, o_ref, l_ref, kp, vp):
        step = pl.program_id(2)

        @pl.when(step == 0)
        def _():
            for c in range(n_cls):
                for m in range(inner):
                    _fill_padded(k_ref.at[m, c], kp.at[c * inner + m])
                    _fill_padded(v_ref.at[m, c], vp.at[c * inner + m])

        h0 = _head_masks()
        for c, m, sub in [(c, m, sub) for c in range(n_cls) for m in range(inner) for sub in range(nsub)]:
            qb = step * nsub + sub
            st = _win_start(qb)
            var = _variant(qb, nq)
            kw = kp[c * inner + m, pl.ds(st, K_WIN), :]
            vw = vp[c * inner + m, pl.ds(st, K_WIN), :]
            qs = q_ref[m, c, sub * Q_BLK:(sub + 1) * Q_BLK, :] * ATTN_SCALE
            zq = jnp.zeros_like(qs)
            q2 = jnp.concatenate([jnp.where(h0, qs, zq), jnp.where(h0, zq, qs)], axis=0)
            s = _dot_nt(q2, kw) + b_ref[var].reshape(2 * Q_BLK, K_WIN)
            top = jnp.max(s, axis=-1, keepdims=True)
            p = jnp.exp(s - top)
            l = jnp.sum(p, axis=-1, keepdims=True)
            out = _dot(p.astype(BF16), vw) / l
            lse = top + jnp.log(l)
            o_ref.at[c][qrows(m, sub)] = jnp.where(h0, out[0:Q_BLK], out[Q_BLK:2 * Q_BLK])
            l_ref.at[c][qrows(m, sub)] = jnp.where(h0, lse[0:Q_BLK], lse[Q_BLK:2 * Q_BLK])

    return pl.pallas_call(
        body, grid=grid, name=f"attn_fwd_d{dil}",
        in_specs=[qspec, kspec, kspec, bspec], out_specs=[qfspec, qfspec], out_shape=[fshape, fshape],
        scratch_shapes=[pltpu.VMEM((n_cls * inner, L + 2 * HALF_WIN, 128), BF16)] * 2,
        compiler_params=_params(3),
    )(view(q), view(k), view(v), bias)


def _attn_bwd(q, k, v, bias, do, o, lse, rider=None):
    dil, L, _ = q.shape
    nq = L // Q_BLK
    inner, nsub, grid, qspec, kspec, bspec, qfspec, kfspec, fshape, view, qrows, krows = _attn_layout(dil, L)
    nstep = grid[2]

    def body(q_ref, k_ref, v_ref, b_ref, do_ref, o_ref, l_ref, dq_ref, dk_ref, dv_ref, db_ref, db_s,
             kp, vp, dkp, dvp, carry):
        hp, step = pl.program_id(0), pl.program_id(2)
        first = (hp == 0) & (pl.program_id(1) == 0) & (step == 0)
        last = (hp == grid[0] - 1) & (pl.program_id(1) == grid[1] - 1) & (step == nstep - 1)

        @pl.when(first)
        def _():
            db_s[...] = jnp.zeros_like(db_s)

        @pl.when(step == 0)
        def _():
            for m in range(inner):
                _fill_padded(k_ref.at[m], kp.at[m])
                _fill_padded(v_ref.at[m], vp.at[m])
            carry[...] = jnp.zeros_like(carry)

        h0 = _head_masks()
        for m, sub in [(m, sub) for m in range(inner) for sub in range(nsub)]:
            if sub == 0:
                carry_k, carry_v = carry[m, 0], carry[m, 1]
            qb = step * nsub + sub
            st = _win_start(qb)
            var = _variant(qb, nq)
            kw = kp[m, pl.ds(st, K_WIN), :]
            vw = vp[m, pl.ds(st, K_WIN), :]
            qs = q_ref[m, sub * Q_BLK:(sub + 1) * Q_BLK, :] * ATTN_SCALE
            dof = do_ref[qrows(m, sub)]
            dob = dof.astype(BF16)
            prod = dof * o_ref[qrows(m, sub)]
            lsev = l_ref[qrows(m, sub)]
            zq, zd = jnp.zeros_like(qs), jnp.zeros_like(dob)
            q2 = jnp.concatenate([jnp.where(h0, qs, zq), jnp.where(h0, zq, qs)], axis=0)
            do2 = jnp.concatenate([jnp.where(h0, dob, zd), jnp.where(h0, zd, dob)], axis=0)
            lse2 = jnp.concatenate([lsev[:, 0:1], lsev[:, HEAD_DIM:HEAD_DIM + 1]], axis=0)
            dd2 = jnp.concatenate([jnp.sum(jnp.where(h0, prod, 0.0), axis=-1, keepdims=True),
                                   jnp.sum(jnp.where(h0, 0.0, prod), axis=-1, keepdims=True)], axis=0)
            s = _dot_nt(q2, kw) + b_ref[var].reshape(2 * Q_BLK, K_WIN)
            p = jnp.exp(s - lse2)
            ds = p * (_dot_nt(do2, vw) - dd2)
            db_s[pl.ds(hp * 2, 2)] += ds.reshape(2, Q_BLK, K_WIN)
            dsb = ds.astype(BF16)
            dv_acc = _dot_tn(p.astype(BF16), do2)
            dk_acc = _dot_tn(dsb, q2)
            dq2 = _dot(dsb, kw) * ATTN_SCALE
            dq_ref[qrows(m, sub)] = jnp.where(h0, dq2[0:Q_BLK], dq2[Q_BLK:2 * Q_BLK]).astype(dq_ref.dtype)
            dkp[m, pl.ds(st, Q_BLK), :] = carry_k + dk_acc[0:Q_BLK]
            dvp[m, pl.ds(st, Q_BLK), :] = carry_v + dv_acc[0:Q_BLK]
            carry_k, carry_v = dk_acc[Q_BLK:K_WIN], dv_acc[Q_BLK:K_WIN]
            if sub == nsub - 1:
                carry[m, 0] = carry_k
                carry[m, 1] = carry_v

        @pl.when(step == nstep - 1)
        def _():
            for m in range(inner):
                dkp[m, L:L + Q_BLK, :] = carry[m, 0]
                dvp[m, L:L + Q_BLK, :] = carry[m, 1]
                dk_ref[krows(m)] = dkp[m, HALF_WIN:HALF_WIN + L, :].astype(dk_ref.dtype)
                dv_ref[krows(m)] = dvp[m, HALF_WIN:HALF_WIN + L, :].astype(dv_ref.dtype)

        @pl.when(last)
        def _():
            db_ref[...] = db_s[...]

    dbshape = (N_HEADS, Q_BLK, K_WIN)
    gshape = jax.ShapeDtypeStruct(fshape.shape, BF16 if inner == 1 else F32)
    return _call(
        body, f"attn_bwd_d{dil}", grid,
        [qspec, kspec, kspec, bspec, qfspec, qfspec, qfspec],
        [qfspec, kfspec, kfspec, _whole_vmem()],
        [gshape, gshape, gshape, jax.ShapeDtypeStruct(dbshape, F32)],
        [pltpu.VMEM(dbshape, F32)] + [pltpu.VMEM((inner, L + 2 * HALF_WIN, 128), BF16)] * 2
        + [pltpu.VMEM((inner, L + 2 * HALF_WIN, 128), F32)] * 2 + [pltpu.VMEM((inner, 2, Q_BLK, 128), F32)],
        (view(q), view(k), view(v), bias, do, o, lse), rider)


def _bucket_onehots(dil):
    m = np.zeros((3, K_WIN, N_BUCKETS), np.float32)
    for var, off in enumerate(_VARIANT_OFFSETS):
        for rel in range(-HALF_WIN, HALF_WIN + 1):
            col = (rel - off + Q_BLK - 1) % K_WIN
            m[var, col, int(_t5_bucket_np(np.asarray(rel * dil)))] = 1.0
    return jnp.asarray(m)


def _bias_grad(dbs):
    onehots = [_bucket_onehots(d) for d in DILATIONS]
    flip = jnp.asarray(np.eye(Q_BLK, dtype=np.float32)[::-1].copy())

    def body(d1, d2, d3, m1, m2, m3, flip_ref, out_ref):
        hp = lax.Precision.HIGHEST
        acc = jnp.zeros((N_HEADS, N_BUCKETS), F32)
        for d_ref, m_ref in ((d1, m1), (d2, m2), (d3, m3)):
            rows = []
            for h in range(N_HEADS):
                xrev = jnp.dot(flip_ref[...], d_ref[h], precision=hp, preferred_element_type=F32)
                y = pltpu.roll(xrev, 0, 1, stride=1, stride_axis=0)
                rows.append(jnp.sum(y, axis=0, keepdims=True))
            acc = acc + jnp.dot(jnp.concatenate(rows, axis=0), m_ref[1], precision=hp, preferred_element_type=F32)
        out_ref[...] = acc

    return pl.pallas_call(
        body, name="bias_grad", out_shape=jax.ShapeDtypeStruct((N_HEADS, N_BUCKETS), F32),
        compiler_params=_params(0),
    )(*dbs, *onehots, flip)


def _block_diag(w):
    eye = jnp.eye(N_RNN_BLOCKS, dtype=w.dtype)
    return jnp.einsum("ncd,nm->ncmd", w, eye).reshape(D_RNN, D_RNN).astype(BF16)


def _diag_blocks(dense):
    d = dense.reshape(N_RNN_BLOCKS, RNN_BLOCK, N_RNN_BLOCKS, RNN_BLOCK)
    return jnp.stack([d[n, :, n, :] for n in range(N_RNN_BLOCKS)])


EARLY = ("w_out", "w_up", "w_down")


def _local_step(x, target, p, shards=None):
    p = dict(p)
    first = None if shards is None else _gather_rider(["w_in"], [shards["w_in"]], shards["conv_w"])
    biases, got = _bias_mats(p["rel_bias"], first)
    if shards is not None:
        p["w_in"] = got[0]
        p["conv_w"] = jnp.transpose(got[1], (1, 0, 2)).reshape(4, D_RNN)
    lru = {}
    for dname in ("fwd", "bwd"):
        lru[dname] = (_block_diag(p["lru_wa_" + dname]), p["lru_ba_" + dname], _block_diag(p["lru_wx_" + dname]),
                      p["lru_bx_" + dname], p["lru_lam_" + dname])

    def gather(name):
        return None if shards is None else _gather_rider([name], [shards[name]])

    (xr, gate, *qkv), got = _inproj_fwd(x, p["attn_norm_g"], p["w_in"], gather("w_out"))
    p.update(zip(["w_out"], got))
    qs, ks, vs = qkv[0:3], qkv[3:6], qkv[6:9]
    (hf, xc), got = _rnn_fwd(xr, p["conv_w"], p["conv_b"], *lru["fwd"], reverse=False, rider=gather("w_up"))
    p.update(zip(["w_up"], got))
    (hb,), got = _rnn_fwd(xr, p["conv_w"], p["conv_b"], *lru["bwd"], reverse=True, rider=gather("w_down"), xc=xc)
    p.update(zip(["w_down"], got))
    outs, lses = [], []
    for q, k, v, bias in zip(qs, ks, vs, biases):
        o, l = _attn_fwd(q, k, v, bias)
        outs.append(o)
        lses.append(l)
    x1, mixb, *yl = _mix_fwd(outs, lses, hf, hb, gate, x, p["norm_rnn_g"], p["norm_attn_g"], p["w_out"])
    yas, lsts = [yl[0], yl[1], yl[1]], [yl[2], yl[3], yl[3]]
    dx1, h2b, a2b, dub, dx2b, loss_vec, dg_fin, dg_mlp = _mlp_fwd_bwd(
        x1, target, p["mlp_norm_g"], p["final_norm_g"], p["w_up"], p["w_down"])
    dhs, dgate, dya1, dya4, dw_out, dg_rnn, dg_attn = _mix_bwd(dx1, p["w_out"], mixb, yas[0], hf, hb, gate,
                                                               p["norm_rnn_g"], p["norm_attn_g"])
    dyas = [dya1, dya4, dya4]
    dw_up = _dw_matmul(h2b, dub, D_MODEL, FF_BLK, "dw_up")
    dw_down = _dw_matmul(a2b, dx2b, FF_BLK, D_MODEL, "dw_down")
    early = [dw_out, dw_up, dw_down]
    dqs, dks, dvs, dbs = [], [], [], []
    for i, (q, k, v, bias, dya, ya, lse) in enumerate(zip(qs, ks, vs, biases, dyas, yas, lsts)):
        rider = None
        if shards is not None:
            make = (lambda: _pair_exchange_rider(EARLY, early), lambda: _chip_exchange_rider(early),
                    lambda: _pair_share_rider(EARLY, early))[i]
            rider = make()
        (dq, dk, dv, db), got = _attn_bwd(q, k, v, bias, dya, ya, lse, rider)
        if shards is not None and i == 0:
            core = lax.axis_index("c").reshape(1).astype(jnp.int32)
            early = [_pair_add(core, g, o, "grad_pair_add_" + n) for n, g, o in zip(EARLY, early, got)]
        elif shards is not None and i == 1:
            early = [_chip_sum(t, "grad_chip_sum_" + n) for n, t in zip(EARLY, got)]
        elif shards is not None:
            early = got
        dqs.append(dq)
        dks.append(dk)
        dvs.append(dv)
        dbs.append(db)
    d_rel_bias = _bias_grad(dbs).T
    (dxc_f, dwa_f, dwx_f, dvec_f), _ = _rnn_bwd(xc, hf, dhs, *lru["fwd"], reverse=False)
    small = {
        "lru_wa_fwd": _diag_blocks(dwa_f), "lru_ba_fwd": dvec_f[0:1], "lru_wx_fwd": _diag_blocks(dwx_f),
        "lru_bx_fwd": dvec_f[1:2], "lru_lam_fwd": dvec_f[3:4],
        "rel_bias": d_rel_bias, "norm_rnn_g": dg_rnn, "norm_attn_g": dg_attn,
        "mlp_norm_g": dg_mlp, "final_norm_g": dg_fin,
    }
    loss_local = (0.5 / D_MODEL) * jnp.sum(loss_vec)
    rider = None
    if shards is not None:
        rider = _small_gather_rider(_pack([small[n].reshape(shp) for n, shp in SMALL if n in small]
                                          + [loss_local.reshape(1)]))
    (dxc_b, dwa_b, dwx_b, dvec_b), gathered = _rnn_bwd(xc, hb, dhs, *lru["bwd"], reverse=True, rider=rider)
    grad_x, dw_in, dg1, dconv = _inproj_bwd(x, dx1, xr, dxc_f, dxc_b, dgate, dqs, dks, dvs,
                                            p["attn_norm_g"], p["conv_w"], p["w_in"])[0]
    last = {"lru_wa_bwd": _diag_blocks(dwa_b), "lru_ba_bwd": dvec_b[0:1], "lru_wx_bwd": _diag_blocks(dwx_b),
            "lru_bx_bwd": dvec_b[1:2], "lru_lam_bwd": dvec_b[3:4],
            "attn_norm_g": dg1, "conv_w": dconv[0:4], "conv_b": dconv[4:5]}
    if shards is None:
        big = {"w_in": dw_in, "w_out": dw_out, "w_up": dw_up, "w_down": dw_down}
        return loss_local, grad_x, {**small, **last}, None, big, {}
    return loss_local, grad_x, last, gathered[0], {"w_in": dw_in}, dict(zip(EARLY, early))


BIG = ("w_in", "w_out", "w_up", "w_down")
BIG_SHARD = {"w_in": (D_MODEL, IN_BLK), "w_out": (OUT_BLK, D_MODEL), "w_up": (D_MODEL, FF_BLK), "w_down": (FF_BLK, D_MODEL)}
N_BIG = len(BIG)
N_CHIP_PEERS = 3
ANY = pl.BlockSpec(memory_space=pl.ANY)


def _place():
    x, y, c = lax.axis_index("x"), lax.axis_index("y"), lax.axis_index("c")
    chips = [(1 - x, y), (x, 1 - y), (1 - x, 1 - y)]
    return x, y, c, chips


def _remote(src, dst, send_sem, recv_sem, dev):
    return pltpu.make_async_remote_copy(src_ref=src, dst_ref=dst, send_sem=send_sem, recv_sem=recv_sem,
                                        device_id=dev, device_id_type=MESH)


def _staged_start(srcs, bufs, sems):
    legs = [pltpu.make_async_copy(s, b, sems.at[i]) for i, (s, b) in enumerate(zip(srcs, bufs))]
    for cp in legs:
        cp.start()
    return legs


def _staged_finish(legs, bufs, dsts, sems):
    out = []
    for i, (leg, b, d) in enumerate(zip(legs, bufs, dsts)):
        leg.wait()
        cp = pltpu.make_async_copy(b, d, sems.at[i])
        cp.start()
        out.append(cp)
    return out


class _Rider:
    def __init__(self, inputs, out_shape, scratch, first, late, last):
        self.inputs, self.out_shape, self.scratch = list(inputs), list(out_shape), list(scratch)
        self.first, self.late, self.last = first, late, last


def _call(body, name, grid, in_specs, out_specs, out_shape, scratch, operands, rider=None):
    n_grid = len(grid)
    if rider is None:
        res = pl.pallas_call(body, grid=grid, name=name, in_specs=in_specs, out_specs=out_specs, out_shape=out_shape,
                             scratch_shapes=scratch, compiler_params=_params(n_grid))(*operands)
        return list(res), []
    n_in, n_out, n_scr = len(in_specs), len(out_specs), len(scratch)
    ri, ro = len(rider.inputs), len(rider.out_shape)
    nsteps = int(np.prod(grid))
    late_step = max(nsteps - 3, 1)

    def wrapped(*refs):
        a, b = n_in, n_in + ri
        c, d = b + n_out, b + n_out + ro
        e = d + n_scr
        mine = refs[:a] + refs[b:c] + refs[d:e]
        theirs = (refs[a:b], refs[c:d], refs[e:])
        step = pl.program_id(0)
        for ax in range(1, n_grid):
            step = step * grid[ax] + pl.program_id(ax)
        pl.when(step == 0)(lambda: rider.first(*theirs))
        pl.when(step == late_step)(lambda: rider.late(*theirs))
        body(*mine)
        pl.when(step == nsteps - 1)(lambda: rider.last(*theirs))

    res = pl.pallas_call(
        wrapped, grid=grid, name=name, in_specs=list(in_specs) + [ANY] * ri, out_specs=list(out_specs) + [ANY] * ro,
        out_shape=list(out_shape) + rider.out_shape, scratch_shapes=list(scratch) + rider.scratch,
        compiler_params=_params(n_grid),
    )(*operands, *rider.inputs)
    return list(res[:n_out]), list(res[n_out:])


def _run_rider(rider, name):
    ri, ro = len(rider.inputs), len(rider.out_shape)

    def body(*refs):
        parts = (refs[:ri], refs[ri:ri + ro], refs[ri + ro:])
        rider.first(*parts)
        rider.late(*parts)
        rider.last(*parts)

    return list(pl.pallas_call(
        body, name=name, in_specs=[ANY] * ri, out_specs=[ANY] * ro, out_shape=rider.out_shape, scratch_shapes=rider.scratch,
        compiler_params=pltpu.CompilerParams(has_side_effects=True, vmem_limit_bytes=VMEM_LIMIT),
    )(*rider.inputs))


def _nothing(ins, outs, scr):
    return None


def _gather_rider(names, shards, conv_w=None):
    n = len(names)
    items = n + (conv_w is not None)
    halves = [BIG_SHARD[nm][0] // 2 for nm in names]

    def parts(ins, outs, scr):
        x, y, c, chips = _place()
        return x, y, c, chips, 2 * x + y, (x, y, 1 - c), scr[:8], scr[8:]

    def piece(outs, w, chip, core_half):
        return outs[w].at[chip, pl.ds(core_half * halves[w], halves[w])]

    def ici(ins, outs, sems, w, k, chip_xy, c, me):
        return _remote(ins[w].at[pl.ds(c * halves[w], halves[w])], piece(outs, w, me, c),
                       sems[0].at[w, k], sems[1].at[w, k], (*chip_xy, c))

    def first(ins, outs, scr):
        x, y, c, chips, me, sibling, sems, bufs = parts(ins, outs, scr)
        legs = _staged_start(ins, bufs, sems[6])
        for w in range(n):
            for k, chip_xy in enumerate(chips):
                ici(ins, outs, sems, w, k, chip_xy, c, me).start()
        if conv_w is not None:
            for k, (px, py) in enumerate(chips):
                _remote(ins[n], outs[n].at[me], sems[4].at[k], sems[5].at[k], (px, py, c)).start()
        _staged_finish(legs, bufs, [o.at[me] for o in outs], sems[7])

    def late(ins, outs, scr):
        x, y, c, chips, me, sibling, sems, bufs = parts(ins, outs, scr)
        for w in range(n):
            for k, (px, py) in enumerate(chips):
                landed = piece(outs, w, 2 * px + py, c)
                _remote(landed, landed, sems[0].at[w, k], sems[1].at[w, k], (px, py, c)).wait_recv()
                _remote(landed, landed, sems[2].at[w, k], sems[3].at[w, k], sibling).start()

    def last(ins, outs, scr):
        x, y, c, chips, me, sibling, sems, bufs = parts(ins, outs, scr)
        for w in range(n):
            for k, (px, py) in enumerate(chips):
                other = piece(outs, w, 2 * px + py, 1 - c)
                _remote(other, other, sems[2].at[w, k], sems[3].at[w, k], sibling).wait_recv()
        if conv_w is not None:
            for k, (px, py) in enumerate(chips):
                got = outs[n].at[2 * px + py]
                _remote(got, got, sems[4].at[k], sems[5].at[k], (px, py, c)).wait_recv()
                _remote(ins[n], outs[n].at[me], sems[4].at[k], sems[5].at[k], (px, py, c)).wait_send()
        for i in range(items):
            pltpu.make_async_copy(bufs[i], outs[i].at[me], sems[7].at[i]).wait()
        for w in range(n):
            for k, (px, py) in enumerate(chips):
                ici(ins, outs, sems, w, k, (px, py), c, me).wait_send()
                landed = piece(outs, w, 2 * px + py, c)
                _remote(landed, landed, sems[2].at[w, k], sems[3].at[w, k], sibling).wait_send()

    out_shape = [jax.ShapeDtypeStruct((N_SHARD,) + BIG_SHARD[nm], BF16) for nm in names]
    stage = [pltpu.VMEM(BIG_SHARD[nm], BF16) for nm in names]
    inputs = list(shards)
    if conv_w is not None:
        out_shape.append(jax.ShapeDtypeStruct((N_SHARD,) + conv_w.shape, F32))
        stage.append(pltpu.VMEM(conv_w.shape, F32))
        inputs.append(conv_w)
    scratch = ([pltpu.SemaphoreType.DMA((n, N_CHIP_PEERS))] * 4 + [pltpu.SemaphoreType.DMA((N_CHIP_PEERS,))] * 2
               + [pltpu.SemaphoreType.DMA((items,))] * 2 + stage)
    return _Rider(inputs, out_shape, scratch, first, late, last)


def _pair_exchange_rider(names, grads):
    def copies(ins, outs, scr):
        x, y, c, _ = _place()
        out = []
        for w, nm in enumerate(names):
            h = BIG_SHARD[nm][0] // 2
            out.append(_remote(ins[w].at[:, pl.ds((1 - c) * h, h), :], outs[w], scr[0].at[w], scr[1].at[w], (x, y, 1 - c)))
        return out

    def first(ins, outs, scr):
        for cp in copies(ins, outs, scr):
            cp.start()

    def last(ins, outs, scr):
        for cp in copies(ins, outs, scr):
            cp.wait()

    out_shape = [jax.ShapeDtypeStruct((N_SHARD, BIG_SHARD[nm][0] // 2, BIG_SHARD[nm][1]), F32) for nm in names]
    return _Rider(grads, out_shape, [pltpu.SemaphoreType.DMA((len(names),))] * 2, first, _nothing, last)


def _pair_add(core, grad, other, name):
    _, r, cols = grad.shape
    h = r // 2
    th = min(h, 256)
    per = h // th

    def body(c_ref, g_ref, o_ref, out_ref):
        out_ref[...] = (g_ref[...] + o_ref[...]).astype(BF16)

    return pl.pallas_call(
        body, name=name,
        grid_spec=pltpu.PrefetchScalarGridSpec(
            num_scalar_prefetch=1, grid=(N_SHARD, per),
            in_specs=[pl.BlockSpec((1, th, cols), lambda j, i, c_ref: (j, c_ref[0] * per + i, 0)),
                      pl.BlockSpec((1, th, cols), lambda j, i, c_ref: (j, i, 0))],
            out_specs=pl.BlockSpec((1, th, cols), lambda j, i, c_ref: (j, i, 0))),
        out_shape=jax.ShapeDtypeStruct((N_SHARD, h, cols), BF16),
        compiler_params=_params(2),
    )(core, grad, other)


def _chip_exchange_rider(parts):
    n = len(parts)

    def sends(ins, outs, scr):
        x, y, c, chips = _place()
        me = 2 * x + y
        return [_remote(ins[w].at[2 * px + py], outs[w].at[me], scr[0].at[w, k], scr[1].at[w, k], (px, py, c))
                for w in range(n) for k, (px, py) in enumerate(chips)]

    def first(ins, outs, scr):
        x, y, c, chips = _place()
        me = 2 * x + y
        legs = _staged_start([r.at[me] for r in ins], scr[4:], scr[2])
        for cp in sends(ins, outs, scr):
            cp.start()
        _staged_finish(legs, scr[4:], [o.at[me] for o in outs], scr[3])

    def last(ins, outs, scr):
        x, y, c, chips = _place()
        me = 2 * x + y
        for w in range(n):
            for k, (px, py) in enumerate(chips):
                got = outs[w].at[2 * px + py]
                _remote(got, got, scr[0].at[w, k], scr[1].at[w, k], (px, py, c)).wait_recv()
        for cp in sends(ins, outs, scr):
            cp.wait_send()
        for w in range(n):
            pltpu.make_async_copy(scr[4 + w], outs[w].at[me], scr[3].at[w]).wait()

    out_shape = [jax.ShapeDtypeStruct(p.shape, BF16) for p in parts]
    scratch = ([pltpu.SemaphoreType.DMA((n, N_CHIP_PEERS))] * 2 + [pltpu.SemaphoreType.DMA((n,))] * 2
               + [pltpu.VMEM(p.shape[1:], BF16) for p in parts])
    return _Rider(parts, out_shape, scratch, first, _nothing, last)


def _chip_sum(parts, name):
    _, h, cols = parts.shape
    th = min(h, 256)

    def body(p_ref, out_ref):
        acc = p_ref[0].astype(F32)
        for j in range(1, N_SHARD):
            acc = acc + p_ref[j].astype(F32)
        out_ref[...] = acc

    return pl.pallas_call(
        body, name=name, grid=(h // th,),
        in_specs=[pl.BlockSpec((N_SHARD, th, cols), lambda i: (0, i, 0))],
        out_specs=pl.BlockSpec((th, cols), lambda i: (i, 0)),
        out_shape=jax.ShapeDtypeStruct((h, cols), F32),
        compiler_params=_params(),
    )(parts)


def _pair_share_rider(names, halves):
    n = len(names)
    hs = [BIG_SHARD[nm][0] // 2 for nm in names]

    def mine(outs, c):
        return [outs[w].at[pl.ds(c * hs[w], hs[w])] for w in range(n)]

    def first(ins, outs, scr):
        x, y, c, _ = _place()
        legs = _staged_start(ins, scr[4:], scr[2])
        for w, dst in enumerate(mine(outs, c)):
            _remote(ins[w], dst, scr[0].at[w], scr[1].at[w], (x, y, 1 - c)).start()
        _staged_finish(legs, scr[4:], mine(outs, c), scr[3])

    def last(ins, outs, scr):
        x, y, c, _ = _place()
        for w, (theirs, dst) in enumerate(zip(mine(outs, 1 - c), mine(outs, c))):
            _remote(theirs, theirs, scr[0].at[w], scr[1].at[w], (x, y, 1 - c)).wait_recv()
            _remote(ins[w], dst, scr[0].at[w], scr[1].at[w], (x, y, 1 - c)).wait_send()
            pltpu.make_async_copy(scr[4 + w], dst, scr[3].at[w]).wait()

    out_shape = [jax.ShapeDtypeStruct(BIG_SHARD[nm], F32) for nm in names]
    scratch = [pltpu.SemaphoreType.DMA((n,))] * 4 + [pltpu.VMEM((h, BIG_SHARD[nm][1]), F32) for nm, h in zip(names, hs)]
    return _Rider(halves, out_shape, scratch, first, _nothing, last)


N_DEV = 8


def _all_peers(x, y, c):
    return [((1 - x) if fx else x, (1 - y) if fy else y, (1 - c) if fc else c)
            for fx in (0, 1) for fy in (0, 1) for fc in (0, 1) if fx or fy or fc]


def _small_gather_rider(vec):
    def sends(ins, outs, scr):
        x, y, c, _ = _place()
        me = 4 * x + 2 * y + c
        return [_remote(ins[0], outs[0].at[me], scr[0].at[k], scr[1].at[k], dev) for k, dev in enumerate(_all_peers(x, y, c))]

    def first(ins, outs, scr):
        x, y, c, _ = _place()
        legs = _staged_start(ins, scr[4:], scr[2])
        for cp in sends(ins, outs, scr):
            cp.start()
        _staged_finish(legs, scr[4:], [outs[0].at[4 * x + 2 * y + c]], scr[3])

    def last(ins, outs, scr):
        x, y, c, _ = _place()
        for k, (px, py, pc) in enumerate(_all_peers(x, y, c)):
            got = outs[0].at[4 * px + 2 * py + pc]
            _remote(got, got, scr[0].at[k], scr[1].at[k], (px, py, pc)).wait_recv()
        for cp in sends(ins, outs, scr):
            cp.wait_send()
        pltpu.make_async_copy(scr[4], outs[0].at[4 * x + 2 * y + c], scr[3].at[0]).wait()

    scratch = ([pltpu.SemaphoreType.DMA((N_DEV - 1,))] * 2 + [pltpu.SemaphoreType.DMA((1,))] * 2
               + [pltpu.VMEM(vec.shape, F32)])
    return _Rider([vec], [jax.ShapeDtypeStruct((N_DEV,) + vec.shape, F32)], scratch, first, _nothing, last)


def _sum_devices(gathered):
    def body(g_ref, out_ref):
        acc = g_ref[0]
        for j in range(1, N_DEV):
            acc = acc + g_ref[j]
        out_ref[...] = acc

    return pl.pallas_call(body, name="sum_devices", out_shape=jax.ShapeDtypeStruct(gathered.shape[1:], F32),
                          compiler_params=_params(0))(gathered)


def _allreduce_small(vec):
    rows = vec.shape[0]

    def body(v_ref, sum_ref, gat_ref, send, recv, loc_sem):
        x, y, c, chips = _place()
        sibling = (x, y, 1 - c)
        slot = lambda px, py, pc: gat_ref.at[4 * px + 2 * py + pc]
        lc = pltpu.make_async_copy(v_ref, slot(x, y, c), loc_sem)
        lc.start()
        sends = [_remote(v_ref, slot(x, y, c), send.at[0], recv.at[0], sibling)]
        sends += [_remote(v_ref, slot(x, y, c), send.at[1 + k], recv.at[1 + k], (px, py, c))
                  for k, (px, py) in enumerate(chips)]
        for cp in sends:
            cp.start()
        for k, (px, py) in enumerate(chips):
            got = slot(px, py, c)
            _remote(got, got, send.at[1 + k], recv.at[1 + k], (px, py, c)).wait_recv()
            cp = _remote(got, got, send.at[4 + k], recv.at[4 + k], sibling)
            cp.start()
            sends.append(cp)
        got = slot(x, y, 1 - c)
        _remote(got, got, send.at[0], recv.at[0], sibling).wait_recv()
        for k, (px, py) in enumerate(chips):
            got = slot(px, py, 1 - c)
            _remote(got, got, send.at[4 + k], recv.at[4 + k], sibling).wait_recv()
        for cp in sends:
            cp.wait_send()
        lc.wait()
        acc = gat_ref[0]
        for j in range(1, N_DEV):
            acc = acc + gat_ref[j]
        sum_ref[...] = acc

    total, _ = pl.pallas_call(
        body, name="allreduce_small",
        in_specs=[_whole_vmem()], out_specs=[_whole_vmem(), _whole_vmem()],
        out_shape=[jax.ShapeDtypeStruct((rows, 128), F32), jax.ShapeDtypeStruct((N_DEV, rows, 128), F32)],
        scratch_shapes=[pltpu.SemaphoreType.DMA((N_DEV - 1,))] * 2 + [pltpu.SemaphoreType.DMA(())],
        compiler_params=pltpu.CompilerParams(has_side_effects=True, vmem_limit_bytes=VMEM_LIMIT),
    )(vec)
    return total


def _adam_math(w_ref, g_ref, m_ref, v_ref, d_ref, m2_ref, v2_ref):
    c1 = 1.0 - ADAM_B1 ** ADAM_STEP
    c2 = 1.0 - ADAM_B2 ** ADAM_STEP
    gv = g_ref[...]
    m2 = ADAM_B1 * m_ref[...] + (1.0 - ADAM_B1) * gv
    v2 = ADAM_B2 * v_ref[...] + (1.0 - ADAM_B2) * (gv * gv)
    m2_ref[...] = m2
    v2_ref[...] = v2
    d_ref[...] = -ADAM_LR * ((m2 / c1) / (jnp.sqrt(v2 / c2) + ADAM_EPS) + ADAM_WD * w_ref[...])


def _adamw_many(ws, gs, ms, vs):
    n = len(ws)

    def body(*refs):
        for i in range(n):
            _adam_math(*[refs[k * n + i] for k in range(7)])

    shapes = [jax.ShapeDtypeStruct(w.shape, F32) for w in ws]
    res = pl.pallas_call(body, name="adamw_small", out_shape=shapes * 3, compiler_params=_params(0))(*ws, *gs, *ms, *vs)
    return res[:n], res[n:2 * n], res[2 * n:]


def _adamw(w, g, m, v, name):
    rows, cols = w.shape
    tr = 256 if rows % 256 == 0 else rows

    def body(w_ref, g_ref, m_ref, v_ref, d_ref, m2_ref, v2_ref):
        _adam_math(w_ref, g_ref, m_ref, v_ref, d_ref, m2_ref, v2_ref)

    spec = pl.BlockSpec((tr, cols), lambda i: (i, 0))
    shp = jax.ShapeDtypeStruct((rows, cols), F32)
    return pl.pallas_call(
        body, name=name, grid=(rows // tr,), in_specs=[spec] * 4, out_specs=[spec] * 3, out_shape=[shp] * 3,
        compiler_params=_params(),
    )(w, g, m, v)


SMALL = (
    ("attn_norm_g", (1, 1024)), ("conv_w", (1, 4, 512)), ("conv_b", (1, 512)),
    ("lru_wa_fwd", (1, 8, 64, 64)), ("lru_ba_fwd", (1, 512)), ("lru_wx_fwd", (1, 8, 64, 64)), ("lru_bx_fwd", (1, 512)),
    ("lru_lam_fwd", (1, 512)),
    ("lru_wa_bwd", (1, 8, 64, 64)), ("lru_ba_bwd", (1, 512)), ("lru_wx_bwd", (1, 8, 64, 64)), ("lru_bx_bwd", (1, 512)),
    ("lru_lam_bwd", (1, 512)),
    ("rel_bias", (32, 8)), ("norm_rnn_g", (1, 512)), ("norm_attn_g", (1, 512)), ("mlp_norm_g", (1, 1024)),
    ("final_norm_g", (1024,)),
)
PACK_ROW = 8 * 128


def _pack(parts):
    flat = jnp.concatenate([p.reshape(-1) for p in parts])
    pad = (-flat.shape[0]) % PACK_ROW
    return jnp.pad(flat, (0, pad)).reshape(-1, 128)


def _unpack(packed, shapes):
    flat = packed.reshape(-1)
    out, off = [], 0
    for shp in shapes:
        n = int(np.prod(shp))
        out.append(flat[off:off + n].reshape(shp))
        off += n
    return out


WEIGHT_ORDER = ("attn_norm_g", "w_in", "conv_w", "conv_b", "lru_wa_fwd", "lru_ba_fwd", "lru_wx_fwd", "lru_bx_fwd",
                "lru_lam_fwd", "lru_wa_bwd", "lru_ba_bwd", "lru_wx_bwd", "lru_bx_bwd", "lru_lam_bwd", "rel_bias",
                "norm_rnn_g", "norm_attn_g", "w_out", "mlp_norm_g", "w_up", "w_down", "final_norm_g")


def kernel(x, attn_norm_g, w_in, conv_w, conv_b, lru_wa_fwd, lru_ba_fwd, lru_wx_fwd, lru_bx_fwd, lru_lam_fwd, lru_wa_bwd, lru_ba_bwd, lru_wx_bwd, lru_bx_bwd, lru_lam_bwd, rel_bias, norm_rnn_g, norm_attn_g, w_out, mlp_norm_g, w_up, w_down, final_norm_g, loss_target, m_attn_norm_g, m_w_in, m_conv_w, m_conv_b, m_lru_wa_fwd, m_lru_ba_fwd, m_lru_wx_fwd, m_lru_bx_fwd, m_lru_lam_fwd, m_lru_wa_bwd, m_lru_ba_bwd, m_lru_wx_bwd, m_lru_bx_bwd, m_lru_lam_bwd, m_rel_bias, m_norm_rnn_g, m_norm_attn_g, m_w_out, m_mlp_norm_g, m_w_up, m_w_down, m_final_norm_g, v_attn_norm_g, v_w_in, v_conv_w, v_conv_b, v_lru_wa_fwd, v_lru_ba_fwd, v_lru_wx_fwd, v_lru_bx_fwd, v_lru_lam_fwd, v_lru_wa_bwd, v_lru_ba_bwd, v_lru_wx_bwd, v_lru_bx_bwd, v_lru_lam_bwd, v_rel_bias, v_norm_rnn_g, v_norm_attn_g, v_w_out, v_mlp_norm_g, v_w_up, v_w_down, v_final_norm_g):
    given = dict(locals())
    w = {n: given[n] for n in WEIGHT_ORDER}
    m = {n: given["m_" + n] for n in WEIGHT_ORDER}
    v = {n: given["v_" + n] for n in WEIGHT_ORDER}

    chip = lax.axis_index("x") * 2 + lax.axis_index("y")
    core = lax.axis_index("c")

    shards = {n: w[n][0].astype(BF16) for n in BIG}
    shards["conv_w"] = w["conv_w"][0]
    p = {n: (t[0] if t.ndim >= 3 else t) for n, t in w.items() if n not in BIG and n != "conv_w"}
    p["final_norm_g"] = w["final_norm_g"].reshape(1, D_MODEL)

    _, grad_x, small, gathered, big, reduced = _local_step(x[0], loss_target[0], p, shards)

    late = tuple(big)
    grads = [big[n] for n in late]
    others = _run_rider(_pair_exchange_rider(late, grads), "grad_pair_exchange")
    core_arr = core.reshape(1).astype(jnp.int32)
    parts = [_pair_add(core_arr, g, o, "grad_pair_add_" + n) for n, g, o in zip(late, grads, others)]
    landed = _run_rider(_chip_exchange_rider(parts), "grad_chip_exchange")
    halves = [_chip_sum(t, "grad_chip_sum_" + n) for n, t in zip(late, landed)]
    reduced.update(zip(late, _run_rider(_pair_share_rider(late, halves), "grad_pair_share")))

    early_small = [(n, shp) for n, shp in SMALL if n not in small]
    late_small = [(n, shp) for n, shp in SMALL if n in small]
    *early_g, loss = _unpack(_sum_devices(gathered), [shp for _, shp in early_small] + [(1,)])
    late_g = _unpack(_allreduce_small(_pack([small[n].reshape(shp) for n, shp in late_small])),
                     [shp for _, shp in late_small])
    g = dict(zip([n for n, _ in early_small + late_small], early_g + late_g))
    g["conv_w"] = lax.dynamic_slice_in_dim(g["conv_w"], chip * (D_RNN // N_SHARD), D_RNN // N_SHARD, axis=2)
    for n in BIG:
        g[n] = reduced[n][None]

    delta, new_m, new_v = {}, {}, {}
    for n in BIG:
        d2, m2, v2 = _adamw(w[n][0], reduced[n], m[n][0], v[n][0], "adamw_" + n)
        delta[n], new_m[n], new_v[n] = d2[None], m2[None], v2[None]
    names = [n for n, _ in SMALL]
    for dst, src in zip((delta, new_m, new_v), _adamw_many(*[[t[n] for n in names] for t in (w, g, m, v)])):
        dst.update(dict(zip(names, src)))

    return (loss.reshape(()), grad_x[None], *[g[n] for n in WEIGHT_ORDER], *[delta[n] for n in WEIGHT_ORDER],
            *[new_m[n] for n in WEIGHT_ORDER], *[new_v[n] for n in WEIGHT_ORDER])
```

```python
import functools
import math

import numpy as np
import jax
import jax.numpy as jnp
from jax import lax
from jax.experimental import pallas as pl
from jax.experimental.pallas import tpu as pltpu

F32 = jnp.float32
BF16 = jnp.bfloat16

D_MODEL = 1024
D_RNN = 512
D_ATTN = 512
N_HEADS = 8
HEAD_DIM = 64
N_RNN_BLOCKS = 8
RNN_BLOCK = 64
D_IN = 2 * D_RNN + 3 * D_ATTN
D_FF = 4 * D_MODEL
N_SHARD = 4
IN_BLK = D_IN // N_SHARD
OUT_BLK = D_MODEL // N_SHARD
FF_BLK = D_FF // N_SHARD
EPS = 1e-6
NEG_INF = -1e30
LRU_C = 8.0
DILATIONS = (1, 4, 16)
F32_LAYOUT = 4
HALF_WIN = 64
Q_BLK = 128
K_WIN = 256
N_BUCKETS = 32
MAX_DISTANCE = 1024
ATTN_SCALE = HEAD_DIM ** -0.5

ADAM_LR = 0.001
ADAM_B1 = 0.9
ADAM_B2 = 0.999
ADAM_EPS = 1e-08
ADAM_WD = 0.01
ADAM_STEP = 10

TS = 512
TS_RNN_BWD = 1024
TS_MLP = 256
TS_INPROJ_BWD = 512
ATTN_SUB = 32
ATTN_SUB_FWD = 32
TK_DW = 4096
SCAN_UNROLL = 8
SUB = 8
VMEM_LIMIT = 56 * 1024 * 1024
GELU_C0 = math.sqrt(2.0 / math.pi)
GELU_C1 = 0.044715

MESH = pl.DeviceIdType.MESH


def _params(n_grid=1):
    return pltpu.CompilerParams(vmem_limit_bytes=VMEM_LIMIT, dimension_semantics=("arbitrary",) * n_grid)


def _whole_vmem():
    return pl.BlockSpec(memory_space=pltpu.VMEM)


def _rows(width, tile=TS):
    return pl.BlockSpec((tile, width), lambda i: (i, 0))


def _sigmoid(z):
    return 0.5 * jnp.tanh(0.5 * z) + 0.5


def _log1p(u):
    w = 1.0 + u
    return jnp.where(w == 1.0, u, jnp.log(w) * (u / (w - 1.0)))


def _softplus(z):
    return jnp.maximum(z, 0.0) + _log1p(jnp.exp(-jnp.abs(z)))


def _gelu_parts(g):
    inner = GELU_C0 * (g + GELU_C1 * g * g * g)
    t = jnp.tanh(inner)
    val = 0.5 * g * (1.0 + t)
    dinner = GELU_C0 * (1.0 + 3.0 * GELU_C1 * g * g)
    grad = 0.5 * (1.0 + t) + 0.5 * g * (1.0 - t * t) * dinner
    return val, grad


def _rms(x):
    rstd = lax.rsqrt(jnp.mean(x * x, axis=-1, keepdims=True) + EPS)
    return rstd, x * rstd


def _rms_bwd(dy, g, xhat, rstd):
    dxh = dy * g
    dx = rstd * (dxh - xhat * jnp.mean(dxh * xhat, axis=-1, keepdims=True))
    dg = jnp.sum(dy * xhat, axis=0, keepdims=True)
    return dx, dg


def _dot(a, b):
    return jnp.dot(a, b, preferred_element_type=F32)


def _dot_nt(a, b):
    return lax.dot_general(a, b, (((1,), (1,)), ((), ())), preferred_element_type=F32)


def _dot_tn(a, b):
    return lax.dot_general(a, b, (((0,), (0,)), ((), ())), preferred_element_type=F32)


def _shifted(tile, prev8, next8, k):
    n = tile.shape[0]
    row = lax.broadcasted_iota(jnp.int32, tile.shape, 0)
    if k == 0:
        return tile
    if k < 0:
        r = pltpu.roll(tile, -k, 0)
        for j in range(-k):
            r = jnp.where(row == j, prev8[SUB + j + k:SUB + j + k + 1, :], r)
        return r
    r = pltpu.roll(tile, n - k, 0)
    for j in range(k):
        r = jnp.where(row == n - k + j, next8[j:j + 1, :], r)
    return r


def _to_lane_blocks(val, s_ref):
    for j in range(val.shape[1] // 128):
        s_ref[j] = val[:, j * 128:(j + 1) * 128]


def _from_lane_blocks(s_ref):
    return jnp.concatenate([s_ref[j] for j in range(s_ref.shape[0])], axis=-1)


def _class_rows(s_ref, r, dil):
    n = s_ref.shape[1] // dil
    return jnp.concatenate([s_ref[j, pl.ds(r, n, stride=dil), :] for j in range(s_ref.shape[0])], axis=-1)


def _split_classes(val, s_ref, out_ref, dil):
    _to_lane_blocks(val, s_ref)
    for r in range(dil):
        out_ref[r] = _class_rows(s_ref, r, dil).astype(out_ref.dtype)


def _merge_classes(in_ref, s_ref, dil, also_ref=None):
    n = s_ref.shape[1] // dil
    for r in range(dil):
        v = in_ref[r].astype(F32)
        if also_ref is not None:
            v = v + also_ref[r].astype(F32)
        for j in range(s_ref.shape[0]):
            s_ref[j, pl.ds(r, n, stride=dil), :] = v[:, j * 128:(j + 1) * 128]
    return _from_lane_blocks(s_ref)


def _class_spec(dil, tile=TS):
    return pl.BlockSpec((dil, tile // dil, 512), lambda i: (0, i, 0))


def _class_shape(S, dil, dtype):
    return jax.ShapeDtypeStruct((dil, S // dil, 512), dtype)


def _scan_tile(a_ref, b_ref, h_ref, carry_ref, reverse):
    n = a_ref.shape[0]
    width = a_ref.shape[1]
    groups = n // SUB
    row = lax.broadcasted_iota(jnp.int32, (SUB, width), 0)

    def group_scan(g):
        r0 = pl.multiple_of(g * SUB, SUB)
        a = a_ref[pl.ds(r0, SUB), :]
        b = b_ref[pl.ds(r0, SUB), :]
        for s in (1, 2, 4):
            if reverse:
                a_sh = pltpu.roll(a, SUB - s, 0)
                b_sh = pltpu.roll(b, SUB - s, 0)
                m = row < SUB - s
            else:
                a_sh = pltpu.roll(a, s, 0)
                b_sh = pltpu.roll(b, s, 0)
                m = row >= s
            b = jnp.where(m, a * b_sh + b, b)
            a = jnp.where(m, a * a_sh, a)
        return r0, a, b

    def step(i, carry):
        first = i * SCAN_UNROLL
        order = [(groups - 1 - (first + u)) if reverse else (first + u) for u in range(SCAN_UNROLL)]
        scans = [group_scan(g) for g in order]
        for r0, a, b in scans:
            h = b + a * carry
            h_ref[pl.ds(r0, SUB), :] = h
            edge = h[0:1, :] if reverse else h[SUB - 1:SUB, :]
            carry = jnp.broadcast_to(edge, (SUB, width))
        return carry

    carry_ref[...] = lax.fori_loop(0, groups // SCAN_UNROLL, step, carry_ref[...])


def _conv_fwd(xr, prev8, next8, cw, cb):
    y = cb + _shifted(xr, prev8, next8, -2) * cw[0:1, :]
    y = y + _shifted(xr, prev8, next8, -1) * cw[1:2, :]
    y = y + xr * cw[2:3, :]
    y = y + _shifted(xr, prev8, next8, 1) * cw[3:4, :]
    return y


def _lru_gates(xc, wa_ref, ba, wx_ref, bx, lam):
    xcb = xc.astype(BF16)
    r = _sigmoid(_dot(xcb, wa_ref[...]) + ba)
    i = _sigmoid(_dot(xcb, wx_ref[...]) + bx)
    cl = -LRU_C * _softplus(-lam)
    la = cl * r
    a = jnp.exp(la)
    m2 = -jnp.tanh(la) * (a * a + 1.0)
    inv = jnp.where(m2 > 0.0, lax.rsqrt(m2), 0.0)
    mult = m2 * inv
    return xcb, r, i, cl, a, mult, inv


def _inproj_fwd(x, g1, w_in, rider=None):
    S = x.shape[0]

    def body(x_ref, g_ref, w_ref, xr_ref, gate_ref, *rest):
        qkv_refs, s_ref, s4_ref, w_full = rest[:9], rest[9], rest[10], rest[11]

        @pl.when(pl.program_id(0) == 0)
        def _():
            for j in range(N_SHARD):
                w_full[:, j * IN_BLK:(j + 1) * IN_BLK] = w_ref[j]

        _, xh = _rms(x_ref[...])
        h = (xh * g_ref[...]).astype(BF16)
        proj = _dot(h, w_full[...])
        xr_ref[...] = proj[:, 0:512]
        gate_ref[...] = proj[:, 512:1024]
        for t in range(3):
            val = proj[:, 1024 + 512 * t:1536 + 512 * t]
            d1_ref, d4_ref, d16_ref = qkv_refs[3 * t:3 * t + 3]
            d1_ref[0] = val.astype(BF16)
            _to_lane_blocks(val, s_ref)
            for r4 in range(4):
                c4 = _class_rows(s_ref, r4, 4)
                d4_ref[r4] = c4.astype(BF16)
                _to_lane_blocks(c4, s4_ref.at[r4])
            for r4 in range(4):
                for m in range(4):
                    d16_ref[r4 + 4 * m] = _class_rows(s4_ref.at[r4], m, 4).astype(BF16)

    f = jax.ShapeDtypeStruct((S, 512), F32)
    return _call(
        body, "inproj_fwd", (S // TS,),
        [_rows(D_MODEL), _whole_vmem(), _whole_vmem()],
        [_rows(512)] * 2 + [_class_spec(d) for d in DILATIONS] * 3,
        [f, f] + [_class_shape(S, d, BF16) for d in DILATIONS] * 3,
        [pltpu.VMEM((4, TS, 128), F32), pltpu.VMEM((4, 4, TS // 4, 128), F32), pltpu.VMEM((D_MODEL, D_IN), BF16)],
        (x, g1, w_in), rider)


def _halo_specs(S, order, tile=TS):
    per = tile // SUB
    last = S // SUB - 1
    return [
        pl.BlockSpec((tile, 512), lambda i: (order(i), 0)),
        pl.BlockSpec((SUB, 512), lambda i: (jnp.maximum(order(i) * per - 1, 0), 0)),
        pl.BlockSpec((SUB, 512), lambda i: (jnp.minimum((order(i) + 1) * per, last), 0)),
    ]


def _rnn_fwd(xr, conv_w, conv_b, wa, ba, wx, bx, lam, reverse, rider=None, xc=None):
    S = xr.shape[0]
    nt = S // TS
    order = (lambda i: nt - 1 - i) if reverse else (lambda i: i)
    with_conv = xc is None
    n_x = 5 if with_conv else 1
    tile = pl.BlockSpec((TS, 512), lambda i: (order(i), 0))

    def body(*refs):
        wa_ref, ba_ref, wx_ref, bx_ref, lam_ref, h_ref = refs[n_x:n_x + 6]
        a_s, b_s, carry = refs[-3:]
        i = pl.program_id(0)
        t = order(i)

        @pl.when(i == 0)
        def _():
            carry[...] = jnp.zeros_like(carry)

        if with_conv:
            x_ref, xp_ref, xn_ref, cw_ref, cb_ref = refs[:5]
            prev8 = jnp.where(t > 0, xp_ref[...], 0.0)
            next8 = jnp.where(t < nt - 1, xn_ref[...], 0.0)
            xcv = _conv_fwd(x_ref[...], prev8, next8, cw_ref[...], cb_ref[...])
            refs[n_x + 6][...] = xcv
        else:
            xcv = refs[0][...]
        _, _, gi, _, a, mult, _ = _lru_gates(xcv, wa_ref, ba_ref[...], wx_ref, bx_ref[...], lam_ref[...])
        a_s[...] = a
        b_s[...] = mult * (gi * xcv)
        _scan_tile(a_s, b_s, h_ref, carry, reverse)

    f512 = jax.ShapeDtypeStruct((S, 512), F32)
    return _call(
        body, "rnn_fwd_rev" if reverse else "rnn_fwd_fwd", (nt,),
        (_halo_specs(S, order) + [_whole_vmem()] * 2 if with_conv else [tile]) + [_whole_vmem()] * 5,
        [tile, tile] if with_conv else [tile], [f512, f512] if with_conv else [f512],
        [pltpu.VMEM((TS, 512), F32), pltpu.VMEM((TS, 512), F32), pltpu.VMEM((SUB, 512), F32)],
        ((xr, xr, xr, conv_w, conv_b) if with_conv else (xc,)) + (wa, ba, wx, bx, lam), rider)


def _mix_fwd(o3, l3, hf, hb, gate, x, g_rnn, g_attn, w_out):
    S = x.shape[0]

    def body(o1, o2, o3_, l1, l2, l3_, hf_ref, hb_ref, gate_ref, x_ref, gr_ref, ga_ref, w_ref,
             x1_ref, mix_ref, ya1, ya2, ls1, ls2, s_ref):
        la, lb, lc = l1[0], _merge_classes(l2, s_ref, F32_LAYOUT), _merge_classes(l3_, s_ref, F32_LAYOUT)
        m = jnp.maximum(jnp.maximum(la, lb), lc)
        ea, eb, ec = jnp.exp(la - m), jnp.exp(lb - m), jnp.exp(lc - m)
        den = ea + eb + ec
        lse = m + jnp.log(den)
        ya = (ea * o1[0] + eb * _merge_classes(o2, s_ref, F32_LAYOUT) + ec * _merge_classes(o3_, s_ref, F32_LAYOUT)) / den
        ya1[0] = ya
        ls1[0] = lse
        _split_classes(ya, s_ref, ya2, F32_LAYOUT)
        _split_classes(lse, s_ref, ls2, F32_LAYOUT)
        gg, _ = _gelu_parts(gate_ref[...])
        yr = (hf_ref[...] + hb_ref[...]) * gg
        _, xh_r = _rms(yr)
        _, xh_a = _rms(ya)
        mix = jnp.concatenate([xh_r * gr_ref[...], xh_a * ga_ref[...]], axis=-1).astype(BF16)
        mix_ref[...] = mix
        acc = x_ref[...]
        for j in range(N_SHARD):
            acc = acc + _dot(mix[:, j * OUT_BLK:(j + 1) * OUT_BLK], w_ref[j])
        x1_ref[...] = acc

    one, four = _class_spec(1), _class_spec(F32_LAYOUT)
    return pl.pallas_call(
        body, grid=(S // TS,), name="mix_fwd",
        in_specs=[one, four, four] * 2 + [_rows(512)] * 3 + [_rows(D_MODEL)] + [_whole_vmem()] * 3,
        out_specs=[_rows(D_MODEL), _rows(D_MODEL)] + [one, four] * 2,
        out_shape=[jax.ShapeDtypeStruct((S, D_MODEL), F32), jax.ShapeDtypeStruct((S, D_MODEL), BF16)]
        + [_class_shape(S, 1, F32), _class_shape(S, F32_LAYOUT, F32)] * 2,
        scratch_shapes=[pltpu.VMEM((4, TS, 128), F32)],
        compiler_params=_params(),
    )(*o3, *l3, hf, hb, gate, x, g_rnn, g_attn, w_out)


def _mlp_fwd_bwd(x1, target, g_mlp, g_fin, w_up, w_down):
    S = x1.shape[0]
    tm = TS_MLP

    def body(x1_ref, t_ref, gm_ref, gf_ref, wu_ref, wd_ref,
             dx1_ref, h2_ref, a2_ref, du_ref, dx2_ref, loss_ref, dgf_ref, dgm_ref, relu_s):
        @pl.when(pl.program_id(0) == 0)
        def _():
            loss_ref[...] = jnp.zeros_like(loss_ref)
            dgf_ref[...] = jnp.zeros_like(dgf_ref)
            dgm_ref[...] = jnp.zeros_like(dgm_ref)

        x1v = x1_ref[...]
        rstd1, xh1 = _rms(x1v)
        h2 = (xh1 * gm_ref[...]).astype(BF16)
        h2_ref[...] = h2
        x2 = x1v
        for j in range(N_SHARD):
            r = jnp.maximum(_dot(h2, wu_ref[j]), 0.0)
            relu_s[j] = r
            a2 = (r * r).astype(BF16)
            a2_ref[:, j * FF_BLK:(j + 1) * FF_BLK] = a2
            x2 = x2 + _dot(a2, wd_ref[j])
        rstd2, xh2 = _rms(x2)
        err = xh2 * gf_ref[...] - t_ref[...]
        loss_ref[...] += jnp.sum(err * err, axis=0, keepdims=True)
        dy = err * (1.0 / D_MODEL)
        dx2, dgf = _rms_bwd(dy, gf_ref[...], xh2, rstd2)
        dgf_ref[...] += dgf
        dx2b = dx2.astype(BF16)
        dx2_ref[...] = dx2b
        dh2 = jnp.zeros((tm, D_MODEL), F32)
        for j in range(N_SHARD):
            du = (_dot_nt(dx2b, wd_ref[j]) * (2.0 * relu_s[j])).astype(BF16)
            du_ref[:, j * FF_BLK:(j + 1) * FF_BLK] = du
            dh2 = dh2 + _dot_nt(du, wu_ref[j])
        dx1n, dgm = _rms_bwd(dh2, gm_ref[...], xh1, rstd1)
        dgm_ref[...] += dgm
        dx1_ref[...] = dx2 + dx1n

    vec = jax.ShapeDtypeStruct((1, D_MODEL), F32)
    return pl.pallas_call(
        body, grid=(S // tm,), name="mlp_fwd_bwd",
        in_specs=[_rows(D_MODEL, tm), _rows(D_MODEL, tm)] + [_whole_vmem()] * 4,
        out_specs=[_rows(D_MODEL, tm), _rows(D_MODEL, tm), _rows(D_FF, tm), _rows(D_FF, tm), _rows(D_MODEL, tm)]
        + [_whole_vmem()] * 3,
        out_shape=[jax.ShapeDtypeStruct((S, D_MODEL), F32), jax.ShapeDtypeStruct((S, D_MODEL), BF16),
                   jax.ShapeDtypeStruct((S, D_FF), BF16), jax.ShapeDtypeStruct((S, D_FF), BF16),
                   jax.ShapeDtypeStruct((S, D_MODEL), BF16), vec, vec, vec],
        scratch_shapes=[pltpu.VMEM((N_SHARD, tm, FF_BLK), F32)],
        compiler_params=_params(),
    )(x1, target, g_mlp, g_fin, w_up, w_down)


def _mix_bwd(dx1, w_out, mixb, ya, hf, hb, gate, g_rnn, g_attn):
    S = dx1.shape[0]

    def body(dx1_ref, w_ref, mix_ref, ya_ref, hf_ref, hb_ref, gate_ref, gr_ref, ga_ref,
             dhs_ref, dgate_ref, dya1, dya2, dw_ref, dgr_ref, dga_ref, s_ref):
        @pl.when(pl.program_id(0) == 0)
        def _():
            dw_ref[...] = jnp.zeros_like(dw_ref)
            dgr_ref[...] = jnp.zeros_like(dgr_ref)
            dga_ref[...] = jnp.zeros_like(dga_ref)

        dx1b = dx1_ref[...].astype(BF16)
        mix = mix_ref[...]
        for j in range(N_SHARD):
            dw_ref[j] += _dot_tn(mix[:, j * OUT_BLK:(j + 1) * OUT_BLK], dx1b)
        dmix = jnp.concatenate([_dot_nt(dx1b, w_ref[j]) for j in range(N_SHARD)], axis=-1)
        gg, dgg = _gelu_parts(gate_ref[...])
        hs = hf_ref[...] + hb_ref[...]
        rstd_r, xh_r = _rms(hs * gg)
        dyr, dgr = _rms_bwd(dmix[:, 0:D_RNN], gr_ref[...], xh_r, rstd_r)
        dgr_ref[...] += dgr
        rstd_a, xh_a = _rms(ya_ref[0])
        dya, dga = _rms_bwd(dmix[:, D_RNN:], ga_ref[...], xh_a, rstd_a)
        dga_ref[...] += dga
        dya1[0] = dya
        _split_classes(dya, s_ref, dya2, F32_LAYOUT)
        dhs_ref[...] = dyr * gg
        dgate_ref[...] = dyr * hs * dgg

    f512 = jax.ShapeDtypeStruct((S, 512), F32)
    vec = jax.ShapeDtypeStruct((1, 512), F32)
    return pl.pallas_call(
        body, grid=(S // TS,), name="mix_bwd",
        in_specs=[_rows(D_MODEL), _whole_vmem(), _rows(D_MODEL), _class_spec(1)] + [_rows(512)] * 3 + [_whole_vmem()] * 2,
        out_specs=[_rows(512)] * 2 + [_class_spec(1), _class_spec(F32_LAYOUT)] + [_whole_vmem()] * 3,
        out_shape=[f512, f512, _class_shape(S, 1, F32), _class_shape(S, F32_LAYOUT, F32),
                   jax.ShapeDtypeStruct((N_SHARD, OUT_BLK, D_MODEL), F32), vec, vec],
        scratch_shapes=[pltpu.VMEM((4, TS, 128), F32)],
        compiler_params=_params(),
    )(dx1, w_out, mixb, ya, hf, hb, gate, g_rnn, g_attn)


def _rnn_bwd(xc, h, dhs, wa, ba, wx, bx, lam, reverse, rider=None):
    S = xc.shape[0]
    nt = S // TS_RNN_BWD
    order = (lambda i: i) if reverse else (lambda i: nt - 1 - i)
    per = TS_RNN_BWD // SUB
    last = S // SUB - 1
    if reverse:
        h_halo = pl.BlockSpec((SUB, 512), lambda i: (jnp.minimum((order(i) + 1) * per, last), 0))
    else:
        h_halo = pl.BlockSpec((SUB, 512), lambda i: (jnp.maximum(order(i) * per - 1, 0), 0))
    tile = pl.BlockSpec((TS_RNN_BWD, 512), lambda i: (order(i), 0))

    def body(xc_ref, h_ref, hh_ref, dh_ref, wa_ref, ba_ref, wx_ref, bx_ref, lam_ref,
             dxc_ref, dwa_ref, dwx_ref, dvec_ref, a_s, g_s, carry, edge):
        i = pl.program_id(0)
        t = order(i)

        @pl.when(i == 0)
        def _():
            carry[...] = jnp.zeros_like(carry)
            edge[...] = jnp.zeros_like(edge)
            dwa_ref[...] = jnp.zeros_like(dwa_ref)
            dwx_ref[...] = jnp.zeros_like(dwx_ref)
            dvec_ref[...] = jnp.zeros_like(dvec_ref)

        xc = xc_ref[...]
        xcb, r, gi, cl, a, mult, inv_mult = _lru_gates(xc, wa_ref, ba_ref[...], wx_ref, bx_ref[...], lam_ref[...])
        hv = h_ref[...]
        if reverse:
            a_s[...] = _shifted(a, edge[...], None, -1)
            edge[...] = a[TS_RNN_BWD - SUB:TS_RNN_BWD, :]
            hh = jnp.where(t < nt - 1, hh_ref[...], 0.0)
            h_prev = _shifted(hv, None, hh, 1)
        else:
            a_s[...] = _shifted(a, None, edge[...], 1)
            edge[...] = a[0:SUB, :]
            hh = jnp.where(t > 0, hh_ref[...], 0.0)
            h_prev = _shifted(hv, hh, None, -1)
        _scan_tile(a_s, dh_ref, g_s, carry, not reverse)
        g = g_s[...]
        da = g * h_prev
        gm = g * mult
        d_i = gm * xc
        dmult = g * gi * xc
        dla = da * a - dmult * (a * a) * inv_mult
        d_r = dla * cl
        dpre_r = d_r * r * (1.0 - r)
        dpre_i = d_i * gi * (1.0 - gi)
        dprb = dpre_r.astype(BF16)
        dpib = dpre_i.astype(BF16)
        dwa_ref[...] += _dot_tn(xcb, dprb)
        dwx_ref[...] += _dot_tn(xcb, dpib)
        dvec_ref[0:1, :] += jnp.sum(dpre_r, axis=0, keepdims=True)
        dvec_ref[1:2, :] += jnp.sum(dpre_i, axis=0, keepdims=True)
        dvec_ref[2:3, :] += jnp.sum(dla * r, axis=0, keepdims=True)
        dvec_ref[3:4, :] = dvec_ref[2:3, :] * (LRU_C * _sigmoid(-lam_ref[...]))
        dxc_ref[...] = gm * gi + _dot_nt(dprb, wa_ref[...]) + _dot_nt(dpib, wx_ref[...])

    sq = jax.ShapeDtypeStruct((D_RNN, D_RNN), F32)
    return _call(
        body, "rnn_bwd_rev" if reverse else "rnn_bwd_fwd", (nt,),
        [tile, tile, h_halo, tile] + [_whole_vmem()] * 5,
        [tile, _whole_vmem(), _whole_vmem(), _whole_vmem()],
        [jax.ShapeDtypeStruct((S, 512), F32), sq, sq, jax.ShapeDtypeStruct((SUB, 512), F32)],
        [pltpu.VMEM((TS_RNN_BWD, 512), F32), pltpu.VMEM((TS_RNN_BWD, 512), F32), pltpu.VMEM((SUB, 512), F32),
         pltpu.VMEM((SUB, 512), F32)],
        (xc, h, h, dhs, wa, ba, wx, bx, lam), rider)


def _inproj_bwd(x, dx1, xr, dxc_f, dxc_b, dgate, dq3, dk3, dv3, g1, conv_w, w_in, rider=None):
    S = x.shape[0]
    tb = TS_INPROJ_BWD
    nt = S // tb
    ident = lambda i: i

    def body(x_ref, dx1_ref, xr_ref, xrp_ref, xrn_ref, cf_ref, cfp_ref, cfn_ref, cb_ref, cbp_ref, cbn_ref, dgate_ref,
             dq1, dq2, dq3_, dk1, dk2, dk3_, dv1, dv2, dv3_, g_ref, cw_ref, w_ref,
             dx_ref, dw_ref, dg_ref, dcw_ref, s_ref, w_full, dw_full, sems):
        i = pl.program_id(0)

        def blocks(full, blocked, k0):
            return [(full.at[:, j * IN_BLK:(j + 1) * IN_BLK], blocked.at[j], sems.at[k0 + j]) for j in range(N_SHARD)]

        @pl.when(i == 0)
        def _():
            copies = [pltpu.make_async_copy(src, dst, sem) for dst, src, sem in blocks(w_full, w_ref, 0)]
            for cp in copies:
                cp.start()
            for cp in copies:
                cp.wait()
            dw_full[...] = jnp.zeros_like(dw_full)
            dg_ref[...] = jnp.zeros_like(dg_ref)
            dcw_ref[...] = jnp.zeros_like(dcw_ref)

        first, last = i > 0, i < nt - 1
        dxc = cf_ref[...] + cb_ref[...]
        dxc_p = jnp.where(first, cfp_ref[...] + cbp_ref[...], 0.0)
        dxc_n = jnp.where(last, cfn_ref[...] + cbn_ref[...], 0.0)
        cw = cw_ref[...]
        dxr = (_shifted(dxc, dxc_p, dxc_n, 2) * cw[0:1, :] + _shifted(dxc, dxc_p, dxc_n, 1) * cw[1:2, :]
               + dxc * cw[2:3, :] + _shifted(dxc, dxc_p, dxc_n, -1) * cw[3:4, :])
        xrv = xr_ref[...]
        xr_p = jnp.where(first, xrp_ref[...], 0.0)
        xr_n = jnp.where(last, xrn_ref[...], 0.0)
        for k, off in enumerate((-2, -1, 0, 1)):
            dcw_ref[k:k + 1, :] += jnp.sum(dxc * _shifted(xrv, xr_p, xr_n, off), axis=0, keepdims=True)
        dcw_ref[4:5, :] += jnp.sum(dxc, axis=0, keepdims=True)

        def total(a, b, c_):
            return a[0].astype(F32) + _merge_classes(b, s_ref, F32_LAYOUT, c_)

        dproj = jnp.concatenate(
            [dxr, dgate_ref[...], total(dq1, dq2, dq3_), total(dk1, dk2, dk3_), total(dv1, dv2, dv3_)],
            axis=-1).astype(BF16)
        xv = x_ref[...]
        rstd, xh = _rms(xv)
        hb = (xh * g_ref[...]).astype(BF16)
        dh = _dot_nt(dproj, w_full[...])
        dw_full[...] += _dot_tn(hb, dproj)
        dxn, dg = _rms_bwd(dh, g_ref[...], xh, rstd)
        dg_ref[...] += dg
        dx_ref[...] = dx1_ref[...] + dxn

        @pl.when(i == nt - 1)
        def _():
            copies = [pltpu.make_async_copy(src, dst, sem) for src, dst, sem in blocks(dw_full, dw_ref, N_SHARD)]
            for cp in copies:
                cp.start()
            for cp in copies:
                cp.wait()

    halo = _halo_specs(S, ident, tb)
    return _call(
        body, "inproj_bwd", (nt,),
        [_rows(D_MODEL, tb), _rows(D_MODEL, tb)] + halo * 3 + [_rows(512, tb)]
        + [_class_spec(1, tb), _class_spec(F32_LAYOUT, tb), _class_spec(F32_LAYOUT, tb)] * 3 + [_whole_vmem()] * 2 + [ANY],
        [_rows(D_MODEL, tb), ANY, _whole_vmem(), _whole_vmem()],
        [jax.ShapeDtypeStruct((S, D_MODEL), F32), jax.ShapeDtypeStruct((N_SHARD, D_MODEL, IN_BLK), F32),
         jax.ShapeDtypeStruct((1, D_MODEL), F32), jax.ShapeDtypeStruct((SUB, 512), F32)],
        [pltpu.VMEM((4, tb, 128), F32), pltpu.VMEM((D_MODEL, D_IN), BF16), pltpu.VMEM((D_MODEL, D_IN), F32),
         pltpu.SemaphoreType.DMA((2 * N_SHARD,))],
        (x, dx1, xr, xr, xr, dxc_f, dxc_f, dxc_f, dxc_b, dxc_b, dxc_b, dgate, *dq3, *dk3, *dv3, g1, conv_w, w_in), rider)


def _dw_matmul(a, b, a_cols, b_cols, name):
    S = a.shape[0]
    tk = min(S, TK_DW)
    a_shared = a.shape[1] == a_cols
    b_shared = b.shape[1] == b_cols

    def body(a_ref, b_ref, o_ref):
        @pl.when(pl.program_id(1) == 0)
        def _():
            o_ref[...] = jnp.zeros_like(o_ref)
        o_ref[0] += _dot_tn(a_ref[...], b_ref[...])

    return pl.pallas_call(
        body, grid=(N_SHARD, S // tk), name=name,
        in_specs=[pl.BlockSpec((tk, a_cols), (lambda j, k: (k, 0)) if a_shared else (lambda j, k: (k, j))),
                  pl.BlockSpec((tk, b_cols), (lambda j, k: (k, 0)) if b_shared else (lambda j, k: (k, j)))],
        out_specs=pl.BlockSpec((1, a_cols, b_cols), lambda j, k: (j, 0, 0)),
        out_shape=jax.ShapeDtypeStruct((N_SHARD, a_cols, b_cols), F32),
        compiler_params=_params(2),
    )(a, b)


def _t5_bucket_np(rel):
    nb = N_BUCKETS // 2
    max_exact = nb // 2
    ret = np.where(rel > 0, nb, 0)
    n = np.abs(rel)
    nf = np.maximum(n, 1).astype(np.float32)
    large = max_exact + (np.log(nf / np.float32(max_exact)) / np.float32(math.log(MAX_DISTANCE / max_exact))
                         * np.float32(nb - max_exact)).astype(np.int32)
    large = np.minimum(large, nb - 1)
    return ret + np.where(n < max_exact, n, large)


_VARIANT_OFFSETS = (-HALF_WIN,) * 3


def _band_index():
    kk = np.arange(K_WIN)[None, :]
    ql = np.arange(Q_BLK)[:, None]
    rel = np.stack([kk - ql + off for off in _VARIANT_OFFSETS])
    band = np.abs(rel) <= HALF_WIN
    inside = np.stack([np.broadcast_to(kk >= HALF_WIN, band[0].shape), np.ones_like(band[0]),
                       np.broadcast_to(kk < K_WIN - HALF_WIN, band[0].shape)])
    return rel, band & inside


def _bucket_tables(dil):
    rel, valid = _band_index()
    bucket = _t5_bucket_np(np.clip(rel, -HALF_WIN, HALF_WIN) * dil)
    return np.where(valid, bucket, -1).astype(np.int32)


def _bias_mats(rel_bias, rider=None):
    tables = [_bucket_tables(d) for d in DILATIONS]
    used = [sorted(set(t[t >= 0].tolist())) for t in tables]

    def one_pattern(rb_ref, t_ref, o_ref, buckets):
        bk = t_ref[1]
        for h in range(N_HEADS):
            acc = jnp.full((Q_BLK, K_WIN), NEG_INF, F32)
            for b in buckets:
                acc = jnp.where(bk == b, rb_ref[b, h], acc)
            o_ref[1, h] = acc
            for var in (0, 2):
                o_ref[var, h] = jnp.where(t_ref[var] >= 0, acc, NEG_INF)

    def body(rb_ref, t1, t2, t3, o1, o2, o3):
        for i, (t_ref, o_ref) in enumerate(((t1, o1), (t2, o2), (t3, o3))):
            pl.when(pl.program_id(0) == i)(functools.partial(one_pattern, rb_ref, t_ref, o_ref, used[i]))

    shp = jax.ShapeDtypeStruct((3, N_HEADS, Q_BLK, K_WIN), F32)
    return _call(
        body, "bias_tables", (len(DILATIONS),), [pl.BlockSpec(memory_space=pltpu.SMEM)] + [_whole_vmem()] * 3,
        [_whole_vmem()] * 3, [shp] * 3, [], (rel_bias, *[jnp.asarray(t) for t in tables]), rider)


def _variant(qb, nq):
    return jnp.where(qb == 0, 0, jnp.where(qb == nq - 1, 2, 1))


def _win_start(qb):
    return pl.multiple_of(qb * Q_BLK, Q_BLK)


def _fill_padded(src_ref, pad_ref):
    L = src_ref.shape[0]
    edge = jnp.zeros((HALF_WIN, 128), pad_ref.dtype)
    pad_ref[0:HALF_WIN, :] = edge
    pad_ref[HALF_WIN:HALF_WIN + L, :] = src_ref[...]
    pad_ref[HALF_WIN + L:2 * HALF_WIN + L, :] = edge


INNER = {1: 1, 4: 1, 16: 4}


def _attn_layout(dil, L, blocks=ATTN_SUB, outer=None):
    inner = INNER[dil]
    n_outer = dil // inner
    nsub = min(blocks // inner, L // Q_BLK)
    qt = nsub * Q_BLK
    grid = (4, n_outer, L // qt)
    qspec = pl.BlockSpec((inner, None, qt, 128), lambda hp, r, s: (0, r, s, hp))
    kspec = pl.BlockSpec((inner, None, L, 128), lambda hp, r, s: (0, r, 0, hp))
    bspec = pl.BlockSpec((3, 2, Q_BLK, K_WIN), lambda hp, r, s: (0, hp, 0, 0))
    kfspec = pl.BlockSpec((None, inner * L, 128), lambda hp, r, s: (r, 0, hp))
    qfspec = kfspec if inner > 1 else pl.BlockSpec((None, qt, 128), lambda hp, r, s: (r, s, hp))
    fshape = jax.ShapeDtypeStruct((n_outer, inner * L, D_ATTN), F32)
    view = lambda t: t.reshape(inner, n_outer, L, D_ATTN)

    def qrows(m, sub):
        if inner == 1:
            return (slice(sub * Q_BLK, (sub + 1) * Q_BLK), slice(None))
        first = (pl.program_id(2) * nsub + sub) * Q_BLK
        return (pl.ds(m + inner * first, Q_BLK, stride=inner), slice(None))

    def krows(m):
        if inner == 1:
            return (slice(None), slice(None))
        return (pl.ds(m, L, stride=inner), slice(None))

    if outer is not None:
        grid = (4, n_outer // outer, L // qt)
        qspec = pl.BlockSpec((inner, outer, qt, 128), lambda hp, r, s: (0, r, s, hp))
        kspec = pl.BlockSpec((inner, outer, L, 128), lambda hp, r, s: (0, r, 0, hp))
        kfspec = pl.BlockSpec((outer, inner * L, 128), lambda hp, r, s: (r, 0, hp))
        qfspec = kfspec if inner > 1 else pl.BlockSpec((outer, qt, 128), lambda hp, r, s: (r, s, hp))
    return inner, nsub, grid, qspec, kspec, bspec, qfspec, kfspec, fshape, view, qrows, krows


def _head_masks():
    lane = lax.broadcasted_iota(jnp.int32, (Q_BLK, 128), 1)
    return lane < HEAD_DIM


def _attn_fwd(q, k, v, bias):
    dil, L, _ = q.shape
    nq = L // Q_BLK
    n_cls = dil // INNER[dil]
    inner, nsub, grid, qspec, kspec, bspec, qfspec, kfspec, fshape, view, qrows, krows = _attn_layout(
        dil, L, ATTN_SUB_FWD, n_cls)

    def body(q_ref, k_ref, v_ref, b_ref, o_ref, l_ref, kp, vp):
        step = pl.program_id(2)

        @pl.when(step == 0)
        def _():
            for c in range(n_cls):
                for m in range(inner):
                    _fill_padded(k_ref.at[m, c], kp.at[c * inner + m])
                    _fill_padded(v_ref.at[m, c], vp.at[c * inner + m])

        h0 = _head_masks()
        for c, m, sub in [(c, m, sub) for c in range(n_cls) for m in range(inner) for sub in range(nsub)]:
            qb = step * nsub + sub
            st = _win_start(qb)
            var = _variant(qb, nq)
            kw = kp[c * inner + m, pl.ds(st, K_WIN), :]
            vw = vp[c * inner + m, pl.ds(st, K_WIN), :]
            qs = q_ref[m, c, sub * Q_BLK:(sub + 1) * Q_BLK, :] * ATTN_SCALE
            zq = jnp.zeros_like(qs)
            q2 = jnp.concatenate([jnp.where(h0, qs, zq), jnp.where(h0, zq, qs)], axis=0)
            s = _dot_nt(q2, kw) + b_ref[var].reshape(2 * Q_BLK, K_WIN)
            top = jnp.max(s, axis=-1, keepdims=True)
            p = jnp.exp(s - top)
            l = jnp.sum(p, axis=-1, keepdims=True)
            out = _dot(p.astype(BF16), vw) / l
            lse = top + jnp.log(l)
            o_ref.at[c][qrows(m, sub)] = jnp.where(h0, out[0:Q_BLK], out[Q_BLK:2 * Q_BLK])
            l_ref.at[c][qrows(m, sub)] = jnp.where(h0, lse[0:Q_BLK], lse[Q_BLK:2 * Q_BLK])

    return pl.pallas_call(
        body, grid=grid, name=f"attn_fwd_d{dil}",
        in_specs=[qspec, kspec, kspec, bspec], out_specs=[qfspec, qfspec], out_shape=[fshape, fshape],
        scratch_shapes=[pltpu.VMEM((n_cls * inner, L + 2 * HALF_WIN, 128), BF16)] * 2,
        compiler_params=_params(3),
    )(view(q), view(k), view(v), bias)


def _attn_bwd(q, k, v, bias, do, o, lse, rider=None):
    dil, L, _ = q.shape
    nq = L // Q_BLK
    inner, nsub, grid, qspec, kspec, bspec, qfspec, kfspec, fshape, view, qrows, krows = _attn_layout(dil, L)
    nstep = grid[2]

    def body(q_ref, k_ref, v_ref, b_ref, do_ref, o_ref, l_ref, dq_ref, dk_ref, dv_ref, db_ref, db_s,
             kp, vp, dkp, dvp, carry):
        hp, step = pl.program_id(0), pl.program_id(2)
        first = (hp == 0) & (pl.program_id(1) == 0) & (step == 0)
        last = (hp == grid[0] - 1) & (pl.program_id(1) == grid[1] - 1) & (step == nstep - 1)

        @pl.when(first)
        def _():
            db_s[...] = jnp.zeros_like(db_s)

        @pl.when(step == 0)
        def _():
            for m in range(inner):
                _fill_padded(k_ref.at[m], kp.at[m])
                _fill_padded(v_ref.at[m], vp.at[m])
            carry[...] = jnp.zeros_like(carry)

        h0 = _head_masks()
        for m, sub in [(m, sub) for m in range(inner) for sub in range(nsub)]:
            if sub == 0:
                carry_k, carry_v = carry[m, 0], carry[m, 1]
            qb = step * nsub + sub
            st = _win_start(qb)
            var = _variant(qb, nq)
            kw = kp[m, pl.ds(st, K_WIN), :]
            vw = vp[m, pl.ds(st, K_WIN), :]
            qs = q_ref[m, sub * Q_BLK:(sub + 1) * Q_BLK, :] * ATTN_SCALE
            dof = do_ref[qrows(m, sub)]
            dob = dof.astype(BF16)
            prod = dof * o_ref[qrows(m, sub)]
            lsev = l_ref[qrows(m, sub)]
            zq, zd = jnp.zeros_like(qs), jnp.zeros_like(dob)
            q2 = jnp.concatenate([jnp.where(h0, qs, zq), jnp.where(h0, zq, qs)], axis=0)
            do2 = jnp.concatenate([jnp.where(h0, dob, zd), jnp.where(h0, zd, dob)], axis=0)
            lse2 = jnp.concatenate([lsev[:, 0:1], lsev[:, HEAD_DIM:HEAD_DIM + 1]], axis=0)
            dd2 = jnp.concatenate([jnp.sum(jnp.where(h0, prod, 0.0), axis=-1, keepdims=True),
                                   jnp.sum(jnp.where(h0, 0.0, prod), axis=-1, keepdims=True)], axis=0)
            s = _dot_nt(q2, kw) + b_ref[var].reshape(2 * Q_BLK, K_WIN)
            p = jnp.exp(s - lse2)
            ds = p * (_dot_nt(do2, vw) - dd2)
            db_s[pl.ds(hp * 2, 2)] += ds.reshape(2, Q_BLK, K_WIN)
            dsb = ds.astype(BF16)
            dv_acc = _dot_tn(p.astype(BF16), do2)
            dk_acc = _dot_tn(dsb, q2)
            dq2 = _dot(dsb, kw) * ATTN_SCALE
            dq_ref[qrows(m, sub)] = jnp.where(h0, dq2[0:Q_BLK], dq2[Q_BLK:2 * Q_BLK]).astype(dq_ref.dtype)
            dkp[m, pl.ds(st, Q_BLK), :] = carry_k + dk_acc[0:Q_BLK]
            dvp[m, pl.ds(st, Q_BLK), :] = carry_v + dv_acc[0:Q_BLK]
            carry_k, carry_v = dk_acc[Q_BLK:K_WIN], dv_acc[Q_BLK:K_WIN]
            if sub == nsub - 1:
                carry[m, 0] = carry_k
                carry[m, 1] = carry_v

        @pl.when(step == nstep - 1)
        def _():
            for m in range(inner):
                dkp[m, L:L + Q_BLK, :] = carry[m, 0]
                dvp[m, L:L + Q_BLK, :] = carry[m, 1]
                dk_ref[krows(m)] = dkp[m, HALF_WIN:HALF_WIN + L, :].astype(dk_ref.dtype)
                dv_ref[krows(m)] = dvp[m, HALF_WIN:HALF_WIN + L, :].astype(dv_ref.dtype)

        @pl.when(last)
        def _():
            db_ref[...] = db_s[...]

    dbshape = (N_HEADS, Q_BLK, K_WIN)
    gshape = jax.ShapeDtypeStruct(fshape.shape, BF16 if inner == 1 else F32)
    return _call(
        body, f"attn_bwd_d{dil}", grid,
        [qspec, kspec, kspec, bspec, qfspec, qfspec, qfspec],
        [qfspec, kfspec, kfspec, _whole_vmem()],
        [gshape, gshape, gshape, jax.ShapeDtypeStruct(dbshape, F32)],
        [pltpu.VMEM(dbshape, F32)] + [pltpu.VMEM((inner, L + 2 * HALF_WIN, 128), BF16)] * 2
        + [pltpu.VMEM((inner, L + 2 * HALF_WIN, 128), F32)] * 2 + [pltpu.VMEM((inner, 2, Q_BLK, 128), F32)],
        (view(q), view(k), view(v), bias, do, o, lse), rider)


def _bucket_onehots(dil):
    m = np.zeros((3, K_WIN, N_BUCKETS), np.float32)
    for var, off in enumerate(_VARIANT_OFFSETS):
        for rel in range(-HALF_WIN, HALF_WIN + 1):
            col = (rel - off + Q_BLK - 1) % K_WIN
            m[var, col, int(_t5_bucket_np(np.asarray(rel * dil)))] = 1.0
    return jnp.asarray(m)


def _bias_grad(dbs):
    onehots = [_bucket_onehots(d) for d in DILATIONS]
    flip = jnp.asarray(np.eye(Q_BLK, dtype=np.float32)[::-1].copy())

    def body(d1, d2, d3, m1, m2, m3, flip_ref, out_ref):
        hp = lax.Precision.HIGHEST
        acc = jnp.zeros((N_HEADS, N_BUCKETS), F32)
        for d_ref, m_ref in ((d1, m1), (d2, m2), (d3, m3)):
            rows = []
            for h in range(N_HEADS):
                xrev = jnp.dot(flip_ref[...], d_ref[h], precision=hp, preferred_element_type=F32)
                y = pltpu.roll(xrev, 0, 1, stride=1, stride_axis=0)
                rows.append(jnp.sum(y, axis=0, keepdims=True))
            acc = acc + jnp.dot(jnp.concatenate(rows, axis=0), m_ref[1], precision=hp, preferred_element_type=F32)
        out_ref[...] = acc

    return pl.pallas_call(
        body, name="bias_grad", out_shape=jax.ShapeDtypeStruct((N_HEADS, N_BUCKETS), F32),
        compiler_params=_params(0),
    )(*dbs, *onehots, flip)


def _block_diag(w):
    eye = jnp.eye(N_RNN_BLOCKS, dtype=w.dtype)
    return jnp.einsum("ncd,nm->ncmd", w, eye).reshape(D_RNN, D_RNN).astype(BF16)


def _diag_blocks(dense):
    d = dense.reshape(N_RNN_BLOCKS, RNN_BLOCK, N_RNN_BLOCKS, RNN_BLOCK)
    return jnp.stack([d[n, :, n, :] for n in range(N_RNN_BLOCKS)])


EARLY = ("w_out", "w_up", "w_down")


def _local_step(x, target, p, shards=None):
    p = dict(p)
    first = None if shards is None else _gather_rider(["w_in"], [shards["w_in"]], shards["conv_w"])
    biases, got = _bias_mats(p["rel_bias"], first)
    if shards is not None:
        p["w_in"] = got[0]
        p["conv_w"] = jnp.transpose(got[1], (1, 0, 2)).reshape(4, D_RNN)
    lru = {}
    for dname in ("fwd", "bwd"):
        lru[dname] = (_block_diag(p["lru_wa_" + dname]), p["lru_ba_" + dname], _block_diag(p["lru_wx_" + dname]),
                      p["lru_bx_" + dname], p["lru_lam_" + dname])

    def gather(name):
        return None if shards is None else _gather_rider([name], [shards[name]])

    (xr, gate, *qkv), got = _inproj_fwd(x, p["attn_norm_g"], p["w_in"], gather("w_out"))
    p.update(zip(["w_out"], got))
    qs, ks, vs = qkv[0:3], qkv[3:6], qkv[6:9]
    (hf, xc), got = _rnn_fwd(xr, p["conv_w"], p["conv_b"], *lru["fwd"], reverse=False, rider=gather("w_up"))
    p.update(zip(["w_up"], got))
    (hb,), got = _rnn_fwd(xr, p["conv_w"], p["conv_b"], *lru["bwd"], reverse=True, rider=gather("w_down"), xc=xc)
    p.update(zip(["w_down"], got))
    outs, lses = [], []
    for q, k, v, bias in zip(qs, ks, vs, biases):
        o, l = _attn_fwd(q, k, v, bias)
        outs.append(o)
        lses.append(l)
    x1, mixb, *yl = _mix_fwd(outs, lses, hf, hb, gate, x, p["norm_rnn_g"], p["norm_attn_g"], p["w_out"])
    yas, lsts = [yl[0], yl[1], yl[1]], [yl[2], yl[3], yl[3]]
    dx1, h2b, a2b, dub, dx2b, loss_vec, dg_fin, dg_mlp = _mlp_fwd_bwd(
        x1, target, p["mlp_norm_g"], p["final_norm_g"], p["w_up"], p["w_down"])
    dhs, dgate, dya1, dya4, dw_out, dg_rnn, dg_attn = _mix_bwd(dx1, p["w_out"], mixb, yas[0], hf, hb, gate,
                                                               p["norm_rnn_g"], p["norm_attn_g"])
    dyas = [dya1, dya4, dya4]
    dw_up = _dw_matmul(h2b, dub, D_MODEL, FF_BLK, "dw_up")
    dw_down = _dw_matmul(a2b, dx2b, FF_BLK, D_MODEL, "dw_down")
    early = [dw_out, dw_up, dw_down]
    dqs, dks, dvs, dbs = [], [], [], []
    for i, (q, k, v, bias, dya, ya, lse) in enumerate(zip(qs, ks, vs, biases, dyas, yas, lsts)):
        rider = None
        if shards is not None:
            make = (lambda: _pair_exchange_rider(EARLY, early), lambda: _chip_exchange_rider(early),
                    lambda: _pair_share_rider(EARLY, early))[i]
            rider = make()
        (dq, dk, dv, db), got = _attn_bwd(q, k, v, bias, dya, ya, lse, rider)
        if shards is not None and i == 0:
            core = lax.axis_index("c").reshape(1).astype(jnp.int32)
            early = [_pair_add(core, g, o, "grad_pair_add_" + n) for n, g, o in zip(EARLY, early, got)]
        elif shards is not None and i == 1:
            early = [_chip_sum(t, "grad_chip_sum_" + n) for n, t in zip(EARLY, got)]
        elif shards is not None:
            early = got
        dqs.append(dq)
        dks.append(dk)
        dvs.append(dv)
        dbs.append(db)
    d_rel_bias = _bias_grad(dbs).T
    (dxc_f, dwa_f, dwx_f, dvec_f), _ = _rnn_bwd(xc, hf, dhs, *lru["fwd"], reverse=False)
    small = {
        "lru_wa_fwd": _diag_blocks(dwa_f), "lru_ba_fwd": dvec_f[0:1], "lru_wx_fwd": _diag_blocks(dwx_f),
        "lru_bx_fwd": dvec_f[1:2], "lru_lam_fwd": dvec_f[3:4],
        "rel_bias": d_rel_bias, "norm_rnn_g": dg_rnn, "norm_attn_g": dg_attn,
        "mlp_norm_g": dg_mlp, "final_norm_g": dg_fin,
    }
    loss_local = (0.5 / D_MODEL) * jnp.sum(loss_vec)
    rider = None
    if shards is not None:
        rider = _small_gather_rider(_pack([small[n].reshape(shp) for n, shp in SMALL if n in small]
                                          + [loss_local.reshape(1)]))
    (dxc_b, dwa_b, dwx_b, dvec_b), gathered = _rnn_bwd(xc, hb, dhs, *lru["bwd"], reverse=True, rider=rider)
    grad_x, dw_in, dg1, dconv = _inproj_bwd(x, dx1, xr, dxc_f, dxc_b, dgate, dqs, dks, dvs,
                                            p["attn_norm_g"], p["conv_w"], p["w_in"])[0]
    last = {"lru_wa_bwd": _diag_blocks(dwa_b), "lru_ba_bwd": dvec_b[0:1], "lru_wx_bwd": _diag_blocks(dwx_b),
            "lru_bx_bwd": dvec_b[1:2], "lru_lam_bwd": dvec_b[3:4],
            "attn_norm_g": dg1, "conv_w": dconv[0:4], "conv_b": dconv[4:5]}
    if shards is None:
        big = {"w_in": dw_in, "w_out": dw_out, "w_up": dw_up, "w_down": dw_down}
        return loss_local, grad_x, {**small, **last}, None, big, {}
    return loss_local, grad_x, last, gathered[0], {"w_in": dw_in}, dict(zip(EARLY, early))


BIG = ("w_in", "w_out", "w_up", "w_down")
BIG_SHARD = {"w_in": (D_MODEL, IN_BLK), "w_out": (OUT_BLK, D_MODEL), "w_up": (D_MODEL, FF_BLK), "w_down": (FF_BLK, D_MODEL)}
N_BIG = len(BIG)
N_CHIP_PEERS = 3
ANY = pl.BlockSpec(memory_space=pl.ANY)


def _place():
    x, y, c = lax.axis_index("x"), lax.axis_index("y"), lax.axis_index("c")
    chips = [(1 - x, y), (x, 1 - y), (1 - x, 1 - y)]
    return x, y, c, chips


def _remote(src, dst, send_sem, recv_sem, dev):
    return pltpu.make_async_remote_copy(src_ref=src, dst_ref=dst, send_sem=send_sem, recv_sem=recv_sem,
                                        device_id=dev, device_id_type=MESH)


def _staged_start(srcs, bufs, sems):
    legs = [pltpu.make_async_copy(s, b, sems.at[i]) for i, (s, b) in enumerate(zip(srcs, bufs))]
    for cp in legs:
        cp.start()
    return legs


def _staged_finish(legs, bufs, dsts, sems):
    out = []
    for i, (leg, b, d) in enumerate(zip(legs, bufs, dsts)):
        leg.wait()
        cp = pltpu.make_async_copy(b, d, sems.at[i])
        cp.start()
        out.append(cp)
    return out


class _Rider:
    def __init__(self, inputs, out_shape, scratch, first, late, last):
        self.inputs, self.out_shape, self.scratch = list(inputs), list(out_shape), list(scratch)
        self.first, self.late, self.last = first, late, last


def _call(body, name, grid, in_specs, out_specs, out_shape, scratch, operands, rider=None):
    n_grid = len(grid)
    if rider is None:
        res = pl.pallas_call(body, grid=grid, name=name, in_specs=in_specs, out_specs=out_specs, out_shape=out_shape,
                             scratch_shapes=scratch, compiler_params=_params(n_grid))(*operands)
        return list(res), []
    n_in, n_out, n_scr = len(in_specs), len(out_specs), len(scratch)
    ri, ro = len(rider.inputs), len(rider.out_shape)
    nsteps = int(np.prod(grid))
    late_step = max(nsteps - 3, 1)

    def wrapped(*refs):
        a, b = n_in, n_in + ri
        c, d = b + n_out, b + n_out + ro
        e = d + n_scr
        mine = refs[:a] + refs[b:c] + refs[d:e]
        theirs = (refs[a:b], refs[c:d], refs[e:])
        step = pl.program_id(0)
        for ax in range(1, n_grid):
            step = step * grid[ax] + pl.program_id(ax)
        pl.when(step == 0)(lambda: rider.first(*theirs))
        pl.when(step == late_step)(lambda: rider.late(*theirs))
        body(*mine)
        pl.when(step == nsteps - 1)(lambda: rider.last(*theirs))

    res = pl.pallas_call(
        wrapped, grid=grid, name=name, in_specs=list(in_specs) + [ANY] * ri, out_specs=list(out_specs) + [ANY] * ro,
        out_shape=list(out_shape) + rider.out_shape, scratch_shapes=list(scratch) + rider.scratch,
        compiler_params=_params(n_grid),
    )(*operands, *rider.inputs)
    return list(res[:n_out]), list(res[n_out:])


def _run_rider(rider, name):
    ri, ro = len(rider.inputs), len(rider.out_shape)

    def body(*refs):
        parts = (refs[:ri], refs[ri:ri + ro], refs[ri + ro:])
        rider.first(*parts)
        rider.late(*parts)
        rider.last(*parts)

    return list(pl.pallas_call(
        body, name=name, in_specs=[ANY] * ri, out_specs=[ANY] * ro, out_shape=rider.out_shape, scratch_shapes=rider.scratch,
        compiler_params=pltpu.CompilerParams(has_side_effects=True, vmem_limit_bytes=VMEM_LIMIT),
    )(*rider.inputs))


def _nothing(ins, outs, scr):
    return None


def _gather_rider(names, shards, conv_w=None):
    n = len(names)
    items = n + (conv_w is not None)
    halves = [BIG_SHARD[nm][0] // 2 for nm in names]

    def parts(ins, outs, scr):
        x, y, c, chips = _place()
        return x, y, c, chips, 2 * x + y, (x, y, 1 - c), scr[:8], scr[8:]

    def piece(outs, w, chip, core_half):
        return outs[w].at[chip, pl.ds(core_half * halves[w], halves[w])]

    def ici(ins, outs, sems, w, k, chip_xy, c, me):
        return _remote(ins[w].at[pl.ds(c * halves[w], halves[w])], piece(outs, w, me, c),
                       sems[0].at[w, k], sems[1].at[w, k], (*chip_xy, c))

    def first(ins, outs, scr):
        x, y, c, chips, me, sibling, sems, bufs = parts(ins, outs, scr)
        legs = _staged_start(ins, bufs, sems[6])
        for w in range(n):
            for k, chip_xy in enumerate(chips):
                ici(ins, outs, sems, w, k, chip_xy, c, me).start()
        if conv_w is not None:
            for k, (px, py) in enumerate(chips):
                _remote(ins[n], outs[n].at[me], sems[4].at[k], sems[5].at[k], (px, py, c)).start()
        _staged_finish(legs, bufs, [o.at[me] for o in outs], sems[7])

    def late(ins, outs, scr):
        x, y, c, chips, me, sibling, sems, bufs = parts(ins, outs, scr)
        for w in range(n):
            for k, (px, py) in enumerate(chips):
                landed = piece(outs, w, 2 * px + py, c)
                _remote(landed, landed, sems[0].at[w, k], sems[1].at[w, k], (px, py, c)).wait_recv()
                _remote(landed, landed, sems[2].at[w, k], sems[3].at[w, k], sibling).start()

    def last(ins, outs, scr):
        x, y, c, chips, me, sibling, sems, bufs = parts(ins, outs, scr)
        for w in range(n):
            for k, (px, py) in enumerate(chips):
                other = piece(outs, w, 2 * px + py, 1 - c)
                _remote(other, other, sems[2].at[w, k], sems[3].at[w, k], sibling).wait_recv()
        if conv_w is not None:
            for k, (px, py) in enumerate(chips):
                got = outs[n].at[2 * px + py]
                _remote(got, got, sems[4].at[k], sems[5].at[k], (px, py, c)).wait_recv()
                _remote(ins[n], outs[n].at[me], sems[4].at[k], sems[5].at[k], (px, py, c)).wait_send()
        for i in range(items):
            pltpu.make_async_copy(bufs[i], outs[i].at[me], sems[7].at[i]).wait()
        for w in range(n):
            for k, (px, py) in enumerate(chips):
                ici(ins, outs, sems, w, k, (px, py), c, me).wait_send()
                landed = piece(outs, w, 2 * px + py, c)
                _remote(landed, landed, sems[2].at[w, k], sems[3].at[w, k], sibling).wait_send()

    out_shape = [jax.ShapeDtypeStruct((N_SHARD,) + BIG_SHARD[nm], BF16) for nm in names]
    stage = [pltpu.VMEM(BIG_SHARD[nm], BF16) for nm in names]
    inputs = list(shards)
    if conv_w is not None:
        out_shape.append(jax.ShapeDtypeStruct((N_SHARD,) + conv_w.shape, F32))
        stage.append(pltpu.VMEM(conv_w.shape, F32))
        inputs.append(conv_w)
    scratch = ([pltpu.SemaphoreType.DMA((n, N_CHIP_PEERS))] * 4 + [pltpu.SemaphoreType.DMA((N_CHIP_PEERS,))] * 2
               + [pltpu.SemaphoreType.DMA((items,))] * 2 + stage)
    return _Rider(inputs, out_shape, scratch, first, late, last)


def _pair_exchange_rider(names, grads):
    def copies(ins, outs, scr):
        x, y, c, _ = _place()
        out = []
        for w, nm in enumerate(names):
            h = BIG_SHARD[nm][0] // 2
            out.append(_remote(ins[w].at[:, pl.ds((1 - c) * h, h), :], outs[w], scr[0].at[w], scr[1].at[w], (x, y, 1 - c)))
        return out

    def first(ins, outs, scr):
        for cp in copies(ins, outs, scr):
            cp.start()

    def last(ins, outs, scr):
        for cp in copies(ins, outs, scr):
            cp.wait()

    out_shape = [jax.ShapeDtypeStruct((N_SHARD, BIG_SHARD[nm][0] // 2, BIG_SHARD[nm][1]), F32) for nm in names]
    return _Rider(grads, out_shape, [pltpu.SemaphoreType.DMA((len(names),))] * 2, first, _nothing, last)


def _pair_add(core, grad, other, name):
    _, r, cols = grad.shape
    h = r // 2
    th = min(h, 256)
    per = h // th

    def body(c_ref, g_ref, o_ref, out_ref):
        out_ref[...] = (g_ref[...] + o_ref[...]).astype(BF16)

    return pl.pallas_call(
        body, name=name,
        grid_spec=pltpu.PrefetchScalarGridSpec(
            num_scalar_prefetch=1, grid=(N_SHARD, per),
            in_specs=[pl.BlockSpec((1, th, cols), lambda j, i, c_ref: (j, c_ref[0] * per + i, 0)),
                      pl.BlockSpec((1, th, cols), lambda j, i, c_ref: (j, i, 0))],
            out_specs=pl.BlockSpec((1, th, cols), lambda j, i, c_ref: (j, i, 0))),
        out_shape=jax.ShapeDtypeStruct((N_SHARD, h, cols), BF16),
        compiler_params=_params(2),
    )(core, grad, other)


def _chip_exchange_rider(parts):
    n = len(parts)

    def sends(ins, outs, scr):
        x, y, c, chips = _place()
        me = 2 * x + y
        return [_remote(ins[w].at[2 * px + py], outs[w].at[me], scr[0].at[w, k], scr[1].at[w, k], (px, py, c))
                for w in range(n) for k, (px, py) in enumerate(chips)]

    def first(ins, outs, scr):
        x, y, c, chips = _place()
        me = 2 * x + y
        legs = _staged_start([r.at[me] for r in ins], scr[4:], scr[2])
        for cp in sends(ins, outs, scr):
            cp.start()
        _staged_finish(legs, scr[4:], [o.at[me] for o in outs], scr[3])

    def last(ins, outs, scr):
        x, y, c, chips = _place()
        me = 2 * x + y
        for w in range(n):
            for k, (px, py) in enumerate(chips):
                got = outs[w].at[2 * px + py]
                _remote(got, got, scr[0].at[w, k], scr[1].at[w, k], (px, py, c)).wait_recv()
        for cp in sends(ins, outs, scr):
            cp.wait_send()
        for w in range(n):
            pltpu.make_async_copy(scr[4 + w], outs[w].at[me], scr[3].at[w]).wait()

    out_shape = [jax.ShapeDtypeStruct(p.shape, BF16) for p in parts]
    scratch = ([pltpu.SemaphoreType.DMA((n, N_CHIP_PEERS))] * 2 + [pltpu.SemaphoreType.DMA((n,))] * 2
               + [pltpu.VMEM(p.shape[1:], BF16) for p in parts])
    return _Rider(parts, out_shape, scratch, first, _nothing, last)


def _chip_sum(parts, name):
    _, h, cols = parts.shape
    th = min(h, 256)

    def body(p_ref, out_ref):
        acc = p_ref[0].astype(F32)
        for j in range(1, N_SHARD):
            acc = acc + p_ref[j].astype(F32)
        out_ref[...] = acc

    return pl.pallas_call(
        body, name=name, grid=(h // th,),
        in_specs=[pl.BlockSpec((N_SHARD, th, cols), lambda i: (0, i, 0))],
        out_specs=pl.BlockSpec((th, cols), lambda i: (i, 0)),
        out_shape=jax.ShapeDtypeStruct((h, cols), F32),
        compiler_params=_params(),
    )(parts)


def _pair_share_rider(names, halves):
    n = len(names)
    hs = [BIG_SHARD[nm][0] // 2 for nm in names]

    def mine(outs, c):
        return [outs[w].at[pl.ds(c * hs[w], hs[w])] for w in range(n)]

    def first(ins, outs, scr):
        x, y, c, _ = _place()
        legs = _staged_start(ins, scr[4:], scr[2])
        for w, dst in enumerate(mine(outs, c)):
            _remote(ins[w], dst, scr[0].at[w], scr[1].at[w], (x, y, 1 - c)).start()
        _staged_finish(legs, scr[4:], mine(outs, c), scr[3])

    def last(ins, outs, scr):
        x, y, c, _ = _place()
        for w, (theirs, dst) in enumerate(zip(mine(outs, 1 - c), mine(outs, c))):
            _remote(theirs, theirs, scr[0].at[w], scr[1].at[w], (x, y, 1 - c)).wait_recv()
            _remote(ins[w], dst, scr[0].at[w], scr[1].at[w], (x, y, 1 - c)).wait_send()
            pltpu.make_async_copy(scr[4 + w], dst, scr[3].at[w]).wait()

    out_shape = [jax.ShapeDtypeStruct(BIG_SHARD[nm], F32) for nm in names]
    scratch = [pltpu.SemaphoreType.DMA((n,))] * 4 + [pltpu.VMEM((h, BIG_SHARD[nm][1]), F32) for nm, h in zip(names, hs)]
    return _Rider(halves, out_shape, scratch, first, _nothing, last)


N_DEV = 8


def _all_peers(x, y, c):
    return [((1 - x) if fx else x, (1 - y) if fy else y, (1 - c) if fc else c)
            for fx in (0, 1) for fy in (0, 1) for fc in (0, 1) if fx or fy or fc]


def _small_gather_rider(vec):
    def sends(ins, outs, scr):
        x, y, c, _ = _place()
        me = 4 * x + 2 * y + c
        return [_remote(ins[0], outs[0].at[me], scr[0].at[k], scr[1].at[k], dev) for k, dev in enumerate(_all_peers(x, y, c))]

    def first(ins, outs, scr):
        x, y, c, _ = _place()
        legs = _staged_start(ins, scr[4:], scr[2])
        for cp in sends(ins, outs, scr):
            cp.start()
        _staged_finish(legs, scr[4:], [outs[0].at[4 * x + 2 * y + c]], scr[3])

    def last(ins, outs, scr):
        x, y, c, _ = _place()
        for k, (px, py, pc) in enumerate(_all_peers(x, y, c)):
            got = outs[0].at[4 * px + 2 * py + pc]
            _remote(got, got, scr[0].at[k], scr[1].at[k], (px, py, pc)).wait_recv()
        for cp in sends(ins, outs, scr):
            cp.wait_send()
        pltpu.make_async_copy(scr[4], outs[0].at[4 * x + 2 * y + c], scr[3].at[0]).wait()

    scratch = ([pltpu.SemaphoreType.DMA((N_DEV - 1,))] * 2 + [pltpu.SemaphoreType.DMA((1,))] * 2
               + [pltpu.VMEM(vec.shape, F32)])
    return _Rider([vec], [jax.ShapeDtypeStruct((N_DEV,) + vec.shape, F32)], scratch, first, _nothing, last)


def _sum_devices(gathered):
    def body(g_ref, out_ref):
        acc = g_ref[0]
        for j in range(1, N_DEV):
            acc = acc + g_ref[j]
        out_ref[...] = acc

    return pl.pallas_call(body, name="sum_devices", out_shape=jax.ShapeDtypeStruct(gathered.shape[1:], F32),
                          compiler_params=_params(0))(gathered)


def _allreduce_small(vec):
    rows = vec.shape[0]

    def body(v_ref, sum_ref, gat_ref, send, recv, loc_sem):
        x, y, c, chips = _place()
        sibling = (x, y, 1 - c)
        slot = lambda px, py, pc: gat_ref.at[4 * px + 2 * py + pc]
        lc = pltpu.make_async_copy(v_ref, slot(x, y, c), loc_sem)
        lc.start()
        sends = [_remote(v_ref, slot(x, y, c), send.at[0], recv.at[0], sibling)]
        sends += [_remote(v_ref, slot(x, y, c), send.at[1 + k], recv.at[1 + k], (px, py, c))
                  for k, (px, py) in enumerate(chips)]
        for cp in sends:
            cp.start()
        for k, (px, py) in enumerate(chips):
            got = slot(px, py, c)
            _remote(got, got, send.at[1 + k], recv.at[1 + k], (px, py, c)).wait_recv()
            cp = _remote(got, got, send.at[4 + k], recv.at[4 + k], sibling)
            cp.start()
            sends.append(cp)
        got = slot(x, y, 1 - c)
        _remote(got, got, send.at[0], recv.at[0], sibling).wait_recv()
        for k, (px, py) in enumerate(chips):
            got = slot(px, py, 1 - c)
            _remote(got, got, send.at[4 + k], recv.at[4 + k], sibling).wait_recv()
        for cp in sends:
            cp.wait_send()
        lc.wait()
        acc = gat_ref[0]
        for j in range(1, N_DEV):
            acc = acc + gat_ref[j]
        sum_ref[...] = acc

    total, _ = pl.pallas_call(
        body, name="allreduce_small",
        in_specs=[_whole_vmem()], out_specs=[_whole_vmem(), _whole_vmem()],
        out_shape=[jax.ShapeDtypeStruct((rows, 128), F32), jax.ShapeDtypeStruct((N_DEV, rows, 128), F32)],
        scratch_shapes=[pltpu.SemaphoreType.DMA((N_DEV - 1,))] * 2 + [pltpu.SemaphoreType.DMA(())],
        compiler_params=pltpu.CompilerParams(has_side_effects=True, vmem_limit_bytes=VMEM_LIMIT),
    )(vec)
    return total


def _adam_math(w_ref, g_ref, m_ref, v_ref, d_ref, m2_ref, v2_ref):
    c1 = 1.0 - ADAM_B1 ** ADAM_STEP
    c2 = 1.0 - ADAM_B2 ** ADAM_STEP
    gv = g_ref[...]
    m2 = ADAM_B1 * m_ref[...] + (1.0 - ADAM_B1) * gv
    v2 = ADAM_B2 * v_ref[...] + (1.0 - ADAM_B2) * (gv * gv)
    m2_ref[...] = m2
    v2_ref[...] = v2
    d_ref[...] = -ADAM_LR * ((m2 / c1) / (jnp.sqrt(v2 / c2) + ADAM_EPS) + ADAM_WD * w_ref[...])


def _adamw_many(ws, gs, ms, vs):
    n = len(ws)

    def body(*refs):
        for i in range(n):
            _adam_math(*[refs[k * n + i] for k in range(7)])

    shapes = [jax.ShapeDtypeStruct(w.shape, F32) for w in ws]
    res = pl.pallas_call(body, name="adamw_small", out_shape=shapes * 3, compiler_params=_params(0))(*ws, *gs, *ms, *vs)
    return res[:n], res[n:2 * n], res[2 * n:]


def _adamw(w, g, m, v, name):
    rows, cols = w.shape
    tr = 256 if rows % 256 == 0 else rows

    def body(w_ref, g_ref, m_ref, v_ref, d_ref, m2_ref, v2_ref):
        _adam_math(w_ref, g_ref, m_ref, v_ref, d_ref, m2_ref, v2_ref)

    spec = pl.BlockSpec((tr, cols), lambda i: (i, 0))
    shp = jax.ShapeDtypeStruct((rows, cols), F32)
    return pl.pallas_call(
        body, name=name, grid=(rows // tr,), in_specs=[spec] * 4, out_specs=[spec] * 3, out_shape=[shp] * 3,
        compiler_params=_params(),
    )(w, g, m, v)


SMALL = (
    ("attn_norm_g", (1, 1024)), ("conv_w", (1, 4, 512)), ("conv_b", (1, 512)),
    ("lru_wa_fwd", (1, 8, 64, 64)), ("lru_ba_fwd", (1, 512)), ("lru_wx_fwd", (1, 8, 64, 64)), ("lru_bx_fwd", (1, 512)),
    ("lru_lam_fwd", (1, 512)),
    ("lru_wa_bwd", (1, 8, 64, 64)), ("lru_ba_bwd", (1, 512)), ("lru_wx_bwd", (1, 8, 64, 64)), ("lru_bx_bwd", (1, 512)),
    ("lru_lam_bwd", (1, 512)),
    ("rel_bias", (32, 8)), ("norm_rnn_g", (1, 512)), ("norm_attn_g", (1, 512)), ("mlp_norm_g", (1, 1024)),
    ("final_norm_g", (1024,)),
)
PACK_ROW = 8 * 128


def _pack(parts):
    flat = jnp.concatenate([p.reshape(-1) for p in parts])
    pad = (-flat.shape[0]) % PACK_ROW
    return jnp.pad(flat, (0, pad)).reshape(-1, 128)


def _unpack(packed, shapes):
    flat = packed.reshape(-1)
    out, off = [], 0
    for shp in shapes:
        n = int(np.prod(shp))
        out.append(flat[off:off + n].reshape(shp))
        off += n
    return out


WEIGHT_ORDER = ("attn_norm_g", "w_in", "conv_w", "conv_b", "lru_wa_fwd", "lru_ba_fwd", "lru_wx_fwd", "lru_bx_fwd",
                "lru_lam_fwd", "lru_wa_bwd", "lru_ba_bwd", "lru_wx_bwd", "lru_bx_bwd", "lru_lam_bwd", "rel_bias",
                "norm_rnn_g", "norm_attn_g", "w_out", "mlp_norm_g", "w_up", "w_down", "final_norm_g")


def kernel(x, attn_norm_g, w_in, conv_w, conv_b, lru_wa_fwd, lru_ba_fwd, lru_wx_fwd, lru_bx_fwd, lru_lam_fwd, lru_wa_bwd, lru_ba_bwd, lru_wx_bwd, lru_bx_bwd, lru_lam_bwd, rel_bias, norm_rnn_g, norm_attn_g, w_out, mlp_norm_g, w_up, w_down, final_norm_g, loss_target, m_attn_norm_g, m_w_in, m_conv_w, m_conv_b, m_lru_wa_fwd, m_lru_ba_fwd, m_lru_wx_fwd, m_lru_bx_fwd, m_lru_lam_fwd, m_lru_wa_bwd, m_lru_ba_bwd, m_lru_wx_bwd, m_lru_bx_bwd, m_lru_lam_bwd, m_rel_bias, m_norm_rnn_g, m_norm_attn_g, m_w_out, m_mlp_norm_g, m_w_up, m_w_down, m_final_norm_g, v_attn_norm_g, v_w_in, v_conv_w, v_conv_b, v_lru_wa_fwd, v_lru_ba_fwd, v_lru_wx_fwd, v_lru_bx_fwd, v_lru_lam_fwd, v_lru_wa_bwd, v_lru_ba_bwd, v_lru_wx_bwd, v_lru_bx_bwd, v_lru_lam_bwd, v_rel_bias, v_norm_rnn_g, v_norm_attn_g, v_w_out, v_mlp_norm_g, v_w_up, v_w_down, v_final_norm_g):
    given = dict(locals())
    w = {n: given[n] for n in WEIGHT_ORDER}
    m = {n: given["m_" + n] for n in WEIGHT_ORDER}
    v = {n: given["v_" + n] for n in WEIGHT_ORDER}

    chip = lax.axis_index("x") * 2 + lax.axis_index("y")
    core = lax.axis_index("c")

    shards = {n: w[n][0].astype(BF16) for n in BIG}
    shards["conv_w"] = w["conv_w"][0]
    p = {n: (t[0] if t.ndim >= 3 else t) for n, t in w.items() if n not in BIG and n != "conv_w"}
    p["final_norm_g"] = w["final_norm_g"].reshape(1, D_MODEL)

    _, grad_x, small, gathered, big, reduced = _local_step(x[0], loss_target[0], p, shards)

    late = tuple(big)
    grads = [big[n] for n in late]
    others = _run_rider(_pair_exchange_rider(late, grads), "grad_pair_exchange")
    core_arr = core.reshape(1).astype(jnp.int32)
    parts = [_pair_add(core_arr, g, o, "grad_pair_add_" + n) for n, g, o in zip(late, grads, others)]
    landed = _run_rider(_chip_exchange_rider(parts), "grad_chip_exchange")
    halves = [_chip_sum(t, "grad_chip_sum_" + n) for n, t in zip(late, landed)]
    reduced.update(zip(late, _run_rider(_pair_share_rider(late, halves), "grad_pair_share")))

    early_small = [(n, shp) for n, shp in SMALL if n not in small]
    late_small = [(n, shp) for n, shp in SMALL if n in small]
    *early_g, loss = _unpack(_sum_devices(gathered), [shp for _, shp in early_small] + [(1,)])
    late_g = _unpack(_allreduce_small(_pack([small[n].reshape(shp) for n, shp in late_small])),
                     [shp for _, shp in late_small])
    g = dict(zip([n for n, _ in early_small + late_small], early_g + late_g))
    g["conv_w"] = lax.dynamic_slice_in_dim(g["conv_w"], chip * (D_RNN // N_SHARD), D_RNN // N_SHARD, axis=2)
    for n in BIG:
        g[n] = reduced[n][None]

    delta, new_m, new_v = {}, {}, {}
    for n in BIG:
        d2, m2, v2 = _adamw(w[n][0], reduced[n], m[n][0], v[n][0], "adamw_" + n)
        delta[n], new_m[n], new_v[n] = d2[None], m2[None], v2[None]
    names = [n for n, _ in SMALL]
    for dst, src in zip((delta, new_m, new_v), _adamw_many(*[[t[n] for n in names] for t in (w, g, m, v)])):
        dst.update(dict(zip(names, src)))

    return (loss.reshape(()), grad_x[None], *[g[n] for n in WEIGHT_ORDER], *[delta[n] for n in WEIGHT_ORDER],
            *[new_m[n] for n in WEIGHT_ORDER], *[new_v[n] for n in WEIGHT_ORDER])
```

```python
import functools
import math

import numpy as np
import jax
import jax.numpy as jnp
from jax import lax
from jax.experimental import pallas as pl
from jax.experimental.pallas import tpu as pltpu

F32 = jnp.float32
BF16 = jnp.bfloat16

D_MODEL = 1024
D_RNN = 512
D_ATTN = 512
N_HEADS = 8
HEAD_DIM = 64
N_RNN_BLOCKS = 8
RNN_BLOCK = 64
D_IN = 2 * D_RNN + 3 * D_ATTN
D_FF = 4 * D_MODEL
N_SHARD = 4
IN_BLK = D_IN // N_SHARD
OUT_BLK = D_MODEL // N_SHARD
FF_BLK = D_FF // N_SHARD
EPS = 1e-6
NEG_INF = -1e30
LRU_C = 8.0
DILATIONS = (1, 4, 16)
F32_LAYOUT = 4
HALF_WIN = 64
Q_BLK = 128
K_WIN = 256
N_BUCKETS = 32
MAX_DISTANCE = 1024
ATTN_SCALE = HEAD_DIM ** -0.5

ADAM_LR = 0.001
ADAM_B1 = 0.9
ADAM_B2 = 0.999
ADAM_EPS = 1e-08
ADAM_WD = 0.01
ADAM_STEP = 10

TS = 512
TS_RNN_BWD = 1024
TS_MLP = 256
TS_INPROJ_BWD = 512
ATTN_SUB = 32
ATTN_SUB_FWD = 32
TK_DW = 4096
SCAN_UNROLL = 8
SUB = 8
VMEM_LIMIT = 56 * 1024 * 1024
GELU_C0 = math.sqrt(2.0 / math.pi)
GELU_C1 = 0.044715

MESH = pl.DeviceIdType.MESH


def _params(n_grid=1):
    return pltpu.CompilerParams(vmem_limit_bytes=VMEM_LIMIT, dimension_semantics=("arbitrary",) * n_grid)


def _whole_vmem():
    return pl.BlockSpec(memory_space=pltpu.VMEM)


def _rows(width, tile=TS):
    return pl.BlockSpec((tile, width), lambda i: (i, 0))


def _sigmoid(z):
    return 0.5 * jnp.tanh(0.5 * z) + 0.5


def _log1p(u):
    w = 1.0 + u
    return jnp.where(w == 1.0, u, jnp.log(w) * (u / (w - 1.0)))


def _softplus(z):
    return jnp.maximum(z, 0.0) + _log1p(jnp.exp(-jnp.abs(z)))


def _gelu_parts(g):
    inner = GELU_C0 * (g + GELU_C1 * g * g * g)
    t = jnp.tanh(inner)
    val = 0.5 * g * (1.0 + t)
    dinner = GELU_C0 * (1.0 + 3.0 * GELU_C1 * g * g)
    grad = 0.5 * (1.0 + t) + 0.5 * g * (1.0 - t * t) * dinner
    return val, grad


def _rms(x):
    rstd = lax.rsqrt(jnp.mean(x * x, axis=-1, keepdims=True) + EPS)
    return rstd, x * rstd


def _rms_bwd(dy, g, xhat, rstd):
    dxh = dy * g
    dx = rstd * (dxh - xhat * jnp.mean(dxh * xhat, axis=-1, keepdims=True))
    dg = jnp.sum(dy * xhat, axis=0, keepdims=True)
    return dx, dg


def _dot(a, b):
    return jnp.dot(a, b, preferred_element_type=F32)


def _dot_nt(a, b):
    return lax.dot_general(a, b, (((1,), (1,)), ((), ())), preferred_element_type=F32)


def _dot_tn(a, b):
    return lax.dot_general(a, b, (((0,), (0,)), ((), ())), preferred_element_type=F32)


def _shifted(tile, prev8, next8, k):
    n = tile.shape[0]
    row = lax.broadcasted_iota(jnp.int32, tile.shape, 0)
    if k == 0:
        return tile
    if k < 0:
        r = pltpu.roll(tile, -k, 0)
        for j in range(-k):
            r = jnp.where(row == j, prev8[SUB + j + k:SUB + j + k + 1, :], r)
        return r
    r = pltpu.roll(tile, n - k, 0)
    for j in range(k):
        r = jnp.where(row == n - k + j, next8[j:j + 1, :], r)
    return r


def _to_lane_blocks(val, s_ref):
    for j in range(val.shape[1] // 128):
        s_ref[j] = val[:, j * 128:(j + 1) * 128]


def _from_lane_blocks(s_ref):
    return jnp.concatenate([s_ref[j] for j in range(s_ref.shape[0])], axis=-1)


def _class_rows(s_ref, r, dil):
    n = s_ref.shape[1] // dil
    return jnp.concatenate([s_ref[j, pl.ds(r, n, stride=dil), :] for j in range(s_ref.shape[0])], axis=-1)


def _split_classes(val, s_ref, out_ref, dil):
    _to_lane_blocks(val, s_ref)
    for r in range(dil):
        out_ref[r] = _class_rows(s_ref, r, dil).astype(out_ref.dtype)


def _merge_classes(in_ref, s_ref, dil, also_ref=None):
    n = s_ref.shape[1] // dil
    for r in range(dil):
        v = in_ref[r].astype(F32)
        if also_ref is not None:
            v = v + also_ref[r].astype(F32)
        for j in range(s_ref.shape[0]):
            s_ref[j, pl.ds(r, n, stride=dil), :] = v[:, j * 128:(j + 1) * 128]
    return _from_lane_blocks(s_ref)


def _class_spec(dil, tile=TS):
    return pl.BlockSpec((dil, tile // dil, 512), lambda i: (0, i, 0))


def _class_shape(S, dil, dtype):
    return jax.ShapeDtypeStruct((dil, S // dil, 512), dtype)


def _scan_tile(a_ref, b_ref, h_ref, carry_ref, reverse):
    n = a_ref.shape[0]
    width = a_ref.shape[1]
    groups = n // SUB
    row = lax.broadcasted_iota(jnp.int32, (SUB, width), 0)

    def group_scan(g):
        r0 = pl.multiple_of(g * SUB, SUB)
        a = a_ref[pl.ds(r0, SUB), :]
        b = b_ref[pl.ds(r0, SUB), :]
        for s in (1, 2, 4):
            if reverse:
                a_sh = pltpu.roll(a, SUB - s, 0)
                b_sh = pltpu.roll(b, SUB - s, 0)
                m = row < SUB - s
            else:
                a_sh = pltpu.roll(a, s, 0)
                b_sh = pltpu.roll(b, s, 0)
                m = row >= s
            b = jnp.where(m, a * b_sh + b, b)
            a = jnp.where(m, a * a_sh, a)
        return r0, a, b

    def step(i, carry):
        first = i * SCAN_UNROLL
        order = [(groups - 1 - (first + u)) if reverse else (first + u) for u in range(SCAN_UNROLL)]
        scans = [group_scan(g) for g in order]
        for r0, a, b in scans:
            h = b + a * carry
            h_ref[pl.ds(r0, SUB), :] = h
            edge = h[0:1, :] if reverse else h[SUB - 1:SUB, :]
            carry = jnp.broadcast_to(edge, (SUB, width))
        return carry

    carry_ref[...] = lax.fori_loop(0, groups // SCAN_UNROLL, step, carry_ref[...])


def _conv_fwd(xr, prev8, next8, cw, cb):
    y = cb + _shifted(xr, prev8, next8, -2) * cw[0:1, :]
    y = y + _shifted(xr, prev8, next8, -1) * cw[1:2, :]
    y = y + xr * cw[2:3, :]
    y = y + _shifted(xr, prev8, next8, 1) * cw[3:4, :]
    return y


def _lru_gates(xc, wa_ref, ba, wx_ref, bx, lam):
    xcb = xc.astype(BF16)
    r = _sigmoid(_dot(xcb, wa_ref[...]) + ba)
    i = _sigmoid(_dot(xcb, wx_ref[...]) + bx)
    cl = -LRU_C * _softplus(-lam)
    la = cl * r
    a = jnp.exp(la)
    m2 = -jnp.tanh(la) * (a * a + 1.0)
    inv = jnp.where(m2 > 0.0, lax.rsqrt(m2), 0.0)
    mult = m2 * inv
    return xcb, r, i, cl, a, mult, inv


def _inproj_fwd(x, g1, w_in, rider=None):
    S = x.shape[0]

    def body(x_ref, g_ref, w_ref, xr_ref, gate_ref, *rest):
        qkv_refs, s_ref, s4_ref, w_full = rest[:9], rest[9], rest[10], rest[11]

        @pl.when(pl.program_id(0) == 0)
        def _():
            for j in range(N_SHARD):
                w_full[:, j * IN_BLK:(j + 1) * IN_BLK] = w_ref[j]

        _, xh = _rms(x_ref[...])
        h = (xh * g_ref[...]).astype(BF16)
        proj = _dot(h, w_full[...])
        xr_ref[...] = proj[:, 0:512]
        gate_ref[...] = proj[:, 512:1024]
        for t in range(3):
            val = proj[:, 1024 + 512 * t:1536 + 512 * t]
            d1_ref, d4_ref, d16_ref = qkv_refs[3 * t:3 * t + 3]
            d1_ref[0] = val.astype(BF16)
            _to_lane_blocks(val, s_ref)
            for r4 in range(4):
                c4 = _class_rows(s_ref, r4, 4)
                d4_ref[r4] = c4.astype(BF16)
                _to_lane_blocks(c4, s4_ref.at[r4])
            for r4 in range(4):
                for m in range(4):
                    d16_ref[r4 + 4 * m] = _class_rows(s4_ref.at[r4], m, 4).astype(BF16)

    f = jax.ShapeDtypeStruct((S, 512), F32)
    return _call(
        body, "inproj_fwd", (S // TS,),
        [_rows(D_MODEL), _whole_vmem(), _whole_vmem()],
        [_rows(512)] * 2 + [_class_spec(d) for d in DILATIONS] * 3,
        [f, f] + [_class_shape(S, d, BF16) for d in DILATIONS] * 3,
        [pltpu.VMEM((4, TS, 128), F32), pltpu.VMEM((4, 4, TS // 4, 128), F32), pltpu.VMEM((D_MODEL, D_IN), BF16)],
        (x, g1, w_in), rider)


def _halo_specs(S, order, tile=TS):
    per = tile // SUB
    last = S // SUB - 1
    return [
        pl.BlockSpec((tile, 512), lambda i: (order(i), 0)),
        pl.BlockSpec((SUB, 512), lambda i: (jnp.maximum(order(i) * per - 1, 0), 0)),
        pl.BlockSpec((SUB, 512), lambda i: (jnp.minimum((order(i) + 1) * per, last), 0)),
    ]


def _rnn_fwd(xr, conv_w, conv_b, wa, ba, wx, bx, lam, reverse, rider=None, xc=None):
    S = xr.shape[0]
    nt = S // TS
    order = (lambda i: nt - 1 - i) if reverse else (lambda i: i)
    with_conv = xc is None
    n_x = 5 if with_conv else 1
    tile = pl.BlockSpec((TS, 512), lambda i: (order(i), 0))

    def body(*refs):
        wa_ref, ba_ref, wx_ref, bx_ref, lam_ref, h_ref = refs[n_x:n_x + 6]
        a_s, b_s, carry = refs[-3:]
        i = pl.program_id(0)
        t = order(i)

        @pl.when(i == 0)
        def _():
            carry[...] = jnp.zeros_like(carry)

        if with_conv:
            x_ref, xp_ref, xn_ref, cw_ref, cb_ref = refs[:5]
            prev8 = jnp.where(t > 0, xp_ref[...], 0.0)
            next8 = jnp.where(t < nt - 1, xn_ref[...], 0.0)
            xcv = _conv_fwd(x_ref[...], prev8, next8, cw_ref[...], cb_ref[...])
            refs[n_x + 6][...] = xcv
        else:
            xcv = refs[0][...]
        _, _, gi, _, a, mult, _ = _lru_gates(xcv, wa_ref, ba_ref[...], wx_ref, bx_ref[...], lam_ref[...])
        a_s[...] = a
        b_s[...] = mult * (gi * xcv)
        _scan_tile(a_s, b_s, h_ref, carry, reverse)

    f512 = jax.ShapeDtypeStruct((S, 512), F32)
    return _call(
        body, "rnn_fwd_rev" if reverse else "rnn_fwd_fwd", (nt,),
        (_halo_specs(S, order) + [_whole_vmem()] * 2 if with_conv else [tile]) + [_whole_vmem()] * 5,
        [tile, tile] if with_conv else [tile], [f512, f512] if with_conv else [f512],
        [pltpu.VMEM((TS, 512), F32), pltpu.VMEM((TS, 512), F32), pltpu.VMEM((SUB, 512), F32)],
        ((xr, xr, xr, conv_w, conv_b) if with_conv else (xc,)) + (wa, ba, wx, bx, lam), rider)


def _mix_fwd(o3, l3, hf, hb, gate, x, g_rnn, g_attn, w_out):
    S = x.shape[0]

    nt = S // TS
    ring = 3
    dils = (1, F32_LAYOUT, F32_LAYOUT) * 2

    def body(*refs):
        src = refs[0:6]
        hf_ref, hb_ref, gate_ref, x_ref, gr_ref, ga_ref, w_ref, x1_ref, mix_ref, ya1, ya2, ls1, ls2, s_ref = refs[6:20]
        bufs, sems = refs[20:26], refs[26]
        i = pl.program_id(0)

        def fetch(j, t):
            rows = TS // dils[j]
            return pltpu.make_async_copy(src[j].at[:, pl.ds(t * rows, rows), :], bufs[j].at[t % ring],
                                         sems.at[j, t % ring])

        @pl.when(i == 0)
        def _():
            for t in range(min(ring - 1, nt)):
                for j in range(6):
                    fetch(j, t).start()

        @pl.when(i + ring - 1 < nt)
        def _():
            for j in range(6):
                fetch(j, i + ring - 1).start()

        for j in range(6):
            fetch(j, i).wait()
        o1, o2, o3_, l1, l2, l3_ = [bufs[j].at[i % ring] for j in range(6)]
        la, lb, lc = l1[0], _merge_classes(l2, s_ref, F32_LAYOUT), _merge_classes(l3_, s_ref, F32_LAYOUT)
        m = jnp.maximum(jnp.maximum(la, lb), lc)
        ea, eb, ec = jnp.exp(la - m), jnp.exp(lb - m), jnp.exp(lc - m)
        den = ea + eb + ec
        lse = m + jnp.log(den)
        ya = (ea * o1[0] + eb * _merge_classes(o2, s_ref, F32_LAYOUT) + ec * _merge_classes(o3_, s_ref, F32_LAYOUT)) / den
        ya1[0] = ya
        ls1[0] = lse
        _split_classes(ya, s_ref, ya2, F32_LAYOUT)
        _split_classes(lse, s_ref, ls2, F32_LAYOUT)
        gg, _ = _gelu_parts(gate_ref[...])
        yr = (hf_ref[...] + hb_ref[...]) * gg
        _, xh_r = _rms(yr)
        _, xh_a = _rms(ya)
        mix = jnp.concatenate([xh_r * gr_ref[...], xh_a * ga_ref[...]], axis=-1).astype(BF16)
        mix_ref[...] = mix
        acc = x_ref[...]
        for j in range(N_SHARD):
            acc = acc + _dot(mix[:, j * OUT_BLK:(j + 1) * OUT_BLK], w_ref[j])
        x1_ref[...] = acc

    one, four = _class_spec(1), _class_spec(F32_LAYOUT)
    return pl.pallas_call(
        body, grid=(nt,), name="mix_fwd",
        in_specs=[ANY] * 6 + [_rows(512)] * 3 + [_rows(D_MODEL)] + [_whole_vmem()] * 3,
        out_specs=[_rows(D_MODEL), _rows(D_MODEL)] + [one, four] * 2,
        out_shape=[jax.ShapeDtypeStruct((S, D_MODEL), F32), jax.ShapeDtypeStruct((S, D_MODEL), BF16)]
        + [_class_shape(S, 1, F32), _class_shape(S, F32_LAYOUT, F32)] * 2,
        scratch_shapes=[pltpu.VMEM((4, TS, 128), F32)]
        + [pltpu.VMEM((ring, d, TS // d, 512), F32) for d in dils] + [pltpu.SemaphoreType.DMA((6, ring))],
        compiler_params=_params(),
    )(*o3, *l3, hf, hb, gate, x, g_rnn, g_attn, w_out)


def _mlp_fwd_bwd(x1, target, g_mlp, g_fin, w_up, w_down):
    S = x1.shape[0]
    tm = TS_MLP

    def body(x1_ref, t_ref, gm_ref, gf_ref, wu_ref, wd_ref,
             dx1_ref, h2_ref, a2_ref, du_ref, dx2_ref, loss_ref, dgf_ref, dgm_ref, relu_s):
        @pl.when(pl.program_id(0) == 0)
        def _():
            loss_ref[...] = jnp.zeros_like(loss_ref)
            dgf_ref[...] = jnp.zeros_like(dgf_ref)
            dgm_ref[...] = jnp.zeros_like(dgm_ref)

        x1v = x1_ref[...]
        rstd1, xh1 = _rms(x1v)
        h2 = (xh1 * gm_ref[...]).astype(BF16)
        h2_ref[...] = h2
        x2 = x1v
        for j in range(N_SHARD):
            r = jnp.maximum(_dot(h2, wu_ref[j]), 0.0)
            relu_s[j] = r
            a2 = (r * r).astype(BF16)
            a2_ref[:, j * FF_BLK:(j + 1) * FF_BLK] = a2
            x2 = x2 + _dot(a2, wd_ref[j])
        rstd2, xh2 = _rms(x2)
        err = xh2 * gf_ref[...] - t_ref[...]
        loss_ref[...] += jnp.sum(err * err, axis=0, keepdims=True)
        dy = err * (1.0 / D_MODEL)
        dx2, dgf = _rms_bwd(dy, gf_ref[...], xh2, rstd2)
        dgf_ref[...] += dgf
        dx2b = dx2.astype(BF16)
        dx2_ref[...] = dx2b
        dh2 = jnp.zeros((tm, D_MODEL), F32)
        for j in range(N_SHARD):
            du = (_dot_nt(dx2b, wd_ref[j]) * (2.0 * relu_s[j])).astype(BF16)
            du_ref[:, j * FF_BLK:(j + 1) * FF_BLK] = du
            dh2 = dh2 + _dot_nt(du, wu_ref[j])
        dx1n, dgm = _rms_bwd(dh2, gm_ref[...], xh1, rstd1)
        dgm_ref[...] += dgm
        dx1_ref[...] = dx2 + dx1n

    vec = jax.ShapeDtypeStruct((1, D_MODEL), F32)
    return pl.pallas_call(
        body, grid=(S // tm,), name="mlp_fwd_bwd",
        in_specs=[_rows(D_MODEL, tm), _rows(D_MODEL, tm)] + [_whole_vmem()] * 4,
        out_specs=[_rows(D_MODEL, tm), _rows(D_MODEL, tm), _rows(D_FF, tm), _rows(D_FF, tm), _rows(D_MODEL, tm)]
        + [_whole_vmem()] * 3,
        out_shape=[jax.ShapeDtypeStruct((S, D_MODEL), F32), jax.ShapeDtypeStruct((S, D_MODEL), BF16),
                   jax.ShapeDtypeStruct((S, D_FF), BF16), jax.ShapeDtypeStruct((S, D_FF), BF16),
                   jax.ShapeDtypeStruct((S, D_MODEL), BF16), vec, vec, vec],
        scratch_shapes=[pltpu.VMEM((N_SHARD, tm, FF_BLK), F32)],
        compiler_params=_params(),
    )(x1, target, g_mlp, g_fin, w_up, w_down)


def _mix_bwd(dx1, w_out, mixb, ya, hf, hb, gate, g_rnn, g_attn):
    S = dx1.shape[0]

    def body(dx1_ref, w_ref, mix_ref, ya_ref, hf_ref, hb_ref, gate_ref, gr_ref, ga_ref,
             dhs_ref, dgate_ref, dya1, dya2, dw_ref, dgr_ref, dga_ref, s_ref):
        @pl.when(pl.program_id(0) == 0)
        def _():
            dw_ref[...] = jnp.zeros_like(dw_ref)
            dgr_ref[...] = jnp.zeros_like(dgr_ref)
            dga_ref[...] = jnp.zeros_like(dga_ref)

        dx1b = dx1_ref[...].astype(BF16)
        mix = mix_ref[...]
        for j in range(N_SHARD):
            dw_ref[j] += _dot_tn(mix[:, j * OUT_BLK:(j + 1) * OUT_BLK], dx1b)
        dmix = jnp.concatenate([_dot_nt(dx1b, w_ref[j]) for j in range(N_SHARD)], axis=-1)
        gg, dgg = _gelu_parts(gate_ref[...])
        hs = hf_ref[...] + hb_ref[...]
        rstd_r, xh_r = _rms(hs * gg)
        dyr, dgr = _rms_bwd(dmix[:, 0:D_RNN], gr_ref[...], xh_r, rstd_r)
        dgr_ref[...] += dgr
        rstd_a, xh_a = _rms(ya_ref[0])
        dya, dga = _rms_bwd(dmix[:, D_RNN:], ga_ref[...], xh_a, rstd_a)
        dga_ref[...] += dga
        dya1[0] = dya
        _split_classes(dya, s_ref, dya2, F32_LAYOUT)
        dhs_ref[...] = dyr * gg
        dgate_ref[...] = dyr * hs * dgg

    f512 = jax.ShapeDtypeStruct((S, 512), F32)
    vec = jax.ShapeDtypeStruct((1, 512), F32)
    return pl.pallas_call(
        body, grid=(S // TS,), name="mix_bwd",
        in_specs=[_rows(D_MODEL), _whole_vmem(), _rows(D_MODEL), _class_spec(1)] + [_rows(512)] * 3 + [_whole_vmem()] * 2,
        out_specs=[_rows(512)] * 2 + [_class_spec(1), _class_spec(F32_LAYOUT)] + [_whole_vmem()] * 3,
        out_shape=[f512, f512, _class_shape(S, 1, F32), _class_shape(S, F32_LAYOUT, F32),
                   jax.ShapeDtypeStruct((N_SHARD, OUT_BLK, D_MODEL), F32), vec, vec],
        scratch_shapes=[pltpu.VMEM((4, TS, 128), F32)],
        compiler_params=_params(),
    )(dx1, w_out, mixb, ya, hf, hb, gate, g_rnn, g_attn)


def _rnn_bwd(xc, h, dhs, wa, ba, wx, bx, lam, reverse, rider=None):
    S = xc.shape[0]
    nt = S // TS_RNN_BWD
    order = (lambda i: i) if reverse else (lambda i: nt - 1 - i)
    per = TS_RNN_BWD // SUB
    last = S // SUB - 1
    if reverse:
        h_halo = pl.BlockSpec((SUB, 512), lambda i: (jnp.minimum((order(i) + 1) * per, last), 0))
    else:
        h_halo = pl.BlockSpec((SUB, 512), lambda i: (jnp.maximum(order(i) * per - 1, 0), 0))
    tile = pl.BlockSpec((TS_RNN_BWD, 512), lambda i: (order(i), 0))

    def body(xc_ref, h_ref, hh_ref, dh_ref, wa_ref, ba_ref, wx_ref, bx_ref, lam_ref,
             dxc_ref, dwa_ref, dwx_ref, dvec_ref, a_s, g_s, carry, edge):
        i = pl.program_id(0)
        t = order(i)

        @pl.when(i == 0)
        def _():
            carry[...] = jnp.zeros_like(carry)
            edge[...] = jnp.zeros_like(edge)
            dwa_ref[...] = jnp.zeros_like(dwa_ref)
            dwx_ref[...] = jnp.zeros_like(dwx_ref)
            dvec_ref[...] = jnp.zeros_like(dvec_ref)

        xc = xc_ref[...]
        xcb, r, gi, cl, a, mult, inv_mult = _lru_gates(xc, wa_ref, ba_ref[...], wx_ref, bx_ref[...], lam_ref[...])
        hv = h_ref[...]
        if reverse:
            a_s[...] = _shifted(a, edge[...], None, -1)
            edge[...] = a[TS_RNN_BWD - SUB:TS_RNN_BWD, :]
            hh = jnp.where(t < nt - 1, hh_ref[...], 0.0)
            h_prev = _shifted(hv, None, hh, 1)
        else:
            a_s[...] = _shifted(a, None, edge[...], 1)
            edge[...] = a[0:SUB, :]
            hh = jnp.where(t > 0, hh_ref[...], 0.0)
            h_prev = _shifted(hv, hh, None, -1)
        _scan_tile(a_s, dh_ref, g_s, carry, not reverse)
        g = g_s[...]
        da = g * h_prev
        gm = g * mult
        d_i = gm * xc
        dmult = g * gi * xc
        dla = da * a - dmult * (a * a) * inv_mult
        d_r = dla * cl
        dpre_r = d_r * r * (1.0 - r)
        dpre_i = d_i * gi * (1.0 - gi)
        dprb = dpre_r.astype(BF16)
        dpib = dpre_i.astype(BF16)
        dwa_ref[...] += _dot_tn(xcb, dprb)
        dwx_ref[...] += _dot_tn(xcb, dpib)
        dvec_ref[0:1, :] += jnp.sum(dpre_r, axis=0, keepdims=True)
        dvec_ref[1:2, :] += jnp.sum(dpre_i, axis=0, keepdims=True)
        dvec_ref[2:3, :] += jnp.sum(dla * r, axis=0, keepdims=True)
        dvec_ref[3:4, :] = dvec_ref[2:3, :] * (LRU_C * _sigmoid(-lam_ref[...]))
        dxc_ref[...] = gm * gi + _dot_nt(dprb, wa_ref[...]) + _dot_nt(dpib, wx_ref[...])

    sq = jax.ShapeDtypeStruct((D_RNN, D_RNN), F32)
    return _call(
        body, "rnn_bwd_rev" if reverse else "rnn_bwd_fwd", (nt,),
        [tile, tile, h_halo, tile] + [_whole_vmem()] * 5,
        [tile, _whole_vmem(), _whole_vmem(), _whole_vmem()],
        [jax.ShapeDtypeStruct((S, 512), F32), sq, sq, jax.ShapeDtypeStruct((SUB, 512), F32)],
        [pltpu.VMEM((TS_RNN_BWD, 512), F32), pltpu.VMEM((TS_RNN_BWD, 512), F32), pltpu.VMEM((SUB, 512), F32),
         pltpu.VMEM((SUB, 512), F32)],
        (xc, h, h, dhs, wa, ba, wx, bx, lam), rider)


def _inproj_bwd(x, dx1, xr, dxc_f, dxc_b, dgate, dq3, dk3, dv3, g1, conv_w, w_in, rider=None):
    S = x.shape[0]
    tb = TS_INPROJ_BWD
    nt = S // tb
    ident = lambda i: i

    def body(x_ref, dx1_ref, xr_ref, xrp_ref, xrn_ref, cf_ref, cfp_ref, cfn_ref, cb_ref, cbp_ref, cbn_ref, dgate_ref,
             dq1, dq2, dq3_, dk1, dk2, dk3_, dv1, dv2, dv3_, g_ref, cw_ref, w_ref,
             dx_ref, dw_ref, dg_ref, dcw_ref, s_ref, w_full, dw_full, sems):
        i = pl.program_id(0)

        def blocks(full, blocked, k0):
            return [(full.at[:, j * IN_BLK:(j + 1) * IN_BLK], blocked.at[j], sems.at[k0 + j]) for j in range(N_SHARD)]

        @pl.when(i == 0)
        def _():
            copies = [pltpu.make_async_copy(src, dst, sem) for dst, src, sem in blocks(w_full, w_ref, 0)]
            for cp in copies:
                cp.start()
            for cp in copies:
                cp.wait()
            dw_full[...] = jnp.zeros_like(dw_full)
            dg_ref[...] = jnp.zeros_like(dg_ref)
            dcw_ref[...] = jnp.zeros_like(dcw_ref)

        first, last = i > 0, i < nt - 1
        dxc = cf_ref[...] + cb_ref[...]
        dxc_p = jnp.where(first, cfp_ref[...] + cbp_ref[...], 0.0)
        dxc_n = jnp.where(last, cfn_ref[...] + cbn_ref[...], 0.0)
        cw = cw_ref[...]
        dxr = (_shifted(dxc, dxc_p, dxc_n, 2) * cw[0:1, :] + _shifted(dxc, dxc_p, dxc_n, 1) * cw[1:2, :]
               + dxc * cw[2:3, :] + _shifted(dxc, dxc_p, dxc_n, -1) * cw[3:4, :])
        xrv = xr_ref[...]
        xr_p = jnp.where(first, xrp_ref[...], 0.0)
        xr_n = jnp.where(last, xrn_ref[...], 0.0)
        for k, off in enumerate((-2, -1, 0, 1)):
            dcw_ref[k:k + 1, :] += jnp.sum(dxc * _shifted(xrv, xr_p, xr_n, off), axis=0, keepdims=True)
        dcw_ref[4:5, :] += jnp.sum(dxc, axis=0, keepdims=True)

        def total(a, b, c_):
            return a[0].astype(F32) + _merge_classes(b, s_ref, F32_LAYOUT, c_)

        dproj = jnp.concatenate(
            [dxr, dgate_ref[...], total(dq1, dq2, dq3_), total(dk1, dk2, dk3_), total(dv1, dv2, dv3_)],
            axis=-1).astype(BF16)
        xv = x_ref[...]
        rstd, xh = _rms(xv)
        hb = (xh * g_ref[...]).astype(BF16)
        dh = _dot_nt(dproj, w_full[...])
        dw_full[...] += _dot_tn(hb, dproj)
        dxn, dg = _rms_bwd(dh, g_ref[...], xh, rstd)
        dg_ref[...] += dg
        dx_ref[...] = dx1_ref[...] + dxn

        @pl.when(i == nt - 1)
        def _():
            copies = [pltpu.make_async_copy(src, dst, sem) for src, dst, sem in blocks(dw_full, dw_ref, N_SHARD)]
            for cp in copies:
                cp.start()
            for cp in copies:
                cp.wait()

    halo = _halo_specs(S, ident, tb)
    return _call(
        body, "inproj_bwd", (nt,),
        [_rows(D_MODEL, tb), _rows(D_MODEL, tb)] + halo * 3 + [_rows(512, tb)]
        + [_class_spec(1, tb), _class_spec(F32_LAYOUT, tb), _class_spec(F32_LAYOUT, tb)] * 3 + [_whole_vmem()] * 2 + [ANY],
        [_rows(D_MODEL, tb), ANY, _whole_vmem(), _whole_vmem()],
        [jax.ShapeDtypeStruct((S, D_MODEL), F32), jax.ShapeDtypeStruct((N_SHARD, D_MODEL, IN_BLK), F32),
         jax.ShapeDtypeStruct((1, D_MODEL), F32), jax.ShapeDtypeStruct((SUB, 512), F32)],
        [pltpu.VMEM((4, tb, 128), F32), pltpu.VMEM((D_MODEL, D_IN), BF16), pltpu.VMEM((D_MODEL, D_IN), F32),
         pltpu.SemaphoreType.DMA((2 * N_SHARD,))],
        (x, dx1, xr, xr, xr, dxc_f, dxc_f, dxc_f, dxc_b, dxc_b, dxc_b, dgate, *dq3, *dk3, *dv3, g1, conv_w, w_in), rider)


def _dw_matmul(a, b, a_cols, b_cols, name):
    S = a.shape[0]
    tk = min(S, TK_DW)
    a_shared = a.shape[1] == a_cols
    b_shared = b.shape[1] == b_cols

    def body(a_ref, b_ref, o_ref):
        @pl.when(pl.program_id(1) == 0)
        def _():
            o_ref[...] = jnp.zeros_like(o_ref)
        o_ref[0] += _dot_tn(a_ref[...], b_ref[...])

    return pl.pallas_call(
        body, grid=(N_SHARD, S // tk), name=name,
        in_specs=[pl.BlockSpec((tk, a_cols), (lambda j, k: (k, 0)) if a_shared else (lambda j, k: (k, j))),
                  pl.BlockSpec((tk, b_cols), (lambda j, k: (k, 0)) if b_shared else (lambda j, k: (k, j)))],
        out_specs=pl.BlockSpec((1, a_cols, b_cols), lambda j, k: (j, 0, 0)),
        out_shape=jax.ShapeDtypeStruct((N_SHARD, a_cols, b_cols), F32),
        compiler_params=_params(2),
    )(a, b)


def _t5_bucket_np(rel):
    nb = N_BUCKETS // 2
    max_exact = nb // 2
    ret = np.where(rel > 0, nb, 0)
    n = np.abs(rel)
    nf = np.maximum(n, 1).astype(np.float32)
    large = max_exact + (np.log(nf / np.float32(max_exact)) / np.float32(math.log(MAX_DISTANCE / max_exact))
                         * np.float32(nb - max_exact)).astype(np.int32)
    large = np.minimum(large, nb - 1)
    return ret + np.where(n < max_exact, n, large)


_VARIANT_OFFSETS = (-HALF_WIN,) * 3


def _band_index():
    kk = np.arange(K_WIN)[None, :]
    ql = np.arange(Q_BLK)[:, None]
    rel = np.stack([kk - ql + off for off in _VARIANT_OFFSETS])
    band = np.abs(rel) <= HALF_WIN
    inside = np.stack([np.broadcast_to(kk >= HALF_WIN, band[0].shape), np.ones_like(band[0]),
                       np.broadcast_to(kk < K_WIN - HALF_WIN, band[0].shape)])
    return rel, band & inside


def _bucket_tables(dil):
    rel, valid = _band_index()
    bucket = _t5_bucket_np(np.clip(rel, -HALF_WIN, HALF_WIN) * dil)
    return np.where(valid, bucket, -1).astype(np.int32)


def _bias_mats(rel_bias, rider=None):
    tables = [_bucket_tables(d) for d in DILATIONS]
    used = [sorted(set(t[t >= 0].tolist())) for t in tables]

    def one_pattern(rb_ref, t_ref, o_ref, buckets):
        bk = t_ref[1]
        for h in range(N_HEADS):
            acc = jnp.full((Q_BLK, K_WIN), NEG_INF, F32)
            for b in buckets:
                acc = jnp.where(bk == b, rb_ref[b, h], acc)
            o_ref[1, h] = acc
            for var in (0, 2):
                o_ref[var, h] = jnp.where(t_ref[var] >= 0, acc, NEG_INF)

    def body(rb_ref, t1, t2, t3, o1, o2, o3):
        for i, (t_ref, o_ref) in enumerate(((t1, o1), (t2, o2), (t3, o3))):
            pl.when(pl.program_id(0) == i)(functools.partial(one_pattern, rb_ref, t_ref, o_ref, used[i]))

    shp = jax.ShapeDtypeStruct((3, N_HEADS, Q_BLK, K_WIN), F32)
    return _call(
        body, "bias_tables", (len(DILATIONS),), [pl.BlockSpec(memory_space=pltpu.SMEM)] + [_whole_vmem()] * 3,
        [_whole_vmem()] * 3, [shp] * 3, [], (rel_bias, *[jnp.asarray(t) for t in tables]), rider)


def _variant(qb, nq):
    return jnp.where(qb == 0, 0, jnp.where(qb == nq - 1, 2, 1))


def _win_start(qb):
    return pl.multiple_of(qb * Q_BLK, Q_BLK)


def _fill_padded(src_ref, pad_ref):
    L = src_ref.shape[0]
    edge = jnp.zeros((HALF_WIN, 128), pad_ref.dtype)
    pad_ref[0:HALF_WIN, :] = edge
    pad_ref[HALF_WIN:HALF_WIN + L, :] = src_ref[...]
    pad_ref[HALF_WIN + L:2 * HALF_WIN + L, :] = edge


INNER = {1: 1, 4: 1, 16: 4}


def _attn_layout(dil, L, blocks=ATTN_SUB, outer=None):
    inner = INNER[dil]
    n_outer = dil // inner
    nsub = min(blocks // inner, L // Q_BLK)
    qt = nsub * Q_BLK
    grid = (4, n_outer, L // qt)
    qspec = pl.BlockSpec((inner, None, qt, 128), lambda hp, r, s: (0, r, s, hp))
    kspec = pl.BlockSpec((inner, None, L, 128), lambda hp, r, s: (0, r, 0, hp))
    bspec = pl.BlockSpec((3, 2, Q_BLK, K_WIN), lambda hp, r, s: (0, hp, 0, 0))
    kfspec = pl.BlockSpec((None, inner * L, 128), lambda hp, r, s: (r, 0, hp))
    qfspec = kfspec if inner > 1 else pl.BlockSpec((None, qt, 128), lambda hp, r, s: (r, s, hp))
    fshape = jax.ShapeDtypeStruct((n_outer, inner * L, D_ATTN), F32)
    view = lambda t: t.reshape(inner, n_outer, L, D_ATTN)

    def qrows(m, sub):
        if inner == 1:
            return (slice(sub * Q_BLK, (sub + 1) * Q_BLK), slice(None))
        first = (pl.program_id(2) * nsub + sub) * Q_BLK
        return (pl.ds(m + inner * first, Q_BLK, stride=inner), slice(None))

    def krows(m):
        if inner == 1:
            return (slice(None), slice(None))
        return (pl.ds(m, L, stride=inner), slice(None))

    if outer is not None:
        grid = (4, n_outer // outer, L // qt)
        qspec = pl.BlockSpec((inner, outer, qt, 128), lambda hp, r, s: (0, r, s, hp))
        kspec = pl.BlockSpec((inner, outer, L, 128), lambda hp, r, s: (0, r, 0, hp))
        kfspec = pl.BlockSpec((outer, inner * L, 128), lambda hp, r, s: (r, 0, hp))
        qfspec = kfspec if inner > 1 else pl.BlockSpec((outer, qt, 128), lambda hp, r, s: (r, s, hp))
    return inner, nsub, grid, qspec, kspec, bspec, qfspec, kfspec, fshape, view, qrows, krows


def _head_masks():
    lane = lax.broadcasted_iota(jnp.int32, (Q_BLK, 128), 1)
    return lane < HEAD_DIM


def _attn_fwd(q, k, v, bias):
    dil, L, _ = q.shape
    nq = L // Q_BLK
    n_cls = dil // INNER[dil]
    inner, nsub, grid, qspec, kspec, bspec, qfspec, kfspec, fshape, view, qrows, krows = _attn_layout(
        dil, L, ATTN_SUB_FWD, n_cls)

    def body(q_ref, k_ref, v_ref, b_ref, o_ref, l_ref, kp, vp):
        step = pl.program_id(2)

        @pl.when(step == 0)
        def _():
            for c in range(n_cls):
                for m in range(inner):
                    _fill_padded(k_ref.at[m, c], kp.at[c * inner + m])
                    _fill_padded(v_ref.at[m, c], vp.at[c * inner + m])

        h0 = _head_masks()
        for c, m, sub in [(c, m, sub) for c in range(n_cls) for m in range(inner) for sub in range(nsub)]:
            qb = step * nsub + sub
            st = _win_start(qb)
            var = _variant(qb, nq)
            kw = kp[c * inner + m, pl.ds(st, K_WIN), :]
            vw = vp[c * inner + m, pl.ds(st, K_WIN), :]
            qs = q_ref[m, c, sub * Q_BLK:(sub + 1) * Q_BLK, :] * ATTN_SCALE
            zq = jnp.zeros_like(qs)
            q2 = jnp.concatenate([jnp.where(h0, qs, zq), jnp.where(h0, zq, qs)], axis=0)
            s = _dot_nt(q2, kw) + b_ref[var].reshape(2 * Q_BLK, K_WIN)
            top = jnp.max(s, axis=-1, keepdims=True)
            p = jnp.exp(s - top)
            l = jnp.sum(p, axis=-1, keepdims=True)
            out = _dot(p.astype(BF16), vw) / l
            lse = top + jnp.log(l)
            o_ref.at[c][qrows(m, sub)] = jnp.where(h0, out[0:Q_BLK], out[Q_BLK:2 * Q_BLK])
            l_ref.at[c][qrows(m, sub)] = jnp.where(h0, lse[0:Q_BLK], lse[Q_BLK:2 * Q_BLK])

    return pl.pallas_call(
        body, grid=grid, name=f"attn_fwd_d{dil}",
        in_specs=[qspec, kspec, kspec, bspec], out_specs=[qfspec, qfspec], out_shape=[fshape, fshape],
        scratch_shapes=[pltpu.VMEM((n_cls * inner, L + 2 * HALF_WIN, 128), BF16)] * 2,
        compiler_params=_params(3),
    )(view(q), view(k), view(v), bias)


def _attn_bwd(q, k, v, bias, do, o, lse, rider=None):
    dil, L, _ = q.shape
    nq = L // Q_BLK
    inner, nsub, grid, qspec, kspec, bspec, qfspec, kfspec, fshape, view, qrows, krows = _attn_layout(dil, L)
    nstep = grid[2]

    def body(q_ref, k_ref, v_ref, b_ref, do_ref, o_ref, l_ref, dq_ref, dk_ref, dv_ref, db_ref, db_s,
             kp, vp, dkp, dvp, carry):
        hp, step = pl.program_id(0), pl.program_id(2)
        first = (hp == 0) & (pl.program_id(1) == 0) & (step == 0)
        last = (hp == grid[0] - 1) & (pl.program_id(1) == grid[1] - 1) & (step == nstep - 1)

        @pl.when(first)
        def _():
            db_s[...] = jnp.zeros_like(db_s)

        @pl.when(step == 0)
        def _():
            for m in range(inner):
                _fill_padded(k_ref.at[m], kp.at[m])
                _fill_padded(v_ref.at[m], vp.at[m])
            carry[...] = jnp.zeros_like(carry)

        h0 = _head_masks()
        for m, sub in [(m, sub) for m in range(inner) for sub in range(nsub)]:
            if sub == 0:
                carry_k, carry_v = carry[m, 0], carry[m, 1]
            qb = step * nsub + sub
            st = _win_start(qb)
            var = _variant(qb, nq)
            kw = kp[m, pl.ds(st, K_WIN), :]
            vw = vp[m, pl.ds(st, K_WIN), :]
            qs = q_ref[m, sub * Q_BLK:(sub + 1) * Q_BLK, :] * ATTN_SCALE
            dof = do_ref[qrows(m, sub)]
            dob = dof.astype(BF16)
            prod = dof * o_ref[qrows(m, sub)]
            lsev = l_ref[qrows(m, sub)]
            zq, zd = jnp.zeros_like(qs), jnp.zeros_like(dob)
            q2 = jnp.concatenate([jnp.where(h0, qs, zq), jnp.where(h0, zq, qs)], axis=0)
            do2 = jnp.concatenate([jnp.where(h0, dob, zd), jnp.where(h0, zd, dob)], axis=0)
            lse2 = jnp.concatenate([lsev[:, 0:1], lsev[:, HEAD_DIM:HEAD_DIM + 1]], axis=0)
            dd2 = jnp.concatenate([jnp.sum(jnp.where(h0, prod, 0.0), axis=-1, keepdims=True),
                                   jnp.sum(jnp.where(h0, 0.0, prod), axis=-1, keepdims=True)], axis=0)
            s = _dot_nt(q2, kw) + b_ref[var].reshape(2 * Q_BLK, K_WIN)
            p = jnp.exp(s - lse2)
            ds = p * (_dot_nt(do2, vw) - dd2)
            db_s[pl.ds(hp * 2, 2)] += ds.reshape(2, Q_BLK, K_WIN)
            dsb = ds.astype(BF16)
            dv_acc = _dot_tn(p.astype(BF16), do2)
            dk_acc = _dot_tn(dsb, q2)
            dq2 = _dot(dsb, kw) * ATTN_SCALE
            dq_ref[qrows(m, sub)] = jnp.where(h0, dq2[0:Q_BLK], dq2[Q_BLK:2 * Q_BLK]).astype(dq_ref.dtype)
            dkp[m, pl.ds(st, Q_BLK), :] = carry_k + dk_acc[0:Q_BLK]
            dvp[m, pl.ds(st, Q_BLK), :] = carry_v + dv_acc[0:Q_BLK]
            carry_k, carry_v = dk_acc[Q_BLK:K_WIN], dv_acc[Q_BLK:K_WIN]
            if sub == nsub - 1:
                carry[m, 0] = carry_k
                carry[m, 1] = carry_v

        @pl.when(step == nstep - 1)
        def _():
            for m in range(inner):
                dkp[m, L:L + Q_BLK, :] = carry[m, 0]
                dvp[m, L:L + Q_BLK, :] = carry[m, 1]
                dk_ref[krows(m)] = dkp[m, HALF_WIN:HALF_WIN + L, :].astype(dk_ref.dtype)
                dv_ref[krows(m)] = dvp[m, HALF_WIN:HALF_WIN + L, :].astype(dv_ref.dtype)

        @pl.when(last)
        def _():
            db_ref[...] = db_s[...]

    dbshape = (N_HEADS, Q_BLK, K_WIN)
    gshape = jax.ShapeDtypeStruct(fshape.shape, BF16 if inner == 1 else F32)
    return _call(
        body, f"attn_bwd_d{dil}", grid,
        [qspec, kspec, kspec, bspec, qfspec, qfspec, qfspec],
        [qfspec, kfspec, kfspec, _whole_vmem()],
        [gshape, gshape, gshape, jax.ShapeDtypeStruct(dbshape, F32)],
        [pltpu.VMEM(dbshape, F32)] + [pltpu.VMEM((inner, L + 2 * HALF_WIN, 128), BF16)] * 2
        + [pltpu.VMEM((inner, L + 2 * HALF_WIN, 128), F32)] * 2 + [pltpu.VMEM((inner, 2, Q_BLK, 128), F32)],
        (view(q), view(k), view(v), bias, do, o, lse), rider)


def _bucket_onehots(dil):
    m = np.zeros((3, K_WIN, N_BUCKETS), np.float32)
    for var, off in enumerate(_VARIANT_OFFSETS):
        for rel in range(-HALF_WIN, HALF_WIN + 1):
            col = (rel - off + Q_BLK - 1) % K_WIN
            m[var, col, int(_t5_bucket_np(np.asarray(rel * dil)))] = 1.0
    return jnp.asarray(m)


def _bias_grad(dbs):
    onehots = [_bucket_onehots(d) for d in DILATIONS]
    flip = jnp.asarray(np.eye(Q_BLK, dtype=np.float32)[::-1].copy())

    def body(d1, d2, d3, m1, m2, m3, flip_ref, out_ref):
        hp = lax.Precision.HIGHEST
        acc = jnp.zeros((N_HEADS, N_BUCKETS), F32)
        for d_ref, m_ref in ((d1, m1), (d2, m2), (d3, m3)):
            rows = []
            for h in range(N_HEADS):
                xrev = jnp.dot(flip_ref[...], d_ref[h], precision=hp, preferred_element_type=F32)
                y = pltpu.roll(xrev, 0, 1, stride=1, stride_axis=0)
                rows.append(jnp.sum(y, axis=0, keepdims=True))
            acc = acc + jnp.dot(jnp.concatenate(rows, axis=0), m_ref[1], precision=hp, preferred_element_type=F32)
        out_ref[...] = acc

    return pl.pallas_call(
        body, name="bias_grad", out_shape=jax.ShapeDtypeStruct((N_HEADS, N_BUCKETS), F32),
        compiler_params=_params(0),
    )(*dbs, *onehots, flip)


def _block_diag(w):
    eye = jnp.eye(N_RNN_BLOCKS, dtype=w.dtype)
    return jnp.einsum("ncd,nm->ncmd", w, eye).reshape(D_RNN, D_RNN).astype(BF16)


def _diag_blocks(dense):
    d = dense.reshape(N_RNN_BLOCKS, RNN_BLOCK, N_RNN_BLOCKS, RNN_BLOCK)
    return jnp.stack([d[n, :, n, :] for n in range(N_RNN_BLOCKS)])


EARLY = ("w_out", "w_up", "w_down")


def _local_step(x, target, p, shards=None):
    p = dict(p)
    first = None if shards is None else _gather_rider(["w_in"], [shards["w_in"]], shards["conv_w"])
    biases, got = _bias_mats(p["rel_bias"], first)
    if shards is not None:
        p["w_in"] = got[0]
        p["conv_w"] = jnp.transpose(got[1], (1, 0, 2)).reshape(4, D_RNN)
    lru = {}
    for dname in ("fwd", "bwd"):
        lru[dname] = (_block_diag(p["lru_wa_" + dname]), p["lru_ba_" + dname], _block_diag(p["lru_wx_" + dname]),
                      p["lru_bx_" + dname], p["lru_lam_" + dname])

    def gather(name):
        return None if shards is None else _gather_rider([name], [shards[name]])

    (xr, gate, *qkv), got = _inproj_fwd(x, p["attn_norm_g"], p["w_in"], gather("w_out"))
    p.update(zip(["w_out"], got))
    qs, ks, vs = qkv[0:3], qkv[3:6], qkv[6:9]
    (hf, xc), got = _rnn_fwd(xr, p["conv_w"], p["conv_b"], *lru["fwd"], reverse=False, rider=gather("w_up"))
    p.update(zip(["w_up"], got))
    (hb,), got = _rnn_fwd(xr, p["conv_w"], p["conv_b"], *lru["bwd"], reverse=True, rider=gather("w_down"), xc=xc)
    p.update(zip(["w_down"], got))
    outs, lses = [], []
    for q, k, v, bias in zip(qs, ks, vs, biases):
        o, l = _attn_fwd(q, k, v, bias)
        outs.append(o)
        lses.append(l)
    x1, mixb, *yl = _mix_fwd(outs, lses, hf, hb, gate, x, p["norm_rnn_g"], p["norm_attn_g"], p["w_out"])
    yas, lsts = [yl[0], yl[1], yl[1]], [yl[2], yl[3], yl[3]]
    dx1, h2b, a2b, dub, dx2b, loss_vec, dg_fin, dg_mlp = _mlp_fwd_bwd(
        x1, target, p["mlp_norm_g"], p["final_norm_g"], p["w_up"], p["w_down"])
    dhs, dgate, dya1, dya4, dw_out, dg_rnn, dg_attn = _mix_bwd(dx1, p["w_out"], mixb, yas[0], hf, hb, gate,
                                                               p["norm_rnn_g"], p["norm_attn_g"])
    dyas = [dya1, dya4, dya4]
    dw_up = _dw_matmul(h2b, dub, D_MODEL, FF_BLK, "dw_up")
    dw_down = _dw_matmul(a2b, dx2b, FF_BLK, D_MODEL, "dw_down")
    early = [dw_out, dw_up, dw_down]
    dqs, dks, dvs, dbs = [], [], [], []
    for i, (q, k, v, bias, dya, ya, lse) in enumerate(zip(qs, ks, vs, biases, dyas, yas, lsts)):
        rider = None
        if shards is not None:
            make = (lambda: _pair_exchange_rider(EARLY, early), lambda: _chip_exchange_rider(early),
                    lambda: _pair_share_rider(EARLY, early))[i]
            rider = make()
        (dq, dk, dv, db), got = _attn_bwd(q, k, v, bias, dya, ya, lse, rider)
        if shards is not None and i == 0:
            core = lax.axis_index("c").reshape(1).astype(jnp.int32)
            early = [_pair_add(core, g, o, "grad_pair_add_" + n) for n, g, o in zip(EARLY, early, got)]
        elif shards is not None and i == 1:
            early = [_chip_sum(t, "grad_chip_sum_" + n) for n, t in zip(EARLY, got)]
        elif shards is not None:
            early = got
        dqs.append(dq)
        dks.append(dk)
        dvs.append(dv)
        dbs.append(db)
    d_rel_bias = _bias_grad(dbs).T
    (dxc_f, dwa_f, dwx_f, dvec_f), _ = _rnn_bwd(xc, hf, dhs, *lru["fwd"], reverse=False)
    small = {
        "lru_wa_fwd": _diag_blocks(dwa_f), "lru_ba_fwd": dvec_f[0:1], "lru_wx_fwd": _diag_blocks(dwx_f),
        "lru_bx_fwd": dvec_f[1:2], "lru_lam_fwd": dvec_f[3:4],
        "rel_bias": d_rel_bias, "norm_rnn_g": dg_rnn, "norm_attn_g": dg_attn,
        "mlp_norm_g": dg_mlp, "final_norm_g": dg_fin,
    }
    loss_local = (0.5 / D_MODEL) * jnp.sum(loss_vec)
    rider = None
    if shards is not None:
        rider = _small_gather_rider(_pack([small[n].reshape(shp) for n, shp in SMALL if n in small]
                                          + [loss_local.reshape(1)]))
    (dxc_b, dwa_b, dwx_b, dvec_b), gathered = _rnn_bwd(xc, hb, dhs, *lru["bwd"], reverse=True, rider=rider)
    grad_x, dw_in, dg1, dconv = _inproj_bwd(x, dx1, xr, dxc_f, dxc_b, dgate, dqs, dks, dvs,
                                            p["attn_norm_g"], p["conv_w"], p["w_in"])[0]
    last = {"lru_wa_bwd": _diag_blocks(dwa_b), "lru_ba_bwd": dvec_b[0:1], "lru_wx_bwd": _diag_blocks(dwx_b),
            "lru_bx_bwd": dvec_b[1:2], "lru_lam_bwd": dvec_b[3:4],
            "attn_norm_g": dg1, "conv_w": dconv[0:4], "conv_b": dconv[4:5]}
    if shards is None:
        big = {"w_in": dw_in, "w_out": dw_out, "w_up": dw_up, "w_down": dw_down}
        return loss_local, grad_x, {**small, **last}, None, big, {}
    return loss_local, grad_x, last, gathered[0], {"w_in": dw_in}, dict(zip(EARLY, early))


BIG = ("w_in", "w_out", "w_up", "w_down")
BIG_SHARD = {"w_in": (D_MODEL, IN_BLK), "w_out": (OUT_BLK, D_MODEL), "w_up": (D_MODEL, FF_BLK), "w_down": (FF_BLK, D_MODEL)}
N_BIG = len(BIG)
N_CHIP_PEERS = 3
ANY = pl.BlockSpec(memory_space=pl.ANY)


def _place():
    x, y, c = lax.axis_index("x"), lax.axis_index("y"), lax.axis_index("c")
    chips = [(1 - x, y), (x, 1 - y), (1 - x, 1 - y)]
    return x, y, c, chips


def _remote(src, dst, send_sem, recv_sem, dev):
    return pltpu.make_async_remote_copy(src_ref=src, dst_ref=dst, send_sem=send_sem, recv_sem=recv_sem,
                                        device_id=dev, device_id_type=MESH)


def _staged_start(srcs, bufs, sems):
    legs = [pltpu.make_async_copy(s, b, sems.at[i]) for i, (s, b) in enumerate(zip(srcs, bufs))]
    for cp in legs:
        cp.start()
    return legs


def _staged_finish(legs, bufs, dsts, sems):
    out = []
    for i, (leg, b, d) in enumerate(zip(legs, bufs, dsts)):
        leg.wait()
        cp = pltpu.make_async_copy(b, d, sems.at[i])
        cp.start()
        out.append(cp)
    return out


class _Rider:
    def __init__(self, inputs, out_shape, scratch, first, late, last):
        self.inputs, self.out_shape, self.scratch = list(inputs), list(out_shape), list(scratch)
        self.first, self.late, self.last = first, late, last


def _call(body, name, grid, in_specs, out_specs, out_shape, scratch, operands, rider=None):
    n_grid = len(grid)
    if rider is None:
        res = pl.pallas_call(body, grid=grid, name=name, in_specs=in_specs, out_specs=out_specs, out_shape=out_shape,
                             scratch_shapes=scratch, compiler_params=_params(n_grid))(*operands)
        return list(res), []
    n_in, n_out, n_scr = len(in_specs), len(out_specs), len(scratch)
    ri, ro = len(rider.inputs), len(rider.out_shape)
    nsteps = int(np.prod(grid))
    late_step = max(nsteps - 3, 1)

    def wrapped(*refs):
        a, b = n_in, n_in + ri
        c, d = b + n_out, b + n_out + ro
        e = d + n_scr
        mine = refs[:a] + refs[b:c] + refs[d:e]
        theirs = (refs[a:b], refs[c:d], refs[e:])
        step = pl.program_id(0)
        for ax in range(1, n_grid):
            step = step * grid[ax] + pl.program_id(ax)
        pl.when(step == 0)(lambda: rider.first(*theirs))
        pl.when(step == late_step)(lambda: rider.late(*theirs))
        body(*mine)
        pl.when(step == nsteps - 1)(lambda: rider.last(*theirs))

    res = pl.pallas_call(
        wrapped, grid=grid, name=name, in_specs=list(in_specs) + [ANY] * ri, out_specs=list(out_specs) + [ANY] * ro,
        out_shape=list(out_shape) + rider.out_shape, scratch_shapes=list(scratch) + rider.scratch,
        compiler_params=_params(n_grid),
    )(*operands, *rider.inputs)
    return list(res[:n_out]), list(res[n_out:])


def _run_rider(rider, name):
    ri, ro = len(rider.inputs), len(rider.out_shape)

    def body(*refs):
        parts = (refs[:ri], refs[ri:ri + ro], refs[ri + ro:])
        rider.first(*parts)
        rider.late(*parts)
        rider.last(*parts)

    return list(pl.pallas_call(
        body, name=name, in_specs=[ANY] * ri, out_specs=[ANY] * ro, out_shape=rider.out_shape, scratch_shapes=rider.scratch,
        compiler_params=pltpu.CompilerParams(has_side_effects=True, vmem_limit_bytes=VMEM_LIMIT),
    )(*rider.inputs))


def _nothing(ins, outs, scr):
    return None


def _gather_rider(names, shards, conv_w=None):
    n = len(names)
    items = n + (conv_w is not None)
    halves = [BIG_SHARD[nm][0] // 2 for nm in names]

    def parts(ins, outs, scr):
        x, y, c, chips = _place()
        return x, y, c, chips, 2 * x + y, (x, y, 1 - c), scr[:8], scr[8:]

    def piece(outs, w, chip, core_half):
        return outs[w].at[chip, pl.ds(core_half * halves[w], halves[w])]

    def ici(ins, outs, sems, w, k, chip_xy, c, me):
        return _remote(ins[w].at[pl.ds(c * halves[w], halves[w])], piece(outs, w, me, c),
                       sems[0].at[w, k], sems[1].at[w, k], (*chip_xy, c))

    def first(ins, outs, scr):
        x, y, c, chips, me, sibling, sems, bufs = parts(ins, outs, scr)
        legs = _staged_start(ins, bufs, sems[6])
        for w in range(n):
            for k, chip_xy in enumerate(chips):
                ici(ins, outs, sems, w, k, chip_xy, c, me).start()
        if conv_w is not None:
            for k, (px, py) in enumerate(chips):
                _remote(ins[n], outs[n].at[me], sems[4].at[k], sems[5].at[k], (px, py, c)).start()
        _staged_finish(legs, bufs, [o.at[me] for o in outs], sems[7])

    def late(ins, outs, scr):
        x, y, c, chips, me, sibling, sems, bufs = parts(ins, outs, scr)
        for w in range(n):
            for k, (px, py) in enumerate(chips):
                landed = piece(outs, w, 2 * px + py, c)
                _remote(landed, landed, sems[0].at[w, k], sems[1].at[w, k], (px, py, c)).wait_recv()
                _remote(landed, landed, sems[2].at[w, k], sems[3].at[w, k], sibling).start()

    def last(ins, outs, scr):
        x, y, c, chips, me, sibling, sems, bufs = parts(ins, outs, scr)
        for w in range(n):
            for k, (px, py) in enumerate(chips):
                other = piece(outs, w, 2 * px + py, 1 - c)
                _remote(other, other, sems[2].at[w, k], sems[3].at[w, k], sibling).wait_recv()
        if conv_w is not None:
            for k, (px, py) in enumerate(chips):
                got = outs[n].at[2 * px + py]
                _remote(got, got, sems[4].at[k], sems[5].at[k], (px, py, c)).wait_recv()
                _remote(ins[n], outs[n].at[me], sems[4].at[k], sems[5].at[k], (px, py, c)).wait_send()
        for i in range(items):
            pltpu.make_async_copy(bufs[i], outs[i].at[me], sems[7].at[i]).wait()
        for w in range(n):
            for k, (px, py) in enumerate(chips):
                ici(ins, outs, sems, w, k, (px, py), c, me).wait_send()
                landed = piece(outs, w, 2 * px + py, c)
                _remote(landed, landed, sems[2].at[w, k], sems[3].at[w, k], sibling).wait_send()

    out_shape = [jax.ShapeDtypeStruct((N_SHARD,) + BIG_SHARD[nm], BF16) for nm in names]
    stage = [pltpu.VMEM(BIG_SHARD[nm], BF16) for nm in names]
    inputs = list(shards)
    if conv_w is not None:
        out_shape.append(jax.ShapeDtypeStruct((N_SHARD,) + conv_w.shape, F32))
        stage.append(pltpu.VMEM(conv_w.shape, F32))
        inputs.append(conv_w)
    scratch = ([pltpu.SemaphoreType.DMA((n, N_CHIP_PEERS))] * 4 + [pltpu.SemaphoreType.DMA((N_CHIP_PEERS,))] * 2
               + [pltpu.SemaphoreType.DMA((items,))] * 2 + stage)
    return _Rider(inputs, out_shape, scratch, first, late, last)


def _pair_exchange_rider(names, grads):
    def copies(ins, outs, scr):
        x, y, c, _ = _place()
        out = []
        for w, nm in enumerate(names):
            h = BIG_SHARD[nm][0] // 2
            out.append(_remote(ins[w].at[:, pl.ds((1 - c) * h, h), :], outs[w], scr[0].at[w], scr[1].at[w], (x, y, 1 - c)))
        return out

    def first(ins, outs, scr):
        for cp in copies(ins, outs, scr):
            cp.start()

    def last(ins, outs, scr):
        for cp in copies(ins, outs, scr):
            cp.wait()

    out_shape = [jax.ShapeDtypeStruct((N_SHARD, BIG_SHARD[nm][0] // 2, BIG_SHARD[nm][1]), F32) for nm in names]
    return _Rider(grads, out_shape, [pltpu.SemaphoreType.DMA((len(names),))] * 2, first, _nothing, last)


def _pair_add(core, grad, other, name):
    _, r, cols = grad.shape
    h = r // 2
    th = min(h, 256)
    per = h // th

    def body(c_ref, g_ref, o_ref, out_ref):
        out_ref[...] = (g_ref[...] + o_ref[...]).astype(BF16)

    return pl.pallas_call(
        body, name=name,
        grid_spec=pltpu.PrefetchScalarGridSpec(
            num_scalar_prefetch=1, grid=(N_SHARD, per),
            in_specs=[pl.BlockSpec((1, th, cols), lambda j, i, c_ref: (j, c_ref[0] * per + i, 0)),
                      pl.BlockSpec((1, th, cols), lambda j, i, c_ref: (j, i, 0))],
            out_specs=pl.BlockSpec((1, th, cols), lambda j, i, c_ref: (j, i, 0))),
        out_shape=jax.ShapeDtypeStruct((N_SHARD, h, cols), BF16),
        compiler_params=_params(2),
    )(core, grad, other)


def _chip_exchange_rider(parts):
    n = len(parts)

    def sends(ins, outs, scr):
        x, y, c, chips = _place()
        me = 2 * x + y
        return [_remote(ins[w].at[2 * px + py], outs[w].at[me], scr[0].at[w, k], scr[1].at[w, k], (px, py, c))
                for w in range(n) for k, (px, py) in enumerate(chips)]

    def first(ins, outs, scr):
        x, y, c, chips = _place()
        me = 2 * x + y
        legs = _staged_start([r.at[me] for r in ins], scr[4:], scr[2])
        for cp in sends(ins, outs, scr):
            cp.start()
        _staged_finish(legs, scr[4:], [o.at[me] for o in outs], scr[3])

    def last(ins, outs, scr):
        x, y, c, chips = _place()
        me = 2 * x + y
        for w in range(n):
            for k, (px, py) in enumerate(chips):
                got = outs[w].at[2 * px + py]
                _remote(got, got, scr[0].at[w, k], scr[1].at[w, k], (px, py, c)).wait_recv()
        for cp in sends(ins, outs, scr):
            cp.wait_send()
        for w in range(n):
            pltpu.make_async_copy(scr[4 + w], outs[w].at[me], scr[3].at[w]).wait()

    out_shape = [jax.ShapeDtypeStruct(p.shape, BF16) for p in parts]
    scratch = ([pltpu.SemaphoreType.DMA((n, N_CHIP_PEERS))] * 2 + [pltpu.SemaphoreType.DMA((n,))] * 2
               + [pltpu.VMEM(p.shape[1:], BF16) for p in parts])
    return _Rider(parts, out_shape, scratch, first, _nothing, last)


def _chip_sum(parts, name):
    _, h, cols = parts.shape
    th = min(h, 256)

    def body(p_ref, out_ref):
        acc = p_ref[0].astype(F32)
        for j in range(1, N_SHARD):
            acc = acc + p_ref[j].astype(F32)
        out_ref[...] = acc

    return pl.pallas_call(
        body, name=name, grid=(h // th,),
        in_specs=[pl.BlockSpec((N_SHARD, th, cols), lambda i: (0, i, 0))],
        out_specs=pl.BlockSpec((th, cols), lambda i: (i, 0)),
        out_shape=jax.ShapeDtypeStruct((h, cols), F32),
        compiler_params=_params(),
    )(parts)


def _pair_share_rider(names, halves):
    n = len(names)
    hs = [BIG_SHARD[nm][0] // 2 for nm in names]

    def mine(outs, c):
        return [outs[w].at[pl.ds(c * hs[w], hs[w])] for w in range(n)]

    def first(ins, outs, scr):
        x, y, c, _ = _place()
        legs = _staged_start(ins, scr[4:], scr[2])
        for w, dst in enumerate(mine(outs, c)):
            _remote(ins[w], dst, scr[0].at[w], scr[1].at[w], (x, y, 1 - c)).start()
        _staged_finish(legs, scr[4:], mine(outs, c), scr[3])

    def last(ins, outs, scr):
        x, y, c, _ = _place()
        for w, (theirs, dst) in enumerate(zip(mine(outs, 1 - c), mine(outs, c))):
            _remote(theirs, theirs, scr[0].at[w], scr[1].at[w], (x, y, 1 - c)).wait_recv()
            _remote(ins[w], dst, scr[0].at[w], scr[1].at[w], (x, y, 1 - c)).wait_send()
            pltpu.make_async_copy(scr[4 + w], dst, scr[3].at[w]).wait()

    out_shape = [jax.ShapeDtypeStruct(BIG_SHARD[nm], F32) for nm in names]
    scratch = [pltpu.SemaphoreType.DMA((n,))] * 4 + [pltpu.VMEM((h, BIG_SHARD[nm][1]), F32) for nm, h in zip(names, hs)]
    return _Rider(halves, out_shape, scratch, first, _nothing, last)


N_DEV = 8


def _all_peers(x, y, c):
    return [((1 - x) if fx else x, (1 - y) if fy else y, (1 - c) if fc else c)
            for fx in (0, 1) for fy in (0, 1) for fc in (0, 1) if fx or fy or fc]


def _small_gather_rider(vec):
    def sends(ins, outs, scr):
        x, y, c, _ = _place()
        me = 4 * x + 2 * y + c
        return [_remote(ins[0], outs[0].at[me], scr[0].at[k], scr[1].at[k], dev) for k, dev in enumerate(_all_peers(x, y, c))]

    def first(ins, outs, scr):
        x, y, c, _ = _place()
        legs = _staged_start(ins, scr[4:], scr[2])
        for cp in sends(ins, outs, scr):
            cp.start()
        _staged_finish(legs, scr[4:], [outs[0].at[4 * x + 2 * y + c]], scr[3])

    def last(ins, outs, scr):
        x, y, c, _ = _place()
        for k, (px, py, pc) in enumerate(_all_peers(x, y, c)):
            got = outs[0].at[4 * px + 2 * py + pc]
            _remote(got, got, scr[0].at[k], scr[1].at[k], (px, py, pc)).wait_recv()
        for cp in sends(ins, outs, scr):
            cp.wait_send()
        pltpu.make_async_copy(scr[4], outs[0].at[4 * x + 2 * y + c], scr[3].at[0]).wait()

    scratch = ([pltpu.SemaphoreType.DMA((N_DEV - 1,))] * 2 + [pltpu.SemaphoreType.DMA((1,))] * 2
               + [pltpu.VMEM(vec.shape, F32)])
    return _Rider([vec], [jax.ShapeDtypeStruct((N_DEV,) + vec.shape, F32)], scratch, first, _nothing, last)


def _sum_devices(gathered):
    def body(g_ref, out_ref):
        acc = g_ref[0]
        for j in range(1, N_DEV):
            acc = acc + g_ref[j]
        out_ref[...] = acc

    return pl.pallas_call(body, name="sum_devices", out_shape=jax.ShapeDtypeStruct(gathered.shape[1:], F32),
                          compiler_params=_params(0))(gathered)


def _allreduce_small(vec):
    rows = vec.shape[0]

    def body(v_ref, sum_ref, gat_ref, send, recv, loc_sem):
        x, y, c, chips = _place()
        sibling = (x, y, 1 - c)
        slot = lambda px, py, pc: gat_ref.at[4 * px + 2 * py + pc]
        lc = pltpu.make_async_copy(v_ref, slot(x, y, c), loc_sem)
        lc.start()
        sends = [_remote(v_ref, slot(x, y, c), send.at[0], recv.at[0], sibling)]
        sends += [_remote(v_ref, slot(x, y, c), send.at[1 + k], recv.at[1 + k], (px, py, c))
                  for k, (px, py) in enumerate(chips)]
        for cp in sends:
            cp.start()
        for k, (px, py) in enumerate(chips):
            got = slot(px, py, c)
            _remote(got, got, send.at[1 + k], recv.at[1 + k], (px, py, c)).wait_recv()
            cp = _remote(got, got, send.at[4 + k], recv.at[4 + k], sibling)
            cp.start()
            sends.append(cp)
        got = slot(x, y, 1 - c)
        _remote(got, got, send.at[0], recv.at[0], sibling).wait_recv()
        for k, (px, py) in enumerate(chips):
            got = slot(px, py, 1 - c)
            _remote(got, got, send.at[4 + k], recv.at[4 + k], sibling).wait_recv()
        for cp in sends:
            cp.wait_send()
        lc.wait()
        acc = gat_ref[0]
        for j in range(1, N_DEV):
            acc = acc + gat_ref[j]
        sum_ref[...] = acc

    total, _ = pl.pallas_call(
        body, name="allreduce_small",
        in_specs=[_whole_vmem()], out_specs=[_whole_vmem(), _whole_vmem()],
        out_shape=[jax.ShapeDtypeStruct((rows, 128), F32), jax.ShapeDtypeStruct((N_DEV, rows, 128), F32)],
        scratch_shapes=[pltpu.SemaphoreType.DMA((N_DEV - 1,))] * 2 + [pltpu.SemaphoreType.DMA(())],
        compiler_params=pltpu.CompilerParams(has_side_effects=True, vmem_limit_bytes=VMEM_LIMIT),
    )(vec)
    return total


def _adam_math(w_ref, g_ref, m_ref, v_ref, d_ref, m2_ref, v2_ref):
    c1 = 1.0 - ADAM_B1 ** ADAM_STEP
    c2 = 1.0 - ADAM_B2 ** ADAM_STEP
    gv = g_ref[...]
    m2 = ADAM_B1 * m_ref[...] + (1.0 - ADAM_B1) * gv
    v2 = ADAM_B2 * v_ref[...] + (1.0 - ADAM_B2) * (gv * gv)
    m2_ref[...] = m2
    v2_ref[...] = v2
    d_ref[...] = -ADAM_LR * ((m2 / c1) / (jnp.sqrt(v2 / c2) + ADAM_EPS) + ADAM_WD * w_ref[...])


def _adamw_many(ws, gs, ms, vs):
    n = len(ws)

    def body(*refs):
        for i in range(n):
            _adam_math(*[refs[k * n + i] for k in range(7)])

    shapes = [jax.ShapeDtypeStruct(w.shape, F32) for w in ws]
    res = pl.pallas_call(body, name="adamw_small", out_shape=shapes * 3, compiler_params=_params(0))(*ws, *gs, *ms, *vs)
    return res[:n], res[n:2 * n], res[2 * n:]


def _adamw(w, g, m, v, name):
    rows, cols = w.shape
    tr = 256 if rows % 256 == 0 else rows

    def body(w_ref, g_ref, m_ref, v_ref, d_ref, m2_ref, v2_ref):
        _adam_math(w_ref, g_ref, m_ref, v_ref, d_ref, m2_ref, v2_ref)

    spec = pl.BlockSpec((tr, cols), lambda i: (i, 0))
    shp = jax.ShapeDtypeStruct((rows, cols), F32)
    return pl.pallas_call(
        body, name=name, grid=(rows // tr,), in_specs=[spec] * 4, out_specs=[spec] * 3, out_shape=[shp] * 3,
        compiler_params=_params(),
    )(w, g, m, v)


SMALL = (
    ("attn_norm_g", (1, 1024)), ("conv_w", (1, 4, 512)), ("conv_b", (1, 512)),
    ("lru_wa_fwd", (1, 8, 64, 64)), ("lru_ba_fwd", (1, 512)), ("lru_wx_fwd", (1, 8, 64, 64)), ("lru_bx_fwd", (1, 512)),
    ("lru_lam_fwd", (1, 512)),
    ("lru_wa_bwd", (1, 8, 64, 64)), ("lru_ba_bwd", (1, 512)), ("lru_wx_bwd", (1, 8, 64, 64)), ("lru_bx_bwd", (1, 512)),
    ("lru_lam_bwd", (1, 512)),
    ("rel_bias", (32, 8)), ("norm_rnn_g", (1, 512)), ("norm_attn_g", (1, 512)), ("mlp_norm_g", (1, 1024)),
    ("final_norm_g", (1024,)),
)
PACK_ROW = 8 * 128


def _pack(parts):
    flat = jnp.concatenate([p.reshape(-1) for p in parts])
    pad = (-flat.shape[0]) % PACK_ROW
    return jnp.pad(flat, (0, pad)).reshape(-1, 128)


def _unpack(packed, shapes):
    flat = packed.reshape(-1)
    out, off = [], 0
    for shp in shapes:
        n = int(np.prod(shp))
        out.append(flat[off:off + n].reshape(shp))
        off += n
    return out


WEIGHT_ORDER = ("attn_norm_g", "w_in", "conv_w", "conv_b", "lru_wa_fwd", "lru_ba_fwd", "lru_wx_fwd", "lru_bx_fwd",
                "lru_lam_fwd", "lru_wa_bwd", "lru_ba_bwd", "lru_wx_bwd", "lru_bx_bwd", "lru_lam_bwd", "rel_bias",
                "norm_rnn_g", "norm_attn_g", "w_out", "mlp_norm_g", "w_up", "w_down", "final_norm_g")


def kernel(x, attn_norm_g, w_in, conv_w, conv_b, lru_wa_fwd, lru_ba_fwd, lru_wx_fwd, lru_bx_fwd, lru_lam_fwd, lru_wa_bwd, lru_ba_bwd, lru_wx_bwd, lru_bx_bwd, lru_lam_bwd, rel_bias, norm_rnn_g, norm_attn_g, w_out, mlp_norm_g, w_up, w_down, final_norm_g, loss_target, m_attn_norm_g, m_w_in, m_conv_w, m_conv_b, m_lru_wa_fwd, m_lru_ba_fwd, m_lru_wx_fwd, m_lru_bx_fwd, m_lru_lam_fwd, m_lru_wa_bwd, m_lru_ba_bwd, m_lru_wx_bwd, m_lru_bx_bwd, m_lru_lam_bwd, m_rel_bias, m_norm_rnn_g, m_norm_attn_g, m_w_out, m_mlp_norm_g, m_w_up, m_w_down, m_final_norm_g, v_attn_norm_g, v_w_in, v_conv_w, v_conv_b, v_lru_wa_fwd, v_lru_ba_fwd, v_lru_wx_fwd, v_lru_bx_fwd, v_lru_lam_fwd, v_lru_wa_bwd, v_lru_ba_bwd, v_lru_wx_bwd, v_lru_bx_bwd, v_lru_lam_bwd, v_rel_bias, v_norm_rnn_g, v_norm_attn_g, v_w_out, v_mlp_norm_g, v_w_up, v_w_down, v_final_norm_g):
    given = dict(locals())
    w = {n: given[n] for n in WEIGHT_ORDER}
    m = {n: given["m_" + n] for n in WEIGHT_ORDER}
    v = {n: given["v_" + n] for n in WEIGHT_ORDER}

    chip = lax.axis_index("x") * 2 + lax.axis_index("y")
    core = lax.axis_index("c")

    shards = {n: w[n][0].astype(BF16) for n in BIG}
    shards["conv_w"] = w["conv_w"][0]
    p = {n: (t[0] if t.ndim >= 3 else t) for n, t in w.items() if n not in BIG and n != "conv_w"}
    p["final_norm_g"] = w["final_norm_g"].reshape(1, D_MODEL)

    _, grad_x, small, gathered, big, reduced = _local_step(x[0], loss_target[0], p, shards)

    late = tuple(big)
    grads = [big[n] for n in late]
    others = _run_rider(_pair_exchange_rider(late, grads), "grad_pair_exchange")
    core_arr = core.reshape(1).astype(jnp.int32)
    parts = [_pair_add(core_arr, g, o, "grad_pair_add_" + n) for n, g, o in zip(late, grads, others)]
    landed = _run_rider(_chip_exchange_rider(parts), "grad_chip_exchange")
    halves = [_chip_sum(t, "grad_chip_sum_" + n) for n, t in zip(late, landed)]
    reduced.update(zip(late, _run_rider(_pair_share_rider(late, halves), "grad_pair_share")))

    early_small = [(n, shp) for n, shp in SMALL if n not in small]
    late_small = [(n, shp) for n, shp in SMALL if n in small]
    *early_g, loss = _unpack(_sum_devices(gathered), [shp for _, shp in early_small] + [(1,)])
    late_g = _unpack(_allreduce_small(_pack([small[n].reshape(shp) for n, shp in late_small])),
                     [shp for _, shp in late_small])
    g = dict(zip([n for n, _ in early_small + late_small], early_g + late_g))
    g["conv_w"] = lax.dynamic_slice_in_dim(g["conv_w"], chip * (D_RNN // N_SHARD), D_RNN // N_SHARD, axis=2)
    for n in BIG:
        g[n] = reduced[n][None]

    delta, new_m, new_v = {}, {}, {}
    for n in BIG:
        d2, m2, v2 = _adamw(w[n][0], reduced[n], m[n][0], v[n][0], "adamw_" + n)
        delta[n], new_m[n], new_v[n] = d2[None], m2[None], v2[None]
    names = [n for n, _ in SMALL]
    for dst, src in zip((delta, new_m, new_v), _adamw_many(*[[t[n] for n in names] for t in (w, g, m, v)])):
        dst.update(dict(zip(names, src)))

    return (loss.reshape(()), grad_x[None], *[g[n] for n in WEIGHT_ORDER], *[delta[n] for n in WEIGHT_ORDER],
            *[new_m[n] for n in WEIGHT_ORDER], *[new_v[n] for n in WEIGHT_ORDER])
```
